```python
import math
import jax, jax.numpy as jnp
from jax import lax
import numpy as np

D_MODEL = 2048
BATCH = 16
SEQ = 2048
DEPTH = 1

ATTN_HEADS = 16
ATTN_KV_HEADS = 4
ATTN_HEAD_DIM = 64
ATTN_GROUP = ATTN_HEADS // ATTN_KV_HEADS
WINDOW = 128
ATTN_BLOCK = WINDOW
ATTN_WIDTH = ATTN_HEADS * ATTN_HEAD_DIM
KV_WIDTH = ATTN_KV_HEADS * ATTN_HEAD_DIM
HGRN_HEADS = 8
HGRN_KEY_DIM = 128
HGRN_VALUE_DIM = 128
HGRN_WIDTH = HGRN_HEADS * HGRN_VALUE_DIM
HGRN_CHUNK = 64
REL_BUCKETS = 32
REL_MAX_DIST = 128
NORM_EPS = 1e-6
IN_WIDTHS = (ATTN_WIDTH, KV_WIDTH, KV_WIDTH, ATTN_WIDTH,
             HGRN_HEADS * HGRN_KEY_DIM, HGRN_HEADS * HGRN_KEY_DIM, HGRN_WIDTH, HGRN_WIDTH,
             D_MODEL, D_MODEL)
IN_PROJ_WIDTH = 10752

kernel_name = 'hybrid_swa_sinks_hgrn2_gated_merge'


def rms_norm(x, gain):
    xf = x.astype(jnp.float32)
    y = xf * lax.rsqrt(jnp.mean(xf * xf, axis=-1, keepdims=True) + NORM_EPS)
    return (y * gain.astype(jnp.float32)).astype(x.dtype)


def t5_bucket(dist):
    max_exact = REL_BUCKETS // 2
    d = jnp.maximum(dist, 0)
    df = jnp.maximum(d, 1).astype(jnp.float32)
    large = max_exact + (jnp.log(df / max_exact) / math.log(REL_MAX_DIST / max_exact)
                         * (REL_BUCKETS - max_exact)).astype(jnp.int32)
    large = jnp.minimum(large, REL_BUCKETS - 1)
    return jnp.where(d < max_exact, d, large)


def sliding_window_attention(q, k, v, sinks, rel_bias):
    B, S, _ = q.shape
    nb = S // ATTN_BLOCK
    qb = q.astype(jnp.float32).reshape(B, nb, ATTN_BLOCK, ATTN_KV_HEADS, ATTN_GROUP, ATTN_HEAD_DIM)

    def windows(t):
        t = t.astype(jnp.float32).reshape(B, S, ATTN_KV_HEADS, ATTN_HEAD_DIM)
        t = jnp.pad(t, ((0, 0), (ATTN_BLOCK, 0), (0, 0), (0, 0)))
        t = t.reshape(B, nb + 1, ATTN_BLOCK, ATTN_KV_HEADS, ATTN_HEAD_DIM)
        return jnp.concatenate([t[:, :-1], t[:, 1:]], axis=2)

    kw, vw = windows(k), windows(v)
    scores = jnp.einsum('bnqhgd,bnshd->bhgnqs', qb, kw) * (ATTN_HEAD_DIM ** -0.5)
    qi = jnp.arange(ATTN_BLOCK)[:, None]
    si = jnp.arange(2 * ATTN_BLOCK)[None, :]
    dist = qi + ATTN_BLOCK - si
    band = (dist >= 0) & (dist < WINDOW)
    key_pos = jnp.arange(nb)[:, None] * ATTN_BLOCK - ATTN_BLOCK + jnp.arange(2 * ATTN_BLOCK)[None, :]
    mask = band[None] & (key_pos >= 0)[:, None, :]
    bias = rel_bias[t5_bucket(dist)].astype(jnp.float32)
    bias = jnp.transpose(bias, (2, 0, 1)).reshape(ATTN_KV_HEADS, ATTN_GROUP, 1, ATTN_BLOCK, 2 * ATTN_BLOCK)
    scores = jnp.where(mask, scores + bias, -jnp.inf)
    sink = jnp.broadcast_to(sinks.astype(jnp.float32).reshape(ATTN_KV_HEADS, ATTN_GROUP, 1, 1, 1),
                            scores.shape[:-1] + (1,))
    probs = jax.nn.softmax(jnp.concatenate([scores, sink], axis=-1), axis=-1)[..., :-1]
    out = jnp.einsum('bhgnqs,bnshd->bnqhgd', probs, vw)
    return out.reshape(B, S, ATTN_WIDTH)


def hgrn2_recurrence(q, f_pre, i, lb):
    B, S, _ = q.shape
    nc = S // HGRN_CHUNK

    def chunks(t):
        return t.reshape(B, nc, HGRN_CHUNK, HGRN_HEADS, -1).transpose(1, 0, 3, 2, 4)

    lbf = lb.astype(jnp.float32)
    f = lbf + (1.0 - lbf) * jax.nn.sigmoid(f_pre.astype(jnp.float32))
    qc = chunks(jax.nn.silu(q.astype(jnp.float32)))
    kc = chunks(1.0 - f)
    vc = chunks(i.astype(jnp.float32))
    gc = jnp.cumsum(chunks(jnp.log(f)), axis=3)
    causal = jnp.tril(jnp.ones((HGRN_CHUNK, HGRN_CHUNK), dtype=bool))

    def step(state, inp):
        qt, kt, vt, gt = inp
        inter = jnp.einsum('bhcd,bhde->bhce', qt * jnp.exp(gt), state)
        diff = gt[:, :, :, None, :] - gt[:, :, None, :, :]
        decay = jnp.exp(jnp.where(causal[:, :, None], diff, -jnp.inf))
        attn = jnp.einsum('bhtd,bhsd,bhtsd->bhts', qt, kt, decay)
        intra = jnp.einsum('bhts,bhse->bhte', attn, vt)
        g_last = gt[:, :, -1]
        k_dec = kt * jnp.exp(g_last[:, :, None, :] - gt)
        new_state = jnp.exp(g_last)[..., None] * state + jnp.einsum('bhsd,bhse->bhde', k_dec, vt)
        return new_state, inter + intra

    s0 = jnp.zeros((B, HGRN_HEADS, HGRN_KEY_DIM, HGRN_VALUE_DIM), jnp.float32)
    _, o = lax.scan(step, s0, (qc, kc, vc, gc))
    return o.transpose(1, 0, 3, 2, 4).reshape(B, S, HGRN_HEADS, HGRN_VALUE_DIM)


def _fwd_setup_inputs(seed: int = 0) -> dict:
    key = jax.random.key(seed)
    ks = jax.random.split(key, 12)
    f32 = jnp.float32
    return {
        'x': jax.random.normal(ks[0], (BATCH, SEQ, D_MODEL), f32),
        'norm_pre': 1.0 + 0.1 * jax.random.normal(ks[1], (DEPTH, D_MODEL), f32),
        'w_in': jax.random.normal(ks[2], (DEPTH, D_MODEL, IN_PROJ_WIDTH), f32) * D_MODEL ** -0.5,
        'rel_bias': 0.5 * jax.random.normal(ks[3], (REL_BUCKETS, ATTN_HEADS), f32),
        'attn_sinks': jax.random.normal(ks[4], (DEPTH, ATTN_HEADS), f32),
        'lb_logits': 0.5 * jax.random.normal(ks[5], (DEPTH + 1, HGRN_HEADS * HGRN_KEY_DIM), f32),
        'hgrn_norm': 1.0 + 0.1 * jax.random.normal(ks[6], (DEPTH, HGRN_HEADS, HGRN_VALUE_DIM), f32),
        'w_branch_attn': jax.random.normal(ks[7], (DEPTH, ATTN_WIDTH, D_MODEL), f32) * ATTN_WIDTH ** -0.5,
        'w_branch_hgrn': jax.random.normal(ks[8], (DEPTH, HGRN_WIDTH, D_MODEL), f32) * HGRN_WIDTH ** -0.5,
        'w_out': jax.random.normal(ks[9], (DEPTH, D_MODEL, D_MODEL), f32) * D_MODEL ** -0.5,
        'norm_post': 1.0 + 0.1 * jax.random.normal(ks[10], (DEPTH, D_MODEL), f32),
    }


def _fwd_reference(x, norm_pre, w_in, rel_bias, attn_sinks, lb_logits, hgrn_norm,
              w_branch_attn, w_branch_hgrn, w_out, norm_post):
    split_points = [int(p) for p in np.cumsum(IN_WIDTHS)[:-1]]
    lower_bounds = jnp.cumsum(jax.nn.softmax(lb_logits.astype(jnp.float32), axis=0), axis=0)[:DEPTH]
    for layer in range(DEPTH):
        h = rms_norm(x, norm_pre[layer])
        proj = jnp.matmul(h, w_in[layer])
        aq, ak, av, ag, hq, hf, hi, hg, gate_a, gate_h = jnp.split(proj, split_points, axis=-1)
        ya = sliding_window_attention(aq, ak, av, attn_sinks[layer], rel_bias)
        ya = (ya * jax.nn.silu(ag.astype(jnp.float32))).astype(x.dtype)
        oh = rms_norm(hgrn2_recurrence(hq, hf, hi, lower_bounds[layer]), hgrn_norm[layer])
        yh = (oh.reshape(oh.shape[0], oh.shape[1], HGRN_WIDTH)
              * jax.nn.silu(hg.astype(jnp.float32))).astype(x.dtype)
        ua = jnp.matmul(ya, w_branch_attn[layer])
        uh = jnp.matmul(yh, w_branch_hgrn[layer])
        merged = jax.nn.sigmoid(gate_a) * ua + jax.nn.sigmoid(gate_h) * uh
        y = jnp.matmul(merged, w_out[layer])
        x = x + rms_norm(y, norm_post[layer]).astype(x.dtype)
    return x


import jax as _jax
import jax.numpy as _jnp

TWIN_FORMAT = 'train_step'
FWD_PARAMS = ['x', 'norm_pre', 'w_in', 'rel_bias', 'attn_sinks', 'lb_logits', 'hgrn_norm', 'w_branch_attn', 'w_branch_hgrn', 'w_out', 'norm_post']
TWIN_WEIGHTS = ['norm_pre', 'w_in', 'rel_bias', 'attn_sinks', 'lb_logits', 'hgrn_norm', 'w_branch_attn', 'w_branch_hgrn', 'w_out', 'norm_post']
TWIN_DIFF_INPUT = 'x'
TWIN_INPUTS = ['x', 'norm_pre', 'w_in', 'rel_bias', 'attn_sinks', 'lb_logits', 'hgrn_norm', 'w_branch_attn', 'w_branch_hgrn', 'w_out', 'norm_post', 'loss_target', 'm_norm_pre', 'm_w_in', 'm_rel_bias', 'm_attn_sinks', 'm_lb_logits', 'm_hgrn_norm', 'm_w_branch_attn', 'm_w_branch_hgrn', 'm_w_out', 'm_norm_post', 'v_norm_pre', 'v_w_in', 'v_rel_bias', 'v_attn_sinks', 'v_lb_logits', 'v_hgrn_norm', 'v_w_branch_attn', 'v_w_branch_hgrn', 'v_w_out', 'v_norm_post']
TWIN_OUTPUTS = ['loss', 'grad_x', 'grad_norm_pre', 'grad_w_in', 'grad_rel_bias', 'grad_attn_sinks', 'grad_lb_logits', 'grad_hgrn_norm', 'grad_w_branch_attn', 'grad_w_branch_hgrn', 'grad_w_out', 'grad_norm_post', 'delta_norm_pre', 'delta_w_in', 'delta_rel_bias', 'delta_attn_sinks', 'delta_lb_logits', 'delta_hgrn_norm', 'delta_w_branch_attn', 'delta_w_branch_hgrn', 'delta_w_out', 'delta_norm_post', 'new_m_norm_pre', 'new_m_w_in', 'new_m_rel_bias', 'new_m_attn_sinks', 'new_m_lb_logits', 'new_m_hgrn_norm', 'new_m_w_branch_attn', 'new_m_w_branch_hgrn', 'new_m_w_out', 'new_m_norm_post', 'new_v_norm_pre', 'new_v_w_in', 'new_v_rel_bias', 'new_v_attn_sinks', 'new_v_lb_logits', 'new_v_hgrn_norm', 'new_v_w_branch_attn', 'new_v_w_branch_hgrn', 'new_v_w_out', 'new_v_norm_post']
TWIN_LEAF_KINDS = {'loss': 'loss', 'grad_x': 'grad_x', 'grad_norm_pre': 'grad_w', 'grad_w_in': 'grad_w', 'grad_rel_bias': 'grad_w', 'grad_attn_sinks': 'grad_w', 'grad_lb_logits': 'grad_w', 'grad_hgrn_norm': 'grad_w', 'grad_w_branch_attn': 'grad_w', 'grad_w_branch_hgrn': 'grad_w', 'grad_w_out': 'grad_w', 'grad_norm_post': 'grad_w', 'delta_norm_pre': 'delta_w', 'delta_w_in': 'delta_w', 'delta_rel_bias': 'delta_w', 'delta_attn_sinks': 'delta_w', 'delta_lb_logits': 'delta_w', 'delta_hgrn_norm': 'delta_w', 'delta_w_branch_attn': 'delta_w', 'delta_w_branch_hgrn': 'delta_w', 'delta_w_out': 'delta_w', 'delta_norm_post': 'delta_w', 'new_m_norm_pre': 'new_m', 'new_m_w_in': 'new_m', 'new_m_rel_bias': 'new_m', 'new_m_attn_sinks': 'new_m', 'new_m_lb_logits': 'new_m', 'new_m_hgrn_norm': 'new_m', 'new_m_w_branch_attn': 'new_m', 'new_m_w_branch_hgrn': 'new_m', 'new_m_w_out': 'new_m', 'new_m_norm_post': 'new_m', 'new_v_norm_pre': 'new_v', 'new_v_w_in': 'new_v', 'new_v_rel_bias': 'new_v', 'new_v_attn_sinks': 'new_v', 'new_v_lb_logits': 'new_v', 'new_v_hgrn_norm': 'new_v', 'new_v_w_branch_attn': 'new_v', 'new_v_w_branch_hgrn': 'new_v', 'new_v_w_out': 'new_v', 'new_v_norm_post': 'new_v'}


def _forward(args):
    return _fwd_reference(*[args[k] for k in FWD_PARAMS])


def _output_shape():
    out = _jax.eval_shape(lambda: _forward(_fwd_setup_inputs(0)))
    return out.shape, out.dtype

N_MICROBATCH = 1
ADAM_LR = 0.001
ADAM_B1 = 0.9
ADAM_B2 = 0.999
ADAM_EPS = 1e-08
ADAM_WD = 0.01
ADAM_STEP = 10
PER_EXAMPLE_BATCH_AXIS = {'x': 0, 'loss_target': 0}
SHARED_INPUTS = []
_WEIGHT_DTYPES = {'norm_pre': _jnp.float32, 'w_in': _jnp.float32, 'rel_bias': _jnp.float32, 'attn_sinks': _jnp.float32, 'lb_logits': _jnp.float32, 'hgrn_norm': _jnp.float32, 'w_branch_attn': _jnp.float32, 'w_branch_hgrn': _jnp.float32, 'w_out': _jnp.float32, 'norm_post': _jnp.float32}
MOMENT_SCALE = {'norm_pre': 1.972003e-01, 'w_in': 8.574063e-02, 'rel_bias': 4.773241e-02, 'attn_sinks': 2.135302e-02, 'lb_logits': 1.574521e-02, 'hgrn_norm': 1.999075e-01, 'w_branch_attn': 2.743780e-02, 'w_branch_hgrn': 1.404481e-01, 'w_out': 1.455924e-01, 'norm_post': 1.610144e+01}


def _to_microbatches(a, axis):
    t = _jnp.moveaxis(a, axis, 0)
    t = t.reshape((N_MICROBATCH, t.shape[0] // N_MICROBATCH) + t.shape[1:])
    return _jnp.moveaxis(t, 1, axis + 1)


def setup_inputs(seed: int = 0) -> dict:
    inp = _fwd_setup_inputs(seed)
    key = _jax.random.fold_in(_jax.random.key(seed), 7919)
    shape, _ = _output_shape()
    out = dict(inp)
    out["loss_target"] = _jax.random.normal(_jax.random.fold_in(key, 0), shape, _jnp.float32)
    for i, name in enumerate(TWIN_WEIGHTS):
        w = inp[name].astype(_jnp.float32)
        if MOMENT_SCALE is None:
            s = _jnp.sqrt(_jnp.mean(_jnp.square(w)) + 1e-30)
        else:
            s = MOMENT_SCALE[name]
        km, kv = _jax.random.split(_jax.random.fold_in(key, i + 1))
        out[name] = w
        out["m_" + name] = s * _jax.random.normal(km, w.shape, _jnp.float32)
        out["v_" + name] = (s * s) * _jax.random.uniform(kv, w.shape, _jnp.float32, 0.5, 1.5)
    if N_MICROBATCH > 1:
        for name, axis in PER_EXAMPLE_BATCH_AXIS.items():
            out[name] = _to_microbatches(out[name], axis)
    return {'x': out['x'], 'norm_pre': out['norm_pre'], 'w_in': out['w_in'], 'rel_bias': out['rel_bias'], 'attn_sinks': out['attn_sinks'], 'lb_logits': out['lb_logits'], 'hgrn_norm': out['hgrn_norm'], 'w_branch_attn': out['w_branch_attn'], 'w_branch_hgrn': out['w_branch_hgrn'], 'w_out': out['w_out'], 'norm_post': out['norm_post'], 'loss_target': out['loss_target'], 'm_norm_pre': out['m_norm_pre'], 'm_w_in': out['m_w_in'], 'm_rel_bias': out['m_rel_bias'], 'm_attn_sinks': out['m_attn_sinks'], 'm_lb_logits': out['m_lb_logits'], 'm_hgrn_norm': out['m_hgrn_norm'], 'm_w_branch_attn': out['m_w_branch_attn'], 'm_w_branch_hgrn': out['m_w_branch_hgrn'], 'm_w_out': out['m_w_out'], 'm_norm_post': out['m_norm_post'], 'v_norm_pre': out['v_norm_pre'], 'v_w_in': out['v_w_in'], 'v_rel_bias': out['v_rel_bias'], 'v_attn_sinks': out['v_attn_sinks'], 'v_lb_logits': out['v_lb_logits'], 'v_hgrn_norm': out['v_hgrn_norm'], 'v_w_branch_attn': out['v_w_branch_attn'], 'v_w_branch_hgrn': out['v_w_branch_hgrn'], 'v_w_out': out['v_w_out'], 'v_norm_post': out['v_norm_post']}


def _loss(weights, diff, rest, loss_target):
    with _jax.named_scope("forward"):
        args = {**rest, TWIN_DIFF_INPUT: diff, **{k: w.astype(_WEIGHT_DTYPES[k]) for k, w in weights.items()}}
        y = _forward(args)
    with _jax.named_scope("loss_head"):
        err = _jnp.square(y.astype(_jnp.float32) - loss_target)
        return 0.5 * _jnp.sum(_jnp.mean(err, axis=-1)) if err.ndim else 0.5 * err


def _adamw(w, g, m, v):
    m = ADAM_B1 * m + (1.0 - ADAM_B1) * g
    v = ADAM_B2 * v + (1.0 - ADAM_B2) * _jnp.square(g)
    m_hat = m / (1.0 - ADAM_B1 ** ADAM_STEP)
    v_hat = v / (1.0 - ADAM_B2 ** ADAM_STEP)
    delta = -ADAM_LR * (m_hat / (_jnp.sqrt(v_hat) + ADAM_EPS) + ADAM_WD * w)
    return delta, m, v


def reference(x, norm_pre, w_in, rel_bias, attn_sinks, lb_logits, hgrn_norm, w_branch_attn, w_branch_hgrn, w_out, norm_post, loss_target, m_norm_pre, m_w_in, m_rel_bias, m_attn_sinks, m_lb_logits, m_hgrn_norm, m_w_branch_attn, m_w_branch_hgrn, m_w_out, m_norm_post, v_norm_pre, v_w_in, v_rel_bias, v_attn_sinks, v_lb_logits, v_hgrn_norm, v_w_branch_attn, v_w_branch_hgrn, v_w_out, v_norm_post):
    given = dict(x=x, norm_pre=norm_pre, w_in=w_in, rel_bias=rel_bias, attn_sinks=attn_sinks, lb_logits=lb_logits, hgrn_norm=hgrn_norm, w_branch_attn=w_branch_attn, w_branch_hgrn=w_branch_hgrn, w_out=w_out, norm_post=norm_post, loss_target=loss_target, m_norm_pre=m_norm_pre, m_w_in=m_w_in, m_rel_bias=m_rel_bias, m_attn_sinks=m_attn_sinks, m_lb_logits=m_lb_logits, m_hgrn_norm=m_hgrn_norm, m_w_branch_attn=m_w_branch_attn, m_w_branch_hgrn=m_w_branch_hgrn, m_w_out=m_w_out, m_norm_post=m_norm_post, v_norm_pre=v_norm_pre, v_w_in=v_w_in, v_rel_bias=v_rel_bias, v_attn_sinks=v_attn_sinks, v_lb_logits=v_lb_logits, v_hgrn_norm=v_hgrn_norm, v_w_branch_attn=v_w_branch_attn, v_w_branch_hgrn=v_w_branch_hgrn, v_w_out=v_w_out, v_norm_post=v_norm_post)
    weights = {n: given[n] for n in TWIN_WEIGHTS}
    shared = {n: given[n] for n in SHARED_INPUTS}
    per_example = {n: given[n] for n in ['x']}
    grad_fn = _jax.value_and_grad(_loss, argnums=(0, 1))

    def one_microbatch(ex, loss_target):
        ex = dict(ex)
        diff = ex.pop(TWIN_DIFF_INPUT)
        return grad_fn(weights, diff, {**shared, **ex}, loss_target)

    if N_MICROBATCH == 1:
        loss, (grad_w, grad_x) = one_microbatch(per_example, given["loss_target"])
    else:
        def body(carry, xs):
            loss_sum, grad_sum = carry
            l_k, (gw_k, gx_k) = one_microbatch(xs[0], xs[1])
            with _jax.named_scope("update"):
                return (loss_sum + l_k, _jax.tree.map(_jnp.add, grad_sum, gw_k)), gx_k

        init = (_jnp.zeros((), _jnp.float32), _jax.tree.map(_jnp.zeros_like, weights))
        (loss, grad_w), grad_x = _jax.lax.scan(body, init, (per_example, given["loss_target"]))
    with _jax.named_scope("update"):
        delta_w, new_m, new_v = {}, {}, {}
        for n in TWIN_WEIGHTS:
            delta_w[n], new_m[n], new_v[n] = _adamw(weights[n], grad_w[n], given["m_" + n], given["v_" + n])
    return (loss, grad_x, *[grad_w[n] for n in TWIN_WEIGHTS], *[delta_w[n] for n in TWIN_WEIGHTS],
            *[new_m[n] for n in TWIN_WEIGHTS], *[new_v[n] for n in TWIN_WEIGHTS])
```

```python
import functools
import math

import jax
import jax.numpy as jnp
from jax import lax
from jax.experimental import pallas as pl
from jax.experimental.pallas import tpu as pltpu

F32 = jnp.float32
BF16 = jnp.bfloat16
MESH = pl.DeviceIdType.MESH

ATTN_HEADS = 16
ATTN_KV_HEADS = 4
HEAD_DIM = 64
GROUP = ATTN_HEADS // ATTN_KV_HEADS
WINDOW = 128
ATTN_WIDTH = ATTN_HEADS * HEAD_DIM
KV_WIDTH = ATTN_KV_HEADS * HEAD_DIM
HGRN_HEADS = 8
HGRN_DIM = 128
HGRN_WIDTH = HGRN_HEADS * HGRN_DIM
CHUNK = 64
SUB = 16
NSUB = CHUNK // SUB
REL_BUCKETS = 32
REL_MAX_DIST = 128
NORM_EPS = 1e-6
ADAM_LR = 0.001
ADAM_B1 = 0.9
ADAM_B2 = 0.999
ADAM_EPS = 1e-08
ADAM_WD = 0.01
ADAM_STEP = 10
LANES = 128
N_CHIPS = 4
N_DEV = 8
VMEM_LIMIT = 48 * 1024 * 1024

OFF_AQ = 0
OFF_AK = OFF_AQ + ATTN_WIDTH
OFF_AV = OFF_AK + KV_WIDTH
OFF_AG = OFF_AV + KV_WIDTH
OFF_HQ = OFF_AG + ATTN_WIDTH
OFF_HF = OFF_HQ + HGRN_WIDTH
OFF_HI = OFF_HF + HGRN_WIDTH
OFF_HG = OFF_HI + HGRN_WIDTH
OFF_GA = OFF_HG + HGRN_WIDTH

NT_DIMS = (((1,), (1,)), ((), ()))
TN_DIMS = (((0,), (0,)), ((), ()))
NN_DIMS = (((1,), (0,)), ((), ()))


def _pick(n, cands):
    for c in cands:
        if n % c == 0:
            return c
    raise ValueError(f"no tile for {n} in {cands}")


def _params(sem):
    return pltpu.CompilerParams(dimension_semantics=sem, vmem_limit_bytes=VMEM_LIMIT)


def _bdot(a, b, dims=NN_DIMS):
    return lax.dot_general(a.astype(BF16), b.astype(BF16), dims, preferred_element_type=F32)


def _matmul(a, b, mode, out_dtype, name, slabs=1):
    if mode == "nn":
        (M, K), (K2, N) = a.shape, b.shape
    elif mode == "nt":
        (M, K), (N, K2) = a.shape, b.shape
    else:
        (K, M), (K2, N) = a.shape, b.shape
    assert K == K2
    nslab = N // slabs
    tm = _pick(M, (1024, 512, 256, 128))
    tk = _pick(K, (512, 256, 128))
    tn = _pick(nslab, (1024, 896, 512, 384, 256, 128))
    nk = K // tk
    per = nslab // tn
    dims = {"nn": NN_DIMS, "nt": NT_DIMS, "tn": TN_DIMS}[mode]

    def body(a_ref, b_ref, o_ref, acc_ref):
        k = pl.program_id(2)

        @pl.when(k == 0)
        def _():
            acc_ref[...] = jnp.zeros_like(acc_ref)

        acc_ref[...] += lax.dot_general(a_ref[...], b_ref[...], dims, preferred_element_type=F32)

        @pl.when(k == nk - 1)
        def _():
            o_ref[...] = acc_ref[...].astype(o_ref.dtype)

    if mode == "tn":
        a_spec = pl.BlockSpec((tk, tm), lambda i, j, k: (k, i))
    else:
        a_spec = pl.BlockSpec((tm, tk), lambda i, j, k: (i, k))
    if mode == "nt":
        b_spec = pl.BlockSpec((tn, tk), lambda i, j, k: (j, k))
    else:
        b_spec = pl.BlockSpec((tk, tn), lambda i, j, k: (k, j))
    if slabs == 1:
        o_shape = jax.ShapeDtypeStruct((M, N), out_dtype)
        o_spec = pl.BlockSpec((tm, tn), lambda i, j, k: (i, j))
    else:
        o_shape = jax.ShapeDtypeStruct((slabs, M, nslab), out_dtype)
        o_spec = pl.BlockSpec((None, tm, tn), lambda i, j, k: (j // per, i, j % per))
    return pl.pallas_call(
        body, name=name, grid=(M // tm, N // tn, nk), in_specs=[a_spec, b_spec], out_specs=o_spec, out_shape=o_shape,
        scratch_shapes=[pltpu.VMEM((tm, tn), F32)],
        compiler_params=_params(("parallel", "parallel", "arbitrary")),
    )(a, b)


def _rmsnorm_fwd(x2, gain):
    T, D = x2.shape
    tr = _pick(T, (256, 128))

    def body(x_ref, g_ref, h_ref, r_ref):
        xv = x_ref[...]
        r = lax.rsqrt(jnp.mean(xv * xv, axis=-1, keepdims=True) + NORM_EPS)
        h_ref[...] = (xv * r * g_ref[...]).astype(BF16)
        r_ref[...] = r

    return pl.pallas_call(
        body, name="rmsnorm_pre_fwd", grid=(T // tr,),
        in_specs=[pl.BlockSpec((tr, D), lambda i: (i, 0)), pl.BlockSpec((1, D), lambda i: (0, 0))],
        out_specs=[pl.BlockSpec((tr, D), lambda i: (i, 0)), pl.BlockSpec((tr, 1), lambda i: (i, 0))],
        out_shape=[jax.ShapeDtypeStruct((T, D), BF16), jax.ShapeDtypeStruct((T, 1), F32)],
        compiler_params=_params(("parallel",)),
    )(x2, gain)


def _rmsnorm_bwd(dh, x2, rstd, gain, dout):
    T, D = x2.shape
    tr = _pick(T, (256, 128))

    def body(dh_ref, x_ref, r_ref, g_ref, do_ref, gx_ref, gg_ref):
        @pl.when(pl.program_id(0) == 0)
        def _():
            gg_ref[...] = jnp.zeros_like(gg_ref)

        n = x_ref[...] * r_ref[...]
        dhv = dh_ref[...]
        dn = dhv * g_ref[...]
        gx_ref[...] = do_ref[...] + r_ref[...] * (dn - n * jnp.mean(dn * n, axis=-1, keepdims=True))
        gg_ref[...] += jnp.sum(dhv * n, axis=0, keepdims=True)

    row = pl.BlockSpec((tr, D), lambda i: (i, 0))
    vec = pl.BlockSpec((1, D), lambda i: (0, 0))
    return pl.pallas_call(
        body, name="rmsnorm_pre_bwd", grid=(T // tr,),
        in_specs=[row, row, pl.BlockSpec((tr, 1), lambda i: (i, 0)), vec, row],
        out_specs=[row, vec],
        out_shape=[jax.ShapeDtypeStruct((T, D), F32), jax.ShapeDtypeStruct((1, D), F32)],
        compiler_params=_params(("arbitrary",)),
    )(dh, x2, rstd, gain, dout)


def _post_loss(yv, x2, tgt2, gain):
    T, D = x2.shape
    tr = _pick(T, (256, 128))

    def body(y_ref, x_ref, t_ref, g_ref, dy_ref, do_ref, loss_ref, gg_ref):
        @pl.when(pl.program_id(0) == 0)
        def _():
            gg_ref[...] = jnp.zeros_like(gg_ref)
            loss_ref[...] = jnp.zeros_like(loss_ref)

        yv_ = y_ref[...]
        r = lax.rsqrt(jnp.mean(yv_ * yv_, axis=-1, keepdims=True) + NORM_EPS)
        n = yv_ * r
        e = (x_ref[...] + n * g_ref[...]) - t_ref[...]
        loss_ref[...] += 0.5 * jnp.sum(jnp.mean(e * e, axis=-1, keepdims=True), axis=0, keepdims=True)
        dz = e / D
        do_ref[...] = dz
        gg_ref[...] += jnp.sum(dz * n, axis=0, keepdims=True)
        dn = dz * g_ref[...]
        dy_ref[...] = (r * (dn - n * jnp.mean(dn * n, axis=-1, keepdims=True))).astype(BF16)

    row = pl.BlockSpec((tr, D), lambda i: (i, 0))
    vec = pl.BlockSpec((1, D), lambda i: (0, 0))
    return pl.pallas_call(
        body, name="post_norm_loss", grid=(T // tr,),
        in_specs=[row, row, row, vec],
        out_specs=[row, row, pl.BlockSpec((1, 1), lambda i: (0, 0)), vec],
        out_shape=[jax.ShapeDtypeStruct((T, D), BF16), jax.ShapeDtypeStruct((T, D), F32),
                   jax.ShapeDtypeStruct((1, 1), F32), jax.ShapeDtypeStruct((1, D), F32)],
        compiler_params=_params(("arbitrary",)),
    )(yv, x2, tgt2, gain)


def _merge_fwd(proj, ua, uh):
    T, D = ua.shape
    tr = _pick(T, (256, 128))
    bw = _pick(D, (512, 256))
    assert OFF_GA % bw == 0
    oa, oh = OFF_GA // bw, (OFF_GA + D) // bw

    def body(ga_ref, gh_ref, ua_ref, uh_ref, m_ref):
        m_ref[...] = (jax.nn.sigmoid(ga_ref[...]) * ua_ref[...] + jax.nn.sigmoid(gh_ref[...]) * uh_ref[...]).astype(BF16)

    blk = pl.BlockSpec((tr, bw), lambda i, j: (i, j))
    return pl.pallas_call(
        body, name="merge_fwd", grid=(T // tr, D // bw),
        in_specs=[pl.BlockSpec((tr, bw), lambda i, j: (i, oa + j)), pl.BlockSpec((tr, bw), lambda i, j: (i, oh + j)), blk, blk],
        out_specs=blk, out_shape=jax.ShapeDtypeStruct((T, D), BF16),
        compiler_params=_params(("parallel", "parallel")),
    )(proj, proj, ua, uh)


def _merge_bwd(proj, ua, uh, dm):
    T, D = ua.shape
    tr = _pick(T, (256, 128))
    bw = _pick(D, (512, 256))
    oa, oh = OFF_GA // bw, (OFF_GA + D) // bw

    def body(ga_ref, gh_ref, ua_ref, uh_ref, dm_ref, dua_ref, duh_ref, dga_ref, dgh_ref):
        sa = jax.nn.sigmoid(ga_ref[...])
        sh = jax.nn.sigmoid(gh_ref[...])
        d = dm_ref[...]
        dua_ref[...] = (d * sa).astype(BF16)
        duh_ref[...] = (d * sh).astype(BF16)
        dga_ref[...] = (d * ua_ref[...] * sa * (1.0 - sa)).astype(BF16)
        dgh_ref[...] = (d * uh_ref[...] * sh * (1.0 - sh)).astype(BF16)

    blk = pl.BlockSpec((tr, bw), lambda i, j: (i, j))
    o = jax.ShapeDtypeStruct((T, D), BF16)
    return pl.pallas_call(
        body, name="merge_bwd", grid=(T // tr, D // bw),
        in_specs=[pl.BlockSpec((tr, bw), lambda i, j: (i, oa + j)), pl.BlockSpec((tr, bw), lambda i, j: (i, oh + j)), blk, blk, blk],
        out_specs=[blk, blk, blk, blk], out_shape=[o, o, o, o],
        compiler_params=_params(("parallel", "parallel")),
    )(proj, proj, ua, uh, dm)


KV_PAIR = 2
PAIR_HEADS = KV_PAIR * GROUP


def _attn_mask(n):
    qi = lax.broadcasted_iota(jnp.int32, (WINDOW, 2 * WINDOW), 0)
    si = lax.broadcasted_iota(jnp.int32, (WINDOW, 2 * WINDOW), 1)
    dist = qi + WINDOW - si
    return (dist >= 0) & (dist < WINDOW) & ((si >= WINDOW) | (n > 0))


def _attn_group_fn(mask):
    def f(q4, k, v, ag4, bias4, sink4):
        kb = k.astype(BF16)
        vb = v.astype(BF16)
        outs = []
        for g in range(GROUP):
            s = lax.dot_general(q4[g].astype(BF16), kb, NT_DIMS, preferred_element_type=F32) * (HEAD_DIM ** -0.5)
            s = jnp.where(mask, s + bias4[g], -1e30)
            m = lax.stop_gradient(jnp.maximum(jnp.max(s, axis=-1, keepdims=True), sink4[g]))
            p = jnp.exp(s - m)
            den = jnp.sum(p, axis=-1, keepdims=True) + jnp.exp(sink4[g] - m)
            o = jnp.dot((p / den).astype(BF16), vb, preferred_element_type=F32)
            outs.append(o * jax.nn.silu(ag4[g]))
        return outs

    return f


def _attn_specs(B_loc, nb, order):
    qw = PAIR_HEADS * HEAD_DIM
    kw = KV_PAIR * HEAD_DIM

    def rows(g):
        b, p, n = order(*g)
        return b * nb + n

    def prev(g):
        b, p, n = order(*g)
        return b * nb + jnp.maximum(n - 1, 0)

    def pp(g):
        return order(*g)[1]

    q = pl.BlockSpec((WINDOW, qw), lambda *g: (rows(g), OFF_AQ // qw + pp(g)))
    kc = pl.BlockSpec((WINDOW, kw), lambda *g: (rows(g), OFF_AK // kw + pp(g)))
    kp = pl.BlockSpec((WINDOW, kw), lambda *g: (prev(g), OFF_AK // kw + pp(g)))
    vc = pl.BlockSpec((WINDOW, kw), lambda *g: (rows(g), OFF_AV // kw + pp(g)))
    vp = pl.BlockSpec((WINDOW, kw), lambda *g: (prev(g), OFF_AV // kw + pp(g)))
    ag = pl.BlockSpec((WINDOW, qw), lambda *g: (rows(g), OFF_AG // qw + pp(g)))
    bias = pl.BlockSpec((PAIR_HEADS, WINDOW, 2 * WINDOW), lambda *g: (pp(g), 0, 0))
    sink = pl.BlockSpec((PAIR_HEADS, 8, LANES), lambda *g: (pp(g), 0, 0))
    return [q, kc, kp, vc, vp, ag, bias, sink], rows, pp


def _attn_operands(q_ref, kc_ref, kp_ref, vc_ref, vp_ref, ag_ref, bias_ref, sink_ref, j):
    lo, hi = j * HEAD_DIM, (j + 1) * HEAD_DIM
    k = jnp.concatenate([kp_ref[:, lo:hi], kc_ref[:, lo:hi]], axis=0)
    v = jnp.concatenate([vp_ref[:, lo:hi], vc_ref[:, lo:hi]], axis=0)
    q4 = [q_ref[:, (j * GROUP + g) * HEAD_DIM:(j * GROUP + g + 1) * HEAD_DIM] for g in range(GROUP)]
    ag4 = [ag_ref[:, (j * GROUP + g) * HEAD_DIM:(j * GROUP + g + 1) * HEAD_DIM] for g in range(GROUP)]
    bias4 = [bias_ref[j * GROUP + g] for g in range(GROUP)]
    sink4 = [sink_ref[j * GROUP + g, 0:1, 0:1] for g in range(GROUP)]
    return q4, k, v, ag4, bias4, sink4


def _attn_fwd(proj, bias_tab, sinks_b, B_loc, S):
    T = B_loc * S
    nb = S // WINDOW
    n_pairs = ATTN_KV_HEADS // KV_PAIR
    in_specs, rows, pp = _attn_specs(B_loc, nb, lambda b, p, n: (b, p, n))

    def body(q_ref, kc_ref, kp_ref, vc_ref, vp_ref, ag_ref, bias_ref, sink_ref, ya_ref):
        f = _attn_group_fn(_attn_mask(pl.program_id(2)))
        for j in range(KV_PAIR):
            outs = f(*_attn_operands(q_ref, kc_ref, kp_ref, vc_ref, vp_ref, ag_ref, bias_ref, sink_ref, j))
            for g in range(GROUP):
                h = j * GROUP + g
                ya_ref[:, h * HEAD_DIM:(h + 1) * HEAD_DIM] = outs[g].astype(BF16)

    return pl.pallas_call(
        body, name="attn_fwd", grid=(B_loc, n_pairs, nb), in_specs=in_specs,
        out_specs=pl.BlockSpec((WINDOW, PAIR_HEADS * HEAD_DIM), lambda *g: (rows(g), pp(g))),
        out_shape=jax.ShapeDtypeStruct((T, ATTN_WIDTH), BF16),
        compiler_params=_params(("parallel", "parallel", "parallel")),
    )(proj, proj, proj, proj, proj, proj, bias_tab, sinks_b)


def _attn_bwd(proj, bias_tab, sinks_b, dya, B_loc, S):
    T = B_loc * S
    nb = S // WINDOW
    n_pairs = ATTN_KV_HEADS // KV_PAIR
    order = lambda p, b, i: (b, p, nb - 1 - i)
    in_specs, rows, pp = _attn_specs(B_loc, nb, order)
    qw = PAIR_HEADS * HEAD_DIM
    kw = KV_PAIR * HEAD_DIM

    def body(q_ref, kc_ref, kp_ref, vc_ref, vp_ref, ag_ref, bias_ref, sink_ref, dya_ref,
             dq_ref, dk_ref, dv_ref, dag_ref, dbias_ref, dsink_ref, dkc_ref, dvc_ref):
        b, i = pl.program_id(1), pl.program_id(2)
        n = nb - 1 - i

        @pl.when((b == 0) & (i == 0))
        def _():
            dbias_ref[...] = jnp.zeros_like(dbias_ref)
            dsink_ref[...] = jnp.zeros_like(dsink_ref)

        @pl.when(i == 0)
        def _():
            dkc_ref[...] = jnp.zeros_like(dkc_ref)
            dvc_ref[...] = jnp.zeros_like(dvc_ref)

        f = _attn_group_fn(_attn_mask(n))
        for j in range(KV_PAIR):
            ops = _attn_operands(q_ref, kc_ref, kp_ref, vc_ref, vp_ref, ag_ref, bias_ref, sink_ref, j)
            _, vjp = jax.vjp(f, *ops)
            douts = [dya_ref[:, (j * GROUP + g) * HEAD_DIM:(j * GROUP + g + 1) * HEAD_DIM].astype(F32) for g in range(GROUP)]
            dq4, dk, dv, dag4, dbias4, dsink4 = vjp(douts)
            lo, hi = j * HEAD_DIM, (j + 1) * HEAD_DIM
            dk_ref[:, lo:hi] = (dk[WINDOW:] + dkc_ref[:, lo:hi]).astype(BF16)
            dv_ref[:, lo:hi] = (dv[WINDOW:] + dvc_ref[:, lo:hi]).astype(BF16)
            dkc_ref[:, lo:hi] = dk[:WINDOW]
            dvc_ref[:, lo:hi] = dv[:WINDOW]
            for g in range(GROUP):
                h = j * GROUP + g
                dq_ref[:, h * HEAD_DIM:(h + 1) * HEAD_DIM] = dq4[g].astype(BF16)
                dag_ref[:, h * HEAD_DIM:(h + 1) * HEAD_DIM] = dag4[g].astype(BF16)
                dbias_ref[h] += dbias4[g]
                dsink_ref[h] += jnp.broadcast_to(dsink4[g], (8, LANES))

    wide = pl.BlockSpec((WINDOW, qw), lambda *g: (rows(g), pp(g)))
    narrow = pl.BlockSpec((WINDOW, kw), lambda *g: (rows(g), pp(g)))
    return pl.pallas_call(
        body, name="attn_bwd", grid=(n_pairs, B_loc, nb), in_specs=in_specs + [wide],
        out_specs=[wide, narrow, narrow, wide,
                   pl.BlockSpec((PAIR_HEADS, WINDOW, 2 * WINDOW), lambda *g: (pp(g), 0, 0)),
                   pl.BlockSpec((PAIR_HEADS, 8, LANES), lambda *g: (pp(g), 0, 0))],
        out_shape=[jax.ShapeDtypeStruct((T, ATTN_WIDTH), BF16), jax.ShapeDtypeStruct((T, KV_WIDTH), BF16),
                   jax.ShapeDtypeStruct((T, KV_WIDTH), BF16), jax.ShapeDtypeStruct((T, ATTN_WIDTH), BF16),
                   jax.ShapeDtypeStruct((ATTN_HEADS, WINDOW, 2 * WINDOW), F32),
                   jax.ShapeDtypeStruct((ATTN_HEADS, 8, LANES), F32)],
        scratch_shapes=[pltpu.VMEM((WINDOW, kw), F32), pltpu.VMEM((WINDOW, kw), F32)],
        compiler_params=_params(("arbitrary", "arbitrary", "arbitrary")),
    )(proj, proj, proj, proj, proj, proj, bias_tab, sinks_b, dya)


def _hgrn_chunk_fn(state_t, qs, fs, vs, hgs, lb, gain):
    r = lax.broadcasted_iota(jnp.int32, (SUB, SUB), 0)
    c = lax.broadcasted_iota(jnp.int32, (SUB, SUB), 1)
    tri = (r >= c).astype(F32)
    ref = jnp.zeros((1, HGRN_DIM), F32)
    qsil, kk, gg, refs = [], [], [], []
    for i in range(NSUB):
        f = lb + (1.0 - lb) * jax.nn.sigmoid(fs[i])
        logf = jnp.log(f)
        g = jnp.dot(tri, logf, precision=lax.Precision.HIGHEST, preferred_element_type=F32) + ref
        refs.append(ref)
        gg.append(g)
        kk.append(1.0 - f)
        qsil.append(jax.nn.silu(qs[i]))
        ref = ref + jnp.sum(logf, axis=0, keepdims=True)
    g_end = ref
    ys = []
    for i in range(NSUB):
        inter = _bdot(qsil[i] * jnp.exp(gg[i]), state_t, NT_DIMS)
        qd = qsil[i] * jnp.exp(gg[i] - refs[i])
        kcat = jnp.concatenate([kk[j] * jnp.exp(refs[i] - gg[j]) for j in range(i + 1)], axis=0)
        vcat = jnp.concatenate(vs[:i + 1], axis=0)
        a = _bdot(qd, kcat, NT_DIMS)
        rr = lax.broadcasted_iota(jnp.int32, a.shape, 0)
        cc = lax.broadcasted_iota(jnp.int32, a.shape, 1)
        a = jnp.where(cc - i * SUB <= rr, a, 0.0)
        o = inter + _bdot(a, vcat)
        rs = lax.rsqrt(jnp.mean(o * o, axis=-1, keepdims=True) + NORM_EPS)
        ys.append(o * rs * gain * jax.nn.silu(hgs[i]))
    kdec = jnp.concatenate([kk[j] * jnp.exp(g_end - gg[j]) for j in range(NSUB)], axis=0)
    vall = jnp.concatenate(vs, axis=0)
    new_state_t = state_t * jnp.exp(g_end) + _bdot(vall, kdec, TN_DIMS)
    return new_state_t, ys


def _hgrn_specs(nc, order):
    def rows(g):
        b, h, n = order(*g)
        return b * nc + n

    def hh(g):
        return order(*g)[1]

    def col(off):
        return pl.BlockSpec((CHUNK, HGRN_DIM), lambda *g: (rows(g), off // HGRN_DIM + hh(g)))

    vec = pl.BlockSpec((1, HGRN_DIM), lambda *g: (0, hh(g)))
    return [col(OFF_HQ), col(OFF_HF), col(OFF_HI), col(OFF_HG), vec, vec], rows, hh


def _sub_blocks(ref):
    return [ref[i * SUB:(i + 1) * SUB, :] for i in range(NSUB)]


def _hgrn_fwd(proj, lb, gain, B_loc, S):
    T = B_loc * S
    nc = S // CHUNK
    order = lambda b, h, n: (b, h, n)
    in_specs, rows, hh = _hgrn_specs(nc, order)

    def body(q_ref, f_ref, v_ref, hg_ref, lb_ref, gain_ref, yh_ref, st_ref, state_scr):
        @pl.when(pl.program_id(2) == 0)
        def _():
            state_scr[...] = jnp.zeros_like(state_scr)

        st = state_scr[...]
        st_ref[...] = st
        new, ys = _hgrn_chunk_fn(st, _sub_blocks(q_ref), _sub_blocks(f_ref), _sub_blocks(v_ref), _sub_blocks(hg_ref),
                                 lb_ref[...], gain_ref[...])
        state_scr[...] = new
        for i in range(NSUB):
            yh_ref[i * SUB:(i + 1) * SUB, :] = ys[i].astype(BF16)

    return pl.pallas_call(
        body, name="hgrn_fwd", grid=(B_loc, HGRN_HEADS, nc), in_specs=in_specs,
        out_specs=[pl.BlockSpec((CHUNK, HGRN_DIM), lambda *g: (rows(g), hh(g))),
                   pl.BlockSpec((None, None, None, HGRN_DIM, HGRN_DIM), lambda b, h, n: (b, h, n, 0, 0))],
        out_shape=[jax.ShapeDtypeStruct((T, HGRN_WIDTH), BF16),
                   jax.ShapeDtypeStruct((B_loc, HGRN_HEADS, nc, HGRN_DIM, HGRN_DIM), F32)],
        scratch_shapes=[pltpu.VMEM((HGRN_DIM, HGRN_DIM), F32)],
        compiler_params=_params(("parallel", "parallel", "arbitrary")),
    )(proj, proj, proj, proj, lb, gain)


def _hgrn_bwd(proj, states, lb, gain, dyh, B_loc, S):
    T = B_loc * S
    nc = S // CHUNK
    order = lambda b, h, i: (b, h, nc - 1 - i)
    in_specs, rows, hh = _hgrn_specs(nc, order)
    blk = pl.BlockSpec((CHUNK, HGRN_DIM), lambda *g: (rows(g), hh(g)))
    st_spec = pl.BlockSpec((None, None, None, HGRN_DIM, HGRN_DIM), lambda b, h, i: (b, h, nc - 1 - i, 0, 0))
    acc_spec = pl.BlockSpec((None, None, 1, HGRN_DIM), lambda b, h, i: (b, h, 0, 0))

    def body(q_ref, f_ref, v_ref, hg_ref, lb_ref, gain_ref, st_ref, dyh_ref,
             dq_ref, df_ref, dv_ref, dhg_ref, dlb_ref, dgain_ref, dstate_scr):
        @pl.when(pl.program_id(2) == 0)
        def _():
            dstate_scr[...] = jnp.zeros_like(dstate_scr)
            dlb_ref[...] = jnp.zeros_like(dlb_ref)
            dgain_ref[...] = jnp.zeros_like(dgain_ref)

        _, vjp = jax.vjp(_hgrn_chunk_fn, st_ref[...], _sub_blocks(q_ref), _sub_blocks(f_ref), _sub_blocks(v_ref),
                         _sub_blocks(hg_ref), lb_ref[...], gain_ref[...])
        dys = [dyh_ref[i * SUB:(i + 1) * SUB, :].astype(F32) for i in range(NSUB)]
        dst, dqs, dfs, dvs, dhgs, dlb, dgain = vjp((dstate_scr[...], dys))
        dstate_scr[...] = dst
        dlb_ref[...] += dlb
        dgain_ref[...] += dgain
        for i in range(NSUB):
            sl = slice(i * SUB, (i + 1) * SUB)
            dq_ref[sl, :] = dqs[i].astype(BF16)
            df_ref[sl, :] = dfs[i].astype(BF16)
            dv_ref[sl, :] = dvs[i].astype(BF16)
            dhg_ref[sl, :] = dhgs[i].astype(BF16)

    o = jax.ShapeDtypeStruct((T, HGRN_WIDTH), BF16)
    acc = jax.ShapeDtypeStruct((B_loc, HGRN_HEADS, 1, HGRN_DIM), F32)
    return pl.pallas_call(
        body, name="hgrn_bwd", grid=(B_loc, HGRN_HEADS, nc), in_specs=in_specs + [st_spec, blk],
        out_specs=[blk, blk, blk, blk, acc_spec, acc_spec], out_shape=[o, o, o, o, acc, acc],
        scratch_shapes=[pltpu.VMEM((HGRN_DIM, HGRN_DIM), F32)],
        compiler_params=_params(("parallel", "parallel", "arbitrary")),
    )(proj, proj, proj, proj, lb, gain, states, dyh)


def _adamw(w, g, m, v, name):
    R, C = w.shape
    tr = _pick(R, (128, 64, 32, 16, 8)) if C > 1024 else _pick(R, (512, 256, 128, 64, 32, 16, 8))

    def body(w_ref, g_ref, m_ref, v_ref, d_ref, nm_ref, nv_ref):
        gv = g_ref[...]
        nm = ADAM_B1 * m_ref[...] + (1.0 - ADAM_B1) * gv
        nv = ADAM_B2 * v_ref[...] + (1.0 - ADAM_B2) * (gv * gv)
        m_hat = nm / (1.0 - ADAM_B1 ** ADAM_STEP)
        v_hat = nv / (1.0 - ADAM_B2 ** ADAM_STEP)
        d_ref[...] = -ADAM_LR * (m_hat / (jnp.sqrt(v_hat) + ADAM_EPS) + ADAM_WD * w_ref[...])
        nm_ref[...] = nm
        nv_ref[...] = nv

    blk = pl.BlockSpec((tr, C), lambda i: (i, 0))
    o = jax.ShapeDtypeStruct((R, C), F32)
    return pl.pallas_call(
        body, name=name, grid=(R // tr,), in_specs=[blk] * 4, out_specs=[blk] * 3, out_shape=[o, o, o],
        compiler_params=_params(("parallel",)),
    )(w, g, m, v)


ANY = pl.BlockSpec(memory_space=pl.ANY)
VMEM_SPEC = pl.BlockSpec(memory_space=pltpu.VMEM)


def _place():
    x, y, c = lax.axis_index("x"), lax.axis_index("y"), lax.axis_index("c")
    other_chips = [(1 - x, y), (x, 1 - y), (1 - x, 1 - y)]
    return x, y, c, other_chips


def _gather_weights(shards, axes):
    nw = len(shards)
    fulls = []
    for sh, ax in zip(shards, axes):
        R, C = sh.shape
        fulls.append(jax.ShapeDtypeStruct((R, N_CHIPS * C) if ax == 1 else (N_CHIPS * R, C), sh.dtype))

    def body(*refs):
        ins, outs = refs[:nw], refs[nw:2 * nw]
        send_sems, recv_sems, local_sems = refs[2 * nw:]
        x, y, c, chips = _place()
        me, sibling, s = (x, y, c), (x, y, 1 - c), 2 * x + y

        def region(i, t, half):
            R, C = shards[i].shape
            hr = R // 2
            if axes[i] == 1:
                return outs[i].at[pl.ds(half * hr, hr), pl.ds(pl.multiple_of(t * C, LANES), C)]
            return outs[i].at[pl.ds(t * R + half * hr, hr), :]

        def rcopy(i, k, src, dst, to):
            return pltpu.make_async_remote_copy(src_ref=src, dst_ref=dst, send_sem=send_sems.at[i * 6 + k],
                                                recv_sem=recv_sems.at[i * 6 + k], device_id=to, device_id_type=MESH)

        own = []
        for i in range(nw):
            R, C = shards[i].shape
            dst = outs[i].at[:, pl.ds(pl.multiple_of(s * C, LANES), C)] if axes[i] == 1 else outs[i].at[pl.ds(s * R, R), :]
            cp = pltpu.make_async_copy(ins[i], dst, local_sems.at[i])
            cp.start()
            own.append(cp)
        sent = []
        for i in range(nw):
            hr = shards[i].shape[0] // 2
            for j, chip in enumerate(chips):
                cp = rcopy(i, j, ins[i].at[pl.ds(c * hr, hr), :], region(i, s, c), (*chip, c))
                cp.start()
                sent.append(cp)
        for i in range(nw):
            for j, chip in enumerate(chips):
                reg = region(i, 2 * chip[0] + chip[1], c)
                rcopy(i, j, reg, reg, me).wait_recv()
                cp = rcopy(i, 3 + j, reg, reg, sibling)
                cp.start()
                sent.append(cp)
        for i in range(nw):
            for j, chip in enumerate(chips):
                reg = region(i, 2 * chip[0] + chip[1], 1 - c)
                rcopy(i, 3 + j, reg, reg, me).wait_recv()
        for cp in sent:
            cp.wait_send()
        for cp in own:
            cp.wait()

    return pl.pallas_call(
        body, name="gather_weights", in_specs=[ANY] * nw, out_specs=[ANY] * nw, out_shape=fulls,
        scratch_shapes=[pltpu.SemaphoreType.DMA((6 * nw,)), pltpu.SemaphoreType.DMA((6 * nw,)), pltpu.SemaphoreType.DMA((nw,))],
    )(*shards)


def _exchange_sibling_halves(gs):
    nw = len(gs)

    def body(*refs):
        ins, outs = refs[:nw], refs[nw:2 * nw]
        send_sems, recv_sems = refs[2 * nw:]
        x, y, c, _ = _place()
        cps = []
        for i in range(nw):
            cp = pltpu.make_async_remote_copy(src_ref=ins[i].at[:, 1 - c], dst_ref=outs[i], send_sem=send_sems.at[i],
                                              recv_sem=recv_sems.at[i], device_id=(x, y, 1 - c), device_id_type=MESH)
            cp.start()
            cps.append(cp)
        for cp in cps:
            cp.wait()

    return pl.pallas_call(
        body, name="grads_to_sibling", in_specs=[ANY] * nw, out_specs=[ANY] * nw,
        out_shape=[jax.ShapeDtypeStruct((g.shape[0],) + g.shape[2:], g.dtype) for g in gs],
        scratch_shapes=[pltpu.SemaphoreType.DMA((nw,)), pltpu.SemaphoreType.DMA((nw,))],
    )(*gs)


def _exchange_chips(ss, pack):
    nw = len(ss)
    rows = pack.shape[0]

    def body(*refs):
        ins, pack_ref = refs[:nw], refs[nw]
        outs, sum_ref = refs[nw + 1:2 * nw + 1], refs[2 * nw + 1]
        all_ref, send_sems, recv_sems = refs[2 * nw + 2:]
        x, y, c, chips = _place()
        me = 4 * x + 2 * y + c
        cps = []
        for i in range(nw):
            for j, chip in enumerate(chips):
                cp = pltpu.make_async_remote_copy(
                    src_ref=ins[i].at[2 * chip[0] + chip[1]], dst_ref=outs[i].at[j], send_sem=send_sems.at[i * 3 + j],
                    recv_sem=recv_sems.at[i * 3 + j], device_id=(*chip, c), device_id_type=MESH)
                cp.start()
                cps.append(cp)
        all_ref[me] = pack_ref[...]
        for k in range(1, N_DEV):
            to = (1 - x if k & 4 else x, 1 - y if k & 2 else y, 1 - c if k & 1 else c)
            cp = pltpu.make_async_remote_copy(
                src_ref=pack_ref, dst_ref=all_ref.at[me], send_sem=send_sems.at[nw * 3 + k - 1],
                recv_sem=recv_sems.at[nw * 3 + k - 1], device_id=to, device_id_type=MESH)
            cp.start()
            cps.append(cp)
        for cp in cps:
            cp.wait()
        total = all_ref[0]
        for d in range(1, N_DEV):
            total = total + all_ref[d]
        sum_ref[...] = total

    return pl.pallas_call(
        body, name="grads_to_chips", in_specs=[ANY] * nw + [VMEM_SPEC], out_specs=[ANY] * nw + [VMEM_SPEC],
        out_shape=[jax.ShapeDtypeStruct((N_CHIPS - 1,) + s.shape[1:], s.dtype) for s in ss]
        + [jax.ShapeDtypeStruct(pack.shape, F32)],
        scratch_shapes=[pltpu.VMEM((N_DEV, rows, LANES), F32), pltpu.SemaphoreType.DMA((nw * 3 + N_DEV - 1,)),
                        pltpu.SemaphoreType.DMA((nw * 3 + N_DEV - 1,))],
    )(*ss, pack)


def _share_with_sibling(fs):
    nw = len(fs)

    def body(*refs):
        ins, outs = refs[:nw], refs[nw:2 * nw]
        send_sems, recv_sems, local_sems = refs[2 * nw:]
        x, y, c, _ = _place()
        cps = []
        for i in range(nw):
            lc = pltpu.make_async_copy(ins[i], outs[i].at[c], local_sems.at[i])
            lc.start()
            cp = pltpu.make_async_remote_copy(src_ref=ins[i], dst_ref=outs[i].at[c], send_sem=send_sems.at[i],
                                              recv_sem=recv_sems.at[i], device_id=(x, y, 1 - c), device_id_type=MESH)
            cp.start()
            cps += [lc, cp]
        for cp in cps:
            cp.wait()

    return pl.pallas_call(
        body, name="grads_share_sibling", in_specs=[ANY] * nw, out_specs=[ANY] * nw,
        out_shape=[jax.ShapeDtypeStruct((2,) + f.shape, f.dtype) for f in fs],
        scratch_shapes=[pltpu.SemaphoreType.DMA((nw,)), pltpu.SemaphoreType.DMA((nw,)), pltpu.SemaphoreType.DMA((nw,))],
    )(*fs)


def _sum_sibling(g, land, c_arr, name):
    _, _, P, Q = g.shape
    tp = _pick(P, (256, 128, 64))

    def body(c_ref, g_ref, l_ref, s_ref):
        s_ref[...] = (g_ref[...].astype(F32) + l_ref[...].astype(F32)).astype(BF16)

    return pl.pallas_call(
        body, name=name,
        grid_spec=pltpu.PrefetchScalarGridSpec(
            num_scalar_prefetch=1, grid=(N_CHIPS, P // tp),
            in_specs=[pl.BlockSpec((None, None, tp, Q), lambda t, r, cr: (t, cr[0], r, 0)),
                      pl.BlockSpec((None, tp, Q), lambda t, r, cr: (t, r, 0))],
            out_specs=pl.BlockSpec((None, tp, Q), lambda t, r, cr: (t, r, 0))),
        out_shape=jax.ShapeDtypeStruct((N_CHIPS, P, Q), BF16),
        compiler_params=_params(("parallel", "parallel")),
    )(c_arr, g, land)


def _sum_chips(g, land, recv, sc_arr, name):
    _, _, P, Q = g.shape
    tp = _pick(P, (256, 128, 64))

    def body(sc_ref, g_ref, l_ref, r_ref, f_ref):
        acc = g_ref[...].astype(F32) + l_ref[...].astype(F32)
        for j in range(N_CHIPS - 1):
            acc = acc + r_ref[j].astype(F32)
        f_ref[...] = acc

    return pl.pallas_call(
        body, name=name,
        grid_spec=pltpu.PrefetchScalarGridSpec(
            num_scalar_prefetch=1, grid=(P // tp,),
            in_specs=[pl.BlockSpec((None, None, tp, Q), lambda r, sc: (sc[0], sc[1], r, 0)),
                      pl.BlockSpec((None, tp, Q), lambda r, sc: (sc[0], r, 0)),
                      pl.BlockSpec((N_CHIPS - 1, tp, Q), lambda r, sc: (0, r, 0))],
            out_specs=pl.BlockSpec((tp, Q), lambda r, sc: (r, 0))),
        out_shape=jax.ShapeDtypeStruct((P, Q), F32),
        compiler_params=_params(("parallel",)),
    )(sc_arr, g, land, recv)


def _t5_bucket(dist):
    max_exact = REL_BUCKETS // 2
    d = jnp.maximum(dist, 0)
    df = jnp.maximum(d, 1).astype(F32)
    large = max_exact + (jnp.log(df / max_exact) / math.log(REL_MAX_DIST / max_exact)
                         * (REL_BUCKETS - max_exact)).astype(jnp.int32)
    large = jnp.minimum(large, REL_BUCKETS - 1)
    return jnp.where(d < max_exact, d, large)


def _bucket_table():
    qi = jnp.arange(WINDOW)[:, None]
    si = jnp.arange(2 * WINDOW)[None, :]
    return _t5_bucket(qi + WINDOW - si)


def _rows_of(a):
    flat = a.reshape(-1).astype(F32)
    n = -(-flat.shape[0] // LANES) * LANES
    return jnp.pad(flat, (0, n - flat.shape[0])).reshape(-1, LANES)


def _pack_rows(parts, total_rows):
    rows = [_rows_of(p) for p in parts]
    used = sum(r.shape[0] for r in rows)
    if total_rows > used:
        rows.append(jnp.zeros((total_rows - used, LANES), F32))
    return jnp.concatenate(rows, axis=0)


def _unpack_rows(packed, shapes):
    out, at = [], 0
    for shp in shapes:
        n = math.prod(shp)
        nr = -(-n // LANES)
        out.append(packed[at:at + nr].reshape(-1)[:n].reshape(shp))
        at += nr
    return out


def _round8(n):
    return -(-n // 8) * 8


def kernel(x, norm_pre, w_in, rel_bias, attn_sinks, lb_logits, hgrn_norm, w_branch_attn, w_branch_hgrn, w_out, norm_post, loss_target, m_norm_pre, m_w_in, m_rel_bias, m_attn_sinks, m_lb_logits, m_hgrn_norm, m_w_branch_attn, m_w_branch_hgrn, m_w_out, m_norm_post, v_norm_pre, v_w_in, v_rel_bias, v_attn_sinks, v_lb_logits, v_hgrn_norm, v_w_branch_attn, v_w_branch_hgrn, v_w_out, v_norm_post):
    B_loc, S, D = x.shape
    T = B_loc * S
    x2 = x.reshape(T, D)
    tgt2 = loss_target.reshape(T, D)
    my_x, my_y, my_c = lax.axis_index("x"), lax.axis_index("y"), lax.axis_index("c")

    shards = [w_in[0].astype(BF16), w_branch_attn[0].astype(BF16), w_branch_hgrn[0].astype(BF16), w_out[0].astype(BF16)]
    win_f, wba_f, wbh_f, wout_f = _gather_weights(shards, (1, 1, 1, 0))

    buckets = _bucket_table()
    bias_tab = jnp.transpose(rel_bias[buckets].astype(F32), (2, 0, 1))
    sinks_b = jnp.broadcast_to(attn_sinks[0].astype(F32)[:, None, None], (ATTN_HEADS, 8, LANES))
    lb_fn = lambda l: jnp.cumsum(jax.nn.softmax(l.astype(F32), axis=0), axis=0)[:1]
    lb, lb_vjp = jax.vjp(lb_fn, lb_logits)
    gain_h = hgrn_norm[0].reshape(1, HGRN_WIDTH)

    h, rstd = _rmsnorm_fwd(x2, norm_pre)
    proj = _matmul(h, win_f, "nn", F32, "in_proj")
    ya = _attn_fwd(proj, bias_tab, sinks_b, B_loc, S)
    yh, states = _hgrn_fwd(proj, lb, gain_h, B_loc, S)
    ua = _matmul(ya, wba_f, "nn", F32, "branch_attn_proj")
    uh = _matmul(yh, wbh_f, "nn", F32, "branch_hgrn_proj")
    merged = _merge_fwd(proj, ua, uh)
    yv = _matmul(merged, wout_f, "nn", F32, "out_proj")
    dy, dout, loss_p, gnpost_p = _post_loss(yv, x2, tgt2, norm_post)

    dmerged = _matmul(dy, wout_f, "nt", F32, "out_proj_dgrad")
    g_wout = _matmul(merged, dy, "tn", BF16, "out_proj_wgrad")
    d_ua, d_uh, d_ga, d_gh = _merge_bwd(proj, ua, uh, dmerged)
    g_wba = _matmul(ya, d_ua, "tn", BF16, "branch_attn_wgrad", slabs=N_CHIPS)
    g_wbh = _matmul(yh, d_uh, "tn", BF16, "branch_hgrn_wgrad", slabs=N_CHIPS)
    d_ya = _matmul(d_ua, wba_f, "nt", BF16, "branch_attn_dgrad")
    d_yh = _matmul(d_uh, wbh_f, "nt", BF16, "branch_hgrn_dgrad")
    dq, dk, dv, dag, dbias_p, dsinks_p = _attn_bwd(proj, bias_tab, sinks_b, d_ya, B_loc, S)
    dhq, dhf, dhi, dhg, dlb_p, dgain_p = _hgrn_bwd(proj, states, lb, gain_h, d_yh, B_loc, S)
    dproj = jnp.concatenate([dq, dk, dv, dag, dhq, dhf, dhi, dhg, d_ga, d_gh], axis=1)
    g_win = _matmul(h, dproj, "tn", BF16, "in_proj_wgrad", slabs=N_CHIPS)
    dh = _matmul(dproj, win_f, "nt", F32, "in_proj_dgrad")
    grad_x2, gnpre_p = _rmsnorm_bwd(dh, x2, rstd, norm_pre, dout)

    onehot = (buckets.reshape(-1)[:, None] == jnp.arange(REL_BUCKETS)[None, :]).astype(F32)
    grelb_p = jnp.dot(dbias_p.reshape(ATTN_HEADS, -1), onehot, precision=lax.Precision.HIGHEST).T
    gsinks_p = dsinks_p[:, 0, 0]
    dlb_sum = jnp.sum(dlb_p, axis=0).reshape(1, HGRN_WIDTH)
    ghn_p = jnp.sum(dgain_p, axis=0).reshape(HGRN_HEADS, HGRN_DIM)
    small_parts = [gnpre_p, gnpost_p, grelb_p, gsinks_p, dlb_sum, ghn_p, loss_p]
    small_shapes = [p.shape for p in small_parts]
    n_rows = _round8(sum(-(-math.prod(s) // LANES) for s in small_shapes))
    pack = _pack_rows(small_parts, n_rows)

    pieces = [g_win.reshape(N_CHIPS, 2, -1, g_win.shape[-1]), g_wba.reshape(N_CHIPS, 2, -1, g_wba.shape[-1]),
              g_wbh.reshape(N_CHIPS, 2, -1, g_wbh.shape[-1]), g_wout.reshape(N_CHIPS, 2, -1, g_wout.shape[-1])]
    names = ["w_in", "w_branch_attn", "w_branch_hgrn", "w_out"]
    lands = _exchange_sibling_halves(pieces)
    c_arr = jnp.reshape(my_c, (1,)).astype(jnp.int32)
    sc_arr = jnp.stack([2 * my_x + my_y, my_c]).astype(jnp.int32)
    sums = [_sum_sibling(g, l, c_arr, "sum_sibling_" + nm) for g, l, nm in zip(pieces, lands, names)]
    *recvs, pack_sum = _exchange_chips(sums, pack)
    halves = [_sum_chips(g, l, r, sc_arr, "sum_chips_" + nm) for g, l, r, nm in zip(pieces, lands, recvs, names)]
    shared = _share_with_sibling(halves)
    big_w = [w_in, w_branch_attn, w_branch_hgrn, w_out]
    big_m = [m_w_in, m_w_branch_attn, m_w_branch_hgrn, m_w_out]
    big_v = [v_w_in, v_w_branch_attn, v_w_branch_hgrn, v_w_out]
    big = {}
    for nm, gs, w, m, v in zip(names, shared, big_w, big_m, big_v):
        shp = w.shape
        g2 = gs.reshape(shp[1], shp[2])
        d, nm_, nv_ = _adamw(w[0], g2, m[0], v[0], "adamw_" + nm)
        big[nm] = tuple(a.reshape(shp) for a in (g2, d, nm_, nv_))

    gnpre, gnpost, grelb, gsinks, dlb, ghn, loss = _unpack_rows(pack_sum, small_shapes)
    (g_lb_logits,) = lb_vjp(dlb)
    small_names = ["norm_pre", "rel_bias", "attn_sinks", "lb_logits", "hgrn_norm", "norm_post"]
    small_w = [norm_pre, rel_bias, attn_sinks, lb_logits, hgrn_norm, norm_post]
    small_m = [m_norm_pre, m_rel_bias, m_attn_sinks, m_lb_logits, m_hgrn_norm, m_norm_post]
    small_v = [v_norm_pre, v_rel_bias, v_attn_sinks, v_lb_logits, v_hgrn_norm, v_norm_post]
    small_g = [gnpre.reshape(norm_pre.shape), grelb.reshape(rel_bias.shape), gsinks.reshape(attn_sinks.shape),
               g_lb_logits.reshape(lb_logits.shape), ghn.reshape(hgrn_norm.shape), gnpost.reshape(norm_post.shape)]
    shapes = [w.shape for w in small_w]
    a_rows = _round8(sum(-(-math.prod(s) // LANES) for s in shapes))
    d_s, nm_s, nv_s = _adamw(_pack_rows(small_w, a_rows), _pack_rows(small_g, a_rows), _pack_rows(small_m, a_rows),
                             _pack_rows(small_v, a_rows), "adamw_small")
    small = {}
    for nm, g, d, m_, v_ in zip(small_names, small_g, _unpack_rows(d_s, shapes), _unpack_rows(nm_s, shapes),
                                _unpack_rows(nv_s, shapes)):
        small[nm] = (g, d, m_, v_)

    allw = {**big, **small}
    order = ["norm_pre", "w_in", "rel_bias", "attn_sinks", "lb_logits", "hgrn_norm", "w_branch_attn", "w_branch_hgrn",
             "w_out", "norm_post"]
    outs = [loss.reshape(()), grad_x2.reshape(B_loc, S, D)]
    for k in range(4):
        outs += [allw[nm][k] for nm in order]
    return tuple(outs)
```

```python
import functools
import math

import jax
import jax.numpy as jnp
from jax import lax
from jax.experimental import pallas as pl
from jax.experimental.pallas import tpu as pltpu

F32 = jnp.float32
BF16 = jnp.bfloat16
MESH = pl.DeviceIdType.MESH

ATTN_HEADS = 16
ATTN_KV_HEADS = 4
HEAD_DIM = 64
GROUP = ATTN_HEADS // ATTN_KV_HEADS
WINDOW = 128
ATTN_WIDTH = ATTN_HEADS * HEAD_DIM
KV_WIDTH = ATTN_KV_HEADS * HEAD_DIM
HGRN_HEADS = 8
HGRN_DIM = 128
HGRN_WIDTH = HGRN_HEADS * HGRN_DIM
CHUNK = 64
SUB = 16
NSUB = CHUNK // SUB
REL_BUCKETS = 32
REL_MAX_DIST = 128
NORM_EPS = 1e-6
ADAM_LR = 0.001
ADAM_B1 = 0.9
ADAM_B2 = 0.999
ADAM_EPS = 1e-08
ADAM_WD = 0.01
ADAM_STEP = 10
LANES = 128
N_CHIPS = 4
N_DEV = 8
VMEM_LIMIT = 48 * 1024 * 1024

OFF_AQ = 0
OFF_AK = OFF_AQ + ATTN_WIDTH
OFF_AV = OFF_AK + KV_WIDTH
OFF_AG = OFF_AV + KV_WIDTH
OFF_HQ = OFF_AG + ATTN_WIDTH
OFF_HF = OFF_HQ + HGRN_WIDTH
OFF_HI = OFF_HF + HGRN_WIDTH
OFF_HG = OFF_HI + HGRN_WIDTH
OFF_GA = OFF_HG + HGRN_WIDTH

NT_DIMS = (((1,), (1,)), ((), ()))
TN_DIMS = (((0,), (0,)), ((), ()))
NN_DIMS = (((1,), (0,)), ((), ()))


def _pick(n, cands):
    for c in cands:
        if n % c == 0:
            return c
    raise ValueError(f"no tile for {n} in {cands}")


def _params(sem):
    return pltpu.CompilerParams(dimension_semantics=sem, vmem_limit_bytes=VMEM_LIMIT)


def _bdot(a, b, dims=NN_DIMS):
    return lax.dot_general(a.astype(BF16), b.astype(BF16), dims, preferred_element_type=F32)


def _matmul(a, b, mode, out_dtype, name, slabs=1):
    if mode == "nn":
        (M, K), (K2, N) = a.shape, b.shape
    elif mode == "nt":
        (M, K), (N, K2) = a.shape, b.shape
    else:
        (K, M), (K2, N) = a.shape, b.shape
    assert K == K2
    nslab = N // slabs
    tm = _pick(M, (1024, 512, 256, 128))
    tk = _pick(K, (512, 256, 128))
    tn = _pick(nslab, (1024, 896, 512, 384, 256, 128))
    nk = K // tk
    per = nslab // tn
    dims = {"nn": NN_DIMS, "nt": NT_DIMS, "tn": TN_DIMS}[mode]

    def body(a_ref, b_ref, o_ref, acc_ref):
        k = pl.program_id(2)

        @pl.when(k == 0)
        def _():
            acc_ref[...] = jnp.zeros_like(acc_ref)

        acc_ref[...] += lax.dot_general(a_ref[...], b_ref[...], dims, preferred_element_type=F32)

        @pl.when(k == nk - 1)
        def _():
            o_ref[...] = acc_ref[...].astype(o_ref.dtype)

    if mode == "tn":
        a_spec = pl.BlockSpec((tk, tm), lambda i, j, k: (k, i))
    else:
        a_spec = pl.BlockSpec((tm, tk), lambda i, j, k: (i, k))
    if mode == "nt":
        b_spec = pl.BlockSpec((tn, tk), lambda i, j, k: (j, k))
    else:
        b_spec = pl.BlockSpec((tk, tn), lambda i, j, k: (k, j))
    if slabs == 1:
        o_shape = jax.ShapeDtypeStruct((M, N), out_dtype)
        o_spec = pl.BlockSpec((tm, tn), lambda i, j, k: (i, j))
    else:
        o_shape = jax.ShapeDtypeStruct((slabs, M, nslab), out_dtype)
        o_spec = pl.BlockSpec((None, tm, tn), lambda i, j, k: (j // per, i, j % per))
    return pl.pallas_call(
        body, name=name, grid=(M // tm, N // tn, nk), in_specs=[a_spec, b_spec], out_specs=o_spec, out_shape=o_shape,
        scratch_shapes=[pltpu.VMEM((tm, tn), F32)],
        compiler_params=_params(("parallel", "parallel", "arbitrary")),
    )(a, b)


def _rmsnorm_fwd(x2, gain):
    T, D = x2.shape
    tr = _pick(T, (256, 128))

    def body(x_ref, g_ref, h_ref, r_ref):
        xv = x_ref[...]
        r = lax.rsqrt(jnp.mean(xv * xv, axis=-1, keepdims=True) + NORM_EPS)
        h_ref[...] = (xv * r * g_ref[...]).astype(BF16)
        r_ref[...] = r

    return pl.pallas_call(
        body, name="rmsnorm_pre_fwd", grid=(T // tr,),
        in_specs=[pl.BlockSpec((tr, D), lambda i: (i, 0)), pl.BlockSpec((1, D), lambda i: (0, 0))],
        out_specs=[pl.BlockSpec((tr, D), lambda i: (i, 0)), pl.BlockSpec((tr, 1), lambda i: (i, 0))],
        out_shape=[jax.ShapeDtypeStruct((T, D), BF16), jax.ShapeDtypeStruct((T, 1), F32)],
        compiler_params=_params(("parallel",)),
    )(x2, gain)


def _rmsnorm_bwd(dh, x2, rstd, gain, dout):
    T, D = x2.shape
    tr = _pick(T, (256, 128))

    def body(dh_ref, x_ref, r_ref, g_ref, do_ref, gx_ref, gg_ref):
        @pl.when(pl.program_id(0) == 0)
        def _():
            gg_ref[...] = jnp.zeros_like(gg_ref)

        n = x_ref[...] * r_ref[...]
        dhv = dh_ref[...]
        dn = dhv * g_ref[...]
        gx_ref[...] = do_ref[...] + r_ref[...] * (dn - n * jnp.mean(dn * n, axis=-1, keepdims=True))
        gg_ref[...] += jnp.sum(dhv * n, axis=0, keepdims=True)

    row = pl.BlockSpec((tr, D), lambda i: (i, 0))
    vec = pl.BlockSpec((1, D), lambda i: (0, 0))
    return pl.pallas_call(
        body, name="rmsnorm_pre_bwd", grid=(T // tr,),
        in_specs=[row, row, pl.BlockSpec((tr, 1), lambda i: (i, 0)), vec, row],
        out_specs=[row, vec],
        out_shape=[jax.ShapeDtypeStruct((T, D), F32), jax.ShapeDtypeStruct((1, D), F32)],
        compiler_params=_params(("arbitrary",)),
    )(dh, x2, rstd, gain, dout)


def _post_loss(yv, x2, tgt2, gain):
    T, D = x2.shape
    tr = _pick(T, (256, 128))

    def body(y_ref, x_ref, t_ref, g_ref, dy_ref, do_ref, loss_ref, gg_ref):
        @pl.when(pl.program_id(0) == 0)
        def _():
            gg_ref[...] = jnp.zeros_like(gg_ref)
            loss_ref[...] = jnp.zeros_like(loss_ref)

        yv_ = y_ref[...]
        r = lax.rsqrt(jnp.mean(yv_ * yv_, axis=-1, keepdims=True) + NORM_EPS)
        n = yv_ * r
        e = (x_ref[...] + n * g_ref[...]) - t_ref[...]
        loss_ref[...] += 0.5 * jnp.sum(jnp.mean(e * e, axis=-1, keepdims=True), axis=0, keepdims=True)
        dz = e / D
        do_ref[...] = dz
        gg_ref[...] += jnp.sum(dz * n, axis=0, keepdims=True)
        dn = dz * g_ref[...]
        dy_ref[...] = (r * (dn - n * jnp.mean(dn * n, axis=-1, keepdims=True))).astype(BF16)

    row = pl.BlockSpec((tr, D), lambda i: (i, 0))
    vec = pl.BlockSpec((1, D), lambda i: (0, 0))
    return pl.pallas_call(
        body, name="post_norm_loss", grid=(T // tr,),
        in_specs=[row, row, row, vec],
        out_specs=[row, row, pl.BlockSpec((1, 1), lambda i: (0, 0)), vec],
        out_shape=[jax.ShapeDtypeStruct((T, D), BF16), jax.ShapeDtypeStruct((T, D), F32),
                   jax.ShapeDtypeStruct((1, 1), F32), jax.ShapeDtypeStruct((1, D), F32)],
        compiler_params=_params(("arbitrary",)),
    )(yv, x2, tgt2, gain)


def _merge_fwd(proj, ua, uh):
    T, D = ua.shape
    tr = _pick(T, (256, 128))
    bw = _pick(D, (512, 256))
    assert OFF_GA % bw == 0
    oa, oh = OFF_GA // bw, (OFF_GA + D) // bw

    def body(ga_ref, gh_ref, ua_ref, uh_ref, m_ref):
        m_ref[...] = (jax.nn.sigmoid(ga_ref[...]) * ua_ref[...] + jax.nn.sigmoid(gh_ref[...]) * uh_ref[...]).astype(BF16)

    blk = pl.BlockSpec((tr, bw), lambda i, j: (i, j))
    return pl.pallas_call(
        body, name="merge_fwd", grid=(T // tr, D // bw),
        in_specs=[pl.BlockSpec((tr, bw), lambda i, j: (i, oa + j)), pl.BlockSpec((tr, bw), lambda i, j: (i, oh + j)), blk, blk],
        out_specs=blk, out_shape=jax.ShapeDtypeStruct((T, D), BF16),
        compiler_params=_params(("parallel", "parallel")),
    )(proj, proj, ua, uh)


def _merge_bwd(proj, ua, uh, dm):
    T, D = ua.shape
    tr = _pick(T, (256, 128))
    bw = _pick(D, (512, 256))
    oa, oh = OFF_GA // bw, (OFF_GA + D) // bw

    def body(ga_ref, gh_ref, ua_ref, uh_ref, dm_ref, dua_ref, duh_ref, dga_ref, dgh_ref):
        sa = jax.nn.sigmoid(ga_ref[...])
        sh = jax.nn.sigmoid(gh_ref[...])
        d = dm_ref[...]
        dua_ref[...] = (d * sa).astype(BF16)
        duh_ref[...] = (d * sh).astype(BF16)
        dga_ref[...] = (d * ua_ref[...] * sa * (1.0 - sa)).astype(BF16)
        dgh_ref[...] = (d * uh_ref[...] * sh * (1.0 - sh)).astype(BF16)

    blk = pl.BlockSpec((tr, bw), lambda i, j: (i, j))
    o = jax.ShapeDtypeStruct((T, D), BF16)
    return pl.pallas_call(
        body, name="merge_bwd", grid=(T // tr, D // bw),
        in_specs=[pl.BlockSpec((tr, bw), lambda i, j: (i, oa + j)), pl.BlockSpec((tr, bw), lambda i, j: (i, oh + j)), blk, blk, blk],
        out_specs=[blk, blk, blk, blk], out_shape=[o, o, o, o],
        compiler_params=_params(("parallel", "parallel")),
    )(proj, proj, ua, uh, dm)


KV_PAIR = 2
PAIR_HEADS = KV_PAIR * GROUP


def _attn_mask(n):
    qi = lax.broadcasted_iota(jnp.int32, (WINDOW, 2 * WINDOW), 0)
    si = lax.broadcasted_iota(jnp.int32, (WINDOW, 2 * WINDOW), 1)
    dist = qi + WINDOW - si
    return (dist >= 0) & (dist < WINDOW) & ((si >= WINDOW) | (n > 0))


def _attn_group_fn(mask):
    def f(q4, k, v, ag4, bias4, sink4):
        kb = k.astype(BF16)
        vb = v.astype(BF16)
        outs = []
        for g in range(GROUP):
            s = lax.dot_general(q4[g].astype(BF16), kb, NT_DIMS, preferred_element_type=F32) * (HEAD_DIM ** -0.5)
            s = jnp.where(mask, s + bias4[g], -1e30)
            m = lax.stop_gradient(jnp.maximum(jnp.max(s, axis=-1, keepdims=True), sink4[g]))
            p = jnp.exp(s - m)
            den = jnp.sum(p, axis=-1, keepdims=True) + jnp.exp(sink4[g] - m)
            o = jnp.dot((p / den).astype(BF16), vb, preferred_element_type=F32)
            outs.append(o * jax.nn.silu(ag4[g]))
        return outs

    return f


def _attn_specs(B_loc, nb, order):
    qw = PAIR_HEADS * HEAD_DIM
    kw = KV_PAIR * HEAD_DIM

    def rows(g):
        b, p, n = order(*g)
        return b * nb + n

    def prev(g):
        b, p, n = order(*g)
        return b * nb + jnp.maximum(n - 1, 0)

    def pp(g):
        return order(*g)[1]

    q = pl.BlockSpec((WINDOW, qw), lambda *g: (rows(g), OFF_AQ // qw + pp(g)))
    kc = pl.BlockSpec((WINDOW, kw), lambda *g: (rows(g), OFF_AK // kw + pp(g)))
    kp = pl.BlockSpec((WINDOW, kw), lambda *g: (prev(g), OFF_AK // kw + pp(g)))
    vc = pl.BlockSpec((WINDOW, kw), lambda *g: (rows(g), OFF_AV // kw + pp(g)))
    vp = pl.BlockSpec((WINDOW, kw), lambda *g: (prev(g), OFF_AV // kw + pp(g)))
    ag = pl.BlockSpec((WINDOW, qw), lambda *g: (rows(g), OFF_AG // qw + pp(g)))
    bias = pl.BlockSpec((PAIR_HEADS, WINDOW, 2 * WINDOW), lambda *g: (pp(g), 0, 0))
    sink = pl.BlockSpec((PAIR_HEADS, 8, LANES), lambda *g: (pp(g), 0, 0))
    return [q, kc, kp, vc, vp, ag, bias, sink], rows, pp


def _attn_operands(q_ref, kc_ref, kp_ref, vc_ref, vp_ref, ag_ref, bias_ref, sink_ref, j):
    lo, hi = j * HEAD_DIM, (j + 1) * HEAD_DIM
    k = jnp.concatenate([kp_ref[:, lo:hi], kc_ref[:, lo:hi]], axis=0)
    v = jnp.concatenate([vp_ref[:, lo:hi], vc_ref[:, lo:hi]], axis=0)
    q4 = [q_ref[:, (j * GROUP + g) * HEAD_DIM:(j * GROUP + g + 1) * HEAD_DIM] for g in range(GROUP)]
    ag4 = [ag_ref[:, (j * GROUP + g) * HEAD_DIM:(j * GROUP + g + 1) * HEAD_DIM] for g in range(GROUP)]
    bias4 = [bias_ref[j * GROUP + g] for g in range(GROUP)]
    sink4 = [sink_ref[j * GROUP + g, 0:1, 0:1] for g in range(GROUP)]
    return q4, k, v, ag4, bias4, sink4


def _attn_fwd(proj, bias_tab, sinks_b, B_loc, S):
    T = B_loc * S
    nb = S // WINDOW
    n_pairs = ATTN_KV_HEADS // KV_PAIR
    in_specs, rows, pp = _attn_specs(B_loc, nb, lambda b, p, n: (b, p, n))

    def body(q_ref, kc_ref, kp_ref, vc_ref, vp_ref, ag_ref, bias_ref, sink_ref, ya_ref):
        f = _attn_group_fn(_attn_mask(pl.program_id(2)))
        for j in range(KV_PAIR):
            outs = f(*_attn_operands(q_ref, kc_ref, kp_ref, vc_ref, vp_ref, ag_ref, bias_ref, sink_ref, j))
            for g in range(GROUP):
                h = j * GROUP + g
                ya_ref[:, h * HEAD_DIM:(h + 1) * HEAD_DIM] = outs[g].astype(BF16)

    return pl.pallas_call(
        body, name="attn_fwd", grid=(B_loc, n_pairs, nb), in_specs=in_specs,
        out_specs=pl.BlockSpec((WINDOW, PAIR_HEADS * HEAD_DIM), lambda *g: (rows(g), pp(g))),
        out_shape=jax.ShapeDtypeStruct((T, ATTN_WIDTH), BF16),
        compiler_params=_params(("parallel", "parallel", "parallel")),
    )(proj, proj, proj, proj, proj, proj, bias_tab, sinks_b)


def _attn_bwd(proj, bias_tab, sinks_b, dya, B_loc, S):
    T = B_loc * S
    nb = S // WINDOW
    n_pairs = ATTN_KV_HEADS // KV_PAIR
    order = lambda p, b, i: (b, p, nb - 1 - i)
    in_specs, rows, pp = _attn_specs(B_loc, nb, order)
    qw = PAIR_HEADS * HEAD_DIM
    kw = KV_PAIR * HEAD_DIM

    def body(q_ref, kc_ref, kp_ref, vc_ref, vp_ref, ag_ref, bias_ref, sink_ref, dya_ref,
             dq_ref, dk_ref, dv_ref, dag_ref, dbias_ref, dsink_ref, dkc_ref, dvc_ref):
        b, i = pl.program_id(1), pl.program_id(2)
        n = nb - 1 - i

        @pl.when((b == 0) & (i == 0))
        def _():
            dbias_ref[...] = jnp.zeros_like(dbias_ref)
            dsink_ref[...] = jnp.zeros_like(dsink_ref)

        @pl.when(i == 0)
        def _():
            dkc_ref[...] = jnp.zeros_like(dkc_ref)
            dvc_ref[...] = jnp.zeros_like(dvc_ref)

        f = _attn_group_fn(_attn_mask(n))
        for j in range(KV_PAIR):
            ops = _attn_operands(q_ref, kc_ref, kp_ref, vc_ref, vp_ref, ag_ref, bias_ref, sink_ref, j)
            _, vjp = jax.vjp(f, *ops)
            douts = [dya_ref[:, (j * GROUP + g) * HEAD_DIM:(j * GROUP + g + 1) * HEAD_DIM].astype(F32) for g in range(GROUP)]
            dq4, dk, dv, dag4, dbias4, dsink4 = vjp(douts)
            lo, hi = j * HEAD_DIM, (j + 1) * HEAD_DIM
            dk_ref[:, lo:hi] = (dk[WINDOW:] + dkc_ref[:, lo:hi]).astype(BF16)
            dv_ref[:, lo:hi] = (dv[WINDOW:] + dvc_ref[:, lo:hi]).astype(BF16)
            dkc_ref[:, lo:hi] = dk[:WINDOW]
            dvc_ref[:, lo:hi] = dv[:WINDOW]
            for g in range(GROUP):
                h = j * GROUP + g
                dq_ref[:, h * HEAD_DIM:(h + 1) * HEAD_DIM] = dq4[g].astype(BF16)
                dag_ref[:, h * HEAD_DIM:(h + 1) * HEAD_DIM] = dag4[g].astype(BF16)
                dbias_ref[h] += dbias4[g]
                dsink_ref[h] += jnp.broadcast_to(dsink4[g], (8, LANES))

    wide = pl.BlockSpec((WINDOW, qw), lambda *g: (rows(g), pp(g)))
    narrow = pl.BlockSpec((WINDOW, kw), lambda *g: (rows(g), pp(g)))
    return pl.pallas_call(
        body, name="attn_bwd", grid=(n_pairs, B_loc, nb), in_specs=in_specs + [wide],
        out_specs=[wide, narrow, narrow, wide,
                   pl.BlockSpec((PAIR_HEADS, WINDOW, 2 * WINDOW), lambda *g: (pp(g), 0, 0)),
                   pl.BlockSpec((PAIR_HEADS, 8, LANES), lambda *g: (pp(g), 0, 0))],
        out_shape=[jax.ShapeDtypeStruct((T, ATTN_WIDTH), BF16), jax.ShapeDtypeStruct((T, KV_WIDTH), BF16),
                   jax.ShapeDtypeStruct((T, KV_WIDTH), BF16), jax.ShapeDtypeStruct((T, ATTN_WIDTH), BF16),
                   jax.ShapeDtypeStruct((ATTN_HEADS, WINDOW, 2 * WINDOW), F32),
                   jax.ShapeDtypeStruct((ATTN_HEADS, 8, LANES), F32)],
        scratch_shapes=[pltpu.VMEM((WINDOW, kw), F32), pltpu.VMEM((WINDOW, kw), F32)],
        compiler_params=_params(("arbitrary", "arbitrary", "arbitrary")),
    )(proj, proj, proj, proj, proj, proj, bias_tab, sinks_b, dya)


def _hgrn_chunk_fn(state_t, qs, fs, vs, hgs, lb, gain):
    r = lax.broadcasted_iota(jnp.int32, (SUB, SUB), 0)
    c = lax.broadcasted_iota(jnp.int32, (SUB, SUB), 1)
    tri = (r >= c).astype(F32)
    ref = jnp.zeros((1, HGRN_DIM), F32)
    qsil, kk, gg, refs = [], [], [], []
    for i in range(NSUB):
        f = lb + (1.0 - lb) * jax.nn.sigmoid(fs[i])
        logf = jnp.log(f)
        g = jnp.dot(tri, logf, precision=lax.Precision.HIGHEST, preferred_element_type=F32) + ref
        refs.append(ref)
        gg.append(g)
        kk.append(1.0 - f)
        qsil.append(jax.nn.silu(qs[i]))
        ref = ref + jnp.sum(logf, axis=0, keepdims=True)
    g_end = ref
    ys = []
    for i in range(NSUB):
        inter = _bdot(qsil[i] * jnp.exp(gg[i]), state_t, NT_DIMS)
        qd = qsil[i] * jnp.exp(gg[i] - refs[i])
        kcat = jnp.concatenate([kk[j] * jnp.exp(refs[i] - gg[j]) for j in range(i + 1)], axis=0)
        vcat = jnp.concatenate(vs[:i + 1], axis=0)
        a = _bdot(qd, kcat, NT_DIMS)
        rr = lax.broadcasted_iota(jnp.int32, a.shape, 0)
        cc = lax.broadcasted_iota(jnp.int32, a.shape, 1)
        a = jnp.where(cc - i * SUB <= rr, a, 0.0)
        o = inter + _bdot(a, vcat)
        rs = lax.rsqrt(jnp.mean(o * o, axis=-1, keepdims=True) + NORM_EPS)
        ys.append(o * rs * gain * jax.nn.silu(hgs[i]))
    kdec = jnp.concatenate([kk[j] * jnp.exp(g_end - gg[j]) for j in range(NSUB)], axis=0)
    vall = jnp.concatenate(vs, axis=0)
    new_state_t = state_t * jnp.exp(g_end) + _bdot(vall, kdec, TN_DIMS)
    return new_state_t, ys


def _head_blocks(ref, hd):
    lanes = slice(hd * HGRN_DIM, (hd + 1) * HGRN_DIM)
    return [ref[i * SUB:(i + 1) * SUB, lanes] for i in range(NSUB)]


HEADS_PER_STEP = 4
GROUP_WIDTH = HEADS_PER_STEP * HGRN_DIM
N_HEAD_GROUPS = HGRN_HEADS // HEADS_PER_STEP


def _hgrn_specs(nc, order):
    def rows(g):
        b, grp, n = order(*g)
        return b * nc + n

    def grp_of(g):
        return order(*g)[1]

    def col(off):
        assert off % GROUP_WIDTH == 0
        return pl.BlockSpec((CHUNK, GROUP_WIDTH), lambda *g: (rows(g), off // GROUP_WIDTH + grp_of(g)))

    vec = pl.BlockSpec((1, GROUP_WIDTH), lambda *g: (0, grp_of(g)))
    return [col(OFF_HQ), col(OFF_HF), col(OFF_HI), col(OFF_HG), vec, vec], rows, grp_of


def _hgrn_fwd(proj, lb, gain, B_loc, S):
    T = B_loc * S
    nc = S // CHUNK
    in_specs, rows, grp_of = _hgrn_specs(nc, lambda b, grp, n: (b, grp, n))

    def body(q_ref, f_ref, v_ref, hg_ref, lb_ref, gain_ref, yh_ref, st_ref, state_scr):
        @pl.when(pl.program_id(2) == 0)
        def _():
            state_scr[...] = jnp.zeros_like(state_scr)

        for hd in range(HEADS_PER_STEP):
            lanes = slice(hd * HGRN_DIM, (hd + 1) * HGRN_DIM)
            st = state_scr[hd]
            st_ref[hd] = st
            new, ys = _hgrn_chunk_fn(st, _head_blocks(q_ref, hd), _head_blocks(f_ref, hd), _head_blocks(v_ref, hd),
                                     _head_blocks(hg_ref, hd), lb_ref[:, lanes], gain_ref[:, lanes])
            state_scr[hd] = new
            for i in range(NSUB):
                yh_ref[i * SUB:(i + 1) * SUB, lanes] = ys[i].astype(BF16)

    return pl.pallas_call(
        body, name="hgrn_fwd", grid=(B_loc, N_HEAD_GROUPS, nc), in_specs=in_specs,
        out_specs=[pl.BlockSpec((CHUNK, GROUP_WIDTH), lambda *g: (rows(g), grp_of(g))),
                   pl.BlockSpec((None, None, HEADS_PER_STEP, HGRN_DIM, HGRN_DIM), lambda b, grp, n: (b, n, grp, 0, 0))],
        out_shape=[jax.ShapeDtypeStruct((T, HGRN_WIDTH), BF16),
                   jax.ShapeDtypeStruct((B_loc, nc, HGRN_HEADS, HGRN_DIM, HGRN_DIM), F32)],
        scratch_shapes=[pltpu.VMEM((HEADS_PER_STEP, HGRN_DIM, HGRN_DIM), F32)],
        compiler_params=_params(("parallel", "parallel", "arbitrary")),
    )(proj, proj, proj, proj, lb, gain)


def _hgrn_bwd(proj, states, lb, gain, dyh, B_loc, S):
    T = B_loc * S
    nc = S // CHUNK
    in_specs, rows, grp_of = _hgrn_specs(nc, lambda b, grp, i: (b, grp, nc - 1 - i))
    blk = pl.BlockSpec((CHUNK, GROUP_WIDTH), lambda *g: (rows(g), grp_of(g)))
    st_spec = pl.BlockSpec((None, None, HEADS_PER_STEP, HGRN_DIM, HGRN_DIM), lambda b, grp, i: (b, nc - 1 - i, grp, 0, 0))
    acc_spec = pl.BlockSpec((None, 1, GROUP_WIDTH), lambda b, grp, i: (b, 0, grp))

    def body(q_ref, f_ref, v_ref, hg_ref, lb_ref, gain_ref, st_ref, dyh_ref,
             dq_ref, df_ref, dv_ref, dhg_ref, dlb_ref, dgain_ref, dstate_scr):
        @pl.when(pl.program_id(2) == 0)
        def _():
            dstate_scr[...] = jnp.zeros_like(dstate_scr)
            dlb_ref[...] = jnp.zeros_like(dlb_ref)
            dgain_ref[...] = jnp.zeros_like(dgain_ref)

        for hd in range(HEADS_PER_STEP):
            lanes = slice(hd * HGRN_DIM, (hd + 1) * HGRN_DIM)
            _, vjp = jax.vjp(_hgrn_chunk_fn, st_ref[hd], _head_blocks(q_ref, hd), _head_blocks(f_ref, hd),
                             _head_blocks(v_ref, hd), _head_blocks(hg_ref, hd), lb_ref[:, lanes], gain_ref[:, lanes])
            dys = [dyh_ref[i * SUB:(i + 1) * SUB, lanes].astype(F32) for i in range(NSUB)]
            dst, dqs, dfs, dvs, dhgs, dlb, dgain = vjp((dstate_scr[hd], dys))
            dstate_scr[hd] = dst
            dlb_ref[:, lanes] += dlb
            dgain_ref[:, lanes] += dgain
            for i in range(NSUB):
                sl = slice(i * SUB, (i + 1) * SUB)
                dq_ref[sl, lanes] = dqs[i].astype(BF16)
                df_ref[sl, lanes] = dfs[i].astype(BF16)
                dv_ref[sl, lanes] = dvs[i].astype(BF16)
                dhg_ref[sl, lanes] = dhgs[i].astype(BF16)

    o = jax.ShapeDtypeStruct((T, HGRN_WIDTH), BF16)
    acc = jax.ShapeDtypeStruct((B_loc, 1, HGRN_WIDTH), F32)
    return pl.pallas_call(
        body, name="hgrn_bwd", grid=(B_loc, N_HEAD_GROUPS, nc), in_specs=in_specs + [st_spec, blk],
        out_specs=[blk, blk, blk, blk, acc_spec, acc_spec], out_shape=[o, o, o, o, acc, acc],
        scratch_shapes=[pltpu.VMEM((HEADS_PER_STEP, HGRN_DIM, HGRN_DIM), F32)],
        compiler_params=_params(("parallel", "parallel", "arbitrary")),
    )(proj, proj, proj, proj, lb, gain, states, dyh)


def _adamw(w, g, m, v, name):
    R, C = w.shape
    tr = _pick(R, (128, 64, 32, 16, 8)) if C > 1024 else _pick(R, (512, 256, 128, 64, 32, 16, 8))

    def body(w_ref, g_ref, m_ref, v_ref, d_ref, nm_ref, nv_ref):
        gv = g_ref[...]
        nm = ADAM_B1 * m_ref[...] + (1.0 - ADAM_B1) * gv
        nv = ADAM_B2 * v_ref[...] + (1.0 - ADAM_B2) * (gv * gv)
        m_hat = nm / (1.0 - ADAM_B1 ** ADAM_STEP)
        v_hat = nv / (1.0 - ADAM_B2 ** ADAM_STEP)
        d_ref[...] = -ADAM_LR * (m_hat / (jnp.sqrt(v_hat) + ADAM_EPS) + ADAM_WD * w_ref[...])
        nm_ref[...] = nm
        nv_ref[...] = nv

    blk = pl.BlockSpec((tr, C), lambda i: (i, 0))
    o = jax.ShapeDtypeStruct((R, C), F32)
    return pl.pallas_call(
        body, name=name, grid=(R // tr,), in_specs=[blk] * 4, out_specs=[blk] * 3, out_shape=[o, o, o],
        compiler_params=_params(("parallel",)),
    )(w, g, m, v)


ANY = pl.BlockSpec(memory_space=pl.ANY)
VMEM_SPEC = pl.BlockSpec(memory_space=pltpu.VMEM)


def _place():
    x, y, c = lax.axis_index("x"), lax.axis_index("y"), lax.axis_index("c")
    other_chips = [(1 - x, y), (x, 1 - y), (1 - x, 1 - y)]
    return x, y, c, other_chips


def _cast_into_full(w, ax, s_arr, name):
    R, C = w.shape
    tr = _pick(R, (256, 128))
    nr = R // tr

    def body(s_ref, w_ref, o_ref):
        o_ref[...] = w_ref[...].astype(BF16)

    if ax == 1:
        shape, o_map = (R, N_CHIPS * C), lambda i, s: (i, s[0])
    else:
        shape, o_map = (N_CHIPS * R, C), lambda i, s: (s[0] * nr + i, 0)
    return pl.pallas_call(
        body, name=name,
        grid_spec=pltpu.PrefetchScalarGridSpec(
            num_scalar_prefetch=1, grid=(nr,), in_specs=[pl.BlockSpec((tr, C), lambda i, s: (i, 0))],
            out_specs=pl.BlockSpec((tr, C), o_map)),
        out_shape=jax.ShapeDtypeStruct(shape, BF16),
        compiler_params=_params(("parallel",)),
    )(s_arr, w)


def _gather_weights(fulls, shard_shapes, axes):
    nw = len(fulls)

    def body(*refs):
        ins, outs = refs[:nw], refs[nw:2 * nw]
        send_sems, recv_sems = refs[2 * nw:]
        x, y, c, chips = _place()
        me, sibling, s = (x, y, c), (x, y, 1 - c), 2 * x + y

        def region(ref, i, t, half):
            R, C = shard_shapes[i]
            hr = R // 2
            if axes[i] == 1:
                return ref.at[pl.ds(half * hr, hr), pl.ds(pl.multiple_of(t * C, LANES), C)]
            return ref.at[pl.ds(t * R + half * hr, hr), :]

        def rcopy(i, k, src, dst, to):
            return pltpu.make_async_remote_copy(src_ref=src, dst_ref=dst, send_sem=send_sems.at[i * 6 + k],
                                                recv_sem=recv_sems.at[i * 6 + k], device_id=to, device_id_type=MESH)

        sent = []
        for i in range(nw):
            for j, chip in enumerate(chips):
                cp = rcopy(i, j, region(ins[i], i, s, c), region(outs[i], i, s, c), (*chip, c))
                cp.start()
                sent.append(cp)
        for i in range(nw):
            for j, chip in enumerate(chips):
                reg = region(outs[i], i, 2 * chip[0] + chip[1], c)
                rcopy(i, j, reg, reg, me).wait_recv()
                cp = rcopy(i, 3 + j, reg, reg, sibling)
                cp.start()
                sent.append(cp)
        for i in range(nw):
            for j, chip in enumerate(chips):
                reg = region(outs[i], i, 2 * chip[0] + chip[1], 1 - c)
                rcopy(i, 3 + j, reg, reg, me).wait_recv()
        for cp in sent:
            cp.wait_send()

    return pl.pallas_call(
        body, name="gather_weights", in_specs=[ANY] * nw, out_specs=[ANY] * nw,
        out_shape=[jax.ShapeDtypeStruct(f.shape, f.dtype) for f in fulls],
        input_output_aliases={i: i for i in range(nw)},
        scratch_shapes=[pltpu.SemaphoreType.DMA((6 * nw,)), pltpu.SemaphoreType.DMA((6 * nw,))],
    )(*fulls)


def _exchange_sibling_halves(gs):
    nw = len(gs)

    def body(*refs):
        ins, outs = refs[:nw], refs[nw:2 * nw]
        send_sems, recv_sems = refs[2 * nw:]
        x, y, c, _ = _place()
        cps = []
        for i in range(nw):
            cp = pltpu.make_async_remote_copy(src_ref=ins[i].at[:, 1 - c], dst_ref=outs[i], send_sem=send_sems.at[i],
                                              recv_sem=recv_sems.at[i], device_id=(x, y, 1 - c), device_id_type=MESH)
            cp.start()
            cps.append(cp)
        for cp in cps:
            cp.wait()

    return pl.pallas_call(
        body, name="grads_to_sibling", in_specs=[ANY] * nw, out_specs=[ANY] * nw,
        out_shape=[jax.ShapeDtypeStruct((g.shape[0],) + g.shape[2:], g.dtype) for g in gs],
        scratch_shapes=[pltpu.SemaphoreType.DMA((nw,)), pltpu.SemaphoreType.DMA((nw,))],
    )(*gs)


def _exchange_chips(ss, pack):
    nw = len(ss)
    rows = pack.shape[0]

    def body(*refs):
        ins, pack_ref = refs[:nw], refs[nw]
        outs, sum_ref = refs[nw + 1:2 * nw + 1], refs[2 * nw + 1]
        all_ref, send_sems, recv_sems = refs[2 * nw + 2:]
        x, y, c, chips = _place()
        me = 4 * x + 2 * y + c
        cps = []
        for i in range(nw):
            for j, chip in enumerate(chips):
                cp = pltpu.make_async_remote_copy(
                    src_ref=ins[i].at[2 * chip[0] + chip[1]], dst_ref=outs[i].at[j], send_sem=send_sems.at[i * 3 + j],
                    recv_sem=recv_sems.at[i * 3 + j], device_id=(*chip, c), device_id_type=MESH)
                cp.start()
                cps.append(cp)
        all_ref[me] = pack_ref[...]
        for k in range(1, N_DEV):
            to = (1 - x if k & 4 else x, 1 - y if k & 2 else y, 1 - c if k & 1 else c)
            cp = pltpu.make_async_remote_copy(
                src_ref=pack_ref, dst_ref=all_ref.at[me], send_sem=send_sems.at[nw * 3 + k - 1],
                recv_sem=recv_sems.at[nw * 3 + k - 1], device_id=to, device_id_type=MESH)
            cp.start()
            cps.append(cp)
        for cp in cps:
            cp.wait()
        total = all_ref[0]
        for d in range(1, N_DEV):
            total = total + all_ref[d]
        sum_ref[...] = total

    return pl.pallas_call(
        body, name="grads_to_chips", in_specs=[ANY] * nw + [VMEM_SPEC], out_specs=[ANY] * nw + [VMEM_SPEC],
        out_shape=[jax.ShapeDtypeStruct((N_CHIPS - 1,) + s.shape[1:], s.dtype) for s in ss]
        + [jax.ShapeDtypeStruct(pack.shape, F32)],
        scratch_shapes=[pltpu.VMEM((N_DEV, rows, LANES), F32), pltpu.SemaphoreType.DMA((nw * 3 + N_DEV - 1,)),
                        pltpu.SemaphoreType.DMA((nw * 3 + N_DEV - 1,))],
    )(*ss, pack)


def _share_with_sibling(fs):
    nw = len(fs)

    def body(*refs):
        ins, outs = refs[:nw], refs[nw:2 * nw]
        send_sems, recv_sems = refs[2 * nw:]
        x, y, c, _ = _place()
        cps = []
        for i in range(nw):
            cp = pltpu.make_async_remote_copy(src_ref=ins[i].at[c], dst_ref=outs[i].at[c], send_sem=send_sems.at[i],
                                              recv_sem=recv_sems.at[i], device_id=(x, y, 1 - c), device_id_type=MESH)
            cp.start()
            cps.append(cp)
        for cp in cps:
            cp.wait()

    return pl.pallas_call(
        body, name="grads_share_sibling", in_specs=[ANY] * nw, out_specs=[ANY] * nw,
        out_shape=[jax.ShapeDtypeStruct(f.shape, f.dtype) for f in fs],
        input_output_aliases={i: i for i in range(nw)},
        scratch_shapes=[pltpu.SemaphoreType.DMA((nw,)), pltpu.SemaphoreType.DMA((nw,))],
    )(*fs)


def _sum_sibling(g, land, c_arr, name):
    _, _, P, Q = g.shape
    tp = _pick(P, (256, 128, 64))

    def body(c_ref, g_ref, l_ref, s_ref):
        s_ref[...] = (g_ref[...].astype(F32) + l_ref[...].astype(F32)).astype(BF16)

    return pl.pallas_call(
        body, name=name,
        grid_spec=pltpu.PrefetchScalarGridSpec(
            num_scalar_prefetch=1, grid=(N_CHIPS, P // tp),
            in_specs=[pl.BlockSpec((None, None, tp, Q), lambda t, r, cr: (t, cr[0], r, 0)),
                      pl.BlockSpec((None, tp, Q), lambda t, r, cr: (t, r, 0))],
            out_specs=pl.BlockSpec((None, tp, Q), lambda t, r, cr: (t, r, 0))),
        out_shape=jax.ShapeDtypeStruct((N_CHIPS, P, Q), BF16),
        compiler_params=_params(("parallel", "parallel")),
    )(c_arr, g, land)


def _sum_chips(g, land, recv, sc_arr, name):
    _, _, P, Q = g.shape
    tp = _pick(P, (256, 128, 64))

    def body(sc_ref, g_ref, l_ref, r_ref, f_ref):
        acc = g_ref[...].astype(F32) + l_ref[...].astype(F32)
        for j in range(N_CHIPS - 1):
            acc = acc + r_ref[j].astype(F32)
        f_ref[...] = acc

    return pl.pallas_call(
        body, name=name,
        grid_spec=pltpu.PrefetchScalarGridSpec(
            num_scalar_prefetch=1, grid=(P // tp,),
            in_specs=[pl.BlockSpec((None, None, tp, Q), lambda r, sc: (sc[0], sc[1], r, 0)),
                      pl.BlockSpec((None, tp, Q), lambda r, sc: (sc[0], r, 0)),
                      pl.BlockSpec((N_CHIPS - 1, tp, Q), lambda r, sc: (0, r, 0))],
            out_specs=pl.BlockSpec((None, tp, Q), lambda r, sc: (sc[1], r, 0))),
        out_shape=jax.ShapeDtypeStruct((2, P, Q), F32),
        compiler_params=_params(("parallel",)),
    )(sc_arr, g, land, recv)


def _t5_bucket(dist):
    max_exact = REL_BUCKETS // 2
    d = jnp.maximum(dist, 0)
    df = jnp.maximum(d, 1).astype(F32)
    large = max_exact + (jnp.log(df / max_exact) / math.log(REL_MAX_DIST / max_exact)
                         * (REL_BUCKETS - max_exact)).astype(jnp.int32)
    large = jnp.minimum(large, REL_BUCKETS - 1)
    return jnp.where(d < max_exact, d, large)


def _bucket_table():
    qi = jnp.arange(WINDOW)[:, None]
    si = jnp.arange(2 * WINDOW)[None, :]
    return _t5_bucket(qi + WINDOW - si)


def _rows_of(a):
    flat = a.reshape(-1).astype(F32)
    n = -(-flat.shape[0] // LANES) * LANES
    return jnp.pad(flat, (0, n - flat.shape[0])).reshape(-1, LANES)


def _pack_rows(parts, total_rows):
    rows = [_rows_of(p) for p in parts]
    used = sum(r.shape[0] for r in rows)
    if total_rows > used:
        rows.append(jnp.zeros((total_rows - used, LANES), F32))
    return jnp.concatenate(rows, axis=0)


def _unpack_rows(packed, shapes):
    out, at = [], 0
    for shp in shapes:
        n = math.prod(shp)
        nr = -(-n // LANES)
        out.append(packed[at:at + nr].reshape(-1)[:n].reshape(shp))
        at += nr
    return out


def _round8(n):
    return -(-n // 8) * 8


def kernel(x, norm_pre, w_in, rel_bias, attn_sinks, lb_logits, hgrn_norm, w_branch_attn, w_branch_hgrn, w_out, norm_post, loss_target, m_norm_pre, m_w_in, m_rel_bias, m_attn_sinks, m_lb_logits, m_hgrn_norm, m_w_branch_attn, m_w_branch_hgrn, m_w_out, m_norm_post, v_norm_pre, v_w_in, v_rel_bias, v_attn_sinks, v_lb_logits, v_hgrn_norm, v_w_branch_attn, v_w_branch_hgrn, v_w_out, v_norm_post):
    B_loc, S, D = x.shape
    T = B_loc * S
    x2 = x.reshape(T, D)
    tgt2 = loss_target.reshape(T, D)
    my_x, my_y, my_c = lax.axis_index("x"), lax.axis_index("y"), lax.axis_index("c")

    c_arr = jnp.reshape(my_c, (1,)).astype(jnp.int32)
    s_arr = jnp.reshape(2 * my_x + my_y, (1,)).astype(jnp.int32)
    sc_arr = jnp.concatenate([s_arr, c_arr])
    shard_ws = [w_in[0], w_branch_attn[0], w_branch_hgrn[0], w_out[0]]
    shard_axes = (1, 1, 1, 0)
    names = ["w_in", "w_branch_attn", "w_branch_hgrn", "w_out"]
    placed = [_cast_into_full(w, ax, s_arr, "cast_" + nm) for w, ax, nm in zip(shard_ws, shard_axes, names)]
    win_f, wba_f, wbh_f, wout_f = _gather_weights(placed, [w.shape for w in shard_ws], shard_axes)

    buckets = _bucket_table()
    bias_tab = jnp.transpose(rel_bias[buckets].astype(F32), (2, 0, 1))
    sinks_b = jnp.broadcast_to(attn_sinks[0].astype(F32)[:, None, None], (ATTN_HEADS, 8, LANES))
    lb_fn = lambda l: jnp.cumsum(jax.nn.softmax(l.astype(F32), axis=0), axis=0)[:1]
    lb, lb_vjp = jax.vjp(lb_fn, lb_logits)
    gain_h = hgrn_norm[0].reshape(1, HGRN_WIDTH)

    h, rstd = _rmsnorm_fwd(x2, norm_pre)
    proj = _matmul(h, win_f, "nn", F32, "in_proj")
    ya = _attn_fwd(proj, bias_tab, sinks_b, B_loc, S)
    yh, states = _hgrn_fwd(proj, lb, gain_h, B_loc, S)
    ua = _matmul(ya, wba_f, "nn", F32, "branch_attn_proj")
    uh = _matmul(yh, wbh_f, "nn", F32, "branch_hgrn_proj")
    merged = _merge_fwd(proj, ua, uh)
    yv = _matmul(merged, wout_f, "nn", F32, "out_proj")
    dy, dout, loss_p, gnpost_p = _post_loss(yv, x2, tgt2, norm_post)

    dmerged = _matmul(dy, wout_f, "nt", F32, "out_proj_dgrad")
    g_wout = _matmul(merged, dy, "tn", BF16, "out_proj_wgrad")
    d_ua, d_uh, d_ga, d_gh = _merge_bwd(proj, ua, uh, dmerged)
    g_wba = _matmul(ya, d_ua, "tn", BF16, "branch_attn_wgrad", slabs=N_CHIPS)
    g_wbh = _matmul(yh, d_uh, "tn", BF16, "branch_hgrn_wgrad", slabs=N_CHIPS)
    d_ya = _matmul(d_ua, wba_f, "nt", BF16, "branch_attn_dgrad")
    d_yh = _matmul(d_uh, wbh_f, "nt", BF16, "branch_hgrn_dgrad")
    dq, dk, dv, dag, dbias_p, dsinks_p = _attn_bwd(proj, bias_tab, sinks_b, d_ya, B_loc, S)
    dhq, dhf, dhi, dhg, dlb_p, dgain_p = _hgrn_bwd(proj, states, lb, gain_h, d_yh, B_loc, S)
    dproj = jnp.concatenate([dq, dk, dv, dag, dhq, dhf, dhi, dhg, d_ga, d_gh], axis=1)
    g_win = _matmul(h, dproj, "tn", BF16, "in_proj_wgrad", slabs=N_CHIPS)
    dh = _matmul(dproj, win_f, "nt", F32, "in_proj_dgrad")
    grad_x2, gnpre_p = _rmsnorm_bwd(dh, x2, rstd, norm_pre, dout)

    onehot = (buckets.reshape(-1)[:, None] == jnp.arange(REL_BUCKETS)[None, :]).astype(F32)
    grelb_p = jnp.dot(dbias_p.reshape(ATTN_HEADS, -1), onehot, precision=lax.Precision.HIGHEST).T
    gsinks_p = dsinks_p[:, 0, 0]
    dlb_sum = jnp.sum(dlb_p, axis=0).reshape(1, HGRN_WIDTH)
    ghn_p = jnp.sum(dgain_p, axis=0).reshape(HGRN_HEADS, HGRN_DIM)
    small_parts = [gnpre_p, gnpost_p, grelb_p, gsinks_p, dlb_sum, ghn_p, loss_p]
    small_shapes = [p.shape for p in small_parts]
    n_rows = _round8(sum(-(-math.prod(s) // LANES) for s in small_shapes))
    pack = _pack_rows(small_parts, n_rows)

    pieces = [g_win.reshape(N_CHIPS, 2, -1, g_win.shape[-1]), g_wba.reshape(N_CHIPS, 2, -1, g_wba.shape[-1]),
              g_wbh.reshape(N_CHIPS, 2, -1, g_wbh.shape[-1]), g_wout.reshape(N_CHIPS, 2, -1, g_wout.shape[-1])]
    lands = _exchange_sibling_halves(pieces)
    sums = [_sum_sibling(g, l, c_arr, "sum_sibling_" + nm) for g, l, nm in zip(pieces, lands, names)]
    *recvs, pack_sum = _exchange_chips(sums, pack)
    halves = [_sum_chips(g, l, r, sc_arr, "sum_chips_" + nm) for g, l, r, nm in zip(pieces, lands, recvs, names)]
    shared = _share_with_sibling(halves)
    big_w = [w_in, w_branch_attn, w_branch_hgrn, w_out]
    big_m = [m_w_in, m_w_branch_attn, m_w_branch_hgrn, m_w_out]
    big_v = [v_w_in, v_w_branch_attn, v_w_branch_hgrn, v_w_out]
    big = {}
    for nm, gs, w, m, v in zip(names, shared, big_w, big_m, big_v):
        shp = w.shape
        g2 = gs.reshape(shp[1], shp[2])
        d, nm_, nv_ = _adamw(w[0], g2, m[0], v[0], "adamw_" + nm)
        big[nm] = tuple(a.reshape(shp) for a in (g2, d, nm_, nv_))

    gnpre, gnpost, grelb, gsinks, dlb, ghn, loss = _unpack_rows(pack_sum, small_shapes)
    (g_lb_logits,) = lb_vjp(dlb)
    small_names = ["norm_pre", "rel_bias", "attn_sinks", "lb_logits", "hgrn_norm", "norm_post"]
    small_w = [norm_pre, rel_bias, attn_sinks, lb_logits, hgrn_norm, norm_post]
    small_m = [m_norm_pre, m_rel_bias, m_attn_sinks, m_lb_logits, m_hgrn_norm, m_norm_post]
    small_v = [v_norm_pre, v_rel_bias, v_attn_sinks, v_lb_logits, v_hgrn_norm, v_norm_post]
    small_g = [gnpre.reshape(norm_pre.shape), grelb.reshape(rel_bias.shape), gsinks.reshape(attn_sinks.shape),
               g_lb_logits.reshape(lb_logits.shape), ghn.reshape(hgrn_norm.shape), gnpost.reshape(norm_post.shape)]
    shapes = [w.shape for w in small_w]
    a_rows = _round8(sum(-(-math.prod(s) // LANES) for s in shapes))
    d_s, nm_s, nv_s = _adamw(_pack_rows(small_w, a_rows), _pack_rows(small_g, a_rows), _pack_rows(small_m, a_rows),
                             _pack_rows(small_v, a_rows), "adamw_small")
    small = {}
    for nm, g, d, m_, v_ in zip(small_names, small_g, _unpack_rows(d_s, shapes), _unpack_rows(nm_s, shapes),
                                _unpack_rows(nv_s, shapes)):
        small[nm] = (g, d, m_, v_)

    allw = {**big, **small}
    order = ["norm_pre", "w_in", "rel_bias", "attn_sinks", "lb_logits", "hgrn_norm", "w_branch_attn", "w_branch_hgrn",
             "w_out", "norm_post"]
    outs = [loss.reshape(()), grad_x2.reshape(B_loc, S, D)]
    for k in range(4):
        outs += [allw[nm][k] for nm in order]
    return tuple(outs)
```

```python
import functools
import math

import jax
import jax.numpy as jnp
from jax import lax
from jax.experimental import pallas as pl
from jax.experimental.pallas import tpu as pltpu

F32 = jnp.float32
BF16 = jnp.bfloat16
MESH = pl.DeviceIdType.MESH

ATTN_HEADS = 16
ATTN_KV_HEADS = 4
HEAD_DIM = 64
GROUP = ATTN_HEADS // ATTN_KV_HEADS
WINDOW = 128
ATTN_WIDTH = ATTN_HEADS * HEAD_DIM
KV_WIDTH = ATTN_KV_HEADS * HEAD_DIM
HGRN_HEADS = 8
HGRN_DIM = 128
HGRN_WIDTH = HGRN_HEADS * HGRN_DIM
CHUNK = 64
SUB = 16
NSUB = CHUNK // SUB
REL_BUCKETS = 32
REL_MAX_DIST = 128
NORM_EPS = 1e-6
ADAM_LR = 0.001
ADAM_B1 = 0.9
ADAM_B2 = 0.999
ADAM_EPS = 1e-08
ADAM_WD = 0.01
ADAM_STEP = 10
LANES = 128
N_CHIPS = 4
N_DEV = 8
VMEM_LIMIT = 48 * 1024 * 1024
MATMUL_OPERAND_BYTES = 20 * 1024 * 1024

OFF_AQ = 0
OFF_AK = OFF_AQ + ATTN_WIDTH
OFF_AV = OFF_AK + KV_WIDTH
OFF_AG = OFF_AV + KV_WIDTH
OFF_HQ = OFF_AG + ATTN_WIDTH
OFF_HF = OFF_HQ + HGRN_WIDTH
OFF_HI = OFF_HF + HGRN_WIDTH
OFF_HG = OFF_HI + HGRN_WIDTH
OFF_GA = OFF_HG + HGRN_WIDTH

NT_DIMS = (((1,), (1,)), ((), ()))
TN_DIMS = (((0,), (0,)), ((), ()))
NN_DIMS = (((1,), (0,)), ((), ()))


def _pick(n, cands):
    for c in cands:
        if n % c == 0:
            return c
    raise ValueError(f"no tile for {n} in {cands}")


def _params(sem):
    return pltpu.CompilerParams(dimension_semantics=sem, vmem_limit_bytes=VMEM_LIMIT)


def _bdot(a, b, dims=NN_DIMS):
    return lax.dot_general(a.astype(BF16), b.astype(BF16), dims, preferred_element_type=F32)


def _matmul(a, b, mode, out_dtype, name, slabs=1):
    if mode == "nn":
        (M, K), (K2, N) = a.shape, b.shape
    elif mode == "nt":
        (M, K), (N, K2) = a.shape, b.shape
    else:
        (K, M), (K2, N) = a.shape, b.shape
    assert K == K2
    nslab = N // slabs
    tm = _pick(M, (1024, 512, 256, 128))
    tn = _pick(nslab, (1024, 896, 512, 384, 256, 128))
    tk = _pick(K, [t for t in (4096, 2048, 1792, 1536, 1024, 512, 256, 128) if 4 * t * (tm + tn) <= MATMUL_OPERAND_BYTES])
    nk = K // tk
    per = nslab // tn
    dims = {"nn": NN_DIMS, "nt": NT_DIMS, "tn": TN_DIMS}[mode]

    def body(a_ref, b_ref, o_ref, *acc):
        part = lax.dot_general(a_ref[...], b_ref[...], dims, preferred_element_type=F32)
        if nk == 1:
            o_ref[...] = part.astype(o_ref.dtype)
            return
        acc_ref, = acc
        k = pl.program_id(2)

        @pl.when(k == 0)
        def _():
            acc_ref[...] = part

        @pl.when((k > 0) & (k < nk - 1))
        def _():
            acc_ref[...] += part

        @pl.when(k == nk - 1)
        def _():
            o_ref[...] = (acc_ref[...] + part).astype(o_ref.dtype)

    if mode == "tn":
        a_spec = pl.BlockSpec((tk, tm), lambda i, j, k: (k, i))
    else:
        a_spec = pl.BlockSpec((tm, tk), lambda i, j, k: (i, k))
    if mode == "nt":
        b_spec = pl.BlockSpec((tn, tk), lambda i, j, k: (j, k))
    else:
        b_spec = pl.BlockSpec((tk, tn), lambda i, j, k: (k, j))
    if slabs == 1:
        o_shape = jax.ShapeDtypeStruct((M, N), out_dtype)
        o_spec = pl.BlockSpec((tm, tn), lambda i, j, k: (i, j))
    else:
        o_shape = jax.ShapeDtypeStruct((slabs, M, nslab), out_dtype)
        o_spec = pl.BlockSpec((None, tm, tn), lambda i, j, k: (j // per, i, j % per))
    return pl.pallas_call(
        body, name=name, grid=(M // tm, N // tn, nk), in_specs=[a_spec, b_spec], out_specs=o_spec, out_shape=o_shape,
        scratch_shapes=[pltpu.VMEM((tm, tn), F32)] if nk > 1 else [],
        compiler_params=_params(("parallel", "parallel", "arbitrary")),
    )(a, b)


def _rmsnorm_fwd(x2, gain):
    T, D = x2.shape
    tr = _pick(T, (256, 128))

    def body(x_ref, g_ref, h_ref, r_ref):
        xv = x_ref[...]
        r = lax.rsqrt(jnp.mean(xv * xv, axis=-1, keepdims=True) + NORM_EPS)
        h_ref[...] = (xv * r * g_ref[...]).astype(BF16)
        r_ref[...] = r

    return pl.pallas_call(
        body, name="rmsnorm_pre_fwd", grid=(T // tr,),
        in_specs=[pl.BlockSpec((tr, D), lambda i: (i, 0)), pl.BlockSpec((1, D), lambda i: (0, 0))],
        out_specs=[pl.BlockSpec((tr, D), lambda i: (i, 0)), pl.BlockSpec((tr, 1), lambda i: (i, 0))],
        out_shape=[jax.ShapeDtypeStruct((T, D), BF16), jax.ShapeDtypeStruct((T, 1), F32)],
        compiler_params=_params(("parallel",)),
    )(x2, gain)


def _rmsnorm_bwd(dh, x2, rstd, gain, dout):
    T, D = x2.shape
    tr = _pick(T, (256, 128))

    def body(dh_ref, x_ref, r_ref, g_ref, do_ref, gx_ref, gg_ref):
        @pl.when(pl.program_id(0) == 0)
        def _():
            gg_ref[...] = jnp.zeros_like(gg_ref)

        n = x_ref[...] * r_ref[...]
        dhv = dh_ref[...]
        dn = dhv * g_ref[...]
        gx_ref[...] = do_ref[...] + r_ref[...] * (dn - n * jnp.mean(dn * n, axis=-1, keepdims=True))
        gg_ref[...] += jnp.sum(dhv * n, axis=0, keepdims=True)

    row = pl.BlockSpec((tr, D), lambda i: (i, 0))
    vec = pl.BlockSpec((1, D), lambda i: (0, 0))
    return pl.pallas_call(
        body, name="rmsnorm_pre_bwd", grid=(T // tr,),
        in_specs=[row, row, pl.BlockSpec((tr, 1), lambda i: (i, 0)), vec, row],
        out_specs=[row, vec],
        out_shape=[jax.ShapeDtypeStruct((T, D), F32), jax.ShapeDtypeStruct((1, D), F32)],
        compiler_params=_params(("arbitrary",)),
    )(dh, x2, rstd, gain, dout)


def _post_loss(yv, x2, tgt2, gain):
    T, D = x2.shape
    tr = _pick(T, (256, 128))

    def body(y_ref, x_ref, t_ref, g_ref, dy_ref, do_ref, loss_ref, gg_ref):
        @pl.when(pl.program_id(0) == 0)
        def _():
            gg_ref[...] = jnp.zeros_like(gg_ref)
            loss_ref[...] = jnp.zeros_like(loss_ref)

        yv_ = y_ref[...]
        r = lax.rsqrt(jnp.mean(yv_ * yv_, axis=-1, keepdims=True) + NORM_EPS)
        n = yv_ * r
        e = (x_ref[...] + n * g_ref[...]) - t_ref[...]
        loss_ref[...] += 0.5 * jnp.sum(jnp.mean(e * e, axis=-1, keepdims=True), axis=0, keepdims=True)
        dz = e / D
        do_ref[...] = dz
        gg_ref[...] += jnp.sum(dz * n, axis=0, keepdims=True)
        dn = dz * g_ref[...]
        dy_ref[...] = (r * (dn - n * jnp.mean(dn * n, axis=-1, keepdims=True))).astype(BF16)

    row = pl.BlockSpec((tr, D), lambda i: (i, 0))
    vec = pl.BlockSpec((1, D), lambda i: (0, 0))
    return pl.pallas_call(
        body, name="post_norm_loss", grid=(T // tr,),
        in_specs=[row, row, row, vec],
        out_specs=[row, row, pl.BlockSpec((1, 1), lambda i: (0, 0)), vec],
        out_shape=[jax.ShapeDtypeStruct((T, D), BF16), jax.ShapeDtypeStruct((T, D), F32),
                   jax.ShapeDtypeStruct((1, 1), F32), jax.ShapeDtypeStruct((1, D), F32)],
        compiler_params=_params(("arbitrary",)),
    )(yv, x2, tgt2, gain)


def _merge_fwd(proj, ua, uh):
    T, D = ua.shape
    tr = _pick(T, (256, 128))
    bw = _pick(D, (512, 256))
    assert OFF_GA % bw == 0
    oa, oh = OFF_GA // bw, (OFF_GA + D) // bw

    def body(ga_ref, gh_ref, ua_ref, uh_ref, m_ref):
        m_ref[...] = (jax.nn.sigmoid(ga_ref[...]) * ua_ref[...] + jax.nn.sigmoid(gh_ref[...]) * uh_ref[...]).astype(BF16)

    blk = pl.BlockSpec((tr, bw), lambda i, j: (i, j))
    return pl.pallas_call(
        body, name="merge_fwd", grid=(T // tr, D // bw),
        in_specs=[pl.BlockSpec((tr, bw), lambda i, j: (i, oa + j)), pl.BlockSpec((tr, bw), lambda i, j: (i, oh + j)), blk, blk],
        out_specs=blk, out_shape=jax.ShapeDtypeStruct((T, D), BF16),
        compiler_params=_params(("parallel", "parallel")),
    )(proj, proj, ua, uh)


def _merge_bwd(proj, ua, uh, dm):
    T, D = ua.shape
    tr = _pick(T, (256, 128))
    bw = _pick(D, (512, 256))
    oa, oh = OFF_GA // bw, (OFF_GA + D) // bw

    def body(ga_ref, gh_ref, ua_ref, uh_ref, dm_ref, dua_ref, duh_ref, dga_ref, dgh_ref):
        sa = jax.nn.sigmoid(ga_ref[...])
        sh = jax.nn.sigmoid(gh_ref[...])
        d = dm_ref[...]
        dua_ref[...] = (d * sa).astype(BF16)
        duh_ref[...] = (d * sh).astype(BF16)
        dga_ref[...] = (d * ua_ref[...] * sa * (1.0 - sa)).astype(BF16)
        dgh_ref[...] = (d * uh_ref[...] * sh * (1.0 - sh)).astype(BF16)

    blk = pl.BlockSpec((tr, bw), lambda i, j: (i, j))
    o = jax.ShapeDtypeStruct((T, D), BF16)
    return pl.pallas_call(
        body, name="merge_bwd", grid=(T // tr, D // bw),
        in_specs=[pl.BlockSpec((tr, bw), lambda i, j: (i, oa + j)), pl.BlockSpec((tr, bw), lambda i, j: (i, oh + j)), blk, blk, blk],
        out_specs=[blk, blk, blk, blk], out_shape=[o, o, o, o],
        compiler_params=_params(("parallel", "parallel")),
    )(proj, proj, ua, uh, dm)


KV_PAIR = 2
PAIR_HEADS = KV_PAIR * GROUP


def _attn_mask(n):
    qi = lax.broadcasted_iota(jnp.int32, (WINDOW, 2 * WINDOW), 0)
    si = lax.broadcasted_iota(jnp.int32, (WINDOW, 2 * WINDOW), 1)
    dist = qi + WINDOW - si
    return (dist >= 0) & (dist < WINDOW) & ((si >= WINDOW) | (n > 0))


def _attn_group_fn(mask):
    def f(q4, k, v, ag4, bias4, sink4):
        kb = k.astype(BF16)
        vb = v.astype(BF16)
        outs = []
        for g in range(GROUP):
            s = lax.dot_general(q4[g].astype(BF16), kb, NT_DIMS, preferred_element_type=F32) * (HEAD_DIM ** -0.5)
            s = jnp.where(mask, s + bias4[g], -1e30)
            m = lax.stop_gradient(jnp.maximum(jnp.max(s, axis=-1, keepdims=True), sink4[g]))
            p = jnp.exp(s - m)
            den = jnp.sum(p, axis=-1, keepdims=True) + jnp.exp(sink4[g] - m)
            o = jnp.dot((p / den).astype(BF16), vb, preferred_element_type=F32)
            outs.append(o * jax.nn.silu(ag4[g]))
        return outs

    return f


def _attn_specs(B_loc, nb, order):
    qw = PAIR_HEADS * HEAD_DIM
    kw = KV_PAIR * HEAD_DIM

    def rows(g):
        b, p, n = order(*g)
        return b * nb + n

    def prev(g):
        b, p, n = order(*g)
        return b * nb + jnp.maximum(n - 1, 0)

    def pp(g):
        return order(*g)[1]

    q = pl.BlockSpec((WINDOW, qw), lambda *g: (rows(g), OFF_AQ // qw + pp(g)))
    kc = pl.BlockSpec((WINDOW, kw), lambda *g: (rows(g), OFF_AK // kw + pp(g)))
    kp = pl.BlockSpec((WINDOW, kw), lambda *g: (prev(g), OFF_AK // kw + pp(g)))
    vc = pl.BlockSpec((WINDOW, kw), lambda *g: (rows(g), OFF_AV // kw + pp(g)))
    vp = pl.BlockSpec((WINDOW, kw), lambda *g: (prev(g), OFF_AV // kw + pp(g)))
    ag = pl.BlockSpec((WINDOW, qw), lambda *g: (rows(g), OFF_AG // qw + pp(g)))
    bias = pl.BlockSpec((PAIR_HEADS, WINDOW, 2 * WINDOW), lambda *g: (pp(g), 0, 0))
    sink = pl.BlockSpec((PAIR_HEADS, 8, LANES), lambda *g: (pp(g), 0, 0))
    return [q, kc, kp, vc, vp, ag, bias, sink], rows, pp


def _attn_operands(q_ref, kc_ref, kp_ref, vc_ref, vp_ref, ag_ref, bias_ref, sink_ref, j):
    lo, hi = j * HEAD_DIM, (j + 1) * HEAD_DIM
    k = jnp.concatenate([kp_ref[:, lo:hi], kc_ref[:, lo:hi]], axis=0)
    v = jnp.concatenate([vp_ref[:, lo:hi], vc_ref[:, lo:hi]], axis=0)
    q4 = [q_ref[:, (j * GROUP + g) * HEAD_DIM:(j * GROUP + g + 1) * HEAD_DIM] for g in range(GROUP)]
    ag4 = [ag_ref[:, (j * GROUP + g) * HEAD_DIM:(j * GROUP + g + 1) * HEAD_DIM] for g in range(GROUP)]
    bias4 = [bias_ref[j * GROUP + g] for g in range(GROUP)]
    sink4 = [sink_ref[j * GROUP + g, 0:1, 0:1] for g in range(GROUP)]
    return q4, k, v, ag4, bias4, sink4


def _attn_fwd(proj, bias_tab, sinks_b, B_loc, S):
    T = B_loc * S
    nb = S // WINDOW
    n_pairs = ATTN_KV_HEADS // KV_PAIR
    in_specs, rows, pp = _attn_specs(B_loc, nb, lambda b, p, n: (b, p, n))

    def body(q_ref, kc_ref, kp_ref, vc_ref, vp_ref, ag_ref, bias_ref, sink_ref, ya_ref):
        f = _attn_group_fn(_attn_mask(pl.program_id(2)))
        for j in range(KV_PAIR):
            outs = f(*_attn_operands(q_ref, kc_ref, kp_ref, vc_ref, vp_ref, ag_ref, bias_ref, sink_ref, j))
            for g in range(GROUP):
                h = j * GROUP + g
                ya_ref[:, h * HEAD_DIM:(h + 1) * HEAD_DIM] = outs[g].astype(BF16)

    return pl.pallas_call(
        body, name="attn_fwd", grid=(B_loc, n_pairs, nb), in_specs=in_specs,
        out_specs=pl.BlockSpec((WINDOW, PAIR_HEADS * HEAD_DIM), lambda *g: (rows(g), pp(g))),
        out_shape=jax.ShapeDtypeStruct((T, ATTN_WIDTH), BF16),
        compiler_params=_params(("parallel", "parallel", "parallel")),
    )(proj, proj, proj, proj, proj, proj, bias_tab, sinks_b)


def _attn_bwd(proj, bias_tab, sinks_b, dya, B_loc, S):
    T = B_loc * S
    nb = S // WINDOW
    n_pairs = ATTN_KV_HEADS // KV_PAIR
    order = lambda p, b, i: (b, p, nb - 1 - i)
    in_specs, rows, pp = _attn_specs(B_loc, nb, order)
    qw = PAIR_HEADS * HEAD_DIM
    kw = KV_PAIR * HEAD_DIM

    def body(q_ref, kc_ref, kp_ref, vc_ref, vp_ref, ag_ref, bias_ref, sink_ref, dya_ref,
             dq_ref, dk_ref, dv_ref, dag_ref, dbias_ref, dsink_ref, dkc_ref, dvc_ref):
        b, i = pl.program_id(1), pl.program_id(2)
        n = nb - 1 - i

        @pl.when((b == 0) & (i == 0))
        def _():
            dbias_ref[...] = jnp.zeros_like(dbias_ref)
            dsink_ref[...] = jnp.zeros_like(dsink_ref)

        @pl.when(i == 0)
        def _():
            dkc_ref[...] = jnp.zeros_like(dkc_ref)
            dvc_ref[...] = jnp.zeros_like(dvc_ref)

        f = _attn_group_fn(_attn_mask(n))
        for j in range(KV_PAIR):
            ops = _attn_operands(q_ref, kc_ref, kp_ref, vc_ref, vp_ref, ag_ref, bias_ref, sink_ref, j)
            _, vjp = jax.vjp(f, *ops)
            douts = [dya_ref[:, (j * GROUP + g) * HEAD_DIM:(j * GROUP + g + 1) * HEAD_DIM].astype(F32) for g in range(GROUP)]
            dq4, dk, dv, dag4, dbias4, dsink4 = vjp(douts)
            lo, hi = j * HEAD_DIM, (j + 1) * HEAD_DIM
            dk_ref[:, lo:hi] = (dk[WINDOW:] + dkc_ref[:, lo:hi]).astype(BF16)
            dv_ref[:, lo:hi] = (dv[WINDOW:] + dvc_ref[:, lo:hi]).astype(BF16)
            dkc_ref[:, lo:hi] = dk[:WINDOW]
            dvc_ref[:, lo:hi] = dv[:WINDOW]
            for g in range(GROUP):
                h = j * GROUP + g
                dq_ref[:, h * HEAD_DIM:(h + 1) * HEAD_DIM] = dq4[g].astype(BF16)
                dag_ref[:, h * HEAD_DIM:(h + 1) * HEAD_DIM] = dag4[g].astype(BF16)
                dbias_ref[h] += dbias4[g]
                dsink_ref[h] += jnp.broadcast_to(dsink4[g], (8, LANES))

    wide = pl.BlockSpec((WINDOW, qw), lambda *g: (rows(g), pp(g)))
    narrow = pl.BlockSpec((WINDOW, kw), lambda *g: (rows(g), pp(g)))
    return pl.pallas_call(
        body, name="attn_bwd", grid=(n_pairs, B_loc, nb), in_specs=in_specs + [wide],
        out_specs=[wide, narrow, narrow, wide,
                   pl.BlockSpec((PAIR_HEADS, WINDOW, 2 * WINDOW), lambda *g: (pp(g), 0, 0)),
                   pl.BlockSpec((PAIR_HEADS, 8, LANES), lambda *g: (pp(g), 0, 0))],
        out_shape=[jax.ShapeDtypeStruct((T, ATTN_WIDTH), BF16), jax.ShapeDtypeStruct((T, KV_WIDTH), BF16),
                   jax.ShapeDtypeStruct((T, KV_WIDTH), BF16), jax.ShapeDtypeStruct((T, ATTN_WIDTH), BF16),
                   jax.ShapeDtypeStruct((ATTN_HEADS, WINDOW, 2 * WINDOW), F32),
                   jax.ShapeDtypeStruct((ATTN_HEADS, 8, LANES), F32)],
        scratch_shapes=[pltpu.VMEM((WINDOW, kw), F32), pltpu.VMEM((WINDOW, kw), F32)],
        compiler_params=_params(("arbitrary", "arbitrary", "arbitrary")),
    )(proj, proj, proj, proj, proj, proj, bias_tab, sinks_b, dya)


class _HgrnPre:
    def __init__(self, fr, qr, lb, g_scr):
        t = lax.broadcasted_iota(jnp.int32, (CHUNK, CHUNK), 0)
        s = lax.broadcasted_iota(jnp.int32, (CHUNK, CHUNK), 1)
        self.sg = jax.nn.sigmoid(fr)
        self.f = lb + (1.0 - lb) * self.sg
        g = jnp.dot((t >= s).astype(F32), jnp.log(self.f), precision=lax.Precision.HIGHEST, preferred_element_type=F32)
        g_scr[...] = g
        self.g = g
        self.row = lax.broadcasted_iota(jnp.int32, g.shape, 0)
        self.refs = [jnp.zeros((1, g.shape[1]), F32)] + [g_scr[pl.ds(i * SUB - 1, 1), :] for i in range(1, NSUB)]
        self.gend = g_scr[pl.ds(CHUNK - 1, 1), :]
        refrow = jnp.zeros_like(g)
        for i in range(1, NSUB):
            refrow = jnp.where(self.row >= i * SUB, self.refs[i], refrow)
        self.sigq = jax.nn.sigmoid(qr)
        self.qs = qr * self.sigq
        self.k = 1.0 - self.f
        self.eg = jnp.exp(g)
        self.eqd = jnp.exp(g - refrow)
        self.ekd = [jnp.exp(jnp.where(self.row < (i + 1) * SUB, self.refs[i] - g, 0.0)) for i in range(NSUB)]
        self.ekdec = jnp.exp(self.gend - g)
        self.qg = self.qs * self.eg
        self.qd = self.qs * self.eqd
        self.kd = [self.k * e for e in self.ekd]
        self.kdec = self.k * self.ekdec
        self.egend = jnp.exp(self.gend)


def _hgrn_pair_mask():
    t = lax.broadcasted_iota(jnp.int32, (CHUNK, NSUB * CHUNK), 0)
    col = lax.broadcasted_iota(jnp.int32, (CHUNK, NSUB * CHUNK), 1)
    return ((t // SUB) == (col // CHUNK)) & ((col % CHUNK) <= t)


def _hgrn_head_out(p, lanes, state_t, v, mask):
    qg, qd = p.qg[:, lanes], p.qd[:, lanes]
    kall = jnp.concatenate([kd[:, lanes] for kd in p.kd], axis=0)
    vst = jnp.concatenate([v] * NSUB, axis=0)
    am = jnp.where(mask, _bdot(qd, kall, NT_DIMS), 0.0)
    o = _bdot(qg, state_t, NT_DIMS) + _bdot(am, vst)
    return o, (qg, qd, kall, am, vst)


HEADS_PER_STEP = 4
GROUP_WIDTH = HEADS_PER_STEP * HGRN_DIM
N_HEAD_GROUPS = HGRN_HEADS // HEADS_PER_STEP


def _hgrn_specs(nc, order):
    def rows(g):
        b, grp, n = order(*g)
        return b * nc + n

    def grp_of(g):
        return order(*g)[1]

    def col(off):
        assert off % GROUP_WIDTH == 0
        return pl.BlockSpec((CHUNK, GROUP_WIDTH), lambda *g: (rows(g), off // GROUP_WIDTH + grp_of(g)))

    vec = pl.BlockSpec((1, GROUP_WIDTH), lambda *g: (0, grp_of(g)))
    return [col(OFF_HQ), col(OFF_HF), col(OFF_HI), col(OFF_HG), vec, vec], rows, grp_of


def _hgrn_fwd(proj, lb, gain, B_loc, S):
    T = B_loc * S
    nc = S // CHUNK
    in_specs, rows, grp_of = _hgrn_specs(nc, lambda b, grp, n: (b, grp, n))

    def body(q_ref, f_ref, v_ref, hg_ref, lb_ref, gain_ref, yh_ref, st_ref, state_scr, g_scr):
        @pl.when(pl.program_id(2) == 0)
        def _():
            state_scr[...] = jnp.zeros_like(state_scr)

        p = _HgrnPre(f_ref[...], q_ref[...], lb_ref[...], g_scr)
        v = v_ref[...]
        gate = gain_ref[...] * jax.nn.silu(hg_ref[...])
        mask = _hgrn_pair_mask()
        for hd in range(HEADS_PER_STEP):
            lanes = slice(hd * HGRN_DIM, (hd + 1) * HGRN_DIM)
            st = state_scr[hd]
            st_ref[hd] = st
            o, _ = _hgrn_head_out(p, lanes, st, v[:, lanes], mask)
            rs = lax.rsqrt(jnp.mean(o * o, axis=-1, keepdims=True) + NORM_EPS)
            yh_ref[:, lanes] = (o * rs * gate[:, lanes]).astype(BF16)
            state_scr[hd] = st * p.egend[:, lanes] + _bdot(v[:, lanes], p.kdec[:, lanes], TN_DIMS)

    return pl.pallas_call(
        body, name="hgrn_fwd", grid=(B_loc, N_HEAD_GROUPS, nc), in_specs=in_specs,
        out_specs=[pl.BlockSpec((CHUNK, GROUP_WIDTH), lambda *g: (rows(g), grp_of(g))),
                   pl.BlockSpec((None, None, HEADS_PER_STEP, HGRN_DIM, HGRN_DIM), lambda b, grp, n: (b, n, grp, 0, 0))],
        out_shape=[jax.ShapeDtypeStruct((T, HGRN_WIDTH), BF16),
                   jax.ShapeDtypeStruct((B_loc, nc, HGRN_HEADS, HGRN_DIM, HGRN_DIM), F32)],
        scratch_shapes=[pltpu.VMEM((HEADS_PER_STEP, HGRN_DIM, HGRN_DIM), F32), pltpu.VMEM((CHUNK, GROUP_WIDTH), F32)],
        compiler_params=_params(("parallel", "parallel", "arbitrary")),
    )(proj, proj, proj, proj, lb, gain)


def _hgrn_bwd(proj, states, lb, gain, dyh, B_loc, S):
    T = B_loc * S
    nc = S // CHUNK
    in_specs, rows, grp_of = _hgrn_specs(nc, lambda b, grp, i: (b, grp, nc - 1 - i))
    blk = pl.BlockSpec((CHUNK, GROUP_WIDTH), lambda *g: (rows(g), grp_of(g)))
    st_spec = pl.BlockSpec((None, None, HEADS_PER_STEP, HGRN_DIM, HGRN_DIM), lambda b, grp, i: (b, nc - 1 - i, grp, 0, 0))
    acc_spec = pl.BlockSpec((None, 1, GROUP_WIDTH), lambda b, grp, i: (b, 0, grp))

    def body(q_ref, f_ref, v_ref, hg_ref, lb_ref, gain_ref, st_ref, dyh_ref,
             dq_ref, df_ref, dv_ref, dhg_ref, dlb_ref, dgain_ref, dstate_scr, g_scr, dg_scr):
        @pl.when(pl.program_id(2) == 0)
        def _():
            dstate_scr[...] = jnp.zeros_like(dstate_scr)
            dlb_ref[...] = jnp.zeros_like(dlb_ref)
            dgain_ref[...] = jnp.zeros_like(dgain_ref)

        qr, lb, gain, hg, v = q_ref[...], lb_ref[...], gain_ref[...], hg_ref[...], v_ref[...]
        p = _HgrnPre(f_ref[...], qr, lb, g_scr)
        sgh = jax.nn.sigmoid(hg)
        sil = hg * sgh
        dy = dyh_ref[...].astype(F32)
        mask = _hgrn_pair_mask()
        dqg, dqd, dkdec, dv, dhg, dgend, dgain = [], [], [], [], [], [], []
        dkd = [[] for _ in range(NSUB)]
        for hd in range(HEADS_PER_STEP):
            lanes = slice(hd * HGRN_DIM, (hd + 1) * HGRN_DIM)
            st, dnew = st_ref[hd], dstate_scr[hd]
            o, (qg, qd, kall, am, vst) = _hgrn_head_out(p, lanes, st, v[:, lanes], mask)
            rs = lax.rsqrt(jnp.mean(o * o, axis=-1, keepdims=True) + NORM_EPS)
            n = o * rs
            dyn = dy[:, lanes] * n
            dgain.append(jnp.sum(dyn * sil[:, lanes], axis=0, keepdims=True))
            dhg.append(dyn * gain[:, lanes] * (sgh * (1.0 + hg * (1.0 - sgh)))[:, lanes])
            dn = dy[:, lanes] * gain[:, lanes] * sil[:, lanes]
            do = rs * (dn - n * jnp.mean(dn * n, axis=-1, keepdims=True))
            dqg.append(_bdot(do, st))
            dstate_scr[hd] = _bdot(do, qg, TN_DIMS) + dnew * p.egend[:, lanes]
            dr = jnp.where(mask, _bdot(do, vst, NT_DIMS), 0.0)
            dvst = _bdot(am, do, TN_DIMS)
            dv.append(sum(dvst[i * CHUNK:(i + 1) * CHUNK] for i in range(NSUB)) + _bdot(p.kdec[:, lanes], dnew, NT_DIMS))
            dqd.append(_bdot(dr, kall))
            dkall = _bdot(dr, qd, TN_DIMS)
            for i in range(NSUB):
                dkd[i].append(dkall[i * CHUNK:(i + 1) * CHUNK])
            dkdec_h = _bdot(v[:, lanes], dnew)
            dkdec.append(dkdec_h)
            dgend.append(jnp.sum(dkdec_h * p.kdec[:, lanes], axis=0, keepdims=True)
                         + jnp.sum(dnew * st, axis=0, keepdims=True) * p.egend[:, lanes])

        wide = lambda parts: jnp.concatenate(parts, axis=1)
        dqg, dqd, dkdec = wide(dqg), wide(dqd), wide(dkdec)
        t2 = dqd * p.qd
        dg = dqg * p.qg + t2 - dkdec * p.kdec
        dk = dkdec * p.ekdec
        dg_scr[...] = jnp.zeros_like(dg_scr)
        for i in range(NSUB):
            dkd_i = wide(dkd[i])
            tk = jnp.where(p.row < (i + 1) * SUB, dkd_i * p.kd[i], 0.0)
            dg = dg - tk
            dk = dk + dkd_i * p.ekd[i]
            if i >= 1:
                in_blk = (p.row >= i * SUB) & (p.row < (i + 1) * SUB)
                dg_scr[pl.ds(i * SUB - 1, 1), :] = (jnp.sum(tk, axis=0, keepdims=True)
                                                    - jnp.sum(jnp.where(in_blk, t2, 0.0), axis=0, keepdims=True))
        dg_scr[pl.ds(CHUNK - 1, 1), :] = wide(dgend)
        t = lax.broadcasted_iota(jnp.int32, (CHUNK, CHUNK), 0)
        s = lax.broadcasted_iota(jnp.int32, (CHUNK, CHUNK), 1)
        dlogf = jnp.dot((t <= s).astype(F32), dg + dg_scr[...], precision=lax.Precision.HIGHEST, preferred_element_type=F32)
        df = dlogf / p.f - dk
        df_ref[...] = (df * (1.0 - lb) * p.sg * (1.0 - p.sg)).astype(BF16)
        dlb_ref[...] += jnp.sum(df * (1.0 - p.sg), axis=0, keepdims=True)
        dq_ref[...] = ((dqg * p.eg + dqd * p.eqd) * p.sigq * (1.0 + qr * (1.0 - p.sigq))).astype(BF16)
        dv_ref[...] = wide(dv).astype(BF16)
        dhg_ref[...] = wide(dhg).astype(BF16)
        dgain_ref[...] += wide(dgain)

    o = jax.ShapeDtypeStruct((T, HGRN_WIDTH), BF16)
    acc = jax.ShapeDtypeStruct((B_loc, 1, HGRN_WIDTH), F32)
    return pl.pallas_call(
        body, name="hgrn_bwd", grid=(B_loc, N_HEAD_GROUPS, nc), in_specs=in_specs + [st_spec, blk],
        out_specs=[blk, blk, blk, blk, acc_spec, acc_spec], out_shape=[o, o, o, o, acc, acc],
        scratch_shapes=[pltpu.VMEM((HEADS_PER_STEP, HGRN_DIM, HGRN_DIM), F32), pltpu.VMEM((CHUNK, GROUP_WIDTH), F32),
                        pltpu.VMEM((CHUNK, GROUP_WIDTH), F32)],
        compiler_params=_params(("parallel", "parallel", "arbitrary")),
    )(proj, proj, proj, proj, lb, gain, states, dyh)


def _adamw(w, g, m, v, name):
    R, C = w.shape
    tr = _pick(R, (128, 64, 32, 16, 8)) if C > 1024 else _pick(R, (512, 256, 128, 64, 32, 16, 8))

    def body(w_ref, g_ref, m_ref, v_ref, d_ref, nm_ref, nv_ref):
        gv = g_ref[...]
        nm = ADAM_B1 * m_ref[...] + (1.0 - ADAM_B1) * gv
        nv = ADAM_B2 * v_ref[...] + (1.0 - ADAM_B2) * (gv * gv)
        m_hat = nm / (1.0 - ADAM_B1 ** ADAM_STEP)
        v_hat = nv / (1.0 - ADAM_B2 ** ADAM_STEP)
        d_ref[...] = -ADAM_LR * (m_hat / (jnp.sqrt(v_hat) + ADAM_EPS) + ADAM_WD * w_ref[...])
        nm_ref[...] = nm
        nv_ref[...] = nv

    blk = pl.BlockSpec((tr, C), lambda i: (i, 0))
    o = jax.ShapeDtypeStruct((R, C), F32)
    return pl.pallas_call(
        body, name=name, grid=(R // tr,), in_specs=[blk] * 4, out_specs=[blk] * 3, out_shape=[o, o, o],
        compiler_params=_params(("parallel",)),
    )(w, g, m, v)


ANY = pl.BlockSpec(memory_space=pl.ANY)
VMEM_SPEC = pl.BlockSpec(memory_space=pltpu.VMEM)


def _place():
    x, y, c = lax.axis_index("x"), lax.axis_index("y"), lax.axis_index("c")
    other_chips = [(1 - x, y), (x, 1 - y), (1 - x, 1 - y)]
    return x, y, c, other_chips


def _cast_into_full(w, ax, s_arr, name):
    R, C = w.shape
    tr = _pick(R, (256, 128))
    nr = R // tr

    def body(s_ref, w_ref, o_ref):
        o_ref[...] = w_ref[...].astype(BF16)

    if ax == 1:
        shape, o_map = (R, N_CHIPS * C), lambda i, s: (i, s[0])
    else:
        shape, o_map = (N_CHIPS * R, C), lambda i, s: (s[0] * nr + i, 0)
    return pl.pallas_call(
        body, name=name,
        grid_spec=pltpu.PrefetchScalarGridSpec(
            num_scalar_prefetch=1, grid=(nr,), in_specs=[pl.BlockSpec((tr, C), lambda i, s: (i, 0))],
            out_specs=pl.BlockSpec((tr, C), o_map)),
        out_shape=jax.ShapeDtypeStruct(shape, BF16),
        compiler_params=_params(("parallel",)),
    )(s_arr, w)


def _gather_weights(fulls, shard_shapes, axes):
    nw = len(fulls)

    def body(*refs):
        ins, outs = refs[:nw], refs[nw:2 * nw]
        send_sems, recv_sems = refs[2 * nw:]
        x, y, c, chips = _place()
        me, sibling, s = (x, y, c), (x, y, 1 - c), 2 * x + y

        def region(ref, i, t, half):
            R, C = shard_shapes[i]
            hr = R // 2
            if axes[i] == 1:
                return ref.at[pl.ds(half * hr, hr), pl.ds(pl.multiple_of(t * C, LANES), C)]
            return ref.at[pl.ds(t * R + half * hr, hr), :]

        def rcopy(i, k, src, dst, to):
            return pltpu.make_async_remote_copy(src_ref=src, dst_ref=dst, send_sem=send_sems.at[i * 6 + k],
                                                recv_sem=recv_sems.at[i * 6 + k], device_id=to, device_id_type=MESH)

        sent = []
        for i in range(nw):
            for j, chip in enumerate(chips):
                cp = rcopy(i, j, region(ins[i], i, s, c), region(outs[i], i, s, c), (*chip, c))
                cp.start()
                sent.append(cp)
        for i in range(nw):
            for j, chip in enumerate(chips):
                reg = region(outs[i], i, 2 * chip[0] + chip[1], c)
                rcopy(i, j, reg, reg, me).wait_recv()
                cp = rcopy(i, 3 + j, reg, reg, sibling)
                cp.start()
                sent.append(cp)
        for i in range(nw):
            for j, chip in enumerate(chips):
                reg = region(outs[i], i, 2 * chip[0] + chip[1], 1 - c)
                rcopy(i, 3 + j, reg, reg, me).wait_recv()
        for cp in sent:
            cp.wait_send()

    return pl.pallas_call(
        body, name="gather_weights", in_specs=[ANY] * nw, out_specs=[ANY] * nw,
        out_shape=[jax.ShapeDtypeStruct(f.shape, f.dtype) for f in fulls],
        input_output_aliases={i: i for i in range(nw)},
        scratch_shapes=[pltpu.SemaphoreType.DMA((6 * nw,)), pltpu.SemaphoreType.DMA((6 * nw,))],
    )(*fulls)


def _exchange_sibling_halves(gs):
    nw = len(gs)

    def body(*refs):
        ins, outs = refs[:nw], refs[nw:2 * nw]
        send_sems, recv_sems = refs[2 * nw:]
        x, y, c, _ = _place()
        cps = []
        for i in range(nw):
            cp = pltpu.make_async_remote_copy(src_ref=ins[i].at[:, 1 - c], dst_ref=outs[i], send_sem=send_sems.at[i],
                                              recv_sem=recv_sems.at[i], device_id=(x, y, 1 - c), device_id_type=MESH)
            cp.start()
            cps.append(cp)
        for cp in cps:
            cp.wait()

    return pl.pallas_call(
        body, name="grads_to_sibling", in_specs=[ANY] * nw, out_specs=[ANY] * nw,
        out_shape=[jax.ShapeDtypeStruct((g.shape[0],) + g.shape[2:], g.dtype) for g in gs],
        scratch_shapes=[pltpu.SemaphoreType.DMA((nw,)), pltpu.SemaphoreType.DMA((nw,))],
    )(*gs)


def _exchange_chips(ss, pack):
    nw = len(ss)
    rows = pack.shape[0]

    def body(*refs):
        ins, pack_ref = refs[:nw], refs[nw]
        outs, sum_ref = refs[nw + 1:2 * nw + 1], refs[2 * nw + 1]
        all_ref, send_sems, recv_sems = refs[2 * nw + 2:]
        x, y, c, chips = _place()
        me = 4 * x + 2 * y + c
        cps = []
        for i in range(nw):
            for j, chip in enumerate(chips):
                cp = pltpu.make_async_remote_copy(
                    src_ref=ins[i].at[2 * chip[0] + chip[1]], dst_ref=outs[i].at[j], send_sem=send_sems.at[i * 3 + j],
                    recv_sem=recv_sems.at[i * 3 + j], device_id=(*chip, c), device_id_type=MESH)
                cp.start()
                cps.append(cp)
        all_ref[me] = pack_ref[...]
        for k in range(1, N_DEV):
            to = (1 - x if k & 4 else x, 1 - y if k & 2 else y, 1 - c if k & 1 else c)
            cp = pltpu.make_async_remote_copy(
                src_ref=pack_ref, dst_ref=all_ref.at[me], send_sem=send_sems.at[nw * 3 + k - 1],
                recv_sem=recv_sems.at[nw * 3 + k - 1], device_id=to, device_id_type=MESH)
            cp.start()
            cps.append(cp)
        for cp in cps:
            cp.wait()
        total = all_ref[0]
        for d in range(1, N_DEV):
            total = total + all_ref[d]
        sum_ref[...] = total

    return pl.pallas_call(
        body, name="grads_to_chips", in_specs=[ANY] * nw + [VMEM_SPEC], out_specs=[ANY] * nw + [VMEM_SPEC],
        out_shape=[jax.ShapeDtypeStruct((N_CHIPS - 1,) + s.shape[1:], s.dtype) for s in ss]
        + [jax.ShapeDtypeStruct(pack.shape, F32)],
        scratch_shapes=[pltpu.VMEM((N_DEV, rows, LANES), F32), pltpu.SemaphoreType.DMA((nw * 3 + N_DEV - 1,)),
                        pltpu.SemaphoreType.DMA((nw * 3 + N_DEV - 1,))],
    )(*ss, pack)


def _share_with_sibling(fs):
    nw = len(fs)

    def body(*refs):
        ins, outs = refs[:nw], refs[nw:2 * nw]
        send_sems, recv_sems = refs[2 * nw:]
        x, y, c, _ = _place()
        cps = []
        for i in range(nw):
            cp = pltpu.make_async_remote_copy(src_ref=ins[i].at[c], dst_ref=outs[i].at[c], send_sem=send_sems.at[i],
                                              recv_sem=recv_sems.at[i], device_id=(x, y, 1 - c), device_id_type=MESH)
            cp.start()
            cps.append(cp)
        for cp in cps:
            cp.wait()

    return pl.pallas_call(
        body, name="grads_share_sibling", in_specs=[ANY] * nw, out_specs=[ANY] * nw,
        out_shape=[jax.ShapeDtypeStruct(f.shape, f.dtype) for f in fs],
        input_output_aliases={i: i for i in range(nw)},
        scratch_shapes=[pltpu.SemaphoreType.DMA((nw,)), pltpu.SemaphoreType.DMA((nw,))],
    )(*fs)


def _sum_sibling(g, land, c_arr, name):
    _, _, P, Q = g.shape
    tp = _pick(P, (256, 128, 64))

    def body(c_ref, g_ref, l_ref, s_ref):
        s_ref[...] = (g_ref[...].astype(F32) + l_ref[...].astype(F32)).astype(BF16)

    return pl.pallas_call(
        body, name=name,
        grid_spec=pltpu.PrefetchScalarGridSpec(
            num_scalar_prefetch=1, grid=(N_CHIPS, P // tp),
            in_specs=[pl.BlockSpec((None, None, tp, Q), lambda t, r, cr: (t, cr[0], r, 0)),
                      pl.BlockSpec((None, tp, Q), lambda t, r, cr: (t, r, 0))],
            out_specs=pl.BlockSpec((None, tp, Q), lambda t, r, cr: (t, r, 0))),
        out_shape=jax.ShapeDtypeStruct((N_CHIPS, P, Q), BF16),
        compiler_params=_params(("parallel", "parallel")),
    )(c_arr, g, land)


def _sum_chips(g, land, recv, sc_arr, name):
    _, _, P, Q = g.shape
    tp = _pick(P, (256, 128, 64))

    def body(sc_ref, g_ref, l_ref, r_ref, f_ref):
        acc = g_ref[...].astype(F32) + l_ref[...].astype(F32)
        for j in range(N_CHIPS - 1):
            acc = acc + r_ref[j].astype(F32)
        f_ref[...] = acc

    return pl.pallas_call(
        body, name=name,
        grid_spec=pltpu.PrefetchScalarGridSpec(
            num_scalar_prefetch=1, grid=(P // tp,),
            in_specs=[pl.BlockSpec((None, None, tp, Q), lambda r, sc: (sc[0], sc[1], r, 0)),
                      pl.BlockSpec((None, tp, Q), lambda r, sc: (sc[0], r, 0)),
                      pl.BlockSpec((N_CHIPS - 1, tp, Q), lambda r, sc: (0, r, 0))],
            out_specs=pl.BlockSpec((None, tp, Q), lambda r, sc: (sc[1], r, 0))),
        out_shape=jax.ShapeDtypeStruct((2, P, Q), F32),
        compiler_params=_params(("parallel",)),
    )(sc_arr, g, land, recv)


def _t5_bucket(dist):
    max_exact = REL_BUCKETS // 2
    d = jnp.maximum(dist, 0)
    df = jnp.maximum(d, 1).astype(F32)
    large = max_exact + (jnp.log(df / max_exact) / math.log(REL_MAX_DIST / max_exact)
                         * (REL_BUCKETS - max_exact)).astype(jnp.int32)
    large = jnp.minimum(large, REL_BUCKETS - 1)
    return jnp.where(d < max_exact, d, large)


def _bucket_table():
    qi = jnp.arange(WINDOW)[:, None]
    si = jnp.arange(2 * WINDOW)[None, :]
    return _t5_bucket(qi + WINDOW - si)


def _rows_of(a):
    flat = a.reshape(-1).astype(F32)
    n = -(-flat.shape[0] // LANES) * LANES
    return jnp.pad(flat, (0, n - flat.shape[0])).reshape(-1, LANES)


def _pack_rows(parts, total_rows):
    rows = [_rows_of(p) for p in parts]
    used = sum(r.shape[0] for r in rows)
    if total_rows > used:
        rows.append(jnp.zeros((total_rows - used, LANES), F32))
    return jnp.concatenate(rows, axis=0)


def _unpack_rows(packed, shapes):
    out, at = [], 0
    for shp in shapes:
        n = math.prod(shp)
        nr = -(-n // LANES)
        out.append(packed[at:at + nr].reshape(-1)[:n].reshape(shp))
        at += nr
    return out


def _round8(n):
    return -(-n // 8) * 8


def kernel(x, norm_pre, w_in, rel_bias, attn_sinks, lb_logits, hgrn_norm, w_branch_attn, w_branch_hgrn, w_out, norm_post, loss_target, m_norm_pre, m_w_in, m_rel_bias, m_attn_sinks, m_lb_logits, m_hgrn_norm, m_w_branch_attn, m_w_branch_hgrn, m_w_out, m_norm_post, v_norm_pre, v_w_in, v_rel_bias, v_attn_sinks, v_lb_logits, v_hgrn_norm, v_w_branch_attn, v_w_branch_hgrn, v_w_out, v_norm_post):
    B_loc, S, D = x.shape
    T = B_loc * S
    x2 = x.reshape(T, D)
    tgt2 = loss_target.reshape(T, D)
    my_x, my_y, my_c = lax.axis_index("x"), lax.axis_index("y"), lax.axis_index("c")

    c_arr = jnp.reshape(my_c, (1,)).astype(jnp.int32)
    s_arr = jnp.reshape(2 * my_x + my_y, (1,)).astype(jnp.int32)
    sc_arr = jnp.concatenate([s_arr, c_arr])
    shard_ws = [w_in[0], w_branch_attn[0], w_branch_hgrn[0], w_out[0]]
    shard_axes = (1, 1, 1, 0)
    names = ["w_in", "w_branch_attn", "w_branch_hgrn", "w_out"]
    placed = [_cast_into_full(w, ax, s_arr, "cast_" + nm) for w, ax, nm in zip(shard_ws, shard_axes, names)]
    win_f, wba_f, wbh_f, wout_f = _gather_weights(placed, [w.shape for w in shard_ws], shard_axes)

    buckets = _bucket_table()
    onehot = (buckets.reshape(-1)[:, None] == jnp.arange(REL_BUCKETS)[None, :]).astype(F32)
    bias_tab = jnp.dot(onehot, rel_bias.astype(F32), precision=lax.Precision.HIGHEST).T.reshape(ATTN_HEADS, WINDOW, 2 * WINDOW)
    sinks_b = jnp.broadcast_to(attn_sinks[0].astype(F32)[:, None, None], (ATTN_HEADS, 8, LANES))
    lb_fn = lambda l: jnp.cumsum(jax.nn.softmax(l.astype(F32), axis=0), axis=0)[:1]
    lb, lb_vjp = jax.vjp(lb_fn, lb_logits)
    gain_h = hgrn_norm[0].reshape(1, HGRN_WIDTH)

    h, rstd = _rmsnorm_fwd(x2, norm_pre)
    proj = _matmul(h, win_f, "nn", F32, "in_proj")
    ya = _attn_fwd(proj, bias_tab, sinks_b, B_loc, S)
    yh, states = _hgrn_fwd(proj, lb, gain_h, B_loc, S)
    ua = _matmul(ya, wba_f, "nn", F32, "branch_attn_proj")
    uh = _matmul(yh, wbh_f, "nn", F32, "branch_hgrn_proj")
    merged = _merge_fwd(proj, ua, uh)
    yv = _matmul(merged, wout_f, "nn", F32, "out_proj")
    dy, dout, loss_p, gnpost_p = _post_loss(yv, x2, tgt2, norm_post)

    dmerged = _matmul(dy, wout_f, "nt", F32, "out_proj_dgrad")
    g_wout = _matmul(merged, dy, "tn", BF16, "out_proj_wgrad")
    d_ua, d_uh, d_ga, d_gh = _merge_bwd(proj, ua, uh, dmerged)
    g_wba = _matmul(ya, d_ua, "tn", BF16, "branch_attn_wgrad", slabs=N_CHIPS)
    g_wbh = _matmul(yh, d_uh, "tn", BF16, "branch_hgrn_wgrad", slabs=N_CHIPS)
    d_ya = _matmul(d_ua, wba_f, "nt", BF16, "branch_attn_dgrad")
    d_yh = _matmul(d_uh, wbh_f, "nt", BF16, "branch_hgrn_dgrad")
    dq, dk, dv, dag, dbias_p, dsinks_p = _attn_bwd(proj, bias_tab, sinks_b, d_ya, B_loc, S)
    dhq, dhf, dhi, dhg, dlb_p, dgain_p = _hgrn_bwd(proj, states, lb, gain_h, d_yh, B_loc, S)
    dproj = jnp.concatenate([dq, dk, dv, dag, dhq, dhf, dhi, dhg, d_ga, d_gh], axis=1)
    g_win = _matmul(h, dproj, "tn", BF16, "in_proj_wgrad", slabs=N_CHIPS)
    dh = _matmul(dproj, win_f, "nt", F32, "in_proj_dgrad")
    grad_x2, gnpre_p = _rmsnorm_bwd(dh, x2, rstd, norm_pre, dout)

    grelb_p = jnp.dot(dbias_p.reshape(ATTN_HEADS, -1), onehot, precision=lax.Precision.HIGHEST).T
    gsinks_p = dsinks_p[:, 0, 0]
    dlb_sum = jnp.sum(dlb_p, axis=0).reshape(1, HGRN_WIDTH)
    ghn_p = jnp.sum(dgain_p, axis=0).reshape(HGRN_HEADS, HGRN_DIM)
    small_parts = [gnpre_p, gnpost_p, grelb_p, gsinks_p, dlb_sum, ghn_p, loss_p]
    small_shapes = [p.shape for p in small_parts]
    n_rows = _round8(sum(-(-math.prod(s) // LANES) for s in small_shapes))
    pack = _pack_rows(small_parts, n_rows)

    pieces = [g_win.reshape(N_CHIPS, 2, -1, g_win.shape[-1]), g_wba.reshape(N_CHIPS, 2, -1, g_wba.shape[-1]),
              g_wbh.reshape(N_CHIPS, 2, -1, g_wbh.shape[-1]), g_wout.reshape(N_CHIPS, 2, -1, g_wout.shape[-1])]
    lands = _exchange_sibling_halves(pieces)
    sums = [_sum_sibling(g, l, c_arr, "sum_sibling_" + nm) for g, l, nm in zip(pieces, lands, names)]
    *recvs, pack_sum = _exchange_chips(sums, pack)
    halves = [_sum_chips(g, l, r, sc_arr, "sum_chips_" + nm) for g, l, r, nm in zip(pieces, lands, recvs, names)]
    shared = _share_with_sibling(halves)
    big_w = [w_in, w_branch_attn, w_branch_hgrn, w_out]
    big_m = [m_w_in, m_w_branch_attn, m_w_branch_hgrn, m_w_out]
    big_v = [v_w_in, v_w_branch_attn, v_w_branch_hgrn, v_w_out]
    big = {}
    for nm, gs, w, m, v in zip(names, shared, big_w, big_m, big_v):
        shp = w.shape
        g2 = gs.reshape(shp[1], shp[2])
        d, nm_, nv_ = _adamw(w[0], g2, m[0], v[0], "adamw_" + nm)
        big[nm] = tuple(a.reshape(shp) for a in (g2, d, nm_, nv_))

    gnpre, gnpost, grelb, gsinks, dlb, ghn, loss = _unpack_rows(pack_sum, small_shapes)
    (g_lb_logits,) = lb_vjp(dlb)
    small_names = ["norm_pre", "rel_bias", "attn_sinks", "lb_logits", "hgrn_norm", "norm_post"]
    small_w = [norm_pre, rel_bias, attn_sinks, lb_logits, hgrn_norm, norm_post]
    small_m = [m_norm_pre, m_rel_bias, m_attn_sinks, m_lb_logits, m_hgrn_norm, m_norm_post]
    small_v = [v_norm_pre, v_rel_bias, v_attn_sinks, v_lb_logits, v_hgrn_norm, v_norm_post]
    small_g = [gnpre.reshape(norm_pre.shape), grelb.reshape(rel_bias.shape), gsinks.reshape(attn_sinks.shape),
               g_lb_logits.reshape(lb_logits.shape), ghn.reshape(hgrn_norm.shape), gnpost.reshape(norm_post.shape)]
    shapes = [w.shape for w in small_w]
    a_rows = _round8(sum(-(-math.prod(s) // LANES) for s in shapes))
    d_s, nm_s, nv_s = _adamw(_pack_rows(small_w, a_rows), _pack_rows(small_g, a_rows), _pack_rows(small_m, a_rows),
                             _pack_rows(small_v, a_rows), "adamw_small")
    small = {}
    for nm, g, d, m_, v_ in zip(small_names, small_g, _unpack_rows(d_s, shapes), _unpack_rows(nm_s, shapes),
                                _unpack_rows(nv_s, shapes)):
        small[nm] = (g, d, m_, v_)

    allw = {**big, **small}
    order = ["norm_pre", "w_in", "rel_bias", "attn_sinks", "lb_logits", "hgrn_norm", "w_branch_attn", "w_branch_hgrn",
             "w_out", "norm_post"]
    outs = [loss.reshape(()), grad_x2.reshape(B_loc, S, D)]
    for k in range(4):
        outs += [allw[nm][k] for nm in order]
    return tuple(outs)
```

```python
import functools
import math

import jax
import jax.numpy as jnp
from jax import lax
from jax.experimental import pallas as pl
from jax.experimental.pallas import tpu as pltpu

F32 = jnp.float32
BF16 = jnp.bfloat16
MESH = pl.DeviceIdType.MESH

ATTN_HEADS = 16
ATTN_KV_HEADS = 4
HEAD_DIM = 64
GROUP = ATTN_HEADS // ATTN_KV_HEADS
WINDOW = 128
ATTN_WIDTH = ATTN_HEADS * HEAD_DIM
KV_WIDTH = ATTN_KV_HEADS * HEAD_DIM
HGRN_HEADS = 8
HGRN_DIM = 128
HGRN_WIDTH = HGRN_HEADS * HGRN_DIM
CHUNK = 64
SUB = 16
NSUB = CHUNK // SUB
REL_BUCKETS = 32
REL_MAX_DIST = 128
NORM_EPS = 1e-6
ADAM_LR = 0.001
ADAM_B1 = 0.9
ADAM_B2 = 0.999
ADAM_EPS = 1e-08
ADAM_WD = 0.01
ADAM_STEP = 10
LANES = 128
N_CHIPS = 4
N_DEV = 8
VMEM_LIMIT = 48 * 1024 * 1024
MATMUL_OPERAND_BYTES = 20 * 1024 * 1024

OFF_AQ = 0
OFF_AK = OFF_AQ + ATTN_WIDTH
OFF_AV = OFF_AK + KV_WIDTH
OFF_AG = OFF_AV + KV_WIDTH
OFF_HQ = OFF_AG + ATTN_WIDTH
OFF_HF = OFF_HQ + HGRN_WIDTH
OFF_HI = OFF_HF + HGRN_WIDTH
OFF_HG = OFF_HI + HGRN_WIDTH
OFF_GA = OFF_HG + HGRN_WIDTH

NT_DIMS = (((1,), (1,)), ((), ()))
TN_DIMS = (((0,), (0,)), ((), ()))
NN_DIMS = (((1,), (0,)), ((), ()))


def _pick(n, cands):
    for c in cands:
        if n % c == 0:
            return c
    raise ValueError(f"no tile for {n} in {cands}")


def _params(sem):
    return pltpu.CompilerParams(dimension_semantics=sem, vmem_limit_bytes=VMEM_LIMIT)


def _bdot(a, b, dims=NN_DIMS):
    return lax.dot_general(a.astype(BF16), b.astype(BF16), dims, preferred_element_type=F32)


def _matmul(a, b, mode, out_dtype, name, slabs=1, after=None):
    if mode == "nn":
        (M, K), (K2, N) = a.shape, b.shape
    elif mode == "nt":
        (M, K), (N, K2) = a.shape, b.shape
    else:
        (K, M), (K2, N) = a.shape, b.shape
    assert K == K2
    nslab = N // slabs
    tm = _pick(M, (1024, 512, 256, 128))
    tn = _pick(nslab, (1024, 896, 512, 384, 256, 128))
    tk = _pick(K, [t for t in (4096, 2048, 1792, 1536, 1024, 512, 256, 128) if 4 * t * (tm + tn) <= MATMUL_OPERAND_BYTES])
    nk = K // tk
    per = nslab // tn
    dims = {"nn": NN_DIMS, "nt": NT_DIMS, "tn": TN_DIMS}[mode]

    n_in = 2 if after is None else 3

    def body(*refs):
        a_ref, b_ref, o_ref, acc = refs[0], refs[1], refs[n_in], refs[n_in + 1:]
        part = lax.dot_general(a_ref[...], b_ref[...], dims, preferred_element_type=F32)
        if nk == 1:
            o_ref[...] = part.astype(o_ref.dtype)
            return
        acc_ref, = acc
        k = pl.program_id(2)

        @pl.when(k == 0)
        def _():
            acc_ref[...] = part

        @pl.when((k > 0) & (k < nk - 1))
        def _():
            acc_ref[...] += part

        @pl.when(k == nk - 1)
        def _():
            o_ref[...] = (acc_ref[...] + part).astype(o_ref.dtype)

    if mode == "tn":
        a_spec = pl.BlockSpec((tk, tm), lambda i, j, k: (k, i))
    else:
        a_spec = pl.BlockSpec((tm, tk), lambda i, j, k: (i, k))
    if mode == "nt":
        b_spec = pl.BlockSpec((tn, tk), lambda i, j, k: (j, k))
    else:
        b_spec = pl.BlockSpec((tk, tn), lambda i, j, k: (k, j))
    if slabs == 1:
        o_shape = jax.ShapeDtypeStruct((M, N), out_dtype)
        o_spec = pl.BlockSpec((tm, tn), lambda i, j, k: (i, j))
    else:
        o_shape = jax.ShapeDtypeStruct((slabs, M, nslab), out_dtype)
        o_spec = pl.BlockSpec((None, tm, tn), lambda i, j, k: (j // per, i, j % per))
    return pl.pallas_call(
        body, name=name, grid=(M // tm, N // tn, nk), in_specs=[a_spec, b_spec] + ([] if after is None else [ANY]),
        out_specs=o_spec, out_shape=o_shape,
        scratch_shapes=[pltpu.VMEM((tm, tn), F32)] if nk > 1 else [],
        compiler_params=_params(("parallel", "parallel", "arbitrary")),
    )(*((a, b) if after is None else (a, b, after)))


def _rmsnorm_fwd(x2, gain):
    T, D = x2.shape
    tr = _pick(T, (256, 128))

    def body(x_ref, g_ref, h_ref, r_ref):
        xv = x_ref[...]
        r = lax.rsqrt(jnp.mean(xv * xv, axis=-1, keepdims=True) + NORM_EPS)
        h_ref[...] = (xv * r * g_ref[...]).astype(BF16)
        r_ref[...] = r

    return pl.pallas_call(
        body, name="rmsnorm_pre_fwd", grid=(T // tr,),
        in_specs=[pl.BlockSpec((tr, D), lambda i: (i, 0)), pl.BlockSpec((1, D), lambda i: (0, 0))],
        out_specs=[pl.BlockSpec((tr, D), lambda i: (i, 0)), pl.BlockSpec((tr, 1), lambda i: (i, 0))],
        out_shape=[jax.ShapeDtypeStruct((T, D), BF16), jax.ShapeDtypeStruct((T, 1), F32)],
        compiler_params=_params(("parallel",)),
    )(x2, gain)


def _rmsnorm_bwd(dh, x2, rstd, gain, dout):
    T, D = x2.shape
    tr = _pick(T, (256, 128))

    def body(dh_ref, x_ref, r_ref, g_ref, do_ref, gx_ref, gg_ref):
        @pl.when(pl.program_id(0) == 0)
        def _():
            gg_ref[...] = jnp.zeros_like(gg_ref)

        n = x_ref[...] * r_ref[...]
        dhv = dh_ref[...]
        dn = dhv * g_ref[...]
        gx_ref[...] = do_ref[...] + r_ref[...] * (dn - n * jnp.mean(dn * n, axis=-1, keepdims=True))
        gg_ref[...] += jnp.sum(dhv * n, axis=0, keepdims=True)

    row = pl.BlockSpec((tr, D), lambda i: (i, 0))
    vec = pl.BlockSpec((1, D), lambda i: (0, 0))
    return pl.pallas_call(
        body, name="rmsnorm_pre_bwd", grid=(T // tr,),
        in_specs=[row, row, pl.BlockSpec((tr, 1), lambda i: (i, 0)), vec, row],
        out_specs=[row, vec],
        out_shape=[jax.ShapeDtypeStruct((T, D), F32), jax.ShapeDtypeStruct((1, D), F32)],
        compiler_params=_params(("arbitrary",)),
    )(dh, x2, rstd, gain, dout)


def _post_loss(yv, x2, tgt2, gain):
    T, D = x2.shape
    tr = _pick(T, (256, 128))

    def body(y_ref, x_ref, t_ref, g_ref, dy_ref, do_ref, loss_ref, gg_ref):
        @pl.when(pl.program_id(0) == 0)
        def _():
            gg_ref[...] = jnp.zeros_like(gg_ref)
            loss_ref[...] = jnp.zeros_like(loss_ref)

        yv_ = y_ref[...]
        r = lax.rsqrt(jnp.mean(yv_ * yv_, axis=-1, keepdims=True) + NORM_EPS)
        n = yv_ * r
        e = (x_ref[...] + n * g_ref[...]) - t_ref[...]
        loss_ref[...] += 0.5 * jnp.sum(jnp.mean(e * e, axis=-1, keepdims=True), axis=0, keepdims=True)
        dz = e / D
        do_ref[...] = dz
        gg_ref[...] += jnp.sum(dz * n, axis=0, keepdims=True)
        dn = dz * g_ref[...]
        dy_ref[...] = (r * (dn - n * jnp.mean(dn * n, axis=-1, keepdims=True))).astype(BF16)

    row = pl.BlockSpec((tr, D), lambda i: (i, 0))
    vec = pl.BlockSpec((1, D), lambda i: (0, 0))
    return pl.pallas_call(
        body, name="post_norm_loss", grid=(T // tr,),
        in_specs=[row, row, row, vec],
        out_specs=[row, row, pl.BlockSpec((1, 1), lambda i: (0, 0)), vec],
        out_shape=[jax.ShapeDtypeStruct((T, D), BF16), jax.ShapeDtypeStruct((T, D), F32),
                   jax.ShapeDtypeStruct((1, 1), F32), jax.ShapeDtypeStruct((1, D), F32)],
        compiler_params=_params(("arbitrary",)),
    )(yv, x2, tgt2, gain)


def _merge_fwd(proj, ua, uh):
    T, D = ua.shape
    tr = _pick(T, (256, 128))
    bw = _pick(D, (512, 256))
    assert OFF_GA % bw == 0
    oa, oh = OFF_GA // bw, (OFF_GA + D) // bw

    def body(ga_ref, gh_ref, ua_ref, uh_ref, m_ref):
        m_ref[...] = (jax.nn.sigmoid(ga_ref[...]) * ua_ref[...] + jax.nn.sigmoid(gh_ref[...]) * uh_ref[...]).astype(BF16)

    blk = pl.BlockSpec((tr, bw), lambda i, j: (i, j))
    return pl.pallas_call(
        body, name="merge_fwd", grid=(T // tr, D // bw),
        in_specs=[pl.BlockSpec((tr, bw), lambda i, j: (i, oa + j)), pl.BlockSpec((tr, bw), lambda i, j: (i, oh + j)), blk, blk],
        out_specs=blk, out_shape=jax.ShapeDtypeStruct((T, D), BF16),
        compiler_params=_params(("parallel", "parallel")),
    )(proj, proj, ua, uh)


def _merge_bwd(proj, ua, uh, dm):
    T, D = ua.shape
    tr = _pick(T, (256, 128))
    bw = _pick(D, (512, 256))
    oa, oh = OFF_GA // bw, (OFF_GA + D) // bw

    def body(ga_ref, gh_ref, ua_ref, uh_ref, dm_ref, dua_ref, duh_ref, dga_ref, dgh_ref):
        sa = jax.nn.sigmoid(ga_ref[...])
        sh = jax.nn.sigmoid(gh_ref[...])
        d = dm_ref[...]
        dua_ref[...] = (d * sa).astype(BF16)
        duh_ref[...] = (d * sh).astype(BF16)
        dga_ref[...] = (d * ua_ref[...] * sa * (1.0 - sa)).astype(BF16)
        dgh_ref[...] = (d * uh_ref[...] * sh * (1.0 - sh)).astype(BF16)

    blk = pl.BlockSpec((tr, bw), lambda i, j: (i, j))
    o = jax.ShapeDtypeStruct((T, D), BF16)
    return pl.pallas_call(
        body, name="merge_bwd", grid=(T // tr, D // bw),
        in_specs=[pl.BlockSpec((tr, bw), lambda i, j: (i, oa + j)), pl.BlockSpec((tr, bw), lambda i, j: (i, oh + j)), blk, blk, blk],
        out_specs=[blk, blk, blk, blk], out_shape=[o, o, o, o],
        compiler_params=_params(("parallel", "parallel")),
    )(proj, proj, ua, uh, dm)


KV_PAIR = 2
PAIR_HEADS = KV_PAIR * GROUP


def _attn_mask(n):
    qi = lax.broadcasted_iota(jnp.int32, (WINDOW, 2 * WINDOW), 0)
    si = lax.broadcasted_iota(jnp.int32, (WINDOW, 2 * WINDOW), 1)
    dist = qi + WINDOW - si
    return (dist >= 0) & (dist < WINDOW) & ((si >= WINDOW) | (n > 0))


def _attn_group_fn(mask):
    def f(q4, k, v, ag4, bias4, sink4):
        kb = k.astype(BF16)
        vb = v.astype(BF16)
        outs = []
        for g in range(GROUP):
            s = lax.dot_general(q4[g].astype(BF16), kb, NT_DIMS, preferred_element_type=F32) * (HEAD_DIM ** -0.5)
            s = jnp.where(mask, s + bias4[g], -1e30)
            m = lax.stop_gradient(jnp.maximum(jnp.max(s, axis=-1, keepdims=True), sink4[g]))
            p = jnp.exp(s - m)
            den = jnp.sum(p, axis=-1, keepdims=True) + jnp.exp(sink4[g] - m)
            o = jnp.dot((p / den).astype(BF16), vb, preferred_element_type=F32)
            outs.append(o * jax.nn.silu(ag4[g]))
        return outs

    return f


def _attn_specs(B_loc, nb, order):
    qw = PAIR_HEADS * HEAD_DIM
    kw = KV_PAIR * HEAD_DIM

    def rows(g):
        b, p, n = order(*g)
        return b * nb + n

    def prev(g):
        b, p, n = order(*g)
        return b * nb + jnp.maximum(n - 1, 0)

    def pp(g):
        return order(*g)[1]

    q = pl.BlockSpec((WINDOW, qw), lambda *g: (rows(g), OFF_AQ // qw + pp(g)))
    kc = pl.BlockSpec((WINDOW, kw), lambda *g: (rows(g), OFF_AK // kw + pp(g)))
    kp = pl.BlockSpec((WINDOW, kw), lambda *g: (prev(g), OFF_AK // kw + pp(g)))
    vc = pl.BlockSpec((WINDOW, kw), lambda *g: (rows(g), OFF_AV // kw + pp(g)))
    vp = pl.BlockSpec((WINDOW, kw), lambda *g: (prev(g), OFF_AV // kw + pp(g)))
    ag = pl.BlockSpec((WINDOW, qw), lambda *g: (rows(g), OFF_AG // qw + pp(g)))
    bias = pl.BlockSpec((PAIR_HEADS, WINDOW, 2 * WINDOW), lambda *g: (pp(g), 0, 0))
    sink = pl.BlockSpec((PAIR_HEADS, 8, LANES), lambda *g: (pp(g), 0, 0))
    return [q, kc, kp, vc, vp, ag, bias, sink], rows, pp


def _attn_operands(q_ref, kc_ref, kp_ref, vc_ref, vp_ref, ag_ref, bias_ref, sink_ref, j):
    lo, hi = j * HEAD_DIM, (j + 1) * HEAD_DIM
    k = jnp.concatenate([kp_ref[:, lo:hi], kc_ref[:, lo:hi]], axis=0)
    v = jnp.concatenate([vp_ref[:, lo:hi], vc_ref[:, lo:hi]], axis=0)
    q4 = [q_ref[:, (j * GROUP + g) * HEAD_DIM:(j * GROUP + g + 1) * HEAD_DIM] for g in range(GROUP)]
    ag4 = [ag_ref[:, (j * GROUP + g) * HEAD_DIM:(j * GROUP + g + 1) * HEAD_DIM] for g in range(GROUP)]
    bias4 = [bias_ref[j * GROUP + g] for g in range(GROUP)]
    sink4 = [sink_ref[j * GROUP + g, 0:1, 0:1] for g in range(GROUP)]
    return q4, k, v, ag4, bias4, sink4


def _attn_fwd(proj, bias_tab, sinks_b, B_loc, S):
    T = B_loc * S
    nb = S // WINDOW
    n_pairs = ATTN_KV_HEADS // KV_PAIR
    in_specs, rows, pp = _attn_specs(B_loc, nb, lambda b, p, n: (b, p, n))

    def body(q_ref, kc_ref, kp_ref, vc_ref, vp_ref, ag_ref, bias_ref, sink_ref, ya_ref):
        f = _attn_group_fn(_attn_mask(pl.program_id(2)))
        for j in range(KV_PAIR):
            outs = f(*_attn_operands(q_ref, kc_ref, kp_ref, vc_ref, vp_ref, ag_ref, bias_ref, sink_ref, j))
            for g in range(GROUP):
                h = j * GROUP + g
                ya_ref[:, h * HEAD_DIM:(h + 1) * HEAD_DIM] = outs[g].astype(BF16)

    return pl.pallas_call(
        body, name="attn_fwd", grid=(B_loc, n_pairs, nb), in_specs=in_specs,
        out_specs=pl.BlockSpec((WINDOW, PAIR_HEADS * HEAD_DIM), lambda *g: (rows(g), pp(g))),
        out_shape=jax.ShapeDtypeStruct((T, ATTN_WIDTH), BF16),
        compiler_params=_params(("parallel", "parallel", "parallel")),
    )(proj, proj, proj, proj, proj, proj, bias_tab, sinks_b)


def _attn_bwd(proj, bias_tab, sinks_b, dya, B_loc, S):
    T = B_loc * S
    nb = S // WINDOW
    n_pairs = ATTN_KV_HEADS // KV_PAIR
    order = lambda p, b, i: (b, p, nb - 1 - i)
    in_specs, rows, pp = _attn_specs(B_loc, nb, order)
    qw = PAIR_HEADS * HEAD_DIM
    kw = KV_PAIR * HEAD_DIM

    def body(q_ref, kc_ref, kp_ref, vc_ref, vp_ref, ag_ref, bias_ref, sink_ref, dya_ref,
             dq_ref, dk_ref, dv_ref, dag_ref, dbias_ref, dsink_ref, dkc_ref, dvc_ref):
        b, i = pl.program_id(1), pl.program_id(2)
        n = nb - 1 - i

        @pl.when((b == 0) & (i == 0))
        def _():
            dbias_ref[...] = jnp.zeros_like(dbias_ref)
            dsink_ref[...] = jnp.zeros_like(dsink_ref)

        @pl.when(i == 0)
        def _():
            dkc_ref[...] = jnp.zeros_like(dkc_ref)
            dvc_ref[...] = jnp.zeros_like(dvc_ref)

        f = _attn_group_fn(_attn_mask(n))
        for j in range(KV_PAIR):
            ops = _attn_operands(q_ref, kc_ref, kp_ref, vc_ref, vp_ref, ag_ref, bias_ref, sink_ref, j)
            _, vjp = jax.vjp(f, *ops)
            douts = [dya_ref[:, (j * GROUP + g) * HEAD_DIM:(j * GROUP + g + 1) * HEAD_DIM].astype(F32) for g in range(GROUP)]
            dq4, dk, dv, dag4, dbias4, dsink4 = vjp(douts)
            lo, hi = j * HEAD_DIM, (j + 1) * HEAD_DIM
            dk_ref[:, lo:hi] = (dk[WINDOW:] + dkc_ref[:, lo:hi]).astype(BF16)
            dv_ref[:, lo:hi] = (dv[WINDOW:] + dvc_ref[:, lo:hi]).astype(BF16)
            dkc_ref[:, lo:hi] = dk[:WINDOW]
            dvc_ref[:, lo:hi] = dv[:WINDOW]
            for g in range(GROUP):
                h = j * GROUP + g
                dq_ref[:, h * HEAD_DIM:(h + 1) * HEAD_DIM] = dq4[g].astype(BF16)
                dag_ref[:, h * HEAD_DIM:(h + 1) * HEAD_DIM] = dag4[g].astype(BF16)
                dbias_ref[h] += dbias4[g]
                dsink_ref[h] += jnp.broadcast_to(dsink4[g], (8, LANES))

    wide = pl.BlockSpec((WINDOW, qw), lambda *g: (rows(g), pp(g)))
    narrow = pl.BlockSpec((WINDOW, kw), lambda *g: (rows(g), pp(g)))
    return pl.pallas_call(
        body, name="attn_bwd", grid=(n_pairs, B_loc, nb), in_specs=in_specs + [wide],
        out_specs=[wide, narrow, narrow, wide,
                   pl.BlockSpec((PAIR_HEADS, WINDOW, 2 * WINDOW), lambda *g: (pp(g), 0, 0)),
                   pl.BlockSpec((PAIR_HEADS, 8, LANES), lambda *g: (pp(g), 0, 0))],
        out_shape=[jax.ShapeDtypeStruct((T, ATTN_WIDTH), BF16), jax.ShapeDtypeStruct((T, KV_WIDTH), BF16),
                   jax.ShapeDtypeStruct((T, KV_WIDTH), BF16), jax.ShapeDtypeStruct((T, ATTN_WIDTH), BF16),
                   jax.ShapeDtypeStruct((ATTN_HEADS, WINDOW, 2 * WINDOW), F32),
                   jax.ShapeDtypeStruct((ATTN_HEADS, 8, LANES), F32)],
        scratch_shapes=[pltpu.VMEM((WINDOW, kw), F32), pltpu.VMEM((WINDOW, kw), F32)],
        compiler_params=_params(("arbitrary", "arbitrary", "arbitrary")),
    )(proj, proj, proj, proj, proj, proj, bias_tab, sinks_b, dya)


class _HgrnPre:
    def __init__(self, fr, qr, lb, g_scr):
        t = lax.broadcasted_iota(jnp.int32, (CHUNK, CHUNK), 0)
        s = lax.broadcasted_iota(jnp.int32, (CHUNK, CHUNK), 1)
        self.sg = jax.nn.sigmoid(fr)
        self.f = lb + (1.0 - lb) * self.sg
        g = jnp.dot((t >= s).astype(F32), jnp.log(self.f), precision=lax.Precision.HIGHEST, preferred_element_type=F32)
        g_scr[...] = g
        self.g = g
        self.row = lax.broadcasted_iota(jnp.int32, g.shape, 0)
        self.refs = [jnp.zeros((1, g.shape[1]), F32)] + [g_scr[pl.ds(i * SUB - 1, 1), :] for i in range(1, NSUB)]
        self.gend = g_scr[pl.ds(CHUNK - 1, 1), :]
        refrow = jnp.zeros_like(g)
        for i in range(1, NSUB):
            refrow = jnp.where(self.row >= i * SUB, self.refs[i], refrow)
        self.sigq = jax.nn.sigmoid(qr)
        self.qs = qr * self.sigq
        self.k = 1.0 - self.f
        self.eg = jnp.exp(g)
        self.eqd = jnp.exp(g - refrow)
        self.ekd = [jnp.exp(jnp.where(self.row < (i + 1) * SUB, self.refs[i] - g, 0.0)) for i in range(NSUB)]
        self.ekdec = jnp.exp(self.gend - g)
        self.qg = self.qs * self.eg
        self.qd = self.qs * self.eqd
        self.kd = [self.k * e for e in self.ekd]
        self.kdec = self.k * self.ekdec
        self.egend = jnp.exp(self.gend)


def _hgrn_pair_mask():
    t = lax.broadcasted_iota(jnp.int32, (CHUNK, NSUB * CHUNK), 0)
    col = lax.broadcasted_iota(jnp.int32, (CHUNK, NSUB * CHUNK), 1)
    return ((t // SUB) == (col // CHUNK)) & ((col % CHUNK) <= t)


def _hgrn_head_out(p, lanes, state_t, v, mask):
    qg, qd = p.qg[:, lanes], p.qd[:, lanes]
    kall = jnp.concatenate([kd[:, lanes] for kd in p.kd], axis=0)
    vst = jnp.concatenate([v] * NSUB, axis=0)
    am = jnp.where(mask, _bdot(qd, kall, NT_DIMS), 0.0)
    o = _bdot(qg, state_t, NT_DIMS) + _bdot(am, vst)
    return o, (qg, qd, kall, am, vst)


HEADS_PER_STEP = 4
GROUP_WIDTH = HEADS_PER_STEP * HGRN_DIM
N_HEAD_GROUPS = HGRN_HEADS // HEADS_PER_STEP


def _hgrn_specs(nc, order):
    def rows(g):
        b, grp, n = order(*g)
        return b * nc + n

    def grp_of(g):
        return order(*g)[1]

    def col(off):
        assert off % GROUP_WIDTH == 0
        return pl.BlockSpec((CHUNK, GROUP_WIDTH), lambda *g: (rows(g), off // GROUP_WIDTH + grp_of(g)))

    vec = pl.BlockSpec((1, GROUP_WIDTH), lambda *g: (0, grp_of(g)))
    return [col(OFF_HQ), col(OFF_HF), col(OFF_HI), col(OFF_HG), vec, vec], rows, grp_of


def _hgrn_fwd(proj, lb, gain, B_loc, S):
    T = B_loc * S
    nc = S // CHUNK
    in_specs, rows, grp_of = _hgrn_specs(nc, lambda b, grp, n: (b, grp, n))

    def body(q_ref, f_ref, v_ref, hg_ref, lb_ref, gain_ref, yh_ref, st_ref, state_scr, g_scr):
        @pl.when(pl.program_id(2) == 0)
        def _():
            state_scr[...] = jnp.zeros_like(state_scr)

        p = _HgrnPre(f_ref[...], q_ref[...], lb_ref[...], g_scr)
        v = v_ref[...]
        gate = gain_ref[...] * jax.nn.silu(hg_ref[...])
        mask = _hgrn_pair_mask()
        for hd in range(HEADS_PER_STEP):
            lanes = slice(hd * HGRN_DIM, (hd + 1) * HGRN_DIM)
            st = state_scr[hd]
            st_ref[hd] = st
            o, _ = _hgrn_head_out(p, lanes, st, v[:, lanes], mask)
            rs = lax.rsqrt(jnp.mean(o * o, axis=-1, keepdims=True) + NORM_EPS)
            yh_ref[:, lanes] = (o * rs * gate[:, lanes]).astype(BF16)
            state_scr[hd] = st * p.egend[:, lanes] + _bdot(v[:, lanes], p.kdec[:, lanes], TN_DIMS)

    return pl.pallas_call(
        body, name="hgrn_fwd", grid=(B_loc, N_HEAD_GROUPS, nc), in_specs=in_specs,
        out_specs=[pl.BlockSpec((CHUNK, GROUP_WIDTH), lambda *g: (rows(g), grp_of(g))),
                   pl.BlockSpec((None, None, HEADS_PER_STEP, HGRN_DIM, HGRN_DIM), lambda b, grp, n: (b, n, grp, 0, 0))],
        out_shape=[jax.ShapeDtypeStruct((T, HGRN_WIDTH), BF16),
                   jax.ShapeDtypeStruct((B_loc, nc, HGRN_HEADS, HGRN_DIM, HGRN_DIM), F32)],
        scratch_shapes=[pltpu.VMEM((HEADS_PER_STEP, HGRN_DIM, HGRN_DIM), F32), pltpu.VMEM((CHUNK, GROUP_WIDTH), F32)],
        compiler_params=_params(("parallel", "parallel", "arbitrary")),
    )(proj, proj, proj, proj, lb, gain)


def _hgrn_bwd(proj, states, lb, gain, dyh, B_loc, S):
    T = B_loc * S
    nc = S // CHUNK
    in_specs, rows, grp_of = _hgrn_specs(nc, lambda b, grp, i: (b, grp, nc - 1 - i))
    blk = pl.BlockSpec((CHUNK, GROUP_WIDTH), lambda *g: (rows(g), grp_of(g)))
    st_spec = pl.BlockSpec((None, None, HEADS_PER_STEP, HGRN_DIM, HGRN_DIM), lambda b, grp, i: (b, nc - 1 - i, grp, 0, 0))
    acc_spec = pl.BlockSpec((None, 1, GROUP_WIDTH), lambda b, grp, i: (b, 0, grp))

    def body(q_ref, f_ref, v_ref, hg_ref, lb_ref, gain_ref, st_ref, dyh_ref,
             dq_ref, df_ref, dv_ref, dhg_ref, dlb_ref, dgain_ref, dstate_scr, g_scr, dg_scr):
        @pl.when(pl.program_id(2) == 0)
        def _():
            dstate_scr[...] = jnp.zeros_like(dstate_scr)
            dlb_ref[...] = jnp.zeros_like(dlb_ref)
            dgain_ref[...] = jnp.zeros_like(dgain_ref)

        qr, lb, gain, hg, v = q_ref[...], lb_ref[...], gain_ref[...], hg_ref[...], v_ref[...]
        p = _HgrnPre(f_ref[...], qr, lb, g_scr)
        sgh = jax.nn.sigmoid(hg)
        sil = hg * sgh
        dy = dyh_ref[...].astype(F32)
        mask = _hgrn_pair_mask()
        dqg, dqd, dkdec, dv, dhg, dgend, dgain = [], [], [], [], [], [], []
        dkd = [[] for _ in range(NSUB)]
        for hd in range(HEADS_PER_STEP):
            lanes = slice(hd * HGRN_DIM, (hd + 1) * HGRN_DIM)
            st, dnew = st_ref[hd], dstate_scr[hd]
            o, (qg, qd, kall, am, vst) = _hgrn_head_out(p, lanes, st, v[:, lanes], mask)
            rs = lax.rsqrt(jnp.mean(o * o, axis=-1, keepdims=True) + NORM_EPS)
            n = o * rs
            dyn = dy[:, lanes] * n
            dgain.append(jnp.sum(dyn * sil[:, lanes], axis=0, keepdims=True))
            dhg.append(dyn * gain[:, lanes] * (sgh * (1.0 + hg * (1.0 - sgh)))[:, lanes])
            dn = dy[:, lanes] * gain[:, lanes] * sil[:, lanes]
            do = rs * (dn - n * jnp.mean(dn * n, axis=-1, keepdims=True))
            dqg.append(_bdot(do, st))
            dstate_scr[hd] = _bdot(do, qg, TN_DIMS) + dnew * p.egend[:, lanes]
            dr = jnp.where(mask, _bdot(do, vst, NT_DIMS), 0.0)
            dvst = _bdot(am, do, TN_DIMS)
            dv.append(sum(dvst[i * CHUNK:(i + 1) * CHUNK] for i in range(NSUB)) + _bdot(p.kdec[:, lanes], dnew, NT_DIMS))
            dqd.append(_bdot(dr, kall))
            dkall = _bdot(dr, qd, TN_DIMS)
            for i in range(NSUB):
                dkd[i].append(dkall[i * CHUNK:(i + 1) * CHUNK])
            dkdec_h = _bdot(v[:, lanes], dnew)
            dkdec.append(dkdec_h)
            dgend.append(jnp.sum(dkdec_h * p.kdec[:, lanes], axis=0, keepdims=True)
                         + jnp.sum(dnew * st, axis=0, keepdims=True) * p.egend[:, lanes])

        wide = lambda parts: jnp.concatenate(parts, axis=1)
        dqg, dqd, dkdec = wide(dqg), wide(dqd), wide(dkdec)
        t2 = dqd * p.qd
        dg = dqg * p.qg + t2 - dkdec * p.kdec
        dk = dkdec * p.ekdec
        dg_scr[...] = jnp.zeros_like(dg_scr)
        for i in range(NSUB):
            dkd_i = wide(dkd[i])
            tk = jnp.where(p.row < (i + 1) * SUB, dkd_i * p.kd[i], 0.0)
            dg = dg - tk
            dk = dk + dkd_i * p.ekd[i]
            if i >= 1:
                in_blk = (p.row >= i * SUB) & (p.row < (i + 1) * SUB)
                dg_scr[pl.ds(i * SUB - 1, 1), :] = (jnp.sum(tk, axis=0, keepdims=True)
                                                    - jnp.sum(jnp.where(in_blk, t2, 0.0), axis=0, keepdims=True))
        dg_scr[pl.ds(CHUNK - 1, 1), :] = wide(dgend)
        t = lax.broadcasted_iota(jnp.int32, (CHUNK, CHUNK), 0)
        s = lax.broadcasted_iota(jnp.int32, (CHUNK, CHUNK), 1)
        dlogf = jnp.dot((t <= s).astype(F32), dg + dg_scr[...], precision=lax.Precision.HIGHEST, preferred_element_type=F32)
        df = dlogf / p.f - dk
        df_ref[...] = (df * (1.0 - lb) * p.sg * (1.0 - p.sg)).astype(BF16)
        dlb_ref[...] += jnp.sum(df * (1.0 - p.sg), axis=0, keepdims=True)
        dq_ref[...] = ((dqg * p.eg + dqd * p.eqd) * p.sigq * (1.0 + qr * (1.0 - p.sigq))).astype(BF16)
        dv_ref[...] = wide(dv).astype(BF16)
        dhg_ref[...] = wide(dhg).astype(BF16)
        dgain_ref[...] += wide(dgain)

    o = jax.ShapeDtypeStruct((T, HGRN_WIDTH), BF16)
    acc = jax.ShapeDtypeStruct((B_loc, 1, HGRN_WIDTH), F32)
    return pl.pallas_call(
        body, name="hgrn_bwd", grid=(B_loc, N_HEAD_GROUPS, nc), in_specs=in_specs + [st_spec, blk],
        out_specs=[blk, blk, blk, blk, acc_spec, acc_spec], out_shape=[o, o, o, o, acc, acc],
        scratch_shapes=[pltpu.VMEM((HEADS_PER_STEP, HGRN_DIM, HGRN_DIM), F32), pltpu.VMEM((CHUNK, GROUP_WIDTH), F32),
                        pltpu.VMEM((CHUNK, GROUP_WIDTH), F32)],
        compiler_params=_params(("parallel", "parallel", "arbitrary")),
    )(proj, proj, proj, proj, lb, gain, states, dyh)


def _adamw(w, g, m, v, name):
    R, C = w.shape
    tr = _pick(R, (128, 64, 32, 16, 8)) if C > 1024 else _pick(R, (512, 256, 128, 64, 32, 16, 8))

    def body(w_ref, g_ref, m_ref, v_ref, d_ref, nm_ref, nv_ref):
        gv = g_ref[...]
        nm = ADAM_B1 * m_ref[...] + (1.0 - ADAM_B1) * gv
        nv = ADAM_B2 * v_ref[...] + (1.0 - ADAM_B2) * (gv * gv)
        m_hat = nm / (1.0 - ADAM_B1 ** ADAM_STEP)
        v_hat = nv / (1.0 - ADAM_B2 ** ADAM_STEP)
        d_ref[...] = -ADAM_LR * (m_hat / (jnp.sqrt(v_hat) + ADAM_EPS) + ADAM_WD * w_ref[...])
        nm_ref[...] = nm
        nv_ref[...] = nv

    blk = pl.BlockSpec((tr, C), lambda i: (i, 0))
    o = jax.ShapeDtypeStruct((R, C), F32)
    return pl.pallas_call(
        body, name=name, grid=(R // tr,), in_specs=[blk] * 4, out_specs=[blk] * 3, out_shape=[o, o, o],
        compiler_params=_params(("parallel",)),
    )(w, g, m, v)


ANY = pl.BlockSpec(memory_space=pl.ANY)
VMEM_SPEC = pl.BlockSpec(memory_space=pltpu.VMEM)


def _place():
    x, y, c = lax.axis_index("x"), lax.axis_index("y"), lax.axis_index("c")
    other_chips = [(1 - x, y), (x, 1 - y), (1 - x, 1 - y)]
    return x, y, c, other_chips


def _cast_into_full(w, ax, s_arr, name):
    R, C = w.shape
    tr = _pick(R, (256, 128))
    nr = R // tr

    def body(s_ref, w_ref, o_ref):
        o_ref[...] = w_ref[...].astype(BF16)

    if ax == 1:
        shape, o_map = (R, N_CHIPS * C), lambda i, s: (i, s[0])
    else:
        shape, o_map = (N_CHIPS * R, C), lambda i, s: (s[0] * nr + i, 0)
    return pl.pallas_call(
        body, name=name,
        grid_spec=pltpu.PrefetchScalarGridSpec(
            num_scalar_prefetch=1, grid=(nr,), in_specs=[pl.BlockSpec((tr, C), lambda i, s: (i, 0))],
            out_specs=pl.BlockSpec((tr, C), o_map)),
        out_shape=jax.ShapeDtypeStruct(shape, BF16),
        compiler_params=_params(("parallel",)),
    )(s_arr, w)


def _gather_weights(fulls, shard_shapes, axes):
    nw = len(fulls)

    def body(*refs):
        ins, outs = refs[:nw], refs[nw:2 * nw]
        send_sems, recv_sems = refs[2 * nw:]
        x, y, c, chips = _place()
        me, sibling, s = (x, y, c), (x, y, 1 - c), 2 * x + y

        def region(ref, i, t, half):
            R, C = shard_shapes[i]
            hr = R // 2
            if axes[i] == 1:
                return ref.at[pl.ds(half * hr, hr), pl.ds(pl.multiple_of(t * C, LANES), C)]
            return ref.at[pl.ds(t * R + half * hr, hr), :]

        def rcopy(i, k, src, dst, to):
            return pltpu.make_async_remote_copy(src_ref=src, dst_ref=dst, send_sem=send_sems.at[i * 6 + k],
                                                recv_sem=recv_sems.at[i * 6 + k], device_id=to, device_id_type=MESH)

        sent = []
        for i in range(nw):
            for j, chip in enumerate(chips):
                cp = rcopy(i, j, region(ins[i], i, s, c), region(outs[i], i, s, c), (*chip, c))
                cp.start()
                sent.append(cp)
        for i in range(nw):
            for j, chip in enumerate(chips):
                reg = region(outs[i], i, 2 * chip[0] + chip[1], c)
                rcopy(i, j, reg, reg, me).wait_recv()
                cp = rcopy(i, 3 + j, reg, reg, sibling)
                cp.start()
                sent.append(cp)
        for i in range(nw):
            for j, chip in enumerate(chips):
                reg = region(outs[i], i, 2 * chip[0] + chip[1], 1 - c)
                rcopy(i, 3 + j, reg, reg, me).wait_recv()
        for cp in sent:
            cp.wait_send()

    return pl.pallas_call(
        body, name="gather_weights", in_specs=[ANY] * nw, out_specs=[ANY] * nw,
        out_shape=[jax.ShapeDtypeStruct(f.shape, f.dtype) for f in fulls],
        input_output_aliases={i: i for i in range(nw)},
        scratch_shapes=[pltpu.SemaphoreType.DMA((6 * nw,)), pltpu.SemaphoreType.DMA((6 * nw,))],
    )(*fulls)


def _exchange_sibling_halves(gs, name):
    nw = len(gs)

    def body(*refs):
        ins, outs = refs[:nw], refs[nw:2 * nw]
        send_sems, recv_sems = refs[2 * nw:]
        x, y, c, _ = _place()
        cps = []
        for i in range(nw):
            cp = pltpu.make_async_remote_copy(src_ref=ins[i].at[:, 1 - c], dst_ref=outs[i], send_sem=send_sems.at[i],
                                              recv_sem=recv_sems.at[i], device_id=(x, y, 1 - c), device_id_type=MESH)
            cp.start()
            cps.append(cp)
        for cp in cps:
            cp.wait()

    return pl.pallas_call(
        body, name=name, in_specs=[ANY] * nw, out_specs=[ANY] * nw,
        out_shape=[jax.ShapeDtypeStruct((g.shape[0],) + g.shape[2:], g.dtype) for g in gs],
        scratch_shapes=[pltpu.SemaphoreType.DMA((nw,)), pltpu.SemaphoreType.DMA((nw,))],
    )(*gs)


HBM_SPEC = pl.BlockSpec(memory_space=pltpu.HBM)
SEM_SPEC = pl.BlockSpec(memory_space=pltpu.SEMAPHORE)
DATAFLOW = pltpu.SideEffectType.DATAFLOW_SIDE_EFFECTING


def _chip_copies(ins, lands, send_sems, recv_sems):
    x, y, c, chips = _place()
    return [pltpu.make_async_remote_copy(
        src_ref=ins[i].at[2 * chip[0] + chip[1]], dst_ref=lands[i].at[j], send_sem=send_sems.at[i * 3 + j],
        recv_sem=recv_sems.at[i * 3 + j], device_id=(*chip, c), device_id_type=MESH)
        for i in range(len(ins)) for j, chip in enumerate(chips)]


def _chips_send_start(ss, name):
    nw = len(ss)
    lands = [pltpu.with_memory_space_constraint(lax.empty((N_CHIPS - 1,) + s.shape[1:], s.dtype), pltpu.HBM) for s in ss]

    def body(*refs):
        ins, land_refs = refs[:nw], refs[nw:2 * nw]
        send_sems, recv_sems = refs[2 * nw], refs[2 * nw + 1]
        token = refs[-1]
        for cp in _chip_copies(ins, land_refs, send_sems, recv_sems):
            cp.start()
        token[...] = jnp.zeros_like(token)

    n = 3 * nw
    out = pl.pallas_call(
        body, name=name,
        out_shape=(pltpu.SemaphoreType.DMA((n,)), pltpu.SemaphoreType.DMA((n,)),
                   *[pltpu.HBM(s.shape, s.dtype) for s in ss], *[pltpu.HBM(l.shape, l.dtype) for l in lands],
                   jax.ShapeDtypeStruct((8, LANES), F32)),
        in_specs=[HBM_SPEC] * (2 * nw), out_specs=(SEM_SPEC, SEM_SPEC, *[HBM_SPEC] * (2 * nw), VMEM_SPEC),
        input_output_aliases={k: 2 + k for k in range(2 * nw)},
        compiler_params=pltpu.CompilerParams(has_side_effects=DATAFLOW),
    )(*[pltpu.with_memory_space_constraint(s, pltpu.HBM) for s in ss], *lands)
    return out[0], out[1], list(out[2:2 + nw]), list(out[2 + nw:2 + 2 * nw]), out[-1]


def _chips_send_wait(send_sems, recv_sems, ss, lands, after, name):
    nw = len(ss)

    def body(*refs):
        ins, land_refs = refs[:nw], refs[nw:2 * nw]
        s_sems, r_sems = refs[2 * nw], refs[2 * nw + 1]
        for cp in _chip_copies(ins, land_refs, s_sems, r_sems):
            cp.wait_send()
            cp.wait_recv()

    out = pl.pallas_call(
        body, name=name,
        out_shape=(*[pltpu.HBM(s.shape, s.dtype) for s in ss], *[pltpu.HBM(l.shape, l.dtype) for l in lands]),
        in_specs=[HBM_SPEC] * (2 * nw) + [SEM_SPEC, SEM_SPEC, ANY], out_specs=tuple([HBM_SPEC] * (2 * nw)),
        input_output_aliases={k: k for k in range(2 * nw)},
        compiler_params=pltpu.CompilerParams(has_side_effects=DATAFLOW),
    )(*ss, *lands, send_sems, recv_sems, after)
    return list(out[nw:])


def _sum_small(pack):
    rows = pack.shape[0]

    def body(pack_ref, sum_ref, all_ref, send_sems, recv_sems):
        x, y, c, _ = _place()
        me = 4 * x + 2 * y + c
        all_ref[me] = pack_ref[...]
        cps = []
        for k in range(1, N_DEV):
            to = (1 - x if k & 4 else x, 1 - y if k & 2 else y, 1 - c if k & 1 else c)
            cp = pltpu.make_async_remote_copy(
                src_ref=pack_ref, dst_ref=all_ref.at[me], send_sem=send_sems.at[k - 1],
                recv_sem=recv_sems.at[k - 1], device_id=to, device_id_type=MESH)
            cp.start()
            cps.append(cp)
        for cp in cps:
            cp.wait()
        total = all_ref[0]
        for d in range(1, N_DEV):
            total = total + all_ref[d]
        sum_ref[...] = total

    return pl.pallas_call(
        body, name="sum_small_grads", in_specs=[VMEM_SPEC], out_specs=VMEM_SPEC,
        out_shape=jax.ShapeDtypeStruct(pack.shape, F32),
        scratch_shapes=[pltpu.VMEM((N_DEV, rows, LANES), F32), pltpu.SemaphoreType.DMA((N_DEV - 1,)),
                        pltpu.SemaphoreType.DMA((N_DEV - 1,))],
    )(pack)


def _share_with_sibling(fs, name):
    nw = len(fs)

    def body(*refs):
        ins, outs = refs[:nw], refs[nw:2 * nw]
        send_sems, recv_sems = refs[2 * nw:]
        x, y, c, _ = _place()
        cps = []
        for i in range(nw):
            cp = pltpu.make_async_remote_copy(src_ref=ins[i].at[c], dst_ref=outs[i].at[c], send_sem=send_sems.at[i],
                                              recv_sem=recv_sems.at[i], device_id=(x, y, 1 - c), device_id_type=MESH)
            cp.start()
            cps.append(cp)
        for cp in cps:
            cp.wait()

    return pl.pallas_call(
        body, name=name, in_specs=[ANY] * nw, out_specs=[ANY] * nw,
        out_shape=[jax.ShapeDtypeStruct(f.shape, f.dtype) for f in fs],
        input_output_aliases={i: i for i in range(nw)},
        scratch_shapes=[pltpu.SemaphoreType.DMA((nw,)), pltpu.SemaphoreType.DMA((nw,))],
    )(*fs)


def _sum_sibling(g, land, c_arr, name):
    _, _, P, Q = g.shape
    tp = _pick(P, (256, 128, 64))

    def body(c_ref, g_ref, l_ref, s_ref):
        s_ref[...] = (g_ref[...].astype(F32) + l_ref[...].astype(F32)).astype(BF16)

    return pl.pallas_call(
        body, name=name,
        grid_spec=pltpu.PrefetchScalarGridSpec(
            num_scalar_prefetch=1, grid=(N_CHIPS, P // tp),
            in_specs=[pl.BlockSpec((None, None, tp, Q), lambda t, r, cr: (t, cr[0], r, 0)),
                      pl.BlockSpec((None, tp, Q), lambda t, r, cr: (t, r, 0))],
            out_specs=pl.BlockSpec((None, tp, Q), lambda t, r, cr: (t, r, 0))),
        out_shape=jax.ShapeDtypeStruct((N_CHIPS, P, Q), BF16),
        compiler_params=_params(("parallel", "parallel")),
    )(c_arr, g, land)


def _sum_chips(g, land, recv, sc_arr, name):
    _, _, P, Q = g.shape
    tp = _pick(P, (256, 128, 64))

    def body(sc_ref, g_ref, l_ref, r_ref, f_ref):
        acc = g_ref[...].astype(F32) + l_ref[...].astype(F32)
        for j in range(N_CHIPS - 1):
            acc = acc + r_ref[j].astype(F32)
        f_ref[...] = acc

    return pl.pallas_call(
        body, name=name,
        grid_spec=pltpu.PrefetchScalarGridSpec(
            num_scalar_prefetch=1, grid=(P // tp,),
            in_specs=[pl.BlockSpec((None, None, tp, Q), lambda r, sc: (sc[0], sc[1], r, 0)),
                      pl.BlockSpec((None, tp, Q), lambda r, sc: (sc[0], r, 0)),
                      pl.BlockSpec((N_CHIPS - 1, tp, Q), lambda r, sc: (0, r, 0))],
            out_specs=pl.BlockSpec((None, tp, Q), lambda r, sc: (sc[1], r, 0))),
        out_shape=jax.ShapeDtypeStruct((2, P, Q), F32),
        compiler_params=_params(("parallel",)),
    )(sc_arr, g, land, recv)


class _Reduction:
    def __init__(self, names, pieces, lands, flight):
        self.names, self.pieces, self.lands = names, pieces, lands
        self.send_sems, self.recv_sems, self.sums, self.zones, self.token = flight


def _reduce_start(grads, names, c_arr):
    pieces = [g.reshape(N_CHIPS, 2, -1, g.shape[-1]) for g in grads]
    tag = names[0] if len(names) == 1 else "branches"
    lands = _exchange_sibling_halves(pieces, "grads_to_sibling_" + tag)
    sums = [_sum_sibling(g, l, c_arr, "sum_sibling_" + nm) for g, l, nm in zip(pieces, lands, names)]
    return _Reduction(names, pieces, lands, _chips_send_start(sums, "grads_to_chips_start_" + tag))


def _reduce_finish(red, after, sc_arr):
    tag = red.names[0] if len(red.names) == 1 else "branches"
    recvs = _chips_send_wait(red.send_sems, red.recv_sems, red.sums, red.zones, after, "grads_to_chips_wait_" + tag)
    halves = [_sum_chips(g, l, r, sc_arr, "sum_chips_" + nm) for g, l, r, nm in zip(red.pieces, red.lands, recvs, red.names)]
    return _share_with_sibling(halves, "grads_share_sibling_" + tag)


def _t5_bucket(dist):
    max_exact = REL_BUCKETS // 2
    d = jnp.maximum(dist, 0)
    df = jnp.maximum(d, 1).astype(F32)
    large = max_exact + (jnp.log(df / max_exact) / math.log(REL_MAX_DIST / max_exact)
                         * (REL_BUCKETS - max_exact)).astype(jnp.int32)
    large = jnp.minimum(large, REL_BUCKETS - 1)
    return jnp.where(d < max_exact, d, large)


def _bucket_table():
    qi = jnp.arange(WINDOW)[:, None]
    si = jnp.arange(2 * WINDOW)[None, :]
    return _t5_bucket(qi + WINDOW - si)


TILE_WORDS = 8 * LANES


def _tile_rows(shape):
    return -(-math.prod(shape) // TILE_WORDS) * 8


def _rows_of(a):
    flat = a.reshape(-1).astype(F32)
    n = _tile_rows(a.shape) * LANES
    return jnp.pad(flat, (0, n - flat.shape[0])).reshape(-1, LANES)


def _pack_rows(parts):
    return jnp.concatenate([_rows_of(p) for p in parts], axis=0)


def _unpack_rows(packed, shapes):
    out, at = [], 0
    for shp in shapes:
        n, nr = math.prod(shp), _tile_rows(shp)
        out.append(packed[at:at + nr].reshape(-1)[:n].reshape(shp))
        at += nr
    return out


def kernel(x, norm_pre, w_in, rel_bias, attn_sinks, lb_logits, hgrn_norm, w_branch_attn, w_branch_hgrn, w_out, norm_post, loss_target, m_norm_pre, m_w_in, m_rel_bias, m_attn_sinks, m_lb_logits, m_hgrn_norm, m_w_branch_attn, m_w_branch_hgrn, m_w_out, m_norm_post, v_norm_pre, v_w_in, v_rel_bias, v_attn_sinks, v_lb_logits, v_hgrn_norm, v_w_branch_attn, v_w_branch_hgrn, v_w_out, v_norm_post):
    B_loc, S, D = x.shape
    T = B_loc * S
    x2 = x.reshape(T, D)
    tgt2 = loss_target.reshape(T, D)
    my_x, my_y, my_c = lax.axis_index("x"), lax.axis_index("y"), lax.axis_index("c")

    c_arr = jnp.reshape(my_c, (1,)).astype(jnp.int32)
    s_arr = jnp.reshape(2 * my_x + my_y, (1,)).astype(jnp.int32)
    sc_arr = jnp.concatenate([s_arr, c_arr])
    shard_ws = [w_in[0], w_branch_attn[0], w_branch_hgrn[0], w_out[0]]
    shard_axes = (1, 1, 1, 0)
    names = ["w_in", "w_branch_attn", "w_branch_hgrn", "w_out"]
    placed = [_cast_into_full(w, ax, s_arr, "cast_" + nm) for w, ax, nm in zip(shard_ws, shard_axes, names)]
    win_f, wba_f, wbh_f, wout_f = _gather_weights(placed, [w.shape for w in shard_ws], shard_axes)

    buckets = _bucket_table()
    onehot = (buckets.reshape(-1)[:, None] == jnp.arange(REL_BUCKETS)[None, :]).astype(F32)
    bias_tab = jnp.dot(onehot, rel_bias.astype(F32), precision=lax.Precision.HIGHEST).T.reshape(ATTN_HEADS, WINDOW, 2 * WINDOW)
    sinks_b = jnp.broadcast_to(attn_sinks[0].astype(F32)[:, None, None], (ATTN_HEADS, 8, LANES))
    lb_fn = lambda l: jnp.cumsum(jax.nn.softmax(l.astype(F32), axis=0), axis=0)[:1]
    lb, lb_vjp = jax.vjp(lb_fn, lb_logits)
    gain_h = hgrn_norm[0].reshape(1, HGRN_WIDTH)

    h, rstd = _rmsnorm_fwd(x2, norm_pre)
    proj = _matmul(h, win_f, "nn", F32, "in_proj")
    ya = _attn_fwd(proj, bias_tab, sinks_b, B_loc, S)
    yh, states = _hgrn_fwd(proj, lb, gain_h, B_loc, S)
    ua = _matmul(ya, wba_f, "nn", F32, "branch_attn_proj")
    uh = _matmul(yh, wbh_f, "nn", F32, "branch_hgrn_proj")
    merged = _merge_fwd(proj, ua, uh)
    yv = _matmul(merged, wout_f, "nn", F32, "out_proj")
    dy, dout, loss_p, gnpost_p = _post_loss(yv, x2, tgt2, norm_post)

    dmerged = _matmul(dy, wout_f, "nt", F32, "out_proj_dgrad")
    g_wout = _matmul(merged, dy, "tn", BF16, "out_proj_wgrad")
    d_ua, d_uh, d_ga, d_gh = _merge_bwd(proj, ua, uh, dmerged)
    g_wba = _matmul(ya, d_ua, "tn", BF16, "branch_attn_wgrad", slabs=N_CHIPS)
    g_wbh = _matmul(yh, d_uh, "tn", BF16, "branch_hgrn_wgrad", slabs=N_CHIPS)
    late = _reduce_start([g_wba, g_wbh, g_wout], names[1:], c_arr)
    d_ya = _matmul(d_ua, wba_f, "nt", BF16, "branch_attn_dgrad", after=late.token)
    d_yh = _matmul(d_uh, wbh_f, "nt", BF16, "branch_hgrn_dgrad", after=late.token)
    dq, dk, dv, dag, dbias_p, dsinks_p = _attn_bwd(proj, bias_tab, sinks_b, d_ya, B_loc, S)
    dhq, dhf, dhi, dhg, dlb_p, dgain_p = _hgrn_bwd(proj, states, lb, gain_h, d_yh, B_loc, S)
    dproj = jnp.concatenate([dq, dk, dv, dag, dhq, dhf, dhi, dhg, d_ga, d_gh], axis=1)
    g_win = _matmul(h, dproj, "tn", BF16, "in_proj_wgrad", slabs=N_CHIPS)
    last = _reduce_start([g_win], names[:1], c_arr)
    dh = _matmul(dproj, win_f, "nt", F32, "in_proj_dgrad", after=last.token)
    grad_x2, gnpre_p = _rmsnorm_bwd(dh, x2, rstd, norm_pre, dout)
    shared = _reduce_finish(last, grad_x2, sc_arr) + _reduce_finish(late, grad_x2, sc_arr)

    grelb_p = jnp.dot(dbias_p.reshape(ATTN_HEADS, -1), onehot, precision=lax.Precision.HIGHEST).T
    gsinks_p = dsinks_p[:, 0, 0]
    dlb_sum = jnp.sum(dlb_p, axis=0).reshape(1, HGRN_WIDTH)
    ghn_p = jnp.sum(dgain_p, axis=0).reshape(HGRN_HEADS, HGRN_DIM)
    small_parts = [gnpre_p, gnpost_p, grelb_p, gsinks_p, dlb_sum, ghn_p, loss_p]
    small_shapes = [p.shape for p in small_parts]
    pack_sum = _sum_small(_pack_rows(small_parts))
    big_w = [w_in, w_branch_attn, w_branch_hgrn, w_out]
    big_m = [m_w_in, m_w_branch_attn, m_w_branch_hgrn, m_w_out]
    big_v = [v_w_in, v_w_branch_attn, v_w_branch_hgrn, v_w_out]
    big = {}
    for nm, gs, w, m, v in zip(names, shared, big_w, big_m, big_v):
        shp = w.shape
        g2 = gs.reshape(shp[1], shp[2])
        d, nm_, nv_ = _adamw(w[0], g2, m[0], v[0], "adamw_" + nm)
        big[nm] = tuple(a.reshape(shp) for a in (g2, d, nm_, nv_))

    gnpre, gnpost, grelb, gsinks, dlb, ghn, loss = _unpack_rows(pack_sum, small_shapes)
    (g_lb_logits,) = lb_vjp(dlb)
    small_names = ["norm_pre", "rel_bias", "attn_sinks", "lb_logits", "hgrn_norm", "norm_post"]
    small_w = [norm_pre, rel_bias, attn_sinks, lb_logits, hgrn_norm, norm_post]
    small_m = [m_norm_pre, m_rel_bias, m_attn_sinks, m_lb_logits, m_hgrn_norm, m_norm_post]
    small_v = [v_norm_pre, v_rel_bias, v_attn_sinks, v_lb_logits, v_hgrn_norm, v_norm_post]
    small_g = [gnpre.reshape(norm_pre.shape), grelb.reshape(rel_bias.shape), gsinks.reshape(attn_sinks.shape),
               g_lb_logits.reshape(lb_logits.shape), ghn.reshape(hgrn_norm.shape), gnpost.reshape(norm_post.shape)]
    shapes = [w.shape for w in small_w]
    d_s, nm_s, nv_s = _adamw(_pack_rows(small_w), _pack_rows(small_g), _pack_rows(small_m), _pack_rows(small_v),
                             "adamw_small")
    small = {}
    for nm, g, d, m_, v_ in zip(small_names, small_g, _unpack_rows(d_s, shapes), _unpack_rows(nm_s, shapes),
                                _unpack_rows(nv_s, shapes)):
        small[nm] = (g, d, m_, v_)

    allw = {**big, **small}
    order = ["norm_pre", "w_in", "rel_bias", "attn_sinks", "lb_logits", "hgrn_norm", "w_branch_attn", "w_branch_hgrn",
             "w_out", "norm_post"]
    outs = [loss.reshape(()), grad_x2.reshape(B_loc, S, D)]
    for k in range(4):
        outs += [allw[nm][k] for nm in order]
    return tuple(outs)
```

```python
import functools
import math

import jax
import jax.numpy as jnp
from jax import lax
from jax.experimental import pallas as pl
from jax.experimental.pallas import tpu as pltpu

F32 = jnp.float32
BF16 = jnp.bfloat16
MESH = pl.DeviceIdType.MESH

ATTN_HEADS = 16
ATTN_KV_HEADS = 4
HEAD_DIM = 64
GROUP = ATTN_HEADS // ATTN_KV_HEADS
WINDOW = 128
ATTN_WIDTH = ATTN_HEADS * HEAD_DIM
KV_WIDTH = ATTN_KV_HEADS * HEAD_DIM
HGRN_HEADS = 8
HGRN_DIM = 128
HGRN_WIDTH = HGRN_HEADS * HGRN_DIM
CHUNK = 64
SUB = 16
NSUB = CHUNK // SUB
REL_BUCKETS = 32
REL_MAX_DIST = 128
NORM_EPS = 1e-6
ADAM_LR = 0.001
ADAM_B1 = 0.9
ADAM_B2 = 0.999
ADAM_EPS = 1e-08
ADAM_WD = 0.01
ADAM_STEP = 10
LANES = 128
N_CHIPS = 4
N_DEV = 8
VMEM_LIMIT = 48 * 1024 * 1024
MATMUL_OPERAND_BYTES = 20 * 1024 * 1024

OFF_AQ = 0
OFF_AK = OFF_AQ + ATTN_WIDTH
OFF_AV = OFF_AK + KV_WIDTH
OFF_AG = OFF_AV + KV_WIDTH
OFF_HQ = OFF_AG + ATTN_WIDTH
OFF_HF = OFF_HQ + HGRN_WIDTH
OFF_HI = OFF_HF + HGRN_WIDTH
OFF_HG = OFF_HI + HGRN_WIDTH
OFF_GA = OFF_HG + HGRN_WIDTH

NT_DIMS = (((1,), (1,)), ((), ()))
TN_DIMS = (((0,), (0,)), ((), ()))
NN_DIMS = (((1,), (0,)), ((), ()))


def _pick(n, cands):
    for c in cands:
        if n % c == 0:
            return c
    raise ValueError(f"no tile for {n} in {cands}")


def _params(sem):
    return pltpu.CompilerParams(dimension_semantics=sem, vmem_limit_bytes=VMEM_LIMIT)


def _bdot(a, b, dims=NN_DIMS):
    return lax.dot_general(a.astype(BF16), b.astype(BF16), dims, preferred_element_type=F32)


def _matmul(a, b, mode, out_dtype, name, slabs=1, after=None):
    if mode == "nn":
        (M, K), (K2, N) = a.shape, b.shape
    elif mode == "nt":
        (M, K), (N, K2) = a.shape, b.shape
    else:
        (K, M), (K2, N) = a.shape, b.shape
    assert K == K2
    nslab = N // slabs
    tm = _pick(M, (1024, 512, 256, 128))
    tn = _pick(nslab, (1024, 896, 512, 384, 256, 128))
    tk = _pick(K, [t for t in (4096, 2048, 1792, 1536, 1024, 512, 256, 128) if 4 * t * (tm + tn) <= MATMUL_OPERAND_BYTES])
    nk = K // tk
    per = nslab // tn
    dims = {"nn": NN_DIMS, "nt": NT_DIMS, "tn": TN_DIMS}[mode]

    n_in = 2 if after is None else 3

    def body(*refs):
        a_ref, b_ref, o_ref, acc = refs[0], refs[1], refs[n_in], refs[n_in + 1:]
        part = lax.dot_general(a_ref[...], b_ref[...], dims, preferred_element_type=F32)
        if nk == 1:
            o_ref[...] = part.astype(o_ref.dtype)
            return
        acc_ref, = acc
        k = pl.program_id(2)

        @pl.when(k == 0)
        def _():
            acc_ref[...] = part

        @pl.when((k > 0) & (k < nk - 1))
        def _():
            acc_ref[...] += part

        @pl.when(k == nk - 1)
        def _():
            o_ref[...] = (acc_ref[...] + part).astype(o_ref.dtype)

    if mode == "tn":
        a_spec = pl.BlockSpec((tk, tm), lambda i, j, k: (k, i))
    else:
        a_spec = pl.BlockSpec((tm, tk), lambda i, j, k: (i, k))
    if mode == "nt":
        b_spec = pl.BlockSpec((tn, tk), lambda i, j, k: (j, k))
    else:
        b_spec = pl.BlockSpec((tk, tn), lambda i, j, k: (k, j))
    if slabs == 1:
        o_shape = jax.ShapeDtypeStruct((M, N), out_dtype)
        o_spec = pl.BlockSpec((tm, tn), lambda i, j, k: (i, j))
    else:
        o_shape = jax.ShapeDtypeStruct((slabs, M, nslab), out_dtype)
        o_spec = pl.BlockSpec((None, tm, tn), lambda i, j, k: (j // per, i, j % per))
    return pl.pallas_call(
        body, name=name, grid=(M // tm, N // tn, nk), in_specs=[a_spec, b_spec] + ([] if after is None else [ANY]),
        out_specs=o_spec, out_shape=o_shape,
        scratch_shapes=[pltpu.VMEM((tm, tn), F32)] if nk > 1 else [],
        compiler_params=_params(("parallel", "parallel", "arbitrary")),
    )(*((a, b) if after is None else (a, b, after)))


def _rmsnorm_fwd(x2, gain):
    T, D = x2.shape
    tr = _pick(T, (256, 128))

    def body(x_ref, g_ref, h_ref, r_ref):
        xv = x_ref[...]
        r = lax.rsqrt(jnp.mean(xv * xv, axis=-1, keepdims=True) + NORM_EPS)
        h_ref[...] = (xv * r * g_ref[...]).astype(BF16)
        r_ref[...] = r

    return pl.pallas_call(
        body, name="rmsnorm_pre_fwd", grid=(T // tr,),
        in_specs=[pl.BlockSpec((tr, D), lambda i: (i, 0)), pl.BlockSpec((1, D), lambda i: (0, 0))],
        out_specs=[pl.BlockSpec((tr, D), lambda i: (i, 0)), pl.BlockSpec((tr, 1), lambda i: (i, 0))],
        out_shape=[jax.ShapeDtypeStruct((T, D), BF16), jax.ShapeDtypeStruct((T, 1), F32)],
        compiler_params=_params(("parallel",)),
    )(x2, gain)


def _rmsnorm_bwd(dh, x2, rstd, gain, dout):
    T, D = x2.shape
    tr = _pick(T, (256, 128))

    def body(dh_ref, x_ref, r_ref, g_ref, do_ref, gx_ref, gg_ref):
        @pl.when(pl.program_id(0) == 0)
        def _():
            gg_ref[...] = jnp.zeros_like(gg_ref)

        n = x_ref[...] * r_ref[...]
        dhv = dh_ref[...]
        dn = dhv * g_ref[...]
        gx_ref[...] = do_ref[...] + r_ref[...] * (dn - n * jnp.mean(dn * n, axis=-1, keepdims=True))
        gg_ref[...] += jnp.sum(dhv * n, axis=0, keepdims=True)

    row = pl.BlockSpec((tr, D), lambda i: (i, 0))
    vec = pl.BlockSpec((1, D), lambda i: (0, 0))
    return pl.pallas_call(
        body, name="rmsnorm_pre_bwd", grid=(T // tr,),
        in_specs=[row, row, pl.BlockSpec((tr, 1), lambda i: (i, 0)), vec, row],
        out_specs=[row, vec],
        out_shape=[jax.ShapeDtypeStruct((T, D), F32), jax.ShapeDtypeStruct((1, D), F32)],
        compiler_params=_params(("arbitrary",)),
    )(dh, x2, rstd, gain, dout)


def _post_loss(yv, x2, tgt2, gain):
    T, D = x2.shape
    tr = _pick(T, (256, 128))

    def body(y_ref, x_ref, t_ref, g_ref, dy_ref, do_ref, loss_ref, gg_ref):
        @pl.when(pl.program_id(0) == 0)
        def _():
            gg_ref[...] = jnp.zeros_like(gg_ref)
            loss_ref[...] = jnp.zeros_like(loss_ref)

        yv_ = y_ref[...]
        r = lax.rsqrt(jnp.mean(yv_ * yv_, axis=-1, keepdims=True) + NORM_EPS)
        n = yv_ * r
        e = (x_ref[...] + n * g_ref[...]) - t_ref[...]
        loss_ref[...] += 0.5 * jnp.sum(jnp.mean(e * e, axis=-1, keepdims=True), axis=0, keepdims=True)
        dz = e / D
        do_ref[...] = dz
        gg_ref[...] += jnp.sum(dz * n, axis=0, keepdims=True)
        dn = dz * g_ref[...]
        dy_ref[...] = (r * (dn - n * jnp.mean(dn * n, axis=-1, keepdims=True))).astype(BF16)

    row = pl.BlockSpec((tr, D), lambda i: (i, 0))
    vec = pl.BlockSpec((1, D), lambda i: (0, 0))
    return pl.pallas_call(
        body, name="post_norm_loss", grid=(T // tr,),
        in_specs=[row, row, row, vec],
        out_specs=[row, row, pl.BlockSpec((1, 1), lambda i: (0, 0)), vec],
        out_shape=[jax.ShapeDtypeStruct((T, D), BF16), jax.ShapeDtypeStruct((T, D), F32),
                   jax.ShapeDtypeStruct((1, 1), F32), jax.ShapeDtypeStruct((1, D), F32)],
        compiler_params=_params(("arbitrary",)),
    )(yv, x2, tgt2, gain)


def _merge_fwd(proj, ua, uh):
    T, D = ua.shape
    tr = _pick(T, (256, 128))
    bw = _pick(D, (512, 256))
    assert OFF_GA % bw == 0
    oa, oh = OFF_GA // bw, (OFF_GA + D) // bw

    def body(ga_ref, gh_ref, ua_ref, uh_ref, m_ref):
        m_ref[...] = (jax.nn.sigmoid(ga_ref[...]) * ua_ref[...] + jax.nn.sigmoid(gh_ref[...]) * uh_ref[...]).astype(BF16)

    blk = pl.BlockSpec((tr, bw), lambda i, j: (i, j))
    return pl.pallas_call(
        body, name="merge_fwd", grid=(T // tr, D // bw),
        in_specs=[pl.BlockSpec((tr, bw), lambda i, j: (i, oa + j)), pl.BlockSpec((tr, bw), lambda i, j: (i, oh + j)), blk, blk],
        out_specs=blk, out_shape=jax.ShapeDtypeStruct((T, D), BF16),
        compiler_params=_params(("parallel", "parallel")),
    )(proj, proj, ua, uh)


def _merge_bwd(proj, ua, uh, dm):
    T, D = ua.shape
    tr = _pick(T, (256, 128))
    bw = _pick(D, (512, 256))
    oa, oh = OFF_GA // bw, (OFF_GA + D) // bw

    def body(ga_ref, gh_ref, ua_ref, uh_ref, dm_ref, dua_ref, duh_ref, dga_ref, dgh_ref):
        sa = jax.nn.sigmoid(ga_ref[...])
        sh = jax.nn.sigmoid(gh_ref[...])
        d = dm_ref[...]
        dua_ref[...] = (d * sa).astype(BF16)
        duh_ref[...] = (d * sh).astype(BF16)
        dga_ref[...] = (d * ua_ref[...] * sa * (1.0 - sa)).astype(BF16)
        dgh_ref[...] = (d * uh_ref[...] * sh * (1.0 - sh)).astype(BF16)

    blk = pl.BlockSpec((tr, bw), lambda i, j: (i, j))
    o = jax.ShapeDtypeStruct((T, D), BF16)
    return pl.pallas_call(
        body, name="merge_bwd", grid=(T // tr, D // bw),
        in_specs=[pl.BlockSpec((tr, bw), lambda i, j: (i, oa + j)), pl.BlockSpec((tr, bw), lambda i, j: (i, oh + j)), blk, blk, blk],
        out_specs=[blk, blk, blk, blk], out_shape=[o, o, o, o],
        compiler_params=_params(("parallel", "parallel")),
    )(proj, proj, ua, uh, dm)


KV_PAIR = 2
PAIR_HEADS = KV_PAIR * GROUP


def _attn_mask(n):
    qi = lax.broadcasted_iota(jnp.int32, (WINDOW, 2 * WINDOW), 0)
    si = lax.broadcasted_iota(jnp.int32, (WINDOW, 2 * WINDOW), 1)
    dist = qi + WINDOW - si
    return (dist >= 0) & (dist < WINDOW) & ((si >= WINDOW) | (n > 0))


def _attn_group_fn(mask):
    def f(q4, k, v, ag4, bias4, sink4):
        kb = k.astype(BF16)
        vb = v.astype(BF16)
        outs = []
        for g in range(GROUP):
            s = lax.dot_general(q4[g].astype(BF16), kb, NT_DIMS, preferred_element_type=F32) * (HEAD_DIM ** -0.5)
            s = jnp.where(mask, s + bias4[g], -1e30)
            m = lax.stop_gradient(jnp.maximum(jnp.max(s, axis=-1, keepdims=True), sink4[g]))
            p = jnp.exp(s - m)
            den = jnp.sum(p, axis=-1, keepdims=True) + jnp.exp(sink4[g] - m)
            o = jnp.dot((p / den).astype(BF16), vb, preferred_element_type=F32)
            outs.append(o * jax.nn.silu(ag4[g]))
        return outs

    return f


def _attn_specs(B_loc, nb, order):
    qw = PAIR_HEADS * HEAD_DIM
    kw = KV_PAIR * HEAD_DIM

    def rows(g):
        b, p, n = order(*g)
        return b * nb + n

    def prev(g):
        b, p, n = order(*g)
        return b * nb + jnp.maximum(n - 1, 0)

    def pp(g):
        return order(*g)[1]

    q = pl.BlockSpec((WINDOW, qw), lambda *g: (rows(g), OFF_AQ // qw + pp(g)))
    kc = pl.BlockSpec((WINDOW, kw), lambda *g: (rows(g), OFF_AK // kw + pp(g)))
    kp = pl.BlockSpec((WINDOW, kw), lambda *g: (prev(g), OFF_AK // kw + pp(g)))
    vc = pl.BlockSpec((WINDOW, kw), lambda *g: (rows(g), OFF_AV // kw + pp(g)))
    vp = pl.BlockSpec((WINDOW, kw), lambda *g: (prev(g), OFF_AV // kw + pp(g)))
    ag = pl.BlockSpec((WINDOW, qw), lambda *g: (rows(g), OFF_AG // qw + pp(g)))
    bias = pl.BlockSpec((PAIR_HEADS, WINDOW, 2 * WINDOW), lambda *g: (pp(g), 0, 0))
    sink = pl.BlockSpec((PAIR_HEADS, 8, LANES), lambda *g: (pp(g), 0, 0))
    return [q, kc, kp, vc, vp, ag, bias, sink], rows, pp


def _attn_operands(q_ref, kc_ref, kp_ref, vc_ref, vp_ref, ag_ref, bias_ref, sink_ref, j):
    lo, hi = j * HEAD_DIM, (j + 1) * HEAD_DIM
    k = jnp.concatenate([kp_ref[:, lo:hi], kc_ref[:, lo:hi]], axis=0)
    v = jnp.concatenate([vp_ref[:, lo:hi], vc_ref[:, lo:hi]], axis=0)
    q4 = [q_ref[:, (j * GROUP + g) * HEAD_DIM:(j * GROUP + g + 1) * HEAD_DIM] for g in range(GROUP)]
    ag4 = [ag_ref[:, (j * GROUP + g) * HEAD_DIM:(j * GROUP + g + 1) * HEAD_DIM] for g in range(GROUP)]
    bias4 = [bias_ref[j * GROUP + g] for g in range(GROUP)]
    sink4 = [sink_ref[j * GROUP + g, 0:1, 0:1] for g in range(GROUP)]
    return q4, k, v, ag4, bias4, sink4


def _attn_fwd(proj, bias_tab, sinks_b, B_loc, S):
    T = B_loc * S
    nb = S // WINDOW
    n_pairs = ATTN_KV_HEADS // KV_PAIR
    in_specs, rows, pp = _attn_specs(B_loc, nb, lambda b, p, n: (b, p, n))

    def body(q_ref, kc_ref, kp_ref, vc_ref, vp_ref, ag_ref, bias_ref, sink_ref, ya_ref):
        f = _attn_group_fn(_attn_mask(pl.program_id(2)))
        for j in range(KV_PAIR):
            outs = f(*_attn_operands(q_ref, kc_ref, kp_ref, vc_ref, vp_ref, ag_ref, bias_ref, sink_ref, j))
            for g in range(GROUP):
                h = j * GROUP + g
                ya_ref[:, h * HEAD_DIM:(h + 1) * HEAD_DIM] = outs[g].astype(BF16)

    return pl.pallas_call(
        body, name="attn_fwd", grid=(B_loc, n_pairs, nb), in_specs=in_specs,
        out_specs=pl.BlockSpec((WINDOW, PAIR_HEADS * HEAD_DIM), lambda *g: (rows(g), pp(g))),
        out_shape=jax.ShapeDtypeStruct((T, ATTN_WIDTH), BF16),
        compiler_params=_params(("parallel", "parallel", "parallel")),
    )(proj, proj, proj, proj, proj, proj, bias_tab, sinks_b)


def _attn_bwd(proj, bias_tab, sinks_b, dya, B_loc, S):
    T = B_loc * S
    nb = S // WINDOW
    n_pairs = ATTN_KV_HEADS // KV_PAIR
    order = lambda p, b, i: (b, p, nb - 1 - i)
    in_specs, rows, pp = _attn_specs(B_loc, nb, order)
    qw = PAIR_HEADS * HEAD_DIM
    kw = KV_PAIR * HEAD_DIM

    def body(q_ref, kc_ref, kp_ref, vc_ref, vp_ref, ag_ref, bias_ref, sink_ref, dya_ref,
             dq_ref, dk_ref, dv_ref, dag_ref, dbias_ref, dsink_ref, dkc_ref, dvc_ref):
        b, i = pl.program_id(1), pl.program_id(2)
        n = nb - 1 - i

        @pl.when((b == 0) & (i == 0))
        def _():
            dbias_ref[...] = jnp.zeros_like(dbias_ref)
            dsink_ref[...] = jnp.zeros_like(dsink_ref)

        @pl.when(i == 0)
        def _():
            dkc_ref[...] = jnp.zeros_like(dkc_ref)
            dvc_ref[...] = jnp.zeros_like(dvc_ref)

        f = _attn_group_fn(_attn_mask(n))
        for j in range(KV_PAIR):
            ops = _attn_operands(q_ref, kc_ref, kp_ref, vc_ref, vp_ref, ag_ref, bias_ref, sink_ref, j)
            _, vjp = jax.vjp(f, *ops)
            douts = [dya_ref[:, (j * GROUP + g) * HEAD_DIM:(j * GROUP + g + 1) * HEAD_DIM].astype(F32) for g in range(GROUP)]
            dq4, dk, dv, dag4, dbias4, dsink4 = vjp(douts)
            lo, hi = j * HEAD_DIM, (j + 1) * HEAD_DIM
            dk_ref[:, lo:hi] = (dk[WINDOW:] + dkc_ref[:, lo:hi]).astype(BF16)
            dv_ref[:, lo:hi] = (dv[WINDOW:] + dvc_ref[:, lo:hi]).astype(BF16)
            dkc_ref[:, lo:hi] = dk[:WINDOW]
            dvc_ref[:, lo:hi] = dv[:WINDOW]
            for g in range(GROUP):
                h = j * GROUP + g
                dq_ref[:, h * HEAD_DIM:(h + 1) * HEAD_DIM] = dq4[g].astype(BF16)
                dag_ref[:, h * HEAD_DIM:(h + 1) * HEAD_DIM] = dag4[g].astype(BF16)
                dbias_ref[h] += dbias4[g]
                dsink_ref[h] += jnp.broadcast_to(dsink4[g], (8, LANES))

    wide = pl.BlockSpec((WINDOW, qw), lambda *g: (rows(g), pp(g)))
    narrow = pl.BlockSpec((WINDOW, kw), lambda *g: (rows(g), pp(g)))
    return pl.pallas_call(
        body, name="attn_bwd", grid=(n_pairs, B_loc, nb), in_specs=in_specs + [wide],
        out_specs=[wide, narrow, narrow, wide,
                   pl.BlockSpec((PAIR_HEADS, WINDOW, 2 * WINDOW), lambda *g: (pp(g), 0, 0)),
                   pl.BlockSpec((PAIR_HEADS, 8, LANES), lambda *g: (pp(g), 0, 0))],
        out_shape=[jax.ShapeDtypeStruct((T, ATTN_WIDTH), BF16), jax.ShapeDtypeStruct((T, KV_WIDTH), BF16),
                   jax.ShapeDtypeStruct((T, KV_WIDTH), BF16), jax.ShapeDtypeStruct((T, ATTN_WIDTH), BF16),
                   jax.ShapeDtypeStruct((ATTN_HEADS, WINDOW, 2 * WINDOW), F32),
                   jax.ShapeDtypeStruct((ATTN_HEADS, 8, LANES), F32)],
        scratch_shapes=[pltpu.VMEM((WINDOW, kw), F32), pltpu.VMEM((WINDOW, kw), F32)],
        compiler_params=_params(("arbitrary", "arbitrary", "arbitrary")),
    )(proj, proj, proj, proj, proj, proj, bias_tab, sinks_b, dya)


class _HgrnPre:
    def __init__(self, fr, qr, lb, g_scr):
        t = lax.broadcasted_iota(jnp.int32, (CHUNK, CHUNK), 0)
        s = lax.broadcasted_iota(jnp.int32, (CHUNK, CHUNK), 1)
        self.sg = jax.nn.sigmoid(fr)
        self.f = lb + (1.0 - lb) * self.sg
        g = jnp.dot((t >= s).astype(F32), jnp.log(self.f), precision=lax.Precision.HIGHEST, preferred_element_type=F32)
        g_scr[...] = g
        self.g = g
        self.row = lax.broadcasted_iota(jnp.int32, g.shape, 0)
        self.refs = [jnp.zeros((1, g.shape[1]), F32)] + [g_scr[pl.ds(i * SUB - 1, 1), :] for i in range(1, NSUB)]
        self.gend = g_scr[pl.ds(CHUNK - 1, 1), :]
        refrow = jnp.zeros_like(g)
        for i in range(1, NSUB):
            refrow = jnp.where(self.row >= i * SUB, self.refs[i], refrow)
        self.sigq = jax.nn.sigmoid(qr)
        self.qs = qr * self.sigq
        self.k = 1.0 - self.f
        self.eg = jnp.exp(g)
        self.eqd = jnp.exp(g - refrow)
        self.ekd = [jnp.exp(jnp.where(self.row < (i + 1) * SUB, self.refs[i] - g, 0.0)) for i in range(NSUB)]
        self.ekdec = jnp.exp(self.gend - g)
        self.qg = self.qs * self.eg
        self.qd = self.qs * self.eqd
        self.kd = [self.k * e for e in self.ekd]
        self.kdec = self.k * self.ekdec
        self.egend = jnp.exp(self.gend)


def _hgrn_pair_mask():
    t = lax.broadcasted_iota(jnp.int32, (CHUNK, NSUB * CHUNK), 0)
    col = lax.broadcasted_iota(jnp.int32, (CHUNK, NSUB * CHUNK), 1)
    return ((t // SUB) == (col // CHUNK)) & ((col % CHUNK) <= t)


def _hgrn_head_out(p, lanes, state_t, v, mask):
    qg, qd = p.qg[:, lanes], p.qd[:, lanes]
    kall = jnp.concatenate([kd[:, lanes] for kd in p.kd], axis=0)
    vst = jnp.concatenate([v] * NSUB, axis=0)
    am = jnp.where(mask, _bdot(qd, kall, NT_DIMS), 0.0)
    o = _bdot(qg, state_t, NT_DIMS) + _bdot(am, vst)
    return o, (qg, qd, kall, am, vst)


HEADS_PER_STEP = 4
GROUP_WIDTH = HEADS_PER_STEP * HGRN_DIM
N_HEAD_GROUPS = HGRN_HEADS // HEADS_PER_STEP


def _hgrn_specs(nc, order):
    def rows(g):
        b, grp, n = order(*g)
        return b * nc + n

    def grp_of(g):
        return order(*g)[1]

    def col(off):
        assert off % GROUP_WIDTH == 0
        return pl.BlockSpec((CHUNK, GROUP_WIDTH), lambda *g: (rows(g), off // GROUP_WIDTH + grp_of(g)))

    vec = pl.BlockSpec((1, GROUP_WIDTH), lambda *g: (0, grp_of(g)))
    return [col(OFF_HQ), col(OFF_HF), col(OFF_HI), col(OFF_HG), vec, vec], rows, grp_of


def _hgrn_fwd(proj, lb, gain, B_loc, S):
    T = B_loc * S
    nc = S // CHUNK
    in_specs, rows, grp_of = _hgrn_specs(nc, lambda b, grp, n: (b, grp, n))

    def body(q_ref, f_ref, v_ref, hg_ref, lb_ref, gain_ref, yh_ref, st_ref, state_scr, g_scr):
        @pl.when(pl.program_id(2) == 0)
        def _():
            state_scr[...] = jnp.zeros_like(state_scr)

        p = _HgrnPre(f_ref[...], q_ref[...], lb_ref[...], g_scr)
        v = v_ref[...]
        gate = gain_ref[...] * jax.nn.silu(hg_ref[...])
        mask = _hgrn_pair_mask()
        for hd in range(HEADS_PER_STEP):
            lanes = slice(hd * HGRN_DIM, (hd + 1) * HGRN_DIM)
            st = state_scr[hd]
            st_ref[hd] = st
            o, _ = _hgrn_head_out(p, lanes, st, v[:, lanes], mask)
            rs = lax.rsqrt(jnp.mean(o * o, axis=-1, keepdims=True) + NORM_EPS)
            yh_ref[:, lanes] = (o * rs * gate[:, lanes]).astype(BF16)
            state_scr[hd] = st * p.egend[:, lanes] + _bdot(v[:, lanes], p.kdec[:, lanes], TN_DIMS)

    return pl.pallas_call(
        body, name="hgrn_fwd", grid=(B_loc, N_HEAD_GROUPS, nc), in_specs=in_specs,
        out_specs=[pl.BlockSpec((CHUNK, GROUP_WIDTH), lambda *g: (rows(g), grp_of(g))),
                   pl.BlockSpec((None, None, HEADS_PER_STEP, HGRN_DIM, HGRN_DIM), lambda b, grp, n: (b, n, grp, 0, 0))],
        out_shape=[jax.ShapeDtypeStruct((T, HGRN_WIDTH), BF16),
                   jax.ShapeDtypeStruct((B_loc, nc, HGRN_HEADS, HGRN_DIM, HGRN_DIM), F32)],
        scratch_shapes=[pltpu.VMEM((HEADS_PER_STEP, HGRN_DIM, HGRN_DIM), F32), pltpu.VMEM((CHUNK, GROUP_WIDTH), F32)],
        compiler_params=_params(("parallel", "parallel", "arbitrary")),
    )(proj, proj, proj, proj, lb, gain)


def _hgrn_bwd(proj, states, lb, gain, dyh, B_loc, S):
    T = B_loc * S
    nc = S // CHUNK
    in_specs, rows, grp_of = _hgrn_specs(nc, lambda b, grp, i: (b, grp, nc - 1 - i))
    blk = pl.BlockSpec((CHUNK, GROUP_WIDTH), lambda *g: (rows(g), grp_of(g)))
    st_spec = pl.BlockSpec((None, None, HEADS_PER_STEP, HGRN_DIM, HGRN_DIM), lambda b, grp, i: (b, nc - 1 - i, grp, 0, 0))
    acc_spec = pl.BlockSpec((None, 1, GROUP_WIDTH), lambda b, grp, i: (b, 0, grp))

    def body(q_ref, f_ref, v_ref, hg_ref, lb_ref, gain_ref, st_ref, dyh_ref,
             dq_ref, df_ref, dv_ref, dhg_ref, dlb_ref, dgain_ref, dstate_scr, g_scr, dg_scr):
        @pl.when(pl.program_id(2) == 0)
        def _():
            dstate_scr[...] = jnp.zeros_like(dstate_scr)
            dlb_ref[...] = jnp.zeros_like(dlb_ref)
            dgain_ref[...] = jnp.zeros_like(dgain_ref)

        qr, lb, gain, hg, v = q_ref[...], lb_ref[...], gain_ref[...], hg_ref[...], v_ref[...]
        p = _HgrnPre(f_ref[...], qr, lb, g_scr)
        sgh = jax.nn.sigmoid(hg)
        sil = hg * sgh
        dy = dyh_ref[...].astype(F32)
        mask = _hgrn_pair_mask()
        dqg, dqd, dkdec, dv, dhg, dgend, dgain = [], [], [], [], [], [], []
        dkd = [[] for _ in range(NSUB)]
        for hd in range(HEADS_PER_STEP):
            lanes = slice(hd * HGRN_DIM, (hd + 1) * HGRN_DIM)
            st, dnew = st_ref[hd], dstate_scr[hd]
            o, (qg, qd, kall, am, vst) = _hgrn_head_out(p, lanes, st, v[:, lanes], mask)
            rs = lax.rsqrt(jnp.mean(o * o, axis=-1, keepdims=True) + NORM_EPS)
            n = o * rs
            dyn = dy[:, lanes] * n
            dgain.append(jnp.sum(dyn * sil[:, lanes], axis=0, keepdims=True))
            dhg.append(dyn * gain[:, lanes] * (sgh * (1.0 + hg * (1.0 - sgh)))[:, lanes])
            dn = dy[:, lanes] * gain[:, lanes] * sil[:, lanes]
            do = rs * (dn - n * jnp.mean(dn * n, axis=-1, keepdims=True))
            dqg.append(_bdot(do, st))
            dstate_scr[hd] = _bdot(do, qg, TN_DIMS) + dnew * p.egend[:, lanes]
            dr = jnp.where(mask, _bdot(do, vst, NT_DIMS), 0.0)
            dvst = _bdot(am, do, TN_DIMS)
            dv.append(sum(dvst[i * CHUNK:(i + 1) * CHUNK] for i in range(NSUB)) + _bdot(p.kdec[:, lanes], dnew, NT_DIMS))
            dqd.append(_bdot(dr, kall))
            dkall = _bdot(dr, qd, TN_DIMS)
            for i in range(NSUB):
                dkd[i].append(dkall[i * CHUNK:(i + 1) * CHUNK])
            dkdec_h = _bdot(v[:, lanes], dnew)
            dkdec.append(dkdec_h)
            dgend.append(jnp.sum(dkdec_h * p.kdec[:, lanes], axis=0, keepdims=True)
                         + jnp.sum(dnew * st, axis=0, keepdims=True) * p.egend[:, lanes])

        wide = lambda parts: jnp.concatenate(parts, axis=1)
        dqg, dqd, dkdec = wide(dqg), wide(dqd), wide(dkdec)
        t2 = dqd * p.qd
        dg = dqg * p.qg + t2 - dkdec * p.kdec
        dk = dkdec * p.ekdec
        dg_scr[...] = jnp.zeros_like(dg_scr)
        for i in range(NSUB):
            dkd_i = wide(dkd[i])
            tk = jnp.where(p.row < (i + 1) * SUB, dkd_i * p.kd[i], 0.0)
            dg = dg - tk
            dk = dk + dkd_i * p.ekd[i]
            if i >= 1:
                in_blk = (p.row >= i * SUB) & (p.row < (i + 1) * SUB)
                dg_scr[pl.ds(i * SUB - 1, 1), :] = (jnp.sum(tk, axis=0, keepdims=True)
                                                    - jnp.sum(jnp.where(in_blk, t2, 0.0), axis=0, keepdims=True))
        dg_scr[pl.ds(CHUNK - 1, 1), :] = wide(dgend)
        t = lax.broadcasted_iota(jnp.int32, (CHUNK, CHUNK), 0)
        s = lax.broadcasted_iota(jnp.int32, (CHUNK, CHUNK), 1)
        dlogf = jnp.dot((t <= s).astype(F32), dg + dg_scr[...], precision=lax.Precision.HIGHEST, preferred_element_type=F32)
        df = dlogf / p.f - dk
        df_ref[...] = (df * (1.0 - lb) * p.sg * (1.0 - p.sg)).astype(BF16)
        dlb_ref[...] += jnp.sum(df * (1.0 - p.sg), axis=0, keepdims=True)
        dq_ref[...] = ((dqg * p.eg + dqd * p.eqd) * p.sigq * (1.0 + qr * (1.0 - p.sigq))).astype(BF16)
        dv_ref[...] = wide(dv).astype(BF16)
        dhg_ref[...] = wide(dhg).astype(BF16)
        dgain_ref[...] += wide(dgain)

    o = jax.ShapeDtypeStruct((T, HGRN_WIDTH), BF16)
    acc = jax.ShapeDtypeStruct((B_loc, 1, HGRN_WIDTH), F32)
    return pl.pallas_call(
        body, name="hgrn_bwd", grid=(B_loc, N_HEAD_GROUPS, nc), in_specs=in_specs + [st_spec, blk],
        out_specs=[blk, blk, blk, blk, acc_spec, acc_spec], out_shape=[o, o, o, o, acc, acc],
        scratch_shapes=[pltpu.VMEM((HEADS_PER_STEP, HGRN_DIM, HGRN_DIM), F32), pltpu.VMEM((CHUNK, GROUP_WIDTH), F32),
                        pltpu.VMEM((CHUNK, GROUP_WIDTH), F32)],
        compiler_params=_params(("parallel", "parallel", "arbitrary")),
    )(proj, proj, proj, proj, lb, gain, states, dyh)


def _adamw(w, g, m, v, name):
    R, C = w.shape
    tr = _pick(R, (128, 64, 32, 16, 8)) if C > 1024 else _pick(R, (512, 256, 128, 64, 32, 16, 8))

    def body(w_ref, g_ref, m_ref, v_ref, d_ref, nm_ref, nv_ref):
        gv = g_ref[...]
        nm = ADAM_B1 * m_ref[...] + (1.0 - ADAM_B1) * gv
        nv = ADAM_B2 * v_ref[...] + (1.0 - ADAM_B2) * (gv * gv)
        m_hat = nm / (1.0 - ADAM_B1 ** ADAM_STEP)
        v_hat = nv / (1.0 - ADAM_B2 ** ADAM_STEP)
        d_ref[...] = -ADAM_LR * (m_hat / (jnp.sqrt(v_hat) + ADAM_EPS) + ADAM_WD * w_ref[...])
        nm_ref[...] = nm
        nv_ref[...] = nv

    blk = pl.BlockSpec((tr, C), lambda i: (i, 0))
    o = jax.ShapeDtypeStruct((R, C), F32)
    return pl.pallas_call(
        body, name=name, grid=(R // tr,), in_specs=[blk] * 4, out_specs=[blk] * 3, out_shape=[o, o, o],
        compiler_params=_params(("parallel",)),
    )(w, g, m, v)


ANY = pl.BlockSpec(memory_space=pl.ANY)
VMEM_SPEC = pl.BlockSpec(memory_space=pltpu.VMEM)


def _place():
    x, y, c = lax.axis_index("x"), lax.axis_index("y"), lax.axis_index("c")
    other_chips = [(1 - x, y), (x, 1 - y), (1 - x, 1 - y)]
    return x, y, c, other_chips


def _cast_into_full(w, ax, s_arr, name):
    R, C = w.shape
    tr = _pick(R, (256, 128))
    nr = R // tr

    def body(s_ref, w_ref, o_ref):
        o_ref[...] = w_ref[...].astype(BF16)

    if ax == 1:
        shape, o_map = (R, N_CHIPS * C), lambda i, s: (i, s[0])
    else:
        shape, o_map = (N_CHIPS * R, C), lambda i, s: (s[0] * nr + i, 0)
    return pl.pallas_call(
        body, name=name,
        grid_spec=pltpu.PrefetchScalarGridSpec(
            num_scalar_prefetch=1, grid=(nr,), in_specs=[pl.BlockSpec((tr, C), lambda i, s: (i, 0))],
            out_specs=pl.BlockSpec((tr, C), o_map)),
        out_shape=jax.ShapeDtypeStruct(shape, BF16),
        compiler_params=_params(("parallel",)),
    )(s_arr, w)


class _Gather:
    def __init__(self, fulls, shard_shapes, axes, tag, after=None):
        self.shapes, self.axes, self.tag, self.nw = shard_shapes, axes, tag, len(fulls)
        nw = self.nw

        def body(*refs):
            ins, sems = refs[:nw], refs[nw + (after is not None):nw + (after is not None) + 2 * (N_CHIPS - 1)]
            for j in range(N_CHIPS - 1):
                for cp in self._peer_copies(ins, sems[2 * j], sems[2 * j + 1], j):
                    cp.start()
            refs[-1][...] = jnp.zeros_like(refs[-1])

        out = pl.pallas_call(
            body, name="gather_start_" + tag,
            out_shape=(*[pltpu.SemaphoreType.DMA((nw,))] * (2 * (N_CHIPS - 1)),
                       *[pltpu.HBM(f.shape, f.dtype) for f in fulls], jax.ShapeDtypeStruct((8, LANES), F32)),
            in_specs=[HBM_SPEC] * nw + ([] if after is None else [ANY]),
            out_specs=(*[SEM_SPEC] * (2 * (N_CHIPS - 1)), *[HBM_SPEC] * nw, VMEM_SPEC),
            input_output_aliases={k: 2 * (N_CHIPS - 1) + k for k in range(nw)},
            compiler_params=pltpu.CompilerParams(has_side_effects=DATAFLOW),
        )(*[pltpu.with_memory_space_constraint(f, pltpu.HBM) for f in fulls], *(() if after is None else (after,)))
        self.sems = [(out[2 * j], out[2 * j + 1]) for j in range(N_CHIPS - 1)]
        self.fulls = list(out[2 * (N_CHIPS - 1):2 * (N_CHIPS - 1) + nw])
        self.token = out[-1]

    def _region(self, ref, i, t, half):
        R, C = self.shapes[i]
        hr = R // 2
        if self.axes[i] == 1:
            return ref.at[pl.ds(half * hr, hr), pl.ds(pl.multiple_of(t * C, LANES), C)]
        return ref.at[pl.ds(t * R + half * hr, hr), :]

    def _peer_copies(self, refs, send_sems, recv_sems, j):
        x, y, c, chips = _place()
        s = 2 * x + y
        return [pltpu.make_async_remote_copy(
            src_ref=self._region(refs[i], i, s, c), dst_ref=self._region(refs[i], i, s, c), send_sem=send_sems.at[i],
            recv_sem=recv_sems.at[i], device_id=(*chips[j], c), device_id_type=MESH) for i in range(self.nw)]

    def wait(self, peers, after):
        nw, np_ = self.nw, len(peers)

        def body(*refs):
            ins, sems = refs[:nw], refs[nw:nw + 2 * np_]
            for k, j in enumerate(peers):
                for cp in self._peer_copies(ins, sems[2 * k], sems[2 * k + 1], j):
                    cp.wait_send()
                    cp.wait_recv()

        sem_args = [s for j in peers for s in self.sems[j]]
        out = pl.pallas_call(
            body, name="gather_wait_%s_%s" % (self.tag, "".join(map(str, peers))),
            out_shape=tuple(pltpu.HBM(f.shape, f.dtype) for f in self.fulls),
            in_specs=[HBM_SPEC] * nw + [SEM_SPEC] * (2 * np_) + [ANY], out_specs=tuple([HBM_SPEC] * nw),
            input_output_aliases={k: k for k in range(nw)},
            compiler_params=pltpu.CompilerParams(has_side_effects=DATAFLOW),
        )(*self.fulls, *sem_args, after)
        self.fulls = list(out)

    def forward(self, peers):
        nw, np_ = self.nw, len(peers)

        def body(*refs):
            ins, outs = refs[:nw], refs[nw:2 * nw]
            send_sems, recv_sems = refs[2 * nw:]
            x, y, c, chips = _place()
            cps = []
            for i in range(nw):
                for k, j in enumerate(peers):
                    t = 2 * chips[j][0] + chips[j][1]
                    cp = pltpu.make_async_remote_copy(
                        src_ref=self._region(ins[i], i, t, c), dst_ref=self._region(outs[i], i, t, c),
                        send_sem=send_sems.at[i * np_ + k], recv_sem=recv_sems.at[i * np_ + k],
                        device_id=(x, y, 1 - c), device_id_type=MESH)
                    cp.start()
                    cps.append(cp)
            for cp in cps:
                cp.wait()

        out = pl.pallas_call(
            body, name="gather_forward_%s_%s" % (self.tag, "".join(map(str, peers))),
            in_specs=[ANY] * nw, out_specs=[ANY] * nw,
            out_shape=[jax.ShapeDtypeStruct(f.shape, f.dtype) for f in self.fulls],
            input_output_aliases={i: i for i in range(nw)},
            scratch_shapes=[pltpu.SemaphoreType.DMA((nw * np_,)), pltpu.SemaphoreType.DMA((nw * np_,))],
        )(*self.fulls)
        self.fulls = list(out)


def _matmul_slab(a, wfull, slab_arr, prev, after, name):
    M, K = a.shape
    N = wfull.shape[1]
    nslab = N // N_CHIPS
    tm = _pick(M, (1024, 512, 256, 128))
    tn = _pick(nslab, (896, 512, 384, 256, 128))
    per = nslab // tn
    extra = [e for e in (prev, after) if e is not None]

    def body(slab_ref, a_ref, b_ref, *rest):
        rest[len(extra)][...] = jnp.dot(a_ref[...], b_ref[...], preferred_element_type=F32)

    return pl.pallas_call(
        body, name=name,
        grid_spec=pltpu.PrefetchScalarGridSpec(
            num_scalar_prefetch=1, grid=(M // tm, per),
            in_specs=[pl.BlockSpec((tm, K), lambda i, j, sl: (i, 0)),
                      pl.BlockSpec((K, tn), lambda i, j, sl: (0, sl[0] * per + j))] + [ANY] * len(extra),
            out_specs=pl.BlockSpec((tm, tn), lambda i, j, sl: (i, sl[0] * per + j))),
        out_shape=jax.ShapeDtypeStruct((M, N), F32),
        input_output_aliases={} if prev is None else {3: 0},
        compiler_params=_params(("parallel", "arbitrary")),
    )(slab_arr, a, wfull, *extra)


def _exchange_sibling_halves(gs, name):
    nw = len(gs)

    def body(*refs):
        ins, outs = refs[:nw], refs[nw:2 * nw]
        send_sems, recv_sems = refs[2 * nw:]
        x, y, c, _ = _place()
        cps = []
        for i in range(nw):
            cp = pltpu.make_async_remote_copy(src_ref=ins[i].at[:, 1 - c], dst_ref=outs[i], send_sem=send_sems.at[i],
                                              recv_sem=recv_sems.at[i], device_id=(x, y, 1 - c), device_id_type=MESH)
            cp.start()
            cps.append(cp)
        for cp in cps:
            cp.wait()

    return pl.pallas_call(
        body, name=name, in_specs=[ANY] * nw, out_specs=[ANY] * nw,
        out_shape=[jax.ShapeDtypeStruct((g.shape[0],) + g.shape[2:], g.dtype) for g in gs],
        scratch_shapes=[pltpu.SemaphoreType.DMA((nw,)), pltpu.SemaphoreType.DMA((nw,))],
    )(*gs)


HBM_SPEC = pl.BlockSpec(memory_space=pltpu.HBM)
SEM_SPEC = pl.BlockSpec(memory_space=pltpu.SEMAPHORE)
DATAFLOW = pltpu.SideEffectType.DATAFLOW_SIDE_EFFECTING


def _chip_copies(ins, lands, send_sems, recv_sems):
    x, y, c, chips = _place()
    return [pltpu.make_async_remote_copy(
        src_ref=ins[i].at[2 * chip[0] + chip[1]], dst_ref=lands[i].at[j], send_sem=send_sems.at[i * 3 + j],
        recv_sem=recv_sems.at[i * 3 + j], device_id=(*chip, c), device_id_type=MESH)
        for i in range(len(ins)) for j, chip in enumerate(chips)]


def _chips_send_start(ss, name):
    nw = len(ss)
    lands = [pltpu.with_memory_space_constraint(lax.empty((N_CHIPS - 1,) + s.shape[1:], s.dtype), pltpu.HBM) for s in ss]

    def body(*refs):
        ins, land_refs = refs[:nw], refs[nw:2 * nw]
        send_sems, recv_sems = refs[2 * nw], refs[2 * nw + 1]
        token = refs[-1]
        for cp in _chip_copies(ins, land_refs, send_sems, recv_sems):
            cp.start()
        token[...] = jnp.zeros_like(token)

    n = 3 * nw
    out = pl.pallas_call(
        body, name=name,
        out_shape=(pltpu.SemaphoreType.DMA((n,)), pltpu.SemaphoreType.DMA((n,)),
                   *[pltpu.HBM(s.shape, s.dtype) for s in ss], *[pltpu.HBM(l.shape, l.dtype) for l in lands],
                   jax.ShapeDtypeStruct((8, LANES), F32)),
        in_specs=[HBM_SPEC] * (2 * nw), out_specs=(SEM_SPEC, SEM_SPEC, *[HBM_SPEC] * (2 * nw), VMEM_SPEC),
        input_output_aliases={k: 2 + k for k in range(2 * nw)},
        compiler_params=pltpu.CompilerParams(has_side_effects=DATAFLOW),
    )(*[pltpu.with_memory_space_constraint(s, pltpu.HBM) for s in ss], *lands)
    return out[0], out[1], list(out[2:2 + nw]), list(out[2 + nw:2 + 2 * nw]), out[-1]


def _chips_send_wait(send_sems, recv_sems, ss, lands, after, name):
    nw = len(ss)

    def body(*refs):
        ins, land_refs = refs[:nw], refs[nw:2 * nw]
        s_sems, r_sems = refs[2 * nw], refs[2 * nw + 1]
        for cp in _chip_copies(ins, land_refs, s_sems, r_sems):
            cp.wait_send()
            cp.wait_recv()

    out = pl.pallas_call(
        body, name=name,
        out_shape=(*[pltpu.HBM(s.shape, s.dtype) for s in ss], *[pltpu.HBM(l.shape, l.dtype) for l in lands]),
        in_specs=[HBM_SPEC] * (2 * nw) + [SEM_SPEC, SEM_SPEC, ANY], out_specs=tuple([HBM_SPEC] * (2 * nw)),
        input_output_aliases={k: k for k in range(2 * nw)},
        compiler_params=pltpu.CompilerParams(has_side_effects=DATAFLOW),
    )(*ss, *lands, send_sems, recv_sems, after)
    return list(out[nw:])


def _sum_small(pack):
    rows = pack.shape[0]

    def body(pack_ref, sum_ref, all_ref, send_sems, recv_sems):
        x, y, c, _ = _place()
        me = 4 * x + 2 * y + c
        all_ref[me] = pack_ref[...]
        cps = []
        for k in range(1, N_DEV):
            to = (1 - x if k & 4 else x, 1 - y if k & 2 else y, 1 - c if k & 1 else c)
            cp = pltpu.make_async_remote_copy(
                src_ref=pack_ref, dst_ref=all_ref.at[me], send_sem=send_sems.at[k - 1],
                recv_sem=recv_sems.at[k - 1], device_id=to, device_id_type=MESH)
            cp.start()
            cps.append(cp)
        for cp in cps:
            cp.wait()
        total = all_ref[0]
        for d in range(1, N_DEV):
            total = total + all_ref[d]
        sum_ref[...] = total

    return pl.pallas_call(
        body, name="sum_small_grads", in_specs=[VMEM_SPEC], out_specs=VMEM_SPEC,
        out_shape=jax.ShapeDtypeStruct(pack.shape, F32),
        scratch_shapes=[pltpu.VMEM((N_DEV, rows, LANES), F32), pltpu.SemaphoreType.DMA((N_DEV - 1,)),
                        pltpu.SemaphoreType.DMA((N_DEV - 1,))],
    )(pack)


def _share_with_sibling(fs, name):
    nw = len(fs)

    def body(*refs):
        ins, outs = refs[:nw], refs[nw:2 * nw]
        send_sems, recv_sems = refs[2 * nw:]
        x, y, c, _ = _place()
        cps = []
        for i in range(nw):
            cp = pltpu.make_async_remote_copy(src_ref=ins[i].at[c], dst_ref=outs[i].at[c], send_sem=send_sems.at[i],
                                              recv_sem=recv_sems.at[i], device_id=(x, y, 1 - c), device_id_type=MESH)
            cp.start()
            cps.append(cp)
        for cp in cps:
            cp.wait()

    return pl.pallas_call(
        body, name=name, in_specs=[ANY] * nw, out_specs=[ANY] * nw,
        out_shape=[jax.ShapeDtypeStruct(f.shape, f.dtype) for f in fs],
        input_output_aliases={i: i for i in range(nw)},
        scratch_shapes=[pltpu.SemaphoreType.DMA((nw,)), pltpu.SemaphoreType.DMA((nw,))],
    )(*fs)


def _sum_sibling(g, land, c_arr, name):
    _, _, P, Q = g.shape
    tp = _pick(P, (256, 128, 64))

    def body(c_ref, g_ref, l_ref, s_ref):
        s_ref[...] = (g_ref[...].astype(F32) + l_ref[...].astype(F32)).astype(BF16)

    return pl.pallas_call(
        body, name=name,
        grid_spec=pltpu.PrefetchScalarGridSpec(
            num_scalar_prefetch=1, grid=(N_CHIPS, P // tp),
            in_specs=[pl.BlockSpec((None, None, tp, Q), lambda t, r, cr: (t, cr[0], r, 0)),
                      pl.BlockSpec((None, tp, Q), lambda t, r, cr: (t, r, 0))],
            out_specs=pl.BlockSpec((None, tp, Q), lambda t, r, cr: (t, r, 0))),
        out_shape=jax.ShapeDtypeStruct((N_CHIPS, P, Q), BF16),
        compiler_params=_params(("parallel", "parallel")),
    )(c_arr, g, land)


def _sum_chips(g, land, recv, sc_arr, name):
    _, _, P, Q = g.shape
    tp = _pick(P, (256, 128, 64))

    def body(sc_ref, g_ref, l_ref, r_ref, f_ref):
        acc = g_ref[...].astype(F32) + l_ref[...].astype(F32)
        for j in range(N_CHIPS - 1):
            acc = acc + r_ref[j].astype(F32)
        f_ref[...] = acc

    return pl.pallas_call(
        body, name=name,
        grid_spec=pltpu.PrefetchScalarGridSpec(
            num_scalar_prefetch=1, grid=(P // tp,),
            in_specs=[pl.BlockSpec((None, None, tp, Q), lambda r, sc: (sc[0], sc[1], r, 0)),
                      pl.BlockSpec((None, tp, Q), lambda r, sc: (sc[0], r, 0)),
                      pl.BlockSpec((N_CHIPS - 1, tp, Q), lambda r, sc: (0, r, 0))],
            out_specs=pl.BlockSpec((None, tp, Q), lambda r, sc: (sc[1], r, 0))),
        out_shape=jax.ShapeDtypeStruct((2, P, Q), F32),
        compiler_params=_params(("parallel",)),
    )(sc_arr, g, land, recv)


class _Reduction:
    def __init__(self, names, pieces, lands, flight):
        self.names, self.pieces, self.lands = names, pieces, lands
        self.send_sems, self.recv_sems, self.sums, self.zones, self.token = flight


def _reduce_start(grads, names, c_arr):
    pieces = [g.reshape(N_CHIPS, 2, -1, g.shape[-1]) for g in grads]
    tag = names[0] if len(names) == 1 else "branches"
    lands = _exchange_sibling_halves(pieces, "grads_to_sibling_" + tag)
    sums = [_sum_sibling(g, l, c_arr, "sum_sibling_" + nm) for g, l, nm in zip(pieces, lands, names)]
    return _Reduction(names, pieces, lands, _chips_send_start(sums, "grads_to_chips_start_" + tag))


def _reduce_finish(red, after, sc_arr):
    tag = red.names[0] if len(red.names) == 1 else "branches"
    recvs = _chips_send_wait(red.send_sems, red.recv_sems, red.sums, red.zones, after, "grads_to_chips_wait_" + tag)
    halves = [_sum_chips(g, l, r, sc_arr, "sum_chips_" + nm) for g, l, r, nm in zip(red.pieces, red.lands, recvs, red.names)]
    return _share_with_sibling(halves, "grads_share_sibling_" + tag)


def _t5_bucket(dist):
    max_exact = REL_BUCKETS // 2
    d = jnp.maximum(dist, 0)
    df = jnp.maximum(d, 1).astype(F32)
    large = max_exact + (jnp.log(df / max_exact) / math.log(REL_MAX_DIST / max_exact)
                         * (REL_BUCKETS - max_exact)).astype(jnp.int32)
    large = jnp.minimum(large, REL_BUCKETS - 1)
    return jnp.where(d < max_exact, d, large)


def _bucket_table():
    qi = jnp.arange(WINDOW)[:, None]
    si = jnp.arange(2 * WINDOW)[None, :]
    return _t5_bucket(qi + WINDOW - si)


TILE_WORDS = 8 * LANES


def _tile_rows(shape):
    return -(-math.prod(shape) // TILE_WORDS) * 8


def _rows_of(a):
    flat = a.reshape(-1).astype(F32)
    n = _tile_rows(a.shape) * LANES
    return jnp.pad(flat, (0, n - flat.shape[0])).reshape(-1, LANES)


def _pack_rows(parts):
    return jnp.concatenate([_rows_of(p) for p in parts], axis=0)


def _unpack_rows(packed, shapes):
    out, at = [], 0
    for shp in shapes:
        n, nr = math.prod(shp), _tile_rows(shp)
        out.append(packed[at:at + nr].reshape(-1)[:n].reshape(shp))
        at += nr
    return out


def kernel(x, norm_pre, w_in, rel_bias, attn_sinks, lb_logits, hgrn_norm, w_branch_attn, w_branch_hgrn, w_out, norm_post, loss_target, m_norm_pre, m_w_in, m_rel_bias, m_attn_sinks, m_lb_logits, m_hgrn_norm, m_w_branch_attn, m_w_branch_hgrn, m_w_out, m_norm_post, v_norm_pre, v_w_in, v_rel_bias, v_attn_sinks, v_lb_logits, v_hgrn_norm, v_w_branch_attn, v_w_branch_hgrn, v_w_out, v_norm_post):
    B_loc, S, D = x.shape
    T = B_loc * S
    x2 = x.reshape(T, D)
    tgt2 = loss_target.reshape(T, D)
    my_x, my_y, my_c = lax.axis_index("x"), lax.axis_index("y"), lax.axis_index("c")

    c_arr = jnp.reshape(my_c, (1,)).astype(jnp.int32)
    s_arr = jnp.reshape(2 * my_x + my_y, (1,)).astype(jnp.int32)
    sc_arr = jnp.concatenate([s_arr, c_arr])
    shard_ws = [w_in[0], w_branch_attn[0], w_branch_hgrn[0], w_out[0]]
    shard_axes = (1, 1, 1, 0)
    names = ["w_in", "w_branch_attn", "w_branch_hgrn", "w_out"]
    placed = [_cast_into_full(w, ax, s_arr, "cast_" + nm) for w, ax, nm in zip(shard_ws, shard_axes, names)]
    peer_slabs = [jnp.reshape(t, (1,)).astype(jnp.int32)
                  for t in (2 * (1 - my_x) + my_y, 2 * my_x + 1 - my_y, 2 * (1 - my_x) + 1 - my_y)]

    buckets = _bucket_table()
    onehot = (buckets.reshape(-1)[:, None] == jnp.arange(REL_BUCKETS)[None, :]).astype(F32)
    bias_tab = jnp.dot(onehot, rel_bias.astype(F32), precision=lax.Precision.HIGHEST).T.reshape(ATTN_HEADS, WINDOW, 2 * WINDOW)
    sinks_b = jnp.broadcast_to(attn_sinks[0].astype(F32)[:, None, None], (ATTN_HEADS, 8, LANES))
    lb_fn = lambda l: jnp.cumsum(jax.nn.softmax(l.astype(F32), axis=0), axis=0)[:1]
    lb, lb_vjp = jax.vjp(lb_fn, lb_logits)
    gain_h = hgrn_norm[0].reshape(1, HGRN_WIDTH)

    h, rstd = _rmsnorm_fwd(x2, norm_pre)
    gather_in = _Gather(placed[:1], [shard_ws[0].shape], shard_axes[:1], "w_in")
    proj = _matmul_slab(h, gather_in.fulls[0], s_arr, None, gather_in.token, "in_proj_own")
    for j in range(N_CHIPS - 1):
        gather_in.wait([j], proj)
        gather_in.forward([j])
        after = None
        if j == N_CHIPS - 2:
            gather_rest = _Gather(placed[1:], [w.shape for w in shard_ws[1:]], shard_axes[1:], "rest", gather_in.fulls[0])
            after = gather_rest.token
        proj = _matmul_slab(h, gather_in.fulls[0], peer_slabs[j], proj, after, "in_proj_peer%d" % j)
    win_f = gather_in.fulls[0]
    ya = _attn_fwd(proj, bias_tab, sinks_b, B_loc, S)
    yh, states = _hgrn_fwd(proj, lb, gain_h, B_loc, S)
    gather_rest.wait([0, 1, 2], yh)
    gather_rest.forward([0, 1, 2])
    wba_f, wbh_f, wout_f = gather_rest.fulls
    ua = _matmul(ya, wba_f, "nn", F32, "branch_attn_proj")
    uh = _matmul(yh, wbh_f, "nn", F32, "branch_hgrn_proj")
    merged = _merge_fwd(proj, ua, uh)
    yv = _matmul(merged, wout_f, "nn", F32, "out_proj")
    dy, dout, loss_p, gnpost_p = _post_loss(yv, x2, tgt2, norm_post)

    dmerged = _matmul(dy, wout_f, "nt", F32, "out_proj_dgrad")
    g_wout = _matmul(merged, dy, "tn", BF16, "out_proj_wgrad")
    d_ua, d_uh, d_ga, d_gh = _merge_bwd(proj, ua, uh, dmerged)
    g_wba = _matmul(ya, d_ua, "tn", BF16, "branch_attn_wgrad", slabs=N_CHIPS)
    g_wbh = _matmul(yh, d_uh, "tn", BF16, "branch_hgrn_wgrad", slabs=N_CHIPS)
    late = _reduce_start([g_wba, g_wbh, g_wout], names[1:], c_arr)
    d_ya = _matmul(d_ua, wba_f, "nt", BF16, "branch_attn_dgrad", after=late.token)
    d_yh = _matmul(d_uh, wbh_f, "nt", BF16, "branch_hgrn_dgrad", after=late.token)
    dq, dk, dv, dag, dbias_p, dsinks_p = _attn_bwd(proj, bias_tab, sinks_b, d_ya, B_loc, S)
    dhq, dhf, dhi, dhg, dlb_p, dgain_p = _hgrn_bwd(proj, states, lb, gain_h, d_yh, B_loc, S)
    dproj = jnp.concatenate([dq, dk, dv, dag, dhq, dhf, dhi, dhg, d_ga, d_gh], axis=1)
    g_win = _matmul(h, dproj, "tn", BF16, "in_proj_wgrad", slabs=N_CHIPS)
    last = _reduce_start([g_win], names[:1], c_arr)
    dh = _matmul(dproj, win_f, "nt", F32, "in_proj_dgrad", after=last.token)
    grad_x2, gnpre_p = _rmsnorm_bwd(dh, x2, rstd, norm_pre, dout)
    shared = _reduce_finish(last, grad_x2, sc_arr) + _reduce_finish(late, grad_x2, sc_arr)

    grelb_p = jnp.dot(dbias_p.reshape(ATTN_HEADS, -1), onehot, precision=lax.Precision.HIGHEST).T
    gsinks_p = dsinks_p[:, 0, 0]
    dlb_sum = jnp.sum(dlb_p, axis=0).reshape(1, HGRN_WIDTH)
    ghn_p = jnp.sum(dgain_p, axis=0).reshape(HGRN_HEADS, HGRN_DIM)
    small_parts = [gnpre_p, gnpost_p, grelb_p, gsinks_p, dlb_sum, ghn_p, loss_p]
    small_shapes = [p.shape for p in small_parts]
    pack_sum = _sum_small(_pack_rows(small_parts))
    big_w = [w_in, w_branch_attn, w_branch_hgrn, w_out]
    big_m = [m_w_in, m_w_branch_attn, m_w_branch_hgrn, m_w_out]
    big_v = [v_w_in, v_w_branch_attn, v_w_branch_hgrn, v_w_out]
    big = {}
    for nm, gs, w, m, v in zip(names, shared, big_w, big_m, big_v):
        shp = w.shape
        g2 = gs.reshape(shp[1], shp[2])
        d, nm_, nv_ = _adamw(w[0], g2, m[0], v[0], "adamw_" + nm)
        big[nm] = tuple(a.reshape(shp) for a in (g2, d, nm_, nv_))

    gnpre, gnpost, grelb, gsinks, dlb, ghn, loss = _unpack_rows(pack_sum, small_shapes)
    (g_lb_logits,) = lb_vjp(dlb)
    small_names = ["norm_pre", "rel_bias", "attn_sinks", "lb_logits", "hgrn_norm", "norm_post"]
    small_w = [norm_pre, rel_bias, attn_sinks, lb_logits, hgrn_norm, norm_post]
    small_m = [m_norm_pre, m_rel_bias, m_attn_sinks, m_lb_logits, m_hgrn_norm, m_norm_post]
    small_v = [v_norm_pre, v_rel_bias, v_attn_sinks, v_lb_logits, v_hgrn_norm, v_norm_post]
    small_g = [gnpre.reshape(norm_pre.shape), grelb.reshape(rel_bias.shape), gsinks.reshape(attn_sinks.shape),
               g_lb_logits.reshape(lb_logits.shape), ghn.reshape(hgrn_norm.shape), gnpost.reshape(norm_post.shape)]
    shapes = [w.shape for w in small_w]
    d_s, nm_s, nv_s = _adamw(_pack_rows(small_w), _pack_rows(small_g), _pack_rows(small_m), _pack_rows(small_v),
                             "adamw_small")
    small = {}
    for nm, g, d, m_, v_ in zip(small_names, small_g, _unpack_rows(d_s, shapes), _unpack_rows(nm_s, shapes),
                                _unpack_rows(nv_s, shapes)):
        small[nm] = (g, d, m_, v_)

    allw = {**big, **small}
    order = ["norm_pre", "w_in", "rel_bias", "attn_sinks", "lb_logits", "hgrn_norm", "w_branch_attn", "w_branch_hgrn",
             "w_out", "norm_post"]
    outs = [loss.reshape(()), grad_x2.reshape(B_loc, S, D)]
    for k in range(4):
        outs += [allw[nm][k] for nm in order]
    return tuple(outs)
```

```python
import functools
import math

import jax
import jax.numpy as jnp
from jax import lax
from jax.experimental import pallas as pl
from jax.experimental.pallas import tpu as pltpu

F32 = jnp.float32
BF16 = jnp.bfloat16
MESH = pl.DeviceIdType.MESH

ATTN_HEADS = 16
ATTN_KV_HEADS = 4
HEAD_DIM = 64
GROUP = ATTN_HEADS // ATTN_KV_HEADS
WINDOW = 128
ATTN_WIDTH = ATTN_HEADS * HEAD_DIM
KV_WIDTH = ATTN_KV_HEADS * HEAD_DIM
HGRN_HEADS = 8
HGRN_DIM = 128
HGRN_WIDTH = HGRN_HEADS * HGRN_DIM
CHUNK = 64
SUB = 16
NSUB = CHUNK // SUB
REL_BUCKETS = 32
REL_MAX_DIST = 128
NORM_EPS = 1e-6
ADAM_LR = 0.001
ADAM_B1 = 0.9
ADAM_B2 = 0.999
ADAM_EPS = 1e-08
ADAM_WD = 0.01
ADAM_STEP = 10
LANES = 128
N_CHIPS = 4
N_DEV = 8
VMEM_LIMIT = 48 * 1024 * 1024
MATMUL_OPERAND_BYTES = 20 * 1024 * 1024

OFF_AQ = 0
OFF_AK = OFF_AQ + ATTN_WIDTH
OFF_AV = OFF_AK + KV_WIDTH
OFF_AG = OFF_AV + KV_WIDTH
OFF_HQ = OFF_AG + ATTN_WIDTH
OFF_HF = OFF_HQ + HGRN_WIDTH
OFF_HI = OFF_HF + HGRN_WIDTH
OFF_HG = OFF_HI + HGRN_WIDTH
OFF_GA = OFF_HG + HGRN_WIDTH

NT_DIMS = (((1,), (1,)), ((), ()))
TN_DIMS = (((0,), (0,)), ((), ()))
NN_DIMS = (((1,), (0,)), ((), ()))


def _pick(n, cands):
    for c in cands:
        if n % c == 0:
            return c
    raise ValueError(f"no tile for {n} in {cands}")


def _params(sem):
    return pltpu.CompilerParams(dimension_semantics=sem, vmem_limit_bytes=VMEM_LIMIT)


def _bdot(a, b, dims=NN_DIMS):
    return lax.dot_general(a.astype(BF16), b.astype(BF16), dims, preferred_element_type=F32)


def _matmul(a, b, mode, out_dtype, name, slabs=1, after=None):
    if mode == "nn":
        (M, K), (K2, N) = a.shape, b.shape
    elif mode == "nt":
        (M, K), (N, K2) = a.shape, b.shape
    else:
        (K, M), (K2, N) = a.shape, b.shape
    assert K == K2
    nslab = N // slabs
    tm = _pick(M, (1024, 512, 256, 128))
    tn = _pick(nslab, (1024, 896, 512, 384, 256, 128))
    tk = _pick(K, [t for t in (4096, 2048, 1792, 1536, 1024, 512, 256, 128) if 4 * t * (tm + tn) <= MATMUL_OPERAND_BYTES])
    nk = K // tk
    per = nslab // tn
    dims = {"nn": NN_DIMS, "nt": NT_DIMS, "tn": TN_DIMS}[mode]

    n_in = 2 if after is None else 3

    def body(*refs):
        a_ref, b_ref, o_ref, acc = refs[0], refs[1], refs[n_in], refs[n_in + 1:]
        part = lax.dot_general(a_ref[...], b_ref[...], dims, preferred_element_type=F32)
        if nk == 1:
            o_ref[...] = part.astype(o_ref.dtype)
            return
        acc_ref, = acc
        k = pl.program_id(2)

        @pl.when(k == 0)
        def _():
            acc_ref[...] = part

        @pl.when((k > 0) & (k < nk - 1))
        def _():
            acc_ref[...] += part

        @pl.when(k == nk - 1)
        def _():
            o_ref[...] = (acc_ref[...] + part).astype(o_ref.dtype)

    if mode == "tn":
        a_spec = pl.BlockSpec((tk, tm), lambda i, j, k: (k, i))
    else:
        a_spec = pl.BlockSpec((tm, tk), lambda i, j, k: (i, k))
    if mode == "nt":
        b_spec = pl.BlockSpec((tn, tk), lambda i, j, k: (j, k))
    else:
        b_spec = pl.BlockSpec((tk, tn), lambda i, j, k: (k, j))
    if slabs == 1:
        o_shape = jax.ShapeDtypeStruct((M, N), out_dtype)
        o_spec = pl.BlockSpec((tm, tn), lambda i, j, k: (i, j))
    else:
        o_shape = jax.ShapeDtypeStruct((slabs, M, nslab), out_dtype)
        o_spec = pl.BlockSpec((None, tm, tn), lambda i, j, k: (j // per, i, j % per))
    return pl.pallas_call(
        body, name=name, grid=(M // tm, N // tn, nk), in_specs=[a_spec, b_spec] + ([] if after is None else [ANY]),
        out_specs=o_spec, out_shape=o_shape,
        scratch_shapes=[pltpu.VMEM((tm, tn), F32)] if nk > 1 else [],
        compiler_params=_params(("parallel", "parallel", "arbitrary")),
    )(*((a, b) if after is None else (a, b, after)))


def _rmsnorm_fwd(x2, gain):
    T, D = x2.shape
    tr = _pick(T, (256, 128))

    def body(x_ref, g_ref, h_ref, r_ref):
        xv = x_ref[...]
        r = lax.rsqrt(jnp.mean(xv * xv, axis=-1, keepdims=True) + NORM_EPS)
        h_ref[...] = (xv * r * g_ref[...]).astype(BF16)
        r_ref[...] = r

    return pl.pallas_call(
        body, name="rmsnorm_pre_fwd", grid=(T // tr,),
        in_specs=[pl.BlockSpec((tr, D), lambda i: (i, 0)), pl.BlockSpec((1, D), lambda i: (0, 0))],
        out_specs=[pl.BlockSpec((tr, D), lambda i: (i, 0)), pl.BlockSpec((tr, 1), lambda i: (i, 0))],
        out_shape=[jax.ShapeDtypeStruct((T, D), BF16), jax.ShapeDtypeStruct((T, 1), F32)],
        compiler_params=_params(("parallel",)),
    )(x2, gain)


def _rmsnorm_bwd(dh, x2, rstd, gain, dout):
    T, D = x2.shape
    tr = _pick(T, (256, 128))

    def body(dh_ref, x_ref, r_ref, g_ref, do_ref, gx_ref, gg_ref):
        @pl.when(pl.program_id(0) == 0)
        def _():
            gg_ref[...] = jnp.zeros_like(gg_ref)

        n = x_ref[...] * r_ref[...]
        dhv = dh_ref[...]
        dn = dhv * g_ref[...]
        gx_ref[...] = do_ref[...] + r_ref[...] * (dn - n * jnp.mean(dn * n, axis=-1, keepdims=True))
        gg_ref[...] += jnp.sum(dhv * n, axis=0, keepdims=True)

    row = pl.BlockSpec((tr, D), lambda i: (i, 0))
    vec = pl.BlockSpec((1, D), lambda i: (0, 0))
    return pl.pallas_call(
        body, name="rmsnorm_pre_bwd", grid=(T // tr,),
        in_specs=[row, row, pl.BlockSpec((tr, 1), lambda i: (i, 0)), vec, row],
        out_specs=[row, vec],
        out_shape=[jax.ShapeDtypeStruct((T, D), F32), jax.ShapeDtypeStruct((1, D), F32)],
        compiler_params=_params(("arbitrary",)),
    )(dh, x2, rstd, gain, dout)


def _post_loss(yv, x2, tgt2, gain):
    T, D = x2.shape
    tr = _pick(T, (256, 128))

    def body(y_ref, x_ref, t_ref, g_ref, dy_ref, do_ref, loss_ref, gg_ref):
        @pl.when(pl.program_id(0) == 0)
        def _():
            gg_ref[...] = jnp.zeros_like(gg_ref)
            loss_ref[...] = jnp.zeros_like(loss_ref)

        yv_ = y_ref[...]
        r = lax.rsqrt(jnp.mean(yv_ * yv_, axis=-1, keepdims=True) + NORM_EPS)
        n = yv_ * r
        e = (x_ref[...] + n * g_ref[...]) - t_ref[...]
        loss_ref[...] += 0.5 * jnp.sum(jnp.mean(e * e, axis=-1, keepdims=True), axis=0, keepdims=True)
        dz = e / D
        do_ref[...] = dz
        gg_ref[...] += jnp.sum(dz * n, axis=0, keepdims=True)
        dn = dz * g_ref[...]
        dy_ref[...] = (r * (dn - n * jnp.mean(dn * n, axis=-1, keepdims=True))).astype(BF16)

    row = pl.BlockSpec((tr, D), lambda i: (i, 0))
    vec = pl.BlockSpec((1, D), lambda i: (0, 0))
    return pl.pallas_call(
        body, name="post_norm_loss", grid=(T // tr,),
        in_specs=[row, row, row, vec],
        out_specs=[row, row, pl.BlockSpec((1, 1), lambda i: (0, 0)), vec],
        out_shape=[jax.ShapeDtypeStruct((T, D), BF16), jax.ShapeDtypeStruct((T, D), F32),
                   jax.ShapeDtypeStruct((1, 1), F32), jax.ShapeDtypeStruct((1, D), F32)],
        compiler_params=_params(("arbitrary",)),
    )(yv, x2, tgt2, gain)


def _merge_fwd(proj, ua, uh):
    T, D = ua.shape
    tr = _pick(T, (256, 128))
    bw = _pick(D, (512, 256))
    assert OFF_GA % bw == 0
    oa, oh = OFF_GA // bw, (OFF_GA + D) // bw

    def body(ga_ref, gh_ref, ua_ref, uh_ref, m_ref):
        m_ref[...] = (jax.nn.sigmoid(ga_ref[...]) * ua_ref[...] + jax.nn.sigmoid(gh_ref[...]) * uh_ref[...]).astype(BF16)

    blk = pl.BlockSpec((tr, bw), lambda i, j: (i, j))
    return pl.pallas_call(
        body, name="merge_fwd", grid=(T // tr, D // bw),
        in_specs=[pl.BlockSpec((tr, bw), lambda i, j: (i, oa + j)), pl.BlockSpec((tr, bw), lambda i, j: (i, oh + j)), blk, blk],
        out_specs=blk, out_shape=jax.ShapeDtypeStruct((T, D), BF16),
        compiler_params=_params(("parallel", "parallel")),
    )(proj, proj, ua, uh)


def _merge_bwd(proj, ua, uh, dm):
    T, D = ua.shape
    tr = _pick(T, (256, 128))
    bw = _pick(D, (512, 256))
    oa, oh = OFF_GA // bw, (OFF_GA + D) // bw

    def body(ga_ref, gh_ref, ua_ref, uh_ref, dm_ref, dua_ref, duh_ref, dga_ref, dgh_ref):
        sa = jax.nn.sigmoid(ga_ref[...])
        sh = jax.nn.sigmoid(gh_ref[...])
        d = dm_ref[...]
        dua_ref[...] = (d * sa).astype(BF16)
        duh_ref[...] = (d * sh).astype(BF16)
        dga_ref[...] = (d * ua_ref[...] * sa * (1.0 - sa)).astype(BF16)
        dgh_ref[...] = (d * uh_ref[...] * sh * (1.0 - sh)).astype(BF16)

    blk = pl.BlockSpec((tr, bw), lambda i, j: (i, j))
    o = jax.ShapeDtypeStruct((T, D), BF16)
    return pl.pallas_call(
        body, name="merge_bwd", grid=(T // tr, D // bw),
        in_specs=[pl.BlockSpec((tr, bw), lambda i, j: (i, oa + j)), pl.BlockSpec((tr, bw), lambda i, j: (i, oh + j)), blk, blk, blk],
        out_specs=[blk, blk, blk, blk], out_shape=[o, o, o, o],
        compiler_params=_params(("parallel", "parallel")),
    )(proj, proj, ua, uh, dm)


KV_PAIR = 2
PAIR_HEADS = KV_PAIR * GROUP


def _attn_mask(n):
    qi = lax.broadcasted_iota(jnp.int32, (WINDOW, 2 * WINDOW), 0)
    si = lax.broadcasted_iota(jnp.int32, (WINDOW, 2 * WINDOW), 1)
    dist = qi + WINDOW - si
    return (dist >= 0) & (dist < WINDOW) & ((si >= WINDOW) | (n > 0))


def _attn_group_fn(mask):
    mask = jnp.concatenate([mask] * GROUP, axis=0)

    def f(q, k, v, ag, bias, sink):
        s = lax.dot_general(q.astype(BF16), k.astype(BF16), NT_DIMS, preferred_element_type=F32) * (HEAD_DIM ** -0.5)
        s = jnp.where(mask, s + bias, -1e30)
        m = lax.stop_gradient(jnp.maximum(jnp.max(s, axis=-1, keepdims=True), sink))
        p = jnp.exp(s - m)
        den = jnp.sum(p, axis=-1, keepdims=True) + jnp.exp(sink - m)
        o = jnp.dot(p.astype(BF16), v.astype(BF16), preferred_element_type=F32)
        return o * (jax.nn.silu(ag) / den)

    return f


def _attn_specs(B_loc, nb, order):
    qw = PAIR_HEADS * HEAD_DIM
    kw = KV_PAIR * HEAD_DIM

    def rows(g):
        b, p, n = order(*g)
        return b * nb + n

    def prev(g):
        b, p, n = order(*g)
        return b * nb + jnp.maximum(n - 1, 0)

    def pp(g):
        return order(*g)[1]

    q = pl.BlockSpec((WINDOW, qw), lambda *g: (rows(g), OFF_AQ // qw + pp(g)))
    kc = pl.BlockSpec((WINDOW, kw), lambda *g: (rows(g), OFF_AK // kw + pp(g)))
    kp = pl.BlockSpec((WINDOW, kw), lambda *g: (prev(g), OFF_AK // kw + pp(g)))
    vc = pl.BlockSpec((WINDOW, kw), lambda *g: (rows(g), OFF_AV // kw + pp(g)))
    vp = pl.BlockSpec((WINDOW, kw), lambda *g: (prev(g), OFF_AV // kw + pp(g)))
    ag = pl.BlockSpec((WINDOW, qw), lambda *g: (rows(g), OFF_AG // qw + pp(g)))
    bias = pl.BlockSpec((PAIR_HEADS, WINDOW, 2 * WINDOW), lambda *g: (pp(g), 0, 0))
    sink = pl.BlockSpec((PAIR_HEADS, 8, LANES), lambda *g: (pp(g), 0, 0))
    return [q, kc, kp, vc, vp, ag, bias, sink], rows, pp


def _attn_operands(q_ref, kc_ref, kp_ref, vc_ref, vp_ref, ag_ref, bias_ref, sink_ref, j):
    lo, hi = j * HEAD_DIM, (j + 1) * HEAD_DIM
    k = jnp.concatenate([kp_ref[:, lo:hi], kc_ref[:, lo:hi]], axis=0)
    v = jnp.concatenate([vp_ref[:, lo:hi], vc_ref[:, lo:hi]], axis=0)
    heads = [j * GROUP + g for g in range(GROUP)]
    q = jnp.concatenate([q_ref[:, h * HEAD_DIM:(h + 1) * HEAD_DIM] for h in heads], axis=0)
    ag = jnp.concatenate([ag_ref[:, h * HEAD_DIM:(h + 1) * HEAD_DIM] for h in heads], axis=0)
    bias = jnp.concatenate([bias_ref[h] for h in heads], axis=0)
    sink = jnp.concatenate([jnp.broadcast_to(sink_ref[h, 0:1, 0:1], (WINDOW, 1)) for h in heads], axis=0)
    return q, k, v, ag, bias, sink


def _attn_fwd(proj, bias_tab, sinks_b, B_loc, S):
    T = B_loc * S
    nb = S // WINDOW
    n_pairs = ATTN_KV_HEADS // KV_PAIR
    in_specs, rows, pp = _attn_specs(B_loc, nb, lambda b, p, n: (b, p, n))

    def body(q_ref, kc_ref, kp_ref, vc_ref, vp_ref, ag_ref, bias_ref, sink_ref, ya_ref):
        f = _attn_group_fn(_attn_mask(pl.program_id(2)))
        for j in range(KV_PAIR):
            out = f(*_attn_operands(q_ref, kc_ref, kp_ref, vc_ref, vp_ref, ag_ref, bias_ref, sink_ref, j))
            for g in range(GROUP):
                h = j * GROUP + g
                ya_ref[:, h * HEAD_DIM:(h + 1) * HEAD_DIM] = out[g * WINDOW:(g + 1) * WINDOW].astype(BF16)

    return pl.pallas_call(
        body, name="attn_fwd", grid=(B_loc, n_pairs, nb), in_specs=in_specs,
        out_specs=pl.BlockSpec((WINDOW, PAIR_HEADS * HEAD_DIM), lambda *g: (rows(g), pp(g))),
        out_shape=jax.ShapeDtypeStruct((T, ATTN_WIDTH), BF16),
        compiler_params=_params(("parallel", "parallel", "parallel")),
    )(proj, proj, proj, proj, proj, proj, bias_tab, sinks_b)


def _attn_bwd(proj, bias_tab, sinks_b, dya, B_loc, S):
    T = B_loc * S
    nb = S // WINDOW
    n_pairs = ATTN_KV_HEADS // KV_PAIR
    order = lambda p, b, i: (b, p, nb - 1 - i)
    in_specs, rows, pp = _attn_specs(B_loc, nb, order)
    qw = PAIR_HEADS * HEAD_DIM
    kw = KV_PAIR * HEAD_DIM

    def body(q_ref, kc_ref, kp_ref, vc_ref, vp_ref, ag_ref, bias_ref, sink_ref, dya_ref,
             dq_ref, dk_ref, dv_ref, dag_ref, dbias_ref, dsink_ref, dkc_ref, dvc_ref):
        b, i = pl.program_id(1), pl.program_id(2)
        n = nb - 1 - i

        @pl.when((b == 0) & (i == 0))
        def _():
            dbias_ref[...] = jnp.zeros_like(dbias_ref)
            dsink_ref[...] = jnp.zeros_like(dsink_ref)

        @pl.when(i == 0)
        def _():
            dkc_ref[...] = jnp.zeros_like(dkc_ref)
            dvc_ref[...] = jnp.zeros_like(dvc_ref)

        f = _attn_group_fn(_attn_mask(n))
        for j in range(KV_PAIR):
            ops = _attn_operands(q_ref, kc_ref, kp_ref, vc_ref, vp_ref, ag_ref, bias_ref, sink_ref, j)
            _, vjp = jax.vjp(f, *ops)
            dout = jnp.concatenate([dya_ref[:, (j * GROUP + g) * HEAD_DIM:(j * GROUP + g + 1) * HEAD_DIM].astype(F32)
                                    for g in range(GROUP)], axis=0)
            dq, dk, dv, dag, dbias, dsink = vjp(dout)
            lo, hi = j * HEAD_DIM, (j + 1) * HEAD_DIM
            dk_ref[:, lo:hi] = (dk[WINDOW:] + dkc_ref[:, lo:hi]).astype(BF16)
            dv_ref[:, lo:hi] = (dv[WINDOW:] + dvc_ref[:, lo:hi]).astype(BF16)
            dkc_ref[:, lo:hi] = dk[:WINDOW]
            dvc_ref[:, lo:hi] = dv[:WINDOW]
            for g in range(GROUP):
                h = j * GROUP + g
                blk = slice(g * WINDOW, (g + 1) * WINDOW)
                dq_ref[:, h * HEAD_DIM:(h + 1) * HEAD_DIM] = dq[blk].astype(BF16)
                dag_ref[:, h * HEAD_DIM:(h + 1) * HEAD_DIM] = dag[blk].astype(BF16)
                dbias_ref[h] += dbias[blk]
                dsink_ref[h] += jnp.broadcast_to(jnp.sum(dsink[blk], axis=0, keepdims=True), (8, LANES))

    wide = pl.BlockSpec((WINDOW, qw), lambda *g: (rows(g), pp(g)))
    narrow = pl.BlockSpec((WINDOW, kw), lambda *g: (rows(g), pp(g)))
    return pl.pallas_call(
        body, name="attn_bwd", grid=(n_pairs, B_loc, nb), in_specs=in_specs + [wide],
        out_specs=[wide, narrow, narrow, wide,
                   pl.BlockSpec((PAIR_HEADS, WINDOW, 2 * WINDOW), lambda *g: (pp(g), 0, 0)),
                   pl.BlockSpec((PAIR_HEADS, 8, LANES), lambda *g: (pp(g), 0, 0))],
        out_shape=[jax.ShapeDtypeStruct((T, ATTN_WIDTH), BF16), jax.ShapeDtypeStruct((T, KV_WIDTH), BF16),
                   jax.ShapeDtypeStruct((T, KV_WIDTH), BF16), jax.ShapeDtypeStruct((T, ATTN_WIDTH), BF16),
                   jax.ShapeDtypeStruct((ATTN_HEADS, WINDOW, 2 * WINDOW), F32),
                   jax.ShapeDtypeStruct((ATTN_HEADS, 8, LANES), F32)],
        scratch_shapes=[pltpu.VMEM((WINDOW, kw), F32), pltpu.VMEM((WINDOW, kw), F32)],
        compiler_params=_params(("arbitrary", "arbitrary", "arbitrary")),
    )(proj, proj, proj, proj, proj, proj, bias_tab, sinks_b, dya)


class _HgrnPre:
    def __init__(self, fr, qr, lb, g_scr):
        t = lax.broadcasted_iota(jnp.int32, (CHUNK, CHUNK), 0)
        s = lax.broadcasted_iota(jnp.int32, (CHUNK, CHUNK), 1)
        self.sg = jax.nn.sigmoid(fr)
        self.f = lb + (1.0 - lb) * self.sg
        g = jnp.dot((t >= s).astype(F32), jnp.log(self.f), precision=lax.Precision.HIGHEST, preferred_element_type=F32)
        g_scr[...] = g
        self.g = g
        self.row = lax.broadcasted_iota(jnp.int32, g.shape, 0)
        self.refs = [jnp.zeros((1, g.shape[1]), F32)] + [g_scr[pl.ds(i * SUB - 1, 1), :] for i in range(1, NSUB)]
        self.gend = g_scr[pl.ds(CHUNK - 1, 1), :]
        refrow = jnp.zeros_like(g)
        for i in range(1, NSUB):
            refrow = jnp.where(self.row >= i * SUB, self.refs[i], refrow)
        self.sigq = jax.nn.sigmoid(qr)
        self.qs = qr * self.sigq
        self.k = 1.0 - self.f
        self.eg = jnp.exp(g)
        self.eqd = jnp.exp(g - refrow)
        self.ekd = [jnp.exp(jnp.where(self.row < (i + 1) * SUB, self.refs[i] - g, 0.0)) for i in range(NSUB)]
        self.ekdec = jnp.exp(self.gend - g)
        self.qg = self.qs * self.eg
        self.qd = self.qs * self.eqd
        self.kd = [self.k * e for e in self.ekd]
        self.kdec = self.k * self.ekdec
        self.egend = jnp.exp(self.gend)


def _hgrn_pair_mask():
    t = lax.broadcasted_iota(jnp.int32, (CHUNK, NSUB * CHUNK), 0)
    col = lax.broadcasted_iota(jnp.int32, (CHUNK, NSUB * CHUNK), 1)
    return ((t // SUB) == (col // CHUNK)) & ((col % CHUNK) <= t)


def _hgrn_head_out(p, lanes, state_t, v, mask):
    qg, qd = p.qg[:, lanes], p.qd[:, lanes]
    kall = jnp.concatenate([kd[:, lanes] for kd in p.kd], axis=0)
    vst = jnp.concatenate([v] * NSUB, axis=0)
    am = jnp.where(mask, _bdot(qd, kall, NT_DIMS), 0.0)
    o = _bdot(qg, state_t, NT_DIMS) + _bdot(am, vst)
    return o, (qg, qd, kall, am, vst)


HEADS_PER_STEP = 4
GROUP_WIDTH = HEADS_PER_STEP * HGRN_DIM
N_HEAD_GROUPS = HGRN_HEADS // HEADS_PER_STEP


def _hgrn_specs(nc, order):
    def rows(g):
        b, grp, n = order(*g)
        return b * nc + n

    def grp_of(g):
        return order(*g)[1]

    def col(off):
        assert off % GROUP_WIDTH == 0
        return pl.BlockSpec((CHUNK, GROUP_WIDTH), lambda *g: (rows(g), off // GROUP_WIDTH + grp_of(g)))

    vec = pl.BlockSpec((1, GROUP_WIDTH), lambda *g: (0, grp_of(g)))
    return [col(OFF_HQ), col(OFF_HF), col(OFF_HI), col(OFF_HG), vec, vec], rows, grp_of


def _hgrn_fwd(proj, lb, gain, B_loc, S):
    T = B_loc * S
    nc = S // CHUNK
    in_specs, rows, grp_of = _hgrn_specs(nc, lambda b, grp, n: (b, grp, n))

    def body(q_ref, f_ref, v_ref, hg_ref, lb_ref, gain_ref, yh_ref, st_ref, state_scr, g_scr):
        @pl.when(pl.program_id(2) == 0)
        def _():
            state_scr[...] = jnp.zeros_like(state_scr)

        p = _HgrnPre(f_ref[...], q_ref[...], lb_ref[...], g_scr)
        v = v_ref[...]
        gate = gain_ref[...] * jax.nn.silu(hg_ref[...])
        mask = _hgrn_pair_mask()
        heads = [slice(hd * HGRN_DIM, (hd + 1) * HGRN_DIM) for hd in range(HEADS_PER_STEP)]
        sts = [state_scr[hd] for hd in range(HEADS_PER_STEP)]
        for hd in range(HEADS_PER_STEP):
            st_ref[hd] = sts[hd]
        outs = [_hgrn_head_out(p, lanes, st, v[:, lanes], mask)[0] for lanes, st in zip(heads, sts)]
        for hd, lanes in enumerate(heads):
            state_scr[hd] = sts[hd] * p.egend[:, lanes] + _bdot(v[:, lanes], p.kdec[:, lanes], TN_DIMS)
        for lanes, o in zip(heads, outs):
            rs = lax.rsqrt(jnp.mean(o * o, axis=-1, keepdims=True) + NORM_EPS)
            yh_ref[:, lanes] = (o * rs * gate[:, lanes]).astype(BF16)

    return pl.pallas_call(
        body, name="hgrn_fwd", grid=(B_loc, N_HEAD_GROUPS, nc), in_specs=in_specs,
        out_specs=[pl.BlockSpec((CHUNK, GROUP_WIDTH), lambda *g: (rows(g), grp_of(g))),
                   pl.BlockSpec((None, None, HEADS_PER_STEP, HGRN_DIM, HGRN_DIM), lambda b, grp, n: (b, n, grp, 0, 0))],
        out_shape=[jax.ShapeDtypeStruct((T, HGRN_WIDTH), BF16),
                   jax.ShapeDtypeStruct((B_loc, nc, HGRN_HEADS, HGRN_DIM, HGRN_DIM), F32)],
        scratch_shapes=[pltpu.VMEM((HEADS_PER_STEP, HGRN_DIM, HGRN_DIM), F32), pltpu.VMEM((CHUNK, GROUP_WIDTH), F32)],
        compiler_params=_params(("parallel", "parallel", "arbitrary")),
    )(proj, proj, proj, proj, lb, gain)


def _hgrn_bwd(proj, states, lb, gain, dyh, B_loc, S):
    T = B_loc * S
    nc = S // CHUNK
    in_specs, rows, grp_of = _hgrn_specs(nc, lambda b, grp, i: (b, grp, nc - 1 - i))
    blk = pl.BlockSpec((CHUNK, GROUP_WIDTH), lambda *g: (rows(g), grp_of(g)))
    st_spec = pl.BlockSpec((None, None, HEADS_PER_STEP, HGRN_DIM, HGRN_DIM), lambda b, grp, i: (b, nc - 1 - i, grp, 0, 0))
    acc_spec = pl.BlockSpec((None, 1, GROUP_WIDTH), lambda b, grp, i: (b, 0, grp))

    def body(q_ref, f_ref, v_ref, hg_ref, lb_ref, gain_ref, st_ref, dyh_ref,
             dq_ref, df_ref, dv_ref, dhg_ref, dlb_ref, dgain_ref, dstate_scr, g_scr, dg_scr):
        @pl.when(pl.program_id(2) == 0)
        def _():
            dstate_scr[...] = jnp.zeros_like(dstate_scr)
            dlb_ref[...] = jnp.zeros_like(dlb_ref)
            dgain_ref[...] = jnp.zeros_like(dgain_ref)

        qr, lb, gain, hg, v = q_ref[...], lb_ref[...], gain_ref[...], hg_ref[...], v_ref[...]
        p = _HgrnPre(f_ref[...], qr, lb, g_scr)
        sgh = jax.nn.sigmoid(hg)
        sil = hg * sgh
        dy = dyh_ref[...].astype(F32)
        mask = _hgrn_pair_mask()
        dqg, dqd, dkdec, dv, dhg, dgend, dgain = [], [], [], [], [], [], []
        dkd = [[] for _ in range(NSUB)]
        heads = [slice(hd * HGRN_DIM, (hd + 1) * HGRN_DIM) for hd in range(HEADS_PER_STEP)]
        sts = [st_ref[hd] for hd in range(HEADS_PER_STEP)]
        dnews = [dstate_scr[hd] for hd in range(HEADS_PER_STEP)]
        fwd = [_hgrn_head_out(p, lanes, st, v[:, lanes], mask) for lanes, st in zip(heads, sts)]
        for lanes, st, dnew in zip(heads, sts, dnews):
            dkdec_h = _bdot(v[:, lanes], dnew)
            dkdec.append(dkdec_h)
            dgend.append(jnp.sum(dkdec_h * p.kdec[:, lanes], axis=0, keepdims=True)
                         + jnp.sum(dnew * st, axis=0, keepdims=True) * p.egend[:, lanes])
        dos = []
        gate_grad = sgh * (1.0 + hg * (1.0 - sgh))
        for lanes, (o, _) in zip(heads, fwd):
            rs = lax.rsqrt(jnp.mean(o * o, axis=-1, keepdims=True) + NORM_EPS)
            n = o * rs
            dyn = dy[:, lanes] * n
            dgain.append(jnp.sum(dyn * sil[:, lanes], axis=0, keepdims=True))
            dhg.append(dyn * gain[:, lanes] * gate_grad[:, lanes])
            dn = dy[:, lanes] * gain[:, lanes] * sil[:, lanes]
            dos.append(rs * (dn - n * jnp.mean(dn * n, axis=-1, keepdims=True)))
        drs = []
        for hd, (lanes, st, dnew, do, (_, (qg, qd, kall, am, vst))) in enumerate(zip(heads, sts, dnews, dos, fwd)):
            dqg.append(_bdot(do, st))
            dstate_scr[hd] = _bdot(do, qg, TN_DIMS) + dnew * p.egend[:, lanes]
            drs.append(jnp.where(mask, _bdot(do, vst, NT_DIMS), 0.0))
            dvst = _bdot(am, do, TN_DIMS)
            dv.append(sum(dvst[i * CHUNK:(i + 1) * CHUNK] for i in range(NSUB)) + _bdot(p.kdec[:, lanes], dnew, NT_DIMS))
        for dr, (_, (qg, qd, kall, am, vst)) in zip(drs, fwd):
            dqd.append(_bdot(dr, kall))
            dkall = _bdot(dr, qd, TN_DIMS)
            for i in range(NSUB):
                dkd[i].append(dkall[i * CHUNK:(i + 1) * CHUNK])

        wide = lambda parts: jnp.concatenate(parts, axis=1)
        dqg, dqd, dkdec = wide(dqg), wide(dqd), wide(dkdec)
        t2 = dqd * p.qd
        dg = dqg * p.qg + t2 - dkdec * p.kdec
        dk = dkdec * p.ekdec
        dg_scr[...] = jnp.zeros_like(dg_scr)
        for i in range(NSUB):
            dkd_i = wide(dkd[i])
            tk = jnp.where(p.row < (i + 1) * SUB, dkd_i * p.kd[i], 0.0)
            dg = dg - tk
            dk = dk + dkd_i * p.ekd[i]
            if i >= 1:
                in_blk = (p.row >= i * SUB) & (p.row < (i + 1) * SUB)
                dg_scr[pl.ds(i * SUB - 1, 1), :] = (jnp.sum(tk, axis=0, keepdims=True)
                                                    - jnp.sum(jnp.where(in_blk, t2, 0.0), axis=0, keepdims=True))
        dg_scr[pl.ds(CHUNK - 1, 1), :] = wide(dgend)
        t = lax.broadcasted_iota(jnp.int32, (CHUNK, CHUNK), 0)
        s = lax.broadcasted_iota(jnp.int32, (CHUNK, CHUNK), 1)
        dlogf = jnp.dot((t <= s).astype(F32), dg + dg_scr[...], precision=lax.Precision.HIGHEST, preferred_element_type=F32)
        df = dlogf / p.f - dk
        df_ref[...] = (df * (1.0 - lb) * p.sg * (1.0 - p.sg)).astype(BF16)
        dlb_ref[...] += jnp.sum(df * (1.0 - p.sg), axis=0, keepdims=True)
        dq_ref[...] = ((dqg * p.eg + dqd * p.eqd) * p.sigq * (1.0 + qr * (1.0 - p.sigq))).astype(BF16)
        dv_ref[...] = wide(dv).astype(BF16)
        dhg_ref[...] = wide(dhg).astype(BF16)
        dgain_ref[...] += wide(dgain)

    o = jax.ShapeDtypeStruct((T, HGRN_WIDTH), BF16)
    acc = jax.ShapeDtypeStruct((B_loc, 1, HGRN_WIDTH), F32)
    return pl.pallas_call(
        body, name="hgrn_bwd", grid=(B_loc, N_HEAD_GROUPS, nc), in_specs=in_specs + [st_spec, blk],
        out_specs=[blk, blk, blk, blk, acc_spec, acc_spec], out_shape=[o, o, o, o, acc, acc],
        scratch_shapes=[pltpu.VMEM((HEADS_PER_STEP, HGRN_DIM, HGRN_DIM), F32), pltpu.VMEM((CHUNK, GROUP_WIDTH), F32),
                        pltpu.VMEM((CHUNK, GROUP_WIDTH), F32)],
        compiler_params=_params(("parallel", "parallel", "arbitrary")),
    )(proj, proj, proj, proj, lb, gain, states, dyh)


def _adamw(w, g, m, v, name):
    R, C = w.shape
    tr = _pick(R, (128, 64, 32, 16, 8)) if C > 1024 else _pick(R, (512, 256, 128, 64, 32, 16, 8))

    def body(w_ref, g_ref, m_ref, v_ref, d_ref, nm_ref, nv_ref):
        gv = g_ref[...]
        nm = ADAM_B1 * m_ref[...] + (1.0 - ADAM_B1) * gv
        nv = ADAM_B2 * v_ref[...] + (1.0 - ADAM_B2) * (gv * gv)
        m_hat = nm / (1.0 - ADAM_B1 ** ADAM_STEP)
        v_hat = nv / (1.0 - ADAM_B2 ** ADAM_STEP)
        d_ref[...] = -ADAM_LR * (m_hat / (jnp.sqrt(v_hat) + ADAM_EPS) + ADAM_WD * w_ref[...])
        nm_ref[...] = nm
        nv_ref[...] = nv

    blk = pl.BlockSpec((tr, C), lambda i: (i, 0))
    o = jax.ShapeDtypeStruct((R, C), F32)
    return pl.pallas_call(
        body, name=name, grid=(R // tr,), in_specs=[blk] * 4, out_specs=[blk] * 3, out_shape=[o, o, o],
        compiler_params=_params(("parallel",)),
    )(w, g, m, v)


ANY = pl.BlockSpec(memory_space=pl.ANY)
VMEM_SPEC = pl.BlockSpec(memory_space=pltpu.VMEM)


def _place():
    x, y, c = lax.axis_index("x"), lax.axis_index("y"), lax.axis_index("c")
    other_chips = [(1 - x, y), (x, 1 - y), (1 - x, 1 - y)]
    return x, y, c, other_chips


def _cast_into_full(w, ax, s_arr, name):
    R, C = w.shape
    tr = _pick(R, (256, 128))
    nr = R // tr

    def body(s_ref, w_ref, o_ref):
        o_ref[...] = w_ref[...].astype(BF16)

    if ax == 1:
        shape, o_map = (R, N_CHIPS * C), lambda i, s: (i, s[0])
    else:
        shape, o_map = (N_CHIPS * R, C), lambda i, s: (s[0] * nr + i, 0)
    return pl.pallas_call(
        body, name=name,
        grid_spec=pltpu.PrefetchScalarGridSpec(
            num_scalar_prefetch=1, grid=(nr,), in_specs=[pl.BlockSpec((tr, C), lambda i, s: (i, 0))],
            out_specs=pl.BlockSpec((tr, C), o_map)),
        out_shape=jax.ShapeDtypeStruct(shape, BF16),
        compiler_params=_params(("parallel",)),
    )(s_arr, w)


class _Gather:
    def __init__(self, fulls, shard_shapes, axes, tag, after=None):
        self.shapes, self.axes, self.tag, self.nw = shard_shapes, axes, tag, len(fulls)
        nw = self.nw

        def body(*refs):
            ins, sems = refs[:nw], refs[nw + (after is not None):nw + (after is not None) + 2 * (N_CHIPS - 1)]
            for j in range(N_CHIPS - 1):
                for cp in self._peer_copies(ins, sems[2 * j], sems[2 * j + 1], j):
                    cp.start()
            refs[-1][...] = jnp.zeros_like(refs[-1])

        out = pl.pallas_call(
            body, name="gather_start_" + tag,
            out_shape=(*[pltpu.SemaphoreType.DMA((nw,))] * (2 * (N_CHIPS - 1)),
                       *[pltpu.HBM(f.shape, f.dtype) for f in fulls], jax.ShapeDtypeStruct((8, LANES), F32)),
            in_specs=[HBM_SPEC] * nw + ([] if after is None else [ANY]),
            out_specs=(*[SEM_SPEC] * (2 * (N_CHIPS - 1)), *[HBM_SPEC] * nw, VMEM_SPEC),
            input_output_aliases={k: 2 * (N_CHIPS - 1) + k for k in range(nw)},
            compiler_params=pltpu.CompilerParams(has_side_effects=DATAFLOW),
        )(*[pltpu.with_memory_space_constraint(f, pltpu.HBM) for f in fulls], *(() if after is None else (after,)))
        self.sems = [(out[2 * j], out[2 * j + 1]) for j in range(N_CHIPS - 1)]
        self.fulls = list(out[2 * (N_CHIPS - 1):2 * (N_CHIPS - 1) + nw])
        self.token = out[-1]

    def _region(self, ref, i, t, half):
        R, C = self.shapes[i]
        hr = R // 2
        if self.axes[i] == 1:
            return ref.at[pl.ds(half * hr, hr), pl.ds(pl.multiple_of(t * C, LANES), C)]
        return ref.at[pl.ds(t * R + half * hr, hr), :]

    def _peer_copies(self, refs, send_sems, recv_sems, j):
        x, y, c, chips = _place()
        s = 2 * x + y
        return [pltpu.make_async_remote_copy(
            src_ref=self._region(refs[i], i, s, c), dst_ref=self._region(refs[i], i, s, c), send_sem=send_sems.at[i],
            recv_sem=recv_sems.at[i], device_id=(*chips[j], c), device_id_type=MESH) for i in range(self.nw)]

    def wait(self, peers, after):
        nw, np_ = self.nw, len(peers)

        def body(*refs):
            ins, sems = refs[:nw], refs[nw:nw + 2 * np_]
            for k, j in enumerate(peers):
                for cp in self._peer_copies(ins, sems[2 * k], sems[2 * k + 1], j):
                    cp.wait_send()
                    cp.wait_recv()

        sem_args = [s for j in peers for s in self.sems[j]]
        out = pl.pallas_call(
            body, name="gather_wait_%s_%s" % (self.tag, "".join(map(str, peers))),
            out_shape=tuple(pltpu.HBM(f.shape, f.dtype) for f in self.fulls),
            in_specs=[HBM_SPEC] * nw + [SEM_SPEC] * (2 * np_) + [ANY], out_specs=tuple([HBM_SPEC] * nw),
            input_output_aliases={k: k for k in range(nw)},
            compiler_params=pltpu.CompilerParams(has_side_effects=DATAFLOW),
        )(*self.fulls, *sem_args, after)
        self.fulls = list(out)

    def forward(self, peers):
        nw, np_ = self.nw, len(peers)

        def body(*refs):
            ins, outs = refs[:nw], refs[nw:2 * nw]
            send_sems, recv_sems = refs[2 * nw:]
            x, y, c, chips = _place()
            cps = []
            for i in range(nw):
                for k, j in enumerate(peers):
                    t = 2 * chips[j][0] + chips[j][1]
                    cp = pltpu.make_async_remote_copy(
                        src_ref=self._region(ins[i], i, t, c), dst_ref=self._region(outs[i], i, t, c),
                        send_sem=send_sems.at[i * np_ + k], recv_sem=recv_sems.at[i * np_ + k],
                        device_id=(x, y, 1 - c), device_id_type=MESH)
                    cp.start()
                    cps.append(cp)
            for cp in cps:
                cp.wait()

        out = pl.pallas_call(
            body, name="gather_forward_%s_%s" % (self.tag, "".join(map(str, peers))),
            in_specs=[ANY] * nw, out_specs=[ANY] * nw,
            out_shape=[jax.ShapeDtypeStruct(f.shape, f.dtype) for f in self.fulls],
            input_output_aliases={i: i for i in range(nw)},
            scratch_shapes=[pltpu.SemaphoreType.DMA((nw * np_,)), pltpu.SemaphoreType.DMA((nw * np_,))],
        )(*self.fulls)
        self.fulls = list(out)


def _matmul_slab(a, wfull, slab_arr, prev, after, name):
    M, K = a.shape
    N = wfull.shape[1]
    nslab = N // N_CHIPS
    tm = _pick(M, (1024, 512, 256, 128))
    tn = _pick(nslab, (896, 512, 384, 256, 128))
    per = nslab // tn
    extra = [e for e in (prev, after) if e is not None]

    def body(slab_ref, a_ref, b_ref, *rest):
        rest[len(extra)][...] = jnp.dot(a_ref[...], b_ref[...], preferred_element_type=F32)

    return pl.pallas_call(
        body, name=name,
        grid_spec=pltpu.PrefetchScalarGridSpec(
            num_scalar_prefetch=1, grid=(M // tm, per),
            in_specs=[pl.BlockSpec((tm, K), lambda i, j, sl: (i, 0)),
                      pl.BlockSpec((K, tn), lambda i, j, sl: (0, sl[0] * per + j))] + [ANY] * len(extra),
            out_specs=pl.BlockSpec((tm, tn), lambda i, j, sl: (i, sl[0] * per + j))),
        out_shape=jax.ShapeDtypeStruct((M, N), F32),
        input_output_aliases={} if prev is None else {3: 0},
        compiler_params=_params(("parallel", "arbitrary")),
    )(slab_arr, a, wfull, *extra)


def _exchange_sibling_halves(gs, name):
    nw = len(gs)

    def body(*refs):
        ins, outs = refs[:nw], refs[nw:2 * nw]
        send_sems, recv_sems = refs[2 * nw:]
        x, y, c, _ = _place()
        cps = []
        for i in range(nw):
            cp = pltpu.make_async_remote_copy(src_ref=ins[i].at[:, 1 - c], dst_ref=outs[i], send_sem=send_sems.at[i],
                                              recv_sem=recv_sems.at[i], device_id=(x, y, 1 - c), device_id_type=MESH)
            cp.start()
            cps.append(cp)
        for cp in cps:
            cp.wait()

    return pl.pallas_call(
        body, name=name, in_specs=[ANY] * nw, out_specs=[ANY] * nw,
        out_shape=[jax.ShapeDtypeStruct((g.shape[0],) + g.shape[2:], g.dtype) for g in gs],
        scratch_shapes=[pltpu.SemaphoreType.DMA((nw,)), pltpu.SemaphoreType.DMA((nw,))],
    )(*gs)


HBM_SPEC = pl.BlockSpec(memory_space=pltpu.HBM)
SEM_SPEC = pl.BlockSpec(memory_space=pltpu.SEMAPHORE)
DATAFLOW = pltpu.SideEffectType.DATAFLOW_SIDE_EFFECTING


def _chip_copies(ins, lands, send_sems, recv_sems):
    x, y, c, chips = _place()
    return [pltpu.make_async_remote_copy(
        src_ref=ins[i].at[2 * chip[0] + chip[1]], dst_ref=lands[i].at[j], send_sem=send_sems.at[i * 3 + j],
        recv_sem=recv_sems.at[i * 3 + j], device_id=(*chip, c), device_id_type=MESH)
        for i in range(len(ins)) for j, chip in enumerate(chips)]


def _chips_send_start(ss, name):
    nw = len(ss)
    lands = [pltpu.with_memory_space_constraint(lax.empty((N_CHIPS - 1,) + s.shape[1:], s.dtype), pltpu.HBM) for s in ss]

    def body(*refs):
        ins, land_refs = refs[:nw], refs[nw:2 * nw]
        send_sems, recv_sems = refs[2 * nw], refs[2 * nw + 1]
        token = refs[-1]
        for cp in _chip_copies(ins, land_refs, send_sems, recv_sems):
            cp.start()
        token[...] = jnp.zeros_like(token)

    n = 3 * nw
    out = pl.pallas_call(
        body, name=name,
        out_shape=(pltpu.SemaphoreType.DMA((n,)), pltpu.SemaphoreType.DMA((n,)),
                   *[pltpu.HBM(s.shape, s.dtype) for s in ss], *[pltpu.HBM(l.shape, l.dtype) for l in lands],
                   jax.ShapeDtypeStruct((8, LANES), F32)),
        in_specs=[HBM_SPEC] * (2 * nw), out_specs=(SEM_SPEC, SEM_SPEC, *[HBM_SPEC] * (2 * nw), VMEM_SPEC),
        input_output_aliases={k: 2 + k for k in range(2 * nw)},
        compiler_params=pltpu.CompilerParams(has_side_effects=DATAFLOW),
    )(*[pltpu.with_memory_space_constraint(s, pltpu.HBM) for s in ss], *lands)
    return out[0], out[1], list(out[2:2 + nw]), list(out[2 + nw:2 + 2 * nw]), out[-1]


def _chips_send_wait(send_sems, recv_sems, ss, lands, after, name):
    nw = len(ss)

    def body(*refs):
        ins, land_refs = refs[:nw], refs[nw:2 * nw]
        s_sems, r_sems = refs[2 * nw], refs[2 * nw + 1]
        for cp in _chip_copies(ins, land_refs, s_sems, r_sems):
            cp.wait_send()
            cp.wait_recv()

    out = pl.pallas_call(
        body, name=name,
        out_shape=(*[pltpu.HBM(s.shape, s.dtype) for s in ss], *[pltpu.HBM(l.shape, l.dtype) for l in lands]),
        in_specs=[HBM_SPEC] * (2 * nw) + [SEM_SPEC, SEM_SPEC, ANY], out_specs=tuple([HBM_SPEC] * (2 * nw)),
        input_output_aliases={k: k for k in range(2 * nw)},
        compiler_params=pltpu.CompilerParams(has_side_effects=DATAFLOW),
    )(*ss, *lands, send_sems, recv_sems, after)
    return list(out[nw:])


def _sum_small(pack):
    rows = pack.shape[0]

    def body(pack_ref, sum_ref, all_ref, send_sems, recv_sems):
        x, y, c, _ = _place()
        me = 4 * x + 2 * y + c
        all_ref[me] = pack_ref[...]
        cps = []
        for k in range(1, N_DEV):
            to = (1 - x if k & 4 else x, 1 - y if k & 2 else y, 1 - c if k & 1 else c)
            cp = pltpu.make_async_remote_copy(
                src_ref=pack_ref, dst_ref=all_ref.at[me], send_sem=send_sems.at[k - 1],
                recv_sem=recv_sems.at[k - 1], device_id=to, device_id_type=MESH)
            cp.start()
            cps.append(cp)
        for cp in cps:
            cp.wait()
        total = all_ref[0]
        for d in range(1, N_DEV):
            total = total + all_ref[d]
        sum_ref[...] = total

    return pl.pallas_call(
        body, name="sum_small_grads", in_specs=[VMEM_SPEC], out_specs=VMEM_SPEC,
        out_shape=jax.ShapeDtypeStruct(pack.shape, F32),
        scratch_shapes=[pltpu.VMEM((N_DEV, rows, LANES), F32), pltpu.SemaphoreType.DMA((N_DEV - 1,)),
                        pltpu.SemaphoreType.DMA((N_DEV - 1,))],
    )(pack)


def _share_with_sibling(fs, name):
    nw = len(fs)

    def body(*refs):
        ins, outs = refs[:nw], refs[nw:2 * nw]
        send_sems, recv_sems = refs[2 * nw:]
        x, y, c, _ = _place()
        cps = []
        for i in range(nw):
            cp = pltpu.make_async_remote_copy(src_ref=ins[i].at[c], dst_ref=outs[i].at[c], send_sem=send_sems.at[i],
                                              recv_sem=recv_sems.at[i], device_id=(x, y, 1 - c), device_id_type=MESH)
            cp.start()
            cps.append(cp)
        for cp in cps:
            cp.wait()

    return pl.pallas_call(
        body, name=name, in_specs=[ANY] * nw, out_specs=[ANY] * nw,
        out_shape=[jax.ShapeDtypeStruct(f.shape, f.dtype) for f in fs],
        input_output_aliases={i: i for i in range(nw)},
        scratch_shapes=[pltpu.SemaphoreType.DMA((nw,)), pltpu.SemaphoreType.DMA((nw,))],
    )(*fs)


def _sum_sibling(g, land, c_arr, name):
    _, _, P, Q = g.shape
    tp = _pick(P, (256, 128, 64))

    def body(c_ref, g_ref, l_ref, s_ref):
        s_ref[...] = (g_ref[...].astype(F32) + l_ref[...].astype(F32)).astype(BF16)

    return pl.pallas_call(
        body, name=name,
        grid_spec=pltpu.PrefetchScalarGridSpec(
            num_scalar_prefetch=1, grid=(N_CHIPS, P // tp),
            in_specs=[pl.BlockSpec((None, None, tp, Q), lambda t, r, cr: (t, cr[0], r, 0)),
                      pl.BlockSpec((None, tp, Q), lambda t, r, cr: (t, r, 0))],
            out_specs=pl.BlockSpec((None, tp, Q), lambda t, r, cr: (t, r, 0))),
        out_shape=jax.ShapeDtypeStruct((N_CHIPS, P, Q), BF16),
        compiler_params=_params(("parallel", "parallel")),
    )(c_arr, g, land)


def _sum_chips(g, land, recv, sc_arr, name):
    _, _, P, Q = g.shape
    tp = _pick(P, (256, 128, 64))

    def body(sc_ref, g_ref, l_ref, r_ref, f_ref):
        acc = g_ref[...].astype(F32) + l_ref[...].astype(F32)
        for j in range(N_CHIPS - 1):
            acc = acc + r_ref[j].astype(F32)
        f_ref[...] = acc

    return pl.pallas_call(
        body, name=name,
        grid_spec=pltpu.PrefetchScalarGridSpec(
            num_scalar_prefetch=1, grid=(P // tp,),
            in_specs=[pl.BlockSpec((None, None, tp, Q), lambda r, sc: (sc[0], sc[1], r, 0)),
                      pl.BlockSpec((None, tp, Q), lambda r, sc: (sc[0], r, 0)),
                      pl.BlockSpec((N_CHIPS - 1, tp, Q), lambda r, sc: (0, r, 0))],
            out_specs=pl.BlockSpec((None, tp, Q), lambda r, sc: (sc[1], r, 0))),
        out_shape=jax.ShapeDtypeStruct((2, P, Q), F32),
        compiler_params=_params(("parallel",)),
    )(sc_arr, g, land, recv)


class _Reduction:
    def __init__(self, names, pieces, lands, flight):
        self.names, self.pieces, self.lands = names, pieces, lands
        self.send_sems, self.recv_sems, self.sums, self.zones, self.token = flight


def _reduce_start(grads, names, c_arr):
    pieces = [g.reshape(N_CHIPS, 2, -1, g.shape[-1]) for g in grads]
    tag = names[0] if len(names) == 1 else "branches"
    lands = _exchange_sibling_halves(pieces, "grads_to_sibling_" + tag)
    sums = [_sum_sibling(g, l, c_arr, "sum_sibling_" + nm) for g, l, nm in zip(pieces, lands, names)]
    return _Reduction(names, pieces, lands, _chips_send_start(sums, "grads_to_chips_start_" + tag))


def _reduce_finish(red, after, sc_arr):
    tag = red.names[0] if len(red.names) == 1 else "branches"
    recvs = _chips_send_wait(red.send_sems, red.recv_sems, red.sums, red.zones, after, "grads_to_chips_wait_" + tag)
    halves = [_sum_chips(g, l, r, sc_arr, "sum_chips_" + nm) for g, l, r, nm in zip(red.pieces, red.lands, recvs, red.names)]
    return _share_with_sibling(halves, "grads_share_sibling_" + tag)


def _t5_bucket(dist):
    max_exact = REL_BUCKETS // 2
    d = jnp.maximum(dist, 0)
    df = jnp.maximum(d, 1).astype(F32)
    large = max_exact + (jnp.log(df / max_exact) / math.log(REL_MAX_DIST / max_exact)
                         * (REL_BUCKETS - max_exact)).astype(jnp.int32)
    large = jnp.minimum(large, REL_BUCKETS - 1)
    return jnp.where(d < max_exact, d, large)


def _bucket_table():
    qi = jnp.arange(WINDOW)[:, None]
    si = jnp.arange(2 * WINDOW)[None, :]
    return _t5_bucket(qi + WINDOW - si)


TILE_WORDS = 8 * LANES


def _tile_rows(shape):
    return -(-math.prod(shape) // TILE_WORDS) * 8


def _rows_of(a):
    flat = a.reshape(-1).astype(F32)
    n = _tile_rows(a.shape) * LANES
    return jnp.pad(flat, (0, n - flat.shape[0])).reshape(-1, LANES)


def _pack_rows(parts):
    return jnp.concatenate([_rows_of(p) for p in parts], axis=0)


def _unpack_rows(packed, shapes):
    out, at = [], 0
    for shp in shapes:
        n, nr = math.prod(shp), _tile_rows(shp)
        out.append(packed[at:at + nr].reshape(-1)[:n].reshape(shp))
        at += nr
    return out


def kernel(x, norm_pre, w_in, rel_bias, attn_sinks, lb_logits, hgrn_norm, w_branch_attn, w_branch_hgrn, w_out, norm_post, loss_target, m_norm_pre, m_w_in, m_rel_bias, m_attn_sinks, m_lb_logits, m_hgrn_norm, m_w_branch_attn, m_w_branch_hgrn, m_w_out, m_norm_post, v_norm_pre, v_w_in, v_rel_bias, v_attn_sinks, v_lb_logits, v_hgrn_norm, v_w_branch_attn, v_w_branch_hgrn, v_w_out, v_norm_post):
    B_loc, S, D = x.shape
    T = B_loc * S
    x2 = x.reshape(T, D)
    tgt2 = loss_target.reshape(T, D)
    my_x, my_y, my_c = lax.axis_index("x"), lax.axis_index("y"), lax.axis_index("c")

    c_arr = jnp.reshape(my_c, (1,)).astype(jnp.int32)
    s_arr = jnp.reshape(2 * my_x + my_y, (1,)).astype(jnp.int32)
    sc_arr = jnp.concatenate([s_arr, c_arr])
    shard_ws = [w_in[0], w_branch_attn[0], w_branch_hgrn[0], w_out[0]]
    shard_axes = (1, 1, 1, 0)
    names = ["w_in", "w_branch_attn", "w_branch_hgrn", "w_out"]
    placed = [_cast_into_full(w, ax, s_arr, "cast_" + nm) for w, ax, nm in zip(shard_ws, shard_axes, names)]
    peer_slabs = [jnp.reshape(t, (1,)).astype(jnp.int32)
                  for t in (2 * (1 - my_x) + my_y, 2 * my_x + 1 - my_y, 2 * (1 - my_x) + 1 - my_y)]

    buckets = _bucket_table()
    onehot = (buckets.reshape(-1)[:, None] == jnp.arange(REL_BUCKETS)[None, :]).astype(F32)
    bias_tab = jnp.dot(onehot, rel_bias.astype(F32), precision=lax.Precision.HIGHEST).T.reshape(ATTN_HEADS, WINDOW, 2 * WINDOW)
    sinks_b = jnp.broadcast_to(attn_sinks[0].astype(F32)[:, None, None], (ATTN_HEADS, 8, LANES))
    lb_fn = lambda l: jnp.cumsum(jax.nn.softmax(l.astype(F32), axis=0), axis=0)[:1]
    lb, lb_vjp = jax.vjp(lb_fn, lb_logits)
    gain_h = hgrn_norm[0].reshape(1, HGRN_WIDTH)

    h, rstd = _rmsnorm_fwd(x2, norm_pre)
    gather_in = _Gather(placed[:1], [shard_ws[0].shape], shard_axes[:1], "w_in")
    proj = _matmul_slab(h, gather_in.fulls[0], s_arr, None, gather_in.token, "in_proj_own")
    for j in range(N_CHIPS - 1):
        gather_in.wait([j], proj)
        gather_in.forward([j])
        after = None
        if j == N_CHIPS - 2:
            gather_rest = _Gather(placed[1:], [w.shape for w in shard_ws[1:]], shard_axes[1:], "rest", gather_in.fulls[0])
            after = gather_rest.token
        proj = _matmul_slab(h, gather_in.fulls[0], peer_slabs[j], proj, after, "in_proj_peer%d" % j)
    win_f = gather_in.fulls[0]
    ya = _attn_fwd(proj, bias_tab, sinks_b, B_loc, S)
    yh, states = _hgrn_fwd(proj, lb, gain_h, B_loc, S)
    gather_rest.wait([0, 1, 2], yh)
    gather_rest.forward([0, 1, 2])
    wba_f, wbh_f, wout_f = gather_rest.fulls
    ua = _matmul(ya, wba_f, "nn", F32, "branch_attn_proj")
    uh = _matmul(yh, wbh_f, "nn", F32, "branch_hgrn_proj")
    merged = _merge_fwd(proj, ua, uh)
    yv = _matmul(merged, wout_f, "nn", F32, "out_proj")
    dy, dout, loss_p, gnpost_p = _post_loss(yv, x2, tgt2, norm_post)

    dmerged = _matmul(dy, wout_f, "nt", F32, "out_proj_dgrad")
    g_wout = _matmul(merged, dy, "tn", BF16, "out_proj_wgrad")
    d_ua, d_uh, d_ga, d_gh = _merge_bwd(proj, ua, uh, dmerged)
    g_wba = _matmul(ya, d_ua, "tn", BF16, "branch_attn_wgrad", slabs=N_CHIPS)
    g_wbh = _matmul(yh, d_uh, "tn", BF16, "branch_hgrn_wgrad", slabs=N_CHIPS)
    late = _reduce_start([g_wba, g_wbh, g_wout], names[1:], c_arr)
    d_ya = _matmul(d_ua, wba_f, "nt", BF16, "branch_attn_dgrad", after=late.token)
    d_yh = _matmul(d_uh, wbh_f, "nt", BF16, "branch_hgrn_dgrad", after=late.token)
    dq, dk, dv, dag, dbias_p, dsinks_p = _attn_bwd(proj, bias_tab, sinks_b, d_ya, B_loc, S)
    dhq, dhf, dhi, dhg, dlb_p, dgain_p = _hgrn_bwd(proj, states, lb, gain_h, d_yh, B_loc, S)
    dproj = jnp.concatenate([dq, dk, dv, dag, dhq, dhf, dhi, dhg, d_ga, d_gh], axis=1)
    g_win = _matmul(h, dproj, "tn", BF16, "in_proj_wgrad", slabs=N_CHIPS)
    last = _reduce_start([g_win], names[:1], c_arr)
    dh = _matmul(dproj, win_f, "nt", F32, "in_proj_dgrad", after=last.token)
    grad_x2, gnpre_p = _rmsnorm_bwd(dh, x2, rstd, norm_pre, dout)
    shared = _reduce_finish(last, grad_x2, sc_arr) + _reduce_finish(late, grad_x2, sc_arr)

    grelb_p = jnp.dot(dbias_p.reshape(ATTN_HEADS, -1), onehot, precision=lax.Precision.HIGHEST).T
    gsinks_p = dsinks_p[:, 0, 0]
    dlb_sum = jnp.sum(dlb_p, axis=0).reshape(1, HGRN_WIDTH)
    ghn_p = jnp.sum(dgain_p, axis=0).reshape(HGRN_HEADS, HGRN_DIM)
    small_parts = [gnpre_p, gnpost_p, grelb_p, gsinks_p, dlb_sum, ghn_p, loss_p]
    small_shapes = [p.shape for p in small_parts]
    pack_sum = _sum_small(_pack_rows(small_parts))
    big_w = [w_in, w_branch_attn, w_branch_hgrn, w_out]
    big_m = [m_w_in, m_w_branch_attn, m_w_branch_hgrn, m_w_out]
    big_v = [v_w_in, v_w_branch_attn, v_w_branch_hgrn, v_w_out]
    big = {}
    for nm, gs, w, m, v in zip(names, shared, big_w, big_m, big_v):
        shp = w.shape
        g2 = gs.reshape(shp[1], shp[2])
        d, nm_, nv_ = _adamw(w[0], g2, m[0], v[0], "adamw_" + nm)
        big[nm] = tuple(a.reshape(shp) for a in (g2, d, nm_, nv_))

    gnpre, gnpost, grelb, gsinks, dlb, ghn, loss = _unpack_rows(pack_sum, small_shapes)
    (g_lb_logits,) = lb_vjp(dlb)
    small_names = ["norm_pre", "rel_bias", "attn_sinks", "lb_logits", "hgrn_norm", "norm_post"]
    small_w = [norm_pre, rel_bias, attn_sinks, lb_logits, hgrn_norm, norm_post]
    small_m = [m_norm_pre, m_rel_bias, m_attn_sinks, m_lb_logits, m_hgrn_norm, m_norm_post]
    small_v = [v_norm_pre, v_rel_bias, v_attn_sinks, v_lb_logits, v_hgrn_norm, v_norm_post]
    small_g = [gnpre.reshape(norm_pre.shape), grelb.reshape(rel_bias.shape), gsinks.reshape(attn_sinks.shape),
               g_lb_logits.reshape(lb_logits.shape), ghn.reshape(hgrn_norm.shape), gnpost.reshape(norm_post.shape)]
    shapes = [w.shape for w in small_w]
    d_s, nm_s, nv_s = _adamw(_pack_rows(small_w), _pack_rows(small_g), _pack_rows(small_m), _pack_rows(small_v),
                             "adamw_small")
    small = {}
    for nm, g, d, m_, v_ in zip(small_names, small_g, _unpack_rows(d_s, shapes), _unpack_rows(nm_s, shapes),
                                _unpack_rows(nv_s, shapes)):
        small[nm] = (g, d, m_, v_)

    allw = {**big, **small}
    order = ["norm_pre", "w_in", "rel_bias", "attn_sinks", "lb_logits", "hgrn_norm", "w_branch_attn", "w_branch_hgrn",
             "w_out", "norm_post"]
    outs = [loss.reshape(()), grad_x2.reshape(B_loc, S, D)]
    for k in range(4):
        outs += [allw[nm][k] for nm in order]
    return tuple(outs)
```

```python
import functools
import math

import jax
import jax.numpy as jnp
from jax import lax
from jax.experimental import pallas as pl
from jax.experimental.pallas import tpu as pltpu

F32 = jnp.float32
BF16 = jnp.bfloat16
MESH = pl.DeviceIdType.MESH

ATTN_HEADS = 16
ATTN_KV_HEADS = 4
HEAD_DIM = 64
GROUP = ATTN_HEADS // ATTN_KV_HEADS
WINDOW = 128
ATTN_WIDTH = ATTN_HEADS * HEAD_DIM
KV_WIDTH = ATTN_KV_HEADS * HEAD_DIM
HGRN_HEADS = 8
HGRN_DIM = 128
HGRN_WIDTH = HGRN_HEADS * HGRN_DIM
CHUNK = 64
SUB = 16
NSUB = CHUNK // SUB
REL_BUCKETS = 32
REL_MAX_DIST = 128
NORM_EPS = 1e-6
ADAM_LR = 0.001
ADAM_B1 = 0.9
ADAM_B2 = 0.999
ADAM_EPS = 1e-08
ADAM_WD = 0.01
ADAM_STEP = 10
LANES = 128
N_CHIPS = 4
N_DEV = 8
VMEM_LIMIT = 48 * 1024 * 1024
MATMUL_OPERAND_BYTES = 20 * 1024 * 1024

OFF_AQ = 0
OFF_AK = OFF_AQ + ATTN_WIDTH
OFF_AV = OFF_AK + KV_WIDTH
OFF_AG = OFF_AV + KV_WIDTH
OFF_HQ = OFF_AG + ATTN_WIDTH
OFF_HF = OFF_HQ + HGRN_WIDTH
OFF_HI = OFF_HF + HGRN_WIDTH
OFF_HG = OFF_HI + HGRN_WIDTH
OFF_GA = OFF_HG + HGRN_WIDTH

NT_DIMS = (((1,), (1,)), ((), ()))
TN_DIMS = (((0,), (0,)), ((), ()))
NN_DIMS = (((1,), (0,)), ((), ()))


def _pick(n, cands):
    for c in cands:
        if n % c == 0:
            return c
    raise ValueError(f"no tile for {n} in {cands}")


def _params(sem):
    return pltpu.CompilerParams(dimension_semantics=sem, vmem_limit_bytes=VMEM_LIMIT)


def _bdot(a, b, dims=NN_DIMS):
    return lax.dot_general(a.astype(BF16), b.astype(BF16), dims, preferred_element_type=F32)


def _matmul(a, b, mode, out_dtype, name, slabs=1, after=None):
    if mode == "nn":
        (M, K), (K2, N) = a.shape, b.shape
    elif mode == "nt":
        (M, K), (N, K2) = a.shape, b.shape
    else:
        (K, M), (K2, N) = a.shape, b.shape
    assert K == K2
    nslab = N // slabs
    tm = _pick(M, (1024, 512, 256, 128))
    tn = _pick(nslab, (2688, 1024, 896, 512, 384, 256, 128))
    tk = _pick(K, [t for t in (4096, 2048, 1792, 1536, 1024, 512, 256, 128) if 4 * t * (tm + tn) <= MATMUL_OPERAND_BYTES])
    nk = K // tk
    per = nslab // tn
    dims = {"nn": NN_DIMS, "nt": NT_DIMS, "tn": TN_DIMS}[mode]

    n_in = 2 if after is None else 3

    def body(*refs):
        a_ref, b_ref, o_ref, acc = refs[0], refs[1], refs[n_in], refs[n_in + 1:]
        part = lax.dot_general(a_ref[...], b_ref[...], dims, preferred_element_type=F32)
        if nk == 1:
            o_ref[...] = part.astype(o_ref.dtype)
            return
        acc_ref, = acc
        k = pl.program_id(2)

        @pl.when(k == 0)
        def _():
            acc_ref[...] = part

        @pl.when((k > 0) & (k < nk - 1))
        def _():
            acc_ref[...] += part

        @pl.when(k == nk - 1)
        def _():
            o_ref[...] = (acc_ref[...] + part).astype(o_ref.dtype)

    if mode == "tn":
        a_spec = pl.BlockSpec((tk, tm), lambda i, j, k: (k, i))
    else:
        a_spec = pl.BlockSpec((tm, tk), lambda i, j, k: (i, k))
    if mode == "nt":
        b_spec = pl.BlockSpec((tn, tk), lambda i, j, k: (j, k))
    else:
        b_spec = pl.BlockSpec((tk, tn), lambda i, j, k: (k, j))
    if slabs == 1:
        o_shape = jax.ShapeDtypeStruct((M, N), out_dtype)
        o_spec = pl.BlockSpec((tm, tn), lambda i, j, k: (i, j))
    else:
        o_shape = jax.ShapeDtypeStruct((slabs, M, nslab), out_dtype)
        o_spec = pl.BlockSpec((None, tm, tn), lambda i, j, k: (j // per, i, j % per))
    return pl.pallas_call(
        body, name=name, grid=(M // tm, N // tn, nk), in_specs=[a_spec, b_spec] + ([] if after is None else [ANY]),
        out_specs=o_spec, out_shape=o_shape,
        scratch_shapes=[pltpu.VMEM((tm, tn), F32)] if nk > 1 else [],
        compiler_params=_params(("parallel", "parallel", "arbitrary")),
    )(*((a, b) if after is None else (a, b, after)))


def _rmsnorm_fwd(x2, gain):
    T, D = x2.shape
    tr = _pick(T, (256, 128))

    def body(x_ref, g_ref, h_ref, r_ref):
        xv = x_ref[...]
        r = lax.rsqrt(jnp.mean(xv * xv, axis=-1, keepdims=True) + NORM_EPS)
        h_ref[...] = (xv * r * g_ref[...]).astype(BF16)
        r_ref[...] = r

    return pl.pallas_call(
        body, name="rmsnorm_pre_fwd", grid=(T // tr,),
        in_specs=[pl.BlockSpec((tr, D), lambda i: (i, 0)), pl.BlockSpec((1, D), lambda i: (0, 0))],
        out_specs=[pl.BlockSpec((tr, D), lambda i: (i, 0)), pl.BlockSpec((tr, 1), lambda i: (i, 0))],
        out_shape=[jax.ShapeDtypeStruct((T, D), BF16), jax.ShapeDtypeStruct((T, 1), F32)],
        compiler_params=_params(("parallel",)),
    )(x2, gain)


def _rmsnorm_bwd(dh, x2, rstd, gain, dout):
    T, D = x2.shape
    tr = _pick(T, (256, 128))

    def body(dh_ref, x_ref, r_ref, g_ref, do_ref, gx_ref, gg_ref):
        @pl.when(pl.program_id(0) == 0)
        def _():
            gg_ref[...] = jnp.zeros_like(gg_ref)

        n = x_ref[...] * r_ref[...]
        dhv = dh_ref[...]
        dn = dhv * g_ref[...]
        gx_ref[...] = do_ref[...] + r_ref[...] * (dn - n * jnp.mean(dn * n, axis=-1, keepdims=True))
        gg_ref[...] += jnp.sum(dhv * n, axis=0, keepdims=True)

    row = pl.BlockSpec((tr, D), lambda i: (i, 0))
    vec = pl.BlockSpec((1, D), lambda i: (0, 0))
    return pl.pallas_call(
        body, name="rmsnorm_pre_bwd", grid=(T // tr,),
        in_specs=[row, row, pl.BlockSpec((tr, 1), lambda i: (i, 0)), vec, row],
        out_specs=[row, vec],
        out_shape=[jax.ShapeDtypeStruct((T, D), F32), jax.ShapeDtypeStruct((1, D), F32)],
        compiler_params=_params(("arbitrary",)),
    )(dh, x2, rstd, gain, dout)


def _post_loss(yv, x2, tgt2, gain):
    T, D = x2.shape
    tr = _pick(T, (256, 128))

    def body(y_ref, x_ref, t_ref, g_ref, dy_ref, do_ref, loss_ref, gg_ref):
        @pl.when(pl.program_id(0) == 0)
        def _():
            gg_ref[...] = jnp.zeros_like(gg_ref)
            loss_ref[...] = jnp.zeros_like(loss_ref)

        yv_ = y_ref[...]
        r = lax.rsqrt(jnp.mean(yv_ * yv_, axis=-1, keepdims=True) + NORM_EPS)
        n = yv_ * r
        e = (x_ref[...] + n * g_ref[...]) - t_ref[...]
        loss_ref[...] += 0.5 * jnp.sum(jnp.mean(e * e, axis=-1, keepdims=True), axis=0, keepdims=True)
        dz = e / D
        do_ref[...] = dz
        gg_ref[...] += jnp.sum(dz * n, axis=0, keepdims=True)
        dn = dz * g_ref[...]
        dy_ref[...] = (r * (dn - n * jnp.mean(dn * n, axis=-1, keepdims=True))).astype(BF16)

    row = pl.BlockSpec((tr, D), lambda i: (i, 0))
    vec = pl.BlockSpec((1, D), lambda i: (0, 0))
    return pl.pallas_call(
        body, name="post_norm_loss", grid=(T // tr,),
        in_specs=[row, row, row, vec],
        out_specs=[row, row, pl.BlockSpec((1, 1), lambda i: (0, 0)), vec],
        out_shape=[jax.ShapeDtypeStruct((T, D), BF16), jax.ShapeDtypeStruct((T, D), F32),
                   jax.ShapeDtypeStruct((1, 1), F32), jax.ShapeDtypeStruct((1, D), F32)],
        compiler_params=_params(("arbitrary",)),
    )(yv, x2, tgt2, gain)


def _merge_fwd(proj, ua, uh):
    T, D = ua.shape
    tr = _pick(T, (256, 128))
    bw = _pick(D, (512, 256))
    assert OFF_GA % bw == 0
    oa, oh = OFF_GA // bw, (OFF_GA + D) // bw

    def body(ga_ref, gh_ref, ua_ref, uh_ref, m_ref):
        m_ref[...] = (jax.nn.sigmoid(ga_ref[...]) * ua_ref[...].astype(F32)
                      + jax.nn.sigmoid(gh_ref[...]) * uh_ref[...].astype(F32)).astype(BF16)

    blk = pl.BlockSpec((tr, bw), lambda i, j: (i, j))
    return pl.pallas_call(
        body, name="merge_fwd", grid=(T // tr, D // bw),
        in_specs=[pl.BlockSpec((tr, bw), lambda i, j: (i, oa + j)), pl.BlockSpec((tr, bw), lambda i, j: (i, oh + j)), blk, blk],
        out_specs=blk, out_shape=jax.ShapeDtypeStruct((T, D), BF16),
        compiler_params=_params(("parallel", "parallel")),
    )(proj, proj, ua, uh)


def _window(rows, cols, at):
    return pl.BlockSpec((pl.Element(rows), pl.Element(cols)), at)


def _merge_bwd(proj, ua, uh, dm):
    T, D = ua.shape
    tr = _pick(T, (128,))

    def body(ga_ref, gh_ref, ua_ref, uh_ref, dm_ref, dua_ref, duh_ref, dproj_ref):
        sa = jax.nn.sigmoid(ga_ref[...])
        sh = jax.nn.sigmoid(gh_ref[...])
        d = dm_ref[...]
        dua_ref[...] = (d * sa).astype(BF16)
        duh_ref[...] = (d * sh).astype(BF16)
        dproj_ref[:, :D] = (d * ua_ref[...].astype(F32) * sa * (1.0 - sa)).astype(BF16)
        dproj_ref[:, D:] = (d * uh_ref[...].astype(F32) * sh * (1.0 - sh)).astype(BF16)

    blk = pl.BlockSpec((tr, D), lambda i: (i, 0))
    o = jax.ShapeDtypeStruct((T, D), BF16)
    return pl.pallas_call(
        body, name="merge_bwd", grid=(T // tr,),
        in_specs=[_window(tr, D, lambda i: (i * tr, OFF_GA)), _window(tr, D, lambda i: (i * tr, OFF_GA + D)), blk, blk, blk],
        out_specs=[blk, blk, _window(tr, 2 * D, lambda i: (i * tr, OFF_GA))],
        out_shape=[o, o, jax.ShapeDtypeStruct((T, proj.shape[1]), BF16)],
        compiler_params=_params(("parallel",)),
    )(proj, proj, ua, uh, dm)


KV_PAIR = 2
PAIR_HEADS = KV_PAIR * GROUP


def _attn_mask(n):
    qi = lax.broadcasted_iota(jnp.int32, (WINDOW, 2 * WINDOW), 0)
    si = lax.broadcasted_iota(jnp.int32, (WINDOW, 2 * WINDOW), 1)
    dist = qi + WINDOW - si
    return (dist >= 0) & (dist < WINDOW) & ((si >= WINDOW) | (n > 0))


def _attn_group_fn(mask):
    mask = jnp.concatenate([mask] * GROUP, axis=0)

    def f(q, k, v, ag, bias, sink):
        s = lax.dot_general(q.astype(BF16), k.astype(BF16), NT_DIMS, preferred_element_type=F32) * (HEAD_DIM ** -0.5)
        s = jnp.where(mask, s + bias, -1e30)
        m = lax.stop_gradient(jnp.maximum(jnp.max(s, axis=-1, keepdims=True), sink))
        p = jnp.exp(s - m)
        den = jnp.sum(p, axis=-1, keepdims=True) + jnp.exp(sink - m)
        o = jnp.dot(p.astype(BF16), v.astype(BF16), preferred_element_type=F32)
        return o * (jax.nn.silu(ag) / den)

    return f


def _attn_specs(B_loc, nb, order):
    qw = PAIR_HEADS * HEAD_DIM
    kw = KV_PAIR * HEAD_DIM

    def rows(g):
        b, p, n = order(*g)
        return b * nb + n

    def prev(g):
        b, p, n = order(*g)
        return b * nb + jnp.maximum(n - 1, 0)

    def pp(g):
        return order(*g)[1]

    q = pl.BlockSpec((WINDOW, qw), lambda *g: (rows(g), OFF_AQ // qw + pp(g)))
    kc = pl.BlockSpec((WINDOW, kw), lambda *g: (rows(g), OFF_AK // kw + pp(g)))
    kp = pl.BlockSpec((WINDOW, kw), lambda *g: (prev(g), OFF_AK // kw + pp(g)))
    vc = pl.BlockSpec((WINDOW, kw), lambda *g: (rows(g), OFF_AV // kw + pp(g)))
    vp = pl.BlockSpec((WINDOW, kw), lambda *g: (prev(g), OFF_AV // kw + pp(g)))
    ag = pl.BlockSpec((WINDOW, qw), lambda *g: (rows(g), OFF_AG // qw + pp(g)))
    bias = pl.BlockSpec((PAIR_HEADS, WINDOW, 2 * WINDOW), lambda *g: (pp(g), 0, 0))
    sink = pl.BlockSpec((PAIR_HEADS, 8, LANES), lambda *g: (pp(g), 0, 0))
    return [q, kc, kp, vc, vp, ag, bias, sink], rows, pp


def _attn_operands(q_ref, kc_ref, kp_ref, vc_ref, vp_ref, ag_ref, bias_ref, sink_ref, j):
    lo, hi = j * HEAD_DIM, (j + 1) * HEAD_DIM
    k = jnp.concatenate([kp_ref[:, lo:hi], kc_ref[:, lo:hi]], axis=0)
    v = jnp.concatenate([vp_ref[:, lo:hi], vc_ref[:, lo:hi]], axis=0)
    heads = [j * GROUP + g for g in range(GROUP)]
    q = jnp.concatenate([q_ref[:, h * HEAD_DIM:(h + 1) * HEAD_DIM] for h in heads], axis=0)
    ag = jnp.concatenate([ag_ref[:, h * HEAD_DIM:(h + 1) * HEAD_DIM] for h in heads], axis=0)
    bias = jnp.concatenate([bias_ref[h] for h in heads], axis=0)
    sink = jnp.concatenate([jnp.broadcast_to(sink_ref[h, 0:1, 0:1], (WINDOW, 1)) for h in heads], axis=0)
    return q, k, v, ag, bias, sink


def _attn_fwd(proj, bias_tab, sinks_b, B_loc, S):
    T = B_loc * S
    nb = S // WINDOW
    n_pairs = ATTN_KV_HEADS // KV_PAIR
    in_specs, rows, pp = _attn_specs(B_loc, nb, lambda b, p, n: (b, p, n))

    def body(q_ref, kc_ref, kp_ref, vc_ref, vp_ref, ag_ref, bias_ref, sink_ref, ya_ref):
        f = _attn_group_fn(_attn_mask(pl.program_id(2)))
        for j in range(KV_PAIR):
            out = f(*_attn_operands(q_ref, kc_ref, kp_ref, vc_ref, vp_ref, ag_ref, bias_ref, sink_ref, j))
            for g in range(GROUP):
                h = j * GROUP + g
                ya_ref[:, h * HEAD_DIM:(h + 1) * HEAD_DIM] = out[g * WINDOW:(g + 1) * WINDOW].astype(BF16)

    return pl.pallas_call(
        body, name="attn_fwd", grid=(B_loc, n_pairs, nb), in_specs=in_specs,
        out_specs=pl.BlockSpec((WINDOW, PAIR_HEADS * HEAD_DIM), lambda *g: (rows(g), pp(g))),
        out_shape=jax.ShapeDtypeStruct((T, ATTN_WIDTH), BF16),
        compiler_params=_params(("parallel", "parallel", "parallel")),
    )(proj, proj, proj, proj, proj, proj, bias_tab, sinks_b)


def _attn_bwd(proj, bias_tab, sinks_b, dya, dproj, B_loc, S):
    nb = S // WINDOW

    def at(off, back=0):
        return lambda b, i: ((b * nb + jnp.maximum(nb - 1 - i - back, 0)) * WINDOW, off)

    in_specs = [_window(WINDOW, ATTN_WIDTH, at(OFF_AQ)), _window(WINDOW, KV_WIDTH, at(OFF_AK)),
                _window(WINDOW, KV_WIDTH, at(OFF_AK, 1)), _window(WINDOW, KV_WIDTH, at(OFF_AV)),
                _window(WINDOW, KV_WIDTH, at(OFF_AV, 1)), _window(WINDOW, ATTN_WIDTH, at(OFF_AG)),
                pl.BlockSpec((ATTN_HEADS, WINDOW, 2 * WINDOW), lambda b, i: (0, 0, 0)),
                pl.BlockSpec((ATTN_HEADS, 8, LANES), lambda b, i: (0, 0, 0)),
                pl.BlockSpec((WINDOW, ATTN_WIDTH), lambda b, i: (b * nb + nb - 1 - i, 0)), ANY]

    def body(q_ref, kc_ref, kp_ref, vc_ref, vp_ref, ag_ref, bias_ref, sink_ref, dya_ref, dproj_in,
             dproj_ref, dbias_ref, dsink_ref, dkc_ref, dvc_ref):
        b, i = pl.program_id(0), pl.program_id(1)
        n = nb - 1 - i

        @pl.when((b == 0) & (i == 0))
        def _():
            dbias_ref[...] = jnp.zeros_like(dbias_ref)
            dsink_ref[...] = jnp.zeros_like(dsink_ref)

        @pl.when(i == 0)
        def _():
            dkc_ref[...] = jnp.zeros_like(dkc_ref)
            dvc_ref[...] = jnp.zeros_like(dvc_ref)

        f = _attn_group_fn(_attn_mask(n))
        for j in range(ATTN_KV_HEADS):
            ops = _attn_operands(q_ref, kc_ref, kp_ref, vc_ref, vp_ref, ag_ref, bias_ref, sink_ref, j)
            _, vjp = jax.vjp(f, *ops)
            dout = jnp.concatenate([dya_ref[:, (j * GROUP + g) * HEAD_DIM:(j * GROUP + g + 1) * HEAD_DIM].astype(F32)
                                    for g in range(GROUP)], axis=0)
            dq, dk, dv, dag, dbias, dsink = vjp(dout)
            lo, hi = j * HEAD_DIM, (j + 1) * HEAD_DIM
            dproj_ref[:, OFF_AK + lo:OFF_AK + hi] = (dk[WINDOW:] + dkc_ref[:, lo:hi]).astype(BF16)
            dproj_ref[:, OFF_AV + lo:OFF_AV + hi] = (dv[WINDOW:] + dvc_ref[:, lo:hi]).astype(BF16)
            dkc_ref[:, lo:hi] = dk[:WINDOW]
            dvc_ref[:, lo:hi] = dv[:WINDOW]
            for g in range(GROUP):
                h = j * GROUP + g
                blk = slice(g * WINDOW, (g + 1) * WINDOW)
                dproj_ref[:, OFF_AQ + h * HEAD_DIM:OFF_AQ + (h + 1) * HEAD_DIM] = dq[blk].astype(BF16)
                dproj_ref[:, OFF_AG + h * HEAD_DIM:OFF_AG + (h + 1) * HEAD_DIM] = dag[blk].astype(BF16)
                dbias_ref[h] += dbias[blk]
                dsink_ref[h] += jnp.broadcast_to(jnp.sum(dsink[blk], axis=0, keepdims=True), (8, LANES))

    return pl.pallas_call(
        body, name="attn_bwd", grid=(B_loc, nb), in_specs=in_specs,
        out_specs=[_window(WINDOW, OFF_HQ, at(0)),
                   pl.BlockSpec((ATTN_HEADS, WINDOW, 2 * WINDOW), lambda b, i: (0, 0, 0)),
                   pl.BlockSpec((ATTN_HEADS, 8, LANES), lambda b, i: (0, 0, 0))],
        out_shape=[jax.ShapeDtypeStruct(dproj.shape, dproj.dtype),
                   jax.ShapeDtypeStruct((ATTN_HEADS, WINDOW, 2 * WINDOW), F32),
                   jax.ShapeDtypeStruct((ATTN_HEADS, 8, LANES), F32)],
        input_output_aliases={9: 0},
        scratch_shapes=[pltpu.VMEM((WINDOW, KV_WIDTH), F32), pltpu.VMEM((WINDOW, KV_WIDTH), F32)],
        compiler_params=_params(("arbitrary", "arbitrary")),
    )(proj, proj, proj, proj, proj, proj, bias_tab, sinks_b, dya, dproj)


class _HgrnPre:
    def __init__(self, fr, qr, lb, g_scr):
        t = lax.broadcasted_iota(jnp.int32, (CHUNK, CHUNK), 0)
        s = lax.broadcasted_iota(jnp.int32, (CHUNK, CHUNK), 1)
        self.sg = jax.nn.sigmoid(fr)
        self.f = lb + (1.0 - lb) * self.sg
        g = jnp.dot((t >= s).astype(F32), jnp.log(self.f), precision=lax.Precision.HIGHEST, preferred_element_type=F32)
        g_scr[...] = g
        self.g = g
        self.row = lax.broadcasted_iota(jnp.int32, g.shape, 0)
        self.refs = [jnp.zeros((1, g.shape[1]), F32)] + [g_scr[pl.ds(i * SUB - 1, 1), :] for i in range(1, NSUB)]
        self.gend = g_scr[pl.ds(CHUNK - 1, 1), :]
        refrow = jnp.zeros_like(g)
        for i in range(1, NSUB):
            refrow = jnp.where(self.row >= i * SUB, self.refs[i], refrow)
        self.sigq = jax.nn.sigmoid(qr)
        self.qs = qr * self.sigq
        self.k = 1.0 - self.f
        self.eg = jnp.exp(g)
        self.eqd = jnp.exp(g - refrow)
        self.ekd = [jnp.exp(jnp.where(self.row < (i + 1) * SUB, self.refs[i] - g, 0.0)) for i in range(NSUB)]
        self.ekdec = jnp.exp(self.gend - g)
        self.qg = self.qs * self.eg
        self.qd = self.qs * self.eqd
        self.kd = [self.k * e for e in self.ekd]
        self.kdec = self.k * self.ekdec
        self.egend = jnp.exp(self.gend)


def _hgrn_pair_mask():
    t = lax.broadcasted_iota(jnp.int32, (CHUNK, NSUB * CHUNK), 0)
    col = lax.broadcasted_iota(jnp.int32, (CHUNK, NSUB * CHUNK), 1)
    return ((t // SUB) == (col // CHUNK)) & ((col % CHUNK) <= t)


def _hgrn_head_out(p, lanes, state_t, v, mask):
    qg, qd = p.qg[:, lanes], p.qd[:, lanes]
    kall = jnp.concatenate([kd[:, lanes] for kd in p.kd], axis=0)
    vst = jnp.concatenate([v] * NSUB, axis=0)
    am = jnp.where(mask, _bdot(qd, kall, NT_DIMS), 0.0)
    o = _bdot(qg, state_t, NT_DIMS) + _bdot(am, vst)
    return o, (qg, qd, kall, am, vst)


HEADS_PER_STEP = 4
GROUP_WIDTH = HEADS_PER_STEP * HGRN_DIM
N_HEAD_GROUPS = HGRN_HEADS // HEADS_PER_STEP


def _hgrn_specs(nc, order):
    def rows(g):
        b, grp, n = order(*g)
        return b * nc + n

    def grp_of(g):
        return order(*g)[1]

    def col(off):
        assert off % GROUP_WIDTH == 0
        return pl.BlockSpec((CHUNK, GROUP_WIDTH), lambda *g: (rows(g), off // GROUP_WIDTH + grp_of(g)))

    vec = pl.BlockSpec((1, GROUP_WIDTH), lambda *g: (0, grp_of(g)))
    return [col(OFF_HQ), col(OFF_HF), col(OFF_HI), col(OFF_HG), vec, vec], rows, grp_of


def _hgrn_fwd(proj, lb, gain, B_loc, S):
    T = B_loc * S
    nc = S // CHUNK
    in_specs, rows, grp_of = _hgrn_specs(nc, lambda b, grp, n: (b, grp, n))

    def body(q_ref, f_ref, v_ref, hg_ref, lb_ref, gain_ref, yh_ref, st_ref, state_scr, g_scr):
        @pl.when(pl.program_id(2) == 0)
        def _():
            state_scr[...] = jnp.zeros_like(state_scr)

        p = _HgrnPre(f_ref[...], q_ref[...], lb_ref[...], g_scr)
        v = v_ref[...]
        gate = gain_ref[...] * jax.nn.silu(hg_ref[...])
        mask = _hgrn_pair_mask()
        for hd in range(HEADS_PER_STEP):
            lanes = slice(hd * HGRN_DIM, (hd + 1) * HGRN_DIM)
            st = state_scr[hd]
            st_ref[hd] = st
            o, _ = _hgrn_head_out(p, lanes, st, v[:, lanes], mask)
            rs = lax.rsqrt(jnp.mean(o * o, axis=-1, keepdims=True) + NORM_EPS)
            yh_ref[:, lanes] = (o * rs * gate[:, lanes]).astype(BF16)
            state_scr[hd] = st * p.egend[:, lanes] + _bdot(v[:, lanes], p.kdec[:, lanes], TN_DIMS)

    return pl.pallas_call(
        body, name="hgrn_fwd", grid=(B_loc, N_HEAD_GROUPS, nc), in_specs=in_specs,
        out_specs=[pl.BlockSpec((CHUNK, GROUP_WIDTH), lambda *g: (rows(g), grp_of(g))),
                   pl.BlockSpec((None, None, HEADS_PER_STEP, HGRN_DIM, HGRN_DIM), lambda b, grp, n: (b, n, grp, 0, 0))],
        out_shape=[jax.ShapeDtypeStruct((T, HGRN_WIDTH), BF16),
                   jax.ShapeDtypeStruct((B_loc, nc, HGRN_HEADS, HGRN_DIM, HGRN_DIM), F32)],
        scratch_shapes=[pltpu.VMEM((HEADS_PER_STEP, HGRN_DIM, HGRN_DIM), F32), pltpu.VMEM((CHUNK, GROUP_WIDTH), F32)],
        compiler_params=_params(("parallel", "parallel", "arbitrary")),
    )(proj, proj, proj, proj, lb, gain)


def _hgrn_bwd(proj, states, lb, gain, dyh, dproj, B_loc, S):
    nc = S // CHUNK
    nh = HGRN_HEADS

    def at(off):
        return lambda b, i: ((b * nc + nc - 1 - i) * CHUNK, off)

    vec = pl.BlockSpec((1, HGRN_WIDTH), lambda b, i: (0, 0))
    in_specs = [_window(CHUNK, HGRN_WIDTH, at(OFF_HQ)), _window(CHUNK, HGRN_WIDTH, at(OFF_HF)),
                _window(CHUNK, HGRN_WIDTH, at(OFF_HI)), _window(CHUNK, HGRN_WIDTH, at(OFF_HG)), vec, vec,
                pl.BlockSpec((None, None, nh, HGRN_DIM, HGRN_DIM), lambda b, i: (b, nc - 1 - i, 0, 0, 0)),
                pl.BlockSpec((CHUNK, HGRN_WIDTH), lambda b, i: (b * nc + nc - 1 - i, 0)), ANY]
    acc_spec = pl.BlockSpec((None, 1, HGRN_WIDTH), lambda b, i: (b, 0, 0))

    def body(q_ref, f_ref, v_ref, hg_ref, lb_ref, gain_ref, st_ref, dyh_ref, dproj_in,
             dproj_ref, dlb_ref, dgain_ref, dstate_scr, g_scr, dg_scr):
        dq_ref, df_ref, dv_ref, dhg_ref = [dproj_ref.at[:, pl.ds(k * HGRN_WIDTH, HGRN_WIDTH)] for k in range(4)]

        @pl.when(pl.program_id(1) == 0)
        def _():
            dstate_scr[...] = jnp.zeros_like(dstate_scr)
            dlb_ref[...] = jnp.zeros_like(dlb_ref)
            dgain_ref[...] = jnp.zeros_like(dgain_ref)

        qr, lb, gain, hg, v = q_ref[...], lb_ref[...], gain_ref[...], hg_ref[...], v_ref[...]
        p = _HgrnPre(f_ref[...], qr, lb, g_scr)
        sgh = jax.nn.sigmoid(hg)
        sil = hg * sgh
        dy = dyh_ref[...].astype(F32)
        mask = _hgrn_pair_mask()
        dqg, dqd, dkdec, dv, dhg, dgend, dgain = [], [], [], [], [], [], []
        dkd = [[] for _ in range(NSUB)]
        heads = [slice(hd * HGRN_DIM, (hd + 1) * HGRN_DIM) for hd in range(nh)]
        sts = [st_ref[hd] for hd in range(nh)]
        dnews = [dstate_scr[hd] for hd in range(nh)]
        fwd = [_hgrn_head_out(p, lanes, st, v[:, lanes], mask) for lanes, st in zip(heads, sts)]
        for lanes, st, dnew in zip(heads, sts, dnews):
            dkdec_h = _bdot(v[:, lanes], dnew)
            dkdec.append(dkdec_h)
            dgend.append(jnp.sum(dkdec_h * p.kdec[:, lanes], axis=0, keepdims=True)
                         + jnp.sum(dnew * st, axis=0, keepdims=True) * p.egend[:, lanes])
        dos = []
        gate_grad = sgh * (1.0 + hg * (1.0 - sgh))
        for lanes, (o, _) in zip(heads, fwd):
            rs = lax.rsqrt(jnp.mean(o * o, axis=-1, keepdims=True) + NORM_EPS)
            n = o * rs
            dyn = dy[:, lanes] * n
            dgain.append(jnp.sum(dyn * sil[:, lanes], axis=0, keepdims=True))
            dhg.append(dyn * gain[:, lanes] * gate_grad[:, lanes])
            dn = dy[:, lanes] * gain[:, lanes] * sil[:, lanes]
            dos.append(rs * (dn - n * jnp.mean(dn * n, axis=-1, keepdims=True)))
        drs = []
        for hd, (lanes, st, dnew, do, (_, (qg, qd, kall, am, vst))) in enumerate(zip(heads, sts, dnews, dos, fwd)):
            dqg.append(_bdot(do, st))
            dstate_scr[hd] = _bdot(do, qg, TN_DIMS) + dnew * p.egend[:, lanes]
            drs.append(jnp.where(mask, _bdot(do, vst, NT_DIMS), 0.0))
            dvst = _bdot(am, do, TN_DIMS)
            dv.append(sum(dvst[i * CHUNK:(i + 1) * CHUNK] for i in range(NSUB)) + _bdot(p.kdec[:, lanes], dnew, NT_DIMS))
        for dr, (_, (qg, qd, kall, am, vst)) in zip(drs, fwd):
            dqd.append(_bdot(dr, kall))
            dkall = _bdot(dr, qd, TN_DIMS)
            for i in range(NSUB):
                dkd[i].append(dkall[i * CHUNK:(i + 1) * CHUNK])

        wide = lambda parts: jnp.concatenate(parts, axis=1)
        dqg, dqd, dkdec = wide(dqg), wide(dqd), wide(dkdec)
        t2 = dqd * p.qd
        dg = dqg * p.qg + t2 - dkdec * p.kdec
        dk = dkdec * p.ekdec
        dg_scr[...] = jnp.zeros_like(dg_scr)
        for i in range(NSUB):
            dkd_i = wide(dkd[i])
            tk = jnp.where(p.row < (i + 1) * SUB, dkd_i * p.kd[i], 0.0)
            dg = dg - tk
            dk = dk + dkd_i * p.ekd[i]
            if i >= 1:
                in_blk = (p.row >= i * SUB) & (p.row < (i + 1) * SUB)
                dg_scr[pl.ds(i * SUB - 1, 1), :] = (jnp.sum(tk, axis=0, keepdims=True)
                                                    - jnp.sum(jnp.where(in_blk, t2, 0.0), axis=0, keepdims=True))
        dg_scr[pl.ds(CHUNK - 1, 1), :] = wide(dgend)
        t = lax.broadcasted_iota(jnp.int32, (CHUNK, CHUNK), 0)
        s = lax.broadcasted_iota(jnp.int32, (CHUNK, CHUNK), 1)
        dlogf = jnp.dot((t <= s).astype(F32), dg + dg_scr[...], precision=lax.Precision.HIGHEST, preferred_element_type=F32)
        df = dlogf / p.f - dk
        df_ref[...] = (df * (1.0 - lb) * p.sg * (1.0 - p.sg)).astype(BF16)
        dlb_ref[...] += jnp.sum(df * (1.0 - p.sg), axis=0, keepdims=True)
        dq_ref[...] = ((dqg * p.eg + dqd * p.eqd) * p.sigq * (1.0 + qr * (1.0 - p.sigq))).astype(BF16)
        dv_ref[...] = wide(dv).astype(BF16)
        dhg_ref[...] = wide(dhg).astype(BF16)
        dgain_ref[...] += wide(dgain)

    acc = jax.ShapeDtypeStruct((B_loc, 1, HGRN_WIDTH), F32)
    return pl.pallas_call(
        body, name="hgrn_bwd", grid=(B_loc, nc), in_specs=in_specs,
        out_specs=[_window(CHUNK, 4 * HGRN_WIDTH, at(OFF_HQ)), acc_spec, acc_spec],
        out_shape=[jax.ShapeDtypeStruct(dproj.shape, dproj.dtype), acc, acc],
        input_output_aliases={8: 0},
        scratch_shapes=[pltpu.VMEM((nh, HGRN_DIM, HGRN_DIM), F32), pltpu.VMEM((CHUNK, HGRN_WIDTH), F32),
                        pltpu.VMEM((CHUNK, HGRN_WIDTH), F32)],
        compiler_params=_params(("parallel", "arbitrary")),
    )(proj, proj, proj, proj, lb, gain, states, dyh, dproj)


def _adamw(w, g, m, v, name):
    R, C = w.shape
    tr = _pick(R, (128, 64, 32, 16, 8)) if C > 1024 else _pick(R, (512, 256, 128, 64, 32, 16, 8))

    def body(w_ref, g_ref, m_ref, v_ref, d_ref, nm_ref, nv_ref):
        gv = g_ref[...]
        nm = ADAM_B1 * m_ref[...] + (1.0 - ADAM_B1) * gv
        nv = ADAM_B2 * v_ref[...] + (1.0 - ADAM_B2) * (gv * gv)
        m_hat = nm / (1.0 - ADAM_B1 ** ADAM_STEP)
        v_hat = nv / (1.0 - ADAM_B2 ** ADAM_STEP)
        d_ref[...] = -ADAM_LR * (m_hat / (jnp.sqrt(v_hat) + ADAM_EPS) + ADAM_WD * w_ref[...])
        nm_ref[...] = nm
        nv_ref[...] = nv

    blk = pl.BlockSpec((tr, C), lambda i: (i, 0))
    o = jax.ShapeDtypeStruct((R, C), F32)
    return pl.pallas_call(
        body, name=name, grid=(R // tr,), in_specs=[blk] * 4, out_specs=[blk] * 3, out_shape=[o, o, o],
        compiler_params=_params(("parallel",)),
    )(w, g, m, v)


ANY = pl.BlockSpec(memory_space=pl.ANY)
VMEM_SPEC = pl.BlockSpec(memory_space=pltpu.VMEM)


def _place():
    x, y, c = lax.axis_index("x"), lax.axis_index("y"), lax.axis_index("c")
    other_chips = [(1 - x, y), (x, 1 - y), (1 - x, 1 - y)]
    return x, y, c, other_chips


def _cast_into_full(w, ax, s_arr, name):
    R, C = w.shape
    tr = _pick(R, (256, 128))
    nr = R // tr

    def body(s_ref, w_ref, o_ref):
        o_ref[...] = w_ref[...].astype(BF16)

    if ax == 1:
        shape, o_map = (R, N_CHIPS * C), lambda i, s: (i, s[0])
    else:
        shape, o_map = (N_CHIPS * R, C), lambda i, s: (s[0] * nr + i, 0)
    return pl.pallas_call(
        body, name=name,
        grid_spec=pltpu.PrefetchScalarGridSpec(
            num_scalar_prefetch=1, grid=(nr,), in_specs=[pl.BlockSpec((tr, C), lambda i, s: (i, 0))],
            out_specs=pl.BlockSpec((tr, C), o_map)),
        out_shape=jax.ShapeDtypeStruct(shape, BF16),
        compiler_params=_params(("parallel",)),
    )(s_arr, w)


class _Gather:
    def __init__(self, fulls, shard_shapes, axes, tag, after=None):
        self.shapes, self.axes, self.tag, self.nw = shard_shapes, axes, tag, len(fulls)
        nw = self.nw

        def body(*refs):
            ins, sems = refs[:nw], refs[nw + (after is not None):nw + (after is not None) + 2 * (N_CHIPS - 1)]
            for j in range(N_CHIPS - 1):
                for cp in self._peer_copies(ins, sems[2 * j], sems[2 * j + 1], j):
                    cp.start()
            refs[-1][...] = jnp.zeros_like(refs[-1])

        out = pl.pallas_call(
            body, name="gather_start_" + tag,
            out_shape=(*[pltpu.SemaphoreType.DMA((nw,))] * (2 * (N_CHIPS - 1)),
                       *[pltpu.HBM(f.shape, f.dtype) for f in fulls], jax.ShapeDtypeStruct((8, LANES), F32)),
            in_specs=[HBM_SPEC] * nw + ([] if after is None else [ANY]),
            out_specs=(*[SEM_SPEC] * (2 * (N_CHIPS - 1)), *[HBM_SPEC] * nw, VMEM_SPEC),
            input_output_aliases={k: 2 * (N_CHIPS - 1) + k for k in range(nw)},
            compiler_params=pltpu.CompilerParams(has_side_effects=DATAFLOW),
        )(*[pltpu.with_memory_space_constraint(f, pltpu.HBM) for f in fulls], *(() if after is None else (after,)))
        self.sems = [(out[2 * j], out[2 * j + 1]) for j in range(N_CHIPS - 1)]
        self.fulls = list(out[2 * (N_CHIPS - 1):2 * (N_CHIPS - 1) + nw])
        self.token = out[-1]

    def _region(self, ref, i, t, half):
        R, C = self.shapes[i]
        hr = R // 2
        if self.axes[i] == 1:
            return ref.at[pl.ds(half * hr, hr), pl.ds(pl.multiple_of(t * C, LANES), C)]
        return ref.at[pl.ds(t * R + half * hr, hr), :]

    def _peer_copies(self, refs, send_sems, recv_sems, j):
        x, y, c, chips = _place()
        s = 2 * x + y
        return [pltpu.make_async_remote_copy(
            src_ref=self._region(refs[i], i, s, c), dst_ref=self._region(refs[i], i, s, c), send_sem=send_sems.at[i],
            recv_sem=recv_sems.at[i], device_id=(*chips[j], c), device_id_type=MESH) for i in range(self.nw)]

    def wait(self, peers, after):
        nw, np_ = self.nw, len(peers)

        def body(*refs):
            ins, sems = refs[:nw], refs[nw:nw + 2 * np_]
            for k, j in enumerate(peers):
                for cp in self._peer_copies(ins, sems[2 * k], sems[2 * k + 1], j):
                    cp.wait_send()
                    cp.wait_recv()

        sem_args = [s for j in peers for s in self.sems[j]]
        out = pl.pallas_call(
            body, name="gather_wait_%s_%s" % (self.tag, "".join(map(str, peers))),
            out_shape=tuple(pltpu.HBM(f.shape, f.dtype) for f in self.fulls),
            in_specs=[HBM_SPEC] * nw + [SEM_SPEC] * (2 * np_) + [ANY], out_specs=tuple([HBM_SPEC] * nw),
            input_output_aliases={k: k for k in range(nw)},
            compiler_params=pltpu.CompilerParams(has_side_effects=DATAFLOW),
        )(*self.fulls, *sem_args, after)
        self.fulls = list(out)

    def forward(self, peers):
        nw, np_ = self.nw, len(peers)

        def body(*refs):
            ins, outs = refs[:nw], refs[nw:2 * nw]
            send_sems, recv_sems = refs[2 * nw:]
            x, y, c, chips = _place()
            cps = []
            for i in range(nw):
                for k, j in enumerate(peers):
                    t = 2 * chips[j][0] + chips[j][1]
                    cp = pltpu.make_async_remote_copy(
                        src_ref=self._region(ins[i], i, t, c), dst_ref=self._region(outs[i], i, t, c),
                        send_sem=send_sems.at[i * np_ + k], recv_sem=recv_sems.at[i * np_ + k],
                        device_id=(x, y, 1 - c), device_id_type=MESH)
                    cp.start()
                    cps.append(cp)
            for cp in cps:
                cp.wait()

        out = pl.pallas_call(
            body, name="gather_forward_%s_%s" % (self.tag, "".join(map(str, peers))),
            in_specs=[ANY] * nw, out_specs=[ANY] * nw,
            out_shape=[jax.ShapeDtypeStruct(f.shape, f.dtype) for f in self.fulls],
            input_output_aliases={i: i for i in range(nw)},
            scratch_shapes=[pltpu.SemaphoreType.DMA((nw * np_,)), pltpu.SemaphoreType.DMA((nw * np_,))],
        )(*self.fulls)
        self.fulls = list(out)


def _matmul_slab(a, wfull, slab_arr, prev, after, name):
    M, K = a.shape
    N = wfull.shape[1]
    nslab = N // N_CHIPS
    tn = _pick(nslab, (2688, 896, 512, 384, 256, 128))
    tm = _pick(M, (512, 256, 128) if tn > 1024 else (1024, 512, 256, 128))
    per = nslab // tn
    extra = [e for e in (prev, after) if e is not None]

    def body(slab_ref, a_ref, b_ref, *rest):
        rest[len(extra)][...] = jnp.dot(a_ref[...], b_ref[...], preferred_element_type=F32)

    return pl.pallas_call(
        body, name=name,
        grid_spec=pltpu.PrefetchScalarGridSpec(
            num_scalar_prefetch=1, grid=(M // tm, per),
            in_specs=[pl.BlockSpec((tm, K), lambda i, j, sl: (i, 0)),
                      pl.BlockSpec((K, tn), lambda i, j, sl: (0, sl[0] * per + j))] + [ANY] * len(extra),
            out_specs=pl.BlockSpec((tm, tn), lambda i, j, sl: (i, sl[0] * per + j))),
        out_shape=jax.ShapeDtypeStruct((M, N), F32),
        input_output_aliases={} if prev is None else {3: 0},
        compiler_params=_params(("parallel", "arbitrary")),
    )(slab_arr, a, wfull, *extra)


def _exchange_sibling_halves(gs, name):
    nw = len(gs)

    def body(*refs):
        ins, outs = refs[:nw], refs[nw:2 * nw]
        send_sems, recv_sems = refs[2 * nw:]
        x, y, c, _ = _place()
        cps = []
        for i in range(nw):
            cp = pltpu.make_async_remote_copy(src_ref=ins[i].at[:, 1 - c], dst_ref=outs[i], send_sem=send_sems.at[i],
                                              recv_sem=recv_sems.at[i], device_id=(x, y, 1 - c), device_id_type=MESH)
            cp.start()
            cps.append(cp)
        for cp in cps:
            cp.wait()

    return pl.pallas_call(
        body, name=name, in_specs=[ANY] * nw, out_specs=[ANY] * nw,
        out_shape=[jax.ShapeDtypeStruct((g.shape[0],) + g.shape[2:], g.dtype) for g in gs],
        scratch_shapes=[pltpu.SemaphoreType.DMA((nw,)), pltpu.SemaphoreType.DMA((nw,))],
    )(*gs)


HBM_SPEC = pl.BlockSpec(memory_space=pltpu.HBM)
SEM_SPEC = pl.BlockSpec(memory_space=pltpu.SEMAPHORE)
DATAFLOW = pltpu.SideEffectType.DATAFLOW_SIDE_EFFECTING


def _chip_copies(ins, lands, send_sems, recv_sems):
    x, y, c, chips = _place()
    return [pltpu.make_async_remote_copy(
        src_ref=ins[i].at[2 * chip[0] + chip[1]], dst_ref=lands[i].at[j], send_sem=send_sems.at[i * 3 + j],
        recv_sem=recv_sems.at[i * 3 + j], device_id=(*chip, c), device_id_type=MESH)
        for i in range(len(ins)) for j, chip in enumerate(chips)]


def _chips_send_start(ss, name):
    nw = len(ss)
    lands = [pltpu.with_memory_space_constraint(lax.empty((N_CHIPS - 1,) + s.shape[1:], s.dtype), pltpu.HBM) for s in ss]

    def body(*refs):
        ins, land_refs = refs[:nw], refs[nw:2 * nw]
        send_sems, recv_sems = refs[2 * nw], refs[2 * nw + 1]
        token = refs[-1]
        for cp in _chip_copies(ins, land_refs, send_sems, recv_sems):
            cp.start()
        token[...] = jnp.zeros_like(token)

    n = 3 * nw
    out = pl.pallas_call(
        body, name=name,
        out_shape=(pltpu.SemaphoreType.DMA((n,)), pltpu.SemaphoreType.DMA((n,)),
                   *[pltpu.HBM(s.shape, s.dtype) for s in ss], *[pltpu.HBM(l.shape, l.dtype) for l in lands],
                   jax.ShapeDtypeStruct((8, LANES), F32)),
        in_specs=[HBM_SPEC] * (2 * nw), out_specs=(SEM_SPEC, SEM_SPEC, *[HBM_SPEC] * (2 * nw), VMEM_SPEC),
        input_output_aliases={k: 2 + k for k in range(2 * nw)},
        compiler_params=pltpu.CompilerParams(has_side_effects=DATAFLOW),
    )(*[pltpu.with_memory_space_constraint(s, pltpu.HBM) for s in ss], *lands)
    return out[0], out[1], list(out[2:2 + nw]), list(out[2 + nw:2 + 2 * nw]), out[-1]


def _chips_send_wait(send_sems, recv_sems, ss, lands, after, name):
    nw = len(ss)

    def body(*refs):
        ins, land_refs = refs[:nw], refs[nw:2 * nw]
        s_sems, r_sems = refs[2 * nw], refs[2 * nw + 1]
        for cp in _chip_copies(ins, land_refs, s_sems, r_sems):
            cp.wait_send()
            cp.wait_recv()

    out = pl.pallas_call(
        body, name=name,
        out_shape=(*[pltpu.HBM(s.shape, s.dtype) for s in ss], *[pltpu.HBM(l.shape, l.dtype) for l in lands]),
        in_specs=[HBM_SPEC] * (2 * nw) + [SEM_SPEC, SEM_SPEC, ANY], out_specs=tuple([HBM_SPEC] * (2 * nw)),
        input_output_aliases={k: k for k in range(2 * nw)},
        compiler_params=pltpu.CompilerParams(has_side_effects=DATAFLOW),
    )(*ss, *lands, send_sems, recv_sems, after)
    return list(out[nw:])


def _sum_small(pack):
    rows = pack.shape[0]

    def body(pack_ref, sum_ref, all_ref, send_sems, recv_sems):
        x, y, c, _ = _place()
        me = 4 * x + 2 * y + c
        all_ref[me] = pack_ref[...]
        cps = []
        for k in range(1, N_DEV):
            to = (1 - x if k & 4 else x, 1 - y if k & 2 else y, 1 - c if k & 1 else c)
            cp = pltpu.make_async_remote_copy(
                src_ref=pack_ref, dst_ref=all_ref.at[me], send_sem=send_sems.at[k - 1],
                recv_sem=recv_sems.at[k - 1], device_id=to, device_id_type=MESH)
            cp.start()
            cps.append(cp)
        for cp in cps:
            cp.wait()
        total = all_ref[0]
        for d in range(1, N_DEV):
            total = total + all_ref[d]
        sum_ref[...] = total

    return pl.pallas_call(
        body, name="sum_small_grads", in_specs=[VMEM_SPEC], out_specs=VMEM_SPEC,
        out_shape=jax.ShapeDtypeStruct(pack.shape, F32),
        scratch_shapes=[pltpu.VMEM((N_DEV, rows, LANES), F32), pltpu.SemaphoreType.DMA((N_DEV - 1,)),
                        pltpu.SemaphoreType.DMA((N_DEV - 1,))],
    )(pack)


def _share_with_sibling(fs, name):
    nw = len(fs)

    def body(*refs):
        ins, outs = refs[:nw], refs[nw:2 * nw]
        send_sems, recv_sems = refs[2 * nw:]
        x, y, c, _ = _place()
        cps = []
        for i in range(nw):
            cp = pltpu.make_async_remote_copy(src_ref=ins[i].at[c], dst_ref=outs[i].at[c], send_sem=send_sems.at[i],
                                              recv_sem=recv_sems.at[i], device_id=(x, y, 1 - c), device_id_type=MESH)
            cp.start()
            cps.append(cp)
        for cp in cps:
            cp.wait()

    return pl.pallas_call(
        body, name=name, in_specs=[ANY] * nw, out_specs=[ANY] * nw,
        out_shape=[jax.ShapeDtypeStruct(f.shape, f.dtype) for f in fs],
        input_output_aliases={i: i for i in range(nw)},
        scratch_shapes=[pltpu.SemaphoreType.DMA((nw,)), pltpu.SemaphoreType.DMA((nw,))],
    )(*fs)


def _sum_sibling(g, land, c_arr, name):
    _, _, P, Q = g.shape
    tp = _pick(P, (256, 128, 64))

    def body(c_ref, g_ref, l_ref, s_ref):
        s_ref[...] = (g_ref[...].astype(F32) + l_ref[...].astype(F32)).astype(BF16)

    return pl.pallas_call(
        body, name=name,
        grid_spec=pltpu.PrefetchScalarGridSpec(
            num_scalar_prefetch=1, grid=(N_CHIPS, P // tp),
            in_specs=[pl.BlockSpec((None, None, tp, Q), lambda t, r, cr: (t, cr[0], r, 0)),
                      pl.BlockSpec((None, tp, Q), lambda t, r, cr: (t, r, 0))],
            out_specs=pl.BlockSpec((None, tp, Q), lambda t, r, cr: (t, r, 0))),
        out_shape=jax.ShapeDtypeStruct((N_CHIPS, P, Q), BF16),
        compiler_params=_params(("parallel", "parallel")),
    )(c_arr, g, land)


def _sum_chips(g, land, recv, sc_arr, name):
    _, _, P, Q = g.shape
    tp = _pick(P, (256, 128, 64))

    def body(sc_ref, g_ref, l_ref, r_ref, f_ref):
        acc = g_ref[...].astype(F32) + l_ref[...].astype(F32)
        for j in range(N_CHIPS - 1):
            acc = acc + r_ref[j].astype(F32)
        f_ref[...] = acc

    return pl.pallas_call(
        body, name=name,
        grid_spec=pltpu.PrefetchScalarGridSpec(
            num_scalar_prefetch=1, grid=(P // tp,),
            in_specs=[pl.BlockSpec((None, None, tp, Q), lambda r, sc: (sc[0], sc[1], r, 0)),
                      pl.BlockSpec((None, tp, Q), lambda r, sc: (sc[0], r, 0)),
                      pl.BlockSpec((N_CHIPS - 1, tp, Q), lambda r, sc: (0, r, 0))],
            out_specs=pl.BlockSpec((None, tp, Q), lambda r, sc: (sc[1], r, 0))),
        out_shape=jax.ShapeDtypeStruct((2, P, Q), F32),
        compiler_params=_params(("parallel",)),
    )(sc_arr, g, land, recv)


class _Reduction:
    def __init__(self, names, pieces, lands, flight):
        self.names, self.pieces, self.lands = names, pieces, lands
        self.send_sems, self.recv_sems, self.sums, self.zones, self.token = flight


def _reduce_start(grads, names, c_arr):
    pieces = [g.reshape(N_CHIPS, 2, -1, g.shape[-1]) for g in grads]
    tag = names[0] if len(names) == 1 else "branches"
    lands = _exchange_sibling_halves(pieces, "grads_to_sibling_" + tag)
    sums = [_sum_sibling(g, l, c_arr, "sum_sibling_" + nm) for g, l, nm in zip(pieces, lands, names)]
    return _Reduction(names, pieces, lands, _chips_send_start(sums, "grads_to_chips_start_" + tag))


def _reduce_finish(red, after, sc_arr):
    tag = red.names[0] if len(red.names) == 1 else "branches"
    recvs = _chips_send_wait(red.send_sems, red.recv_sems, red.sums, red.zones, after, "grads_to_chips_wait_" + tag)
    halves = [_sum_chips(g, l, r, sc_arr, "sum_chips_" + nm) for g, l, r, nm in zip(red.pieces, red.lands, recvs, red.names)]
    return _share_with_sibling(halves, "grads_share_sibling_" + tag)


def _t5_bucket(dist):
    max_exact = REL_BUCKETS // 2
    d = jnp.maximum(dist, 0)
    df = jnp.maximum(d, 1).astype(F32)
    large = max_exact + (jnp.log(df / max_exact) / math.log(REL_MAX_DIST / max_exact)
                         * (REL_BUCKETS - max_exact)).astype(jnp.int32)
    large = jnp.minimum(large, REL_BUCKETS - 1)
    return jnp.where(d < max_exact, d, large)


def _bucket_table():
    qi = jnp.arange(WINDOW)[:, None]
    si = jnp.arange(2 * WINDOW)[None, :]
    return _t5_bucket(qi + WINDOW - si)


TILE_WORDS = 8 * LANES


def _tile_rows(shape):
    return -(-math.prod(shape) // TILE_WORDS) * 8


def _rows_of(a):
    flat = a.reshape(-1).astype(F32)
    n = _tile_rows(a.shape) * LANES
    return jnp.pad(flat, (0, n - flat.shape[0])).reshape(-1, LANES)


def _pack_rows(parts):
    return jnp.concatenate([_rows_of(p) for p in parts], axis=0)


def _unpack_rows(packed, shapes):
    out, at = [], 0
    for shp in shapes:
        n, nr = math.prod(shp), _tile_rows(shp)
        out.append(packed[at:at + nr].reshape(-1)[:n].reshape(shp))
        at += nr
    return out


def kernel(x, norm_pre, w_in, rel_bias, attn_sinks, lb_logits, hgrn_norm, w_branch_attn, w_branch_hgrn, w_out, norm_post, loss_target, m_norm_pre, m_w_in, m_rel_bias, m_attn_sinks, m_lb_logits, m_hgrn_norm, m_w_branch_attn, m_w_branch_hgrn, m_w_out, m_norm_post, v_norm_pre, v_w_in, v_rel_bias, v_attn_sinks, v_lb_logits, v_hgrn_norm, v_w_branch_attn, v_w_branch_hgrn, v_w_out, v_norm_post):
    B_loc, S, D = x.shape
    T = B_loc * S
    x2 = x.reshape(T, D)
    tgt2 = loss_target.reshape(T, D)
    my_x, my_y, my_c = lax.axis_index("x"), lax.axis_index("y"), lax.axis_index("c")

    c_arr = jnp.reshape(my_c, (1,)).astype(jnp.int32)
    s_arr = jnp.reshape(2 * my_x + my_y, (1,)).astype(jnp.int32)
    sc_arr = jnp.concatenate([s_arr, c_arr])
    shard_ws = [w_in[0], w_branch_attn[0], w_branch_hgrn[0], w_out[0]]
    shard_axes = (1, 1, 1, 0)
    names = ["w_in", "w_branch_attn", "w_branch_hgrn", "w_out"]
    placed = [_cast_into_full(w, ax, s_arr, "cast_" + nm) for w, ax, nm in zip(shard_ws, shard_axes, names)]
    peer_slabs = [jnp.reshape(t, (1,)).astype(jnp.int32)
                  for t in (2 * (1 - my_x) + my_y, 2 * my_x + 1 - my_y, 2 * (1 - my_x) + 1 - my_y)]

    buckets = _bucket_table()
    onehot = (buckets.reshape(-1)[:, None] == jnp.arange(REL_BUCKETS)[None, :]).astype(F32)
    bias_tab = jnp.dot(onehot, rel_bias.astype(F32), precision=lax.Precision.HIGHEST).T.reshape(ATTN_HEADS, WINDOW, 2 * WINDOW)
    sinks_b = jnp.broadcast_to(attn_sinks[0].astype(F32)[:, None, None], (ATTN_HEADS, 8, LANES))
    lb_fn = lambda l: jnp.cumsum(jax.nn.softmax(l.astype(F32), axis=0), axis=0)[:1]
    lb, lb_vjp = jax.vjp(lb_fn, lb_logits)
    gain_h = hgrn_norm[0].reshape(1, HGRN_WIDTH)

    h, rstd = _rmsnorm_fwd(x2, norm_pre)
    gather_in = _Gather(placed[:1], [shard_ws[0].shape], shard_axes[:1], "w_in")
    proj = _matmul_slab(h, gather_in.fulls[0], s_arr, None, gather_in.token, "in_proj_own")
    for j in range(N_CHIPS - 1):
        gather_in.wait([j], proj)
        gather_in.forward([j])
        after = None
        if j == N_CHIPS - 2:
            gather_rest = _Gather(placed[1:], [w.shape for w in shard_ws[1:]], shard_axes[1:], "rest", gather_in.fulls[0])
            after = gather_rest.token
        proj = _matmul_slab(h, gather_in.fulls[0], peer_slabs[j], proj, after, "in_proj_peer%d" % j)
    win_f = gather_in.fulls[0]
    ya = _attn_fwd(proj, bias_tab, sinks_b, B_loc, S)
    yh, states = _hgrn_fwd(proj, lb, gain_h, B_loc, S)
    gather_rest.wait([0, 1, 2], yh)
    gather_rest.forward([0, 1, 2])
    wba_f, wbh_f, wout_f = gather_rest.fulls
    ua = _matmul(ya, wba_f, "nn", BF16, "branch_attn_proj")
    uh = _matmul(yh, wbh_f, "nn", BF16, "branch_hgrn_proj")
    merged = _merge_fwd(proj, ua, uh)
    yv = _matmul(merged, wout_f, "nn", F32, "out_proj")
    dy, dout, loss_p, gnpost_p = _post_loss(yv, x2, tgt2, norm_post)

    dmerged = _matmul(dy, wout_f, "nt", F32, "out_proj_dgrad")
    g_wout = _matmul(merged, dy, "tn", BF16, "out_proj_wgrad")
    d_ua, d_uh, dproj = _merge_bwd(proj, ua, uh, dmerged)
    g_wba = _matmul(ya, d_ua, "tn", BF16, "branch_attn_wgrad", slabs=N_CHIPS)
    g_wbh = _matmul(yh, d_uh, "tn", BF16, "branch_hgrn_wgrad", slabs=N_CHIPS)
    late = _reduce_start([g_wba, g_wbh, g_wout], names[1:], c_arr)
    d_ya = _matmul(d_ua, wba_f, "nt", BF16, "branch_attn_dgrad", after=late.token)
    d_yh = _matmul(d_uh, wbh_f, "nt", BF16, "branch_hgrn_dgrad", after=late.token)
    dproj, dbias_p, dsinks_p = _attn_bwd(proj, bias_tab, sinks_b, d_ya, dproj, B_loc, S)
    dproj, dlb_p, dgain_p = _hgrn_bwd(proj, states, lb, gain_h, d_yh, dproj, B_loc, S)
    g_win = _matmul(h, dproj, "tn", BF16, "in_proj_wgrad", slabs=N_CHIPS)
    last = _reduce_start([g_win], names[:1], c_arr)
    dh = _matmul(dproj, win_f, "nt", F32, "in_proj_dgrad", after=last.token)
    grad_x2, gnpre_p = _rmsnorm_bwd(dh, x2, rstd, norm_pre, dout)
    shared = _reduce_finish(last, grad_x2, sc_arr) + _reduce_finish(late, grad_x2, sc_arr)

    grelb_p = jnp.dot(dbias_p.reshape(ATTN_HEADS, -1), onehot, precision=lax.Precision.HIGHEST).T
    gsinks_p = dsinks_p[:, 0, 0]
    dlb_sum = jnp.sum(dlb_p, axis=0).reshape(1, HGRN_WIDTH)
    ghn_p = jnp.sum(dgain_p, axis=0).reshape(HGRN_HEADS, HGRN_DIM)
    small_parts = [gnpre_p, gnpost_p, grelb_p, gsinks_p, dlb_sum, ghn_p, loss_p]
    small_shapes = [p.shape for p in small_parts]
    pack_sum = _sum_small(_pack_rows(small_parts))
    big_w = [w_in, w_branch_attn, w_branch_hgrn, w_out]
    big_m = [m_w_in, m_w_branch_attn, m_w_branch_hgrn, m_w_out]
    big_v = [v_w_in, v_w_branch_attn, v_w_branch_hgrn, v_w_out]
    big = {}
    for nm, gs, w, m, v in zip(names, shared, big_w, big_m, big_v):
        shp = w.shape
        g2 = gs.reshape(shp[1], shp[2])
        d, nm_, nv_ = _adamw(w[0], g2, m[0], v[0], "adamw_" + nm)
        big[nm] = tuple(a.reshape(shp) for a in (g2, d, nm_, nv_))

    gnpre, gnpost, grelb, gsinks, dlb, ghn, loss = _unpack_rows(pack_sum, small_shapes)
    (g_lb_logits,) = lb_vjp(dlb)
    small_names = ["norm_pre", "rel_bias", "attn_sinks", "lb_logits", "hgrn_norm", "norm_post"]
    small_w = [norm_pre, rel_bias, attn_sinks, lb_logits, hgrn_norm, norm_post]
    small_m = [m_norm_pre, m_rel_bias, m_attn_sinks, m_lb_logits, m_hgrn_norm, m_norm_post]
    small_v = [v_norm_pre, v_rel_bias, v_attn_sinks, v_lb_logits, v_hgrn_norm, v_norm_post]
    small_g = [gnpre.reshape(norm_pre.shape), grelb.reshape(rel_bias.shape), gsinks.reshape(attn_sinks.shape),
               g_lb_logits.reshape(lb_logits.shape), ghn.reshape(hgrn_norm.shape), gnpost.reshape(norm_post.shape)]
    shapes = [w.shape for w in small_w]
    d_s, nm_s, nv_s = _adamw(_pack_rows(small_w), _pack_rows(small_g), _pack_rows(small_m), _pack_rows(small_v),
                             "adamw_small")
    small = {}
    for nm, g, d, m_, v_ in zip(small_names, small_g, _unpack_rows(d_s, shapes), _unpack_rows(nm_s, shapes),
                                _unpack_rows(nv_s, shapes)):
        small[nm] = (g, d, m_, v_)

    allw = {**big, **small}
    order = ["norm_pre", "w_in", "rel_bias", "attn_sinks", "lb_logits", "hgrn_norm", "w_branch_attn", "w_branch_hgrn",
             "w_out", "norm_post"]
    outs = [loss.reshape(()), grad_x2.reshape(B_loc, S, D)]
    for k in range(4):
        outs += [allw[nm][k] for nm in order]
    return tuple(outs)
```

```python
import functools
import math

import jax
import jax.numpy as jnp
from jax import lax
from jax.experimental import pallas as pl
from jax.experimental.pallas import tpu as pltpu

F32 = jnp.float32
BF16 = jnp.bfloat16
MESH = pl.DeviceIdType.MESH

ATTN_HEADS = 16
ATTN_KV_HEADS = 4
HEAD_DIM = 64
GROUP = ATTN_HEADS // ATTN_KV_HEADS
WINDOW = 128
ATTN_WIDTH = ATTN_HEADS * HEAD_DIM
KV_WIDTH = ATTN_KV_HEADS * HEAD_DIM
HGRN_HEADS = 8
HGRN_DIM = 128
HGRN_WIDTH = HGRN_HEADS * HGRN_DIM
CHUNK = 64
SUB = 16
NSUB = CHUNK // SUB
REL_BUCKETS = 32
REL_MAX_DIST = 128
NORM_EPS = 1e-6
ADAM_LR = 0.001
ADAM_B1 = 0.9
ADAM_B2 = 0.999
ADAM_EPS = 1e-08
ADAM_WD = 0.01
ADAM_STEP = 10
LANES = 128
N_CHIPS = 4
N_DEV = 8
VMEM_LIMIT = 48 * 1024 * 1024
MATMUL_OPERAND_BYTES = 20 * 1024 * 1024

OFF_AQ = 0
OFF_AK = OFF_AQ + ATTN_WIDTH
OFF_AV = OFF_AK + KV_WIDTH
OFF_AG = OFF_AV + KV_WIDTH
OFF_HQ = OFF_AG + ATTN_WIDTH
OFF_HF = OFF_HQ + HGRN_WIDTH
OFF_HI = OFF_HF + HGRN_WIDTH
OFF_HG = OFF_HI + HGRN_WIDTH
OFF_GA = OFF_HG + HGRN_WIDTH

NT_DIMS = (((1,), (1,)), ((), ()))
TN_DIMS = (((0,), (0,)), ((), ()))
NN_DIMS = (((1,), (0,)), ((), ()))


def _pick(n, cands):
    for c in cands:
        if n % c == 0:
            return c
    raise ValueError(f"no tile for {n} in {cands}")


def _params(sem):
    return pltpu.CompilerParams(dimension_semantics=sem, vmem_limit_bytes=VMEM_LIMIT)


def _bdot(a, b, dims=NN_DIMS):
    return lax.dot_general(a.astype(BF16), b.astype(BF16), dims, preferred_element_type=F32)


def _matmul(a, b, mode, out_dtype, name, slabs=1, after=None):
    if mode == "nn":
        (M, K), (K2, N) = a.shape, b.shape
    elif mode == "nt":
        (M, K), (N, K2) = a.shape, b.shape
    else:
        (K, M), (K2, N) = a.shape, b.shape
    assert K == K2
    nslab = N // slabs
    tm = _pick(M, (1024, 512, 256, 128))
    tn = _pick(nslab, (2688, 1024, 896, 512, 384, 256, 128))
    tk = _pick(K, [t for t in (4096, 2048, 1792, 1536, 1024, 512, 256, 128) if 4 * t * (tm + tn) <= MATMUL_OPERAND_BYTES])
    nk = K // tk
    per = nslab // tn
    dims = {"nn": NN_DIMS, "nt": NT_DIMS, "tn": TN_DIMS}[mode]

    n_in = 2 if after is None else 3

    def body(*refs):
        a_ref, b_ref, o_ref, acc = refs[0], refs[1], refs[n_in], refs[n_in + 1:]
        part = lax.dot_general(a_ref[...], b_ref[...], dims, preferred_element_type=F32)
        if nk == 1:
            o_ref[...] = part.astype(o_ref.dtype)
            return
        acc_ref, = acc
        k = pl.program_id(2)

        @pl.when(k == 0)
        def _():
            acc_ref[...] = part

        @pl.when((k > 0) & (k < nk - 1))
        def _():
            acc_ref[...] += part

        @pl.when(k == nk - 1)
        def _():
            o_ref[...] = (acc_ref[...] + part).astype(o_ref.dtype)

    if mode == "tn":
        a_spec = pl.BlockSpec((tk, tm), lambda i, j, k: (k, i))
    else:
        a_spec = pl.BlockSpec((tm, tk), lambda i, j, k: (i, k))
    if mode == "nt":
        b_spec = pl.BlockSpec((tn, tk), lambda i, j, k: (j, k))
    else:
        b_spec = pl.BlockSpec((tk, tn), lambda i, j, k: (k, j))
    if slabs == 1:
        o_shape = jax.ShapeDtypeStruct((M, N), out_dtype)
        o_spec = pl.BlockSpec((tm, tn), lambda i, j, k: (i, j))
    else:
        o_shape = jax.ShapeDtypeStruct((slabs, M, nslab), out_dtype)
        o_spec = pl.BlockSpec((None, tm, tn), lambda i, j, k: (j // per, i, j % per))
    return pl.pallas_call(
        body, name=name, grid=(M // tm, N // tn, nk), in_specs=[a_spec, b_spec] + ([] if after is None else [ANY]),
        out_specs=o_spec, out_shape=o_shape,
        scratch_shapes=[pltpu.VMEM((tm, tn), F32)] if nk > 1 else [],
        compiler_params=_params(("parallel", "parallel", "arbitrary")),
    )(*((a, b) if after is None else (a, b, after)))


def _rmsnorm_fwd(x2, gain):
    T, D = x2.shape
    tr = _pick(T, (256, 128))

    def body(x_ref, g_ref, h_ref, r_ref):
        xv = x_ref[...]
        r = lax.rsqrt(jnp.mean(xv * xv, axis=-1, keepdims=True) + NORM_EPS)
        h_ref[...] = (xv * r * g_ref[...]).astype(BF16)
        r_ref[...] = r

    return pl.pallas_call(
        body, name="rmsnorm_pre_fwd", grid=(T // tr,),
        in_specs=[pl.BlockSpec((tr, D), lambda i: (i, 0)), pl.BlockSpec((1, D), lambda i: (0, 0))],
        out_specs=[pl.BlockSpec((tr, D), lambda i: (i, 0)), pl.BlockSpec((tr, 1), lambda i: (i, 0))],
        out_shape=[jax.ShapeDtypeStruct((T, D), BF16), jax.ShapeDtypeStruct((T, 1), F32)],
        compiler_params=_params(("parallel",)),
    )(x2, gain)


def _rmsnorm_bwd(dh, x2, rstd, gain, dout):
    T, D = x2.shape
    tr = _pick(T, (256, 128))

    def body(dh_ref, x_ref, r_ref, g_ref, do_ref, gx_ref, gg_ref):
        @pl.when(pl.program_id(0) == 0)
        def _():
            gg_ref[...] = jnp.zeros_like(gg_ref)

        n = x_ref[...] * r_ref[...]
        dhv = dh_ref[...]
        dn = dhv * g_ref[...]
        gx_ref[...] = do_ref[...] + r_ref[...] * (dn - n * jnp.mean(dn * n, axis=-1, keepdims=True))
        gg_ref[...] += jnp.sum(dhv * n, axis=0, keepdims=True)

    row = pl.BlockSpec((tr, D), lambda i: (i, 0))
    vec = pl.BlockSpec((1, D), lambda i: (0, 0))
    return pl.pallas_call(
        body, name="rmsnorm_pre_bwd", grid=(T // tr,),
        in_specs=[row, row, pl.BlockSpec((tr, 1), lambda i: (i, 0)), vec, row],
        out_specs=[row, vec],
        out_shape=[jax.ShapeDtypeStruct((T, D), F32), jax.ShapeDtypeStruct((1, D), F32)],
        compiler_params=_params(("arbitrary",)),
    )(dh, x2, rstd, gain, dout)


def _post_loss(yv, x2, tgt2, gain):
    T, D = x2.shape
    tr = _pick(T, (256, 128))

    def body(y_ref, x_ref, t_ref, g_ref, dy_ref, do_ref, loss_ref, gg_ref):
        @pl.when(pl.program_id(0) == 0)
        def _():
            gg_ref[...] = jnp.zeros_like(gg_ref)
            loss_ref[...] = jnp.zeros_like(loss_ref)

        yv_ = y_ref[...]
        r = lax.rsqrt(jnp.mean(yv_ * yv_, axis=-1, keepdims=True) + NORM_EPS)
        n = yv_ * r
        e = (x_ref[...] + n * g_ref[...]) - t_ref[...]
        loss_ref[...] += 0.5 * jnp.sum(jnp.mean(e * e, axis=-1, keepdims=True), axis=0, keepdims=True)
        dz = e / D
        do_ref[...] = dz
        gg_ref[...] += jnp.sum(dz * n, axis=0, keepdims=True)
        dn = dz * g_ref[...]
        dy_ref[...] = (r * (dn - n * jnp.mean(dn * n, axis=-1, keepdims=True))).astype(BF16)

    row = pl.BlockSpec((tr, D), lambda i: (i, 0))
    vec = pl.BlockSpec((1, D), lambda i: (0, 0))
    return pl.pallas_call(
        body, name="post_norm_loss", grid=(T // tr,),
        in_specs=[row, row, row, vec],
        out_specs=[row, row, pl.BlockSpec((1, 1), lambda i: (0, 0)), vec],
        out_shape=[jax.ShapeDtypeStruct((T, D), BF16), jax.ShapeDtypeStruct((T, D), F32),
                   jax.ShapeDtypeStruct((1, 1), F32), jax.ShapeDtypeStruct((1, D), F32)],
        compiler_params=_params(("arbitrary",)),
    )(yv, x2, tgt2, gain)


def _merge_fwd(proj, ua, uh):
    T, D = ua.shape
    tr = _pick(T, (256, 128))
    bw = _pick(D, (512, 256))
    assert OFF_GA % bw == 0
    oa, oh = OFF_GA // bw, (OFF_GA + D) // bw

    def body(ga_ref, gh_ref, ua_ref, uh_ref, m_ref):
        m_ref[...] = (jax.nn.sigmoid(ga_ref[...]) * ua_ref[...].astype(F32)
                      + jax.nn.sigmoid(gh_ref[...]) * uh_ref[...].astype(F32)).astype(BF16)

    blk = pl.BlockSpec((tr, bw), lambda i, j: (i, j))
    return pl.pallas_call(
        body, name="merge_fwd", grid=(T // tr, D // bw),
        in_specs=[pl.BlockSpec((tr, bw), lambda i, j: (i, oa + j)), pl.BlockSpec((tr, bw), lambda i, j: (i, oh + j)), blk, blk],
        out_specs=blk, out_shape=jax.ShapeDtypeStruct((T, D), BF16),
        compiler_params=_params(("parallel", "parallel")),
    )(proj, proj, ua, uh)


def _window(rows, cols, at):
    return pl.BlockSpec((pl.Element(rows), pl.Element(cols)), at)


def _merge_bwd(proj, ua, uh, dm):
    T, D = ua.shape
    tr = _pick(T, (128,))

    def body(ga_ref, gh_ref, ua_ref, uh_ref, dm_ref, dua_ref, duh_ref, dproj_ref):
        sa = jax.nn.sigmoid(ga_ref[...])
        sh = jax.nn.sigmoid(gh_ref[...])
        d = dm_ref[...]
        dua_ref[...] = (d * sa).astype(BF16)
        duh_ref[...] = (d * sh).astype(BF16)
        dproj_ref[:, :D] = (d * ua_ref[...].astype(F32) * sa * (1.0 - sa)).astype(BF16)
        dproj_ref[:, D:] = (d * uh_ref[...].astype(F32) * sh * (1.0 - sh)).astype(BF16)

    blk = pl.BlockSpec((tr, D), lambda i: (i, 0))
    o = jax.ShapeDtypeStruct((T, D), BF16)
    return pl.pallas_call(
        body, name="merge_bwd", grid=(T // tr,),
        in_specs=[_window(tr, D, lambda i: (i * tr, OFF_GA)), _window(tr, D, lambda i: (i * tr, OFF_GA + D)), blk, blk, blk],
        out_specs=[blk, blk, _window(tr, 2 * D, lambda i: (i * tr, OFF_GA))],
        out_shape=[o, o, jax.ShapeDtypeStruct((T, proj.shape[1]), BF16)],
        compiler_params=_params(("parallel",)),
    )(proj, proj, ua, uh, dm)


KV_PAIR = 2
PAIR_HEADS = KV_PAIR * GROUP


def _attn_mask(n):
    qi = lax.broadcasted_iota(jnp.int32, (WINDOW, 2 * WINDOW), 0)
    si = lax.broadcasted_iota(jnp.int32, (WINDOW, 2 * WINDOW), 1)
    dist = qi + WINDOW - si
    return (dist >= 0) & (dist < WINDOW) & ((si >= WINDOW) | (n > 0))


def _attn_group_fn(mask):
    mask = jnp.concatenate([mask] * GROUP, axis=0)

    def f(q, k, v, ag, bias, sink):
        s = lax.dot_general(q.astype(BF16), k.astype(BF16), NT_DIMS, preferred_element_type=F32) * (HEAD_DIM ** -0.5)
        s = jnp.where(mask, s + bias, -1e30)
        m = lax.stop_gradient(jnp.maximum(jnp.max(s, axis=-1, keepdims=True), sink))
        p = jnp.exp(s - m)
        den = jnp.sum(p, axis=-1, keepdims=True) + jnp.exp(sink - m)
        o = jnp.dot(p.astype(BF16), v.astype(BF16), preferred_element_type=F32)
        return o * (jax.nn.silu(ag) / den)

    return f


def _attn_specs(B_loc, nb, order):
    qw = PAIR_HEADS * HEAD_DIM
    kw = KV_PAIR * HEAD_DIM

    def rows(g):
        b, p, n = order(*g)
        return b * nb + n

    def prev(g):
        b, p, n = order(*g)
        return b * nb + jnp.maximum(n - 1, 0)

    def pp(g):
        return order(*g)[1]

    q = pl.BlockSpec((WINDOW, qw), lambda *g: (rows(g), OFF_AQ // qw + pp(g)))
    kc = pl.BlockSpec((WINDOW, kw), lambda *g: (rows(g), OFF_AK // kw + pp(g)))
    kp = pl.BlockSpec((WINDOW, kw), lambda *g: (prev(g), OFF_AK // kw + pp(g)))
    vc = pl.BlockSpec((WINDOW, kw), lambda *g: (rows(g), OFF_AV // kw + pp(g)))
    vp = pl.BlockSpec((WINDOW, kw), lambda *g: (prev(g), OFF_AV // kw + pp(g)))
    ag = pl.BlockSpec((WINDOW, qw), lambda *g: (rows(g), OFF_AG // qw + pp(g)))
    bias = pl.BlockSpec((PAIR_HEADS, WINDOW, 2 * WINDOW), lambda *g: (pp(g), 0, 0))
    sink = pl.BlockSpec((PAIR_HEADS, 8, LANES), lambda *g: (pp(g), 0, 0))
    return [q, kc, kp, vc, vp, ag, bias, sink], rows, pp


def _attn_operands(q_ref, kc_ref, kp_ref, vc_ref, vp_ref, ag_ref, bias_ref, sink_ref, j):
    lo, hi = j * HEAD_DIM, (j + 1) * HEAD_DIM
    k = jnp.concatenate([kp_ref[:, lo:hi], kc_ref[:, lo:hi]], axis=0)
    v = jnp.concatenate([vp_ref[:, lo:hi], vc_ref[:, lo:hi]], axis=0)
    heads = [j * GROUP + g for g in range(GROUP)]
    q = jnp.concatenate([q_ref[:, h * HEAD_DIM:(h + 1) * HEAD_DIM] for h in heads], axis=0)
    ag = jnp.concatenate([ag_ref[:, h * HEAD_DIM:(h + 1) * HEAD_DIM] for h in heads], axis=0)
    bias = jnp.concatenate([bias_ref[h] for h in heads], axis=0)
    sink = jnp.concatenate([jnp.broadcast_to(sink_ref[h, 0:1, 0:1], (WINDOW, 1)) for h in heads], axis=0)
    return q, k, v, ag, bias, sink


def _attn_fwd(proj, bias_tab, sinks_b, B_loc, S):
    T = B_loc * S
    nb = S // WINDOW
    n_pairs = ATTN_KV_HEADS // KV_PAIR
    in_specs, rows, pp = _attn_specs(B_loc, nb, lambda b, p, n: (b, p, n))

    def body(q_ref, kc_ref, kp_ref, vc_ref, vp_ref, ag_ref, bias_ref, sink_ref, ya_ref):
        f = _attn_group_fn(_attn_mask(pl.program_id(2)))
        for j in range(KV_PAIR):
            out = f(*_attn_operands(q_ref, kc_ref, kp_ref, vc_ref, vp_ref, ag_ref, bias_ref, sink_ref, j))
            for g in range(GROUP):
                h = j * GROUP + g
                ya_ref[:, h * HEAD_DIM:(h + 1) * HEAD_DIM] = out[g * WINDOW:(g + 1) * WINDOW].astype(BF16)

    return pl.pallas_call(
        body, name="attn_fwd", grid=(B_loc, n_pairs, nb), in_specs=in_specs,
        out_specs=pl.BlockSpec((WINDOW, PAIR_HEADS * HEAD_DIM), lambda *g: (rows(g), pp(g))),
        out_shape=jax.ShapeDtypeStruct((T, ATTN_WIDTH), BF16),
        compiler_params=_params(("parallel", "parallel", "parallel")),
    )(proj, proj, proj, proj, proj, proj, bias_tab, sinks_b)


def _attn_bwd(proj, bias_tab, sinks_b, dya, dproj, B_loc, S):
    nb = S // WINDOW
    n_pairs = ATTN_KV_HEADS // KV_PAIR
    qw, kw = PAIR_HEADS * HEAD_DIM, KV_PAIR * HEAD_DIM

    def at(off, width, back=0):
        return lambda b, i, p: ((b * nb + jnp.maximum(nb - 1 - i - back, 0)) * WINDOW,
                                pl.multiple_of(off + p * width, LANES))

    in_specs = [_window(WINDOW, qw, at(OFF_AQ, qw)), _window(WINDOW, kw, at(OFF_AK, kw)),
                _window(WINDOW, kw, at(OFF_AK, kw, 1)), _window(WINDOW, kw, at(OFF_AV, kw)),
                _window(WINDOW, kw, at(OFF_AV, kw, 1)), _window(WINDOW, qw, at(OFF_AG, qw)),
                pl.BlockSpec((PAIR_HEADS, WINDOW, 2 * WINDOW), lambda b, i, p: (p, 0, 0)),
                pl.BlockSpec((PAIR_HEADS, 8, LANES), lambda b, i, p: (p, 0, 0)),
                pl.BlockSpec((WINDOW, qw), lambda b, i, p: (b * nb + nb - 1 - i, p)), ANY]

    def body(q_ref, kc_ref, kp_ref, vc_ref, vp_ref, ag_ref, bias_ref, sink_ref, dya_ref, dproj_in,
             dproj_ref, dbias_ref, dsink_ref, dkc_ref, dvc_ref):
        b, i, p = pl.program_id(0), pl.program_id(1), pl.program_id(2)
        n = nb - 1 - i

        @pl.when((b == 0) & (i == 0) & (p == 0))
        def _():
            dbias_ref[...] = jnp.zeros_like(dbias_ref)
            dsink_ref[...] = jnp.zeros_like(dsink_ref)

        @pl.when(i == 0)
        def _():
            dkc_ref[p] = jnp.zeros((WINDOW, kw), F32)
            dvc_ref[p] = jnp.zeros((WINDOW, kw), F32)

        f = _attn_group_fn(_attn_mask(n))
        dk_carry, dv_carry = dkc_ref[p], dvc_ref[p]
        dqs, dags, dbiases, dsinks, dks, dvs = [], [], [], [], [], []
        for j in range(KV_PAIR):
            ops = _attn_operands(q_ref, kc_ref, kp_ref, vc_ref, vp_ref, ag_ref, bias_ref, sink_ref, j)
            _, vjp = jax.vjp(f, *ops)
            dout = jnp.concatenate([dya_ref[:, (j * GROUP + g) * HEAD_DIM:(j * GROUP + g + 1) * HEAD_DIM].astype(F32)
                                    for g in range(GROUP)], axis=0)
            dq, dk, dv, dag, dbias, dsink = vjp(dout)
            lo, hi = j * HEAD_DIM, (j + 1) * HEAD_DIM
            dks.append((dk[WINDOW:] + dk_carry[:, lo:hi]).astype(BF16))
            dvs.append((dv[WINDOW:] + dv_carry[:, lo:hi]).astype(BF16))
            dkc_ref[p, :, lo:hi] = dk[:WINDOW]
            dvc_ref[p, :, lo:hi] = dv[:WINDOW]
            for g in range(GROUP):
                blk = slice(g * WINDOW, (g + 1) * WINDOW)
                dqs.append(dq[blk].astype(BF16))
                dags.append(dag[blk].astype(BF16))
                dbiases.append(dbias[blk])
                dsinks.append(jnp.broadcast_to(jnp.sum(dsink[blk], axis=0, keepdims=True), (8, LANES)))

        for pair in range(n_pairs):
            @pl.when(p == pair)
            def _():
                for j in range(KV_PAIR):
                    col = (pair * KV_PAIR + j) * HEAD_DIM
                    dproj_ref[:, OFF_AK + col:OFF_AK + col + HEAD_DIM] = dks[j]
                    dproj_ref[:, OFF_AV + col:OFF_AV + col + HEAD_DIM] = dvs[j]
                for hh in range(PAIR_HEADS):
                    h = pair * PAIR_HEADS + hh
                    dproj_ref[:, OFF_AQ + h * HEAD_DIM:OFF_AQ + (h + 1) * HEAD_DIM] = dqs[hh]
                    dproj_ref[:, OFF_AG + h * HEAD_DIM:OFF_AG + (h + 1) * HEAD_DIM] = dags[hh]
                    dbias_ref[h] += dbiases[hh]
                    dsink_ref[h] += dsinks[hh]

    return pl.pallas_call(
        body, name="attn_bwd", grid=(B_loc, nb, n_pairs), in_specs=in_specs,
        out_specs=[_window(WINDOW, OFF_HQ, lambda b, i, p: ((b * nb + nb - 1 - i) * WINDOW, 0)),
                   pl.BlockSpec((ATTN_HEADS, WINDOW, 2 * WINDOW), lambda b, i, p: (0, 0, 0)),
                   pl.BlockSpec((ATTN_HEADS, 8, LANES), lambda b, i, p: (0, 0, 0))],
        out_shape=[jax.ShapeDtypeStruct(dproj.shape, dproj.dtype),
                   jax.ShapeDtypeStruct((ATTN_HEADS, WINDOW, 2 * WINDOW), F32),
                   jax.ShapeDtypeStruct((ATTN_HEADS, 8, LANES), F32)],
        input_output_aliases={9: 0},
        scratch_shapes=[pltpu.VMEM((n_pairs, WINDOW, kw), F32), pltpu.VMEM((n_pairs, WINDOW, kw), F32)],
        compiler_params=_params(("arbitrary", "arbitrary", "arbitrary")),
    )(proj, proj, proj, proj, proj, proj, bias_tab, sinks_b, dya, dproj)


class _HgrnPre:
    def __init__(self, fr, qr, lb, g_scr):
        t = lax.broadcasted_iota(jnp.int32, (CHUNK, CHUNK), 0)
        s = lax.broadcasted_iota(jnp.int32, (CHUNK, CHUNK), 1)
        self.sg = jax.nn.sigmoid(fr)
        self.f = lb + (1.0 - lb) * self.sg
        g = jnp.dot((t >= s).astype(F32), jnp.log(self.f), precision=lax.Precision.HIGHEST, preferred_element_type=F32)
        g_scr[...] = g
        self.g = g
        self.row = lax.broadcasted_iota(jnp.int32, g.shape, 0)
        self.refs = [jnp.zeros((1, g.shape[1]), F32)] + [g_scr[pl.ds(i * SUB - 1, 1), :] for i in range(1, NSUB)]
        self.gend = g_scr[pl.ds(CHUNK - 1, 1), :]
        refrow = jnp.zeros_like(g)
        for i in range(1, NSUB):
            refrow = jnp.where(self.row >= i * SUB, self.refs[i], refrow)
        self.sigq = jax.nn.sigmoid(qr)
        self.qs = qr * self.sigq
        self.k = 1.0 - self.f
        self.eg = jnp.exp(g)
        self.eqd = jnp.exp(g - refrow)
        self.ekd = [jnp.exp(jnp.where(self.row < (i + 1) * SUB, self.refs[i] - g, 0.0)) for i in range(NSUB)]
        self.ekdec = jnp.exp(self.gend - g)
        self.qg = self.qs * self.eg
        self.qd = self.qs * self.eqd
        self.kd = [self.k * e for e in self.ekd]
        self.kdec = self.k * self.ekdec
        self.egend = jnp.exp(self.gend)


def _hgrn_pair_mask():
    t = lax.broadcasted_iota(jnp.int32, (CHUNK, NSUB * CHUNK), 0)
    col = lax.broadcasted_iota(jnp.int32, (CHUNK, NSUB * CHUNK), 1)
    return ((t // SUB) == (col // CHUNK)) & ((col % CHUNK) <= t)


def _hgrn_head_out(p, lanes, state_t, v, mask):
    qg, qd = p.qg[:, lanes], p.qd[:, lanes]
    kall = jnp.concatenate([kd[:, lanes] for kd in p.kd], axis=0)
    vst = jnp.concatenate([v] * NSUB, axis=0)
    am = jnp.where(mask, _bdot(qd, kall, NT_DIMS), 0.0)
    o = _bdot(qg, state_t, NT_DIMS) + _bdot(am, vst)
    return o, (qg, qd, kall, am, vst)


def _hgrn_fwd(proj, lb, gain, B_loc, S):
    T = B_loc * S
    nc = S // CHUNK
    nh = HGRN_HEADS

    def at(off):
        return lambda b, n: ((b * nc + n) * CHUNK, off)

    vec = pl.BlockSpec((1, HGRN_WIDTH), lambda b, n: (0, 0))

    def body(q_ref, f_ref, v_ref, hg_ref, lb_ref, gain_ref, yh_ref, st_ref, state_scr, g_scr):
        @pl.when(pl.program_id(1) == 0)
        def _():
            state_scr[...] = jnp.zeros_like(state_scr)

        p = _HgrnPre(f_ref[...], q_ref[...], lb_ref[...], g_scr)
        v = v_ref[...]
        gate = gain_ref[...] * jax.nn.silu(hg_ref[...])
        mask = _hgrn_pair_mask()
        for hd in range(nh):
            lanes = slice(hd * HGRN_DIM, (hd + 1) * HGRN_DIM)
            st = state_scr[hd]
            st_ref[hd] = st
            o, _ = _hgrn_head_out(p, lanes, st, v[:, lanes], mask)
            rs = lax.rsqrt(jnp.mean(o * o, axis=-1, keepdims=True) + NORM_EPS)
            yh_ref[:, lanes] = (o * rs * gate[:, lanes]).astype(BF16)
            state_scr[hd] = st * p.egend[:, lanes] + _bdot(v[:, lanes], p.kdec[:, lanes], TN_DIMS)

    return pl.pallas_call(
        body, name="hgrn_fwd", grid=(B_loc, nc),
        in_specs=[_window(CHUNK, HGRN_WIDTH, at(OFF_HQ)), _window(CHUNK, HGRN_WIDTH, at(OFF_HF)),
                  _window(CHUNK, HGRN_WIDTH, at(OFF_HI)), _window(CHUNK, HGRN_WIDTH, at(OFF_HG)), vec, vec],
        out_specs=[pl.BlockSpec((CHUNK, HGRN_WIDTH), lambda b, n: (b * nc + n, 0)),
                   pl.BlockSpec((None, None, nh, HGRN_DIM, HGRN_DIM), lambda b, n: (b, n, 0, 0, 0))],
        out_shape=[jax.ShapeDtypeStruct((T, HGRN_WIDTH), BF16),
                   jax.ShapeDtypeStruct((B_loc, nc, nh, HGRN_DIM, HGRN_DIM), F32)],
        scratch_shapes=[pltpu.VMEM((nh, HGRN_DIM, HGRN_DIM), F32), pltpu.VMEM((CHUNK, HGRN_WIDTH), F32)],
        compiler_params=_params(("parallel", "arbitrary")),
    )(proj, proj, proj, proj, lb, gain)


def _hgrn_bwd(proj, states, lb, gain, dyh, dproj, B_loc, S):
    nc = S // CHUNK
    nh = HGRN_HEADS

    def at(off):
        return lambda b, i: ((b * nc + nc - 1 - i) * CHUNK, off)

    vec = pl.BlockSpec((1, HGRN_WIDTH), lambda b, i: (0, 0))
    in_specs = [_window(CHUNK, HGRN_WIDTH, at(OFF_HQ)), _window(CHUNK, HGRN_WIDTH, at(OFF_HF)),
                _window(CHUNK, HGRN_WIDTH, at(OFF_HI)), _window(CHUNK, HGRN_WIDTH, at(OFF_HG)), vec, vec,
                pl.BlockSpec((None, None, nh, HGRN_DIM, HGRN_DIM), lambda b, i: (b, nc - 1 - i, 0, 0, 0)),
                pl.BlockSpec((CHUNK, HGRN_WIDTH), lambda b, i: (b * nc + nc - 1 - i, 0)), ANY]
    acc_spec = pl.BlockSpec((None, 1, HGRN_WIDTH), lambda b, i: (b, 0, 0))

    def body(q_ref, f_ref, v_ref, hg_ref, lb_ref, gain_ref, st_ref, dyh_ref, dproj_in,
             dproj_ref, dlb_ref, dgain_ref, dstate_scr, g_scr, dg_scr):
        dq_ref, df_ref, dv_ref, dhg_ref = [dproj_ref.at[:, pl.ds(k * HGRN_WIDTH, HGRN_WIDTH)] for k in range(4)]

        @pl.when(pl.program_id(1) == 0)
        def _():
            dstate_scr[...] = jnp.zeros_like(dstate_scr)
            dlb_ref[...] = jnp.zeros_like(dlb_ref)
            dgain_ref[...] = jnp.zeros_like(dgain_ref)

        qr, lb, gain, hg, v = q_ref[...], lb_ref[...], gain_ref[...], hg_ref[...], v_ref[...]
        p = _HgrnPre(f_ref[...], qr, lb, g_scr)
        sgh = jax.nn.sigmoid(hg)
        sil = hg * sgh
        dy = dyh_ref[...].astype(F32)
        mask = _hgrn_pair_mask()
        dqg, dqd, dkdec, dv, dhg, dgend, dgain = [], [], [], [], [], [], []
        dkd = [[] for _ in range(NSUB)]
        heads = [slice(hd * HGRN_DIM, (hd + 1) * HGRN_DIM) for hd in range(nh)]
        sts = [st_ref[hd] for hd in range(nh)]
        dnews = [dstate_scr[hd] for hd in range(nh)]
        fwd = [_hgrn_head_out(p, lanes, st, v[:, lanes], mask) for lanes, st in zip(heads, sts)]
        for lanes, st, dnew in zip(heads, sts, dnews):
            dkdec_h = _bdot(v[:, lanes], dnew)
            dkdec.append(dkdec_h)
            dgend.append(jnp.sum(dkdec_h * p.kdec[:, lanes], axis=0, keepdims=True)
                         + jnp.sum(dnew * st, axis=0, keepdims=True) * p.egend[:, lanes])
        dos = []
        gate_grad = sgh * (1.0 + hg * (1.0 - sgh))
        for lanes, (o, _) in zip(heads, fwd):
            rs = lax.rsqrt(jnp.mean(o * o, axis=-1, keepdims=True) + NORM_EPS)
            n = o * rs
            dyn = dy[:, lanes] * n
            dgain.append(jnp.sum(dyn * sil[:, lanes], axis=0, keepdims=True))
            dhg.append(dyn * gain[:, lanes] * gate_grad[:, lanes])
            dn = dy[:, lanes] * gain[:, lanes] * sil[:, lanes]
            dos.append(rs * (dn - n * jnp.mean(dn * n, axis=-1, keepdims=True)))
        drs = []
        for hd, (lanes, st, dnew, do, (_, (qg, qd, kall, am, vst))) in enumerate(zip(heads, sts, dnews, dos, fwd)):
            dqg.append(_bdot(do, st))
            dstate_scr[hd] = _bdot(do, qg, TN_DIMS) + dnew * p.egend[:, lanes]
            drs.append(jnp.where(mask, _bdot(do, vst, NT_DIMS), 0.0))
            dvst = _bdot(am, do, TN_DIMS)
            dv.append(sum(dvst[i * CHUNK:(i + 1) * CHUNK] for i in range(NSUB)) + _bdot(p.kdec[:, lanes], dnew, NT_DIMS))
        for dr, (_, (qg, qd, kall, am, vst)) in zip(drs, fwd):
            dqd.append(_bdot(dr, kall))
            dkall = _bdot(dr, qd, TN_DIMS)
            for i in range(NSUB):
                dkd[i].append(dkall[i * CHUNK:(i + 1) * CHUNK])

        wide = lambda parts: jnp.concatenate(parts, axis=1)
        dqg, dqd, dkdec = wide(dqg), wide(dqd), wide(dkdec)
        t2 = dqd * p.qd
        dg = dqg * p.qg + t2 - dkdec * p.kdec
        dk = dkdec * p.ekdec
        dg_scr[...] = jnp.zeros_like(dg_scr)
        for i in range(NSUB):
            dkd_i = wide(dkd[i])
            tk = jnp.where(p.row < (i + 1) * SUB, dkd_i * p.kd[i], 0.0)
            dg = dg - tk
            dk = dk + dkd_i * p.ekd[i]
            if i >= 1:
                in_blk = (p.row >= i * SUB) & (p.row < (i + 1) * SUB)
                dg_scr[pl.ds(i * SUB - 1, 1), :] = (jnp.sum(tk, axis=0, keepdims=True)
                                                    - jnp.sum(jnp.where(in_blk, t2, 0.0), axis=0, keepdims=True))
        dg_scr[pl.ds(CHUNK - 1, 1), :] = wide(dgend)
        t = lax.broadcasted_iota(jnp.int32, (CHUNK, CHUNK), 0)
        s = lax.broadcasted_iota(jnp.int32, (CHUNK, CHUNK), 1)
        dlogf = jnp.dot((t <= s).astype(F32), dg + dg_scr[...], precision=lax.Precision.HIGHEST, preferred_element_type=F32)
        df = dlogf / p.f - dk
        df_ref[...] = (df * (1.0 - lb) * p.sg * (1.0 - p.sg)).astype(BF16)
        dlb_ref[...] += jnp.sum(df * (1.0 - p.sg), axis=0, keepdims=True)
        dq_ref[...] = ((dqg * p.eg + dqd * p.eqd) * p.sigq * (1.0 + qr * (1.0 - p.sigq))).astype(BF16)
        dv_ref[...] = wide(dv).astype(BF16)
        dhg_ref[...] = wide(dhg).astype(BF16)
        dgain_ref[...] += wide(dgain)

    acc = jax.ShapeDtypeStruct((B_loc, 1, HGRN_WIDTH), F32)
    return pl.pallas_call(
        body, name="hgrn_bwd", grid=(B_loc, nc), in_specs=in_specs,
        out_specs=[_window(CHUNK, 4 * HGRN_WIDTH, at(OFF_HQ)), acc_spec, acc_spec],
        out_shape=[jax.ShapeDtypeStruct(dproj.shape, dproj.dtype), acc, acc],
        input_output_aliases={8: 0},
        scratch_shapes=[pltpu.VMEM((nh, HGRN_DIM, HGRN_DIM), F32), pltpu.VMEM((CHUNK, HGRN_WIDTH), F32),
                        pltpu.VMEM((CHUNK, HGRN_WIDTH), F32)],
        compiler_params=_params(("parallel", "arbitrary")),
    )(proj, proj, proj, proj, lb, gain, states, dyh, dproj)


def _adamw(w, g, m, v, name):
    R, C = w.shape
    tr = _pick(R, (128, 64, 32, 16, 8)) if C > 1024 else _pick(R, (512, 256, 128, 64, 32, 16, 8))

    def body(w_ref, g_ref, m_ref, v_ref, d_ref, nm_ref, nv_ref):
        gv = g_ref[...]
        nm = ADAM_B1 * m_ref[...] + (1.0 - ADAM_B1) * gv
        nv = ADAM_B2 * v_ref[...] + (1.0 - ADAM_B2) * (gv * gv)
        m_hat = nm / (1.0 - ADAM_B1 ** ADAM_STEP)
        v_hat = nv / (1.0 - ADAM_B2 ** ADAM_STEP)
        d_ref[...] = -ADAM_LR * (m_hat / (jnp.sqrt(v_hat) + ADAM_EPS) + ADAM_WD * w_ref[...])
        nm_ref[...] = nm
        nv_ref[...] = nv

    blk = pl.BlockSpec((tr, C), lambda i: (i, 0))
    o = jax.ShapeDtypeStruct((R, C), F32)
    return pl.pallas_call(
        body, name=name, grid=(R // tr,), in_specs=[blk] * 4, out_specs=[blk] * 3, out_shape=[o, o, o],
        compiler_params=_params(("parallel",)),
    )(w, g, m, v)


ANY = pl.BlockSpec(memory_space=pl.ANY)
VMEM_SPEC = pl.BlockSpec(memory_space=pltpu.VMEM)


def _place():
    x, y, c = lax.axis_index("x"), lax.axis_index("y"), lax.axis_index("c")
    other_chips = [(1 - x, y), (x, 1 - y), (1 - x, 1 - y)]
    return x, y, c, other_chips


def _cast_into_full(w, ax, s_arr, name):
    R, C = w.shape
    tr = _pick(R, (256, 128))
    nr = R // tr

    def body(s_ref, w_ref, o_ref):
        o_ref[...] = w_ref[...].astype(BF16)

    if ax == 1:
        shape, o_map = (R, N_CHIPS * C), lambda i, s: (i, s[0])
    else:
        shape, o_map = (N_CHIPS * R, C), lambda i, s: (s[0] * nr + i, 0)
    return pl.pallas_call(
        body, name=name,
        grid_spec=pltpu.PrefetchScalarGridSpec(
            num_scalar_prefetch=1, grid=(nr,), in_specs=[pl.BlockSpec((tr, C), lambda i, s: (i, 0))],
            out_specs=pl.BlockSpec((tr, C), o_map)),
        out_shape=jax.ShapeDtypeStruct(shape, BF16),
        compiler_params=_params(("parallel",)),
    )(s_arr, w)


class _Gather:
    def __init__(self, fulls, shard_shapes, axes, tag, after=None):
        self.shapes, self.axes, self.tag, self.nw = shard_shapes, axes, tag, len(fulls)
        nw = self.nw

        def body(*refs):
            ins, sems = refs[:nw], refs[nw + (after is not None):nw + (after is not None) + 2 * (N_CHIPS - 1)]
            for j in range(N_CHIPS - 1):
                for cp in self._peer_copies(ins, sems[2 * j], sems[2 * j + 1], j):
                    cp.start()
            refs[-1][...] = jnp.zeros_like(refs[-1])

        out = pl.pallas_call(
            body, name="gather_start_" + tag,
            out_shape=(*[pltpu.SemaphoreType.DMA((nw,))] * (2 * (N_CHIPS - 1)),
                       *[pltpu.HBM(f.shape, f.dtype) for f in fulls], jax.ShapeDtypeStruct((8, LANES), F32)),
            in_specs=[HBM_SPEC] * nw + ([] if after is None else [ANY]),
            out_specs=(*[SEM_SPEC] * (2 * (N_CHIPS - 1)), *[HBM_SPEC] * nw, VMEM_SPEC),
            input_output_aliases={k: 2 * (N_CHIPS - 1) + k for k in range(nw)},
            compiler_params=pltpu.CompilerParams(has_side_effects=DATAFLOW),
        )(*[pltpu.with_memory_space_constraint(f, pltpu.HBM) for f in fulls], *(() if after is None else (after,)))
        self.sems = [(out[2 * j], out[2 * j + 1]) for j in range(N_CHIPS - 1)]
        self.fulls = list(out[2 * (N_CHIPS - 1):2 * (N_CHIPS - 1) + nw])
        self.token = out[-1]

    def _region(self, ref, i, t, half):
        R, C = self.shapes[i]
        hr = R // 2
        if self.axes[i] == 1:
            return ref.at[pl.ds(half * hr, hr), pl.ds(pl.multiple_of(t * C, LANES), C)]
        return ref.at[pl.ds(t * R + half * hr, hr), :]

    def _peer_copies(self, refs, send_sems, recv_sems, j):
        x, y, c, chips = _place()
        s = 2 * x + y
        return [pltpu.make_async_remote_copy(
            src_ref=self._region(refs[i], i, s, c), dst_ref=self._region(refs[i], i, s, c), send_sem=send_sems.at[i],
            recv_sem=recv_sems.at[i], device_id=(*chips[j], c), device_id_type=MESH) for i in range(self.nw)]

    def wait(self, peers, after):
        nw, np_ = self.nw, len(peers)

        def body(*refs):
            ins, sems = refs[:nw], refs[nw:nw + 2 * np_]
            for k, j in enumerate(peers):
                for cp in self._peer_copies(ins, sems[2 * k], sems[2 * k + 1], j):
                    cp.wait_send()
                    cp.wait_recv()

        sem_args = [s for j in peers for s in self.sems[j]]
        out = pl.pallas_call(
            body, name="gather_wait_%s_%s" % (self.tag, "".join(map(str, peers))),
            out_shape=tuple(pltpu.HBM(f.shape, f.dtype) for f in self.fulls),
            in_specs=[HBM_SPEC] * nw + [SEM_SPEC] * (2 * np_) + [ANY], out_specs=tuple([HBM_SPEC] * nw),
            input_output_aliases={k: k for k in range(nw)},
            compiler_params=pltpu.CompilerParams(has_side_effects=DATAFLOW),
        )(*self.fulls, *sem_args, after)
        self.fulls = list(out)

    def forward(self, peers):
        nw, np_ = self.nw, len(peers)

        def body(*refs):
            ins, outs = refs[:nw], refs[nw:2 * nw]
            send_sems, recv_sems = refs[2 * nw:]
            x, y, c, chips = _place()
            cps = []
            for i in range(nw):
                for k, j in enumerate(peers):
                    t = 2 * chips[j][0] + chips[j][1]
                    cp = pltpu.make_async_remote_copy(
                        src_ref=self._region(ins[i], i, t, c), dst_ref=self._region(outs[i], i, t, c),
                        send_sem=send_sems.at[i * np_ + k], recv_sem=recv_sems.at[i * np_ + k],
                        device_id=(x, y, 1 - c), device_id_type=MESH)
                    cp.start()
                    cps.append(cp)
            for cp in cps:
                cp.wait()

        out = pl.pallas_call(
            body, name="gather_forward_%s_%s" % (self.tag, "".join(map(str, peers))),
            in_specs=[ANY] * nw, out_specs=[ANY] * nw,
            out_shape=[jax.ShapeDtypeStruct(f.shape, f.dtype) for f in self.fulls],
            input_output_aliases={i: i for i in range(nw)},
            scratch_shapes=[pltpu.SemaphoreType.DMA((nw * np_,)), pltpu.SemaphoreType.DMA((nw * np_,))],
        )(*self.fulls)
        self.fulls = list(out)


def _matmul_slab(a, wfull, slab_arr, prev, after, name):
    M, K = a.shape
    N = wfull.shape[1]
    nslab = N // N_CHIPS
    tn = _pick(nslab, (2688, 896, 512, 384, 256, 128))
    tm = _pick(M, (512, 256, 128) if tn > 1024 else (1024, 512, 256, 128))
    per = nslab // tn
    extra = [e for e in (prev, after) if e is not None]

    def body(slab_ref, a_ref, b_ref, *rest):
        rest[len(extra)][...] = jnp.dot(a_ref[...], b_ref[...], preferred_element_type=F32)

    return pl.pallas_call(
        body, name=name,
        grid_spec=pltpu.PrefetchScalarGridSpec(
            num_scalar_prefetch=1, grid=(M // tm, per),
            in_specs=[pl.BlockSpec((tm, K), lambda i, j, sl: (i, 0)),
                      pl.BlockSpec((K, tn), lambda i, j, sl: (0, sl[0] * per + j))] + [ANY] * len(extra),
            out_specs=pl.BlockSpec((tm, tn), lambda i, j, sl: (i, sl[0] * per + j))),
        out_shape=jax.ShapeDtypeStruct((M, N), F32),
        input_output_aliases={} if prev is None else {3: 0},
        compiler_params=_params(("parallel", "arbitrary")),
    )(slab_arr, a, wfull, *extra)


def _exchange_sibling_halves(gs, name):
    nw = len(gs)

    def body(*refs):
        ins, outs = refs[:nw], refs[nw:2 * nw]
        send_sems, recv_sems = refs[2 * nw:]
        x, y, c, _ = _place()
        cps = []
        for i in range(nw):
            cp = pltpu.make_async_remote_copy(src_ref=ins[i].at[:, 1 - c], dst_ref=outs[i], send_sem=send_sems.at[i],
                                              recv_sem=recv_sems.at[i], device_id=(x, y, 1 - c), device_id_type=MESH)
            cp.start()
            cps.append(cp)
        for cp in cps:
            cp.wait()

    return pl.pallas_call(
        body, name=name, in_specs=[ANY] * nw, out_specs=[ANY] * nw,
        out_shape=[jax.ShapeDtypeStruct((g.shape[0],) + g.shape[2:], g.dtype) for g in gs],
        scratch_shapes=[pltpu.SemaphoreType.DMA((nw,)), pltpu.SemaphoreType.DMA((nw,))],
    )(*gs)


HBM_SPEC = pl.BlockSpec(memory_space=pltpu.HBM)
SEM_SPEC = pl.BlockSpec(memory_space=pltpu.SEMAPHORE)
DATAFLOW = pltpu.SideEffectType.DATAFLOW_SIDE_EFFECTING


def _chip_copies(ins, lands, send_sems, recv_sems):
    x, y, c, chips = _place()
    return [pltpu.make_async_remote_copy(
        src_ref=ins[i].at[2 * chip[0] + chip[1]], dst_ref=lands[i].at[j], send_sem=send_sems.at[i * 3 + j],
        recv_sem=recv_sems.at[i * 3 + j], device_id=(*chip, c), device_id_type=MESH)
        for i in range(len(ins)) for j, chip in enumerate(chips)]


def _chips_send_start(ss, name):
    nw = len(ss)
    lands = [pltpu.with_memory_space_constraint(lax.empty((N_CHIPS - 1,) + s.shape[1:], s.dtype), pltpu.HBM) for s in ss]

    def body(*refs):
        ins, land_refs = refs[:nw], refs[nw:2 * nw]
        send_sems, recv_sems = refs[2 * nw], refs[2 * nw + 1]
        token = refs[-1]
        for cp in _chip_copies(ins, land_refs, send_sems, recv_sems):
            cp.start()
        token[...] = jnp.zeros_like(token)

    n = 3 * nw
    out = pl.pallas_call(
        body, name=name,
        out_shape=(pltpu.SemaphoreType.DMA((n,)), pltpu.SemaphoreType.DMA((n,)),
                   *[pltpu.HBM(s.shape, s.dtype) for s in ss], *[pltpu.HBM(l.shape, l.dtype) for l in lands],
                   jax.ShapeDtypeStruct((8, LANES), F32)),
        in_specs=[HBM_SPEC] * (2 * nw), out_specs=(SEM_SPEC, SEM_SPEC, *[HBM_SPEC] * (2 * nw), VMEM_SPEC),
        input_output_aliases={k: 2 + k for k in range(2 * nw)},
        compiler_params=pltpu.CompilerParams(has_side_effects=DATAFLOW),
    )(*[pltpu.with_memory_space_constraint(s, pltpu.HBM) for s in ss], *lands)
    return out[0], out[1], list(out[2:2 + nw]), list(out[2 + nw:2 + 2 * nw]), out[-1]


def _chips_send_wait(send_sems, recv_sems, ss, lands, after, name):
    nw = len(ss)

    def body(*refs):
        ins, land_refs = refs[:nw], refs[nw:2 * nw]
        s_sems, r_sems = refs[2 * nw], refs[2 * nw + 1]
        for cp in _chip_copies(ins, land_refs, s_sems, r_sems):
            cp.wait_send()
            cp.wait_recv()

    out = pl.pallas_call(
        body, name=name,
        out_shape=(*[pltpu.HBM(s.shape, s.dtype) for s in ss], *[pltpu.HBM(l.shape, l.dtype) for l in lands]),
        in_specs=[HBM_SPEC] * (2 * nw) + [SEM_SPEC, SEM_SPEC, ANY], out_specs=tuple([HBM_SPEC] * (2 * nw)),
        input_output_aliases={k: k for k in range(2 * nw)},
        compiler_params=pltpu.CompilerParams(has_side_effects=DATAFLOW),
    )(*ss, *lands, send_sems, recv_sems, after)
    return list(out[nw:])


def _sum_small(pack):
    rows = pack.shape[0]

    def body(pack_ref, sum_ref, all_ref, send_sems, recv_sems):
        x, y, c, _ = _place()
        me = 4 * x + 2 * y + c
        all_ref[me] = pack_ref[...]
        cps = []
        for k in range(1, N_DEV):
            to = (1 - x if k & 4 else x, 1 - y if k & 2 else y, 1 - c if k & 1 else c)
            cp = pltpu.make_async_remote_copy(
                src_ref=pack_ref, dst_ref=all_ref.at[me], send_sem=send_sems.at[k - 1],
                recv_sem=recv_sems.at[k - 1], device_id=to, device_id_type=MESH)
            cp.start()
            cps.append(cp)
        for cp in cps:
            cp.wait()
        total = all_ref[0]
        for d in range(1, N_DEV):
            total = total + all_ref[d]
        sum_ref[...] = total

    return pl.pallas_call(
        body, name="sum_small_grads", in_specs=[VMEM_SPEC], out_specs=VMEM_SPEC,
        out_shape=jax.ShapeDtypeStruct(pack.shape, F32),
        scratch_shapes=[pltpu.VMEM((N_DEV, rows, LANES), F32), pltpu.SemaphoreType.DMA((N_DEV - 1,)),
                        pltpu.SemaphoreType.DMA((N_DEV - 1,))],
    )(pack)


def _share_with_sibling(fs, name):
    nw = len(fs)

    def body(*refs):
        ins, outs = refs[:nw], refs[nw:2 * nw]
        send_sems, recv_sems = refs[2 * nw:]
        x, y, c, _ = _place()
        cps = []
        for i in range(nw):
            cp = pltpu.make_async_remote_copy(src_ref=ins[i].at[c], dst_ref=outs[i].at[c], send_sem=send_sems.at[i],
                                              recv_sem=recv_sems.at[i], device_id=(x, y, 1 - c), device_id_type=MESH)
            cp.start()
            cps.append(cp)
        for cp in cps:
            cp.wait()

    return pl.pallas_call(
        body, name=name, in_specs=[ANY] * nw, out_specs=[ANY] * nw,
        out_shape=[jax.ShapeDtypeStruct(f.shape, f.dtype) for f in fs],
        input_output_aliases={i: i for i in range(nw)},
        scratch_shapes=[pltpu.SemaphoreType.DMA((nw,)), pltpu.SemaphoreType.DMA((nw,))],
    )(*fs)


def _sum_sibling(g, land, c_arr, name):
    _, _, P, Q = g.shape
    tp = _pick(P, (256, 128, 64))

    def body(c_ref, g_ref, l_ref, s_ref):
        s_ref[...] = (g_ref[...].astype(F32) + l_ref[...].astype(F32)).astype(BF16)

    return pl.pallas_call(
        body, name=name,
        grid_spec=pltpu.PrefetchScalarGridSpec(
            num_scalar_prefetch=1, grid=(N_CHIPS, P // tp),
            in_specs=[pl.BlockSpec((None, None, tp, Q), lambda t, r, cr: (t, cr[0], r, 0)),
                      pl.BlockSpec((None, tp, Q), lambda t, r, cr: (t, r, 0))],
            out_specs=pl.BlockSpec((None, tp, Q), lambda t, r, cr: (t, r, 0))),
        out_shape=jax.ShapeDtypeStruct((N_CHIPS, P, Q), BF16),
        compiler_params=_params(("parallel", "parallel")),
    )(c_arr, g, land)


def _sum_chips(g, land, recv, sc_arr, name):
    _, _, P, Q = g.shape
    tp = _pick(P, (256, 128, 64))

    def body(sc_ref, g_ref, l_ref, r_ref, f_ref):
        acc = g_ref[...].astype(F32) + l_ref[...].astype(F32)
        for j in range(N_CHIPS - 1):
            acc = acc + r_ref[j].astype(F32)
        f_ref[...] = acc

    return pl.pallas_call(
        body, name=name,
        grid_spec=pltpu.PrefetchScalarGridSpec(
            num_scalar_prefetch=1, grid=(P // tp,),
            in_specs=[pl.BlockSpec((None, None, tp, Q), lambda r, sc: (sc[0], sc[1], r, 0)),
                      pl.BlockSpec((None, tp, Q), lambda r, sc: (sc[0], r, 0)),
                      pl.BlockSpec((N_CHIPS - 1, tp, Q), lambda r, sc: (0, r, 0))],
            out_specs=pl.BlockSpec((None, tp, Q), lambda r, sc: (sc[1], r, 0))),
        out_shape=jax.ShapeDtypeStruct((2, P, Q), F32),
        compiler_params=_params(("parallel",)),
    )(sc_arr, g, land, recv)


class _Reduction:
    def __init__(self, names, pieces, lands, flight):
        self.names, self.pieces, self.lands = names, pieces, lands
        self.send_sems, self.recv_sems, self.sums, self.zones, self.token = flight


def _reduce_start(grads, names, c_arr):
    pieces = [g.reshape(N_CHIPS, 2, -1, g.shape[-1]) for g in grads]
    tag = names[0] if len(names) == 1 else "branches"
    lands = _exchange_sibling_halves(pieces, "grads_to_sibling_" + tag)
    sums = [_sum_sibling(g, l, c_arr, "sum_sibling_" + nm) for g, l, nm in zip(pieces, lands, names)]
    return _Reduction(names, pieces, lands, _chips_send_start(sums, "grads_to_chips_start_" + tag))


def _reduce_finish(red, after, sc_arr):
    tag = red.names[0] if len(red.names) == 1 else "branches"
    recvs = _chips_send_wait(red.send_sems, red.recv_sems, red.sums, red.zones, after, "grads_to_chips_wait_" + tag)
    halves = [_sum_chips(g, l, r, sc_arr, "sum_chips_" + nm) for g, l, r, nm in zip(red.pieces, red.lands, recvs, red.names)]
    return _share_with_sibling(halves, "grads_share_sibling_" + tag)


def _t5_bucket(dist):
    max_exact = REL_BUCKETS // 2
    d = jnp.maximum(dist, 0)
    df = jnp.maximum(d, 1).astype(F32)
    large = max_exact + (jnp.log(df / max_exact) / math.log(REL_MAX_DIST / max_exact)
                         * (REL_BUCKETS - max_exact)).astype(jnp.int32)
    large = jnp.minimum(large, REL_BUCKETS - 1)
    return jnp.where(d < max_exact, d, large)


def _bucket_table():
    qi = jnp.arange(WINDOW)[:, None]
    si = jnp.arange(2 * WINDOW)[None, :]
    return _t5_bucket(qi + WINDOW - si)


TILE_WORDS = 8 * LANES


def _tile_rows(shape):
    return -(-math.prod(shape) // TILE_WORDS) * 8


def _rows_of(a):
    flat = a.reshape(-1).astype(F32)
    n = _tile_rows(a.shape) * LANES
    return jnp.pad(flat, (0, n - flat.shape[0])).reshape(-1, LANES)


def _pack_rows(parts):
    return jnp.concatenate([_rows_of(p) for p in parts], axis=0)


def _unpack_rows(packed, shapes):
    out, at = [], 0
    for shp in shapes:
        n, nr = math.prod(shp), _tile_rows(shp)
        out.append(packed[at:at + nr].reshape(-1)[:n].reshape(shp))
        at += nr
    return out


def kernel(x, norm_pre, w_in, rel_bias, attn_sinks, lb_logits, hgrn_norm, w_branch_attn, w_branch_hgrn, w_out, norm_post, loss_target, m_norm_pre, m_w_in, m_rel_bias, m_attn_sinks, m_lb_logits, m_hgrn_norm, m_w_branch_attn, m_w_branch_hgrn, m_w_out, m_norm_post, v_norm_pre, v_w_in, v_rel_bias, v_attn_sinks, v_lb_logits, v_hgrn_norm, v_w_branch_attn, v_w_branch_hgrn, v_w_out, v_norm_post):
    B_loc, S, D = x.shape
    T = B_loc * S
    x2 = x.reshape(T, D)
    tgt2 = loss_target.reshape(T, D)
    my_x, my_y, my_c = lax.axis_index("x"), lax.axis_index("y"), lax.axis_index("c")

    c_arr = jnp.reshape(my_c, (1,)).astype(jnp.int32)
    s_arr = jnp.reshape(2 * my_x + my_y, (1,)).astype(jnp.int32)
    sc_arr = jnp.concatenate([s_arr, c_arr])
    shard_ws = [w_in[0], w_branch_attn[0], w_branch_hgrn[0], w_out[0]]
    shard_axes = (1, 1, 1, 0)
    names = ["w_in", "w_branch_attn", "w_branch_hgrn", "w_out"]
    placed = [_cast_into_full(w, ax, s_arr, "cast_" + nm) for w, ax, nm in zip(shard_ws, shard_axes, names)]
    peer_slabs = [jnp.reshape(t, (1,)).astype(jnp.int32)
                  for t in (2 * (1 - my_x) + my_y, 2 * my_x + 1 - my_y, 2 * (1 - my_x) + 1 - my_y)]

    buckets = _bucket_table()
    onehot = (buckets.reshape(-1)[:, None] == jnp.arange(REL_BUCKETS)[None, :]).astype(F32)
    bias_tab = jnp.dot(onehot, rel_bias.astype(F32), precision=lax.Precision.HIGHEST).T.reshape(ATTN_HEADS, WINDOW, 2 * WINDOW)
    sinks_b = jnp.broadcast_to(attn_sinks[0].astype(F32)[:, None, None], (ATTN_HEADS, 8, LANES))
    lb_fn = lambda l: jnp.cumsum(jax.nn.softmax(l.astype(F32), axis=0), axis=0)[:1]
    lb, lb_vjp = jax.vjp(lb_fn, lb_logits)
    gain_h = hgrn_norm[0].reshape(1, HGRN_WIDTH)

    h, rstd = _rmsnorm_fwd(x2, norm_pre)
    gather_in = _Gather(placed[:1], [shard_ws[0].shape], shard_axes[:1], "w_in")
    proj = _matmul_slab(h, gather_in.fulls[0], s_arr, None, gather_in.token, "in_proj_own")
    for j in range(N_CHIPS - 1):
        gather_in.wait([j], proj)
        gather_in.forward([j])
        after = None
        if j == N_CHIPS - 2:
            gather_rest = _Gather(placed[1:], [w.shape for w in shard_ws[1:]], shard_axes[1:], "rest", gather_in.fulls[0])
            after = gather_rest.token
        proj = _matmul_slab(h, gather_in.fulls[0], peer_slabs[j], proj, after, "in_proj_peer%d" % j)
    win_f = gather_in.fulls[0]
    ya = _attn_fwd(proj, bias_tab, sinks_b, B_loc, S)
    yh, states = _hgrn_fwd(proj, lb, gain_h, B_loc, S)
    gather_rest.wait([0, 1, 2], yh)
    gather_rest.forward([0, 1, 2])
    wba_f, wbh_f, wout_f = gather_rest.fulls
    ua = _matmul(ya, wba_f, "nn", BF16, "branch_attn_proj")
    uh = _matmul(yh, wbh_f, "nn", BF16, "branch_hgrn_proj")
    merged = _merge_fwd(proj, ua, uh)
    yv = _matmul(merged, wout_f, "nn", F32, "out_proj")
    dy, dout, loss_p, gnpost_p = _post_loss(yv, x2, tgt2, norm_post)

    dmerged = _matmul(dy, wout_f, "nt", F32, "out_proj_dgrad")
    g_wout = _matmul(merged, dy, "tn", BF16, "out_proj_wgrad")
    d_ua, d_uh, dproj = _merge_bwd(proj, ua, uh, dmerged)
    g_wba = _matmul(ya, d_ua, "tn", BF16, "branch_attn_wgrad", slabs=N_CHIPS)
    g_wbh = _matmul(yh, d_uh, "tn", BF16, "branch_hgrn_wgrad", slabs=N_CHIPS)
    late = _reduce_start([g_wba, g_wbh, g_wout], names[1:], c_arr)
    d_ya = _matmul(d_ua, wba_f, "nt", BF16, "branch_attn_dgrad", after=late.token)
    d_yh = _matmul(d_uh, wbh_f, "nt", BF16, "branch_hgrn_dgrad", after=late.token)
    dproj, dbias_p, dsinks_p = _attn_bwd(proj, bias_tab, sinks_b, d_ya, dproj, B_loc, S)
    dproj, dlb_p, dgain_p = _hgrn_bwd(proj, states, lb, gain_h, d_yh, dproj, B_loc, S)
    g_win = _matmul(h, dproj, "tn", BF16, "in_proj_wgrad", slabs=N_CHIPS)
    last = _reduce_start([g_win], names[:1], c_arr)
    dh = _matmul(dproj, win_f, "nt", F32, "in_proj_dgrad", after=last.token)
    grad_x2, gnpre_p = _rmsnorm_bwd(dh, x2, rstd, norm_pre, dout)
    shared = _reduce_finish(last, grad_x2, sc_arr) + _reduce_finish(late, grad_x2, sc_arr)

    grelb_p = jnp.dot(dbias_p.reshape(ATTN_HEADS, -1), onehot, precision=lax.Precision.HIGHEST).T
    gsinks_p = dsinks_p[:, 0, 0]
    dlb_sum = jnp.sum(dlb_p, axis=0).reshape(1, HGRN_WIDTH)
    ghn_p = jnp.sum(dgain_p, axis=0).reshape(HGRN_HEADS, HGRN_DIM)
    small_parts = [gnpre_p, gnpost_p, grelb_p, gsinks_p, dlb_sum, ghn_p, loss_p]
    small_shapes = [p.shape for p in small_parts]
    pack_sum = _sum_small(_pack_rows(small_parts))
    big_w = [w_in, w_branch_attn, w_branch_hgrn, w_out]
    big_m = [m_w_in, m_w_branch_attn, m_w_branch_hgrn, m_w_out]
    big_v = [v_w_in, v_w_branch_attn, v_w_branch_hgrn, v_w_out]
    big = {}
    for nm, gs, w, m, v in zip(names, shared, big_w, big_m, big_v):
        shp = w.shape
        g2 = gs.reshape(shp[1], shp[2])
        d, nm_, nv_ = _adamw(w[0], g2, m[0], v[0], "adamw_" + nm)
        big[nm] = tuple(a.reshape(shp) for a in (g2, d, nm_, nv_))

    gnpre, gnpost, grelb, gsinks, dlb, ghn, loss = _unpack_rows(pack_sum, small_shapes)
    (g_lb_logits,) = lb_vjp(dlb)
    small_names = ["norm_pre", "rel_bias", "attn_sinks", "lb_logits", "hgrn_norm", "norm_post"]
    small_w = [norm_pre, rel_bias, attn_sinks, lb_logits, hgrn_norm, norm_post]
    small_m = [m_norm_pre, m_rel_bias, m_attn_sinks, m_lb_logits, m_hgrn_norm, m_norm_post]
    small_v = [v_norm_pre, v_rel_bias, v_attn_sinks, v_lb_logits, v_hgrn_norm, v_norm_post]
    small_g = [gnpre.reshape(norm_pre.shape), grelb.reshape(rel_bias.shape), gsinks.reshape(attn_sinks.shape),
               g_lb_logits.reshape(lb_logits.shape), ghn.reshape(hgrn_norm.shape), gnpost.reshape(norm_post.shape)]
    shapes = [w.shape for w in small_w]
    d_s, nm_s, nv_s = _adamw(_pack_rows(small_w), _pack_rows(small_g), _pack_rows(small_m), _pack_rows(small_v),
                             "adamw_small")
    small = {}
    for nm, g, d, m_, v_ in zip(small_names, small_g, _unpack_rows(d_s, shapes), _unpack_rows(nm_s, shapes),
                                _unpack_rows(nv_s, shapes)):
        small[nm] = (g, d, m_, v_)

    allw = {**big, **small}
    order = ["norm_pre", "w_in", "rel_bias", "attn_sinks", "lb_logits", "hgrn_norm", "w_branch_attn", "w_branch_hgrn",
             "w_out", "norm_post"]
    outs = [loss.reshape(()), grad_x2.reshape(B_loc, S, D)]
    for k in range(4):
        outs += [allw[nm][k] for nm in order]
    return tuple(outs)
```

```python
import functools
import math

import jax
import jax.numpy as jnp
from jax import lax
from jax.experimental import pallas as pl
from jax.experimental.pallas import tpu as pltpu

F32 = jnp.float32
BF16 = jnp.bfloat16
MESH = pl.DeviceIdType.MESH

ATTN_HEADS = 16
ATTN_KV_HEADS = 4
HEAD_DIM = 64
GROUP = ATTN_HEADS // ATTN_KV_HEADS
WINDOW = 128
ATTN_WIDTH = ATTN_HEADS * HEAD_DIM
KV_WIDTH = ATTN_KV_HEADS * HEAD_DIM
HGRN_HEADS = 8
HGRN_DIM = 128
HGRN_WIDTH = HGRN_HEADS * HGRN_DIM
CHUNK = 64
SUB = 16
NSUB = CHUNK // SUB
REL_BUCKETS = 32
REL_MAX_DIST = 128
NORM_EPS = 1e-6
ADAM_LR = 0.001
ADAM_B1 = 0.9
ADAM_B2 = 0.999
ADAM_EPS = 1e-08
ADAM_WD = 0.01
ADAM_STEP = 10
LANES = 128
N_CHIPS = 4
N_DEV = 8
VMEM_LIMIT = 48 * 1024 * 1024
MATMUL_OPERAND_BYTES = 34 * 1024 * 1024
MATMUL_VMEM_BYTES = 44 * 1024 * 1024

OFF_AQ = 0
OFF_AK = OFF_AQ + ATTN_WIDTH
OFF_AV = OFF_AK + KV_WIDTH
OFF_AG = OFF_AV + KV_WIDTH
OFF_HQ = OFF_AG + ATTN_WIDTH
OFF_HF = OFF_HQ + HGRN_WIDTH
OFF_HI = OFF_HF + HGRN_WIDTH
OFF_HG = OFF_HI + HGRN_WIDTH
OFF_GA = OFF_HG + HGRN_WIDTH

NT_DIMS = (((1,), (1,)), ((), ()))
TN_DIMS = (((0,), (0,)), ((), ()))
NN_DIMS = (((1,), (0,)), ((), ()))


def _pick(n, cands):
    for c in cands:
        if n % c == 0:
            return c
    raise ValueError(f"no tile for {n} in {cands}")


def _params(sem):
    return pltpu.CompilerParams(dimension_semantics=sem, vmem_limit_bytes=VMEM_LIMIT)


def _bdot(a, b, dims=NN_DIMS):
    return lax.dot_general(a.astype(BF16), b.astype(BF16), dims, preferred_element_type=F32)


def _matmul(a, b, mode, out_dtype, name, slabs=1, after=None):
    if mode == "nn":
        (M, K), (K2, N) = a.shape, b.shape
    elif mode == "nt":
        (M, K), (N, K2) = a.shape, b.shape
    else:
        (K, M), (K2, N) = a.shape, b.shape
    assert K == K2
    nslab = N // slabs
    tm = _pick(M, (1024, 512, 256, 128))
    out_bytes = jnp.dtype(out_dtype).itemsize
    choices = []
    for tn in (2688, 1024, 896, 512, 384, 256, 128):
        for tk in (4096, 3584, 2048, 1792, 1536, 1024, 512, 256, 128):
            acc = 0 if tk == K else 4 * tm * tn
            if (nslab % tn == 0 and K % tk == 0 and 4 * tk * (tm + tn) <= MATMUL_OPERAND_BYTES
                    and 4 * tk * (tm + tn) + 2 * out_bytes * tm * tn + acc <= MATMUL_VMEM_BYTES):
                choices.append((K // tk > 1, -tn, tn, tk))
                break
    _, _, tn, tk = min(choices)
    nk = K // tk
    per = nslab // tn
    dims = {"nn": NN_DIMS, "nt": NT_DIMS, "tn": TN_DIMS}[mode]

    n_in = 2 if after is None else 3

    def body(*refs):
        a_ref, b_ref, o_ref, acc = refs[0], refs[1], refs[n_in], refs[n_in + 1:]
        part = lax.dot_general(a_ref[...], b_ref[...], dims, preferred_element_type=F32)
        if nk == 1:
            o_ref[...] = part.astype(o_ref.dtype)
            return
        acc_ref, = acc
        k = pl.program_id(2)

        @pl.when(k == 0)
        def _():
            acc_ref[...] = part

        @pl.when((k > 0) & (k < nk - 1))
        def _():
            acc_ref[...] += part

        @pl.when(k == nk - 1)
        def _():
            o_ref[...] = (acc_ref[...] + part).astype(o_ref.dtype)

    if mode == "tn":
        a_spec = pl.BlockSpec((tk, tm), lambda i, j, k: (k, i))
    else:
        a_spec = pl.BlockSpec((tm, tk), lambda i, j, k: (i, k))
    if mode == "nt":
        b_spec = pl.BlockSpec((tn, tk), lambda i, j, k: (j, k))
    else:
        b_spec = pl.BlockSpec((tk, tn), lambda i, j, k: (k, j))
    if slabs == 1:
        o_shape = jax.ShapeDtypeStruct((M, N), out_dtype)
        o_spec = pl.BlockSpec((tm, tn), lambda i, j, k: (i, j))
    else:
        o_shape = jax.ShapeDtypeStruct((slabs, M, nslab), out_dtype)
        o_spec = pl.BlockSpec((None, tm, tn), lambda i, j, k: (j // per, i, j % per))
    return pl.pallas_call(
        body, name=name, grid=(M // tm, N // tn, nk), in_specs=[a_spec, b_spec] + ([] if after is None else [ANY]),
        out_specs=o_spec, out_shape=o_shape,
        scratch_shapes=[pltpu.VMEM((tm, tn), F32)] if nk > 1 else [],
        compiler_params=_params(("parallel", "parallel", "arbitrary")),
    )(*((a, b) if after is None else (a, b, after)))


def _rmsnorm_fwd(x2, gain):
    T, D = x2.shape
    tr = _pick(T, (256, 128))

    def body(x_ref, g_ref, h_ref, r_ref):
        xv = x_ref[...]
        r = lax.rsqrt(jnp.mean(xv * xv, axis=-1, keepdims=True) + NORM_EPS)
        h_ref[...] = (xv * r * g_ref[...]).astype(BF16)
        r_ref[...] = r

    return pl.pallas_call(
        body, name="rmsnorm_pre_fwd", grid=(T // tr,),
        in_specs=[pl.BlockSpec((tr, D), lambda i: (i, 0)), pl.BlockSpec((1, D), lambda i: (0, 0))],
        out_specs=[pl.BlockSpec((tr, D), lambda i: (i, 0)), pl.BlockSpec((tr, 1), lambda i: (i, 0))],
        out_shape=[jax.ShapeDtypeStruct((T, D), BF16), jax.ShapeDtypeStruct((T, 1), F32)],
        compiler_params=_params(("parallel",)),
    )(x2, gain)


def _rmsnorm_bwd(dh, x2, rstd, gain, dout):
    T, D = x2.shape
    tr = _pick(T, (256, 128))

    def body(dh_ref, x_ref, r_ref, g_ref, do_ref, gx_ref, gg_ref):
        @pl.when(pl.program_id(0) == 0)
        def _():
            gg_ref[...] = jnp.zeros_like(gg_ref)

        n = x_ref[...] * r_ref[...]
        dhv = dh_ref[...]
        dn = dhv * g_ref[...]
        gx_ref[...] = do_ref[...] + r_ref[...] * (dn - n * jnp.mean(dn * n, axis=-1, keepdims=True))
        gg_ref[...] += jnp.sum(dhv * n, axis=0, keepdims=True)

    row = pl.BlockSpec((tr, D), lambda i: (i, 0))
    vec = pl.BlockSpec((1, D), lambda i: (0, 0))
    return pl.pallas_call(
        body, name="rmsnorm_pre_bwd", grid=(T // tr,),
        in_specs=[row, row, pl.BlockSpec((tr, 1), lambda i: (i, 0)), vec, row],
        out_specs=[row, vec],
        out_shape=[jax.ShapeDtypeStruct((T, D), F32), jax.ShapeDtypeStruct((1, D), F32)],
        compiler_params=_params(("arbitrary",)),
    )(dh, x2, rstd, gain, dout)


def _post_loss(yv, x2, tgt2, gain):
    T, D = x2.shape
    tr = _pick(T, (256, 128))

    def body(y_ref, x_ref, t_ref, g_ref, dy_ref, do_ref, loss_ref, gg_ref):
        @pl.when(pl.program_id(0) == 0)
        def _():
            gg_ref[...] = jnp.zeros_like(gg_ref)
            loss_ref[...] = jnp.zeros_like(loss_ref)

        yv_ = y_ref[...]
        r = lax.rsqrt(jnp.mean(yv_ * yv_, axis=-1, keepdims=True) + NORM_EPS)
        n = yv_ * r
        e = (x_ref[...] + n * g_ref[...]) - t_ref[...]
        loss_ref[...] += 0.5 * jnp.sum(jnp.mean(e * e, axis=-1, keepdims=True), axis=0, keepdims=True)
        dz = e / D
        do_ref[...] = dz
        gg_ref[...] += jnp.sum(dz * n, axis=0, keepdims=True)
        dn = dz * g_ref[...]
        dy_ref[...] = (r * (dn - n * jnp.mean(dn * n, axis=-1, keepdims=True))).astype(BF16)

    row = pl.BlockSpec((tr, D), lambda i: (i, 0))
    vec = pl.BlockSpec((1, D), lambda i: (0, 0))
    return pl.pallas_call(
        body, name="post_norm_loss", grid=(T // tr,),
        in_specs=[row, row, row, vec],
        out_specs=[row, row, pl.BlockSpec((1, 1), lambda i: (0, 0)), vec],
        out_shape=[jax.ShapeDtypeStruct((T, D), BF16), jax.ShapeDtypeStruct((T, D), F32),
                   jax.ShapeDtypeStruct((1, 1), F32), jax.ShapeDtypeStruct((1, D), F32)],
        compiler_params=_params(("arbitrary",)),
    )(yv, x2, tgt2, gain)


def _merge_fwd(proj, ua, uh):
    T, D = ua.shape
    tr = _pick(T, (256, 128))
    bw = _pick(D, (512, 256))
    assert OFF_GA % bw == 0
    oa, oh = OFF_GA // bw, (OFF_GA + D) // bw

    def body(ga_ref, gh_ref, ua_ref, uh_ref, m_ref):
        m_ref[...] = (jax.nn.sigmoid(ga_ref[...]) * ua_ref[...].astype(F32)
                      + jax.nn.sigmoid(gh_ref[...]) * uh_ref[...].astype(F32)).astype(BF16)

    blk = pl.BlockSpec((tr, bw), lambda i, j: (i, j))
    return pl.pallas_call(
        body, name="merge_fwd", grid=(T // tr, D // bw),
        in_specs=[pl.BlockSpec((tr, bw), lambda i, j: (i, oa + j)), pl.BlockSpec((tr, bw), lambda i, j: (i, oh + j)), blk, blk],
        out_specs=blk, out_shape=jax.ShapeDtypeStruct((T, D), BF16),
        compiler_params=_params(("parallel", "parallel")),
    )(proj, proj, ua, uh)


def _window(rows, cols, at):
    return pl.BlockSpec((pl.Element(rows), pl.Element(cols)), at)


def _merge_bwd(proj, ua, uh, dm):
    T, D = ua.shape
    tr = _pick(T, (128,))

    def body(ga_ref, gh_ref, ua_ref, uh_ref, dm_ref, dua_ref, duh_ref, dproj_ref):
        sa = jax.nn.sigmoid(ga_ref[...])
        sh = jax.nn.sigmoid(gh_ref[...])
        d = dm_ref[...]
        dua_ref[...] = (d * sa).astype(BF16)
        duh_ref[...] = (d * sh).astype(BF16)
        dproj_ref[:, :D] = (d * ua_ref[...].astype(F32) * sa * (1.0 - sa)).astype(BF16)
        dproj_ref[:, D:] = (d * uh_ref[...].astype(F32) * sh * (1.0 - sh)).astype(BF16)

    blk = pl.BlockSpec((tr, D), lambda i: (i, 0))
    o = jax.ShapeDtypeStruct((T, D), BF16)
    return pl.pallas_call(
        body, name="merge_bwd", grid=(T // tr,),
        in_specs=[_window(tr, D, lambda i: (i * tr, OFF_GA)), _window(tr, D, lambda i: (i * tr, OFF_GA + D)), blk, blk, blk],
        out_specs=[blk, blk, _window(tr, 2 * D, lambda i: (i * tr, OFF_GA))],
        out_shape=[o, o, jax.ShapeDtypeStruct((T, proj.shape[1]), BF16)],
        compiler_params=_params(("parallel",)),
    )(proj, proj, ua, uh, dm)


KV_PAIR = 2
PAIR_HEADS = KV_PAIR * GROUP


def _attn_mask(n):
    qi = lax.broadcasted_iota(jnp.int32, (WINDOW, 2 * WINDOW), 0)
    si = lax.broadcasted_iota(jnp.int32, (WINDOW, 2 * WINDOW), 1)
    dist = qi + WINDOW - si
    return (dist >= 0) & (dist < WINDOW) & ((si >= WINDOW) | (n > 0))


def _attn_group_fn(mask):
    mask = jnp.concatenate([mask] * GROUP, axis=0)

    def f(q, k, v, ag, bias, sink):
        s = lax.dot_general(q.astype(BF16), k.astype(BF16), NT_DIMS, preferred_element_type=F32) * (HEAD_DIM ** -0.5)
        s = jnp.where(mask, s + bias, -1e30)
        m = lax.stop_gradient(jnp.maximum(jnp.max(s, axis=-1, keepdims=True), sink))
        p = jnp.exp(s - m)
        den = jnp.sum(p, axis=-1, keepdims=True) + jnp.exp(sink - m)
        o = jnp.dot(p.astype(BF16), v.astype(BF16), preferred_element_type=F32)
        return o * (jax.nn.silu(ag) / den)

    return f


def _attn_specs(B_loc, nb, order):
    qw = PAIR_HEADS * HEAD_DIM
    kw = KV_PAIR * HEAD_DIM

    def rows(g):
        b, p, n = order(*g)
        return b * nb + n

    def prev(g):
        b, p, n = order(*g)
        return b * nb + jnp.maximum(n - 1, 0)

    def pp(g):
        return order(*g)[1]

    q = pl.BlockSpec((WINDOW, qw), lambda *g: (rows(g), OFF_AQ // qw + pp(g)))
    kc = pl.BlockSpec((WINDOW, kw), lambda *g: (rows(g), OFF_AK // kw + pp(g)))
    kp = pl.BlockSpec((WINDOW, kw), lambda *g: (prev(g), OFF_AK // kw + pp(g)))
    vc = pl.BlockSpec((WINDOW, kw), lambda *g: (rows(g), OFF_AV // kw + pp(g)))
    vp = pl.BlockSpec((WINDOW, kw), lambda *g: (prev(g), OFF_AV // kw + pp(g)))
    ag = pl.BlockSpec((WINDOW, qw), lambda *g: (rows(g), OFF_AG // qw + pp(g)))
    bias = pl.BlockSpec((PAIR_HEADS, WINDOW, 2 * WINDOW), lambda *g: (pp(g), 0, 0))
    sink = pl.BlockSpec((PAIR_HEADS, 8, LANES), lambda *g: (pp(g), 0, 0))
    return [q, kc, kp, vc, vp, ag, bias, sink], rows, pp


def _attn_operands(q_ref, kc_ref, kp_ref, vc_ref, vp_ref, ag_ref, bias_ref, sink_ref, j):
    lo, hi = j * HEAD_DIM, (j + 1) * HEAD_DIM
    k = jnp.concatenate([kp_ref[:, lo:hi], kc_ref[:, lo:hi]], axis=0)
    v = jnp.concatenate([vp_ref[:, lo:hi], vc_ref[:, lo:hi]], axis=0)
    heads = [j * GROUP + g for g in range(GROUP)]
    q = jnp.concatenate([q_ref[:, h * HEAD_DIM:(h + 1) * HEAD_DIM] for h in heads], axis=0)
    ag = jnp.concatenate([ag_ref[:, h * HEAD_DIM:(h + 1) * HEAD_DIM] for h in heads], axis=0)
    bias = jnp.concatenate([bias_ref[h] for h in heads], axis=0)
    sink = jnp.concatenate([jnp.broadcast_to(sink_ref[h, 0:1, 0:1], (WINDOW, 1)) for h in heads], axis=0)
    return q, k, v, ag, bias, sink


def _attn_fwd(proj, bias_tab, sinks_b, B_loc, S):
    T = B_loc * S
    nb = S // WINDOW
    n_pairs = ATTN_KV_HEADS // KV_PAIR
    in_specs, rows, pp = _attn_specs(B_loc, nb, lambda b, p, n: (b, p, n))

    def body(q_ref, kc_ref, kp_ref, vc_ref, vp_ref, ag_ref, bias_ref, sink_ref, ya_ref):
        f = _attn_group_fn(_attn_mask(pl.program_id(2)))
        for j in range(KV_PAIR):
            out = f(*_attn_operands(q_ref, kc_ref, kp_ref, vc_ref, vp_ref, ag_ref, bias_ref, sink_ref, j))
            for g in range(GROUP):
                h = j * GROUP + g
                ya_ref[:, h * HEAD_DIM:(h + 1) * HEAD_DIM] = out[g * WINDOW:(g + 1) * WINDOW].astype(BF16)

    return pl.pallas_call(
        body, name="attn_fwd", grid=(B_loc, n_pairs, nb), in_specs=in_specs,
        out_specs=pl.BlockSpec((WINDOW, PAIR_HEADS * HEAD_DIM), lambda *g: (rows(g), pp(g))),
        out_shape=jax.ShapeDtypeStruct((T, ATTN_WIDTH), BF16),
        compiler_params=_params(("parallel", "parallel", "parallel")),
    )(proj, proj, proj, proj, proj, proj, bias_tab, sinks_b)


def _attn_bwd(proj, bias_tab, sinks_b, dya, dproj, B_loc, S):
    nb = S // WINDOW
    n_pairs = ATTN_KV_HEADS // KV_PAIR
    qw, kw = PAIR_HEADS * HEAD_DIM, KV_PAIR * HEAD_DIM

    def at(off, width, back=0):
        return lambda b, i, p: ((b * nb + jnp.maximum(nb - 1 - i - back, 0)) * WINDOW,
                                pl.multiple_of(off + p * width, LANES))

    in_specs = [_window(WINDOW, qw, at(OFF_AQ, qw)), _window(WINDOW, kw, at(OFF_AK, kw)),
                _window(WINDOW, kw, at(OFF_AK, kw, 1)), _window(WINDOW, kw, at(OFF_AV, kw)),
                _window(WINDOW, kw, at(OFF_AV, kw, 1)), _window(WINDOW, qw, at(OFF_AG, qw)),
                pl.BlockSpec((PAIR_HEADS, WINDOW, 2 * WINDOW), lambda b, i, p: (p, 0, 0)),
                pl.BlockSpec((PAIR_HEADS, 8, LANES), lambda b, i, p: (p, 0, 0)),
                pl.BlockSpec((WINDOW, qw), lambda b, i, p: (b * nb + nb - 1 - i, p)), ANY]

    def body(q_ref, kc_ref, kp_ref, vc_ref, vp_ref, ag_ref, bias_ref, sink_ref, dya_ref, dproj_in,
             dproj_ref, dbias_ref, dsink_ref, dkc_ref, dvc_ref):
        b, i, p = pl.program_id(0), pl.program_id(1), pl.program_id(2)
        n = nb - 1 - i

        @pl.when((b == 0) & (i == 0) & (p == 0))
        def _():
            dbias_ref[...] = jnp.zeros_like(dbias_ref)
            dsink_ref[...] = jnp.zeros_like(dsink_ref)

        @pl.when(i == 0)
        def _():
            dkc_ref[p] = jnp.zeros((WINDOW, kw), F32)
            dvc_ref[p] = jnp.zeros((WINDOW, kw), F32)

        f = _attn_group_fn(_attn_mask(n))
        dk_carry, dv_carry = dkc_ref[p], dvc_ref[p]
        dqs, dags, dbiases, dsinks, dks, dvs = [], [], [], [], [], []
        for j in range(KV_PAIR):
            ops = _attn_operands(q_ref, kc_ref, kp_ref, vc_ref, vp_ref, ag_ref, bias_ref, sink_ref, j)
            _, vjp = jax.vjp(f, *ops)
            dout = jnp.concatenate([dya_ref[:, (j * GROUP + g) * HEAD_DIM:(j * GROUP + g + 1) * HEAD_DIM].astype(F32)
                                    for g in range(GROUP)], axis=0)
            dq, dk, dv, dag, dbias, dsink = vjp(dout)
            lo, hi = j * HEAD_DIM, (j + 1) * HEAD_DIM
            dks.append((dk[WINDOW:] + dk_carry[:, lo:hi]).astype(BF16))
            dvs.append((dv[WINDOW:] + dv_carry[:, lo:hi]).astype(BF16))
            dkc_ref[p, :, lo:hi] = dk[:WINDOW]
            dvc_ref[p, :, lo:hi] = dv[:WINDOW]
            for g in range(GROUP):
                blk = slice(g * WINDOW, (g + 1) * WINDOW)
                dqs.append(dq[blk].astype(BF16))
                dags.append(dag[blk].astype(BF16))
                dbiases.append(dbias[blk])
                dsinks.append(jnp.broadcast_to(jnp.sum(dsink[blk], axis=0, keepdims=True), (8, LANES)))

        for pair in range(n_pairs):
            @pl.when(p == pair)
            def _():
                for j in range(KV_PAIR):
                    col = (pair * KV_PAIR + j) * HEAD_DIM
                    dproj_ref[:, OFF_AK + col:OFF_AK + col + HEAD_DIM] = dks[j]
                    dproj_ref[:, OFF_AV + col:OFF_AV + col + HEAD_DIM] = dvs[j]
                for hh in range(PAIR_HEADS):
                    h = pair * PAIR_HEADS + hh
                    dproj_ref[:, OFF_AQ + h * HEAD_DIM:OFF_AQ + (h + 1) * HEAD_DIM] = dqs[hh]
                    dproj_ref[:, OFF_AG + h * HEAD_DIM:OFF_AG + (h + 1) * HEAD_DIM] = dags[hh]
                    dbias_ref[h] += dbiases[hh]
                    dsink_ref[h] += dsinks[hh]

    return pl.pallas_call(
        body, name="attn_bwd", grid=(B_loc, nb, n_pairs), in_specs=in_specs,
        out_specs=[_window(WINDOW, OFF_HQ, lambda b, i, p: ((b * nb + nb - 1 - i) * WINDOW, 0)),
                   pl.BlockSpec((ATTN_HEADS, WINDOW, 2 * WINDOW), lambda b, i, p: (0, 0, 0)),
                   pl.BlockSpec((ATTN_HEADS, 8, LANES), lambda b, i, p: (0, 0, 0))],
        out_shape=[jax.ShapeDtypeStruct(dproj.shape, dproj.dtype),
                   jax.ShapeDtypeStruct((ATTN_HEADS, WINDOW, 2 * WINDOW), F32),
                   jax.ShapeDtypeStruct((ATTN_HEADS, 8, LANES), F32)],
        input_output_aliases={9: 0},
        scratch_shapes=[pltpu.VMEM((n_pairs, WINDOW, kw), F32), pltpu.VMEM((n_pairs, WINDOW, kw), F32)],
        compiler_params=_params(("arbitrary", "arbitrary", "arbitrary")),
    )(proj, proj, proj, proj, proj, proj, bias_tab, sinks_b, dya, dproj)


class _HgrnPre:
    def __init__(self, fr, qr, lb, g_scr):
        t = lax.broadcasted_iota(jnp.int32, (CHUNK, CHUNK), 0)
        s = lax.broadcasted_iota(jnp.int32, (CHUNK, CHUNK), 1)
        self.sg = jax.nn.sigmoid(fr)
        self.f = lb + (1.0 - lb) * self.sg
        g = jnp.dot((t >= s).astype(F32), jnp.log(self.f), precision=lax.Precision.HIGHEST, preferred_element_type=F32)
        g_scr[...] = g
        self.g = g
        self.row = lax.broadcasted_iota(jnp.int32, g.shape, 0)
        self.refs = [jnp.zeros((1, g.shape[1]), F32)] + [g_scr[pl.ds(i * SUB - 1, 1), :] for i in range(1, NSUB)]
        self.gend = g_scr[pl.ds(CHUNK - 1, 1), :]
        refrow = jnp.zeros_like(g)
        for i in range(1, NSUB):
            refrow = jnp.where(self.row >= i * SUB, self.refs[i], refrow)
        self.sigq = jax.nn.sigmoid(qr)
        self.qs = qr * self.sigq
        self.k = 1.0 - self.f
        self.eg = jnp.exp(g)
        self.eqd = jnp.exp(g - refrow)
        self.ekd = [jnp.exp(jnp.where(self.row < (i + 1) * SUB, self.refs[i] - g, 0.0)) for i in range(NSUB)]
        self.ekdec = jnp.exp(self.gend - g)
        self.qg = self.qs * self.eg
        self.qd = self.qs * self.eqd
        self.kd = [self.k * e for e in self.ekd]
        self.kdec = self.k * self.ekdec
        self.egend = jnp.exp(self.gend)


def _hgrn_pair_mask():
    t = lax.broadcasted_iota(jnp.int32, (CHUNK, NSUB * CHUNK), 0)
    col = lax.broadcasted_iota(jnp.int32, (CHUNK, NSUB * CHUNK), 1)
    return ((t // SUB) == (col // CHUNK)) & ((col % CHUNK) <= t)


def _hgrn_head_out(p, lanes, state_t, v, mask):
    qg, qd = p.qg[:, lanes], p.qd[:, lanes]
    kall = jnp.concatenate([kd[:, lanes] for kd in p.kd], axis=0)
    vst = jnp.concatenate([v] * NSUB, axis=0)
    am = jnp.where(mask, _bdot(qd, kall, NT_DIMS), 0.0)
    o = _bdot(qg, state_t, NT_DIMS) + _bdot(am, vst)
    return o, (qg, qd, kall, am, vst)


def _hgrn_fwd(proj, lb, gain, B_loc, S):
    T = B_loc * S
    nc = S // CHUNK
    nh = HGRN_HEADS

    def at(off):
        return lambda b, n: ((b * nc + n) * CHUNK, off)

    vec = pl.BlockSpec((1, HGRN_WIDTH), lambda b, n: (0, 0))

    def body(q_ref, f_ref, v_ref, hg_ref, lb_ref, gain_ref, yh_ref, st_ref, state_scr, g_scr):
        @pl.when(pl.program_id(1) == 0)
        def _():
            state_scr[...] = jnp.zeros_like(state_scr)

        p = _HgrnPre(f_ref[...], q_ref[...], lb_ref[...], g_scr)
        v = v_ref[...]
        gate = gain_ref[...] * jax.nn.silu(hg_ref[...])
        mask = _hgrn_pair_mask()
        for hd in range(nh):
            lanes = slice(hd * HGRN_DIM, (hd + 1) * HGRN_DIM)
            st = state_scr[hd]
            st_ref[hd] = st
            o, _ = _hgrn_head_out(p, lanes, st, v[:, lanes], mask)
            rs = lax.rsqrt(jnp.mean(o * o, axis=-1, keepdims=True) + NORM_EPS)
            yh_ref[:, lanes] = (o * rs * gate[:, lanes]).astype(BF16)
            state_scr[hd] = st * p.egend[:, lanes] + _bdot(v[:, lanes], p.kdec[:, lanes], TN_DIMS)

    return pl.pallas_call(
        body, name="hgrn_fwd", grid=(B_loc, nc),
        in_specs=[_window(CHUNK, HGRN_WIDTH, at(OFF_HQ)), _window(CHUNK, HGRN_WIDTH, at(OFF_HF)),
                  _window(CHUNK, HGRN_WIDTH, at(OFF_HI)), _window(CHUNK, HGRN_WIDTH, at(OFF_HG)), vec, vec],
        out_specs=[pl.BlockSpec((CHUNK, HGRN_WIDTH), lambda b, n: (b * nc + n, 0)),
                   pl.BlockSpec((None, None, nh, HGRN_DIM, HGRN_DIM), lambda b, n: (b, n, 0, 0, 0))],
        out_shape=[jax.ShapeDtypeStruct((T, HGRN_WIDTH), BF16),
                   jax.ShapeDtypeStruct((B_loc, nc, nh, HGRN_DIM, HGRN_DIM), F32)],
        scratch_shapes=[pltpu.VMEM((nh, HGRN_DIM, HGRN_DIM), F32), pltpu.VMEM((CHUNK, HGRN_WIDTH), F32)],
        compiler_params=_params(("parallel", "arbitrary")),
    )(proj, proj, proj, proj, lb, gain)


def _hgrn_bwd(proj, states, lb, gain, dyh, dproj, B_loc, S):
    nc = S // CHUNK
    nh = HGRN_HEADS

    def at(off):
        return lambda b, i: ((b * nc + nc - 1 - i) * CHUNK, off)

    vec = pl.BlockSpec((1, HGRN_WIDTH), lambda b, i: (0, 0))
    in_specs = [_window(CHUNK, HGRN_WIDTH, at(OFF_HQ)), _window(CHUNK, HGRN_WIDTH, at(OFF_HF)),
                _window(CHUNK, HGRN_WIDTH, at(OFF_HI)), _window(CHUNK, HGRN_WIDTH, at(OFF_HG)), vec, vec,
                pl.BlockSpec((None, None, nh, HGRN_DIM, HGRN_DIM), lambda b, i: (b, nc - 1 - i, 0, 0, 0)),
                pl.BlockSpec((CHUNK, HGRN_WIDTH), lambda b, i: (b * nc + nc - 1 - i, 0)), ANY]
    acc_spec = pl.BlockSpec((None, 1, HGRN_WIDTH), lambda b, i: (b, 0, 0))

    def body(q_ref, f_ref, v_ref, hg_ref, lb_ref, gain_ref, st_ref, dyh_ref, dproj_in,
             dproj_ref, dlb_ref, dgain_ref, dstate_scr, g_scr, dg_scr):
        dq_ref, df_ref, dv_ref, dhg_ref = [dproj_ref.at[:, pl.ds(k * HGRN_WIDTH, HGRN_WIDTH)] for k in range(4)]

        @pl.when(pl.program_id(1) == 0)
        def _():
            dstate_scr[...] = jnp.zeros_like(dstate_scr)
            dlb_ref[...] = jnp.zeros_like(dlb_ref)
            dgain_ref[...] = jnp.zeros_like(dgain_ref)

        qr, lb, gain, hg, v = q_ref[...], lb_ref[...], gain_ref[...], hg_ref[...], v_ref[...]
        p = _HgrnPre(f_ref[...], qr, lb, g_scr)
        sgh = jax.nn.sigmoid(hg)
        sil = hg * sgh
        dy = dyh_ref[...].astype(F32)
        mask = _hgrn_pair_mask()
        dqg, dqd, dkdec, dv, dhg, dgend, dgain = [], [], [], [], [], [], []
        dkd = [[] for _ in range(NSUB)]
        heads = [slice(hd * HGRN_DIM, (hd + 1) * HGRN_DIM) for hd in range(nh)]
        sts = [st_ref[hd] for hd in range(nh)]
        dnews = [dstate_scr[hd] for hd in range(nh)]
        fwd = [_hgrn_head_out(p, lanes, st, v[:, lanes], mask) for lanes, st in zip(heads, sts)]
        for lanes, st, dnew in zip(heads, sts, dnews):
            dkdec_h = _bdot(v[:, lanes], dnew)
            dkdec.append(dkdec_h)
            dgend.append(jnp.sum(dkdec_h * p.kdec[:, lanes], axis=0, keepdims=True)
                         + jnp.sum(dnew * st, axis=0, keepdims=True) * p.egend[:, lanes])
        dos = []
        gate_grad = sgh * (1.0 + hg * (1.0 - sgh))
        for lanes, (o, _) in zip(heads, fwd):
            rs = lax.rsqrt(jnp.mean(o * o, axis=-1, keepdims=True) + NORM_EPS)
            n = o * rs
            dyn = dy[:, lanes] * n
            dgain.append(jnp.sum(dyn * sil[:, lanes], axis=0, keepdims=True))
            dhg.append(dyn * gain[:, lanes] * gate_grad[:, lanes])
            dn = dy[:, lanes] * gain[:, lanes] * sil[:, lanes]
            dos.append(rs * (dn - n * jnp.mean(dn * n, axis=-1, keepdims=True)))
        drs = []
        for hd, (lanes, st, dnew, do, (_, (qg, qd, kall, am, vst))) in enumerate(zip(heads, sts, dnews, dos, fwd)):
            dqg.append(_bdot(do, st))
            dstate_scr[hd] = _bdot(do, qg, TN_DIMS) + dnew * p.egend[:, lanes]
            drs.append(jnp.where(mask, _bdot(do, vst, NT_DIMS), 0.0))
            dvst = _bdot(am, do, TN_DIMS)
            dv.append(sum(dvst[i * CHUNK:(i + 1) * CHUNK] for i in range(NSUB)) + _bdot(p.kdec[:, lanes], dnew, NT_DIMS))
        for dr, (_, (qg, qd, kall, am, vst)) in zip(drs, fwd):
            dqd.append(_bdot(dr, kall))
            dkall = _bdot(dr, qd, TN_DIMS)
            for i in range(NSUB):
                dkd[i].append(dkall[i * CHUNK:(i + 1) * CHUNK])

        wide = lambda parts: jnp.concatenate(parts, axis=1)
        dqg, dqd, dkdec = wide(dqg), wide(dqd), wide(dkdec)
        t2 = dqd * p.qd
        dg = dqg * p.qg + t2 - dkdec * p.kdec
        dk = dkdec * p.ekdec
        dg_scr[...] = jnp.zeros_like(dg_scr)
        for i in range(NSUB):
            dkd_i = wide(dkd[i])
            tk = jnp.where(p.row < (i + 1) * SUB, dkd_i * p.kd[i], 0.0)
            dg = dg - tk
            dk = dk + dkd_i * p.ekd[i]
            if i >= 1:
                in_blk = (p.row >= i * SUB) & (p.row < (i + 1) * SUB)
                dg_scr[pl.ds(i * SUB - 1, 1), :] = (jnp.sum(tk, axis=0, keepdims=True)
                                                    - jnp.sum(jnp.where(in_blk, t2, 0.0), axis=0, keepdims=True))
        dg_scr[pl.ds(CHUNK - 1, 1), :] = wide(dgend)
        t = lax.broadcasted_iota(jnp.int32, (CHUNK, CHUNK), 0)
        s = lax.broadcasted_iota(jnp.int32, (CHUNK, CHUNK), 1)
        dlogf = jnp.dot((t <= s).astype(F32), dg + dg_scr[...], precision=lax.Precision.HIGHEST, preferred_element_type=F32)
        df = dlogf / p.f - dk
        df_ref[...] = (df * (1.0 - lb) * p.sg * (1.0 - p.sg)).astype(BF16)
        dlb_ref[...] += jnp.sum(df * (1.0 - p.sg), axis=0, keepdims=True)
        dq_ref[...] = ((dqg * p.eg + dqd * p.eqd) * p.sigq * (1.0 + qr * (1.0 - p.sigq))).astype(BF16)
        dv_ref[...] = wide(dv).astype(BF16)
        dhg_ref[...] = wide(dhg).astype(BF16)
        dgain_ref[...] += wide(dgain)

    acc = jax.ShapeDtypeStruct((B_loc, 1, HGRN_WIDTH), F32)
    return pl.pallas_call(
        body, name="hgrn_bwd", grid=(B_loc, nc), in_specs=in_specs,
        out_specs=[_window(CHUNK, 4 * HGRN_WIDTH, at(OFF_HQ)), acc_spec, acc_spec],
        out_shape=[jax.ShapeDtypeStruct(dproj.shape, dproj.dtype), acc, acc],
        input_output_aliases={8: 0},
        scratch_shapes=[pltpu.VMEM((nh, HGRN_DIM, HGRN_DIM), F32), pltpu.VMEM((CHUNK, HGRN_WIDTH), F32),
                        pltpu.VMEM((CHUNK, HGRN_WIDTH), F32)],
        compiler_params=_params(("parallel", "arbitrary")),
    )(proj, proj, proj, proj, lb, gain, states, dyh, dproj)


def _adamw(w, g, m, v, name):
    R, C = w.shape
    tr = _pick(R, (128, 64, 32, 16, 8)) if C > 1024 else _pick(R, (512, 256, 128, 64, 32, 16, 8))

    def body(w_ref, g_ref, m_ref, v_ref, d_ref, nm_ref, nv_ref, g_out_ref):
        gv = g_ref[...]
        g_out_ref[...] = gv
        nm = ADAM_B1 * m_ref[...] + (1.0 - ADAM_B1) * gv
        nv = ADAM_B2 * v_ref[...] + (1.0 - ADAM_B2) * (gv * gv)
        m_hat = nm / (1.0 - ADAM_B1 ** ADAM_STEP)
        v_hat = nv / (1.0 - ADAM_B2 ** ADAM_STEP)
        d_ref[...] = -ADAM_LR * (m_hat / (jnp.sqrt(v_hat) + ADAM_EPS) + ADAM_WD * w_ref[...])
        nm_ref[...] = nm
        nv_ref[...] = nv

    blk = pl.BlockSpec((tr, C), lambda i: (i, 0))
    o = jax.ShapeDtypeStruct((R, C), F32)
    return pl.pallas_call(
        body, name=name, grid=(R // tr,), in_specs=[blk] * 4, out_specs=[blk] * 4, out_shape=[o, o, o, o],
        compiler_params=_params(("parallel",)),
    )(w, g, m, v)


ANY = pl.BlockSpec(memory_space=pl.ANY)
VMEM_SPEC = pl.BlockSpec(memory_space=pltpu.VMEM)


def _place():
    x, y, c = lax.axis_index("x"), lax.axis_index("y"), lax.axis_index("c")
    other_chips = [(1 - x, y), (x, 1 - y), (1 - x, 1 - y)]
    return x, y, c, other_chips


def _cast_into_full(w, ax, s_arr, name):
    R, C = w.shape
    tr = _pick(R, (256, 128))
    nr = R // tr

    def body(s_ref, w_ref, o_ref):
        o_ref[...] = w_ref[...].astype(BF16)

    if ax == 1:
        shape, o_map = (R, N_CHIPS * C), lambda i, s: (i, s[0])
    else:
        shape, o_map = (N_CHIPS * R, C), lambda i, s: (s[0] * nr + i, 0)
    return pl.pallas_call(
        body, name=name,
        grid_spec=pltpu.PrefetchScalarGridSpec(
            num_scalar_prefetch=1, grid=(nr,), in_specs=[pl.BlockSpec((tr, C), lambda i, s: (i, 0))],
            out_specs=pl.BlockSpec((tr, C), o_map)),
        out_shape=jax.ShapeDtypeStruct(shape, BF16),
        compiler_params=_params(("parallel",)),
    )(s_arr, w)


class _Gather:
    def __init__(self, fulls, shard_shapes, axes, tag, after=None):
        self.shapes, self.axes, self.tag, self.nw = shard_shapes, axes, tag, len(fulls)
        nw = self.nw

        def body(*refs):
            ins, sems = refs[:nw], refs[nw + (after is not None):nw + (after is not None) + 2 * (N_CHIPS - 1)]
            for j in range(N_CHIPS - 1):
                for cp in self._peer_copies(ins, sems[2 * j], sems[2 * j + 1], j):
                    cp.start()
            refs[-1][...] = jnp.zeros_like(refs[-1])

        out = pl.pallas_call(
            body, name="gather_start_" + tag,
            out_shape=(*[pltpu.SemaphoreType.DMA((nw,))] * (2 * (N_CHIPS - 1)),
                       *[pltpu.HBM(f.shape, f.dtype) for f in fulls], jax.ShapeDtypeStruct((8, LANES), F32)),
            in_specs=[HBM_SPEC] * nw + ([] if after is None else [ANY]),
            out_specs=(*[SEM_SPEC] * (2 * (N_CHIPS - 1)), *[HBM_SPEC] * nw, VMEM_SPEC),
            input_output_aliases={k: 2 * (N_CHIPS - 1) + k for k in range(nw)},
            compiler_params=pltpu.CompilerParams(has_side_effects=DATAFLOW),
        )(*[pltpu.with_memory_space_constraint(f, pltpu.HBM) for f in fulls], *(() if after is None else (after,)))
        self.sems = [(out[2 * j], out[2 * j + 1]) for j in range(N_CHIPS - 1)]
        self.fulls = list(out[2 * (N_CHIPS - 1):2 * (N_CHIPS - 1) + nw])
        self.token = out[-1]

    def _region(self, ref, i, t, half):
        R, C = self.shapes[i]
        hr = R // 2
        if self.axes[i] == 1:
            return ref.at[pl.ds(half * hr, hr), pl.ds(pl.multiple_of(t * C, LANES), C)]
        return ref.at[pl.ds(t * R + half * hr, hr), :]

    def _peer_copies(self, refs, send_sems, recv_sems, j):
        x, y, c, chips = _place()
        s = 2 * x + y
        return [pltpu.make_async_remote_copy(
            src_ref=self._region(refs[i], i, s, c), dst_ref=self._region(refs[i], i, s, c), send_sem=send_sems.at[i],
            recv_sem=recv_sems.at[i], device_id=(*chips[j], c), device_id_type=MESH) for i in range(self.nw)]

    def wait(self, peers, after):
        nw, np_ = self.nw, len(peers)

        def body(*refs):
            ins, sems = refs[:nw], refs[nw:nw + 2 * np_]
            for k, j in enumerate(peers):
                for cp in self._peer_copies(ins, sems[2 * k], sems[2 * k + 1], j):
                    cp.wait_send()
                    cp.wait_recv()

        sem_args = [s for j in peers for s in self.sems[j]]
        out = pl.pallas_call(
            body, name="gather_wait_%s_%s" % (self.tag, "".join(map(str, peers))),
            out_shape=tuple(pltpu.HBM(f.shape, f.dtype) for f in self.fulls),
            in_specs=[HBM_SPEC] * nw + [SEM_SPEC] * (2 * np_) + [ANY], out_specs=tuple([HBM_SPEC] * nw),
            input_output_aliases={k: k for k in range(nw)},
            compiler_params=pltpu.CompilerParams(has_side_effects=DATAFLOW),
        )(*self.fulls, *sem_args, after)
        self.fulls = list(out)

    def forward(self, peers):
        nw, np_ = self.nw, len(peers)

        def body(*refs):
            ins, outs = refs[:nw], refs[nw:2 * nw]
            send_sems, recv_sems = refs[2 * nw:]
            x, y, c, chips = _place()
            cps = []
            for i in range(nw):
                for k, j in enumerate(peers):
                    t = 2 * chips[j][0] + chips[j][1]
                    cp = pltpu.make_async_remote_copy(
                        src_ref=self._region(ins[i], i, t, c), dst_ref=self._region(outs[i], i, t, c),
                        send_sem=send_sems.at[i * np_ + k], recv_sem=recv_sems.at[i * np_ + k],
                        device_id=(x, y, 1 - c), device_id_type=MESH)
                    cp.start()
                    cps.append(cp)
            for cp in cps:
                cp.wait()

        out = pl.pallas_call(
            body, name="gather_forward_%s_%s" % (self.tag, "".join(map(str, peers))),
            in_specs=[ANY] * nw, out_specs=[ANY] * nw,
            out_shape=[jax.ShapeDtypeStruct(f.shape, f.dtype) for f in self.fulls],
            input_output_aliases={i: i for i in range(nw)},
            scratch_shapes=[pltpu.SemaphoreType.DMA((nw * np_,)), pltpu.SemaphoreType.DMA((nw * np_,))],
        )(*self.fulls)
        self.fulls = list(out)


def _matmul_slab(a, wfull, slab_arr, prev, after, name):
    M, K = a.shape
    N = wfull.shape[1]
    nslab = N // N_CHIPS
    tn = _pick(nslab, (2688, 896, 512, 384, 256, 128))
    tm = _pick(M, (512, 256, 128) if tn > 1024 else (1024, 512, 256, 128))
    per = nslab // tn
    extra = [e for e in (prev, after) if e is not None]

    def body(slab_ref, a_ref, b_ref, *rest):
        rest[len(extra)][...] = jnp.dot(a_ref[...], b_ref[...], preferred_element_type=F32)

    return pl.pallas_call(
        body, name=name,
        grid_spec=pltpu.PrefetchScalarGridSpec(
            num_scalar_prefetch=1, grid=(M // tm, per),
            in_specs=[pl.BlockSpec((tm, K), lambda i, j, sl: (i, 0)),
                      pl.BlockSpec((K, tn), lambda i, j, sl: (0, sl[0] * per + j))] + [ANY] * len(extra),
            out_specs=pl.BlockSpec((tm, tn), lambda i, j, sl: (i, sl[0] * per + j))),
        out_shape=jax.ShapeDtypeStruct((M, N), F32),
        input_output_aliases={} if prev is None else {3: 0},
        compiler_params=_params(("parallel", "arbitrary")),
    )(slab_arr, a, wfull, *extra)


def _exchange_sibling_halves(gs, name):
    nw = len(gs)

    def body(*refs):
        ins, outs = refs[:nw], refs[nw:2 * nw]
        send_sems, recv_sems = refs[2 * nw:]
        x, y, c, _ = _place()
        cps = []
        for i in range(nw):
            cp = pltpu.make_async_remote_copy(src_ref=ins[i].at[:, 1 - c], dst_ref=outs[i], send_sem=send_sems.at[i],
                                              recv_sem=recv_sems.at[i], device_id=(x, y, 1 - c), device_id_type=MESH)
            cp.start()
            cps.append(cp)
        for cp in cps:
            cp.wait()

    return pl.pallas_call(
        body, name=name, in_specs=[ANY] * nw, out_specs=[ANY] * nw,
        out_shape=[jax.ShapeDtypeStruct((g.shape[0],) + g.shape[2:], g.dtype) for g in gs],
        scratch_shapes=[pltpu.SemaphoreType.DMA((nw,)), pltpu.SemaphoreType.DMA((nw,))],
    )(*gs)


HBM_SPEC = pl.BlockSpec(memory_space=pltpu.HBM)
SEM_SPEC = pl.BlockSpec(memory_space=pltpu.SEMAPHORE)
DATAFLOW = pltpu.SideEffectType.DATAFLOW_SIDE_EFFECTING


def _chip_copies(ins, lands, send_sems, recv_sems):
    x, y, c, chips = _place()
    return [pltpu.make_async_remote_copy(
        src_ref=ins[i].at[2 * chip[0] + chip[1]], dst_ref=lands[i].at[j], send_sem=send_sems.at[i * 3 + j],
        recv_sem=recv_sems.at[i * 3 + j], device_id=(*chip, c), device_id_type=MESH)
        for i in range(len(ins)) for j, chip in enumerate(chips)]


def _chips_send_start(ss, name):
    nw = len(ss)
    lands = [pltpu.with_memory_space_constraint(lax.empty((N_CHIPS - 1,) + s.shape[1:], s.dtype), pltpu.HBM) for s in ss]

    def body(*refs):
        ins, land_refs = refs[:nw], refs[nw:2 * nw]
        send_sems, recv_sems = refs[2 * nw], refs[2 * nw + 1]
        token = refs[-1]
        for cp in _chip_copies(ins, land_refs, send_sems, recv_sems):
            cp.start()
        token[...] = jnp.zeros_like(token)

    n = 3 * nw
    out = pl.pallas_call(
        body, name=name,
        out_shape=(pltpu.SemaphoreType.DMA((n,)), pltpu.SemaphoreType.DMA((n,)),
                   *[pltpu.HBM(s.shape, s.dtype) for s in ss], *[pltpu.HBM(l.shape, l.dtype) for l in lands],
                   jax.ShapeDtypeStruct((8, LANES), F32)),
        in_specs=[HBM_SPEC] * (2 * nw), out_specs=(SEM_SPEC, SEM_SPEC, *[HBM_SPEC] * (2 * nw), VMEM_SPEC),
        input_output_aliases={k: 2 + k for k in range(2 * nw)},
        compiler_params=pltpu.CompilerParams(has_side_effects=DATAFLOW),
    )(*[pltpu.with_memory_space_constraint(s, pltpu.HBM) for s in ss], *lands)
    return out[0], out[1], list(out[2:2 + nw]), list(out[2 + nw:2 + 2 * nw]), out[-1]


def _chips_send_wait(send_sems, recv_sems, ss, lands, after, name):
    nw = len(ss)

    def body(*refs):
        ins, land_refs = refs[:nw], refs[nw:2 * nw]
        s_sems, r_sems = refs[2 * nw], refs[2 * nw + 1]
        for cp in _chip_copies(ins, land_refs, s_sems, r_sems):
            cp.wait_send()
            cp.wait_recv()

    out = pl.pallas_call(
        body, name=name,
        out_shape=(*[pltpu.HBM(s.shape, s.dtype) for s in ss], *[pltpu.HBM(l.shape, l.dtype) for l in lands]),
        in_specs=[HBM_SPEC] * (2 * nw) + [SEM_SPEC, SEM_SPEC, ANY], out_specs=tuple([HBM_SPEC] * (2 * nw)),
        input_output_aliases={k: k for k in range(2 * nw)},
        compiler_params=pltpu.CompilerParams(has_side_effects=DATAFLOW),
    )(*ss, *lands, send_sems, recv_sems, after)
    return list(out[nw:])


def _sum_small(pack):
    rows = pack.shape[0]

    def body(pack_ref, sum_ref, all_ref, send_sems, recv_sems):
        x, y, c, _ = _place()
        me = 4 * x + 2 * y + c
        all_ref[me] = pack_ref[...]
        cps = []
        for k in range(1, N_DEV):
            to = (1 - x if k & 4 else x, 1 - y if k & 2 else y, 1 - c if k & 1 else c)
            cp = pltpu.make_async_remote_copy(
                src_ref=pack_ref, dst_ref=all_ref.at[me], send_sem=send_sems.at[k - 1],
                recv_sem=recv_sems.at[k - 1], device_id=to, device_id_type=MESH)
            cp.start()
            cps.append(cp)
        for cp in cps:
            cp.wait()
        total = all_ref[0]
        for d in range(1, N_DEV):
            total = total + all_ref[d]
        sum_ref[...] = total

    return pl.pallas_call(
        body, name="sum_small_grads", in_specs=[VMEM_SPEC], out_specs=VMEM_SPEC,
        out_shape=jax.ShapeDtypeStruct(pack.shape, F32),
        scratch_shapes=[pltpu.VMEM((N_DEV, rows, LANES), F32), pltpu.SemaphoreType.DMA((N_DEV - 1,)),
                        pltpu.SemaphoreType.DMA((N_DEV - 1,))],
    )(pack)


def _share_with_sibling(fs, name):
    nw = len(fs)

    def body(*refs):
        ins, outs = refs[:nw], refs[nw:2 * nw]
        send_sems, recv_sems = refs[2 * nw:]
        x, y, c, _ = _place()
        cps = []
        for i in range(nw):
            cp = pltpu.make_async_remote_copy(src_ref=ins[i].at[c], dst_ref=outs[i].at[c], send_sem=send_sems.at[i],
                                              recv_sem=recv_sems.at[i], device_id=(x, y, 1 - c), device_id_type=MESH)
            cp.start()
            cps.append(cp)
        for cp in cps:
            cp.wait()

    return pl.pallas_call(
        body, name=name, in_specs=[ANY] * nw, out_specs=[ANY] * nw,
        out_shape=[jax.ShapeDtypeStruct(f.shape, f.dtype) for f in fs],
        input_output_aliases={i: i for i in range(nw)},
        scratch_shapes=[pltpu.SemaphoreType.DMA((nw,)), pltpu.SemaphoreType.DMA((nw,))],
    )(*fs)


def _sum_sibling(g, land, c_arr, name):
    _, _, P, Q = g.shape
    tp = _pick(P, (256, 128, 64))

    def body(c_ref, g_ref, l_ref, s_ref):
        s_ref[...] = (g_ref[...].astype(F32) + l_ref[...].astype(F32)).astype(BF16)

    return pl.pallas_call(
        body, name=name,
        grid_spec=pltpu.PrefetchScalarGridSpec(
            num_scalar_prefetch=1, grid=(N_CHIPS, P // tp),
            in_specs=[pl.BlockSpec((None, None, tp, Q), lambda t, r, cr: (t, cr[0], r, 0)),
                      pl.BlockSpec((None, tp, Q), lambda t, r, cr: (t, r, 0))],
            out_specs=pl.BlockSpec((None, tp, Q), lambda t, r, cr: (t, r, 0))),
        out_shape=jax.ShapeDtypeStruct((N_CHIPS, P, Q), BF16),
        compiler_params=_params(("parallel", "parallel")),
    )(c_arr, g, land)


def _sum_chips(g, land, recv, sc_arr, name):
    _, _, P, Q = g.shape
    tp = _pick(P, (256, 128, 64))

    def body(sc_ref, g_ref, l_ref, r_ref, f_ref):
        acc = g_ref[...].astype(F32) + l_ref[...].astype(F32)
        for j in range(N_CHIPS - 1):
            acc = acc + r_ref[j].astype(F32)
        f_ref[...] = acc

    return pl.pallas_call(
        body, name=name,
        grid_spec=pltpu.PrefetchScalarGridSpec(
            num_scalar_prefetch=1, grid=(P // tp,),
            in_specs=[pl.BlockSpec((None, None, tp, Q), lambda r, sc: (sc[0], sc[1], r, 0)),
                      pl.BlockSpec((None, tp, Q), lambda r, sc: (sc[0], r, 0)),
                      pl.BlockSpec((N_CHIPS - 1, tp, Q), lambda r, sc: (0, r, 0))],
            out_specs=pl.BlockSpec((None, tp, Q), lambda r, sc: (sc[1], r, 0))),
        out_shape=jax.ShapeDtypeStruct((2, P, Q), F32),
        compiler_params=_params(("parallel",)),
    )(sc_arr, g, land, recv)


class _Reduction:
    def __init__(self, names, pieces, lands, flight):
        self.names, self.pieces, self.lands = names, pieces, lands
        self.send_sems, self.recv_sems, self.sums, self.zones, self.token = flight


def _reduce_start(grads, names, c_arr):
    pieces = [g.reshape(N_CHIPS, 2, -1, g.shape[-1]) for g in grads]
    tag = names[0] if len(names) == 1 else "branches"
    lands = _exchange_sibling_halves(pieces, "grads_to_sibling_" + tag)
    sums = [_sum_sibling(g, l, c_arr, "sum_sibling_" + nm) for g, l, nm in zip(pieces, lands, names)]
    return _Reduction(names, pieces, lands, _chips_send_start(sums, "grads_to_chips_start_" + tag))


def _reduce_finish(red, after, sc_arr):
    tag = red.names[0] if len(red.names) == 1 else "branches"
    recvs = _chips_send_wait(red.send_sems, red.recv_sems, red.sums, red.zones, after, "grads_to_chips_wait_" + tag)
    halves = [_sum_chips(g, l, r, sc_arr, "sum_chips_" + nm) for g, l, r, nm in zip(red.pieces, red.lands, recvs, red.names)]
    return _share_with_sibling(halves, "grads_share_sibling_" + tag)


def _t5_bucket(dist):
    max_exact = REL_BUCKETS // 2
    d = jnp.maximum(dist, 0)
    df = jnp.maximum(d, 1).astype(F32)
    large = max_exact + (jnp.log(df / max_exact) / math.log(REL_MAX_DIST / max_exact)
                         * (REL_BUCKETS - max_exact)).astype(jnp.int32)
    large = jnp.minimum(large, REL_BUCKETS - 1)
    return jnp.where(d < max_exact, d, large)


def _bucket_table():
    qi = jnp.arange(WINDOW)[:, None]
    si = jnp.arange(2 * WINDOW)[None, :]
    return _t5_bucket(qi + WINDOW - si)


TILE_WORDS = 8 * LANES


def _tile_rows(shape):
    return -(-math.prod(shape) // TILE_WORDS) * 8


def _rows_of(a):
    flat = a.reshape(-1).astype(F32)
    n = _tile_rows(a.shape) * LANES
    return jnp.pad(flat, (0, n - flat.shape[0])).reshape(-1, LANES)


def _pack_rows(parts):
    return jnp.concatenate([_rows_of(p) for p in parts], axis=0)


def _unpack_rows(packed, shapes):
    out, at = [], 0
    for shp in shapes:
        n, nr = math.prod(shp), _tile_rows(shp)
        out.append(packed[at:at + nr].reshape(-1)[:n].reshape(shp))
        at += nr
    return out


def kernel(x, norm_pre, w_in, rel_bias, attn_sinks, lb_logits, hgrn_norm, w_branch_attn, w_branch_hgrn, w_out, norm_post, loss_target, m_norm_pre, m_w_in, m_rel_bias, m_attn_sinks, m_lb_logits, m_hgrn_norm, m_w_branch_attn, m_w_branch_hgrn, m_w_out, m_norm_post, v_norm_pre, v_w_in, v_rel_bias, v_attn_sinks, v_lb_logits, v_hgrn_norm, v_w_branch_attn, v_w_branch_hgrn, v_w_out, v_norm_post):
    B_loc, S, D = x.shape
    T = B_loc * S
    x2 = x.reshape(T, D)
    tgt2 = loss_target.reshape(T, D)
    my_x, my_y, my_c = lax.axis_index("x"), lax.axis_index("y"), lax.axis_index("c")

    c_arr = jnp.reshape(my_c, (1,)).astype(jnp.int32)
    s_arr = jnp.reshape(2 * my_x + my_y, (1,)).astype(jnp.int32)
    sc_arr = jnp.concatenate([s_arr, c_arr])
    shard_ws = [w_in[0], w_branch_attn[0], w_branch_hgrn[0], w_out[0]]
    shard_axes = (1, 1, 1, 0)
    names = ["w_in", "w_branch_attn", "w_branch_hgrn", "w_out"]
    placed = [_cast_into_full(w, ax, s_arr, "cast_" + nm) for w, ax, nm in zip(shard_ws, shard_axes, names)]
    peer_slabs = [jnp.reshape(t, (1,)).astype(jnp.int32)
                  for t in (2 * (1 - my_x) + my_y, 2 * my_x + 1 - my_y, 2 * (1 - my_x) + 1 - my_y)]

    buckets = _bucket_table()
    onehot = (buckets.reshape(-1)[:, None] == jnp.arange(REL_BUCKETS)[None, :]).astype(F32)
    bias_tab = jnp.dot(onehot, rel_bias.astype(F32), precision=lax.Precision.HIGHEST).T.reshape(ATTN_HEADS, WINDOW, 2 * WINDOW)
    sinks_b = jnp.broadcast_to(attn_sinks[0].astype(F32)[:, None, None], (ATTN_HEADS, 8, LANES))
    lb_fn = lambda l: jnp.cumsum(jax.nn.softmax(l.astype(F32), axis=0), axis=0)[:1]
    lb, lb_vjp = jax.vjp(lb_fn, lb_logits)
    gain_h = hgrn_norm[0].reshape(1, HGRN_WIDTH)

    h, rstd = _rmsnorm_fwd(x2, norm_pre)
    gather_in = _Gather(placed[:1], [shard_ws[0].shape], shard_axes[:1], "w_in")
    proj = _matmul_slab(h, gather_in.fulls[0], s_arr, None, gather_in.token, "in_proj_own")
    for j in range(N_CHIPS - 1):
        gather_in.wait([j], proj)
        gather_in.forward([j])
        after = None
        if j == N_CHIPS - 2:
            gather_rest = _Gather(placed[1:], [w.shape for w in shard_ws[1:]], shard_axes[1:], "rest", gather_in.fulls[0])
            after = gather_rest.token
        proj = _matmul_slab(h, gather_in.fulls[0], peer_slabs[j], proj, after, "in_proj_peer%d" % j)
    win_f = gather_in.fulls[0]
    ya = _attn_fwd(proj, bias_tab, sinks_b, B_loc, S)
    yh, states = _hgrn_fwd(proj, lb, gain_h, B_loc, S)
    gather_rest.wait([0, 1, 2], yh)
    gather_rest.forward([0, 1, 2])
    wba_f, wbh_f, wout_f = gather_rest.fulls
    ua = _matmul(ya, wba_f, "nn", BF16, "branch_attn_proj")
    uh = _matmul(yh, wbh_f, "nn", BF16, "branch_hgrn_proj")
    merged = _merge_fwd(proj, ua, uh)
    yv = _matmul(merged, wout_f, "nn", F32, "out_proj")
    dy, dout, loss_p, gnpost_p = _post_loss(yv, x2, tgt2, norm_post)

    dmerged = _matmul(dy, wout_f, "nt", F32, "out_proj_dgrad")
    g_wout = _matmul(merged, dy, "tn", BF16, "out_proj_wgrad")
    d_ua, d_uh, dproj = _merge_bwd(proj, ua, uh, dmerged)
    g_wba = _matmul(ya, d_ua, "tn", BF16, "branch_attn_wgrad", slabs=N_CHIPS)
    g_wbh = _matmul(yh, d_uh, "tn", BF16, "branch_hgrn_wgrad", slabs=N_CHIPS)
    late = _reduce_start([g_wba, g_wbh, g_wout], names[1:], c_arr)
    d_ya = _matmul(d_ua, wba_f, "nt", BF16, "branch_attn_dgrad", after=late.token)
    d_yh = _matmul(d_uh, wbh_f, "nt", BF16, "branch_hgrn_dgrad", after=late.token)
    dproj, dbias_p, dsinks_p = _attn_bwd(proj, bias_tab, sinks_b, d_ya, dproj, B_loc, S)
    dproj, dlb_p, dgain_p = _hgrn_bwd(proj, states, lb, gain_h, d_yh, dproj, B_loc, S)
    g_win = _matmul(h, dproj, "tn", BF16, "in_proj_wgrad", slabs=N_CHIPS)
    last = _reduce_start([g_win], names[:1], c_arr)
    dh = _matmul(dproj, win_f, "nt", F32, "in_proj_dgrad", after=last.token)
    grad_x2, gnpre_p = _rmsnorm_bwd(dh, x2, rstd, norm_pre, dout)
    shared = _reduce_finish(last, grad_x2, sc_arr) + _reduce_finish(late, grad_x2, sc_arr)

    grelb_p = jnp.dot(dbias_p.reshape(ATTN_HEADS, -1), onehot, precision=lax.Precision.HIGHEST).T
    gsinks_p = dsinks_p[:, 0, 0]
    dlb_sum = jnp.sum(dlb_p, axis=0).reshape(1, HGRN_WIDTH)
    ghn_p = jnp.sum(dgain_p, axis=0).reshape(HGRN_HEADS, HGRN_DIM)
    small_parts = [gnpre_p, gnpost_p, grelb_p, gsinks_p, dlb_sum, ghn_p, loss_p]
    small_shapes = [p.shape for p in small_parts]
    pack_sum = _sum_small(_pack_rows(small_parts))
    big_w = [w_in, w_branch_attn, w_branch_hgrn, w_out]
    big_m = [m_w_in, m_w_branch_attn, m_w_branch_hgrn, m_w_out]
    big_v = [v_w_in, v_w_branch_attn, v_w_branch_hgrn, v_w_out]
    big = {}
    for nm, gs, w, m, v in zip(names, shared, big_w, big_m, big_v):
        shp = w.shape
        g2 = gs.reshape(shp[1], shp[2])
        d, nm_, nv_, g_out = _adamw(w[0], g2, m[0], v[0], "adamw_" + nm)
        big[nm] = tuple(a.reshape(shp) for a in (g_out, d, nm_, nv_))

    gnpre, gnpost, grelb, gsinks, dlb, ghn, loss = _unpack_rows(pack_sum, small_shapes)
    (g_lb_logits,) = lb_vjp(dlb)
    small_names = ["norm_pre", "rel_bias", "attn_sinks", "lb_logits", "hgrn_norm", "norm_post"]
    small_w = [norm_pre, rel_bias, attn_sinks, lb_logits, hgrn_norm, norm_post]
    small_m = [m_norm_pre, m_rel_bias, m_attn_sinks, m_lb_logits, m_hgrn_norm, m_norm_post]
    small_v = [v_norm_pre, v_rel_bias, v_attn_sinks, v_lb_logits, v_hgrn_norm, v_norm_post]
    small_g = [gnpre.reshape(norm_pre.shape), grelb.reshape(rel_bias.shape), gsinks.reshape(attn_sinks.shape),
               g_lb_logits.reshape(lb_logits.shape), ghn.reshape(hgrn_norm.shape), gnpost.reshape(norm_post.shape)]
    shapes = [w.shape for w in small_w]
    d_s, nm_s, nv_s, _ = _adamw(_pack_rows(small_w), _pack_rows(small_g), _pack_rows(small_m), _pack_rows(small_v),
                             "adamw_small")
    small = {}
    for nm, g, d, m_, v_ in zip(small_names, small_g, _unpack_rows(d_s, shapes), _unpack_rows(nm_s, shapes),
                                _unpack_rows(nv_s, shapes)):
        small[nm] = (g, d, m_, v_)

    allw = {**big, **small}
    order = ["norm_pre", "w_in", "rel_bias", "attn_sinks", "lb_logits", "hgrn_norm", "w_branch_attn", "w_branch_hgrn",
             "w_out", "norm_post"]
    outs = [loss.reshape(()), grad_x2.reshape(B_loc, S, D)]
    for k in range(4):
        outs += [allw[nm][k] for nm in order]
    return tuple(outs)
```

```python
import functools
import math

import jax
import jax.numpy as jnp
from jax import lax
from jax.experimental import pallas as pl
from jax.experimental.pallas import tpu as pltpu

F32 = jnp.float32
BF16 = jnp.bfloat16
MESH = pl.DeviceIdType.MESH

ATTN_HEADS = 16
ATTN_KV_HEADS = 4
HEAD_DIM = 64
GROUP = ATTN_HEADS // ATTN_KV_HEADS
WINDOW = 128
ATTN_WIDTH = ATTN_HEADS * HEAD_DIM
KV_WIDTH = ATTN_KV_HEADS * HEAD_DIM
HGRN_HEADS = 8
HGRN_DIM = 128
HGRN_WIDTH = HGRN_HEADS * HGRN_DIM
CHUNK = 64
SUB = 16
NSUB = CHUNK // SUB
REL_BUCKETS = 32
REL_MAX_DIST = 128
NORM_EPS = 1e-6
ADAM_LR = 0.001
ADAM_B1 = 0.9
ADAM_B2 = 0.999
ADAM_EPS = 1e-08
ADAM_WD = 0.01
ADAM_STEP = 10
LANES = 128
N_CHIPS = 4
N_DEV = 8
VMEM_LIMIT = 48 * 1024 * 1024
MATMUL_OPERAND_BYTES = 34 * 1024 * 1024
MATMUL_VMEM_BYTES = 44 * 1024 * 1024

OFF_AQ = 0
OFF_AK = OFF_AQ + ATTN_WIDTH
OFF_AV = OFF_AK + KV_WIDTH
OFF_AG = OFF_AV + KV_WIDTH
OFF_HQ = OFF_AG + ATTN_WIDTH
OFF_HF = OFF_HQ + HGRN_WIDTH
OFF_HI = OFF_HF + HGRN_WIDTH
OFF_HG = OFF_HI + HGRN_WIDTH
OFF_GA = OFF_HG + HGRN_WIDTH

NT_DIMS = (((1,), (1,)), ((), ()))
TN_DIMS = (((0,), (0,)), ((), ()))
NN_DIMS = (((1,), (0,)), ((), ()))


def _pick(n, cands):
    for c in cands:
        if n % c == 0:
            return c
    raise ValueError(f"no tile for {n} in {cands}")


def _params(sem):
    return pltpu.CompilerParams(dimension_semantics=sem, vmem_limit_bytes=VMEM_LIMIT)


def _bdot(a, b, dims=NN_DIMS):
    return lax.dot_general(a.astype(BF16), b.astype(BF16), dims, preferred_element_type=F32)


def _matmul(a, b, mode, out_dtype, name, slabs=1, after=None):
    if mode == "nn":
        (M, K), (K2, N) = a.shape, b.shape
    elif mode == "nt":
        (M, K), (N, K2) = a.shape, b.shape
    else:
        (K, M), (K2, N) = a.shape, b.shape
    assert K == K2
    nslab = N // slabs
    tm = _pick(M, (1024, 512, 256, 128))
    out_bytes = jnp.dtype(out_dtype).itemsize
    choices = []
    for tn in (2688, 1024, 896, 512, 384, 256, 128):
        for tk in (4096, 3584, 2048, 1792, 1536, 1024, 512, 256, 128):
            acc = 0 if tk == K else 4 * tm * tn
            if (nslab % tn == 0 and K % tk == 0 and 4 * tk * (tm + tn) <= MATMUL_OPERAND_BYTES
                    and 4 * tk * (tm + tn) + 2 * out_bytes * tm * tn + acc <= MATMUL_VMEM_BYTES):
                choices.append((K // tk > 1, -tn, tn, tk))
                break
    _, _, tn, tk = min(choices)
    nk = K // tk
    per = nslab // tn
    dims = {"nn": NN_DIMS, "nt": NT_DIMS, "tn": TN_DIMS}[mode]

    n_in = 2 if after is None else 3

    def body(*refs):
        a_ref, b_ref, o_ref, acc = refs[0], refs[1], refs[n_in], refs[n_in + 1:]
        part = lax.dot_general(a_ref[...], b_ref[...], dims, preferred_element_type=F32)
        if nk == 1:
            o_ref[...] = part.astype(o_ref.dtype)
            return
        acc_ref, = acc
        k = pl.program_id(2)

        @pl.when(k == 0)
        def _():
            acc_ref[...] = part

        @pl.when((k > 0) & (k < nk - 1))
        def _():
            acc_ref[...] += part

        @pl.when(k == nk - 1)
        def _():
            o_ref[...] = (acc_ref[...] + part).astype(o_ref.dtype)

    if mode == "tn":
        a_spec = pl.BlockSpec((tk, tm), lambda i, j, k: (k, i))
    else:
        a_spec = pl.BlockSpec((tm, tk), lambda i, j, k: (i, k))
    if mode == "nt":
        b_spec = pl.BlockSpec((tn, tk), lambda i, j, k: (j, k))
    else:
        b_spec = pl.BlockSpec((tk, tn), lambda i, j, k: (k, j))
    if slabs == 1:
        o_shape = jax.ShapeDtypeStruct((M, N), out_dtype)
        o_spec = pl.BlockSpec((tm, tn), lambda i, j, k: (i, j))
    else:
        o_shape = jax.ShapeDtypeStruct((slabs, M, nslab), out_dtype)
        o_spec = pl.BlockSpec((None, tm, tn), lambda i, j, k: (j // per, i, j % per))
    return pl.pallas_call(
        body, name=name, grid=(M // tm, N // tn, nk), in_specs=[a_spec, b_spec] + ([] if after is None else [ANY]),
        out_specs=o_spec, out_shape=o_shape,
        scratch_shapes=[pltpu.VMEM((tm, tn), F32)] if nk > 1 else [],
        compiler_params=_params(("parallel", "parallel", "arbitrary")),
    )(*((a, b) if after is None else (a, b, after)))


def _rmsnorm_fwd(x2, gain):
    T, D = x2.shape
    tr = _pick(T, (256, 128))

    def body(x_ref, g_ref, h_ref, r_ref):
        xv = x_ref[...]
        r = lax.rsqrt(jnp.mean(xv * xv, axis=-1, keepdims=True) + NORM_EPS)
        h_ref[...] = (xv * r * g_ref[...]).astype(BF16)
        r_ref[...] = r

    return pl.pallas_call(
        body, name="rmsnorm_pre_fwd", grid=(T // tr,),
        in_specs=[pl.BlockSpec((tr, D), lambda i: (i, 0)), pl.BlockSpec((1, D), lambda i: (0, 0))],
        out_specs=[pl.BlockSpec((tr, D), lambda i: (i, 0)), pl.BlockSpec((tr, 1), lambda i: (i, 0))],
        out_shape=[jax.ShapeDtypeStruct((T, D), BF16), jax.ShapeDtypeStruct((T, 1), F32)],
        compiler_params=_params(("parallel",)),
    )(x2, gain)


def _rmsnorm_bwd(dh, x2, rstd, gain, dout):
    T, D = x2.shape
    tr = _pick(T, (256, 128))

    def body(dh_ref, x_ref, r_ref, g_ref, do_ref, gx_ref, gg_ref):
        @pl.when(pl.program_id(0) == 0)
        def _():
            gg_ref[...] = jnp.zeros_like(gg_ref)

        n = x_ref[...] * r_ref[...]
        dhv = dh_ref[...]
        dn = dhv * g_ref[...]
        gx_ref[...] = do_ref[...] + r_ref[...] * (dn - n * jnp.mean(dn * n, axis=-1, keepdims=True))
        gg_ref[...] += jnp.sum(dhv * n, axis=0, keepdims=True)

    row = pl.BlockSpec((tr, D), lambda i: (i, 0))
    vec = pl.BlockSpec((1, D), lambda i: (0, 0))
    return pl.pallas_call(
        body, name="rmsnorm_pre_bwd", grid=(T // tr,),
        in_specs=[row, row, pl.BlockSpec((tr, 1), lambda i: (i, 0)), vec, row],
        out_specs=[row, vec],
        out_shape=[jax.ShapeDtypeStruct((T, D), F32), jax.ShapeDtypeStruct((1, D), F32)],
        compiler_params=_params(("arbitrary",)),
    )(dh, x2, rstd, gain, dout)


def _post_loss(yv, x2, tgt2, gain):
    T, D = x2.shape
    tr = _pick(T, (256, 128))

    def body(y_ref, x_ref, t_ref, g_ref, dy_ref, do_ref, loss_ref, gg_ref):
        @pl.when(pl.program_id(0) == 0)
        def _():
            gg_ref[...] = jnp.zeros_like(gg_ref)
            loss_ref[...] = jnp.zeros_like(loss_ref)

        yv_ = y_ref[...]
        r = lax.rsqrt(jnp.mean(yv_ * yv_, axis=-1, keepdims=True) + NORM_EPS)
        n = yv_ * r
        e = (x_ref[...] + n * g_ref[...]) - t_ref[...]
        loss_ref[...] += 0.5 * jnp.sum(jnp.mean(e * e, axis=-1, keepdims=True), axis=0, keepdims=True)
        dz = e / D
        do_ref[...] = dz
        gg_ref[...] += jnp.sum(dz * n, axis=0, keepdims=True)
        dn = dz * g_ref[...]
        dy_ref[...] = (r * (dn - n * jnp.mean(dn * n, axis=-1, keepdims=True))).astype(BF16)

    row = pl.BlockSpec((tr, D), lambda i: (i, 0))
    vec = pl.BlockSpec((1, D), lambda i: (0, 0))
    return pl.pallas_call(
        body, name="post_norm_loss", grid=(T // tr,),
        in_specs=[row, row, row, vec],
        out_specs=[row, row, pl.BlockSpec((1, 1), lambda i: (0, 0)), vec],
        out_shape=[jax.ShapeDtypeStruct((T, D), BF16), jax.ShapeDtypeStruct((T, D), F32),
                   jax.ShapeDtypeStruct((1, 1), F32), jax.ShapeDtypeStruct((1, D), F32)],
        compiler_params=_params(("arbitrary",)),
    )(yv, x2, tgt2, gain)


def _merge_fwd(proj, ua, uh):
    T, D = ua.shape
    tr = _pick(T, (256, 128))
    bw = _pick(D, (512, 256))
    assert OFF_GA % bw == 0
    oa, oh = OFF_GA // bw, (OFF_GA + D) // bw

    def body(ga_ref, gh_ref, ua_ref, uh_ref, m_ref):
        m_ref[...] = (jax.nn.sigmoid(ga_ref[...]) * ua_ref[...].astype(F32)
                      + jax.nn.sigmoid(gh_ref[...]) * uh_ref[...].astype(F32)).astype(BF16)

    blk = pl.BlockSpec((tr, bw), lambda i, j: (i, j))
    return pl.pallas_call(
        body, name="merge_fwd", grid=(T // tr, D // bw),
        in_specs=[pl.BlockSpec((tr, bw), lambda i, j: (i, oa + j)), pl.BlockSpec((tr, bw), lambda i, j: (i, oh + j)), blk, blk],
        out_specs=blk, out_shape=jax.ShapeDtypeStruct((T, D), BF16),
        compiler_params=_params(("parallel", "parallel")),
    )(proj, proj, ua, uh)


def _window(rows, cols, at):
    return pl.BlockSpec((pl.Element(rows), pl.Element(cols)), at)


def _merge_bwd(proj, ua, uh, dm):
    T, D = ua.shape
    tr = _pick(T, (128,))

    def body(ga_ref, gh_ref, ua_ref, uh_ref, dm_ref, dua_ref, duh_ref, dproj_ref):
        sa = jax.nn.sigmoid(ga_ref[...])
        sh = jax.nn.sigmoid(gh_ref[...])
        d = dm_ref[...]
        dua_ref[...] = (d * sa).astype(BF16)
        duh_ref[...] = (d * sh).astype(BF16)
        dproj_ref[:, :D] = (d * ua_ref[...].astype(F32) * sa * (1.0 - sa)).astype(BF16)
        dproj_ref[:, D:] = (d * uh_ref[...].astype(F32) * sh * (1.0 - sh)).astype(BF16)

    blk = pl.BlockSpec((tr, D), lambda i: (i, 0))
    o = jax.ShapeDtypeStruct((T, D), BF16)
    return pl.pallas_call(
        body, name="merge_bwd", grid=(T // tr,),
        in_specs=[_window(tr, D, lambda i: (i * tr, OFF_GA)), _window(tr, D, lambda i: (i * tr, OFF_GA + D)), blk, blk, blk],
        out_specs=[blk, blk, _window(tr, 2 * D, lambda i: (i * tr, OFF_GA))],
        out_shape=[o, o, jax.ShapeDtypeStruct((T, proj.shape[1]), BF16)],
        compiler_params=_params(("parallel",)),
    )(proj, proj, ua, uh, dm)


KV_PAIR = 2
PAIR_HEADS = KV_PAIR * GROUP


def _attn_mask(n):
    qi = lax.broadcasted_iota(jnp.int32, (WINDOW, 2 * WINDOW), 0)
    si = lax.broadcasted_iota(jnp.int32, (WINDOW, 2 * WINDOW), 1)
    dist = qi + WINDOW - si
    return (dist >= 0) & (dist < WINDOW) & ((si >= WINDOW) | (n > 0))


def _attn_group_fn(mask):
    mask = jnp.concatenate([mask] * GROUP, axis=0)

    def f(q, k, v, ag, bias, sink):
        s = lax.dot_general(q.astype(BF16), k.astype(BF16), NT_DIMS, preferred_element_type=F32) * (HEAD_DIM ** -0.5)
        s = jnp.where(mask, s + bias, -1e30)
        m = lax.stop_gradient(jnp.maximum(jnp.max(s, axis=-1, keepdims=True), sink))
        p = jnp.exp(s - m)
        den = jnp.sum(p, axis=-1, keepdims=True) + jnp.exp(sink - m)
        o = jnp.dot(p.astype(BF16), v.astype(BF16), preferred_element_type=F32)
        return o * (jax.nn.silu(ag) / den)

    return f


def _attn_specs(B_loc, nb, order):
    qw = PAIR_HEADS * HEAD_DIM
    kw = KV_PAIR * HEAD_DIM

    def rows(g):
        b, p, n = order(*g)
        return b * nb + n

    def prev(g):
        b, p, n = order(*g)
        return b * nb + jnp.maximum(n - 1, 0)

    def pp(g):
        return order(*g)[1]

    q = pl.BlockSpec((WINDOW, qw), lambda *g: (rows(g), OFF_AQ // qw + pp(g)))
    kc = pl.BlockSpec((WINDOW, kw), lambda *g: (rows(g), OFF_AK // kw + pp(g)))
    kp = pl.BlockSpec((WINDOW, kw), lambda *g: (prev(g), OFF_AK // kw + pp(g)))
    vc = pl.BlockSpec((WINDOW, kw), lambda *g: (rows(g), OFF_AV // kw + pp(g)))
    vp = pl.BlockSpec((WINDOW, kw), lambda *g: (prev(g), OFF_AV // kw + pp(g)))
    ag = pl.BlockSpec((WINDOW, qw), lambda *g: (rows(g), OFF_AG // qw + pp(g)))
    bias = pl.BlockSpec((PAIR_HEADS, WINDOW, 2 * WINDOW), lambda *g: (pp(g), 0, 0))
    sink = pl.BlockSpec((PAIR_HEADS, 8, LANES), lambda *g: (pp(g), 0, 0))
    return [q, kc, kp, vc, vp, ag, bias, sink], rows, pp


def _attn_operands(q_ref, kc_ref, kp_ref, vc_ref, vp_ref, ag_ref, bias_ref, sink_ref, j):
    lo, hi = j * HEAD_DIM, (j + 1) * HEAD_DIM
    k = jnp.concatenate([kp_ref[:, lo:hi], kc_ref[:, lo:hi]], axis=0)
    v = jnp.concatenate([vp_ref[:, lo:hi], vc_ref[:, lo:hi]], axis=0)
    heads = [j * GROUP + g for g in range(GROUP)]
    q = jnp.concatenate([q_ref[:, h * HEAD_DIM:(h + 1) * HEAD_DIM] for h in heads], axis=0)
    ag = jnp.concatenate([ag_ref[:, h * HEAD_DIM:(h + 1) * HEAD_DIM] for h in heads], axis=0)
    bias = jnp.concatenate([bias_ref[h] for h in heads], axis=0)
    sink = jnp.concatenate([jnp.broadcast_to(sink_ref[h, 0:1, 0:1], (WINDOW, 1)) for h in heads], axis=0)
    return q, k, v, ag, bias, sink


def _attn_fwd(proj, bias_tab, sinks_b, B_loc, S):
    T = B_loc * S
    nb = S // WINDOW
    n_pairs = ATTN_KV_HEADS // KV_PAIR
    in_specs, rows, pp = _attn_specs(B_loc, nb, lambda b, p, n: (b, p, n))

    def body(q_ref, kc_ref, kp_ref, vc_ref, vp_ref, ag_ref, bias_ref, sink_ref, ya_ref):
        f = _attn_group_fn(_attn_mask(pl.program_id(2)))
        for j in range(KV_PAIR):
            out = f(*_attn_operands(q_ref, kc_ref, kp_ref, vc_ref, vp_ref, ag_ref, bias_ref, sink_ref, j))
            for g in range(GROUP):
                h = j * GROUP + g
                ya_ref[:, h * HEAD_DIM:(h + 1) * HEAD_DIM] = out[g * WINDOW:(g + 1) * WINDOW].astype(BF16)

    return pl.pallas_call(
        body, name="attn_fwd", grid=(B_loc, n_pairs, nb), in_specs=in_specs,
        out_specs=pl.BlockSpec((WINDOW, PAIR_HEADS * HEAD_DIM), lambda *g: (rows(g), pp(g))),
        out_shape=jax.ShapeDtypeStruct((T, ATTN_WIDTH), BF16),
        compiler_params=_params(("parallel", "parallel", "parallel")),
    )(proj, proj, proj, proj, proj, proj, bias_tab, sinks_b)


def _attn_bwd(proj, bias_tab, sinks_b, dya, dproj, B_loc, S):
    nb = S // WINDOW
    n_pairs = ATTN_KV_HEADS // KV_PAIR
    qw, kw = PAIR_HEADS * HEAD_DIM, KV_PAIR * HEAD_DIM

    def at(off, width, back=0):
        return lambda b, i, p: ((b * nb + jnp.maximum(nb - 1 - i - back, 0)) * WINDOW,
                                pl.multiple_of(off + p * width, LANES))

    in_specs = [_window(WINDOW, qw, at(OFF_AQ, qw)), _window(WINDOW, kw, at(OFF_AK, kw)),
                _window(WINDOW, kw, at(OFF_AK, kw, 1)), _window(WINDOW, kw, at(OFF_AV, kw)),
                _window(WINDOW, kw, at(OFF_AV, kw, 1)), _window(WINDOW, qw, at(OFF_AG, qw)),
                pl.BlockSpec((PAIR_HEADS, WINDOW, 2 * WINDOW), lambda b, i, p: (p, 0, 0)),
                pl.BlockSpec((PAIR_HEADS, 8, LANES), lambda b, i, p: (p, 0, 0)),
                pl.BlockSpec((WINDOW, qw), lambda b, i, p: (b * nb + nb - 1 - i, p)), ANY]

    def body(q_ref, kc_ref, kp_ref, vc_ref, vp_ref, ag_ref, bias_ref, sink_ref, dya_ref, dproj_in,
             dproj_ref, dbias_ref, dsink_ref, dkc_ref, dvc_ref):
        b, i, p = pl.program_id(0), pl.program_id(1), pl.program_id(2)
        n = nb - 1 - i

        @pl.when((b == 0) & (i == 0) & (p == 0))
        def _():
            dbias_ref[...] = jnp.zeros_like(dbias_ref)
            dsink_ref[...] = jnp.zeros_like(dsink_ref)

        @pl.when(i == 0)
        def _():
            dkc_ref[p] = jnp.zeros((WINDOW, kw), F32)
            dvc_ref[p] = jnp.zeros((WINDOW, kw), F32)

        f = _attn_group_fn(_attn_mask(n))
        dk_carry, dv_carry = dkc_ref[p], dvc_ref[p]
        dqs, dags, dbiases, dsinks, dks, dvs = [], [], [], [], [], []
        for j in range(KV_PAIR):
            ops = _attn_operands(q_ref, kc_ref, kp_ref, vc_ref, vp_ref, ag_ref, bias_ref, sink_ref, j)
            _, vjp = jax.vjp(f, *ops)
            dout = jnp.concatenate([dya_ref[:, (j * GROUP + g) * HEAD_DIM:(j * GROUP + g + 1) * HEAD_DIM].astype(F32)
                                    for g in range(GROUP)], axis=0)
            dq, dk, dv, dag, dbias, dsink = vjp(dout)
            lo, hi = j * HEAD_DIM, (j + 1) * HEAD_DIM
            dks.append((dk[WINDOW:] + dk_carry[:, lo:hi]).astype(BF16))
            dvs.append((dv[WINDOW:] + dv_carry[:, lo:hi]).astype(BF16))
            dkc_ref[p, :, lo:hi] = dk[:WINDOW]
            dvc_ref[p, :, lo:hi] = dv[:WINDOW]
            for g in range(GROUP):
                blk = slice(g * WINDOW, (g + 1) * WINDOW)
                dqs.append(dq[blk].astype(BF16))
                dags.append(dag[blk].astype(BF16))
                dbiases.append(dbias[blk])
                dsinks.append(jnp.broadcast_to(jnp.sum(dsink[blk], axis=0, keepdims=True), (8, LANES)))

        for pair in range(n_pairs):
            @pl.when(p == pair)
            def _():
                for j in range(KV_PAIR):
                    col = (pair * KV_PAIR + j) * HEAD_DIM
                    dproj_ref[:, OFF_AK + col:OFF_AK + col + HEAD_DIM] = dks[j]
                    dproj_ref[:, OFF_AV + col:OFF_AV + col + HEAD_DIM] = dvs[j]
                for hh in range(PAIR_HEADS):
                    h = pair * PAIR_HEADS + hh
                    dproj_ref[:, OFF_AQ + h * HEAD_DIM:OFF_AQ + (h + 1) * HEAD_DIM] = dqs[hh]
                    dproj_ref[:, OFF_AG + h * HEAD_DIM:OFF_AG + (h + 1) * HEAD_DIM] = dags[hh]
                    dbias_ref[h] += dbiases[hh]
                    dsink_ref[h] += dsinks[hh]

    return pl.pallas_call(
        body, name="attn_bwd", grid=(B_loc, nb, n_pairs), in_specs=in_specs,
        out_specs=[_window(WINDOW, OFF_HQ, lambda b, i, p: ((b * nb + nb - 1 - i) * WINDOW, 0)),
                   pl.BlockSpec((ATTN_HEADS, WINDOW, 2 * WINDOW), lambda b, i, p: (0, 0, 0)),
                   pl.BlockSpec((ATTN_HEADS, 8, LANES), lambda b, i, p: (0, 0, 0))],
        out_shape=[jax.ShapeDtypeStruct(dproj.shape, dproj.dtype),
                   jax.ShapeDtypeStruct((ATTN_HEADS, WINDOW, 2 * WINDOW), F32),
                   jax.ShapeDtypeStruct((ATTN_HEADS, 8, LANES), F32)],
        input_output_aliases={9: 0},
        scratch_shapes=[pltpu.VMEM((n_pairs, WINDOW, kw), F32), pltpu.VMEM((n_pairs, WINDOW, kw), F32)],
        compiler_params=_params(("arbitrary", "arbitrary", "arbitrary")),
    )(proj, proj, proj, proj, proj, proj, bias_tab, sinks_b, dya, dproj)


class _HgrnPre:
    def __init__(self, fr, qr, lb, g_scr):
        t = lax.broadcasted_iota(jnp.int32, (CHUNK, CHUNK), 0)
        s = lax.broadcasted_iota(jnp.int32, (CHUNK, CHUNK), 1)
        self.sg = jax.nn.sigmoid(fr)
        self.f = lb + (1.0 - lb) * self.sg
        g = jnp.dot((t >= s).astype(F32), jnp.log(self.f), precision=lax.Precision.HIGHEST, preferred_element_type=F32)
        g_scr[...] = g
        self.g = g
        self.row = lax.broadcasted_iota(jnp.int32, g.shape, 0)
        self.refs = [jnp.zeros((1, g.shape[1]), F32)] + [g_scr[pl.ds(i * SUB - 1, 1), :] for i in range(1, NSUB)]
        self.gend = g_scr[pl.ds(CHUNK - 1, 1), :]
        refrow = jnp.zeros_like(g)
        for i in range(1, NSUB):
            refrow = jnp.where(self.row >= i * SUB, self.refs[i], refrow)
        self.sigq = jax.nn.sigmoid(qr)
        self.qs = qr * self.sigq
        self.k = 1.0 - self.f
        self.eg = jnp.exp(g)
        self.eqd = jnp.exp(g - refrow)
        self.ekd = [jnp.exp(jnp.where(self.row < (i + 1) * SUB, self.refs[i] - g, 0.0)) for i in range(NSUB)]
        self.ekdec = jnp.exp(self.gend - g)
        self.qg = self.qs * self.eg
        self.qd = self.qs * self.eqd
        self.kd = [self.k * e for e in self.ekd]
        self.kdec = self.k * self.ekdec
        self.egend = jnp.exp(self.gend)


def _hgrn_pair_mask():
    t = lax.broadcasted_iota(jnp.int32, (CHUNK, NSUB * CHUNK), 0)
    col = lax.broadcasted_iota(jnp.int32, (CHUNK, NSUB * CHUNK), 1)
    return ((t // SUB) == (col // CHUNK)) & ((col % CHUNK) <= t)


def _hgrn_head_out(p, lanes, state_t, v, mask):
    qg, qd = p.qg[:, lanes], p.qd[:, lanes]
    kall = jnp.concatenate([kd[:, lanes] for kd in p.kd], axis=0)
    vst = jnp.concatenate([v] * NSUB, axis=0)
    am = jnp.where(mask, _bdot(qd, kall, NT_DIMS), 0.0)
    o = _bdot(qg, state_t, NT_DIMS) + _bdot(am, vst)
    return o, (qg, qd, kall, am, vst)


def _hgrn_fwd(proj, lb, gain, B_loc, S):
    T = B_loc * S
    nc = S // CHUNK
    nh = HGRN_HEADS

    def at(off):
        return lambda b, n: ((b * nc + n) * CHUNK, off)

    vec = pl.BlockSpec((1, HGRN_WIDTH), lambda b, n: (0, 0))

    def body(q_ref, f_ref, v_ref, hg_ref, lb_ref, gain_ref, yh_ref, st_ref, state_scr, g_scr):
        @pl.when(pl.program_id(1) == 0)
        def _():
            state_scr[...] = jnp.zeros_like(state_scr)

        p = _HgrnPre(f_ref[...], q_ref[...], lb_ref[...], g_scr)
        v = v_ref[...]
        gate = gain_ref[...] * jax.nn.silu(hg_ref[...])
        mask = _hgrn_pair_mask()
        for hd in range(nh):
            lanes = slice(hd * HGRN_DIM, (hd + 1) * HGRN_DIM)
            st = state_scr[hd]
            st_ref[hd] = st
            o, _ = _hgrn_head_out(p, lanes, st, v[:, lanes], mask)
            rs = lax.rsqrt(jnp.mean(o * o, axis=-1, keepdims=True) + NORM_EPS)
            yh_ref[:, lanes] = (o * rs * gate[:, lanes]).astype(BF16)
            state_scr[hd] = st * p.egend[:, lanes] + _bdot(v[:, lanes], p.kdec[:, lanes], TN_DIMS)

    return pl.pallas_call(
        body, name="hgrn_fwd", grid=(B_loc, nc),
        in_specs=[_window(CHUNK, HGRN_WIDTH, at(OFF_HQ)), _window(CHUNK, HGRN_WIDTH, at(OFF_HF)),
                  _window(CHUNK, HGRN_WIDTH, at(OFF_HI)), _window(CHUNK, HGRN_WIDTH, at(OFF_HG)), vec, vec],
        out_specs=[pl.BlockSpec((CHUNK, HGRN_WIDTH), lambda b, n: (b * nc + n, 0)),
                   pl.BlockSpec((None, None, nh, HGRN_DIM, HGRN_DIM), lambda b, n: (b, n, 0, 0, 0))],
        out_shape=[jax.ShapeDtypeStruct((T, HGRN_WIDTH), BF16),
                   jax.ShapeDtypeStruct((B_loc, nc, nh, HGRN_DIM, HGRN_DIM), F32)],
        scratch_shapes=[pltpu.VMEM((nh, HGRN_DIM, HGRN_DIM), F32), pltpu.VMEM((CHUNK, HGRN_WIDTH), F32)],
        compiler_params=_params(("parallel", "arbitrary")),
    )(proj, proj, proj, proj, lb, gain)


def _hgrn_bwd(proj, states, lb, gain, dyh, dproj, B_loc, S):
    nc = S // CHUNK
    nh = HGRN_HEADS

    def at(off):
        return lambda b, i: ((b * nc + nc - 1 - i) * CHUNK, off)

    vec = pl.BlockSpec((1, HGRN_WIDTH), lambda b, i: (0, 0))
    in_specs = [_window(CHUNK, HGRN_WIDTH, at(OFF_HQ)), _window(CHUNK, HGRN_WIDTH, at(OFF_HF)),
                _window(CHUNK, HGRN_WIDTH, at(OFF_HI)), _window(CHUNK, HGRN_WIDTH, at(OFF_HG)), vec, vec,
                pl.BlockSpec((None, None, nh, HGRN_DIM, HGRN_DIM), lambda b, i: (b, nc - 1 - i, 0, 0, 0)),
                pl.BlockSpec((CHUNK, HGRN_WIDTH), lambda b, i: (b * nc + nc - 1 - i, 0)), ANY]
    acc_spec = pl.BlockSpec((None, 1, HGRN_WIDTH), lambda b, i: (b, 0, 0))

    def body(q_ref, f_ref, v_ref, hg_ref, lb_ref, gain_ref, st_ref, dyh_ref, dproj_in,
             dproj_ref, dlb_ref, dgain_ref, dstate_scr, g_scr, dg_scr):
        dq_ref, df_ref, dv_ref, dhg_ref = [dproj_ref.at[:, pl.ds(k * HGRN_WIDTH, HGRN_WIDTH)] for k in range(4)]

        @pl.when(pl.program_id(1) == 0)
        def _():
            dstate_scr[...] = jnp.zeros_like(dstate_scr)
            dlb_ref[...] = jnp.zeros_like(dlb_ref)
            dgain_ref[...] = jnp.zeros_like(dgain_ref)

        qr, lb, gain, hg, v = q_ref[...], lb_ref[...], gain_ref[...], hg_ref[...], v_ref[...]
        p = _HgrnPre(f_ref[...], qr, lb, g_scr)
        sgh = jax.nn.sigmoid(hg)
        sil = hg * sgh
        dy = dyh_ref[...].astype(F32)
        mask = _hgrn_pair_mask()
        dqg, dqd, dkdec, dv, dhg, dgend, dgain = [], [], [], [], [], [], []
        dkd = [[] for _ in range(NSUB)]
        heads = [slice(hd * HGRN_DIM, (hd + 1) * HGRN_DIM) for hd in range(nh)]
        sts = [st_ref[hd] for hd in range(nh)]
        dnews = [dstate_scr[hd] for hd in range(nh)]
        fwd = [_hgrn_head_out(p, lanes, st, v[:, lanes], mask) for lanes, st in zip(heads, sts)]
        for lanes, st, dnew in zip(heads, sts, dnews):
            dkdec_h = _bdot(v[:, lanes], dnew)
            dkdec.append(dkdec_h)
            dgend.append(jnp.sum(dkdec_h * p.kdec[:, lanes], axis=0, keepdims=True)
                         + jnp.sum(dnew * st, axis=0, keepdims=True) * p.egend[:, lanes])
        dos = []
        gate_grad = sgh * (1.0 + hg * (1.0 - sgh))
        for lanes, (o, _) in zip(heads, fwd):
            rs = lax.rsqrt(jnp.mean(o * o, axis=-1, keepdims=True) + NORM_EPS)
            n = o * rs
            dyn = dy[:, lanes] * n
            dgain.append(jnp.sum(dyn * sil[:, lanes], axis=0, keepdims=True))
            dhg.append(dyn * gain[:, lanes] * gate_grad[:, lanes])
            dn = dy[:, lanes] * gain[:, lanes] * sil[:, lanes]
            dos.append(rs * (dn - n * jnp.mean(dn * n, axis=-1, keepdims=True)))
        drs = []
        for hd, (lanes, st, dnew, do, (_, (qg, qd, kall, am, vst))) in enumerate(zip(heads, sts, dnews, dos, fwd)):
            dqg.append(_bdot(do, st))
            dstate_scr[hd] = _bdot(do, qg, TN_DIMS) + dnew * p.egend[:, lanes]
            drs.append(jnp.where(mask, _bdot(do, vst, NT_DIMS), 0.0))
            dvst = _bdot(am, do, TN_DIMS)
            dv.append(sum(dvst[i * CHUNK:(i + 1) * CHUNK] for i in range(NSUB)) + _bdot(p.kdec[:, lanes], dnew, NT_DIMS))
        for dr, (_, (qg, qd, kall, am, vst)) in zip(drs, fwd):
            dqd.append(_bdot(dr, kall))
            dkall = _bdot(dr, qd, TN_DIMS)
            for i in range(NSUB):
                dkd[i].append(dkall[i * CHUNK:(i + 1) * CHUNK])

        wide = lambda parts: jnp.concatenate(parts, axis=1)
        dqg, dqd, dkdec = wide(dqg), wide(dqd), wide(dkdec)
        t2 = dqd * p.qd
        dg = dqg * p.qg + t2 - dkdec * p.kdec
        dk = dkdec * p.ekdec
        dg_scr[...] = jnp.zeros_like(dg_scr)
        for i in range(NSUB):
            dkd_i = wide(dkd[i])
            tk = jnp.where(p.row < (i + 1) * SUB, dkd_i * p.kd[i], 0.0)
            dg = dg - tk
            dk = dk + dkd_i * p.ekd[i]
            if i >= 1:
                in_blk = (p.row >= i * SUB) & (p.row < (i + 1) * SUB)
                dg_scr[pl.ds(i * SUB - 1, 1), :] = (jnp.sum(tk, axis=0, keepdims=True)
                                                    - jnp.sum(jnp.where(in_blk, t2, 0.0), axis=0, keepdims=True))
        dg_scr[pl.ds(CHUNK - 1, 1), :] = wide(dgend)
        t = lax.broadcasted_iota(jnp.int32, (CHUNK, CHUNK), 0)
        s = lax.broadcasted_iota(jnp.int32, (CHUNK, CHUNK), 1)
        dlogf = jnp.dot((t <= s).astype(F32), dg + dg_scr[...], precision=lax.Precision.HIGHEST, preferred_element_type=F32)
        df = dlogf / p.f - dk
        df_ref[...] = (df * (1.0 - lb) * p.sg * (1.0 - p.sg)).astype(BF16)
        dlb_ref[...] += jnp.sum(df * (1.0 - p.sg), axis=0, keepdims=True)
        dq_ref[...] = ((dqg * p.eg + dqd * p.eqd) * p.sigq * (1.0 + qr * (1.0 - p.sigq))).astype(BF16)
        dv_ref[...] = wide(dv).astype(BF16)
        dhg_ref[...] = wide(dhg).astype(BF16)
        dgain_ref[...] += wide(dgain)

    acc = jax.ShapeDtypeStruct((B_loc, 1, HGRN_WIDTH), F32)
    return pl.pallas_call(
        body, name="hgrn_bwd", grid=(B_loc, nc), in_specs=in_specs,
        out_specs=[_window(CHUNK, 4 * HGRN_WIDTH, at(OFF_HQ)), acc_spec, acc_spec],
        out_shape=[jax.ShapeDtypeStruct(dproj.shape, dproj.dtype), acc, acc],
        input_output_aliases={8: 0},
        scratch_shapes=[pltpu.VMEM((nh, HGRN_DIM, HGRN_DIM), F32), pltpu.VMEM((CHUNK, HGRN_WIDTH), F32),
                        pltpu.VMEM((CHUNK, HGRN_WIDTH), F32)],
        compiler_params=_params(("parallel", "arbitrary")),
    )(proj, proj, proj, proj, lb, gain, states, dyh, dproj)


def _adamw(w, g, m, v, name):
    R, C = w.shape
    tr = _pick(R, (128, 64, 32, 16, 8)) if C > 1024 else _pick(R, (512, 256, 128, 64, 32, 16, 8))

    def body(w_ref, g_ref, m_ref, v_ref, d_ref, nm_ref, nv_ref, g_out_ref):
        gv = g_ref[...]
        g_out_ref[...] = gv
        nm = ADAM_B1 * m_ref[...] + (1.0 - ADAM_B1) * gv
        nv = ADAM_B2 * v_ref[...] + (1.0 - ADAM_B2) * (gv * gv)
        m_hat = nm / (1.0 - ADAM_B1 ** ADAM_STEP)
        v_hat = nv / (1.0 - ADAM_B2 ** ADAM_STEP)
        d_ref[...] = -ADAM_LR * (m_hat / (jnp.sqrt(v_hat) + ADAM_EPS) + ADAM_WD * w_ref[...])
        nm_ref[...] = nm
        nv_ref[...] = nv

    blk = pl.BlockSpec((tr, C), lambda i: (i, 0))
    o = jax.ShapeDtypeStruct((R, C), F32)
    return pl.pallas_call(
        body, name=name, grid=(R // tr,), in_specs=[blk] * 4, out_specs=[blk] * 4, out_shape=[o, o, o, o],
        compiler_params=_params(("parallel",)),
    )(w, g, m, v)


ANY = pl.BlockSpec(memory_space=pl.ANY)
VMEM_SPEC = pl.BlockSpec(memory_space=pltpu.VMEM)


def _place():
    x, y, c = lax.axis_index("x"), lax.axis_index("y"), lax.axis_index("c")
    other_chips = [(1 - x, y), (x, 1 - y), (1 - x, 1 - y)]
    return x, y, c, other_chips


def _cast_into_full(w, ax, s_arr, name):
    R, C = w.shape
    tr = _pick(R, (256, 128))
    nr = R // tr

    def body(s_ref, w_ref, o_ref):
        o_ref[...] = w_ref[...].astype(BF16)

    if ax == 1:
        shape, o_map = (R, N_CHIPS * C), lambda i, s: (i, s[0])
    else:
        shape, o_map = (N_CHIPS * R, C), lambda i, s: (s[0] * nr + i, 0)
    return pl.pallas_call(
        body, name=name,
        grid_spec=pltpu.PrefetchScalarGridSpec(
            num_scalar_prefetch=1, grid=(nr,), in_specs=[pl.BlockSpec((tr, C), lambda i, s: (i, 0))],
            out_specs=pl.BlockSpec((tr, C), o_map)),
        out_shape=jax.ShapeDtypeStruct(shape, BF16),
        compiler_params=_params(("parallel",)),
    )(s_arr, w)


class _Gather:
    def __init__(self, fulls, shard_shapes, axes, tag):
        self.shapes, self.axes, self.tag, self.nw = shard_shapes, axes, tag, len(fulls)
        self.fulls, self.sems, self.token = list(fulls), {}, None

    def start(self, peers, after=None):
        nw, np_ = self.nw, len(peers)

        def body(*refs):
            ins, sems = refs[:nw], refs[nw + (after is not None):nw + (after is not None) + 2 * np_]
            for k, j in enumerate(peers):
                for cp in self._peer_copies(ins, sems[2 * k], sems[2 * k + 1], j):
                    cp.start()
            refs[-1][...] = jnp.zeros_like(refs[-1])

        out = pl.pallas_call(
            body, name="gather_start_%s_%s" % (self.tag, "".join(map(str, peers))),
            out_shape=(*[pltpu.SemaphoreType.DMA((nw,))] * (2 * np_),
                       *[pltpu.HBM(f.shape, f.dtype) for f in self.fulls], jax.ShapeDtypeStruct((8, LANES), F32)),
            in_specs=[HBM_SPEC] * nw + ([] if after is None else [ANY]),
            out_specs=(*[SEM_SPEC] * (2 * np_), *[HBM_SPEC] * nw, VMEM_SPEC),
            input_output_aliases={k: 2 * np_ + k for k in range(nw)},
            compiler_params=pltpu.CompilerParams(has_side_effects=DATAFLOW),
        )(*[pltpu.with_memory_space_constraint(f, pltpu.HBM) for f in self.fulls], *(() if after is None else (after,)))
        for k, j in enumerate(peers):
            self.sems[j] = (out[2 * k], out[2 * k + 1])
        self.fulls = list(out[2 * np_:2 * np_ + nw])
        self.token = out[-1]

    def _region(self, ref, i, t, half):
        R, C = self.shapes[i]
        hr = R // 2
        if self.axes[i] == 1:
            return ref.at[pl.ds(half * hr, hr), pl.ds(pl.multiple_of(t * C, LANES), C)]
        return ref.at[pl.ds(t * R + half * hr, hr), :]

    def _peer_copies(self, refs, send_sems, recv_sems, j):
        x, y, c, chips = _place()
        s = 2 * x + y
        return [pltpu.make_async_remote_copy(
            src_ref=self._region(refs[i], i, s, c), dst_ref=self._region(refs[i], i, s, c), send_sem=send_sems.at[i],
            recv_sem=recv_sems.at[i], device_id=(*chips[j], c), device_id_type=MESH) for i in range(self.nw)]

    def wait(self, peers, after):
        nw, np_ = self.nw, len(peers)

        def body(*refs):
            ins, sems = refs[:nw], refs[nw:nw + 2 * np_]
            for k, j in enumerate(peers):
                for cp in self._peer_copies(ins, sems[2 * k], sems[2 * k + 1], j):
                    cp.wait_send()
                    cp.wait_recv()

        sem_args = [s for j in peers for s in self.sems[j]]
        out = pl.pallas_call(
            body, name="gather_wait_%s_%s" % (self.tag, "".join(map(str, peers))),
            out_shape=tuple(pltpu.HBM(f.shape, f.dtype) for f in self.fulls),
            in_specs=[HBM_SPEC] * nw + [SEM_SPEC] * (2 * np_) + [ANY], out_specs=tuple([HBM_SPEC] * nw),
            input_output_aliases={k: k for k in range(nw)},
            compiler_params=pltpu.CompilerParams(has_side_effects=DATAFLOW),
        )(*self.fulls, *sem_args, after)
        self.fulls = list(out)

    def forward(self, peers):
        nw, np_ = self.nw, len(peers)

        def body(*refs):
            ins, outs = refs[:nw], refs[nw:2 * nw]
            send_sems, recv_sems = refs[2 * nw:]
            x, y, c, chips = _place()
            cps = []
            for i in range(nw):
                for k, j in enumerate(peers):
                    t = 2 * chips[j][0] + chips[j][1]
                    cp = pltpu.make_async_remote_copy(
                        src_ref=self._region(ins[i], i, t, c), dst_ref=self._region(outs[i], i, t, c),
                        send_sem=send_sems.at[i * np_ + k], recv_sem=recv_sems.at[i * np_ + k],
                        device_id=(x, y, 1 - c), device_id_type=MESH)
                    cp.start()
                    cps.append(cp)
            for cp in cps:
                cp.wait()

        out = pl.pallas_call(
            body, name="gather_forward_%s_%s" % (self.tag, "".join(map(str, peers))),
            in_specs=[ANY] * nw, out_specs=[ANY] * nw,
            out_shape=[jax.ShapeDtypeStruct(f.shape, f.dtype) for f in self.fulls],
            input_output_aliases={i: i for i in range(nw)},
            scratch_shapes=[pltpu.SemaphoreType.DMA((nw * np_,)), pltpu.SemaphoreType.DMA((nw * np_,))],
        )(*self.fulls)
        self.fulls = list(out)


def _matmul_slab(a, wfull, slab_arr, prev, after, name):
    M, K = a.shape
    N = wfull.shape[1]
    nslab = N // N_CHIPS
    tn = _pick(nslab, (2688, 896, 512, 384, 256, 128))
    tm = _pick(M, (512, 256, 128) if tn > 1024 else (1024, 512, 256, 128))
    per = nslab // tn
    extra = [e for e in (prev, after) if e is not None]

    def body(slab_ref, a_ref, b_ref, *rest):
        rest[len(extra)][...] = jnp.dot(a_ref[...], b_ref[...], preferred_element_type=F32)

    return pl.pallas_call(
        body, name=name,
        grid_spec=pltpu.PrefetchScalarGridSpec(
            num_scalar_prefetch=1, grid=(M // tm, per),
            in_specs=[pl.BlockSpec((tm, K), lambda i, j, sl: (i, 0)),
                      pl.BlockSpec((K, tn), lambda i, j, sl: (0, sl[0] * per + j))] + [ANY] * len(extra),
            out_specs=pl.BlockSpec((tm, tn), lambda i, j, sl: (i, sl[0] * per + j))),
        out_shape=jax.ShapeDtypeStruct((M, N), F32),
        input_output_aliases={} if prev is None else {3: 0},
        compiler_params=_params(("parallel", "arbitrary")),
    )(slab_arr, a, wfull, *extra)


def _exchange_sibling_halves(gs, name):
    nw = len(gs)

    def body(*refs):
        ins, outs = refs[:nw], refs[nw:2 * nw]
        send_sems, recv_sems = refs[2 * nw:]
        x, y, c, _ = _place()
        cps = []
        for i in range(nw):
            cp = pltpu.make_async_remote_copy(src_ref=ins[i].at[:, 1 - c], dst_ref=outs[i], send_sem=send_sems.at[i],
                                              recv_sem=recv_sems.at[i], device_id=(x, y, 1 - c), device_id_type=MESH)
            cp.start()
            cps.append(cp)
        for cp in cps:
            cp.wait()

    return pl.pallas_call(
        body, name=name, in_specs=[ANY] * nw, out_specs=[ANY] * nw,
        out_shape=[jax.ShapeDtypeStruct((g.shape[0],) + g.shape[2:], g.dtype) for g in gs],
        scratch_shapes=[pltpu.SemaphoreType.DMA((nw,)), pltpu.SemaphoreType.DMA((nw,))],
    )(*gs)


HBM_SPEC = pl.BlockSpec(memory_space=pltpu.HBM)
SEM_SPEC = pl.BlockSpec(memory_space=pltpu.SEMAPHORE)
DATAFLOW = pltpu.SideEffectType.DATAFLOW_SIDE_EFFECTING


def _chip_copies(ins, lands, send_sems, recv_sems):
    x, y, c, chips = _place()
    return [pltpu.make_async_remote_copy(
        src_ref=ins[i].at[2 * chip[0] + chip[1]], dst_ref=lands[i].at[j], send_sem=send_sems.at[i * 3 + j],
        recv_sem=recv_sems.at[i * 3 + j], device_id=(*chip, c), device_id_type=MESH)
        for i in range(len(ins)) for j, chip in enumerate(chips)]


def _chips_send_start(ss, name):
    nw = len(ss)
    lands = [pltpu.with_memory_space_constraint(lax.empty((N_CHIPS - 1,) + s.shape[1:], s.dtype), pltpu.HBM) for s in ss]

    def body(*refs):
        ins, land_refs = refs[:nw], refs[nw:2 * nw]
        send_sems, recv_sems = refs[2 * nw], refs[2 * nw + 1]
        token = refs[-1]
        for cp in _chip_copies(ins, land_refs, send_sems, recv_sems):
            cp.start()
        token[...] = jnp.zeros_like(token)

    n = 3 * nw
    out = pl.pallas_call(
        body, name=name,
        out_shape=(pltpu.SemaphoreType.DMA((n,)), pltpu.SemaphoreType.DMA((n,)),
                   *[pltpu.HBM(s.shape, s.dtype) for s in ss], *[pltpu.HBM(l.shape, l.dtype) for l in lands],
                   jax.ShapeDtypeStruct((8, LANES), F32)),
        in_specs=[HBM_SPEC] * (2 * nw), out_specs=(SEM_SPEC, SEM_SPEC, *[HBM_SPEC] * (2 * nw), VMEM_SPEC),
        input_output_aliases={k: 2 + k for k in range(2 * nw)},
        compiler_params=pltpu.CompilerParams(has_side_effects=DATAFLOW),
    )(*[pltpu.with_memory_space_constraint(s, pltpu.HBM) for s in ss], *lands)
    return out[0], out[1], list(out[2:2 + nw]), list(out[2 + nw:2 + 2 * nw]), out[-1]


def _chips_send_wait(send_sems, recv_sems, ss, lands, after, name):
    nw = len(ss)

    def body(*refs):
        ins, land_refs = refs[:nw], refs[nw:2 * nw]
        s_sems, r_sems = refs[2 * nw], refs[2 * nw + 1]
        for cp in _chip_copies(ins, land_refs, s_sems, r_sems):
            cp.wait_send()
            cp.wait_recv()

    out = pl.pallas_call(
        body, name=name,
        out_shape=(*[pltpu.HBM(s.shape, s.dtype) for s in ss], *[pltpu.HBM(l.shape, l.dtype) for l in lands]),
        in_specs=[HBM_SPEC] * (2 * nw) + [SEM_SPEC, SEM_SPEC, ANY], out_specs=tuple([HBM_SPEC] * (2 * nw)),
        input_output_aliases={k: k for k in range(2 * nw)},
        compiler_params=pltpu.CompilerParams(has_side_effects=DATAFLOW),
    )(*ss, *lands, send_sems, recv_sems, after)
    return list(out[nw:])


def _sum_small(pack):
    rows = pack.shape[0]

    def body(pack_ref, sum_ref, all_ref, send_sems, recv_sems):
        x, y, c, _ = _place()
        me = 4 * x + 2 * y + c
        all_ref[me] = pack_ref[...]
        cps = []
        for k in range(1, N_DEV):
            to = (1 - x if k & 4 else x, 1 - y if k & 2 else y, 1 - c if k & 1 else c)
            cp = pltpu.make_async_remote_copy(
                src_ref=pack_ref, dst_ref=all_ref.at[me], send_sem=send_sems.at[k - 1],
                recv_sem=recv_sems.at[k - 1], device_id=to, device_id_type=MESH)
            cp.start()
            cps.append(cp)
        for cp in cps:
            cp.wait()
        total = all_ref[0]
        for d in range(1, N_DEV):
            total = total + all_ref[d]
        sum_ref[...] = total

    return pl.pallas_call(
        body, name="sum_small_grads", in_specs=[VMEM_SPEC], out_specs=VMEM_SPEC,
        out_shape=jax.ShapeDtypeStruct(pack.shape, F32),
        scratch_shapes=[pltpu.VMEM((N_DEV, rows, LANES), F32), pltpu.SemaphoreType.DMA((N_DEV - 1,)),
                        pltpu.SemaphoreType.DMA((N_DEV - 1,))],
    )(pack)


def _share_with_sibling(fs, name):
    nw = len(fs)

    def body(*refs):
        ins, outs = refs[:nw], refs[nw:2 * nw]
        send_sems, recv_sems = refs[2 * nw:]
        x, y, c, _ = _place()
        cps = []
        for i in range(nw):
            cp = pltpu.make_async_remote_copy(src_ref=ins[i].at[c], dst_ref=outs[i].at[c], send_sem=send_sems.at[i],
                                              recv_sem=recv_sems.at[i], device_id=(x, y, 1 - c), device_id_type=MESH)
            cp.start()
            cps.append(cp)
        for cp in cps:
            cp.wait()

    return pl.pallas_call(
        body, name=name, in_specs=[ANY] * nw, out_specs=[ANY] * nw,
        out_shape=[jax.ShapeDtypeStruct(f.shape, f.dtype) for f in fs],
        input_output_aliases={i: i for i in range(nw)},
        scratch_shapes=[pltpu.SemaphoreType.DMA((nw,)), pltpu.SemaphoreType.DMA((nw,))],
    )(*fs)


def _sum_sibling(g, land, c_arr, name):
    _, _, P, Q = g.shape
    tp = _pick(P, (256, 128, 64))

    def body(c_ref, g_ref, l_ref, s_ref):
        s_ref[...] = (g_ref[...].astype(F32) + l_ref[...].astype(F32)).astype(BF16)

    return pl.pallas_call(
        body, name=name,
        grid_spec=pltpu.PrefetchScalarGridSpec(
            num_scalar_prefetch=1, grid=(N_CHIPS, P // tp),
            in_specs=[pl.BlockSpec((None, None, tp, Q), lambda t, r, cr: (t, cr[0], r, 0)),
                      pl.BlockSpec((None, tp, Q), lambda t, r, cr: (t, r, 0))],
            out_specs=pl.BlockSpec((None, tp, Q), lambda t, r, cr: (t, r, 0))),
        out_shape=jax.ShapeDtypeStruct((N_CHIPS, P, Q), BF16),
        compiler_params=_params(("parallel", "parallel")),
    )(c_arr, g, land)


def _sum_chips(g, land, recv, sc_arr, name):
    _, _, P, Q = g.shape
    tp = _pick(P, (256, 128, 64))

    def body(sc_ref, g_ref, l_ref, r_ref, f_ref):
        acc = g_ref[...].astype(F32) + l_ref[...].astype(F32)
        for j in range(N_CHIPS - 1):
            acc = acc + r_ref[j].astype(F32)
        f_ref[...] = acc

    return pl.pallas_call(
        body, name=name,
        grid_spec=pltpu.PrefetchScalarGridSpec(
            num_scalar_prefetch=1, grid=(P // tp,),
            in_specs=[pl.BlockSpec((None, None, tp, Q), lambda r, sc: (sc[0], sc[1], r, 0)),
                      pl.BlockSpec((None, tp, Q), lambda r, sc: (sc[0], r, 0)),
                      pl.BlockSpec((N_CHIPS - 1, tp, Q), lambda r, sc: (0, r, 0))],
            out_specs=pl.BlockSpec((None, tp, Q), lambda r, sc: (sc[1], r, 0))),
        out_shape=jax.ShapeDtypeStruct((2, P, Q), F32),
        compiler_params=_params(("parallel",)),
    )(sc_arr, g, land, recv)


class _Reduction:
    def __init__(self, names, pieces, lands, flight):
        self.names, self.pieces, self.lands = names, pieces, lands
        self.send_sems, self.recv_sems, self.sums, self.zones, self.token = flight


def _reduce_start(grads, names, c_arr):
    pieces = [g.reshape(N_CHIPS, 2, -1, g.shape[-1]) for g in grads]
    tag = names[0] if len(names) == 1 else "branches"
    lands = _exchange_sibling_halves(pieces, "grads_to_sibling_" + tag)
    sums = [_sum_sibling(g, l, c_arr, "sum_sibling_" + nm) for g, l, nm in zip(pieces, lands, names)]
    return _Reduction(names, pieces, lands, _chips_send_start(sums, "grads_to_chips_start_" + tag))


def _reduce_finish(red, after, sc_arr):
    tag = red.names[0] if len(red.names) == 1 else "branches"
    recvs = _chips_send_wait(red.send_sems, red.recv_sems, red.sums, red.zones, after, "grads_to_chips_wait_" + tag)
    halves = [_sum_chips(g, l, r, sc_arr, "sum_chips_" + nm) for g, l, r, nm in zip(red.pieces, red.lands, recvs, red.names)]
    return _share_with_sibling(halves, "grads_share_sibling_" + tag)


def _t5_bucket(dist):
    max_exact = REL_BUCKETS // 2
    d = jnp.maximum(dist, 0)
    df = jnp.maximum(d, 1).astype(F32)
    large = max_exact + (jnp.log(df / max_exact) / math.log(REL_MAX_DIST / max_exact)
                         * (REL_BUCKETS - max_exact)).astype(jnp.int32)
    large = jnp.minimum(large, REL_BUCKETS - 1)
    return jnp.where(d < max_exact, d, large)


def _bucket_table():
    qi = jnp.arange(WINDOW)[:, None]
    si = jnp.arange(2 * WINDOW)[None, :]
    return _t5_bucket(qi + WINDOW - si)


TILE_WORDS = 8 * LANES


def _tile_rows(shape):
    return -(-math.prod(shape) // TILE_WORDS) * 8


def _rows_of(a):
    flat = a.reshape(-1).astype(F32)
    n = _tile_rows(a.shape) * LANES
    return jnp.pad(flat, (0, n - flat.shape[0])).reshape(-1, LANES)


def _pack_rows(parts):
    return jnp.concatenate([_rows_of(p) for p in parts], axis=0)


def _unpack_rows(packed, shapes):
    out, at = [], 0
    for shp in shapes:
        n, nr = math.prod(shp), _tile_rows(shp)
        out.append(packed[at:at + nr].reshape(-1)[:n].reshape(shp))
        at += nr
    return out


def kernel(x, norm_pre, w_in, rel_bias, attn_sinks, lb_logits, hgrn_norm, w_branch_attn, w_branch_hgrn, w_out, norm_post, loss_target, m_norm_pre, m_w_in, m_rel_bias, m_attn_sinks, m_lb_logits, m_hgrn_norm, m_w_branch_attn, m_w_branch_hgrn, m_w_out, m_norm_post, v_norm_pre, v_w_in, v_rel_bias, v_attn_sinks, v_lb_logits, v_hgrn_norm, v_w_branch_attn, v_w_branch_hgrn, v_w_out, v_norm_post):
    B_loc, S, D = x.shape
    T = B_loc * S
    x2 = x.reshape(T, D)
    tgt2 = loss_target.reshape(T, D)
    my_x, my_y, my_c = lax.axis_index("x"), lax.axis_index("y"), lax.axis_index("c")

    c_arr = jnp.reshape(my_c, (1,)).astype(jnp.int32)
    s_arr = jnp.reshape(2 * my_x + my_y, (1,)).astype(jnp.int32)
    sc_arr = jnp.concatenate([s_arr, c_arr])
    shard_ws = [w_in[0], w_branch_attn[0], w_branch_hgrn[0], w_out[0]]
    shard_axes = (1, 1, 1, 0)
    names = ["w_in", "w_branch_attn", "w_branch_hgrn", "w_out"]
    placed = [_cast_into_full(w, ax, s_arr, "cast_" + nm) for w, ax, nm in zip(shard_ws, shard_axes, names)]
    peer_slabs = [jnp.reshape(t, (1,)).astype(jnp.int32)
                  for t in (2 * (1 - my_x) + my_y, 2 * my_x + 1 - my_y, 2 * (1 - my_x) + 1 - my_y)]

    buckets = _bucket_table()
    onehot = (buckets.reshape(-1)[:, None] == jnp.arange(REL_BUCKETS)[None, :]).astype(F32)
    bias_tab = jnp.dot(onehot, rel_bias.astype(F32), precision=lax.Precision.HIGHEST).T.reshape(ATTN_HEADS, WINDOW, 2 * WINDOW)
    sinks_b = jnp.broadcast_to(attn_sinks[0].astype(F32)[:, None, None], (ATTN_HEADS, 8, LANES))
    lb_fn = lambda l: jnp.cumsum(jax.nn.softmax(l.astype(F32), axis=0), axis=0)[:1]
    lb, lb_vjp = jax.vjp(lb_fn, lb_logits)
    gain_h = hgrn_norm[0].reshape(1, HGRN_WIDTH)

    h, rstd = _rmsnorm_fwd(x2, norm_pre)
    gather_in = _Gather(placed[:1], [shard_ws[0].shape], shard_axes[:1], "w_in")
    gather_in.start([0, 1])
    proj = _matmul_slab(h, gather_in.fulls[0], s_arr, None, gather_in.token, "in_proj_own")
    for j in range(N_CHIPS - 1):
        gather_in.wait([j], proj)
        gather_in.forward([j])
        after = None
        if j == 0:
            gather_in.start([2])
            after = gather_in.token
        if j == N_CHIPS - 2:
            gather_rest = _Gather(placed[1:], [w.shape for w in shard_ws[1:]], shard_axes[1:], "rest")
            gather_rest.start([0, 1, 2], gather_in.fulls[0])
            after = gather_rest.token
        proj = _matmul_slab(h, gather_in.fulls[0], peer_slabs[j], proj, after, "in_proj_peer%d" % j)
    win_f = gather_in.fulls[0]
    ya = _attn_fwd(proj, bias_tab, sinks_b, B_loc, S)
    yh, states = _hgrn_fwd(proj, lb, gain_h, B_loc, S)
    gather_rest.wait([0, 1, 2], yh)
    gather_rest.forward([0, 1, 2])
    wba_f, wbh_f, wout_f = gather_rest.fulls
    ua = _matmul(ya, wba_f, "nn", BF16, "branch_attn_proj")
    uh = _matmul(yh, wbh_f, "nn", BF16, "branch_hgrn_proj")
    merged = _merge_fwd(proj, ua, uh)
    yv = _matmul(merged, wout_f, "nn", F32, "out_proj")
    dy, dout, loss_p, gnpost_p = _post_loss(yv, x2, tgt2, norm_post)

    dmerged = _matmul(dy, wout_f, "nt", F32, "out_proj_dgrad")
    g_wout = _matmul(merged, dy, "tn", BF16, "out_proj_wgrad")
    d_ua, d_uh, dproj = _merge_bwd(proj, ua, uh, dmerged)
    g_wba = _matmul(ya, d_ua, "tn", BF16, "branch_attn_wgrad", slabs=N_CHIPS)
    g_wbh = _matmul(yh, d_uh, "tn", BF16, "branch_hgrn_wgrad", slabs=N_CHIPS)
    late = _reduce_start([g_wba, g_wbh, g_wout], names[1:], c_arr)
    d_ya = _matmul(d_ua, wba_f, "nt", BF16, "branch_attn_dgrad", after=late.token)
    d_yh = _matmul(d_uh, wbh_f, "nt", BF16, "branch_hgrn_dgrad", after=late.token)
    dproj, dbias_p, dsinks_p = _attn_bwd(proj, bias_tab, sinks_b, d_ya, dproj, B_loc, S)
    dproj, dlb_p, dgain_p = _hgrn_bwd(proj, states, lb, gain_h, d_yh, dproj, B_loc, S)
    g_win = _matmul(h, dproj, "tn", BF16, "in_proj_wgrad", slabs=N_CHIPS)
    last = _reduce_start([g_win], names[:1], c_arr)
    dh = _matmul(dproj, win_f, "nt", F32, "in_proj_dgrad", after=last.token)
    grad_x2, gnpre_p = _rmsnorm_bwd(dh, x2, rstd, norm_pre, dout)
    shared = _reduce_finish(last, grad_x2, sc_arr) + _reduce_finish(late, grad_x2, sc_arr)

    grelb_p = jnp.dot(dbias_p.reshape(ATTN_HEADS, -1), onehot, precision=lax.Precision.HIGHEST).T
    gsinks_p = dsinks_p[:, 0, 0]
    dlb_sum = jnp.sum(dlb_p, axis=0).reshape(1, HGRN_WIDTH)
    ghn_p = jnp.sum(dgain_p, axis=0).reshape(HGRN_HEADS, HGRN_DIM)
    small_parts = [gnpre_p, gnpost_p, grelb_p, gsinks_p, dlb_sum, ghn_p, loss_p]
    small_shapes = [p.shape for p in small_parts]
    pack_sum = _sum_small(_pack_rows(small_parts))
    big_w = [w_in, w_branch_attn, w_branch_hgrn, w_out]
    big_m = [m_w_in, m_w_branch_attn, m_w_branch_hgrn, m_w_out]
    big_v = [v_w_in, v_w_branch_attn, v_w_branch_hgrn, v_w_out]
    big = {}
    for nm, gs, w, m, v in zip(names, shared, big_w, big_m, big_v):
        shp = w.shape
        g2 = gs.reshape(shp[1], shp[2])
        d, nm_, nv_, g_out = _adamw(w[0], g2, m[0], v[0], "adamw_" + nm)
        big[nm] = tuple(a.reshape(shp) for a in (g_out, d, nm_, nv_))

    gnpre, gnpost, grelb, gsinks, dlb, ghn, loss = _unpack_rows(pack_sum, small_shapes)
    (g_lb_logits,) = lb_vjp(dlb)
    small_names = ["norm_pre", "rel_bias", "attn_sinks", "lb_logits", "hgrn_norm", "norm_post"]
    small_w = [norm_pre, rel_bias, attn_sinks, lb_logits, hgrn_norm, norm_post]
    small_m = [m_norm_pre, m_rel_bias, m_attn_sinks, m_lb_logits, m_hgrn_norm, m_norm_post]
    small_v = [v_norm_pre, v_rel_bias, v_attn_sinks, v_lb_logits, v_hgrn_norm, v_norm_post]
    small_g = [gnpre.reshape(norm_pre.shape), grelb.reshape(rel_bias.shape), gsinks.reshape(attn_sinks.shape),
               g_lb_logits.reshape(lb_logits.shape), ghn.reshape(hgrn_norm.shape), gnpost.reshape(norm_post.shape)]
    shapes = [w.shape for w in small_w]
    d_s, nm_s, nv_s, _ = _adamw(_pack_rows(small_w), _pack_rows(small_g), _pack_rows(small_m), _pack_rows(small_v),
                             "adamw_small")
    small = {}
    for nm, g, d, m_, v_ in zip(small_names, small_g, _unpack_rows(d_s, shapes), _unpack_rows(nm_s, shapes),
                                _unpack_rows(nv_s, shapes)):
        small[nm] = (g, d, m_, v_)

    allw = {**big, **small}
    order = ["norm_pre", "w_in", "rel_bias", "attn_sinks", "lb_logits", "hgrn_norm", "w_branch_attn", "w_branch_hgrn",
             "w_out", "norm_post"]
    outs = [loss.reshape(()), grad_x2.reshape(B_loc, S, D)]
    for k in range(4):
        outs += [allw[nm][k] for nm in order]
    return tuple(outs)
```

```python
import functools
import math

import jax
import jax.numpy as jnp
from jax import lax
from jax.experimental import pallas as pl
from jax.experimental.pallas import tpu as pltpu

F32 = jnp.float32
BF16 = jnp.bfloat16
MESH = pl.DeviceIdType.MESH

ATTN_HEADS = 16
ATTN_KV_HEADS = 4
HEAD_DIM = 64
GROUP = ATTN_HEADS // ATTN_KV_HEADS
WINDOW = 128
ATTN_WIDTH = ATTN_HEADS * HEAD_DIM
KV_WIDTH = ATTN_KV_HEADS * HEAD_DIM
HGRN_HEADS = 8
HGRN_DIM = 128
HGRN_WIDTH = HGRN_HEADS * HGRN_DIM
CHUNK = 64
SUB = 16
NSUB = CHUNK // SUB
REL_BUCKETS = 32
REL_MAX_DIST = 128
NORM_EPS = 1e-6
ADAM_LR = 0.001
ADAM_B1 = 0.9
ADAM_B2 = 0.999
ADAM_EPS = 1e-08
ADAM_WD = 0.01
ADAM_STEP = 10
LANES = 128
N_CHIPS = 4
N_DEV = 8
VMEM_LIMIT = 48 * 1024 * 1024
MATMUL_OPERAND_BYTES = 34 * 1024 * 1024
MATMUL_VMEM_BYTES = 44 * 1024 * 1024

OFF_AQ = 0
OFF_AK = OFF_AQ + ATTN_WIDTH
OFF_AV = OFF_AK + KV_WIDTH
OFF_AG = OFF_AV + KV_WIDTH
OFF_HQ = OFF_AG + ATTN_WIDTH
OFF_HF = OFF_HQ + HGRN_WIDTH
OFF_HI = OFF_HF + HGRN_WIDTH
OFF_HG = OFF_HI + HGRN_WIDTH
OFF_GA = OFF_HG + HGRN_WIDTH

NT_DIMS = (((1,), (1,)), ((), ()))
TN_DIMS = (((0,), (0,)), ((), ()))
NN_DIMS = (((1,), (0,)), ((), ()))


def _pick(n, cands):
    for c in cands:
        if n % c == 0:
            return c
    raise ValueError(f"no tile for {n} in {cands}")


def _params(sem):
    return pltpu.CompilerParams(dimension_semantics=sem, vmem_limit_bytes=VMEM_LIMIT)


def _bdot(a, b, dims=NN_DIMS):
    return lax.dot_general(a.astype(BF16), b.astype(BF16), dims, preferred_element_type=F32)


def _matmul(a, b, mode, out_dtype, name, slabs=1, after=None):
    if mode == "nn":
        (M, K), (K2, N) = a.shape, b.shape
    elif mode == "nt":
        (M, K), (N, K2) = a.shape, b.shape
    else:
        (K, M), (K2, N) = a.shape, b.shape
    assert K == K2
    nslab = N // slabs
    tm = _pick(M, (1024, 512, 256, 128))
    out_bytes = jnp.dtype(out_dtype).itemsize
    choices = []
    for tn in (2688, 1024, 896, 512, 384, 256, 128):
        for tk in (4096, 3584, 2048, 1792, 1536, 1024, 512, 256, 128):
            acc = 0 if tk == K else 4 * tm * tn
            if (nslab % tn == 0 and K % tk == 0 and 4 * tk * (tm + tn) <= MATMUL_OPERAND_BYTES
                    and 4 * tk * (tm + tn) + 2 * out_bytes * tm * tn + acc <= MATMUL_VMEM_BYTES):
                choices.append((K // tk > 1, -tn, tn, tk))
                break
    _, _, tn, tk = min(choices)
    nk = K // tk
    per = nslab // tn
    dims = {"nn": NN_DIMS, "nt": NT_DIMS, "tn": TN_DIMS}[mode]

    n_in = 2 if after is None else 3

    def body(*refs):
        a_ref, b_ref, o_ref, acc = refs[0], refs[1], refs[n_in], refs[n_in + 1:]
        part = lax.dot_general(a_ref[...], b_ref[...], dims, preferred_element_type=F32)
        if nk == 1:
            o_ref[...] = part.astype(o_ref.dtype)
            return
        acc_ref, = acc
        k = pl.program_id(2)

        @pl.when(k == 0)
        def _():
            acc_ref[...] = part

        @pl.when((k > 0) & (k < nk - 1))
        def _():
            acc_ref[...] += part

        @pl.when(k == nk - 1)
        def _():
            o_ref[...] = (acc_ref[...] + part).astype(o_ref.dtype)

    if mode == "tn":
        a_spec = pl.BlockSpec((tk, tm), lambda i, j, k: (k, i))
    else:
        a_spec = pl.BlockSpec((tm, tk), lambda i, j, k: (i, k))
    if mode == "nt":
        b_spec = pl.BlockSpec((tn, tk), lambda i, j, k: (j, k))
    else:
        b_spec = pl.BlockSpec((tk, tn), lambda i, j, k: (k, j))
    if slabs == 1:
        o_shape = jax.ShapeDtypeStruct((M, N), out_dtype)
        o_spec = pl.BlockSpec((tm, tn), lambda i, j, k: (i, j))
    else:
        o_shape = jax.ShapeDtypeStruct((slabs, M, nslab), out_dtype)
        o_spec = pl.BlockSpec((None, tm, tn), lambda i, j, k: (j // per, i, j % per))
    return pl.pallas_call(
        body, name=name, grid=(M // tm, N // tn, nk), in_specs=[a_spec, b_spec] + ([] if after is None else [ANY]),
        out_specs=o_spec, out_shape=o_shape,
        scratch_shapes=[pltpu.VMEM((tm, tn), F32)] if nk > 1 else [],
        compiler_params=_params(("parallel", "parallel", "arbitrary")),
    )(*((a, b) if after is None else (a, b, after)))


def _rmsnorm_fwd(x2, gain):
    T, D = x2.shape
    tr = _pick(T, (256, 128))

    def body(x_ref, g_ref, h_ref, r_ref):
        xv = x_ref[...]
        r = lax.rsqrt(jnp.mean(xv * xv, axis=-1, keepdims=True) + NORM_EPS)
        h_ref[...] = (xv * r * g_ref[...]).astype(BF16)
        r_ref[...] = r

    return pl.pallas_call(
        body, name="rmsnorm_pre_fwd", grid=(T // tr,),
        in_specs=[pl.BlockSpec((tr, D), lambda i: (i, 0)), pl.BlockSpec((1, D), lambda i: (0, 0))],
        out_specs=[pl.BlockSpec((tr, D), lambda i: (i, 0)), pl.BlockSpec((tr, 1), lambda i: (i, 0))],
        out_shape=[jax.ShapeDtypeStruct((T, D), BF16), jax.ShapeDtypeStruct((T, 1), F32)],
        compiler_params=_params(("parallel",)),
    )(x2, gain)


def _rmsnorm_bwd(dh, x2, rstd, gain, dout):
    T, D = x2.shape
    tr = _pick(T, (256, 128))

    def body(dh_ref, x_ref, r_ref, g_ref, do_ref, gx_ref, gg_ref):
        @pl.when(pl.program_id(0) == 0)
        def _():
            gg_ref[...] = jnp.zeros_like(gg_ref)

        n = x_ref[...] * r_ref[...]
        dhv = dh_ref[...]
        dn = dhv * g_ref[...]
        gx_ref[...] = do_ref[...] + r_ref[...] * (dn - n * jnp.mean(dn * n, axis=-1, keepdims=True))
        gg_ref[...] += jnp.sum(dhv * n, axis=0, keepdims=True)

    row = pl.BlockSpec((tr, D), lambda i: (i, 0))
    vec = pl.BlockSpec((1, D), lambda i: (0, 0))
    return pl.pallas_call(
        body, name="rmsnorm_pre_bwd", grid=(T // tr,),
        in_specs=[row, row, pl.BlockSpec((tr, 1), lambda i: (i, 0)), vec, row],
        out_specs=[row, vec],
        out_shape=[jax.ShapeDtypeStruct((T, D), F32), jax.ShapeDtypeStruct((1, D), F32)],
        compiler_params=_params(("arbitrary",)),
    )(dh, x2, rstd, gain, dout)


def _post_loss(yv, x2, tgt2, gain):
    T, D = x2.shape
    tr = _pick(T, (256, 128))

    def body(y_ref, x_ref, t_ref, g_ref, dy_ref, do_ref, loss_ref, gg_ref):
        @pl.when(pl.program_id(0) == 0)
        def _():
            gg_ref[...] = jnp.zeros_like(gg_ref)
            loss_ref[...] = jnp.zeros_like(loss_ref)

        yv_ = y_ref[...]
        r = lax.rsqrt(jnp.mean(yv_ * yv_, axis=-1, keepdims=True) + NORM_EPS)
        n = yv_ * r
        e = (x_ref[...] + n * g_ref[...]) - t_ref[...]
        loss_ref[...] += 0.5 * jnp.sum(jnp.mean(e * e, axis=-1, keepdims=True), axis=0, keepdims=True)
        dz = e / D
        do_ref[...] = dz
        gg_ref[...] += jnp.sum(dz * n, axis=0, keepdims=True)
        dn = dz * g_ref[...]
        dy_ref[...] = (r * (dn - n * jnp.mean(dn * n, axis=-1, keepdims=True))).astype(BF16)

    row = pl.BlockSpec((tr, D), lambda i: (i, 0))
    vec = pl.BlockSpec((1, D), lambda i: (0, 0))
    return pl.pallas_call(
        body, name="post_norm_loss", grid=(T // tr,),
        in_specs=[row, row, row, vec],
        out_specs=[row, row, pl.BlockSpec((1, 1), lambda i: (0, 0)), vec],
        out_shape=[jax.ShapeDtypeStruct((T, D), BF16), jax.ShapeDtypeStruct((T, D), F32),
                   jax.ShapeDtypeStruct((1, 1), F32), jax.ShapeDtypeStruct((1, D), F32)],
        compiler_params=_params(("arbitrary",)),
    )(yv, x2, tgt2, gain)


def _window(rows, cols, at):
    return pl.BlockSpec((pl.Element(rows), pl.Element(cols)), at)


def _gate_windows(tm, tn, D):
    return [_window(tm, tn, lambda i, j: (i * tm, pl.multiple_of(OFF_GA + j * tn, LANES))),
            _window(tm, tn, lambda i, j: (i * tm, pl.multiple_of(OFF_GA + D + j * tn, LANES)))]


def _branch_proj_merge(yh, wbh, proj, ua):
    T, K = yh.shape
    D = wbh.shape[1]
    tm, tn = _pick(T, (512, 256, 128)), _pick(D, (1024, 512, 256))

    def body(a_ref, b_ref, ga_ref, gh_ref, ua_ref, uh_ref, m_ref):
        uh = jnp.dot(a_ref[...], b_ref[...], preferred_element_type=F32).astype(BF16)
        uh_ref[...] = uh
        m_ref[...] = (jax.nn.sigmoid(ga_ref[...]) * ua_ref[...].astype(F32)
                      + jax.nn.sigmoid(gh_ref[...]) * uh.astype(F32)).astype(BF16)

    blk = pl.BlockSpec((tm, tn), lambda i, j: (i, j))
    o = jax.ShapeDtypeStruct((T, D), BF16)
    return pl.pallas_call(
        body, name="branch_hgrn_proj_merge", grid=(T // tm, D // tn),
        in_specs=[pl.BlockSpec((tm, K), lambda i, j: (i, 0)), pl.BlockSpec((K, tn), lambda i, j: (0, j))]
        + _gate_windows(tm, tn, D) + [blk],
        out_specs=[blk, blk], out_shape=[o, o],
        compiler_params=_params(("parallel", "arbitrary")),
    )(yh, wbh, proj, proj, ua)


def _out_dgrad_merge_bwd(dy, wout, proj, ua, uh):
    T, K = dy.shape
    D = wout.shape[0]
    tm, tn = _pick(T, (512, 256, 128)), _pick(D, (1024, 512, 256))
    nj = D // tn

    def body(a_ref, b_ref, ga_ref, gh_ref, ua_ref, uh_ref, dua_ref, duh_ref, dproj_ref):
        d = lax.dot_general(a_ref[...], b_ref[...], NT_DIMS, preferred_element_type=F32)
        sa = jax.nn.sigmoid(ga_ref[...])
        sh = jax.nn.sigmoid(gh_ref[...])
        dua_ref[...] = (d * sa).astype(BF16)
        duh_ref[...] = (d * sh).astype(BF16)
        dga = (d * ua_ref[...].astype(F32) * sa * (1.0 - sa)).astype(BF16)
        dgh = (d * uh_ref[...].astype(F32) * sh * (1.0 - sh)).astype(BF16)
        for jj in range(nj):
            @pl.when(pl.program_id(1) == jj)
            def _():
                dproj_ref[:, jj * tn:(jj + 1) * tn] = dga
                dproj_ref[:, D + jj * tn:D + (jj + 1) * tn] = dgh

    blk = pl.BlockSpec((tm, tn), lambda i, j: (i, j))
    o = jax.ShapeDtypeStruct((T, D), BF16)
    return pl.pallas_call(
        body, name="out_proj_dgrad_merge_bwd", grid=(T // tm, nj),
        in_specs=[pl.BlockSpec((tm, K), lambda i, j: (i, 0)), pl.BlockSpec((tn, K), lambda i, j: (j, 0))]
        + _gate_windows(tm, tn, D) + [blk, blk],
        out_specs=[blk, blk, _window(tm, 2 * D, lambda i, j: (i * tm, OFF_GA))],
        out_shape=[o, o, jax.ShapeDtypeStruct((T, proj.shape[1]), BF16)],
        compiler_params=_params(("parallel", "arbitrary")),
    )(dy, wout, proj, proj, ua, uh)


KV_PAIR = 2
PAIR_HEADS = KV_PAIR * GROUP


def _attn_mask(n):
    qi = lax.broadcasted_iota(jnp.int32, (WINDOW, 2 * WINDOW), 0)
    si = lax.broadcasted_iota(jnp.int32, (WINDOW, 2 * WINDOW), 1)
    dist = qi + WINDOW - si
    return (dist >= 0) & (dist < WINDOW) & ((si >= WINDOW) | (n > 0))


def _attn_group_fn(mask):
    mask = jnp.concatenate([mask] * GROUP, axis=0)

    def f(q, k, v, ag, bias, sink):
        s = lax.dot_general(q.astype(BF16), k.astype(BF16), NT_DIMS, preferred_element_type=F32) * (HEAD_DIM ** -0.5)
        s = jnp.where(mask, s + bias, -1e30)
        m = lax.stop_gradient(jnp.maximum(jnp.max(s, axis=-1, keepdims=True), sink))
        p = jnp.exp(s - m)
        den = jnp.sum(p, axis=-1, keepdims=True) + jnp.exp(sink - m)
        o = jnp.dot(p.astype(BF16), v.astype(BF16), preferred_element_type=F32)
        return o * (jax.nn.silu(ag) / den)

    return f


def _attn_specs(B_loc, nb, order):
    qw = PAIR_HEADS * HEAD_DIM
    kw = KV_PAIR * HEAD_DIM

    def rows(g):
        b, p, n = order(*g)
        return b * nb + n

    def prev(g):
        b, p, n = order(*g)
        return b * nb + jnp.maximum(n - 1, 0)

    def pp(g):
        return order(*g)[1]

    q = pl.BlockSpec((WINDOW, qw), lambda *g: (rows(g), OFF_AQ // qw + pp(g)))
    kc = pl.BlockSpec((WINDOW, kw), lambda *g: (rows(g), OFF_AK // kw + pp(g)))
    kp = pl.BlockSpec((WINDOW, kw), lambda *g: (prev(g), OFF_AK // kw + pp(g)))
    vc = pl.BlockSpec((WINDOW, kw), lambda *g: (rows(g), OFF_AV // kw + pp(g)))
    vp = pl.BlockSpec((WINDOW, kw), lambda *g: (prev(g), OFF_AV // kw + pp(g)))
    ag = pl.BlockSpec((WINDOW, qw), lambda *g: (rows(g), OFF_AG // qw + pp(g)))
    bias = pl.BlockSpec((PAIR_HEADS, WINDOW, 2 * WINDOW), lambda *g: (pp(g), 0, 0))
    sink = pl.BlockSpec((PAIR_HEADS, 8, LANES), lambda *g: (pp(g), 0, 0))
    return [q, kc, kp, vc, vp, ag, bias, sink], rows, pp


def _attn_operands(q_ref, kc_ref, kp_ref, vc_ref, vp_ref, ag_ref, bias_ref, sink_ref, j):
    lo, hi = j * HEAD_DIM, (j + 1) * HEAD_DIM
    k = jnp.concatenate([kp_ref[:, lo:hi], kc_ref[:, lo:hi]], axis=0)
    v = jnp.concatenate([vp_ref[:, lo:hi], vc_ref[:, lo:hi]], axis=0)
    heads = [j * GROUP + g for g in range(GROUP)]
    q = jnp.concatenate([q_ref[:, h * HEAD_DIM:(h + 1) * HEAD_DIM] for h in heads], axis=0)
    ag = jnp.concatenate([ag_ref[:, h * HEAD_DIM:(h + 1) * HEAD_DIM] for h in heads], axis=0)
    bias = jnp.concatenate([bias_ref[h] for h in heads], axis=0)
    sink = jnp.concatenate([jnp.broadcast_to(sink_ref[h, 0:1, 0:1], (WINDOW, 1)) for h in heads], axis=0)
    return q, k, v, ag, bias, sink


def _attn_fwd(proj, bias_tab, sinks_b, B_loc, S):
    T = B_loc * S
    nb = S // WINDOW
    n_pairs = ATTN_KV_HEADS // KV_PAIR
    in_specs, rows, pp = _attn_specs(B_loc, nb, lambda b, p, n: (b, p, n))

    def body(q_ref, kc_ref, kp_ref, vc_ref, vp_ref, ag_ref, bias_ref, sink_ref, ya_ref):
        f = _attn_group_fn(_attn_mask(pl.program_id(2)))
        for j in range(KV_PAIR):
            out = f(*_attn_operands(q_ref, kc_ref, kp_ref, vc_ref, vp_ref, ag_ref, bias_ref, sink_ref, j))
            for g in range(GROUP):
                h = j * GROUP + g
                ya_ref[:, h * HEAD_DIM:(h + 1) * HEAD_DIM] = out[g * WINDOW:(g + 1) * WINDOW].astype(BF16)

    return pl.pallas_call(
        body, name="attn_fwd", grid=(B_loc, n_pairs, nb), in_specs=in_specs,
        out_specs=pl.BlockSpec((WINDOW, PAIR_HEADS * HEAD_DIM), lambda *g: (rows(g), pp(g))),
        out_shape=jax.ShapeDtypeStruct((T, ATTN_WIDTH), BF16),
        compiler_params=_params(("parallel", "parallel", "parallel")),
    )(proj, proj, proj, proj, proj, proj, bias_tab, sinks_b)


def _attn_bwd(proj, bias_tab, sinks_b, dya, dproj, B_loc, S):
    nb = S // WINDOW
    n_pairs = ATTN_KV_HEADS // KV_PAIR
    qw, kw = PAIR_HEADS * HEAD_DIM, KV_PAIR * HEAD_DIM

    def at(off, width, back=0):
        return lambda b, i, p: ((b * nb + jnp.maximum(nb - 1 - i - back, 0)) * WINDOW,
                                pl.multiple_of(off + p * width, LANES))

    in_specs = [_window(WINDOW, qw, at(OFF_AQ, qw)), _window(WINDOW, kw, at(OFF_AK, kw)),
                _window(WINDOW, kw, at(OFF_AK, kw, 1)), _window(WINDOW, kw, at(OFF_AV, kw)),
                _window(WINDOW, kw, at(OFF_AV, kw, 1)), _window(WINDOW, qw, at(OFF_AG, qw)),
                pl.BlockSpec((PAIR_HEADS, WINDOW, 2 * WINDOW), lambda b, i, p: (p, 0, 0)),
                pl.BlockSpec((PAIR_HEADS, 8, LANES), lambda b, i, p: (p, 0, 0)),
                pl.BlockSpec((WINDOW, qw), lambda b, i, p: (b * nb + nb - 1 - i, p)), ANY]

    def body(q_ref, kc_ref, kp_ref, vc_ref, vp_ref, ag_ref, bias_ref, sink_ref, dya_ref, dproj_in,
             dproj_ref, dbias_ref, dsink_ref, dkc_ref, dvc_ref):
        b, i, p = pl.program_id(0), pl.program_id(1), pl.program_id(2)
        n = nb - 1 - i

        @pl.when((b == 0) & (i == 0) & (p == 0))
        def _():
            dbias_ref[...] = jnp.zeros_like(dbias_ref)
            dsink_ref[...] = jnp.zeros_like(dsink_ref)

        @pl.when(i == 0)
        def _():
            dkc_ref[p] = jnp.zeros((WINDOW, kw), F32)
            dvc_ref[p] = jnp.zeros((WINDOW, kw), F32)

        f = _attn_group_fn(_attn_mask(n))
        dk_carry, dv_carry = dkc_ref[p], dvc_ref[p]
        dqs, dags, dbiases, dsinks, dks, dvs = [], [], [], [], [], []
        for j in range(KV_PAIR):
            ops = _attn_operands(q_ref, kc_ref, kp_ref, vc_ref, vp_ref, ag_ref, bias_ref, sink_ref, j)
            _, vjp = jax.vjp(f, *ops)
            dout = jnp.concatenate([dya_ref[:, (j * GROUP + g) * HEAD_DIM:(j * GROUP + g + 1) * HEAD_DIM].astype(F32)
                                    for g in range(GROUP)], axis=0)
            dq, dk, dv, dag, dbias, dsink = vjp(dout)
            lo, hi = j * HEAD_DIM, (j + 1) * HEAD_DIM
            dks.append((dk[WINDOW:] + dk_carry[:, lo:hi]).astype(BF16))
            dvs.append((dv[WINDOW:] + dv_carry[:, lo:hi]).astype(BF16))
            dkc_ref[p, :, lo:hi] = dk[:WINDOW]
            dvc_ref[p, :, lo:hi] = dv[:WINDOW]
            for g in range(GROUP):
                blk = slice(g * WINDOW, (g + 1) * WINDOW)
                dqs.append(dq[blk].astype(BF16))
                dags.append(dag[blk].astype(BF16))
                dbiases.append(dbias[blk])
                dsinks.append(jnp.broadcast_to(jnp.sum(dsink[blk], axis=0, keepdims=True), (8, LANES)))

        for pair in range(n_pairs):
            @pl.when(p == pair)
            def _():
                for j in range(KV_PAIR):
                    col = (pair * KV_PAIR + j) * HEAD_DIM
                    dproj_ref[:, OFF_AK + col:OFF_AK + col + HEAD_DIM] = dks[j]
                    dproj_ref[:, OFF_AV + col:OFF_AV + col + HEAD_DIM] = dvs[j]
                for hh in range(PAIR_HEADS):
                    h = pair * PAIR_HEADS + hh
                    dproj_ref[:, OFF_AQ + h * HEAD_DIM:OFF_AQ + (h + 1) * HEAD_DIM] = dqs[hh]
                    dproj_ref[:, OFF_AG + h * HEAD_DIM:OFF_AG + (h + 1) * HEAD_DIM] = dags[hh]
                    dbias_ref[h] += dbiases[hh]
                    dsink_ref[h] += dsinks[hh]

    return pl.pallas_call(
        body, name="attn_bwd", grid=(B_loc, nb, n_pairs), in_specs=in_specs,
        out_specs=[_window(WINDOW, OFF_HQ, lambda b, i, p: ((b * nb + nb - 1 - i) * WINDOW, 0)),
                   pl.BlockSpec((ATTN_HEADS, WINDOW, 2 * WINDOW), lambda b, i, p: (0, 0, 0)),
                   pl.BlockSpec((ATTN_HEADS, 8, LANES), lambda b, i, p: (0, 0, 0))],
        out_shape=[jax.ShapeDtypeStruct(dproj.shape, dproj.dtype),
                   jax.ShapeDtypeStruct((ATTN_HEADS, WINDOW, 2 * WINDOW), F32),
                   jax.ShapeDtypeStruct((ATTN_HEADS, 8, LANES), F32)],
        input_output_aliases={9: 0},
        scratch_shapes=[pltpu.VMEM((n_pairs, WINDOW, kw), F32), pltpu.VMEM((n_pairs, WINDOW, kw), F32)],
        compiler_params=_params(("arbitrary", "arbitrary", "arbitrary")),
    )(proj, proj, proj, proj, proj, proj, bias_tab, sinks_b, dya, dproj)


class _HgrnPre:
    def __init__(self, fr, qr, lb, g_scr):
        t = lax.broadcasted_iota(jnp.int32, (CHUNK, CHUNK), 0)
        s = lax.broadcasted_iota(jnp.int32, (CHUNK, CHUNK), 1)
        self.sg = jax.nn.sigmoid(fr)
        self.f = lb + (1.0 - lb) * self.sg
        g = jnp.dot((t >= s).astype(F32), jnp.log(self.f), precision=lax.Precision.HIGHEST, preferred_element_type=F32)
        g_scr[...] = g
        self.g = g
        self.row = lax.broadcasted_iota(jnp.int32, g.shape, 0)
        self.refs = [jnp.zeros((1, g.shape[1]), F32)] + [g_scr[pl.ds(i * SUB - 1, 1), :] for i in range(1, NSUB)]
        self.gend = g_scr[pl.ds(CHUNK - 1, 1), :]
        refrow = jnp.zeros_like(g)
        for i in range(1, NSUB):
            refrow = jnp.where(self.row >= i * SUB, self.refs[i], refrow)
        self.sigq = jax.nn.sigmoid(qr)
        self.qs = qr * self.sigq
        self.k = 1.0 - self.f
        self.eg = jnp.exp(g)
        self.eqd = jnp.exp(g - refrow)
        self.ekd = [jnp.exp(jnp.where(self.row < (i + 1) * SUB, self.refs[i] - g, 0.0)) for i in range(NSUB)]
        self.ekdec = jnp.exp(self.gend - g)
        self.qg = self.qs * self.eg
        self.qd = self.qs * self.eqd
        self.kd = [self.k * e for e in self.ekd]
        self.kdec = self.k * self.ekdec
        self.egend = jnp.exp(self.gend)


def _hgrn_pair_mask():
    t = lax.broadcasted_iota(jnp.int32, (CHUNK, NSUB * CHUNK), 0)
    col = lax.broadcasted_iota(jnp.int32, (CHUNK, NSUB * CHUNK), 1)
    return ((t // SUB) == (col // CHUNK)) & ((col % CHUNK) <= t)


def _hgrn_head_out(p, lanes, state_t, v, mask):
    qg, qd = p.qg[:, lanes], p.qd[:, lanes]
    kall = jnp.concatenate([kd[:, lanes] for kd in p.kd], axis=0)
    vst = jnp.concatenate([v] * NSUB, axis=0)
    am = jnp.where(mask, _bdot(qd, kall, NT_DIMS), 0.0)
    o = _bdot(qg, state_t, NT_DIMS) + _bdot(am, vst)
    return o, (qg, qd, kall, am, vst)


def _hgrn_fwd(proj, lb, gain, B_loc, S):
    T = B_loc * S
    nc = S // CHUNK
    nh = HGRN_HEADS

    def at(off):
        return lambda b, n: ((b * nc + n) * CHUNK, off)

    vec = pl.BlockSpec((1, HGRN_WIDTH), lambda b, n: (0, 0))

    def body(q_ref, f_ref, v_ref, hg_ref, lb_ref, gain_ref, yh_ref, st_ref, state_scr, g_scr):
        @pl.when(pl.program_id(1) == 0)
        def _():
            state_scr[...] = jnp.zeros_like(state_scr)

        p = _HgrnPre(f_ref[...], q_ref[...], lb_ref[...], g_scr)
        v = v_ref[...]
        gate = gain_ref[...] * jax.nn.silu(hg_ref[...])
        mask = _hgrn_pair_mask()
        for hd in range(nh):
            lanes = slice(hd * HGRN_DIM, (hd + 1) * HGRN_DIM)
            st = state_scr[hd]
            st_ref[hd] = st
            o, _ = _hgrn_head_out(p, lanes, st, v[:, lanes], mask)
            rs = lax.rsqrt(jnp.mean(o * o, axis=-1, keepdims=True) + NORM_EPS)
            yh_ref[:, lanes] = (o * rs * gate[:, lanes]).astype(BF16)
            state_scr[hd] = st * p.egend[:, lanes] + _bdot(v[:, lanes], p.kdec[:, lanes], TN_DIMS)

    return pl.pallas_call(
        body, name="hgrn_fwd", grid=(B_loc, nc),
        in_specs=[_window(CHUNK, HGRN_WIDTH, at(OFF_HQ)), _window(CHUNK, HGRN_WIDTH, at(OFF_HF)),
                  _window(CHUNK, HGRN_WIDTH, at(OFF_HI)), _window(CHUNK, HGRN_WIDTH, at(OFF_HG)), vec, vec],
        out_specs=[pl.BlockSpec((CHUNK, HGRN_WIDTH), lambda b, n: (b * nc + n, 0)),
                   pl.BlockSpec((None, None, nh, HGRN_DIM, HGRN_DIM), lambda b, n: (b, n, 0, 0, 0))],
        out_shape=[jax.ShapeDtypeStruct((T, HGRN_WIDTH), BF16),
                   jax.ShapeDtypeStruct((B_loc, nc, nh, HGRN_DIM, HGRN_DIM), F32)],
        scratch_shapes=[pltpu.VMEM((nh, HGRN_DIM, HGRN_DIM), F32), pltpu.VMEM((CHUNK, HGRN_WIDTH), F32)],
        compiler_params=_params(("parallel", "arbitrary")),
    )(proj, proj, proj, proj, lb, gain)


def _hgrn_bwd(proj, states, lb, gain, dyh, dproj, B_loc, S):
    nc = S // CHUNK
    nh = HGRN_HEADS

    def at(off):
        return lambda b, i: ((b * nc + nc - 1 - i) * CHUNK, off)

    vec = pl.BlockSpec((1, HGRN_WIDTH), lambda b, i: (0, 0))
    in_specs = [_window(CHUNK, HGRN_WIDTH, at(OFF_HQ)), _window(CHUNK, HGRN_WIDTH, at(OFF_HF)),
                _window(CHUNK, HGRN_WIDTH, at(OFF_HI)), _window(CHUNK, HGRN_WIDTH, at(OFF_HG)), vec, vec,
                pl.BlockSpec((None, None, nh, HGRN_DIM, HGRN_DIM), lambda b, i: (b, nc - 1 - i, 0, 0, 0)),
                pl.BlockSpec((CHUNK, HGRN_WIDTH), lambda b, i: (b * nc + nc - 1 - i, 0)), ANY]
    acc_spec = pl.BlockSpec((None, 1, HGRN_WIDTH), lambda b, i: (b, 0, 0))

    def body(q_ref, f_ref, v_ref, hg_ref, lb_ref, gain_ref, st_ref, dyh_ref, dproj_in,
             dproj_ref, dlb_ref, dgain_ref, dstate_scr, g_scr, dg_scr):
        dq_ref, df_ref, dv_ref, dhg_ref = [dproj_ref.at[:, pl.ds(k * HGRN_WIDTH, HGRN_WIDTH)] for k in range(4)]

        @pl.when(pl.program_id(1) == 0)
        def _():
            dstate_scr[...] = jnp.zeros_like(dstate_scr)
            dlb_ref[...] = jnp.zeros_like(dlb_ref)
            dgain_ref[...] = jnp.zeros_like(dgain_ref)

        qr, lb, gain, hg, v = q_ref[...], lb_ref[...], gain_ref[...], hg_ref[...], v_ref[...]
        p = _HgrnPre(f_ref[...], qr, lb, g_scr)
        sgh = jax.nn.sigmoid(hg)
        sil = hg * sgh
        dy = dyh_ref[...].astype(F32)
        mask = _hgrn_pair_mask()
        dqg, dqd, dkdec, dv, dhg, dgend, dgain = [], [], [], [], [], [], []
        dkd = [[] for _ in range(NSUB)]
        heads = [slice(hd * HGRN_DIM, (hd + 1) * HGRN_DIM) for hd in range(nh)]
        sts = [st_ref[hd] for hd in range(nh)]
        dnews = [dstate_scr[hd] for hd in range(nh)]
        fwd = [_hgrn_head_out(p, lanes, st, v[:, lanes], mask) for lanes, st in zip(heads, sts)]
        for lanes, st, dnew in zip(heads, sts, dnews):
            dkdec_h = _bdot(v[:, lanes], dnew)
            dkdec.append(dkdec_h)
            dgend.append(jnp.sum(dkdec_h * p.kdec[:, lanes], axis=0, keepdims=True)
                         + jnp.sum(dnew * st, axis=0, keepdims=True) * p.egend[:, lanes])
        dos = []
        gate_grad = sgh * (1.0 + hg * (1.0 - sgh))
        for lanes, (o, _) in zip(heads, fwd):
            rs = lax.rsqrt(jnp.mean(o * o, axis=-1, keepdims=True) + NORM_EPS)
            n = o * rs
            dyn = dy[:, lanes] * n
            dgain.append(jnp.sum(dyn * sil[:, lanes], axis=0, keepdims=True))
            dhg.append(dyn * gain[:, lanes] * gate_grad[:, lanes])
            dn = dy[:, lanes] * gain[:, lanes] * sil[:, lanes]
            dos.append(rs * (dn - n * jnp.mean(dn * n, axis=-1, keepdims=True)))
        drs = []
        for hd, (lanes, st, dnew, do, (_, (qg, qd, kall, am, vst))) in enumerate(zip(heads, sts, dnews, dos, fwd)):
            dqg.append(_bdot(do, st))
            dstate_scr[hd] = _bdot(do, qg, TN_DIMS) + dnew * p.egend[:, lanes]
            drs.append(jnp.where(mask, _bdot(do, vst, NT_DIMS), 0.0))
            dvst = _bdot(am, do, TN_DIMS)
            dv.append(sum(dvst[i * CHUNK:(i + 1) * CHUNK] for i in range(NSUB)) + _bdot(p.kdec[:, lanes], dnew, NT_DIMS))
        for dr, (_, (qg, qd, kall, am, vst)) in zip(drs, fwd):
            dqd.append(_bdot(dr, kall))
            dkall = _bdot(dr, qd, TN_DIMS)
            for i in range(NSUB):
                dkd[i].append(dkall[i * CHUNK:(i + 1) * CHUNK])

        wide = lambda parts: jnp.concatenate(parts, axis=1)
        dqg, dqd, dkdec = wide(dqg), wide(dqd), wide(dkdec)
        t2 = dqd * p.qd
        dg = dqg * p.qg + t2 - dkdec * p.kdec
        dk = dkdec * p.ekdec
        dg_scr[...] = jnp.zeros_like(dg_scr)
        for i in range(NSUB):
            dkd_i = wide(dkd[i])
            tk = jnp.where(p.row < (i + 1) * SUB, dkd_i * p.kd[i], 0.0)
            dg = dg - tk
            dk = dk + dkd_i * p.ekd[i]
            if i >= 1:
                in_blk = (p.row >= i * SUB) & (p.row < (i + 1) * SUB)
                dg_scr[pl.ds(i * SUB - 1, 1), :] = (jnp.sum(tk, axis=0, keepdims=True)
                                                    - jnp.sum(jnp.where(in_blk, t2, 0.0), axis=0, keepdims=True))
        dg_scr[pl.ds(CHUNK - 1, 1), :] = wide(dgend)
        t = lax.broadcasted_iota(jnp.int32, (CHUNK, CHUNK), 0)
        s = lax.broadcasted_iota(jnp.int32, (CHUNK, CHUNK), 1)
        dlogf = jnp.dot((t <= s).astype(F32), dg + dg_scr[...], precision=lax.Precision.HIGHEST, preferred_element_type=F32)
        df = dlogf / p.f - dk
        df_ref[...] = (df * (1.0 - lb) * p.sg * (1.0 - p.sg)).astype(BF16)
        dlb_ref[...] += jnp.sum(df * (1.0 - p.sg), axis=0, keepdims=True)
        dq_ref[...] = ((dqg * p.eg + dqd * p.eqd) * p.sigq * (1.0 + qr * (1.0 - p.sigq))).astype(BF16)
        dv_ref[...] = wide(dv).astype(BF16)
        dhg_ref[...] = wide(dhg).astype(BF16)
        dgain_ref[...] += wide(dgain)

    acc = jax.ShapeDtypeStruct((B_loc, 1, HGRN_WIDTH), F32)
    return pl.pallas_call(
        body, name="hgrn_bwd", grid=(B_loc, nc), in_specs=in_specs,
        out_specs=[_window(CHUNK, 4 * HGRN_WIDTH, at(OFF_HQ)), acc_spec, acc_spec],
        out_shape=[jax.ShapeDtypeStruct(dproj.shape, dproj.dtype), acc, acc],
        input_output_aliases={8: 0},
        scratch_shapes=[pltpu.VMEM((nh, HGRN_DIM, HGRN_DIM), F32), pltpu.VMEM((CHUNK, HGRN_WIDTH), F32),
                        pltpu.VMEM((CHUNK, HGRN_WIDTH), F32)],
        compiler_params=_params(("parallel", "arbitrary")),
    )(proj, proj, proj, proj, lb, gain, states, dyh, dproj)


def _adamw(w, g, m, v, name):
    R, C = w.shape
    tr = _pick(R, (128, 64, 32, 16, 8)) if C > 1024 else _pick(R, (512, 256, 128, 64, 32, 16, 8))

    def body(w_ref, g_ref, m_ref, v_ref, d_ref, nm_ref, nv_ref, g_out_ref):
        gv = g_ref[...]
        g_out_ref[...] = gv
        nm = ADAM_B1 * m_ref[...] + (1.0 - ADAM_B1) * gv
        nv = ADAM_B2 * v_ref[...] + (1.0 - ADAM_B2) * (gv * gv)
        m_hat = nm / (1.0 - ADAM_B1 ** ADAM_STEP)
        v_hat = nv / (1.0 - ADAM_B2 ** ADAM_STEP)
        d_ref[...] = -ADAM_LR * (m_hat / (jnp.sqrt(v_hat) + ADAM_EPS) + ADAM_WD * w_ref[...])
        nm_ref[...] = nm
        nv_ref[...] = nv

    blk = pl.BlockSpec((tr, C), lambda i: (i, 0))
    o = jax.ShapeDtypeStruct((R, C), F32)
    return pl.pallas_call(
        body, name=name, grid=(R // tr,), in_specs=[blk] * 4, out_specs=[blk] * 4, out_shape=[o, o, o, o],
        compiler_params=_params(("parallel",)),
    )(w, g, m, v)


ANY = pl.BlockSpec(memory_space=pl.ANY)
VMEM_SPEC = pl.BlockSpec(memory_space=pltpu.VMEM)


def _place():
    x, y, c = lax.axis_index("x"), lax.axis_index("y"), lax.axis_index("c")
    other_chips = [(1 - x, y), (x, 1 - y), (1 - x, 1 - y)]
    return x, y, c, other_chips


def _cast_into_full(w, ax, s_arr, name):
    R, C = w.shape
    tr = _pick(R, (256, 128))
    nr = R // tr

    def body(s_ref, w_ref, o_ref):
        o_ref[...] = w_ref[...].astype(BF16)

    if ax == 1:
        shape, o_map = (R, N_CHIPS * C), lambda i, s: (i, s[0])
    else:
        shape, o_map = (N_CHIPS * R, C), lambda i, s: (s[0] * nr + i, 0)
    return pl.pallas_call(
        body, name=name,
        grid_spec=pltpu.PrefetchScalarGridSpec(
            num_scalar_prefetch=1, grid=(nr,), in_specs=[pl.BlockSpec((tr, C), lambda i, s: (i, 0))],
            out_specs=pl.BlockSpec((tr, C), o_map)),
        out_shape=jax.ShapeDtypeStruct(shape, BF16),
        compiler_params=_params(("parallel",)),
    )(s_arr, w)


class _Gather:
    def __init__(self, fulls, shard_shapes, axes, tag):
        self.shapes, self.axes, self.tag, self.nw = shard_shapes, axes, tag, len(fulls)
        self.fulls, self.sems, self.token = list(fulls), {}, None

    def start(self, peers, after=None):
        nw, np_ = self.nw, len(peers)

        def body(*refs):
            ins, sems = refs[:nw], refs[nw + (after is not None):nw + (after is not None) + 2 * np_]
            for k, j in enumerate(peers):
                for cp in self._peer_copies(ins, sems[2 * k], sems[2 * k + 1], j):
                    cp.start()
            refs[-1][...] = jnp.zeros_like(refs[-1])

        out = pl.pallas_call(
            body, name="gather_start_%s_%s" % (self.tag, "".join(map(str, peers))),
            out_shape=(*[pltpu.SemaphoreType.DMA((nw,))] * (2 * np_),
                       *[pltpu.HBM(f.shape, f.dtype) for f in self.fulls], jax.ShapeDtypeStruct((8, LANES), F32)),
            in_specs=[HBM_SPEC] * nw + ([] if after is None else [ANY]),
            out_specs=(*[SEM_SPEC] * (2 * np_), *[HBM_SPEC] * nw, VMEM_SPEC),
            input_output_aliases={k: 2 * np_ + k for k in range(nw)},
            compiler_params=pltpu.CompilerParams(has_side_effects=DATAFLOW),
        )(*[pltpu.with_memory_space_constraint(f, pltpu.HBM) for f in self.fulls], *(() if after is None else (after,)))
        for k, j in enumerate(peers):
            self.sems[j] = (out[2 * k], out[2 * k + 1])
        self.fulls = list(out[2 * np_:2 * np_ + nw])
        self.token = out[-1]

    def _region(self, ref, i, t, half):
        R, C = self.shapes[i]
        hr = R // 2
        if self.axes[i] == 1:
            return ref.at[pl.ds(half * hr, hr), pl.ds(pl.multiple_of(t * C, LANES), C)]
        return ref.at[pl.ds(t * R + half * hr, hr), :]

    def _peer_copies(self, refs, send_sems, recv_sems, j):
        x, y, c, chips = _place()
        s = 2 * x + y
        return [pltpu.make_async_remote_copy(
            src_ref=self._region(refs[i], i, s, c), dst_ref=self._region(refs[i], i, s, c), send_sem=send_sems.at[i],
            recv_sem=recv_sems.at[i], device_id=(*chips[j], c), device_id_type=MESH) for i in range(self.nw)]

    def wait(self, peers, after):
        nw, np_ = self.nw, len(peers)

        def body(*refs):
            ins, sems = refs[:nw], refs[nw:nw + 2 * np_]
            for k, j in enumerate(peers):
                for cp in self._peer_copies(ins, sems[2 * k], sems[2 * k + 1], j):
                    cp.wait_send()
                    cp.wait_recv()

        sem_args = [s for j in peers for s in self.sems[j]]
        out = pl.pallas_call(
            body, name="gather_wait_%s_%s" % (self.tag, "".join(map(str, peers))),
            out_shape=tuple(pltpu.HBM(f.shape, f.dtype) for f in self.fulls),
            in_specs=[HBM_SPEC] * nw + [SEM_SPEC] * (2 * np_) + [ANY], out_specs=tuple([HBM_SPEC] * nw),
            input_output_aliases={k: k for k in range(nw)},
            compiler_params=pltpu.CompilerParams(has_side_effects=DATAFLOW),
        )(*self.fulls, *sem_args, after)
        self.fulls = list(out)

    def forward(self, peers):
        nw, np_ = self.nw, len(peers)

        def body(*refs):
            ins, outs = refs[:nw], refs[nw:2 * nw]
            send_sems, recv_sems = refs[2 * nw:]
            x, y, c, chips = _place()
            cps = []
            for i in range(nw):
                for k, j in enumerate(peers):
                    t = 2 * chips[j][0] + chips[j][1]
                    cp = pltpu.make_async_remote_copy(
                        src_ref=self._region(ins[i], i, t, c), dst_ref=self._region(outs[i], i, t, c),
                        send_sem=send_sems.at[i * np_ + k], recv_sem=recv_sems.at[i * np_ + k],
                        device_id=(x, y, 1 - c), device_id_type=MESH)
                    cp.start()
                    cps.append(cp)
            for cp in cps:
                cp.wait()

        out = pl.pallas_call(
            body, name="gather_forward_%s_%s" % (self.tag, "".join(map(str, peers))),
            in_specs=[ANY] * nw, out_specs=[ANY] * nw,
            out_shape=[jax.ShapeDtypeStruct(f.shape, f.dtype) for f in self.fulls],
            input_output_aliases={i: i for i in range(nw)},
            scratch_shapes=[pltpu.SemaphoreType.DMA((nw * np_,)), pltpu.SemaphoreType.DMA((nw * np_,))],
        )(*self.fulls)
        self.fulls = list(out)


def _matmul_slab(a, wfull, slab_arr, prev, after, name):
    M, K = a.shape
    N = wfull.shape[1]
    nslab = N // N_CHIPS
    tn = _pick(nslab, (2688, 896, 512, 384, 256, 128))
    tm = _pick(M, (512, 256, 128) if tn > 1024 else (1024, 512, 256, 128))
    per = nslab // tn
    extra = [e for e in (prev, after) if e is not None]

    def body(slab_ref, a_ref, b_ref, *rest):
        rest[len(extra)][...] = jnp.dot(a_ref[...], b_ref[...], preferred_element_type=F32)

    return pl.pallas_call(
        body, name=name,
        grid_spec=pltpu.PrefetchScalarGridSpec(
            num_scalar_prefetch=1, grid=(M // tm, per),
            in_specs=[pl.BlockSpec((tm, K), lambda i, j, sl: (i, 0)),
                      pl.BlockSpec((K, tn), lambda i, j, sl: (0, sl[0] * per + j))] + [ANY] * len(extra),
            out_specs=pl.BlockSpec((tm, tn), lambda i, j, sl: (i, sl[0] * per + j))),
        out_shape=jax.ShapeDtypeStruct((M, N), F32),
        input_output_aliases={} if prev is None else {3: 0},
        compiler_params=_params(("parallel", "arbitrary")),
    )(slab_arr, a, wfull, *extra)


def _exchange_sibling_halves(gs, name):
    nw = len(gs)

    def body(*refs):
        ins, outs = refs[:nw], refs[nw:2 * nw]
        send_sems, recv_sems = refs[2 * nw:]
        x, y, c, _ = _place()
        cps = []
        for i in range(nw):
            cp = pltpu.make_async_remote_copy(src_ref=ins[i].at[:, 1 - c], dst_ref=outs[i], send_sem=send_sems.at[i],
                                              recv_sem=recv_sems.at[i], device_id=(x, y, 1 - c), device_id_type=MESH)
            cp.start()
            cps.append(cp)
        for cp in cps:
            cp.wait()

    return pl.pallas_call(
        body, name=name, in_specs=[ANY] * nw, out_specs=[ANY] * nw,
        out_shape=[jax.ShapeDtypeStruct((g.shape[0],) + g.shape[2:], g.dtype) for g in gs],
        scratch_shapes=[pltpu.SemaphoreType.DMA((nw,)), pltpu.SemaphoreType.DMA((nw,))],
    )(*gs)


HBM_SPEC = pl.BlockSpec(memory_space=pltpu.HBM)
SEM_SPEC = pl.BlockSpec(memory_space=pltpu.SEMAPHORE)
DATAFLOW = pltpu.SideEffectType.DATAFLOW_SIDE_EFFECTING


def _chip_copies(ins, lands, send_sems, recv_sems):
    x, y, c, chips = _place()
    return [pltpu.make_async_remote_copy(
        src_ref=ins[i].at[2 * chip[0] + chip[1]], dst_ref=lands[i].at[j], send_sem=send_sems.at[i * 3 + j],
        recv_sem=recv_sems.at[i * 3 + j], device_id=(*chip, c), device_id_type=MESH)
        for i in range(len(ins)) for j, chip in enumerate(chips)]


def _chips_send_start(ss, name):
    nw = len(ss)
    lands = [pltpu.with_memory_space_constraint(lax.empty((N_CHIPS - 1,) + s.shape[1:], s.dtype), pltpu.HBM) for s in ss]

    def body(*refs):
        ins, land_refs = refs[:nw], refs[nw:2 * nw]
        send_sems, recv_sems = refs[2 * nw], refs[2 * nw + 1]
        token = refs[-1]
        for cp in _chip_copies(ins, land_refs, send_sems, recv_sems):
            cp.start()
        token[...] = jnp.zeros_like(token)

    n = 3 * nw
    out = pl.pallas_call(
        body, name=name,
        out_shape=(pltpu.SemaphoreType.DMA((n,)), pltpu.SemaphoreType.DMA((n,)),
                   *[pltpu.HBM(s.shape, s.dtype) for s in ss], *[pltpu.HBM(l.shape, l.dtype) for l in lands],
                   jax.ShapeDtypeStruct((8, LANES), F32)),
        in_specs=[HBM_SPEC] * (2 * nw), out_specs=(SEM_SPEC, SEM_SPEC, *[HBM_SPEC] * (2 * nw), VMEM_SPEC),
        input_output_aliases={k: 2 + k for k in range(2 * nw)},
        compiler_params=pltpu.CompilerParams(has_side_effects=DATAFLOW),
    )(*[pltpu.with_memory_space_constraint(s, pltpu.HBM) for s in ss], *lands)
    return out[0], out[1], list(out[2:2 + nw]), list(out[2 + nw:2 + 2 * nw]), out[-1]


def _chips_send_wait(send_sems, recv_sems, ss, lands, after, name):
    nw = len(ss)

    def body(*refs):
        ins, land_refs = refs[:nw], refs[nw:2 * nw]
        s_sems, r_sems = refs[2 * nw], refs[2 * nw + 1]
        for cp in _chip_copies(ins, land_refs, s_sems, r_sems):
            cp.wait_send()
            cp.wait_recv()

    out = pl.pallas_call(
        body, name=name,
        out_shape=(*[pltpu.HBM(s.shape, s.dtype) for s in ss], *[pltpu.HBM(l.shape, l.dtype) for l in lands]),
        in_specs=[HBM_SPEC] * (2 * nw) + [SEM_SPEC, SEM_SPEC, ANY], out_specs=tuple([HBM_SPEC] * (2 * nw)),
        input_output_aliases={k: k for k in range(2 * nw)},
        compiler_params=pltpu.CompilerParams(has_side_effects=DATAFLOW),
    )(*ss, *lands, send_sems, recv_sems, after)
    return list(out[nw:])


def _sum_small(pack):
    rows = pack.shape[0]

    def body(pack_ref, sum_ref, all_ref, send_sems, recv_sems):
        x, y, c, _ = _place()
        me = 4 * x + 2 * y + c
        all_ref[me] = pack_ref[...]
        cps = []
        for k in range(1, N_DEV):
            to = (1 - x if k & 4 else x, 1 - y if k & 2 else y, 1 - c if k & 1 else c)
            cp = pltpu.make_async_remote_copy(
                src_ref=pack_ref, dst_ref=all_ref.at[me], send_sem=send_sems.at[k - 1],
                recv_sem=recv_sems.at[k - 1], device_id=to, device_id_type=MESH)
            cp.start()
            cps.append(cp)
        for cp in cps:
            cp.wait()
        total = all_ref[0]
        for d in range(1, N_DEV):
            total = total + all_ref[d]
        sum_ref[...] = total

    return pl.pallas_call(
        body, name="sum_small_grads", in_specs=[VMEM_SPEC], out_specs=VMEM_SPEC,
        out_shape=jax.ShapeDtypeStruct(pack.shape, F32),
        scratch_shapes=[pltpu.VMEM((N_DEV, rows, LANES), F32), pltpu.SemaphoreType.DMA((N_DEV - 1,)),
                        pltpu.SemaphoreType.DMA((N_DEV - 1,))],
    )(pack)


def _share_with_sibling(fs, name):
    nw = len(fs)

    def body(*refs):
        ins, outs = refs[:nw], refs[nw:2 * nw]
        send_sems, recv_sems = refs[2 * nw:]
        x, y, c, _ = _place()
        cps = []
        for i in range(nw):
            cp = pltpu.make_async_remote_copy(src_ref=ins[i].at[c], dst_ref=outs[i].at[c], send_sem=send_sems.at[i],
                                              recv_sem=recv_sems.at[i], device_id=(x, y, 1 - c), device_id_type=MESH)
            cp.start()
            cps.append(cp)
        for cp in cps:
            cp.wait()

    return pl.pallas_call(
        body, name=name, in_specs=[ANY] * nw, out_specs=[ANY] * nw,
        out_shape=[jax.ShapeDtypeStruct(f.shape, f.dtype) for f in fs],
        input_output_aliases={i: i for i in range(nw)},
        scratch_shapes=[pltpu.SemaphoreType.DMA((nw,)), pltpu.SemaphoreType.DMA((nw,))],
    )(*fs)


def _sum_sibling(g, land, c_arr, name):
    _, _, P, Q = g.shape
    tp = _pick(P, (256, 128, 64))

    def body(c_ref, g_ref, l_ref, s_ref):
        s_ref[...] = (g_ref[...].astype(F32) + l_ref[...].astype(F32)).astype(BF16)

    return pl.pallas_call(
        body, name=name,
        grid_spec=pltpu.PrefetchScalarGridSpec(
            num_scalar_prefetch=1, grid=(N_CHIPS, P // tp),
            in_specs=[pl.BlockSpec((None, None, tp, Q), lambda t, r, cr: (t, cr[0], r, 0)),
                      pl.BlockSpec((None, tp, Q), lambda t, r, cr: (t, r, 0))],
            out_specs=pl.BlockSpec((None, tp, Q), lambda t, r, cr: (t, r, 0))),
        out_shape=jax.ShapeDtypeStruct((N_CHIPS, P, Q), BF16),
        compiler_params=_params(("parallel", "parallel")),
    )(c_arr, g, land)


def _sum_chips(g, land, recv, sc_arr, name):
    _, _, P, Q = g.shape
    tp = _pick(P, (256, 128, 64))

    def body(sc_ref, g_ref, l_ref, r_ref, f_ref):
        acc = g_ref[...].astype(F32) + l_ref[...].astype(F32)
        for j in range(N_CHIPS - 1):
            acc = acc + r_ref[j].astype(F32)
        f_ref[...] = acc

    return pl.pallas_call(
        body, name=name,
        grid_spec=pltpu.PrefetchScalarGridSpec(
            num_scalar_prefetch=1, grid=(P // tp,),
            in_specs=[pl.BlockSpec((None, None, tp, Q), lambda r, sc: (sc[0], sc[1], r, 0)),
                      pl.BlockSpec((None, tp, Q), lambda r, sc: (sc[0], r, 0)),
                      pl.BlockSpec((N_CHIPS - 1, tp, Q), lambda r, sc: (0, r, 0))],
            out_specs=pl.BlockSpec((None, tp, Q), lambda r, sc: (sc[1], r, 0))),
        out_shape=jax.ShapeDtypeStruct((2, P, Q), F32),
        compiler_params=_params(("parallel",)),
    )(sc_arr, g, land, recv)


class _Reduction:
    def __init__(self, names, pieces, lands, flight):
        self.names, self.pieces, self.lands = names, pieces, lands
        self.send_sems, self.recv_sems, self.sums, self.zones, self.token = flight


def _reduce_start(grads, names, c_arr):
    pieces = [g.reshape(N_CHIPS, 2, -1, g.shape[-1]) for g in grads]
    tag = names[0] if len(names) == 1 else "branches"
    lands = _exchange_sibling_halves(pieces, "grads_to_sibling_" + tag)
    sums = [_sum_sibling(g, l, c_arr, "sum_sibling_" + nm) for g, l, nm in zip(pieces, lands, names)]
    return _Reduction(names, pieces, lands, _chips_send_start(sums, "grads_to_chips_start_" + tag))


def _reduce_finish(red, after, sc_arr):
    tag = red.names[0] if len(red.names) == 1 else "branches"
    recvs = _chips_send_wait(red.send_sems, red.recv_sems, red.sums, red.zones, after, "grads_to_chips_wait_" + tag)
    halves = [_sum_chips(g, l, r, sc_arr, "sum_chips_" + nm) for g, l, r, nm in zip(red.pieces, red.lands, recvs, red.names)]
    return _share_with_sibling(halves, "grads_share_sibling_" + tag)


def _t5_bucket(dist):
    max_exact = REL_BUCKETS // 2
    d = jnp.maximum(dist, 0)
    df = jnp.maximum(d, 1).astype(F32)
    large = max_exact + (jnp.log(df / max_exact) / math.log(REL_MAX_DIST / max_exact)
                         * (REL_BUCKETS - max_exact)).astype(jnp.int32)
    large = jnp.minimum(large, REL_BUCKETS - 1)
    return jnp.where(d < max_exact, d, large)


def _bucket_table():
    qi = jnp.arange(WINDOW)[:, None]
    si = jnp.arange(2 * WINDOW)[None, :]
    return _t5_bucket(qi + WINDOW - si)


TILE_WORDS = 8 * LANES


def _tile_rows(shape):
    return -(-math.prod(shape) // TILE_WORDS) * 8


def _rows_of(a):
    flat = a.reshape(-1).astype(F32)
    n = _tile_rows(a.shape) * LANES
    return jnp.pad(flat, (0, n - flat.shape[0])).reshape(-1, LANES)


def _pack_rows(parts):
    return jnp.concatenate([_rows_of(p) for p in parts], axis=0)


def _unpack_rows(packed, shapes):
    out, at = [], 0
    for shp in shapes:
        n, nr = math.prod(shp), _tile_rows(shp)
        out.append(packed[at:at + nr].reshape(-1)[:n].reshape(shp))
        at += nr
    return out


def kernel(x, norm_pre, w_in, rel_bias, attn_sinks, lb_logits, hgrn_norm, w_branch_attn, w_branch_hgrn, w_out, norm_post, loss_target, m_norm_pre, m_w_in, m_rel_bias, m_attn_sinks, m_lb_logits, m_hgrn_norm, m_w_branch_attn, m_w_branch_hgrn, m_w_out, m_norm_post, v_norm_pre, v_w_in, v_rel_bias, v_attn_sinks, v_lb_logits, v_hgrn_norm, v_w_branch_attn, v_w_branch_hgrn, v_w_out, v_norm_post):
    B_loc, S, D = x.shape
    T = B_loc * S
    x2 = x.reshape(T, D)
    tgt2 = loss_target.reshape(T, D)
    my_x, my_y, my_c = lax.axis_index("x"), lax.axis_index("y"), lax.axis_index("c")

    c_arr = jnp.reshape(my_c, (1,)).astype(jnp.int32)
    s_arr = jnp.reshape(2 * my_x + my_y, (1,)).astype(jnp.int32)
    sc_arr = jnp.concatenate([s_arr, c_arr])
    shard_ws = [w_in[0], w_branch_attn[0], w_branch_hgrn[0], w_out[0]]
    shard_axes = (1, 1, 1, 0)
    names = ["w_in", "w_branch_attn", "w_branch_hgrn", "w_out"]
    placed = [_cast_into_full(w, ax, s_arr, "cast_" + nm) for w, ax, nm in zip(shard_ws, shard_axes, names)]
    peer_slabs = [jnp.reshape(t, (1,)).astype(jnp.int32)
                  for t in (2 * (1 - my_x) + my_y, 2 * my_x + 1 - my_y, 2 * (1 - my_x) + 1 - my_y)]

    buckets = _bucket_table()
    onehot = (buckets.reshape(-1)[:, None] == jnp.arange(REL_BUCKETS)[None, :]).astype(F32)
    bias_tab = jnp.dot(onehot, rel_bias.astype(F32), precision=lax.Precision.HIGHEST).T.reshape(ATTN_HEADS, WINDOW, 2 * WINDOW)
    sinks_b = jnp.broadcast_to(attn_sinks[0].astype(F32)[:, None, None], (ATTN_HEADS, 8, LANES))
    lb_fn = lambda l: jnp.cumsum(jax.nn.softmax(l.astype(F32), axis=0), axis=0)[:1]
    lb, lb_vjp = jax.vjp(lb_fn, lb_logits)
    gain_h = hgrn_norm[0].reshape(1, HGRN_WIDTH)

    h, rstd = _rmsnorm_fwd(x2, norm_pre)
    gather_in = _Gather(placed[:1], [shard_ws[0].shape], shard_axes[:1], "w_in")
    gather_in.start([0, 1])
    proj = _matmul_slab(h, gather_in.fulls[0], s_arr, None, gather_in.token, "in_proj_own")
    for j in range(N_CHIPS - 1):
        gather_in.wait([j], proj)
        gather_in.forward([j])
        after = None
        if j == 0:
            gather_in.start([2])
            after = gather_in.token
        if j == N_CHIPS - 2:
            gather_rest = _Gather(placed[1:], [w.shape for w in shard_ws[1:]], shard_axes[1:], "rest")
            gather_rest.start([0, 1, 2], gather_in.fulls[0])
            after = gather_rest.token
        proj = _matmul_slab(h, gather_in.fulls[0], peer_slabs[j], proj, after, "in_proj_peer%d" % j)
    win_f = gather_in.fulls[0]
    ya = _attn_fwd(proj, bias_tab, sinks_b, B_loc, S)
    yh, states = _hgrn_fwd(proj, lb, gain_h, B_loc, S)
    gather_rest.wait([0, 1, 2], yh)
    gather_rest.forward([0, 1, 2])
    wba_f, wbh_f, wout_f = gather_rest.fulls
    ua = _matmul(ya, wba_f, "nn", BF16, "branch_attn_proj")
    uh, merged = _branch_proj_merge(yh, wbh_f, proj, ua)
    yv = _matmul(merged, wout_f, "nn", F32, "out_proj")
    dy, dout, loss_p, gnpost_p = _post_loss(yv, x2, tgt2, norm_post)

    g_wout = _matmul(merged, dy, "tn", BF16, "out_proj_wgrad")
    d_ua, d_uh, dproj = _out_dgrad_merge_bwd(dy, wout_f, proj, ua, uh)
    g_wba = _matmul(ya, d_ua, "tn", BF16, "branch_attn_wgrad", slabs=N_CHIPS)
    g_wbh = _matmul(yh, d_uh, "tn", BF16, "branch_hgrn_wgrad", slabs=N_CHIPS)
    late = _reduce_start([g_wba, g_wbh, g_wout], names[1:], c_arr)
    d_ya = _matmul(d_ua, wba_f, "nt", BF16, "branch_attn_dgrad", after=late.token)
    d_yh = _matmul(d_uh, wbh_f, "nt", BF16, "branch_hgrn_dgrad", after=late.token)
    dproj, dbias_p, dsinks_p = _attn_bwd(proj, bias_tab, sinks_b, d_ya, dproj, B_loc, S)
    dproj, dlb_p, dgain_p = _hgrn_bwd(proj, states, lb, gain_h, d_yh, dproj, B_loc, S)
    g_win = _matmul(h, dproj, "tn", BF16, "in_proj_wgrad", slabs=N_CHIPS)
    last = _reduce_start([g_win], names[:1], c_arr)
    dh = _matmul(dproj, win_f, "nt", F32, "in_proj_dgrad", after=last.token)
    grad_x2, gnpre_p = _rmsnorm_bwd(dh, x2, rstd, norm_pre, dout)
    shared = _reduce_finish(last, grad_x2, sc_arr) + _reduce_finish(late, grad_x2, sc_arr)

    grelb_p = jnp.dot(dbias_p.reshape(ATTN_HEADS, -1), onehot, precision=lax.Precision.HIGHEST).T
    gsinks_p = dsinks_p[:, 0, 0]
    dlb_sum = jnp.sum(dlb_p, axis=0).reshape(1, HGRN_WIDTH)
    ghn_p = jnp.sum(dgain_p, axis=0).reshape(HGRN_HEADS, HGRN_DIM)
    small_parts = [gnpre_p, gnpost_p, grelb_p, gsinks_p, dlb_sum, ghn_p, loss_p]
    small_shapes = [p.shape for p in small_parts]
    pack_sum = _sum_small(_pack_rows(small_parts))
    big_w = [w_in, w_branch_attn, w_branch_hgrn, w_out]
    big_m = [m_w_in, m_w_branch_attn, m_w_branch_hgrn, m_w_out]
    big_v = [v_w_in, v_w_branch_attn, v_w_branch_hgrn, v_w_out]
    big = {}
    for nm, gs, w, m, v in zip(names, shared, big_w, big_m, big_v):
        shp = w.shape
        g2 = gs.reshape(shp[1], shp[2])
        d, nm_, nv_, g_out = _adamw(w[0], g2, m[0], v[0], "adamw_" + nm)
        big[nm] = tuple(a.reshape(shp) for a in (g_out, d, nm_, nv_))

    gnpre, gnpost, grelb, gsinks, dlb, ghn, loss = _unpack_rows(pack_sum, small_shapes)
    (g_lb_logits,) = lb_vjp(dlb)
    small_names = ["norm_pre", "rel_bias", "attn_sinks", "lb_logits", "hgrn_norm", "norm_post"]
    small_w = [norm_pre, rel_bias, attn_sinks, lb_logits, hgrn_norm, norm_post]
    small_m = [m_norm_pre, m_rel_bias, m_attn_sinks, m_lb_logits, m_hgrn_norm, m_norm_post]
    small_v = [v_norm_pre, v_rel_bias, v_attn_sinks, v_lb_logits, v_hgrn_norm, v_norm_post]
    small_g = [gnpre.reshape(norm_pre.shape), grelb.reshape(rel_bias.shape), gsinks.reshape(attn_sinks.shape),
               g_lb_logits.reshape(lb_logits.shape), ghn.reshape(hgrn_norm.shape), gnpost.reshape(norm_post.shape)]
    shapes = [w.shape for w in small_w]
    d_s, nm_s, nv_s, _ = _adamw(_pack_rows(small_w), _pack_rows(small_g), _pack_rows(small_m), _pack_rows(small_v),
                             "adamw_small")
    small = {}
    for nm, g, d, m_, v_ in zip(small_names, small_g, _unpack_rows(d_s, shapes), _unpack_rows(nm_s, shapes),
                                _unpack_rows(nv_s, shapes)):
        small[nm] = (g, d, m_, v_)

    allw = {**big, **small}
    order = ["norm_pre", "w_in", "rel_bias", "attn_sinks", "lb_logits", "hgrn_norm", "w_branch_attn", "w_branch_hgrn",
             "w_out", "norm_post"]
    outs = [loss.reshape(()), grad_x2.reshape(B_loc, S, D)]
    for k in range(4):
        outs += [allw[nm][k] for nm in order]
    return tuple(outs)
```

```python
import functools
import math

import jax
import jax.numpy as jnp
from jax import lax
from jax.experimental import pallas as pl
from jax.experimental.pallas import tpu as pltpu

F32 = jnp.float32
BF16 = jnp.bfloat16
MESH = pl.DeviceIdType.MESH

ATTN_HEADS = 16
ATTN_KV_HEADS = 4
HEAD_DIM = 64
GROUP = ATTN_HEADS // ATTN_KV_HEADS
WINDOW = 128
ATTN_WIDTH = ATTN_HEADS * HEAD_DIM
KV_WIDTH = ATTN_KV_HEADS * HEAD_DIM
HGRN_HEADS = 8
HGRN_DIM = 128
HGRN_WIDTH = HGRN_HEADS * HGRN_DIM
CHUNK = 64
SUB = 16
NSUB = CHUNK // SUB
REL_BUCKETS = 32
REL_MAX_DIST = 128
NORM_EPS = 1e-6
ADAM_LR = 0.001
ADAM_B1 = 0.9
ADAM_B2 = 0.999
ADAM_EPS = 1e-08
ADAM_WD = 0.01
ADAM_STEP = 10
LANES = 128
N_CHIPS = 4
N_DEV = 8
VMEM_LIMIT = 48 * 1024 * 1024
MATMUL_OPERAND_BYTES = 34 * 1024 * 1024
MATMUL_VMEM_BYTES = 44 * 1024 * 1024

OFF_AQ = 0
OFF_AK = OFF_AQ + ATTN_WIDTH
OFF_AV = OFF_AK + KV_WIDTH
OFF_AG = OFF_AV + KV_WIDTH
OFF_HQ = OFF_AG + ATTN_WIDTH
OFF_HF = OFF_HQ + HGRN_WIDTH
OFF_HI = OFF_HF + HGRN_WIDTH
OFF_HG = OFF_HI + HGRN_WIDTH
OFF_GA = OFF_HG + HGRN_WIDTH

NT_DIMS = (((1,), (1,)), ((), ()))
TN_DIMS = (((0,), (0,)), ((), ()))
NN_DIMS = (((1,), (0,)), ((), ()))


def _pick(n, cands):
    for c in cands:
        if n % c == 0:
            return c
    raise ValueError(f"no tile for {n} in {cands}")


def _params(sem):
    return pltpu.CompilerParams(dimension_semantics=sem, vmem_limit_bytes=VMEM_LIMIT)


def _bdot(a, b, dims=NN_DIMS):
    return lax.dot_general(a.astype(BF16), b.astype(BF16), dims, preferred_element_type=F32)


def _matmul(a, b, mode, out_dtype, name, slabs=1, after=None):
    if mode == "nn":
        (M, K), (K2, N) = a.shape, b.shape
    elif mode == "nt":
        (M, K), (N, K2) = a.shape, b.shape
    else:
        (K, M), (K2, N) = a.shape, b.shape
    assert K == K2
    nslab = N // slabs
    tm = _pick(M, (1024, 512, 256, 128))
    out_bytes = jnp.dtype(out_dtype).itemsize
    choices = []
    for tn in (2688, 1024, 896, 512, 384, 256, 128):
        for tk in (4096, 3584, 2048, 1792, 1536, 1024, 512, 256, 128):
            acc = 0 if tk == K else 4 * tm * tn
            if (nslab % tn == 0 and K % tk == 0 and 4 * tk * (tm + tn) <= MATMUL_OPERAND_BYTES
                    and 4 * tk * (tm + tn) + 2 * out_bytes * tm * tn + acc <= MATMUL_VMEM_BYTES):
                choices.append((K // tk > 1, -tn, tn, tk))
                break
    _, _, tn, tk = min(choices)
    nk = K // tk
    per = nslab // tn
    dims = {"nn": NN_DIMS, "nt": NT_DIMS, "tn": TN_DIMS}[mode]

    n_in = 2 if after is None else 3

    def body(*refs):
        a_ref, b_ref, o_ref, acc = refs[0], refs[1], refs[n_in], refs[n_in + 1:]
        part = lax.dot_general(a_ref[...], b_ref[...], dims, preferred_element_type=F32)
        if nk == 1:
            o_ref[...] = part.astype(o_ref.dtype)
            return
        acc_ref, = acc
        k = pl.program_id(2)

        @pl.when(k == 0)
        def _():
            acc_ref[...] = part

        @pl.when((k > 0) & (k < nk - 1))
        def _():
            acc_ref[...] += part

        @pl.when(k == nk - 1)
        def _():
            o_ref[...] = (acc_ref[...] + part).astype(o_ref.dtype)

    if mode == "tn":
        a_spec = pl.BlockSpec((tk, tm), lambda i, j, k: (k, i))
    else:
        a_spec = pl.BlockSpec((tm, tk), lambda i, j, k: (i, k))
    if mode == "nt":
        b_spec = pl.BlockSpec((tn, tk), lambda i, j, k: (j, k))
    else:
        b_spec = pl.BlockSpec((tk, tn), lambda i, j, k: (k, j))
    if slabs == 1:
        o_shape = jax.ShapeDtypeStruct((M, N), out_dtype)
        o_spec = pl.BlockSpec((tm, tn), lambda i, j, k: (i, j))
    else:
        o_shape = jax.ShapeDtypeStruct((slabs, M, nslab), out_dtype)
        o_spec = pl.BlockSpec((None, tm, tn), lambda i, j, k: (j // per, i, j % per))
    return pl.pallas_call(
        body, name=name, grid=(M // tm, N // tn, nk), in_specs=[a_spec, b_spec] + ([] if after is None else [ANY]),
        out_specs=o_spec, out_shape=o_shape,
        scratch_shapes=[pltpu.VMEM((tm, tn), F32)] if nk > 1 else [],
        compiler_params=_params(("parallel", "parallel", "arbitrary")),
    )(*((a, b) if after is None else (a, b, after)))


def _rmsnorm_fwd(x2, gain):
    T, D = x2.shape
    tr = _pick(T, (256, 128))

    def body(x_ref, g_ref, h_ref, r_ref):
        xv = x_ref[...]
        r = lax.rsqrt(jnp.mean(xv * xv, axis=-1, keepdims=True) + NORM_EPS)
        h_ref[...] = (xv * r * g_ref[...]).astype(BF16)
        r_ref[...] = r

    return pl.pallas_call(
        body, name="rmsnorm_pre_fwd", grid=(T // tr,),
        in_specs=[pl.BlockSpec((tr, D), lambda i: (i, 0)), pl.BlockSpec((1, D), lambda i: (0, 0))],
        out_specs=[pl.BlockSpec((tr, D), lambda i: (i, 0)), pl.BlockSpec((tr, 1), lambda i: (i, 0))],
        out_shape=[jax.ShapeDtypeStruct((T, D), BF16), jax.ShapeDtypeStruct((T, 1), F32)],
        compiler_params=_params(("parallel",)),
    )(x2, gain)


def _rmsnorm_bwd(dh, x2, rstd, gain, dout):
    T, D = x2.shape
    tr = _pick(T, (256, 128))

    def body(dh_ref, x_ref, r_ref, g_ref, do_ref, gx_ref, gg_ref):
        @pl.when(pl.program_id(0) == 0)
        def _():
            gg_ref[...] = jnp.zeros_like(gg_ref)

        n = x_ref[...] * r_ref[...]
        dhv = dh_ref[...]
        dn = dhv * g_ref[...]
        gx_ref[...] = do_ref[...] + r_ref[...] * (dn - n * jnp.mean(dn * n, axis=-1, keepdims=True))
        gg_ref[...] += jnp.sum(dhv * n, axis=0, keepdims=True)

    row = pl.BlockSpec((tr, D), lambda i: (i, 0))
    vec = pl.BlockSpec((1, D), lambda i: (0, 0))
    return pl.pallas_call(
        body, name="rmsnorm_pre_bwd", grid=(T // tr,),
        in_specs=[row, row, pl.BlockSpec((tr, 1), lambda i: (i, 0)), vec, row],
        out_specs=[row, vec],
        out_shape=[jax.ShapeDtypeStruct((T, D), F32), jax.ShapeDtypeStruct((1, D), F32)],
        compiler_params=_params(("arbitrary",)),
    )(dh, x2, rstd, gain, dout)


def _post_loss(yv, x2, tgt2, gain):
    T, D = x2.shape
    tr = _pick(T, (256, 128))

    def body(y_ref, x_ref, t_ref, g_ref, dy_ref, do_ref, loss_ref, gg_ref):
        @pl.when(pl.program_id(0) == 0)
        def _():
            gg_ref[...] = jnp.zeros_like(gg_ref)
            loss_ref[...] = jnp.zeros_like(loss_ref)

        yv_ = y_ref[...]
        r = lax.rsqrt(jnp.mean(yv_ * yv_, axis=-1, keepdims=True) + NORM_EPS)
        n = yv_ * r
        e = (x_ref[...] + n * g_ref[...]) - t_ref[...]
        loss_ref[...] += 0.5 * jnp.sum(jnp.mean(e * e, axis=-1, keepdims=True), axis=0, keepdims=True)
        dz = e / D
        do_ref[...] = dz
        gg_ref[...] += jnp.sum(dz * n, axis=0, keepdims=True)
        dn = dz * g_ref[...]
        dy_ref[...] = (r * (dn - n * jnp.mean(dn * n, axis=-1, keepdims=True))).astype(BF16)

    row = pl.BlockSpec((tr, D), lambda i: (i, 0))
    vec = pl.BlockSpec((1, D), lambda i: (0, 0))
    return pl.pallas_call(
        body, name="post_norm_loss", grid=(T // tr,),
        in_specs=[row, row, row, vec],
        out_specs=[row, row, pl.BlockSpec((1, 1), lambda i: (0, 0)), vec],
        out_shape=[jax.ShapeDtypeStruct((T, D), BF16), jax.ShapeDtypeStruct((T, D), F32),
                   jax.ShapeDtypeStruct((1, 1), F32), jax.ShapeDtypeStruct((1, D), F32)],
        compiler_params=_params(("arbitrary",)),
    )(yv, x2, tgt2, gain)


def _window(rows, cols, at):
    return pl.BlockSpec((pl.Element(rows), pl.Element(cols)), at)


def _gate_windows(tm, tn, D):
    return [_window(tm, tn, lambda i, j: (i * tm, pl.multiple_of(OFF_GA + j * tn, LANES))),
            _window(tm, tn, lambda i, j: (i * tm, pl.multiple_of(OFF_GA + D + j * tn, LANES)))]


def _branch_proj_merge(yh, wbh, proj, ua):
    T, K = yh.shape
    D = wbh.shape[1]
    tm, tn = _pick(T, (512, 256, 128)), _pick(D, (1024, 512, 256))

    def body(a_ref, b_ref, ga_ref, gh_ref, ua_ref, uh_ref, m_ref):
        uh = jnp.dot(a_ref[...], b_ref[...], preferred_element_type=F32).astype(BF16)
        uh_ref[...] = uh
        m_ref[...] = (jax.nn.sigmoid(ga_ref[...]) * ua_ref[...].astype(F32)
                      + jax.nn.sigmoid(gh_ref[...]) * uh.astype(F32)).astype(BF16)

    blk = pl.BlockSpec((tm, tn), lambda i, j: (i, j))
    o = jax.ShapeDtypeStruct((T, D), BF16)
    return pl.pallas_call(
        body, name="branch_hgrn_proj_merge", grid=(T // tm, D // tn),
        in_specs=[pl.BlockSpec((tm, K), lambda i, j: (i, 0)), pl.BlockSpec((K, tn), lambda i, j: (0, j))]
        + _gate_windows(tm, tn, D) + [blk],
        out_specs=[blk, blk], out_shape=[o, o],
        compiler_params=_params(("parallel", "arbitrary")),
    )(yh, wbh, proj, proj, ua)


def _out_dgrad_merge_bwd(dy, wout, proj, ua, uh):
    T, K = dy.shape
    D = wout.shape[0]
    tm, tn = _pick(T, (512, 256, 128)), _pick(D, (1024, 512, 256))
    nj = D // tn

    def body(a_ref, b_ref, ga_ref, gh_ref, ua_ref, uh_ref, dua_ref, duh_ref, dproj_ref):
        d = lax.dot_general(a_ref[...], b_ref[...], NT_DIMS, preferred_element_type=F32)
        sa = jax.nn.sigmoid(ga_ref[...])
        sh = jax.nn.sigmoid(gh_ref[...])
        dua_ref[...] = (d * sa).astype(BF16)
        duh_ref[...] = (d * sh).astype(BF16)
        dga = (d * ua_ref[...].astype(F32) * sa * (1.0 - sa)).astype(BF16)
        dgh = (d * uh_ref[...].astype(F32) * sh * (1.0 - sh)).astype(BF16)
        for jj in range(nj):
            @pl.when(pl.program_id(1) == jj)
            def _():
                dproj_ref[:, jj * tn:(jj + 1) * tn] = dga
                dproj_ref[:, D + jj * tn:D + (jj + 1) * tn] = dgh

    blk = pl.BlockSpec((tm, tn), lambda i, j: (i, j))
    o = jax.ShapeDtypeStruct((T, D), BF16)
    return pl.pallas_call(
        body, name="out_proj_dgrad_merge_bwd", grid=(T // tm, nj),
        in_specs=[pl.BlockSpec((tm, K), lambda i, j: (i, 0)), pl.BlockSpec((tn, K), lambda i, j: (j, 0))]
        + _gate_windows(tm, tn, D) + [blk, blk],
        out_specs=[blk, blk, _window(tm, 2 * D, lambda i, j: (i * tm, OFF_GA))],
        out_shape=[o, o, jax.ShapeDtypeStruct((T, proj.shape[1]), BF16)],
        compiler_params=_params(("parallel", "arbitrary")),
    )(dy, wout, proj, proj, ua, uh)


KV_PAIR = 2
PAIR_HEADS = KV_PAIR * GROUP


MASKED = -1e30
assert HEAD_DIM ** -0.5 == 0.125


def _masked_bias_tables(bias_tab):
    qi = jnp.arange(WINDOW)[:, None]
    si = jnp.arange(2 * WINDOW)[None, :]
    dist = qi + WINDOW - si
    band = (dist >= 0) & (dist < WINDOW)
    return jnp.stack([jnp.where(band & (si >= WINDOW), bias_tab, MASKED), jnp.where(band, bias_tab, MASKED)])


def _attn_group_fn(q, k, v, ag, bias, sink):
    s = lax.dot_general((q * HEAD_DIM ** -0.5).astype(BF16), k.astype(BF16), NT_DIMS, preferred_element_type=F32) + bias
    m = lax.stop_gradient(jnp.maximum(jnp.max(s, axis=-1, keepdims=True), sink))
    p = jnp.exp(s - m)
    den = jnp.sum(p, axis=-1, keepdims=True) + jnp.exp(sink - m)
    o = jnp.dot(p.astype(BF16), v.astype(BF16), preferred_element_type=F32)
    return o * (jax.nn.silu(ag) / den)


def _attn_specs(B_loc, nb, order):
    qw = PAIR_HEADS * HEAD_DIM
    kw = KV_PAIR * HEAD_DIM

    def rows(g):
        b, p, n = order(*g)
        return b * nb + n

    def prev(g):
        b, p, n = order(*g)
        return b * nb + jnp.maximum(n - 1, 0)

    def pp(g):
        return order(*g)[1]

    q = pl.BlockSpec((WINDOW, qw), lambda *g: (rows(g), OFF_AQ // qw + pp(g)))
    kc = pl.BlockSpec((WINDOW, kw), lambda *g: (rows(g), OFF_AK // kw + pp(g)))
    kp = pl.BlockSpec((WINDOW, kw), lambda *g: (prev(g), OFF_AK // kw + pp(g)))
    vc = pl.BlockSpec((WINDOW, kw), lambda *g: (rows(g), OFF_AV // kw + pp(g)))
    vp = pl.BlockSpec((WINDOW, kw), lambda *g: (prev(g), OFF_AV // kw + pp(g)))
    ag = pl.BlockSpec((WINDOW, qw), lambda *g: (rows(g), OFF_AG // qw + pp(g)))
    bias = pl.BlockSpec((None, PAIR_HEADS, WINDOW, 2 * WINDOW), lambda *g: (jnp.minimum(order(*g)[2], 1), pp(g), 0, 0))
    sink = pl.BlockSpec((PAIR_HEADS, 8, LANES), lambda *g: (pp(g), 0, 0))
    return [q, kc, kp, vc, vp, ag, bias, sink], rows, pp


def _attn_operands(q_ref, kc_ref, kp_ref, vc_ref, vp_ref, ag_ref, bias_ref, sink_ref, j):
    lo, hi = j * HEAD_DIM, (j + 1) * HEAD_DIM
    k = jnp.concatenate([kp_ref[:, lo:hi], kc_ref[:, lo:hi]], axis=0)
    v = jnp.concatenate([vp_ref[:, lo:hi], vc_ref[:, lo:hi]], axis=0)
    heads = [j * GROUP + g for g in range(GROUP)]
    q = jnp.concatenate([q_ref[:, h * HEAD_DIM:(h + 1) * HEAD_DIM] for h in heads], axis=0)
    ag = jnp.concatenate([ag_ref[:, h * HEAD_DIM:(h + 1) * HEAD_DIM] for h in heads], axis=0)
    bias = jnp.concatenate([bias_ref[h] for h in heads], axis=0)
    sink = jnp.concatenate([jnp.broadcast_to(sink_ref[h, 0:1, 0:1], (WINDOW, 1)) for h in heads], axis=0)
    return q, k, v, ag, bias, sink


def _attn_fwd(proj, bias_tab, sinks_b, B_loc, S):
    T = B_loc * S
    nb = S // WINDOW
    n_pairs = ATTN_KV_HEADS // KV_PAIR
    in_specs, rows, pp = _attn_specs(B_loc, nb, lambda b, p, n: (b, p, n))

    def body(q_ref, kc_ref, kp_ref, vc_ref, vp_ref, ag_ref, bias_ref, sink_ref, ya_ref):
        for j in range(KV_PAIR):
            out = _attn_group_fn(*_attn_operands(q_ref, kc_ref, kp_ref, vc_ref, vp_ref, ag_ref, bias_ref, sink_ref, j))
            for g in range(GROUP):
                h = j * GROUP + g
                ya_ref[:, h * HEAD_DIM:(h + 1) * HEAD_DIM] = out[g * WINDOW:(g + 1) * WINDOW].astype(BF16)

    return pl.pallas_call(
        body, name="attn_fwd", grid=(B_loc, n_pairs, nb), in_specs=in_specs,
        out_specs=pl.BlockSpec((WINDOW, PAIR_HEADS * HEAD_DIM), lambda *g: (rows(g), pp(g))),
        out_shape=jax.ShapeDtypeStruct((T, ATTN_WIDTH), BF16),
        compiler_params=_params(("parallel", "parallel", "parallel")),
    )(proj, proj, proj, proj, proj, proj, bias_tab, sinks_b)


def _attn_bwd(proj, bias_tab, sinks_b, dya, dproj, B_loc, S):
    nb = S // WINDOW
    n_pairs = ATTN_KV_HEADS // KV_PAIR
    qw, kw = PAIR_HEADS * HEAD_DIM, KV_PAIR * HEAD_DIM

    def at(off, width, back=0):
        return lambda b, i, p: ((b * nb + jnp.maximum(nb - 1 - i - back, 0)) * WINDOW,
                                pl.multiple_of(off + p * width, LANES))

    in_specs = [_window(WINDOW, qw, at(OFF_AQ, qw)), _window(WINDOW, kw, at(OFF_AK, kw)),
                _window(WINDOW, kw, at(OFF_AK, kw, 1)), _window(WINDOW, kw, at(OFF_AV, kw)),
                _window(WINDOW, kw, at(OFF_AV, kw, 1)), _window(WINDOW, qw, at(OFF_AG, qw)),
                pl.BlockSpec((None, PAIR_HEADS, WINDOW, 2 * WINDOW), lambda b, i, p: (jnp.minimum(nb - 1 - i, 1), p, 0, 0)),
                pl.BlockSpec((PAIR_HEADS, 8, LANES), lambda b, i, p: (p, 0, 0)),
                pl.BlockSpec((WINDOW, qw), lambda b, i, p: (b * nb + nb - 1 - i, p)), ANY]

    def body(q_ref, kc_ref, kp_ref, vc_ref, vp_ref, ag_ref, bias_ref, sink_ref, dya_ref, dproj_in,
             dproj_ref, dbias_ref, dsink_ref, dkc_ref, dvc_ref):
        b, i, p = pl.program_id(0), pl.program_id(1), pl.program_id(2)
        n = nb - 1 - i

        @pl.when((b == 0) & (i == 0) & (p == 0))
        def _():
            dbias_ref[...] = jnp.zeros_like(dbias_ref)
            dsink_ref[...] = jnp.zeros_like(dsink_ref)

        @pl.when(i == 0)
        def _():
            dkc_ref[p] = jnp.zeros((WINDOW, kw), F32)
            dvc_ref[p] = jnp.zeros((WINDOW, kw), F32)

        dk_carry, dv_carry = dkc_ref[p], dvc_ref[p]
        dqs, dags, dbiases, dsinks, dks, dvs = [], [], [], [], [], []
        for j in range(KV_PAIR):
            ops = _attn_operands(q_ref, kc_ref, kp_ref, vc_ref, vp_ref, ag_ref, bias_ref, sink_ref, j)
            _, vjp = jax.vjp(_attn_group_fn, *ops)
            dout = jnp.concatenate([dya_ref[:, (j * GROUP + g) * HEAD_DIM:(j * GROUP + g + 1) * HEAD_DIM].astype(F32)
                                    for g in range(GROUP)], axis=0)
            dq, dk, dv, dag, dbias, dsink = vjp(dout)
            lo, hi = j * HEAD_DIM, (j + 1) * HEAD_DIM
            dks.append((dk[WINDOW:] + dk_carry[:, lo:hi]).astype(BF16))
            dvs.append((dv[WINDOW:] + dv_carry[:, lo:hi]).astype(BF16))
            dkc_ref[p, :, lo:hi] = dk[:WINDOW]
            dvc_ref[p, :, lo:hi] = dv[:WINDOW]
            for g in range(GROUP):
                blk = slice(g * WINDOW, (g + 1) * WINDOW)
                dqs.append(dq[blk].astype(BF16))
                dags.append(dag[blk].astype(BF16))
                dbiases.append(dbias[blk])
                dsinks.append(jnp.broadcast_to(jnp.sum(dsink[blk], axis=0, keepdims=True), (8, LANES)))

        for pair in range(n_pairs):
            @pl.when(p == pair)
            def _():
                for j in range(KV_PAIR):
                    col = (pair * KV_PAIR + j) * HEAD_DIM
                    dproj_ref[:, OFF_AK + col:OFF_AK + col + HEAD_DIM] = dks[j]
                    dproj_ref[:, OFF_AV + col:OFF_AV + col + HEAD_DIM] = dvs[j]
                for hh in range(PAIR_HEADS):
                    h = pair * PAIR_HEADS + hh
                    dproj_ref[:, OFF_AQ + h * HEAD_DIM:OFF_AQ + (h + 1) * HEAD_DIM] = dqs[hh]
                    dproj_ref[:, OFF_AG + h * HEAD_DIM:OFF_AG + (h + 1) * HEAD_DIM] = dags[hh]
                    dbias_ref[h] += dbiases[hh]
                    dsink_ref[h] += dsinks[hh]

    return pl.pallas_call(
        body, name="attn_bwd", grid=(B_loc, nb, n_pairs), in_specs=in_specs,
        out_specs=[_window(WINDOW, OFF_HQ, lambda b, i, p: ((b * nb + nb - 1 - i) * WINDOW, 0)),
                   pl.BlockSpec((ATTN_HEADS, WINDOW, 2 * WINDOW), lambda b, i, p: (0, 0, 0)),
                   pl.BlockSpec((ATTN_HEADS, 8, LANES), lambda b, i, p: (0, 0, 0))],
        out_shape=[jax.ShapeDtypeStruct(dproj.shape, dproj.dtype),
                   jax.ShapeDtypeStruct((ATTN_HEADS, WINDOW, 2 * WINDOW), F32),
                   jax.ShapeDtypeStruct((ATTN_HEADS, 8, LANES), F32)],
        input_output_aliases={9: 0},
        scratch_shapes=[pltpu.VMEM((n_pairs, WINDOW, kw), F32), pltpu.VMEM((n_pairs, WINDOW, kw), F32)],
        compiler_params=_params(("arbitrary", "arbitrary", "arbitrary")),
    )(proj, proj, proj, proj, proj, proj, bias_tab, sinks_b, dya, dproj)


class _HgrnPre:
    def __init__(self, fr, qr, lb, g_scr):
        t = lax.broadcasted_iota(jnp.int32, (CHUNK, CHUNK), 0)
        s = lax.broadcasted_iota(jnp.int32, (CHUNK, CHUNK), 1)
        self.sg = jax.nn.sigmoid(fr)
        self.f = lb + (1.0 - lb) * self.sg
        g = jnp.dot((t >= s).astype(F32), jnp.log(self.f), precision=lax.Precision.HIGHEST, preferred_element_type=F32)
        g_scr[...] = g
        self.g = g
        self.row = lax.broadcasted_iota(jnp.int32, g.shape, 0)
        self.refs = [jnp.zeros((1, g.shape[1]), F32)] + [g_scr[pl.ds(i * SUB - 1, 1), :] for i in range(1, NSUB)]
        self.gend = g_scr[pl.ds(CHUNK - 1, 1), :]
        refrow = jnp.zeros_like(g)
        for i in range(1, NSUB):
            refrow = jnp.where(self.row >= i * SUB, self.refs[i], refrow)
        self.sigq = jax.nn.sigmoid(qr)
        self.qs = qr * self.sigq
        self.k = 1.0 - self.f
        self.eg = jnp.exp(g)
        self.eqd = jnp.exp(g - refrow)
        self.ekd = [jnp.exp(jnp.where(self.row < (i + 1) * SUB, self.refs[i] - g, 0.0)) for i in range(NSUB)]
        self.ekdec = jnp.exp(self.gend - g)
        self.qg = self.qs * self.eg
        self.qd = self.qs * self.eqd
        self.kd = [self.k * e for e in self.ekd]
        self.kdec = self.k * self.ekdec
        self.egend = jnp.exp(self.gend)


def _hgrn_pair_mask():
    t = lax.broadcasted_iota(jnp.int32, (CHUNK, NSUB * CHUNK), 0)
    col = lax.broadcasted_iota(jnp.int32, (CHUNK, NSUB * CHUNK), 1)
    return ((t // SUB) == (col // CHUNK)) & ((col % CHUNK) <= t)


def _hgrn_head_out(p, lanes, state_t, v, mask):
    qg, qd = p.qg[:, lanes], p.qd[:, lanes]
    kall = jnp.concatenate([kd[:, lanes] for kd in p.kd], axis=0)
    vst = jnp.concatenate([v] * NSUB, axis=0)
    am = jnp.where(mask, _bdot(qd, kall, NT_DIMS), 0.0)
    o = _bdot(qg, state_t, NT_DIMS) + _bdot(am, vst)
    return o, (qg, qd, kall, am, vst)


def _hgrn_fwd(proj, lb, gain, B_loc, S):
    T = B_loc * S
    nc = S // CHUNK
    nh = HGRN_HEADS

    def at(off):
        return lambda b, n: ((b * nc + n) * CHUNK, off)

    vec = pl.BlockSpec((1, HGRN_WIDTH), lambda b, n: (0, 0))

    def body(q_ref, f_ref, v_ref, hg_ref, lb_ref, gain_ref, yh_ref, st_ref, state_scr, g_scr):
        @pl.when(pl.program_id(1) == 0)
        def _():
            state_scr[...] = jnp.zeros_like(state_scr)

        p = _HgrnPre(f_ref[...], q_ref[...], lb_ref[...], g_scr)
        v = v_ref[...]
        gate = gain_ref[...] * jax.nn.silu(hg_ref[...])
        mask = _hgrn_pair_mask()
        for hd in range(nh):
            lanes = slice(hd * HGRN_DIM, (hd + 1) * HGRN_DIM)
            st = state_scr[hd]
            st_ref[hd] = st
            o, _ = _hgrn_head_out(p, lanes, st, v[:, lanes], mask)
            rs = lax.rsqrt(jnp.mean(o * o, axis=-1, keepdims=True) + NORM_EPS)
            yh_ref[:, lanes] = (o * rs * gate[:, lanes]).astype(BF16)
            state_scr[hd] = st * p.egend[:, lanes] + _bdot(v[:, lanes], p.kdec[:, lanes], TN_DIMS)

    return pl.pallas_call(
        body, name="hgrn_fwd", grid=(B_loc, nc),
        in_specs=[_window(CHUNK, HGRN_WIDTH, at(OFF_HQ)), _window(CHUNK, HGRN_WIDTH, at(OFF_HF)),
                  _window(CHUNK, HGRN_WIDTH, at(OFF_HI)), _window(CHUNK, HGRN_WIDTH, at(OFF_HG)), vec, vec],
        out_specs=[pl.BlockSpec((CHUNK, HGRN_WIDTH), lambda b, n: (b * nc + n, 0)),
                   pl.BlockSpec((None, None, nh, HGRN_DIM, HGRN_DIM), lambda b, n: (b, n, 0, 0, 0))],
        out_shape=[jax.ShapeDtypeStruct((T, HGRN_WIDTH), BF16),
                   jax.ShapeDtypeStruct((B_loc, nc, nh, HGRN_DIM, HGRN_DIM), F32)],
        scratch_shapes=[pltpu.VMEM((nh, HGRN_DIM, HGRN_DIM), F32), pltpu.VMEM((CHUNK, HGRN_WIDTH), F32)],
        compiler_params=_params(("parallel", "arbitrary")),
    )(proj, proj, proj, proj, lb, gain)


def _hgrn_bwd(proj, states, lb, gain, dyh, dproj, B_loc, S):
    nc = S // CHUNK
    nh = HGRN_HEADS

    def at(off):
        return lambda b, i: ((b * nc + nc - 1 - i) * CHUNK, off)

    vec = pl.BlockSpec((1, HGRN_WIDTH), lambda b, i: (0, 0))
    in_specs = [_window(CHUNK, HGRN_WIDTH, at(OFF_HQ)), _window(CHUNK, HGRN_WIDTH, at(OFF_HF)),
                _window(CHUNK, HGRN_WIDTH, at(OFF_HI)), _window(CHUNK, HGRN_WIDTH, at(OFF_HG)), vec, vec,
                pl.BlockSpec((None, None, nh, HGRN_DIM, HGRN_DIM), lambda b, i: (b, nc - 1 - i, 0, 0, 0)),
                pl.BlockSpec((CHUNK, HGRN_WIDTH), lambda b, i: (b * nc + nc - 1 - i, 0)), ANY]
    acc_spec = pl.BlockSpec((None, 1, HGRN_WIDTH), lambda b, i: (b, 0, 0))

    def body(q_ref, f_ref, v_ref, hg_ref, lb_ref, gain_ref, st_ref, dyh_ref, dproj_in,
             dproj_ref, dlb_ref, dgain_ref, dstate_scr, g_scr, dg_scr):
        dq_ref, df_ref, dv_ref, dhg_ref = [dproj_ref.at[:, pl.ds(k * HGRN_WIDTH, HGRN_WIDTH)] for k in range(4)]

        @pl.when(pl.program_id(1) == 0)
        def _():
            dstate_scr[...] = jnp.zeros_like(dstate_scr)
            dlb_ref[...] = jnp.zeros_like(dlb_ref)
            dgain_ref[...] = jnp.zeros_like(dgain_ref)

        qr, lb, gain, hg, v = q_ref[...], lb_ref[...], gain_ref[...], hg_ref[...], v_ref[...]
        p = _HgrnPre(f_ref[...], qr, lb, g_scr)
        sgh = jax.nn.sigmoid(hg)
        sil = hg * sgh
        dy = dyh_ref[...].astype(F32)
        mask = _hgrn_pair_mask()
        dqg, dqd, dkdec, dv, dhg, dgend, dgain = [], [], [], [], [], [], []
        dkd = [[] for _ in range(NSUB)]
        heads = [slice(hd * HGRN_DIM, (hd + 1) * HGRN_DIM) for hd in range(nh)]
        sts = [st_ref[hd] for hd in range(nh)]
        dnews = [dstate_scr[hd] for hd in range(nh)]
        fwd = [_hgrn_head_out(p, lanes, st, v[:, lanes], mask) for lanes, st in zip(heads, sts)]
        for lanes, st, dnew in zip(heads, sts, dnews):
            dkdec_h = _bdot(v[:, lanes], dnew)
            dkdec.append(dkdec_h)
            dgend.append(jnp.sum(dkdec_h * p.kdec[:, lanes], axis=0, keepdims=True)
                         + jnp.sum(dnew * st, axis=0, keepdims=True) * p.egend[:, lanes])
        dos = []
        gate_grad = sgh * (1.0 + hg * (1.0 - sgh))
        for lanes, (o, _) in zip(heads, fwd):
            rs = lax.rsqrt(jnp.mean(o * o, axis=-1, keepdims=True) + NORM_EPS)
            n = o * rs
            dyn = dy[:, lanes] * n
            dgain.append(jnp.sum(dyn * sil[:, lanes], axis=0, keepdims=True))
            dhg.append(dyn * gain[:, lanes] * gate_grad[:, lanes])
            dn = dy[:, lanes] * gain[:, lanes] * sil[:, lanes]
            dos.append(rs * (dn - n * jnp.mean(dn * n, axis=-1, keepdims=True)))
        drs = []
        for hd, (lanes, st, dnew, do, (_, (qg, qd, kall, am, vst))) in enumerate(zip(heads, sts, dnews, dos, fwd)):
            dqg.append(_bdot(do, st))
            dstate_scr[hd] = _bdot(do, qg, TN_DIMS) + dnew * p.egend[:, lanes]
            drs.append(jnp.where(mask, _bdot(do, vst, NT_DIMS), 0.0))
            dvst = _bdot(am, do, TN_DIMS)
            dv.append(sum(dvst[i * CHUNK:(i + 1) * CHUNK] for i in range(NSUB)) + _bdot(p.kdec[:, lanes], dnew, NT_DIMS))
        for dr, (_, (qg, qd, kall, am, vst)) in zip(drs, fwd):
            dqd.append(_bdot(dr, kall))
            dkall = _bdot(dr, qd, TN_DIMS)
            for i in range(NSUB):
                dkd[i].append(dkall[i * CHUNK:(i + 1) * CHUNK])

        wide = lambda parts: jnp.concatenate(parts, axis=1)
        dqg, dqd, dkdec = wide(dqg), wide(dqd), wide(dkdec)
        t2 = dqd * p.qd
        dg = dqg * p.qg + t2 - dkdec * p.kdec
        dk = dkdec * p.ekdec
        dg_scr[...] = jnp.zeros_like(dg_scr)
        for i in range(NSUB):
            dkd_i = wide(dkd[i])
            tk = jnp.where(p.row < (i + 1) * SUB, dkd_i * p.kd[i], 0.0)
            dg = dg - tk
            dk = dk + dkd_i * p.ekd[i]
            if i >= 1:
                in_blk = (p.row >= i * SUB) & (p.row < (i + 1) * SUB)
                dg_scr[pl.ds(i * SUB - 1, 1), :] = (jnp.sum(tk, axis=0, keepdims=True)
                                                    - jnp.sum(jnp.where(in_blk, t2, 0.0), axis=0, keepdims=True))
        dg_scr[pl.ds(CHUNK - 1, 1), :] = wide(dgend)
        t = lax.broadcasted_iota(jnp.int32, (CHUNK, CHUNK), 0)
        s = lax.broadcasted_iota(jnp.int32, (CHUNK, CHUNK), 1)
        dlogf = jnp.dot((t <= s).astype(F32), dg + dg_scr[...], precision=lax.Precision.HIGHEST, preferred_element_type=F32)
        df = dlogf / p.f - dk
        df_ref[...] = (df * (1.0 - lb) * p.sg * (1.0 - p.sg)).astype(BF16)
        dlb_ref[...] += jnp.sum(df * (1.0 - p.sg), axis=0, keepdims=True)
        dq_ref[...] = ((dqg * p.eg + dqd * p.eqd) * p.sigq * (1.0 + qr * (1.0 - p.sigq))).astype(BF16)
        dv_ref[...] = wide(dv).astype(BF16)
        dhg_ref[...] = wide(dhg).astype(BF16)
        dgain_ref[...] += wide(dgain)

    acc = jax.ShapeDtypeStruct((B_loc, 1, HGRN_WIDTH), F32)
    return pl.pallas_call(
        body, name="hgrn_bwd", grid=(B_loc, nc), in_specs=in_specs,
        out_specs=[_window(CHUNK, 4 * HGRN_WIDTH, at(OFF_HQ)), acc_spec, acc_spec],
        out_shape=[jax.ShapeDtypeStruct(dproj.shape, dproj.dtype), acc, acc],
        input_output_aliases={8: 0},
        scratch_shapes=[pltpu.VMEM((nh, HGRN_DIM, HGRN_DIM), F32), pltpu.VMEM((CHUNK, HGRN_WIDTH), F32),
                        pltpu.VMEM((CHUNK, HGRN_WIDTH), F32)],
        compiler_params=_params(("parallel", "arbitrary")),
    )(proj, proj, proj, proj, lb, gain, states, dyh, dproj)


def _adamw(w, g, m, v, name):
    R, C = w.shape
    tr = _pick(R, (128, 64, 32, 16, 8)) if C > 1024 else _pick(R, (512, 256, 128, 64, 32, 16, 8))

    def body(w_ref, g_ref, m_ref, v_ref, d_ref, nm_ref, nv_ref, g_out_ref):
        gv = g_ref[...]
        g_out_ref[...] = gv
        nm = ADAM_B1 * m_ref[...] + (1.0 - ADAM_B1) * gv
        nv = ADAM_B2 * v_ref[...] + (1.0 - ADAM_B2) * (gv * gv)
        m_hat = nm / (1.0 - ADAM_B1 ** ADAM_STEP)
        v_hat = nv / (1.0 - ADAM_B2 ** ADAM_STEP)
        d_ref[...] = -ADAM_LR * (m_hat / (jnp.sqrt(v_hat) + ADAM_EPS) + ADAM_WD * w_ref[...])
        nm_ref[...] = nm
        nv_ref[...] = nv

    blk = pl.BlockSpec((tr, C), lambda i: (i, 0))
    o = jax.ShapeDtypeStruct((R, C), F32)
    return pl.pallas_call(
        body, name=name, grid=(R // tr,), in_specs=[blk] * 4, out_specs=[blk] * 4, out_shape=[o, o, o, o],
        compiler_params=_params(("parallel",)),
    )(w, g, m, v)


ANY = pl.BlockSpec(memory_space=pl.ANY)
VMEM_SPEC = pl.BlockSpec(memory_space=pltpu.VMEM)


def _place():
    x, y, c = lax.axis_index("x"), lax.axis_index("y"), lax.axis_index("c")
    other_chips = [(1 - x, y), (x, 1 - y), (1 - x, 1 - y)]
    return x, y, c, other_chips


def _cast_into_full(w, ax, s_arr, name):
    R, C = w.shape
    tr = _pick(R, (256, 128))
    nr = R // tr

    def body(s_ref, w_ref, o_ref):
        o_ref[...] = w_ref[...].astype(BF16)

    if ax == 1:
        shape, o_map = (R, N_CHIPS * C), lambda i, s: (i, s[0])
    else:
        shape, o_map = (N_CHIPS * R, C), lambda i, s: (s[0] * nr + i, 0)
    return pl.pallas_call(
        body, name=name,
        grid_spec=pltpu.PrefetchScalarGridSpec(
            num_scalar_prefetch=1, grid=(nr,), in_specs=[pl.BlockSpec((tr, C), lambda i, s: (i, 0))],
            out_specs=pl.BlockSpec((tr, C), o_map)),
        out_shape=jax.ShapeDtypeStruct(shape, BF16),
        compiler_params=_params(("parallel",)),
    )(s_arr, w)


class _Gather:
    def __init__(self, fulls, shard_shapes, axes, tag):
        self.shapes, self.axes, self.tag, self.nw = shard_shapes, axes, tag, len(fulls)
        self.fulls, self.sems, self.token = list(fulls), {}, None

    def start(self, peers, after=None):
        nw, np_ = self.nw, len(peers)

        def body(*refs):
            ins, sems = refs[:nw], refs[nw + (after is not None):nw + (after is not None) + 2 * np_]
            for k, j in enumerate(peers):
                for cp in self._peer_copies(ins, sems[2 * k], sems[2 * k + 1], j):
                    cp.start()
            refs[-1][...] = jnp.zeros_like(refs[-1])

        out = pl.pallas_call(
            body, name="gather_start_%s_%s" % (self.tag, "".join(map(str, peers))),
            out_shape=(*[pltpu.SemaphoreType.DMA((nw,))] * (2 * np_),
                       *[pltpu.HBM(f.shape, f.dtype) for f in self.fulls], jax.ShapeDtypeStruct((8, LANES), F32)),
            in_specs=[HBM_SPEC] * nw + ([] if after is None else [ANY]),
            out_specs=(*[SEM_SPEC] * (2 * np_), *[HBM_SPEC] * nw, VMEM_SPEC),
            input_output_aliases={k: 2 * np_ + k for k in range(nw)},
            compiler_params=pltpu.CompilerParams(has_side_effects=DATAFLOW),
        )(*[pltpu.with_memory_space_constraint(f, pltpu.HBM) for f in self.fulls], *(() if after is None else (after,)))
        for k, j in enumerate(peers):
            self.sems[j] = (out[2 * k], out[2 * k + 1])
        self.fulls = list(out[2 * np_:2 * np_ + nw])
        self.token = out[-1]

    def _region(self, ref, i, t, half):
        R, C = self.shapes[i]
        hr = R // 2
        if self.axes[i] == 1:
            return ref.at[pl.ds(half * hr, hr), pl.ds(pl.multiple_of(t * C, LANES), C)]
        return ref.at[pl.ds(t * R + half * hr, hr), :]

    def _peer_copies(self, refs, send_sems, recv_sems, j):
        x, y, c, chips = _place()
        s = 2 * x + y
        return [pltpu.make_async_remote_copy(
            src_ref=self._region(refs[i], i, s, c), dst_ref=self._region(refs[i], i, s, c), send_sem=send_sems.at[i],
            recv_sem=recv_sems.at[i], device_id=(*chips[j], c), device_id_type=MESH) for i in range(self.nw)]

    def wait(self, peers, after):
        nw, np_ = self.nw, len(peers)

        def body(*refs):
            ins, sems = refs[:nw], refs[nw:nw + 2 * np_]
            for k, j in enumerate(peers):
                for cp in self._peer_copies(ins, sems[2 * k], sems[2 * k + 1], j):
                    cp.wait_send()
                    cp.wait_recv()

        sem_args = [s for j in peers for s in self.sems[j]]
        out = pl.pallas_call(
            body, name="gather_wait_%s_%s" % (self.tag, "".join(map(str, peers))),
            out_shape=tuple(pltpu.HBM(f.shape, f.dtype) for f in self.fulls),
            in_specs=[HBM_SPEC] * nw + [SEM_SPEC] * (2 * np_) + [ANY], out_specs=tuple([HBM_SPEC] * nw),
            input_output_aliases={k: k for k in range(nw)},
            compiler_params=pltpu.CompilerParams(has_side_effects=DATAFLOW),
        )(*self.fulls, *sem_args, after)
        self.fulls = list(out)

    def forward(self, peers):
        nw, np_ = self.nw, len(peers)

        def body(*refs):
            ins, outs = refs[:nw], refs[nw:2 * nw]
            send_sems, recv_sems = refs[2 * nw:]
            x, y, c, chips = _place()
            cps = []
            for i in range(nw):
                for k, j in enumerate(peers):
                    t = 2 * chips[j][0] + chips[j][1]
                    cp = pltpu.make_async_remote_copy(
                        src_ref=self._region(ins[i], i, t, c), dst_ref=self._region(outs[i], i, t, c),
                        send_sem=send_sems.at[i * np_ + k], recv_sem=recv_sems.at[i * np_ + k],
                        device_id=(x, y, 1 - c), device_id_type=MESH)
                    cp.start()
                    cps.append(cp)
            for cp in cps:
                cp.wait()

        out = pl.pallas_call(
            body, name="gather_forward_%s_%s" % (self.tag, "".join(map(str, peers))),
            in_specs=[ANY] * nw, out_specs=[ANY] * nw,
            out_shape=[jax.ShapeDtypeStruct(f.shape, f.dtype) for f in self.fulls],
            input_output_aliases={i: i for i in range(nw)},
            scratch_shapes=[pltpu.SemaphoreType.DMA((nw * np_,)), pltpu.SemaphoreType.DMA((nw * np_,))],
        )(*self.fulls)
        self.fulls = list(out)


def _matmul_slab(a, wfull, slab_arr, prev, after, name):
    M, K = a.shape
    N = wfull.shape[1]
    nslab = N // N_CHIPS
    tn = _pick(nslab, (2688, 896, 512, 384, 256, 128))
    tm = _pick(M, (512, 256, 128) if tn > 1024 else (1024, 512, 256, 128))
    per = nslab // tn
    extra = [e for e in (prev, after) if e is not None]

    def body(slab_ref, a_ref, b_ref, *rest):
        rest[len(extra)][...] = jnp.dot(a_ref[...], b_ref[...], preferred_element_type=F32)

    return pl.pallas_call(
        body, name=name,
        grid_spec=pltpu.PrefetchScalarGridSpec(
            num_scalar_prefetch=1, grid=(M // tm, per),
            in_specs=[pl.BlockSpec((tm, K), lambda i, j, sl: (i, 0)),
                      pl.BlockSpec((K, tn), lambda i, j, sl: (0, sl[0] * per + j))] + [ANY] * len(extra),
            out_specs=pl.BlockSpec((tm, tn), lambda i, j, sl: (i, sl[0] * per + j))),
        out_shape=jax.ShapeDtypeStruct((M, N), F32),
        input_output_aliases={} if prev is None else {3: 0},
        compiler_params=_params(("parallel", "arbitrary")),
    )(slab_arr, a, wfull, *extra)


def _exchange_sibling_halves(gs, name):
    nw = len(gs)

    def body(*refs):
        ins, outs = refs[:nw], refs[nw:2 * nw]
        send_sems, recv_sems = refs[2 * nw:]
        x, y, c, _ = _place()
        cps = []
        for i in range(nw):
            cp = pltpu.make_async_remote_copy(src_ref=ins[i].at[:, 1 - c], dst_ref=outs[i], send_sem=send_sems.at[i],
                                              recv_sem=recv_sems.at[i], device_id=(x, y, 1 - c), device_id_type=MESH)
            cp.start()
            cps.append(cp)
        for cp in cps:
            cp.wait()

    return pl.pallas_call(
        body, name=name, in_specs=[ANY] * nw, out_specs=[ANY] * nw,
        out_shape=[jax.ShapeDtypeStruct((g.shape[0],) + g.shape[2:], g.dtype) for g in gs],
        scratch_shapes=[pltpu.SemaphoreType.DMA((nw,)), pltpu.SemaphoreType.DMA((nw,))],
    )(*gs)


HBM_SPEC = pl.BlockSpec(memory_space=pltpu.HBM)
SEM_SPEC = pl.BlockSpec(memory_space=pltpu.SEMAPHORE)
DATAFLOW = pltpu.SideEffectType.DATAFLOW_SIDE_EFFECTING


def _chip_copies(ins, lands, send_sems, recv_sems):
    x, y, c, chips = _place()
    return [pltpu.make_async_remote_copy(
        src_ref=ins[i].at[2 * chip[0] + chip[1]], dst_ref=lands[i].at[j], send_sem=send_sems.at[i * 3 + j],
        recv_sem=recv_sems.at[i * 3 + j], device_id=(*chip, c), device_id_type=MESH)
        for i in range(len(ins)) for j, chip in enumerate(chips)]


def _chips_send_start(ss, name):
    nw = len(ss)
    lands = [pltpu.with_memory_space_constraint(lax.empty((N_CHIPS - 1,) + s.shape[1:], s.dtype), pltpu.HBM) for s in ss]

    def body(*refs):
        ins, land_refs = refs[:nw], refs[nw:2 * nw]
        send_sems, recv_sems = refs[2 * nw], refs[2 * nw + 1]
        token = refs[-1]
        for cp in _chip_copies(ins, land_refs, send_sems, recv_sems):
            cp.start()
        token[...] = jnp.zeros_like(token)

    n = 3 * nw
    out = pl.pallas_call(
        body, name=name,
        out_shape=(pltpu.SemaphoreType.DMA((n,)), pltpu.SemaphoreType.DMA((n,)),
                   *[pltpu.HBM(s.shape, s.dtype) for s in ss], *[pltpu.HBM(l.shape, l.dtype) for l in lands],
                   jax.ShapeDtypeStruct((8, LANES), F32)),
        in_specs=[HBM_SPEC] * (2 * nw), out_specs=(SEM_SPEC, SEM_SPEC, *[HBM_SPEC] * (2 * nw), VMEM_SPEC),
        input_output_aliases={k: 2 + k for k in range(2 * nw)},
        compiler_params=pltpu.CompilerParams(has_side_effects=DATAFLOW),
    )(*[pltpu.with_memory_space_constraint(s, pltpu.HBM) for s in ss], *lands)
    return out[0], out[1], list(out[2:2 + nw]), list(out[2 + nw:2 + 2 * nw]), out[-1]


def _chips_send_wait(send_sems, recv_sems, ss, lands, after, name):
    nw = len(ss)

    def body(*refs):
        ins, land_refs = refs[:nw], refs[nw:2 * nw]
        s_sems, r_sems = refs[2 * nw], refs[2 * nw + 1]
        for cp in _chip_copies(ins, land_refs, s_sems, r_sems):
            cp.wait_send()
            cp.wait_recv()

    out = pl.pallas_call(
        body, name=name,
        out_shape=(*[pltpu.HBM(s.shape, s.dtype) for s in ss], *[pltpu.HBM(l.shape, l.dtype) for l in lands]),
        in_specs=[HBM_SPEC] * (2 * nw) + [SEM_SPEC, SEM_SPEC, ANY], out_specs=tuple([HBM_SPEC] * (2 * nw)),
        input_output_aliases={k: k for k in range(2 * nw)},
        compiler_params=pltpu.CompilerParams(has_side_effects=DATAFLOW),
    )(*ss, *lands, send_sems, recv_sems, after)
    return list(out[nw:])


def _sum_small(pack):
    rows = pack.shape[0]

    def body(pack_ref, sum_ref, all_ref, send_sems, recv_sems):
        x, y, c, _ = _place()
        me = 4 * x + 2 * y + c
        all_ref[me] = pack_ref[...]
        cps = []
        for k in range(1, N_DEV):
            to = (1 - x if k & 4 else x, 1 - y if k & 2 else y, 1 - c if k & 1 else c)
            cp = pltpu.make_async_remote_copy(
                src_ref=pack_ref, dst_ref=all_ref.at[me], send_sem=send_sems.at[k - 1],
                recv_sem=recv_sems.at[k - 1], device_id=to, device_id_type=MESH)
            cp.start()
            cps.append(cp)
        for cp in cps:
            cp.wait()
        total = all_ref[0]
        for d in range(1, N_DEV):
            total = total + all_ref[d]
        sum_ref[...] = total

    return pl.pallas_call(
        body, name="sum_small_grads", in_specs=[VMEM_SPEC], out_specs=VMEM_SPEC,
        out_shape=jax.ShapeDtypeStruct(pack.shape, F32),
        scratch_shapes=[pltpu.VMEM((N_DEV, rows, LANES), F32), pltpu.SemaphoreType.DMA((N_DEV - 1,)),
                        pltpu.SemaphoreType.DMA((N_DEV - 1,))],
    )(pack)


def _share_with_sibling(fs, name):
    nw = len(fs)

    def body(*refs):
        ins, outs = refs[:nw], refs[nw:2 * nw]
        send_sems, recv_sems = refs[2 * nw:]
        x, y, c, _ = _place()
        cps = []
        for i in range(nw):
            cp = pltpu.make_async_remote_copy(src_ref=ins[i].at[c], dst_ref=outs[i].at[c], send_sem=send_sems.at[i],
                                              recv_sem=recv_sems.at[i], device_id=(x, y, 1 - c), device_id_type=MESH)
            cp.start()
            cps.append(cp)
        for cp in cps:
            cp.wait()

    return pl.pallas_call(
        body, name=name, in_specs=[ANY] * nw, out_specs=[ANY] * nw,
        out_shape=[jax.ShapeDtypeStruct(f.shape, f.dtype) for f in fs],
        input_output_aliases={i: i for i in range(nw)},
        scratch_shapes=[pltpu.SemaphoreType.DMA((nw,)), pltpu.SemaphoreType.DMA((nw,))],
    )(*fs)


def _sum_sibling(g, land, c_arr, name):
    _, _, P, Q = g.shape
    tp = _pick(P, (256, 128, 64))

    def body(c_ref, g_ref, l_ref, s_ref):
        s_ref[...] = (g_ref[...].astype(F32) + l_ref[...].astype(F32)).astype(BF16)

    return pl.pallas_call(
        body, name=name,
        grid_spec=pltpu.PrefetchScalarGridSpec(
            num_scalar_prefetch=1, grid=(N_CHIPS, P // tp),
            in_specs=[pl.BlockSpec((None, None, tp, Q), lambda t, r, cr: (t, cr[0], r, 0)),
                      pl.BlockSpec((None, tp, Q), lambda t, r, cr: (t, r, 0))],
            out_specs=pl.BlockSpec((None, tp, Q), lambda t, r, cr: (t, r, 0))),
        out_shape=jax.ShapeDtypeStruct((N_CHIPS, P, Q), BF16),
        compiler_params=_params(("parallel", "parallel")),
    )(c_arr, g, land)


def _sum_chips(g, land, recv, sc_arr, name):
    _, _, P, Q = g.shape
    tp = _pick(P, (256, 128, 64))

    def body(sc_ref, g_ref, l_ref, r_ref, f_ref):
        acc = g_ref[...].astype(F32) + l_ref[...].astype(F32)
        for j in range(N_CHIPS - 1):
            acc = acc + r_ref[j].astype(F32)
        f_ref[...] = acc

    return pl.pallas_call(
        body, name=name,
        grid_spec=pltpu.PrefetchScalarGridSpec(
            num_scalar_prefetch=1, grid=(P // tp,),
            in_specs=[pl.BlockSpec((None, None, tp, Q), lambda r, sc: (sc[0], sc[1], r, 0)),
                      pl.BlockSpec((None, tp, Q), lambda r, sc: (sc[0], r, 0)),
                      pl.BlockSpec((N_CHIPS - 1, tp, Q), lambda r, sc: (0, r, 0))],
            out_specs=pl.BlockSpec((None, tp, Q), lambda r, sc: (sc[1], r, 0))),
        out_shape=jax.ShapeDtypeStruct((2, P, Q), F32),
        compiler_params=_params(("parallel",)),
    )(sc_arr, g, land, recv)


class _Reduction:
    def __init__(self, names, pieces, lands, flight):
        self.names, self.pieces, self.lands = names, pieces, lands
        self.send_sems, self.recv_sems, self.sums, self.zones, self.token = flight


def _reduce_start(grads, names, c_arr):
    pieces = [g.reshape(N_CHIPS, 2, -1, g.shape[-1]) for g in grads]
    tag = names[0] if len(names) == 1 else "branches"
    lands = _exchange_sibling_halves(pieces, "grads_to_sibling_" + tag)
    sums = [_sum_sibling(g, l, c_arr, "sum_sibling_" + nm) for g, l, nm in zip(pieces, lands, names)]
    return _Reduction(names, pieces, lands, _chips_send_start(sums, "grads_to_chips_start_" + tag))


def _reduce_finish(red, after, sc_arr):
    tag = red.names[0] if len(red.names) == 1 else "branches"
    recvs = _chips_send_wait(red.send_sems, red.recv_sems, red.sums, red.zones, after, "grads_to_chips_wait_" + tag)
    halves = [_sum_chips(g, l, r, sc_arr, "sum_chips_" + nm) for g, l, r, nm in zip(red.pieces, red.lands, recvs, red.names)]
    return _share_with_sibling(halves, "grads_share_sibling_" + tag)


def _t5_bucket(dist):
    max_exact = REL_BUCKETS // 2
    d = jnp.maximum(dist, 0)
    df = jnp.maximum(d, 1).astype(F32)
    large = max_exact + (jnp.log(df / max_exact) / math.log(REL_MAX_DIST / max_exact)
                         * (REL_BUCKETS - max_exact)).astype(jnp.int32)
    large = jnp.minimum(large, REL_BUCKETS - 1)
    return jnp.where(d < max_exact, d, large)


def _bucket_table():
    qi = jnp.arange(WINDOW)[:, None]
    si = jnp.arange(2 * WINDOW)[None, :]
    return _t5_bucket(qi + WINDOW - si)


TILE_WORDS = 8 * LANES


def _tile_rows(shape):
    return -(-math.prod(shape) // TILE_WORDS) * 8


def _rows_of(a):
    flat = a.reshape(-1).astype(F32)
    n = _tile_rows(a.shape) * LANES
    return jnp.pad(flat, (0, n - flat.shape[0])).reshape(-1, LANES)


def _pack_rows(parts):
    return jnp.concatenate([_rows_of(p) for p in parts], axis=0)


def _unpack_rows(packed, shapes):
    out, at = [], 0
    for shp in shapes:
        n, nr = math.prod(shp), _tile_rows(shp)
        out.append(packed[at:at + nr].reshape(-1)[:n].reshape(shp))
        at += nr
    return out


def kernel(x, norm_pre, w_in, rel_bias, attn_sinks, lb_logits, hgrn_norm, w_branch_attn, w_branch_hgrn, w_out, norm_post, loss_target, m_norm_pre, m_w_in, m_rel_bias, m_attn_sinks, m_lb_logits, m_hgrn_norm, m_w_branch_attn, m_w_branch_hgrn, m_w_out, m_norm_post, v_norm_pre, v_w_in, v_rel_bias, v_attn_sinks, v_lb_logits, v_hgrn_norm, v_w_branch_attn, v_w_branch_hgrn, v_w_out, v_norm_post):
    B_loc, S, D = x.shape
    T = B_loc * S
    x2 = x.reshape(T, D)
    tgt2 = loss_target.reshape(T, D)
    my_x, my_y, my_c = lax.axis_index("x"), lax.axis_index("y"), lax.axis_index("c")

    c_arr = jnp.reshape(my_c, (1,)).astype(jnp.int32)
    s_arr = jnp.reshape(2 * my_x + my_y, (1,)).astype(jnp.int32)
    sc_arr = jnp.concatenate([s_arr, c_arr])
    shard_ws = [w_in[0], w_branch_attn[0], w_branch_hgrn[0], w_out[0]]
    shard_axes = (1, 1, 1, 0)
    names = ["w_in", "w_branch_attn", "w_branch_hgrn", "w_out"]
    placed = [_cast_into_full(w, ax, s_arr, "cast_" + nm) for w, ax, nm in zip(shard_ws, shard_axes, names)]
    peer_slabs = [jnp.reshape(t, (1,)).astype(jnp.int32)
                  for t in (2 * (1 - my_x) + my_y, 2 * my_x + 1 - my_y, 2 * (1 - my_x) + 1 - my_y)]

    buckets = _bucket_table()
    onehot = (buckets.reshape(-1)[:, None] == jnp.arange(REL_BUCKETS)[None, :]).astype(F32)
    bias_tab = _masked_bias_tables(
        jnp.dot(onehot, rel_bias.astype(F32), precision=lax.Precision.HIGHEST).T.reshape(ATTN_HEADS, WINDOW, 2 * WINDOW))
    sinks_b = jnp.broadcast_to(attn_sinks[0].astype(F32)[:, None, None], (ATTN_HEADS, 8, LANES))
    lb_fn = lambda l: jnp.cumsum(jax.nn.softmax(l.astype(F32), axis=0), axis=0)[:1]
    lb, lb_vjp = jax.vjp(lb_fn, lb_logits)
    gain_h = hgrn_norm[0].reshape(1, HGRN_WIDTH)

    h, rstd = _rmsnorm_fwd(x2, norm_pre)
    gather_in = _Gather(placed[:1], [shard_ws[0].shape], shard_axes[:1], "w_in")
    gather_in.start([0, 1])
    proj = _matmul_slab(h, gather_in.fulls[0], s_arr, None, gather_in.token, "in_proj_own")
    for j in range(N_CHIPS - 1):
        gather_in.wait([j], proj)
        gather_in.forward([j])
        after = None
        if j == 0:
            gather_in.start([2])
            after = gather_in.token
        if j == N_CHIPS - 2:
            gather_rest = _Gather(placed[1:], [w.shape for w in shard_ws[1:]], shard_axes[1:], "rest")
            gather_rest.start([0, 1, 2], gather_in.fulls[0])
            after = gather_rest.token
        proj = _matmul_slab(h, gather_in.fulls[0], peer_slabs[j], proj, after, "in_proj_peer%d" % j)
    win_f = gather_in.fulls[0]
    ya = _attn_fwd(proj, bias_tab, sinks_b, B_loc, S)
    yh, states = _hgrn_fwd(proj, lb, gain_h, B_loc, S)
    gather_rest.wait([0, 1, 2], yh)
    gather_rest.forward([0, 1, 2])
    wba_f, wbh_f, wout_f = gather_rest.fulls
    ua = _matmul(ya, wba_f, "nn", BF16, "branch_attn_proj")
    uh, merged = _branch_proj_merge(yh, wbh_f, proj, ua)
    yv = _matmul(merged, wout_f, "nn", F32, "out_proj")
    dy, dout, loss_p, gnpost_p = _post_loss(yv, x2, tgt2, norm_post)

    g_wout = _matmul(merged, dy, "tn", BF16, "out_proj_wgrad")
    d_ua, d_uh, dproj = _out_dgrad_merge_bwd(dy, wout_f, proj, ua, uh)
    g_wba = _matmul(ya, d_ua, "tn", BF16, "branch_attn_wgrad", slabs=N_CHIPS)
    g_wbh = _matmul(yh, d_uh, "tn", BF16, "branch_hgrn_wgrad", slabs=N_CHIPS)
    late = _reduce_start([g_wba, g_wbh, g_wout], names[1:], c_arr)
    d_ya = _matmul(d_ua, wba_f, "nt", BF16, "branch_attn_dgrad", after=late.token)
    d_yh = _matmul(d_uh, wbh_f, "nt", BF16, "branch_hgrn_dgrad", after=late.token)
    dproj, dbias_p, dsinks_p = _attn_bwd(proj, bias_tab, sinks_b, d_ya, dproj, B_loc, S)
    dproj, dlb_p, dgain_p = _hgrn_bwd(proj, states, lb, gain_h, d_yh, dproj, B_loc, S)
    g_win = _matmul(h, dproj, "tn", BF16, "in_proj_wgrad", slabs=N_CHIPS)
    last = _reduce_start([g_win], names[:1], c_arr)
    dh = _matmul(dproj, win_f, "nt", F32, "in_proj_dgrad", after=last.token)
    grad_x2, gnpre_p = _rmsnorm_bwd(dh, x2, rstd, norm_pre, dout)
    shared = _reduce_finish(last, grad_x2, sc_arr) + _reduce_finish(late, grad_x2, sc_arr)

    grelb_p = jnp.dot(dbias_p.reshape(ATTN_HEADS, -1), onehot, precision=lax.Precision.HIGHEST).T
    gsinks_p = dsinks_p[:, 0, 0]
    dlb_sum = jnp.sum(dlb_p, axis=0).reshape(1, HGRN_WIDTH)
    ghn_p = jnp.sum(dgain_p, axis=0).reshape(HGRN_HEADS, HGRN_DIM)
    small_parts = [gnpre_p, gnpost_p, grelb_p, gsinks_p, dlb_sum, ghn_p, loss_p]
    small_shapes = [p.shape for p in small_parts]
    pack_sum = _sum_small(_pack_rows(small_parts))
    big_w = [w_in, w_branch_attn, w_branch_hgrn, w_out]
    big_m = [m_w_in, m_w_branch_attn, m_w_branch_hgrn, m_w_out]
    big_v = [v_w_in, v_w_branch_attn, v_w_branch_hgrn, v_w_out]
    big = {}
    for nm, gs, w, m, v in zip(names, shared, big_w, big_m, big_v):
        shp = w.shape
        g2 = gs.reshape(shp[1], shp[2])
        d, nm_, nv_, g_out = _adamw(w[0], g2, m[0], v[0], "adamw_" + nm)
        big[nm] = tuple(a.reshape(shp) for a in (g_out, d, nm_, nv_))

    gnpre, gnpost, grelb, gsinks, dlb, ghn, loss = _unpack_rows(pack_sum, small_shapes)
    (g_lb_logits,) = lb_vjp(dlb)
    small_names = ["norm_pre", "rel_bias", "attn_sinks", "lb_logits", "hgrn_norm", "norm_post"]
    small_w = [norm_pre, rel_bias, attn_sinks, lb_logits, hgrn_norm, norm_post]
    small_m = [m_norm_pre, m_rel_bias, m_attn_sinks, m_lb_logits, m_hgrn_norm, m_norm_post]
    small_v = [v_norm_pre, v_rel_bias, v_attn_sinks, v_lb_logits, v_hgrn_norm, v_norm_post]
    small_g = [gnpre.reshape(norm_pre.shape), grelb.reshape(rel_bias.shape), gsinks.reshape(attn_sinks.shape),
               g_lb_logits.reshape(lb_logits.shape), ghn.reshape(hgrn_norm.shape), gnpost.reshape(norm_post.shape)]
    shapes = [w.shape for w in small_w]
    d_s, nm_s, nv_s, _ = _adamw(_pack_rows(small_w), _pack_rows(small_g), _pack_rows(small_m), _pack_rows(small_v),
                             "adamw_small")
    small = {}
    for nm, g, d, m_, v_ in zip(small_names, small_g, _unpack_rows(d_s, shapes), _unpack_rows(nm_s, shapes),
                                _unpack_rows(nv_s, shapes)):
        small[nm] = (g, d, m_, v_)

    allw = {**big, **small}
    order = ["norm_pre", "w_in", "rel_bias", "attn_sinks", "lb_logits", "hgrn_norm", "w_branch_attn", "w_branch_hgrn",
             "w_out", "norm_post"]
    outs = [loss.reshape(()), grad_x2.reshape(B_loc, S, D)]
    for k in range(4):
        outs += [allw[nm][k] for nm in order]
    return tuple(outs)
```

```python
import functools
import math

import jax
import jax.numpy as jnp
from jax import lax
from jax.experimental import pallas as pl
from jax.experimental.pallas import tpu as pltpu

F32 = jnp.float32
BF16 = jnp.bfloat16
MESH = pl.DeviceIdType.MESH

ATTN_HEADS = 16
ATTN_KV_HEADS = 4
HEAD_DIM = 64
GROUP = ATTN_HEADS // ATTN_KV_HEADS
WINDOW = 128
ATTN_WIDTH = ATTN_HEADS * HEAD_DIM
KV_WIDTH = ATTN_KV_HEADS * HEAD_DIM
HGRN_HEADS = 8
HGRN_DIM = 128
HGRN_WIDTH = HGRN_HEADS * HGRN_DIM
CHUNK = 64
SUB = 16
NSUB = CHUNK // SUB
REL_BUCKETS = 32
REL_MAX_DIST = 128
NORM_EPS = 1e-6
ADAM_LR = 0.001
ADAM_B1 = 0.9
ADAM_B2 = 0.999
ADAM_EPS = 1e-08
ADAM_WD = 0.01
ADAM_STEP = 10
LANES = 128
N_CHIPS = 4
N_DEV = 8
VMEM_LIMIT = 48 * 1024 * 1024
MATMUL_OPERAND_BYTES = 34 * 1024 * 1024
MATMUL_VMEM_BYTES = 44 * 1024 * 1024

OFF_AQ = 0
OFF_AK = OFF_AQ + ATTN_WIDTH
OFF_AV = OFF_AK + KV_WIDTH
OFF_AG = OFF_AV + KV_WIDTH
OFF_HQ = OFF_AG + ATTN_WIDTH
OFF_HF = OFF_HQ + HGRN_WIDTH
OFF_HI = OFF_HF + HGRN_WIDTH
OFF_HG = OFF_HI + HGRN_WIDTH
OFF_GA = OFF_HG + HGRN_WIDTH

NT_DIMS = (((1,), (1,)), ((), ()))
TN_DIMS = (((0,), (0,)), ((), ()))
NN_DIMS = (((1,), (0,)), ((), ()))


def _pick(n, cands):
    for c in cands:
        if n % c == 0:
            return c
    raise ValueError(f"no tile for {n} in {cands}")


def _params(sem):
    return pltpu.CompilerParams(dimension_semantics=sem, vmem_limit_bytes=VMEM_LIMIT)


def _bdot(a, b, dims=NN_DIMS):
    return lax.dot_general(a.astype(BF16), b.astype(BF16), dims, preferred_element_type=F32)


def _matmul(a, b, mode, out_dtype, name, slabs=1, after=None):
    if mode == "nn":
        (M, K), (K2, N) = a.shape, b.shape
    elif mode == "nt":
        (M, K), (N, K2) = a.shape, b.shape
    else:
        (K, M), (K2, N) = a.shape, b.shape
    assert K == K2
    nslab = N // slabs
    tm = _pick(M, (1024, 512, 256, 128))
    out_bytes = jnp.dtype(out_dtype).itemsize
    choices = []
    for tn in (2688, 1024, 896, 512, 384, 256, 128):
        for tk in (4096, 3584, 2048, 1792, 1536, 1024, 512, 256, 128):
            acc = 0 if tk == K else 4 * tm * tn
            if (nslab % tn == 0 and K % tk == 0 and 4 * tk * (tm + tn) <= MATMUL_OPERAND_BYTES
                    and 4 * tk * (tm + tn) + 2 * out_bytes * tm * tn + acc <= MATMUL_VMEM_BYTES):
                choices.append((K // tk > 1, -tn, tn, tk))
                break
    _, _, tn, tk = min(choices)
    nk = K // tk
    per = nslab // tn
    dims = {"nn": NN_DIMS, "nt": NT_DIMS, "tn": TN_DIMS}[mode]

    n_in = 2 if after is None else 3

    def body(*refs):
        a_ref, b_ref, o_ref, acc = refs[0], refs[1], refs[n_in], refs[n_in + 1:]
        part = lax.dot_general(a_ref[...], b_ref[...], dims, preferred_element_type=F32)
        if nk == 1:
            o_ref[...] = part.astype(o_ref.dtype)
            return
        acc_ref, = acc
        k = pl.program_id(2)

        @pl.when(k == 0)
        def _():
            acc_ref[...] = part

        @pl.when((k > 0) & (k < nk - 1))
        def _():
            acc_ref[...] += part

        @pl.when(k == nk - 1)
        def _():
            o_ref[...] = (acc_ref[...] + part).astype(o_ref.dtype)

    if mode == "tn":
        a_spec = pl.BlockSpec((tk, tm), lambda i, j, k: (k, i))
    else:
        a_spec = pl.BlockSpec((tm, tk), lambda i, j, k: (i, k))
    if mode == "nt":
        b_spec = pl.BlockSpec((tn, tk), lambda i, j, k: (j, k))
    else:
        b_spec = pl.BlockSpec((tk, tn), lambda i, j, k: (k, j))
    if slabs == 1:
        o_shape = jax.ShapeDtypeStruct((M, N), out_dtype)
        o_spec = pl.BlockSpec((tm, tn), lambda i, j, k: (i, j))
    else:
        o_shape = jax.ShapeDtypeStruct((slabs, M, nslab), out_dtype)
        o_spec = pl.BlockSpec((None, tm, tn), lambda i, j, k: (j // per, i, j % per))
    return pl.pallas_call(
        body, name=name, grid=(M // tm, N // tn, nk), in_specs=[a_spec, b_spec] + ([] if after is None else [ANY]),
        out_specs=o_spec, out_shape=o_shape,
        scratch_shapes=[pltpu.VMEM((tm, tn), F32)] if nk > 1 else [],
        compiler_params=_params(("parallel", "parallel", "arbitrary")),
    )(*((a, b) if after is None else (a, b, after)))


def _rmsnorm_fwd(x2, gain):
    T, D = x2.shape
    tr = _pick(T, (256, 128))

    def body(x_ref, g_ref, h_ref, r_ref):
        xv = x_ref[...]
        r = lax.rsqrt(jnp.mean(xv * xv, axis=-1, keepdims=True) + NORM_EPS)
        h_ref[...] = (xv * r * g_ref[...]).astype(BF16)
        r_ref[...] = r

    return pl.pallas_call(
        body, name="rmsnorm_pre_fwd", grid=(T // tr,),
        in_specs=[pl.BlockSpec((tr, D), lambda i: (i, 0)), pl.BlockSpec((1, D), lambda i: (0, 0))],
        out_specs=[pl.BlockSpec((tr, D), lambda i: (i, 0)), pl.BlockSpec((tr, 1), lambda i: (i, 0))],
        out_shape=[jax.ShapeDtypeStruct((T, D), BF16), jax.ShapeDtypeStruct((T, 1), F32)],
        compiler_params=_params(("parallel",)),
    )(x2, gain)


def _rmsnorm_bwd(dh, x2, rstd, gain, dout):
    T, D = x2.shape
    tr = _pick(T, (256, 128))

    def body(dh_ref, x_ref, r_ref, g_ref, do_ref, gx_ref, gg_ref):
        @pl.when(pl.program_id(0) == 0)
        def _():
            gg_ref[...] = jnp.zeros_like(gg_ref)

        n = x_ref[...] * r_ref[...]
        dhv = dh_ref[...]
        dn = dhv * g_ref[...]
        gx_ref[...] = do_ref[...] + r_ref[...] * (dn - n * jnp.mean(dn * n, axis=-1, keepdims=True))
        gg_ref[...] += jnp.sum(dhv * n, axis=0, keepdims=True)

    row = pl.BlockSpec((tr, D), lambda i: (i, 0))
    vec = pl.BlockSpec((1, D), lambda i: (0, 0))
    return pl.pallas_call(
        body, name="rmsnorm_pre_bwd", grid=(T // tr,),
        in_specs=[row, row, pl.BlockSpec((tr, 1), lambda i: (i, 0)), vec, row],
        out_specs=[row, vec],
        out_shape=[jax.ShapeDtypeStruct((T, D), F32), jax.ShapeDtypeStruct((1, D), F32)],
        compiler_params=_params(("arbitrary",)),
    )(dh, x2, rstd, gain, dout)


def _post_loss(yv, x2, tgt2, gain):
    T, D = x2.shape
    tr = _pick(T, (256, 128))

    def body(y_ref, x_ref, t_ref, g_ref, dy_ref, do_ref, loss_ref, gg_ref):
        @pl.when(pl.program_id(0) == 0)
        def _():
            gg_ref[...] = jnp.zeros_like(gg_ref)
            loss_ref[...] = jnp.zeros_like(loss_ref)

        yv_ = y_ref[...]
        r = lax.rsqrt(jnp.mean(yv_ * yv_, axis=-1, keepdims=True) + NORM_EPS)
        n = yv_ * r
        e = (x_ref[...] + n * g_ref[...]) - t_ref[...]
        loss_ref[...] += 0.5 * jnp.sum(jnp.mean(e * e, axis=-1, keepdims=True), axis=0, keepdims=True)
        dz = e / D
        do_ref[...] = dz
        gg_ref[...] += jnp.sum(dz * n, axis=0, keepdims=True)
        dn = dz * g_ref[...]
        dy_ref[...] = (r * (dn - n * jnp.mean(dn * n, axis=-1, keepdims=True))).astype(BF16)

    row = pl.BlockSpec((tr, D), lambda i: (i, 0))
    vec = pl.BlockSpec((1, D), lambda i: (0, 0))
    return pl.pallas_call(
        body, name="post_norm_loss", grid=(T // tr,),
        in_specs=[row, row, row, vec],
        out_specs=[row, row, pl.BlockSpec((1, 1), lambda i: (0, 0)), vec],
        out_shape=[jax.ShapeDtypeStruct((T, D), BF16), jax.ShapeDtypeStruct((T, D), F32),
                   jax.ShapeDtypeStruct((1, 1), F32), jax.ShapeDtypeStruct((1, D), F32)],
        compiler_params=_params(("arbitrary",)),
    )(yv, x2, tgt2, gain)


def _window(rows, cols, at):
    return pl.BlockSpec((pl.Element(rows), pl.Element(cols)), at)


def _gate_windows(tm, tn, D):
    return [_window(tm, tn, lambda i, j: (i * tm, pl.multiple_of(OFF_GA + j * tn, LANES))),
            _window(tm, tn, lambda i, j: (i * tm, pl.multiple_of(OFF_GA + D + j * tn, LANES)))]


def _branch_proj_merge(yh, wbh, proj, ua):
    T, K = yh.shape
    D = wbh.shape[1]
    tm, tn = _pick(T, (512, 256, 128)), _pick(D, (1024, 512, 256))

    def body(a_ref, b_ref, ga_ref, gh_ref, ua_ref, uh_ref, m_ref):
        uh = jnp.dot(a_ref[...], b_ref[...], preferred_element_type=F32).astype(BF16)
        uh_ref[...] = uh
        m_ref[...] = (jax.nn.sigmoid(ga_ref[...]) * ua_ref[...].astype(F32)
                      + jax.nn.sigmoid(gh_ref[...]) * uh.astype(F32)).astype(BF16)

    blk = pl.BlockSpec((tm, tn), lambda i, j: (i, j))
    o = jax.ShapeDtypeStruct((T, D), BF16)
    return pl.pallas_call(
        body, name="branch_hgrn_proj_merge", grid=(T // tm, D // tn),
        in_specs=[pl.BlockSpec((tm, K), lambda i, j: (i, 0)), pl.BlockSpec((K, tn), lambda i, j: (0, j))]
        + _gate_windows(tm, tn, D) + [blk],
        out_specs=[blk, blk], out_shape=[o, o],
        compiler_params=_params(("parallel", "arbitrary")),
    )(yh, wbh, proj, proj, ua)


def _out_dgrad_merge_bwd(dy, wout, proj, ua, uh):
    T, K = dy.shape
    D = wout.shape[0]
    tm, tn = _pick(T, (512, 256, 128)), _pick(D, (1024, 512, 256))
    nj = D // tn

    def body(a_ref, b_ref, ga_ref, gh_ref, ua_ref, uh_ref, dua_ref, duh_ref, dproj_ref):
        d = lax.dot_general(a_ref[...], b_ref[...], NT_DIMS, preferred_element_type=F32)
        sa = jax.nn.sigmoid(ga_ref[...])
        sh = jax.nn.sigmoid(gh_ref[...])
        dua_ref[...] = (d * sa).astype(BF16)
        duh_ref[...] = (d * sh).astype(BF16)
        dga = (d * ua_ref[...].astype(F32) * sa * (1.0 - sa)).astype(BF16)
        dgh = (d * uh_ref[...].astype(F32) * sh * (1.0 - sh)).astype(BF16)
        for jj in range(nj):
            @pl.when(pl.program_id(1) == jj)
            def _():
                dproj_ref[:, jj * tn:(jj + 1) * tn] = dga
                dproj_ref[:, D + jj * tn:D + (jj + 1) * tn] = dgh

    blk = pl.BlockSpec((tm, tn), lambda i, j: (i, j))
    o = jax.ShapeDtypeStruct((T, D), BF16)
    return pl.pallas_call(
        body, name="out_proj_dgrad_merge_bwd", grid=(T // tm, nj),
        in_specs=[pl.BlockSpec((tm, K), lambda i, j: (i, 0)), pl.BlockSpec((tn, K), lambda i, j: (j, 0))]
        + _gate_windows(tm, tn, D) + [blk, blk],
        out_specs=[blk, blk, _window(tm, 2 * D, lambda i, j: (i * tm, OFF_GA))],
        out_shape=[o, o, jax.ShapeDtypeStruct((T, proj.shape[1]), BF16)],
        compiler_params=_params(("parallel", "arbitrary")),
    )(dy, wout, proj, proj, ua, uh)


KV_PAIR = 2
PAIR_HEADS = KV_PAIR * GROUP


def _attn_mask(n):
    qi = lax.broadcasted_iota(jnp.int32, (WINDOW, 2 * WINDOW), 0)
    si = lax.broadcasted_iota(jnp.int32, (WINDOW, 2 * WINDOW), 1)
    dist = qi + WINDOW - si
    return (dist >= 0) & (dist < WINDOW) & ((si >= WINDOW) | (n > 0))


def _attn_group_fn(mask):
    mask = jnp.concatenate([mask] * GROUP, axis=0)

    def f(q, k, v, ag, bias, sink):
        s = lax.dot_general(q.astype(BF16), k.astype(BF16), NT_DIMS, preferred_element_type=F32) * (HEAD_DIM ** -0.5)
        s = jnp.where(mask, s + bias, -1e30)
        m = lax.stop_gradient(jnp.maximum(jnp.max(s, axis=-1, keepdims=True), sink))
        p = jnp.exp(s - m)
        den = jnp.sum(p, axis=-1, keepdims=True) + jnp.exp(sink - m)
        o = jnp.dot(p.astype(BF16), v.astype(BF16), preferred_element_type=F32)
        return o * (jax.nn.silu(ag) / den)

    return f


def _attn_specs(B_loc, nb, order):
    qw = PAIR_HEADS * HEAD_DIM
    kw = KV_PAIR * HEAD_DIM

    def rows(g):
        b, p, n = order(*g)
        return b * nb + n

    def prev(g):
        b, p, n = order(*g)
        return b * nb + jnp.maximum(n - 1, 0)

    def pp(g):
        return order(*g)[1]

    q = pl.BlockSpec((WINDOW, qw), lambda *g: (rows(g), OFF_AQ // qw + pp(g)))
    kc = pl.BlockSpec((WINDOW, kw), lambda *g: (rows(g), OFF_AK // kw + pp(g)))
    kp = pl.BlockSpec((WINDOW, kw), lambda *g: (prev(g), OFF_AK // kw + pp(g)))
    vc = pl.BlockSpec((WINDOW, kw), lambda *g: (rows(g), OFF_AV // kw + pp(g)))
    vp = pl.BlockSpec((WINDOW, kw), lambda *g: (prev(g), OFF_AV // kw + pp(g)))
    ag = pl.BlockSpec((WINDOW, qw), lambda *g: (rows(g), OFF_AG // qw + pp(g)))
    bias = pl.BlockSpec((PAIR_HEADS, WINDOW, 2 * WINDOW), lambda *g: (pp(g), 0, 0))
    sink = pl.BlockSpec((PAIR_HEADS, 8, LANES), lambda *g: (pp(g), 0, 0))
    return [q, kc, kp, vc, vp, ag, bias, sink], rows, pp


def _attn_operands(q_ref, kc_ref, kp_ref, vc_ref, vp_ref, ag_ref, bias_ref, sink_ref, j):
    lo, hi = j * HEAD_DIM, (j + 1) * HEAD_DIM
    k = jnp.concatenate([kp_ref[:, lo:hi], kc_ref[:, lo:hi]], axis=0)
    v = jnp.concatenate([vp_ref[:, lo:hi], vc_ref[:, lo:hi]], axis=0)
    heads = [j * GROUP + g for g in range(GROUP)]
    q = jnp.concatenate([q_ref[:, h * HEAD_DIM:(h + 1) * HEAD_DIM] for h in heads], axis=0)
    ag = jnp.concatenate([ag_ref[:, h * HEAD_DIM:(h + 1) * HEAD_DIM] for h in heads], axis=0)
    bias = jnp.concatenate([bias_ref[h] for h in heads], axis=0)
    sink = jnp.concatenate([jnp.broadcast_to(sink_ref[h, 0:1, 0:1], (WINDOW, 1)) for h in heads], axis=0)
    return q, k, v, ag, bias, sink


def _attn_fwd(proj, bias_tab, sinks_b, B_loc, S):
    T = B_loc * S
    nb = S // WINDOW
    n_pairs = ATTN_KV_HEADS // KV_PAIR
    in_specs, rows, pp = _attn_specs(B_loc, nb, lambda b, p, n: (b, p, n))

    def body(q_ref, kc_ref, kp_ref, vc_ref, vp_ref, ag_ref, bias_ref, sink_ref, ya_ref):
        f = _attn_group_fn(_attn_mask(pl.program_id(2)))
        for j in range(KV_PAIR):
            out = f(*_attn_operands(q_ref, kc_ref, kp_ref, vc_ref, vp_ref, ag_ref, bias_ref, sink_ref, j))
            for g in range(GROUP):
                h = j * GROUP + g
                ya_ref[:, h * HEAD_DIM:(h + 1) * HEAD_DIM] = out[g * WINDOW:(g + 1) * WINDOW].astype(BF16)

    return pl.pallas_call(
        body, name="attn_fwd", grid=(B_loc, n_pairs, nb), in_specs=in_specs,
        out_specs=pl.BlockSpec((WINDOW, PAIR_HEADS * HEAD_DIM), lambda *g: (rows(g), pp(g))),
        out_shape=jax.ShapeDtypeStruct((T, ATTN_WIDTH), BF16),
        compiler_params=_params(("parallel", "parallel", "parallel")),
    )(proj, proj, proj, proj, proj, proj, bias_tab, sinks_b)


def _attn_bwd(proj, bias_tab, sinks_b, dya, dproj, B_loc, S):
    nb = S // WINDOW
    n_pairs = ATTN_KV_HEADS // KV_PAIR
    qw, kw = PAIR_HEADS * HEAD_DIM, KV_PAIR * HEAD_DIM

    def at(off, width, back=0):
        return lambda b, i, p: ((b * nb + jnp.maximum(nb - 1 - i - back, 0)) * WINDOW,
                                pl.multiple_of(off + p * width, LANES))

    in_specs = [_window(WINDOW, qw, at(OFF_AQ, qw)), _window(WINDOW, kw, at(OFF_AK, kw)),
                _window(WINDOW, kw, at(OFF_AK, kw, 1)), _window(WINDOW, kw, at(OFF_AV, kw)),
                _window(WINDOW, kw, at(OFF_AV, kw, 1)), _window(WINDOW, qw, at(OFF_AG, qw)),
                pl.BlockSpec((PAIR_HEADS, WINDOW, 2 * WINDOW), lambda b, i, p: (p, 0, 0)),
                pl.BlockSpec((PAIR_HEADS, 8, LANES), lambda b, i, p: (p, 0, 0)),
                pl.BlockSpec((WINDOW, qw), lambda b, i, p: (b * nb + nb - 1 - i, p)), ANY]

    def body(q_ref, kc_ref, kp_ref, vc_ref, vp_ref, ag_ref, bias_ref, sink_ref, dya_ref, dproj_in,
             dproj_ref, dbias_ref, dsink_ref, dkc_ref, dvc_ref):
        b, i, p = pl.program_id(0), pl.program_id(1), pl.program_id(2)
        n = nb - 1 - i

        @pl.when((b == 0) & (i == 0) & (p == 0))
        def _():
            dbias_ref[...] = jnp.zeros_like(dbias_ref)
            dsink_ref[...] = jnp.zeros_like(dsink_ref)

        @pl.when(i == 0)
        def _():
            dkc_ref[p] = jnp.zeros((WINDOW, kw), F32)
            dvc_ref[p] = jnp.zeros((WINDOW, kw), F32)

        f = _attn_group_fn(_attn_mask(n))
        dk_carry, dv_carry = dkc_ref[p], dvc_ref[p]
        dqs, dags, dbiases, dsinks, dks, dvs = [], [], [], [], [], []
        for j in range(KV_PAIR):
            ops = _attn_operands(q_ref, kc_ref, kp_ref, vc_ref, vp_ref, ag_ref, bias_ref, sink_ref, j)
            _, vjp = jax.vjp(f, *ops)
            dout = jnp.concatenate([dya_ref[:, (j * GROUP + g) * HEAD_DIM:(j * GROUP + g + 1) * HEAD_DIM].astype(F32)
                                    for g in range(GROUP)], axis=0)
            dq, dk, dv, dag, dbias, dsink = vjp(dout)
            lo, hi = j * HEAD_DIM, (j + 1) * HEAD_DIM
            dks.append((dk[WINDOW:] + dk_carry[:, lo:hi]).astype(BF16))
            dvs.append((dv[WINDOW:] + dv_carry[:, lo:hi]).astype(BF16))
            dkc_ref[p, :, lo:hi] = dk[:WINDOW]
            dvc_ref[p, :, lo:hi] = dv[:WINDOW]
            for g in range(GROUP):
                blk = slice(g * WINDOW, (g + 1) * WINDOW)
                dqs.append(dq[blk].astype(BF16))
                dags.append(dag[blk].astype(BF16))
                dbiases.append(dbias[blk])
                dsinks.append(jnp.broadcast_to(jnp.sum(dsink[blk], axis=0, keepdims=True), (8, LANES)))

        for pair in range(n_pairs):
            @pl.when(p == pair)
            def _():
                for j in range(KV_PAIR):
                    col = (pair * KV_PAIR + j) * HEAD_DIM
                    dproj_ref[:, OFF_AK + col:OFF_AK + col + HEAD_DIM] = dks[j]
                    dproj_ref[:, OFF_AV + col:OFF_AV + col + HEAD_DIM] = dvs[j]
                for hh in range(PAIR_HEADS):
                    h = pair * PAIR_HEADS + hh
                    dproj_ref[:, OFF_AQ + h * HEAD_DIM:OFF_AQ + (h + 1) * HEAD_DIM] = dqs[hh]
                    dproj_ref[:, OFF_AG + h * HEAD_DIM:OFF_AG + (h + 1) * HEAD_DIM] = dags[hh]
                    dbias_ref[h] += dbiases[hh]
                    dsink_ref[h] += dsinks[hh]

    return pl.pallas_call(
        body, name="attn_bwd", grid=(B_loc, nb, n_pairs), in_specs=in_specs,
        out_specs=[_window(WINDOW, OFF_HQ, lambda b, i, p: ((b * nb + nb - 1 - i) * WINDOW, 0)),
                   pl.BlockSpec((ATTN_HEADS, WINDOW, 2 * WINDOW), lambda b, i, p: (0, 0, 0)),
                   pl.BlockSpec((ATTN_HEADS, 8, LANES), lambda b, i, p: (0, 0, 0))],
        out_shape=[jax.ShapeDtypeStruct(dproj.shape, dproj.dtype),
                   jax.ShapeDtypeStruct((ATTN_HEADS, WINDOW, 2 * WINDOW), F32),
                   jax.ShapeDtypeStruct((ATTN_HEADS, 8, LANES), F32)],
        input_output_aliases={9: 0},
        scratch_shapes=[pltpu.VMEM((n_pairs, WINDOW, kw), F32), pltpu.VMEM((n_pairs, WINDOW, kw), F32)],
        compiler_params=_params(("arbitrary", "arbitrary", "arbitrary")),
    )(proj, proj, proj, proj, proj, proj, bias_tab, sinks_b, dya, dproj)


class _HgrnPre:
    def __init__(self, fr, qr, lb, g_scr):
        t = lax.broadcasted_iota(jnp.int32, (CHUNK, CHUNK), 0)
        s = lax.broadcasted_iota(jnp.int32, (CHUNK, CHUNK), 1)
        self.sg = jax.nn.sigmoid(fr)
        self.f = lb + (1.0 - lb) * self.sg
        g = jnp.dot((t >= s).astype(F32), jnp.log(self.f), precision=lax.Precision.HIGHEST, preferred_element_type=F32)
        g_scr[...] = g
        self.g = g
        self.row = lax.broadcasted_iota(jnp.int32, g.shape, 0)
        self.refs = [jnp.zeros((1, g.shape[1]), F32)] + [g_scr[pl.ds(i * SUB - 1, 1), :] for i in range(1, NSUB)]
        self.gend = g_scr[pl.ds(CHUNK - 1, 1), :]
        refrow = jnp.zeros_like(g)
        for i in range(1, NSUB):
            refrow = jnp.where(self.row >= i * SUB, self.refs[i], refrow)
        self.sigq = jax.nn.sigmoid(qr)
        self.qs = qr * self.sigq
        self.k = 1.0 - self.f
        self.eg = jnp.exp(g)
        self.eqd = jnp.exp(g - refrow)
        self.ekd = [jnp.exp(jnp.where(self.row < (i + 1) * SUB, self.refs[i] - g, 0.0)) for i in range(NSUB)]
        self.ekdec = jnp.exp(self.gend - g)
        self.qg = self.qs * self.eg
        self.qd = self.qs * self.eqd
        self.kd = [self.k * e for e in self.ekd]
        self.kdec = self.k * self.ekdec
        self.egend = jnp.exp(self.gend)


def _hgrn_pair_mask():
    t = lax.broadcasted_iota(jnp.int32, (CHUNK, NSUB * CHUNK), 0)
    col = lax.broadcasted_iota(jnp.int32, (CHUNK, NSUB * CHUNK), 1)
    return ((t // SUB) == (col // CHUNK)) & ((col % CHUNK) <= t)


def _hgrn_head_out(p, lanes, state_t, v, mask):
    qg, qd = p.qg[:, lanes], p.qd[:, lanes]
    kall = jnp.concatenate([kd[:, lanes] for kd in p.kd], axis=0)
    vst = jnp.concatenate([v] * NSUB, axis=0)
    am = jnp.where(mask, _bdot(qd, kall, NT_DIMS), 0.0)
    o = _bdot(qg, state_t, NT_DIMS) + _bdot(am, vst)
    return o, (qg, qd, kall, am, vst)


def _hgrn_fwd(proj, lb, gain, B_loc, S):
    T = B_loc * S
    nc = S // CHUNK
    nh = HGRN_HEADS

    def at(off):
        return lambda b, n: ((b * nc + n) * CHUNK, off)

    vec = pl.BlockSpec((1, HGRN_WIDTH), lambda b, n: (0, 0))

    def body(q_ref, f_ref, v_ref, hg_ref, lb_ref, gain_ref, yh_ref, st_ref, state_scr, g_scr):
        @pl.when(pl.program_id(1) == 0)
        def _():
            state_scr[...] = jnp.zeros_like(state_scr)

        p = _HgrnPre(f_ref[...], q_ref[...], lb_ref[...], g_scr)
        v = v_ref[...]
        gate = gain_ref[...] * jax.nn.silu(hg_ref[...])
        mask = _hgrn_pair_mask()
        for hd in range(nh):
            lanes = slice(hd * HGRN_DIM, (hd + 1) * HGRN_DIM)
            st = state_scr[hd]
            st_ref[hd] = st
            o, _ = _hgrn_head_out(p, lanes, st, v[:, lanes], mask)
            rs = lax.rsqrt(jnp.mean(o * o, axis=-1, keepdims=True) + NORM_EPS)
            yh_ref[:, lanes] = (o * rs * gate[:, lanes]).astype(BF16)
            state_scr[hd] = st * p.egend[:, lanes] + _bdot(v[:, lanes], p.kdec[:, lanes], TN_DIMS)

    return pl.pallas_call(
        body, name="hgrn_fwd", grid=(B_loc, nc),
        in_specs=[_window(CHUNK, HGRN_WIDTH, at(OFF_HQ)), _window(CHUNK, HGRN_WIDTH, at(OFF_HF)),
                  _window(CHUNK, HGRN_WIDTH, at(OFF_HI)), _window(CHUNK, HGRN_WIDTH, at(OFF_HG)), vec, vec],
        out_specs=[pl.BlockSpec((CHUNK, HGRN_WIDTH), lambda b, n: (b * nc + n, 0)),
                   pl.BlockSpec((None, None, nh, HGRN_DIM, HGRN_DIM), lambda b, n: (b, n, 0, 0, 0))],
        out_shape=[jax.ShapeDtypeStruct((T, HGRN_WIDTH), BF16),
                   jax.ShapeDtypeStruct((B_loc, nc, nh, HGRN_DIM, HGRN_DIM), F32)],
        scratch_shapes=[pltpu.VMEM((nh, HGRN_DIM, HGRN_DIM), F32), pltpu.VMEM((CHUNK, HGRN_WIDTH), F32)],
        compiler_params=_params(("parallel", "arbitrary")),
    )(proj, proj, proj, proj, lb, gain)


def _hgrn_bwd(proj, states, lb, gain, dyh, dproj, B_loc, S):
    nc = S // CHUNK
    nh = HGRN_HEADS

    def at(off):
        return lambda b, i: ((b * nc + nc - 1 - i) * CHUNK, off)

    vec = pl.BlockSpec((1, HGRN_WIDTH), lambda b, i: (0, 0))
    in_specs = [_window(CHUNK, HGRN_WIDTH, at(OFF_HQ)), _window(CHUNK, HGRN_WIDTH, at(OFF_HF)),
                _window(CHUNK, HGRN_WIDTH, at(OFF_HI)), _window(CHUNK, HGRN_WIDTH, at(OFF_HG)), vec, vec,
                pl.BlockSpec((None, None, nh, HGRN_DIM, HGRN_DIM), lambda b, i: (b, nc - 1 - i, 0, 0, 0)),
                pl.BlockSpec((CHUNK, HGRN_WIDTH), lambda b, i: (b * nc + nc - 1 - i, 0)), ANY]
    acc_spec = pl.BlockSpec((None, 1, HGRN_WIDTH), lambda b, i: (b, 0, 0))

    def body(q_ref, f_ref, v_ref, hg_ref, lb_ref, gain_ref, st_ref, dyh_ref, dproj_in,
             dproj_ref, dlb_ref, dgain_ref, dstate_scr, g_scr, dg_scr):
        dq_ref, df_ref, dv_ref, dhg_ref = [dproj_ref.at[:, pl.ds(k * HGRN_WIDTH, HGRN_WIDTH)] for k in range(4)]

        @pl.when(pl.program_id(1) == 0)
        def _():
            dstate_scr[...] = jnp.zeros_like(dstate_scr)
            dlb_ref[...] = jnp.zeros_like(dlb_ref)
            dgain_ref[...] = jnp.zeros_like(dgain_ref)

        qr, lb, gain, hg, v = q_ref[...], lb_ref[...], gain_ref[...], hg_ref[...], v_ref[...]
        p = _HgrnPre(f_ref[...], qr, lb, g_scr)
        sgh = jax.nn.sigmoid(hg)
        sil = hg * sgh
        dy = dyh_ref[...].astype(F32)
        mask = _hgrn_pair_mask()
        dqg, dqd, dkdec, dv, dhg, dgend, dgain = [], [], [], [], [], [], []
        dkd = [[] for _ in range(NSUB)]
        heads = [slice(hd * HGRN_DIM, (hd + 1) * HGRN_DIM) for hd in range(nh)]
        sts = [st_ref[hd] for hd in range(nh)]
        dnews = [dstate_scr[hd] for hd in range(nh)]
        fwd = [_hgrn_head_out(p, lanes, st, v[:, lanes], mask) for lanes, st in zip(heads, sts)]
        for lanes, st, dnew in zip(heads, sts, dnews):
            dkdec_h = _bdot(v[:, lanes], dnew)
            dkdec.append(dkdec_h)
            dgend.append(jnp.sum(dkdec_h * p.kdec[:, lanes], axis=0, keepdims=True)
                         + jnp.sum(dnew * st, axis=0, keepdims=True) * p.egend[:, lanes])
        dos = []
        gate_grad = sgh * (1.0 + hg * (1.0 - sgh))
        for lanes, (o, _) in zip(heads, fwd):
            rs = lax.rsqrt(jnp.mean(o * o, axis=-1, keepdims=True) + NORM_EPS)
            n = o * rs
            dyn = dy[:, lanes] * n
            dgain.append(jnp.sum(dyn * sil[:, lanes], axis=0, keepdims=True))
            dhg.append(dyn * gain[:, lanes] * gate_grad[:, lanes])
            dn = dy[:, lanes] * gain[:, lanes] * sil[:, lanes]
            dos.append(rs * (dn - n * jnp.mean(dn * n, axis=-1, keepdims=True)))
        drs = []
        for hd, (lanes, st, dnew, do, (_, (qg, qd, kall, am, vst))) in enumerate(zip(heads, sts, dnews, dos, fwd)):
            dqg.append(_bdot(do, st))
            dstate_scr[hd] = _bdot(do, qg, TN_DIMS) + dnew * p.egend[:, lanes]
            drs.append(jnp.where(mask, _bdot(do, vst, NT_DIMS), 0.0))
            dvst = _bdot(am, do, TN_DIMS)
            dv.append(sum(dvst[i * CHUNK:(i + 1) * CHUNK] for i in range(NSUB)) + _bdot(p.kdec[:, lanes], dnew, NT_DIMS))
        for dr, (_, (qg, qd, kall, am, vst)) in zip(drs, fwd):
            dqd.append(_bdot(dr, kall))
            dkall = _bdot(dr, qd, TN_DIMS)
            for i in range(NSUB):
                dkd[i].append(dkall[i * CHUNK:(i + 1) * CHUNK])

        wide = lambda parts: jnp.concatenate(parts, axis=1)
        dqg, dqd, dkdec = wide(dqg), wide(dqd), wide(dkdec)
        t2 = dqd * p.qd
        dg = dqg * p.qg + t2 - dkdec * p.kdec
        dk = dkdec * p.ekdec
        dg_scr[...] = jnp.zeros_like(dg_scr)
        for i in range(NSUB):
            dkd_i = wide(dkd[i])
            tk = jnp.where(p.row < (i + 1) * SUB, dkd_i * p.kd[i], 0.0)
            dg = dg - tk
            dk = dk + dkd_i * p.ekd[i]
            if i >= 1:
                in_blk = (p.row >= i * SUB) & (p.row < (i + 1) * SUB)
                dg_scr[pl.ds(i * SUB - 1, 1), :] = (jnp.sum(tk, axis=0, keepdims=True)
                                                    - jnp.sum(jnp.where(in_blk, t2, 0.0), axis=0, keepdims=True))
        dg_scr[pl.ds(CHUNK - 1, 1), :] = wide(dgend)
        t = lax.broadcasted_iota(jnp.int32, (CHUNK, CHUNK), 0)
        s = lax.broadcasted_iota(jnp.int32, (CHUNK, CHUNK), 1)
        dlogf = jnp.dot((t <= s).astype(F32), dg + dg_scr[...], precision=lax.Precision.HIGHEST, preferred_element_type=F32)
        df = dlogf / p.f - dk
        df_ref[...] = (df * (1.0 - lb) * p.sg * (1.0 - p.sg)).astype(BF16)
        dlb_ref[...] += jnp.sum(df * (1.0 - p.sg), axis=0, keepdims=True)
        dq_ref[...] = ((dqg * p.eg + dqd * p.eqd) * p.sigq * (1.0 + qr * (1.0 - p.sigq))).astype(BF16)
        dv_ref[...] = wide(dv).astype(BF16)
        dhg_ref[...] = wide(dhg).astype(BF16)
        dgain_ref[...] += wide(dgain)

    acc = jax.ShapeDtypeStruct((B_loc, 1, HGRN_WIDTH), F32)
    return pl.pallas_call(
        body, name="hgrn_bwd", grid=(B_loc, nc), in_specs=in_specs,
        out_specs=[_window(CHUNK, 4 * HGRN_WIDTH, at(OFF_HQ)), acc_spec, acc_spec],
        out_shape=[jax.ShapeDtypeStruct(dproj.shape, dproj.dtype), acc, acc],
        input_output_aliases={8: 0},
        scratch_shapes=[pltpu.VMEM((nh, HGRN_DIM, HGRN_DIM), F32), pltpu.VMEM((CHUNK, HGRN_WIDTH), F32),
                        pltpu.VMEM((CHUNK, HGRN_WIDTH), F32)],
        compiler_params=_params(("parallel", "arbitrary")),
    )(proj, proj, proj, proj, lb, gain, states, dyh, dproj)


def _adamw(w, g, m, v, name):
    R, C = w.shape
    tr = _pick(R, (128, 64, 32, 16, 8)) if C > 1024 else _pick(R, (512, 256, 128, 64, 32, 16, 8))

    def body(w_ref, g_ref, m_ref, v_ref, d_ref, nm_ref, nv_ref, g_out_ref):
        gv = g_ref[...]
        g_out_ref[...] = gv
        nm = ADAM_B1 * m_ref[...] + (1.0 - ADAM_B1) * gv
        nv = ADAM_B2 * v_ref[...] + (1.0 - ADAM_B2) * (gv * gv)
        m_hat = nm / (1.0 - ADAM_B1 ** ADAM_STEP)
        v_hat = nv / (1.0 - ADAM_B2 ** ADAM_STEP)
        d_ref[...] = -ADAM_LR * (m_hat / (jnp.sqrt(v_hat) + ADAM_EPS) + ADAM_WD * w_ref[...])
        nm_ref[...] = nm
        nv_ref[...] = nv

    blk = pl.BlockSpec((tr, C), lambda i: (i, 0))
    o = jax.ShapeDtypeStruct((R, C), F32)
    return pl.pallas_call(
        body, name=name, grid=(R // tr,), in_specs=[blk] * 4, out_specs=[blk] * 4, out_shape=[o, o, o, o],
        compiler_params=_params(("parallel",)),
    )(w, g, m, v)


ANY = pl.BlockSpec(memory_space=pl.ANY)
VMEM_SPEC = pl.BlockSpec(memory_space=pltpu.VMEM)


def _place():
    x, y, c = lax.axis_index("x"), lax.axis_index("y"), lax.axis_index("c")
    other_chips = [(1 - x, y), (x, 1 - y), (1 - x, 1 - y)]
    return x, y, c, other_chips


def _cast_into_full(w, ax, s_arr, name):
    R, C = w.shape
    tr = _pick(R, (256, 128))
    nr = R // tr

    def body(s_ref, w_ref, o_ref):
        o_ref[...] = w_ref[...].astype(BF16)

    if ax == 1:
        shape, o_map = (R, N_CHIPS * C), lambda i, s: (i, s[0])
    else:
        shape, o_map = (N_CHIPS * R, C), lambda i, s: (s[0] * nr + i, 0)
    return pl.pallas_call(
        body, name=name,
        grid_spec=pltpu.PrefetchScalarGridSpec(
            num_scalar_prefetch=1, grid=(nr,), in_specs=[pl.BlockSpec((tr, C), lambda i, s: (i, 0))],
            out_specs=pl.BlockSpec((tr, C), o_map)),
        out_shape=jax.ShapeDtypeStruct(shape, BF16),
        compiler_params=_params(("parallel",)),
    )(s_arr, w)


class _Gather:
    def __init__(self, fulls, shard_shapes, axes, tag):
        self.shapes, self.axes, self.tag, self.nw = shard_shapes, axes, tag, len(fulls)
        self.fulls, self.sems, self.token = list(fulls), {}, None

    def start(self, peers, after=None):
        nw, np_ = self.nw, len(peers)

        def body(*refs):
            ins, sems = refs[:nw], refs[nw + (after is not None):nw + (after is not None) + 2 * np_]
            for k, j in enumerate(peers):
                for cp in self._peer_copies(ins, sems[2 * k], sems[2 * k + 1], j):
                    cp.start()
            refs[-1][...] = jnp.zeros_like(refs[-1])

        out = pl.pallas_call(
            body, name="gather_start_%s_%s" % (self.tag, "".join(map(str, peers))),
            out_shape=(*[pltpu.SemaphoreType.DMA((nw,))] * (2 * np_),
                       *[pltpu.HBM(f.shape, f.dtype) for f in self.fulls], jax.ShapeDtypeStruct((8, LANES), F32)),
            in_specs=[HBM_SPEC] * nw + ([] if after is None else [ANY]),
            out_specs=(*[SEM_SPEC] * (2 * np_), *[HBM_SPEC] * nw, VMEM_SPEC),
            input_output_aliases={k: 2 * np_ + k for k in range(nw)},
            compiler_params=pltpu.CompilerParams(has_side_effects=DATAFLOW),
        )(*[pltpu.with_memory_space_constraint(f, pltpu.HBM) for f in self.fulls], *(() if after is None else (after,)))
        for k, j in enumerate(peers):
            self.sems[j] = (out[2 * k], out[2 * k + 1])
        self.fulls = list(out[2 * np_:2 * np_ + nw])
        self.token = out[-1]

    def _region(self, ref, i, t, half):
        R, C = self.shapes[i]
        hr = R // 2
        if self.axes[i] == 1:
            return ref.at[pl.ds(half * hr, hr), pl.ds(pl.multiple_of(t * C, LANES), C)]
        return ref.at[pl.ds(t * R + half * hr, hr), :]

    def _peer_copies(self, refs, send_sems, recv_sems, j):
        x, y, c, chips = _place()
        s = 2 * x + y
        return [pltpu.make_async_remote_copy(
            src_ref=self._region(refs[i], i, s, c), dst_ref=self._region(refs[i], i, s, c), send_sem=send_sems.at[i],
            recv_sem=recv_sems.at[i], device_id=(*chips[j], c), device_id_type=MESH) for i in range(self.nw)]

    def wait(self, peers, after):
        nw, np_ = self.nw, len(peers)

        def body(*refs):
            ins, sems = refs[:nw], refs[nw:nw + 2 * np_]
            for k, j in enumerate(peers):
                for cp in self._peer_copies(ins, sems[2 * k], sems[2 * k + 1], j):
                    cp.wait_send()
                    cp.wait_recv()

        sem_args = [s for j in peers for s in self.sems[j]]
        out = pl.pallas_call(
            body, name="gather_wait_%s_%s" % (self.tag, "".join(map(str, peers))),
            out_shape=tuple(pltpu.HBM(f.shape, f.dtype) for f in self.fulls),
            in_specs=[HBM_SPEC] * nw + [SEM_SPEC] * (2 * np_) + [ANY], out_specs=tuple([HBM_SPEC] * nw),
            input_output_aliases={k: k for k in range(nw)},
            compiler_params=pltpu.CompilerParams(has_side_effects=DATAFLOW),
        )(*self.fulls, *sem_args, after)
        self.fulls = list(out)

    def forward(self, peers):
        nw, np_ = self.nw, len(peers)

        def body(*refs):
            ins, outs = refs[:nw], refs[nw:2 * nw]
            send_sems, recv_sems = refs[2 * nw:]
            x, y, c, chips = _place()
            cps = []
            for i in range(nw):
                for k, j in enumerate(peers):
                    t = 2 * chips[j][0] + chips[j][1]
                    cp = pltpu.make_async_remote_copy(
                        src_ref=self._region(ins[i], i, t, c), dst_ref=self._region(outs[i], i, t, c),
                        send_sem=send_sems.at[i * np_ + k], recv_sem=recv_sems.at[i * np_ + k],
                        device_id=(x, y, 1 - c), device_id_type=MESH)
                    cp.start()
                    cps.append(cp)
            for cp in cps:
                cp.wait()

        out = pl.pallas_call(
            body, name="gather_forward_%s_%s" % (self.tag, "".join(map(str, peers))),
            in_specs=[ANY] * nw, out_specs=[ANY] * nw,
            out_shape=[jax.ShapeDtypeStruct(f.shape, f.dtype) for f in self.fulls],
            input_output_aliases={i: i for i in range(nw)},
            scratch_shapes=[pltpu.SemaphoreType.DMA((nw * np_,)), pltpu.SemaphoreType.DMA((nw * np_,))],
        )(*self.fulls)
        self.fulls = list(out)


def _matmul_slab(a, wfull, slab_arr, prev, after, name):
    M, K = a.shape
    N = wfull.shape[1]
    nslab = N // N_CHIPS
    tn = _pick(nslab, (2688, 896, 512, 384, 256, 128))
    tm = _pick(M, (512, 256, 128) if tn > 1024 else (1024, 512, 256, 128))
    per = nslab // tn
    extra = [e for e in (prev, after) if e is not None]

    def body(slab_ref, a_ref, b_ref, *rest):
        rest[len(extra)][...] = jnp.dot(a_ref[...], b_ref[...], preferred_element_type=F32)

    return pl.pallas_call(
        body, name=name,
        grid_spec=pltpu.PrefetchScalarGridSpec(
            num_scalar_prefetch=1, grid=(M // tm, per),
            in_specs=[pl.BlockSpec((tm, K), lambda i, j, sl: (i, 0)),
                      pl.BlockSpec((K, tn), lambda i, j, sl: (0, sl[0] * per + j))] + [ANY] * len(extra),
            out_specs=pl.BlockSpec((tm, tn), lambda i, j, sl: (i, sl[0] * per + j))),
        out_shape=jax.ShapeDtypeStruct((M, N), F32),
        input_output_aliases={} if prev is None else {3: 0},
        compiler_params=_params(("parallel", "arbitrary")),
    )(slab_arr, a, wfull, *extra)


def _exchange_sibling_halves(gs, name):
    nw = len(gs)

    def body(*refs):
        ins, outs = refs[:nw], refs[nw:2 * nw]
        send_sems, recv_sems = refs[2 * nw:]
        x, y, c, _ = _place()
        cps = []
        for i in range(nw):
            cp = pltpu.make_async_remote_copy(src_ref=ins[i].at[:, 1 - c], dst_ref=outs[i], send_sem=send_sems.at[i],
                                              recv_sem=recv_sems.at[i], device_id=(x, y, 1 - c), device_id_type=MESH)
            cp.start()
            cps.append(cp)
        for cp in cps:
            cp.wait()

    return pl.pallas_call(
        body, name=name, in_specs=[ANY] * nw, out_specs=[ANY] * nw,
        out_shape=[jax.ShapeDtypeStruct((g.shape[0],) + g.shape[2:], g.dtype) for g in gs],
        scratch_shapes=[pltpu.SemaphoreType.DMA((nw,)), pltpu.SemaphoreType.DMA((nw,))],
    )(*gs)


HBM_SPEC = pl.BlockSpec(memory_space=pltpu.HBM)
SEM_SPEC = pl.BlockSpec(memory_space=pltpu.SEMAPHORE)
DATAFLOW = pltpu.SideEffectType.DATAFLOW_SIDE_EFFECTING


def _chip_copies(ins, lands, send_sems, recv_sems):
    x, y, c, chips = _place()
    return [pltpu.make_async_remote_copy(
        src_ref=ins[i].at[2 * chip[0] + chip[1]], dst_ref=lands[i].at[j], send_sem=send_sems.at[i * 3 + j],
        recv_sem=recv_sems.at[i * 3 + j], device_id=(*chip, c), device_id_type=MESH)
        for i in range(len(ins)) for j, chip in enumerate(chips)]


def _chips_send_start(ss, name):
    nw = len(ss)
    lands = [pltpu.with_memory_space_constraint(lax.empty((N_CHIPS - 1,) + s.shape[1:], s.dtype), pltpu.HBM) for s in ss]

    def body(*refs):
        ins, land_refs = refs[:nw], refs[nw:2 * nw]
        send_sems, recv_sems = refs[2 * nw], refs[2 * nw + 1]
        token = refs[-1]
        for cp in _chip_copies(ins, land_refs, send_sems, recv_sems):
            cp.start()
        token[...] = jnp.zeros_like(token)

    n = 3 * nw
    out = pl.pallas_call(
        body, name=name,
        out_shape=(pltpu.SemaphoreType.DMA((n,)), pltpu.SemaphoreType.DMA((n,)),
                   *[pltpu.HBM(s.shape, s.dtype) for s in ss], *[pltpu.HBM(l.shape, l.dtype) for l in lands],
                   jax.ShapeDtypeStruct((8, LANES), F32)),
        in_specs=[HBM_SPEC] * (2 * nw), out_specs=(SEM_SPEC, SEM_SPEC, *[HBM_SPEC] * (2 * nw), VMEM_SPEC),
        input_output_aliases={k: 2 + k for k in range(2 * nw)},
        compiler_params=pltpu.CompilerParams(has_side_effects=DATAFLOW),
    )(*[pltpu.with_memory_space_constraint(s, pltpu.HBM) for s in ss], *lands)
    return out[0], out[1], list(out[2:2 + nw]), list(out[2 + nw:2 + 2 * nw]), out[-1]


def _chips_send_wait(send_sems, recv_sems, ss, lands, after, name):
    nw = len(ss)

    def body(*refs):
        ins, land_refs = refs[:nw], refs[nw:2 * nw]
        s_sems, r_sems = refs[2 * nw], refs[2 * nw + 1]
        for cp in _chip_copies(ins, land_refs, s_sems, r_sems):
            cp.wait_send()
            cp.wait_recv()

    out = pl.pallas_call(
        body, name=name,
        out_shape=(*[pltpu.HBM(s.shape, s.dtype) for s in ss], *[pltpu.HBM(l.shape, l.dtype) for l in lands]),
        in_specs=[HBM_SPEC] * (2 * nw) + [SEM_SPEC, SEM_SPEC, ANY], out_specs=tuple([HBM_SPEC] * (2 * nw)),
        input_output_aliases={k: k for k in range(2 * nw)},
        compiler_params=pltpu.CompilerParams(has_side_effects=DATAFLOW),
    )(*ss, *lands, send_sems, recv_sems, after)
    return list(out[nw:])


def _sum_small(pack):
    rows = pack.shape[0]

    def body(pack_ref, sum_ref, all_ref, send_sems, recv_sems):
        x, y, c, _ = _place()
        me = 4 * x + 2 * y + c
        all_ref[me] = pack_ref[...]
        cps = []
        for k in range(1, N_DEV):
            to = (1 - x if k & 4 else x, 1 - y if k & 2 else y, 1 - c if k & 1 else c)
            cp = pltpu.make_async_remote_copy(
                src_ref=pack_ref, dst_ref=all_ref.at[me], send_sem=send_sems.at[k - 1],
                recv_sem=recv_sems.at[k - 1], device_id=to, device_id_type=MESH)
            cp.start()
            cps.append(cp)
        for cp in cps:
            cp.wait()
        total = all_ref[0]
        for d in range(1, N_DEV):
            total = total + all_ref[d]
        sum_ref[...] = total

    return pl.pallas_call(
        body, name="sum_small_grads", in_specs=[VMEM_SPEC], out_specs=VMEM_SPEC,
        out_shape=jax.ShapeDtypeStruct(pack.shape, F32),
        scratch_shapes=[pltpu.VMEM((N_DEV, rows, LANES), F32), pltpu.SemaphoreType.DMA((N_DEV - 1,)),
                        pltpu.SemaphoreType.DMA((N_DEV - 1,))],
    )(pack)


def _share_with_sibling(fs, name):
    nw = len(fs)

    def body(*refs):
        ins, outs = refs[:nw], refs[nw:2 * nw]
        send_sems, recv_sems = refs[2 * nw:]
        x, y, c, _ = _place()
        cps = []
        for i in range(nw):
            cp = pltpu.make_async_remote_copy(src_ref=ins[i].at[c], dst_ref=outs[i].at[c], send_sem=send_sems.at[i],
                                              recv_sem=recv_sems.at[i], device_id=(x, y, 1 - c), device_id_type=MESH)
            cp.start()
            cps.append(cp)
        for cp in cps:
            cp.wait()

    return pl.pallas_call(
        body, name=name, in_specs=[ANY] * nw, out_specs=[ANY] * nw,
        out_shape=[jax.ShapeDtypeStruct(f.shape, f.dtype) for f in fs],
        input_output_aliases={i: i for i in range(nw)},
        scratch_shapes=[pltpu.SemaphoreType.DMA((nw,)), pltpu.SemaphoreType.DMA((nw,))],
    )(*fs)


def _sum_sibling(g, land, where_arr, name):
    P, Q = g.shape[-2:]
    tp = _pick(P, (256, 128, 64))

    def body(w_ref, g_ref, l_ref, s_ref):
        s_ref[...] = (g_ref[...].astype(F32) + l_ref[...].astype(F32)).astype(BF16)

    if g.ndim == 4:
        g_spec = pl.BlockSpec((None, None, tp, Q), lambda j, r, w: (w[1 + j], w[0], r, 0))
    else:
        g_spec = pl.BlockSpec((None, tp, Q), lambda j, r, w: (w[1 + j], r, 0))
    slab = pl.BlockSpec((None, tp, Q), lambda j, r, w: (w[1 + j], r, 0))
    return pl.pallas_call(
        body, name=name,
        grid_spec=pltpu.PrefetchScalarGridSpec(
            num_scalar_prefetch=1, grid=(N_CHIPS - 1, P // tp), in_specs=[g_spec, slab], out_specs=slab),
        out_shape=jax.ShapeDtypeStruct((N_CHIPS, P, Q), BF16),
        compiler_params=_params(("parallel", "parallel")),
    )(where_arr, g, land)


def _sum_chips(g, land, recv, sc_arr, name):
    P, Q = g.shape[-2:]
    tp = _pick(P, (256, 128, 64))

    def body(sc_ref, g_ref, l_ref, r_ref, f_ref):
        acc = g_ref[...].astype(F32) + l_ref[...].astype(F32)
        for j in range(N_CHIPS - 1):
            acc = acc + r_ref[j].astype(F32)
        f_ref[...] = acc

    if g.ndim == 4:
        g_spec = pl.BlockSpec((None, None, tp, Q), lambda r, sc: (sc[0], sc[1], r, 0))
    else:
        g_spec = pl.BlockSpec((None, tp, Q), lambda r, sc: (sc[0], r, 0))
    return pl.pallas_call(
        body, name=name,
        grid_spec=pltpu.PrefetchScalarGridSpec(
            num_scalar_prefetch=1, grid=(P // tp,),
            in_specs=[g_spec,
                      pl.BlockSpec((None, tp, Q), lambda r, sc: (sc[0], r, 0)),
                      pl.BlockSpec((N_CHIPS - 1, tp, Q), lambda r, sc: (0, r, 0))],
            out_specs=pl.BlockSpec((None, tp, Q), lambda r, sc: (sc[1], r, 0))),
        out_shape=jax.ShapeDtypeStruct((2, P, Q), F32),
        compiler_params=_params(("parallel",)),
    )(sc_arr, g, land, recv)


class _Reduction:
    def __init__(self, names, pieces, lands, flight):
        self.names, self.pieces, self.lands = names, pieces, lands
        self.send_sems, self.recv_sems, self.sums, self.zones, self.token = flight


def _reduce_start(pieces, lands, names, where_arr):
    tag = names[0] if len(names) == 1 else "branches"
    sums = [_sum_sibling(g, l, where_arr, "sum_sibling_" + nm) for g, l, nm in zip(pieces, lands, names)]
    return _Reduction(names, pieces, lands, _chips_send_start(sums, "grads_to_chips_start_" + tag))


def _wgrad_half(a, b, half_arr, name, after=None):
    K, M = a.shape
    nslab = b.shape[1] // N_CHIPS
    tm = M // 2
    tn = _pick(nslab, (896, 512, 384, 256, 128))
    per = nslab // tn
    assert 4 * K * (tm + tn) <= MATMUL_OPERAND_BYTES

    def body(h_ref, a_ref, b_ref, *rest):
        rest[-1][...] = lax.dot_general(a_ref[...], b_ref[...], TN_DIMS, preferred_element_type=F32).astype(BF16)

    return pl.pallas_call(
        body, name=name,
        grid_spec=pltpu.PrefetchScalarGridSpec(
            num_scalar_prefetch=1, grid=(b.shape[1] // tn,),
            in_specs=[pl.BlockSpec((K, tm), lambda j, h: (0, h[0])), pl.BlockSpec((K, tn), lambda j, h: (0, j))]
            + ([] if after is None else [ANY]),
            out_specs=pl.BlockSpec((None, tm, tn), lambda j, h: (j // per, 0, j % per))),
        out_shape=jax.ShapeDtypeStruct((N_CHIPS, tm, nslab), BF16),
        compiler_params=_params(("parallel",)),
    )(*((half_arr, a, b) if after is None else (half_arr, a, b, after)))


def _sibling_send_start(g, name):
    zone = pltpu.with_memory_space_constraint(lax.empty(g.shape, g.dtype), pltpu.HBM)

    def body(g_ref, zone_ref, send_sem, recv_sem, g_thru, zone_thru, token):
        x, y, c, _ = _place()
        pltpu.make_async_remote_copy(src_ref=g_ref, dst_ref=zone_ref, send_sem=send_sem, recv_sem=recv_sem,
                                     device_id=(x, y, 1 - c), device_id_type=MESH).start()
        token[...] = jnp.zeros_like(token)

    return pl.pallas_call(
        body, name=name,
        out_shape=(pltpu.SemaphoreType.DMA(()), pltpu.SemaphoreType.DMA(()), pltpu.HBM(g.shape, g.dtype),
                   pltpu.HBM(g.shape, g.dtype), jax.ShapeDtypeStruct((8, LANES), F32)),
        in_specs=[HBM_SPEC, HBM_SPEC], out_specs=(SEM_SPEC, SEM_SPEC, HBM_SPEC, HBM_SPEC, VMEM_SPEC),
        input_output_aliases={0: 2, 1: 3},
        compiler_params=pltpu.CompilerParams(has_side_effects=DATAFLOW),
    )(pltpu.with_memory_space_constraint(g, pltpu.HBM), zone)


def _sibling_send_wait(send_sem, recv_sem, g, zone, after, name):
    def body(g_ref, zone_ref, s_sem, r_sem, after_ref, g_out, zone_out):
        x, y, c, _ = _place()
        cp = pltpu.make_async_remote_copy(src_ref=g_ref, dst_ref=zone_ref, send_sem=s_sem, recv_sem=r_sem,
                                          device_id=(x, y, 1 - c), device_id_type=MESH)
        cp.wait_send()
        cp.wait_recv()

    return pl.pallas_call(
        body, name=name, out_shape=(pltpu.HBM(g.shape, g.dtype), pltpu.HBM(zone.shape, zone.dtype)),
        in_specs=[HBM_SPEC, HBM_SPEC, SEM_SPEC, SEM_SPEC, ANY], out_specs=(HBM_SPEC, HBM_SPEC),
        input_output_aliases={0: 0, 1: 1},
        compiler_params=pltpu.CompilerParams(has_side_effects=DATAFLOW),
    )(g, zone, send_sem, recv_sem, after)[1]


def _reduce_finish(red, after, sc_arr):
    tag = red.names[0] if len(red.names) == 1 else "branches"
    recvs = _chips_send_wait(red.send_sems, red.recv_sems, red.sums, red.zones, after, "grads_to_chips_wait_" + tag)
    halves = [_sum_chips(g, l, r, sc_arr, "sum_chips_" + nm) for g, l, r, nm in zip(red.pieces, red.lands, recvs, red.names)]
    return _share_with_sibling(halves, "grads_share_sibling_" + tag)


def _t5_bucket(dist):
    max_exact = REL_BUCKETS // 2
    d = jnp.maximum(dist, 0)
    df = jnp.maximum(d, 1).astype(F32)
    large = max_exact + (jnp.log(df / max_exact) / math.log(REL_MAX_DIST / max_exact)
                         * (REL_BUCKETS - max_exact)).astype(jnp.int32)
    large = jnp.minimum(large, REL_BUCKETS - 1)
    return jnp.where(d < max_exact, d, large)


def _bucket_table():
    qi = jnp.arange(WINDOW)[:, None]
    si = jnp.arange(2 * WINDOW)[None, :]
    return _t5_bucket(qi + WINDOW - si)


TILE_WORDS = 8 * LANES


def _tile_rows(shape):
    return -(-math.prod(shape) // TILE_WORDS) * 8


def _rows_of(a):
    flat = a.reshape(-1).astype(F32)
    n = _tile_rows(a.shape) * LANES
    return jnp.pad(flat, (0, n - flat.shape[0])).reshape(-1, LANES)


def _pack_rows(parts):
    return jnp.concatenate([_rows_of(p) for p in parts], axis=0)


def _unpack_rows(packed, shapes):
    out, at = [], 0
    for shp in shapes:
        n, nr = math.prod(shp), _tile_rows(shp)
        out.append(packed[at:at + nr].reshape(-1)[:n].reshape(shp))
        at += nr
    return out


def kernel(x, norm_pre, w_in, rel_bias, attn_sinks, lb_logits, hgrn_norm, w_branch_attn, w_branch_hgrn, w_out, norm_post, loss_target, m_norm_pre, m_w_in, m_rel_bias, m_attn_sinks, m_lb_logits, m_hgrn_norm, m_w_branch_attn, m_w_branch_hgrn, m_w_out, m_norm_post, v_norm_pre, v_w_in, v_rel_bias, v_attn_sinks, v_lb_logits, v_hgrn_norm, v_w_branch_attn, v_w_branch_hgrn, v_w_out, v_norm_post):
    B_loc, S, D = x.shape
    T = B_loc * S
    x2 = x.reshape(T, D)
    tgt2 = loss_target.reshape(T, D)
    my_x, my_y, my_c = lax.axis_index("x"), lax.axis_index("y"), lax.axis_index("c")

    c_arr = jnp.reshape(my_c, (1,)).astype(jnp.int32)
    s_arr = jnp.reshape(2 * my_x + my_y, (1,)).astype(jnp.int32)
    sc_arr = jnp.concatenate([s_arr, c_arr])
    shard_ws = [w_in[0], w_branch_attn[0], w_branch_hgrn[0], w_out[0]]
    shard_axes = (1, 1, 1, 0)
    names = ["w_in", "w_branch_attn", "w_branch_hgrn", "w_out"]
    placed = [_cast_into_full(w, ax, s_arr, "cast_" + nm) for w, ax, nm in zip(shard_ws, shard_axes, names)]
    peer_slabs = [jnp.reshape(t, (1,)).astype(jnp.int32)
                  for t in (2 * (1 - my_x) + my_y, 2 * my_x + 1 - my_y, 2 * (1 - my_x) + 1 - my_y)]

    buckets = _bucket_table()
    onehot = (buckets.reshape(-1)[:, None] == jnp.arange(REL_BUCKETS)[None, :]).astype(F32)
    bias_tab = jnp.dot(onehot, rel_bias.astype(F32), precision=lax.Precision.HIGHEST).T.reshape(ATTN_HEADS, WINDOW, 2 * WINDOW)
    sinks_b = jnp.broadcast_to(attn_sinks[0].astype(F32)[:, None, None], (ATTN_HEADS, 8, LANES))
    lb_fn = lambda l: jnp.cumsum(jax.nn.softmax(l.astype(F32), axis=0), axis=0)[:1]
    lb, lb_vjp = jax.vjp(lb_fn, lb_logits)
    gain_h = hgrn_norm[0].reshape(1, HGRN_WIDTH)

    h, rstd = _rmsnorm_fwd(x2, norm_pre)
    gather_in = _Gather(placed[:1], [shard_ws[0].shape], shard_axes[:1], "w_in")
    gather_in.start([0, 1])
    proj = _matmul_slab(h, gather_in.fulls[0], s_arr, None, gather_in.token, "in_proj_own")
    gather_in.wait([0, 1], proj)
    gather_in.forward([0, 1])
    gather_in.start([2])
    proj = _matmul_slab(h, gather_in.fulls[0], peer_slabs[0], proj, gather_in.token, "in_proj_peer0")
    proj = _matmul_slab(h, gather_in.fulls[0], peer_slabs[1], proj, None, "in_proj_peer1")
    gather_in.wait([2], proj)
    gather_in.forward([2])
    gather_rest = _Gather(placed[1:], [w.shape for w in shard_ws[1:]], shard_axes[1:], "rest")
    gather_rest.start([0, 1, 2], gather_in.fulls[0])
    proj = _matmul_slab(h, gather_in.fulls[0], peer_slabs[2], proj, gather_rest.token, "in_proj_peer2")
    win_f = gather_in.fulls[0]
    ya = _attn_fwd(proj, bias_tab, sinks_b, B_loc, S)
    yh, states = _hgrn_fwd(proj, lb, gain_h, B_loc, S)
    gather_rest.wait([0, 1, 2], yh)
    gather_rest.forward([0, 1, 2])
    wba_f, wbh_f, wout_f = gather_rest.fulls
    ua = _matmul(ya, wba_f, "nn", BF16, "branch_attn_proj")
    uh, merged = _branch_proj_merge(yh, wbh_f, proj, ua)
    yv = _matmul(merged, wout_f, "nn", F32, "out_proj")
    dy, dout, loss_p, gnpost_p = _post_loss(yv, x2, tgt2, norm_post)

    g_wout = _matmul(merged, dy, "tn", BF16, "out_proj_wgrad")
    d_ua, d_uh, dproj = _out_dgrad_merge_bwd(dy, wout_f, proj, ua, uh)
    g_wba = _matmul(ya, d_ua, "tn", BF16, "branch_attn_wgrad", slabs=N_CHIPS)
    g_wbh = _matmul(yh, d_uh, "tn", BF16, "branch_hgrn_wgrad", slabs=N_CHIPS)
    where_arr = jnp.concatenate([c_arr] + peer_slabs)
    late_pieces = [g.reshape(N_CHIPS, 2, -1, g.shape[-1]) for g in (g_wba, g_wbh, g_wout)]
    late = _reduce_start(late_pieces, _exchange_sibling_halves(late_pieces, "grads_to_sibling_branches"), names[1:], where_arr)
    d_ya = _matmul(d_ua, wba_f, "nt", BF16, "branch_attn_dgrad", after=late.token)
    d_yh = _matmul(d_uh, wbh_f, "nt", BF16, "branch_hgrn_dgrad", after=late.token)
    dproj, dbias_p, dsinks_p = _attn_bwd(proj, bias_tab, sinks_b, d_ya, dproj, B_loc, S)
    dproj, dlb_p, dgain_p = _hgrn_bwd(proj, states, lb, gain_h, d_yh, dproj, B_loc, S)
    g_give = _wgrad_half(h, dproj, 1 - c_arr, "in_proj_wgrad_sibling_half")
    s_sem, r_sem, g_give, zone, token = _sibling_send_start(g_give, "grads_to_sibling_start_w_in")
    g_keep = _wgrad_half(h, dproj, c_arr, "in_proj_wgrad_own_half", after=token)
    land = _sibling_send_wait(s_sem, r_sem, g_give, zone, g_keep, "grads_to_sibling_wait_w_in")
    last = _reduce_start([g_keep], [land], names[:1], where_arr)
    dh = _matmul(dproj, win_f, "nt", F32, "in_proj_dgrad", after=last.token)
    grad_x2, gnpre_p = _rmsnorm_bwd(dh, x2, rstd, norm_pre, dout)
    shared = _reduce_finish(last, grad_x2, sc_arr) + _reduce_finish(late, grad_x2, sc_arr)

    grelb_p = jnp.dot(dbias_p.reshape(ATTN_HEADS, -1), onehot, precision=lax.Precision.HIGHEST).T
    gsinks_p = dsinks_p[:, 0, 0]
    dlb_sum = jnp.sum(dlb_p, axis=0).reshape(1, HGRN_WIDTH)
    ghn_p = jnp.sum(dgain_p, axis=0).reshape(HGRN_HEADS, HGRN_DIM)
    small_parts = [gnpre_p, gnpost_p, grelb_p, gsinks_p, dlb_sum, ghn_p, loss_p]
    small_shapes = [p.shape for p in small_parts]
    pack_sum = _sum_small(_pack_rows(small_parts))
    big_w = [w_in, w_branch_attn, w_branch_hgrn, w_out]
    big_m = [m_w_in, m_w_branch_attn, m_w_branch_hgrn, m_w_out]
    big_v = [v_w_in, v_w_branch_attn, v_w_branch_hgrn, v_w_out]
    big = {}
    for nm, gs, w, m, v in zip(names, shared, big_w, big_m, big_v):
        shp = w.shape
        g2 = gs.reshape(shp[1], shp[2])
        d, nm_, nv_, g_out = _adamw(w[0], g2, m[0], v[0], "adamw_" + nm)
        big[nm] = tuple(a.reshape(shp) for a in (g_out, d, nm_, nv_))

    gnpre, gnpost, grelb, gsinks, dlb, ghn, loss = _unpack_rows(pack_sum, small_shapes)
    (g_lb_logits,) = lb_vjp(dlb)
    small_names = ["norm_pre", "rel_bias", "attn_sinks", "lb_logits", "hgrn_norm", "norm_post"]
    small_w = [norm_pre, rel_bias, attn_sinks, lb_logits, hgrn_norm, norm_post]
    small_m = [m_norm_pre, m_rel_bias, m_attn_sinks, m_lb_logits, m_hgrn_norm, m_norm_post]
    small_v = [v_norm_pre, v_rel_bias, v_attn_sinks, v_lb_logits, v_hgrn_norm, v_norm_post]
    small_g = [gnpre.reshape(norm_pre.shape), grelb.reshape(rel_bias.shape), gsinks.reshape(attn_sinks.shape),
               g_lb_logits.reshape(lb_logits.shape), ghn.reshape(hgrn_norm.shape), gnpost.reshape(norm_post.shape)]
    shapes = [w.shape for w in small_w]
    d_s, nm_s, nv_s, _ = _adamw(_pack_rows(small_w), _pack_rows(small_g), _pack_rows(small_m), _pack_rows(small_v),
                             "adamw_small")
    small = {}
    for nm, g, d, m_, v_ in zip(small_names, small_g, _unpack_rows(d_s, shapes), _unpack_rows(nm_s, shapes),
                                _unpack_rows(nv_s, shapes)):
        small[nm] = (g, d, m_, v_)

    allw = {**big, **small}
    order = ["norm_pre", "w_in", "rel_bias", "attn_sinks", "lb_logits", "hgrn_norm", "w_branch_attn", "w_branch_hgrn",
             "w_out", "norm_post"]
    outs = [loss.reshape(()), grad_x2.reshape(B_loc, S, D)]
    for k in range(4):
        outs += [allw[nm][k] for nm in order]
    return tuple(outs)
```

```python
import functools
import math

import jax
import jax.numpy as jnp
from jax import lax
from jax.experimental import pallas as pl
from jax.experimental.pallas import tpu as pltpu

F32 = jnp.float32
BF16 = jnp.bfloat16
MESH = pl.DeviceIdType.MESH

ATTN_HEADS = 16
ATTN_KV_HEADS = 4
HEAD_DIM = 64
GROUP = ATTN_HEADS // ATTN_KV_HEADS
WINDOW = 128
ATTN_WIDTH = ATTN_HEADS * HEAD_DIM
KV_WIDTH = ATTN_KV_HEADS * HEAD_DIM
HGRN_HEADS = 8
HGRN_DIM = 128
HGRN_WIDTH = HGRN_HEADS * HGRN_DIM
CHUNK = 64
SUB = 16
NSUB = CHUNK // SUB
REL_BUCKETS = 32
REL_MAX_DIST = 128
NORM_EPS = 1e-6
ADAM_LR = 0.001
ADAM_B1 = 0.9
ADAM_B2 = 0.999
ADAM_EPS = 1e-08
ADAM_WD = 0.01
ADAM_STEP = 10
LANES = 128
N_CHIPS = 4
N_DEV = 8
VMEM_LIMIT = 48 * 1024 * 1024
MATMUL_OPERAND_BYTES = 34 * 1024 * 1024
MATMUL_VMEM_BYTES = 44 * 1024 * 1024

OFF_AQ = 0
OFF_AK = OFF_AQ + ATTN_WIDTH
OFF_AV = OFF_AK + KV_WIDTH
OFF_AG = OFF_AV + KV_WIDTH
OFF_HQ = OFF_AG + ATTN_WIDTH
OFF_HF = OFF_HQ + HGRN_WIDTH
OFF_HI = OFF_HF + HGRN_WIDTH
OFF_HG = OFF_HI + HGRN_WIDTH
OFF_GA = OFF_HG + HGRN_WIDTH

NT_DIMS = (((1,), (1,)), ((), ()))
TN_DIMS = (((0,), (0,)), ((), ()))
NN_DIMS = (((1,), (0,)), ((), ()))


def _pick(n, cands):
    for c in cands:
        if n % c == 0:
            return c
    raise ValueError(f"no tile for {n} in {cands}")


def _params(sem):
    return pltpu.CompilerParams(dimension_semantics=sem, vmem_limit_bytes=VMEM_LIMIT)


def _bdot(a, b, dims=NN_DIMS):
    return lax.dot_general(a.astype(BF16), b.astype(BF16), dims, preferred_element_type=F32)


def _matmul(a, b, mode, out_dtype, name, slabs=1, after=None):
    if mode == "nn":
        (M, K), (K2, N) = a.shape, b.shape
    elif mode == "nt":
        (M, K), (N, K2) = a.shape, b.shape
    else:
        (K, M), (K2, N) = a.shape, b.shape
    assert K == K2
    nslab = N // slabs
    tm = _pick(M, (1024, 512, 256, 128))
    out_bytes = jnp.dtype(out_dtype).itemsize
    choices = []
    for tn in (2688, 1024, 896, 512, 384, 256, 128):
        for tk in (4096, 3584, 2048, 1792, 1536, 1024, 512, 256, 128):
            acc = 0 if tk == K else 4 * tm * tn
            if (nslab % tn == 0 and K % tk == 0 and 4 * tk * (tm + tn) <= MATMUL_OPERAND_BYTES
                    and 4 * tk * (tm + tn) + 2 * out_bytes * tm * tn + acc <= MATMUL_VMEM_BYTES):
                choices.append((K // tk > 1, -tn, tn, tk))
                break
    _, _, tn, tk = min(choices)
    nk = K // tk
    per = nslab // tn
    dims = {"nn": NN_DIMS, "nt": NT_DIMS, "tn": TN_DIMS}[mode]

    n_in = 2 if after is None else 3

    def body(*refs):
        a_ref, b_ref, o_ref, acc = refs[0], refs[1], refs[n_in], refs[n_in + 1:]
        part = lax.dot_general(a_ref[...], b_ref[...], dims, preferred_element_type=F32)
        if nk == 1:
            o_ref[...] = part.astype(o_ref.dtype)
            return
        acc_ref, = acc
        k = pl.program_id(2)

        @pl.when(k == 0)
        def _():
            acc_ref[...] = part

        @pl.when((k > 0) & (k < nk - 1))
        def _():
            acc_ref[...] += part

        @pl.when(k == nk - 1)
        def _():
            o_ref[...] = (acc_ref[...] + part).astype(o_ref.dtype)

    if mode == "tn":
        a_spec = pl.BlockSpec((tk, tm), lambda i, j, k: (k, i))
    else:
        a_spec = pl.BlockSpec((tm, tk), lambda i, j, k: (i, k))
    if mode == "nt":
        b_spec = pl.BlockSpec((tn, tk), lambda i, j, k: (j, k))
    else:
        b_spec = pl.BlockSpec((tk, tn), lambda i, j, k: (k, j))
    if slabs == 1:
        o_shape = jax.ShapeDtypeStruct((M, N), out_dtype)
        o_spec = pl.BlockSpec((tm, tn), lambda i, j, k: (i, j))
    else:
        o_shape = jax.ShapeDtypeStruct((slabs, M, nslab), out_dtype)
        o_spec = pl.BlockSpec((None, tm, tn), lambda i, j, k: (j // per, i, j % per))
    return pl.pallas_call(
        body, name=name, grid=(M // tm, N // tn, nk), in_specs=[a_spec, b_spec] + ([] if after is None else [ANY]),
        out_specs=o_spec, out_shape=o_shape,
        scratch_shapes=[pltpu.VMEM((tm, tn), F32)] if nk > 1 else [],
        compiler_params=_params(("parallel", "parallel", "arbitrary")),
    )(*((a, b) if after is None else (a, b, after)))


def _rmsnorm_fwd(x2, gain):
    T, D = x2.shape
    tr = _pick(T, (256, 128))

    def body(x_ref, g_ref, h_ref, r_ref):
        xv = x_ref[...]
        r = lax.rsqrt(jnp.mean(xv * xv, axis=-1, keepdims=True) + NORM_EPS)
        h_ref[...] = (xv * r * g_ref[...]).astype(BF16)
        r_ref[...] = r

    return pl.pallas_call(
        body, name="rmsnorm_pre_fwd", grid=(T // tr,),
        in_specs=[pl.BlockSpec((tr, D), lambda i: (i, 0)), pl.BlockSpec((1, D), lambda i: (0, 0))],
        out_specs=[pl.BlockSpec((tr, D), lambda i: (i, 0)), pl.BlockSpec((tr, 1), lambda i: (i, 0))],
        out_shape=[jax.ShapeDtypeStruct((T, D), BF16), jax.ShapeDtypeStruct((T, 1), F32)],
        compiler_params=_params(("parallel",)),
    )(x2, gain)


def _rmsnorm_bwd(dh, x2, rstd, gain, dout):
    T, D = x2.shape
    tr = _pick(T, (256, 128))

    def body(dh_ref, x_ref, r_ref, g_ref, do_ref, gx_ref, gg_ref):
        @pl.when(pl.program_id(0) == 0)
        def _():
            gg_ref[...] = jnp.zeros_like(gg_ref)

        n = x_ref[...] * r_ref[...]
        dhv = dh_ref[...]
        dn = dhv * g_ref[...]
        gx_ref[...] = do_ref[...] + r_ref[...] * (dn - n * jnp.mean(dn * n, axis=-1, keepdims=True))
        gg_ref[...] += jnp.sum(dhv * n, axis=0, keepdims=True)

    row = pl.BlockSpec((tr, D), lambda i: (i, 0))
    vec = pl.BlockSpec((1, D), lambda i: (0, 0))
    return pl.pallas_call(
        body, name="rmsnorm_pre_bwd", grid=(T // tr,),
        in_specs=[row, row, pl.BlockSpec((tr, 1), lambda i: (i, 0)), vec, row],
        out_specs=[row, vec],
        out_shape=[jax.ShapeDtypeStruct((T, D), F32), jax.ShapeDtypeStruct((1, D), F32)],
        compiler_params=_params(("arbitrary",)),
    )(dh, x2, rstd, gain, dout)


def _post_loss(yv, x2, tgt2, gain):
    T, D = x2.shape
    tr = _pick(T, (256, 128))

    def body(y_ref, x_ref, t_ref, g_ref, dy_ref, do_ref, loss_ref, gg_ref):
        @pl.when(pl.program_id(0) == 0)
        def _():
            gg_ref[...] = jnp.zeros_like(gg_ref)
            loss_ref[...] = jnp.zeros_like(loss_ref)

        yv_ = y_ref[...]
        r = lax.rsqrt(jnp.mean(yv_ * yv_, axis=-1, keepdims=True) + NORM_EPS)
        n = yv_ * r
        e = (x_ref[...] + n * g_ref[...]) - t_ref[...]
        loss_ref[...] += 0.5 * jnp.sum(jnp.mean(e * e, axis=-1, keepdims=True), axis=0, keepdims=True)
        dz = e / D
        do_ref[...] = dz
        gg_ref[...] += jnp.sum(dz * n, axis=0, keepdims=True)
        dn = dz * g_ref[...]
        dy_ref[...] = (r * (dn - n * jnp.mean(dn * n, axis=-1, keepdims=True))).astype(BF16)

    row = pl.BlockSpec((tr, D), lambda i: (i, 0))
    vec = pl.BlockSpec((1, D), lambda i: (0, 0))
    return pl.pallas_call(
        body, name="post_norm_loss", grid=(T // tr,),
        in_specs=[row, row, row, vec],
        out_specs=[row, row, pl.BlockSpec((1, 1), lambda i: (0, 0)), vec],
        out_shape=[jax.ShapeDtypeStruct((T, D), BF16), jax.ShapeDtypeStruct((T, D), F32),
                   jax.ShapeDtypeStruct((1, 1), F32), jax.ShapeDtypeStruct((1, D), F32)],
        compiler_params=_params(("arbitrary",)),
    )(yv, x2, tgt2, gain)


def _window(rows, cols, at):
    return pl.BlockSpec((pl.Element(rows), pl.Element(cols)), at)


def _gate_windows(tm, tn, D):
    return [_window(tm, tn, lambda i, j: (i * tm, pl.multiple_of(OFF_GA + j * tn, LANES))),
            _window(tm, tn, lambda i, j: (i * tm, pl.multiple_of(OFF_GA + D + j * tn, LANES)))]


def _branch_proj_merge(yh, wbh, proj, ua):
    T, K = yh.shape
    D = wbh.shape[1]
    tm, tn = _pick(T, (512, 256, 128)), _pick(D, (1024, 512, 256))

    def body(a_ref, b_ref, ga_ref, gh_ref, ua_ref, uh_ref, m_ref):
        uh = jnp.dot(a_ref[...], b_ref[...], preferred_element_type=F32).astype(BF16)
        uh_ref[...] = uh
        m_ref[...] = (jax.nn.sigmoid(ga_ref[...]) * ua_ref[...].astype(F32)
                      + jax.nn.sigmoid(gh_ref[...]) * uh.astype(F32)).astype(BF16)

    blk = pl.BlockSpec((tm, tn), lambda i, j: (i, j))
    o = jax.ShapeDtypeStruct((T, D), BF16)
    return pl.pallas_call(
        body, name="branch_hgrn_proj_merge", grid=(T // tm, D // tn),
        in_specs=[pl.BlockSpec((tm, K), lambda i, j: (i, 0)), pl.BlockSpec((K, tn), lambda i, j: (0, j))]
        + _gate_windows(tm, tn, D) + [blk],
        out_specs=[blk, blk], out_shape=[o, o],
        compiler_params=_params(("parallel", "arbitrary")),
    )(yh, wbh, proj, proj, ua)


def _out_dgrad_merge_bwd(dy, wout, proj, ua, uh):
    T, K = dy.shape
    D = wout.shape[0]
    tm, tn = _pick(T, (512, 256, 128)), _pick(D, (1024, 512, 256))
    nj = D // tn

    def body(a_ref, b_ref, ga_ref, gh_ref, ua_ref, uh_ref, dua_ref, duh_ref, dproj_ref):
        d = lax.dot_general(a_ref[...], b_ref[...], NT_DIMS, preferred_element_type=F32)
        sa = jax.nn.sigmoid(ga_ref[...])
        sh = jax.nn.sigmoid(gh_ref[...])
        dua_ref[...] = (d * sa).astype(BF16)
        duh_ref[...] = (d * sh).astype(BF16)
        dga = (d * ua_ref[...].astype(F32) * sa * (1.0 - sa)).astype(BF16)
        dgh = (d * uh_ref[...].astype(F32) * sh * (1.0 - sh)).astype(BF16)
        for jj in range(nj):
            @pl.when(pl.program_id(1) == jj)
            def _():
                dproj_ref[:, jj * tn:(jj + 1) * tn] = dga
                dproj_ref[:, D + jj * tn:D + (jj + 1) * tn] = dgh

    blk = pl.BlockSpec((tm, tn), lambda i, j: (i, j))
    o = jax.ShapeDtypeStruct((T, D), BF16)
    return pl.pallas_call(
        body, name="out_proj_dgrad_merge_bwd", grid=(T // tm, nj),
        in_specs=[pl.BlockSpec((tm, K), lambda i, j: (i, 0)), pl.BlockSpec((tn, K), lambda i, j: (j, 0))]
        + _gate_windows(tm, tn, D) + [blk, blk],
        out_specs=[blk, blk, _window(tm, 2 * D, lambda i, j: (i * tm, OFF_GA))],
        out_shape=[o, o, jax.ShapeDtypeStruct((T, proj.shape[1]), BF16)],
        compiler_params=_params(("parallel", "arbitrary")),
    )(dy, wout, proj, proj, ua, uh)


KV_PAIR = 2
PAIR_HEADS = KV_PAIR * GROUP


def _attn_mask(n):
    qi = lax.broadcasted_iota(jnp.int32, (WINDOW, 2 * WINDOW), 0)
    si = lax.broadcasted_iota(jnp.int32, (WINDOW, 2 * WINDOW), 1)
    dist = qi + WINDOW - si
    return (dist >= 0) & (dist < WINDOW) & ((si >= WINDOW) | (n > 0))


def _attn_group_fwd(mask, q, k, v, ag, bias, sink):
    mask = jnp.concatenate([mask] * GROUP, axis=0)
    s = lax.dot_general(q.astype(BF16), k.astype(BF16), NT_DIMS, preferred_element_type=F32) * (HEAD_DIM ** -0.5)
    s = jnp.where(mask, s + bias, -1e30)
    m = jnp.maximum(jnp.max(s, axis=-1, keepdims=True), sink)
    p = jnp.exp(s - m).astype(BF16)
    e_sink = jnp.exp(sink - m)
    rden = 1.0 / (jnp.sum(p.astype(F32), axis=-1, keepdims=True) + e_sink)
    o = jnp.dot(p, v.astype(BF16), preferred_element_type=F32) * rden
    return o * jax.nn.silu(ag), p, rden, e_sink * rden


def _attn_group_bwd(q, k, v, ag, p, rden, p_sink, dout):
    qb, kb, vb = q.astype(BF16), k.astype(BF16), v.astype(BF16)
    o = jnp.dot(p, vb, preferred_element_type=F32) * rden
    sg = jax.nn.sigmoid(ag)
    d_o = dout * (ag * sg)
    dag = dout * o * (sg * (1.0 + ag * (1.0 - sg)))
    d_row = jnp.sum(d_o * o, axis=-1, keepdims=True)
    ds = (p.astype(F32) * rden) * (lax.dot_general(d_o.astype(BF16), vb, NT_DIMS, preferred_element_type=F32) - d_row)
    dsb = ds.astype(BF16)
    dq = jnp.dot(dsb, kb, preferred_element_type=F32) * (HEAD_DIM ** -0.5)
    dk = lax.dot_general(dsb, qb, TN_DIMS, preferred_element_type=F32) * (HEAD_DIM ** -0.5)
    dv = lax.dot_general(p, (d_o * rden).astype(BF16), TN_DIMS, preferred_element_type=F32)
    return dq, dk, dv, dag, ds, -(p_sink * d_row)


def _attn_specs(B_loc, nb, order):
    qw = PAIR_HEADS * HEAD_DIM
    kw = KV_PAIR * HEAD_DIM

    def rows(g):
        b, p, n = order(*g)
        return b * nb + n

    def prev(g):
        b, p, n = order(*g)
        return b * nb + jnp.maximum(n - 1, 0)

    def pp(g):
        return order(*g)[1]

    q = pl.BlockSpec((WINDOW, qw), lambda *g: (rows(g), OFF_AQ // qw + pp(g)))
    kc = pl.BlockSpec((WINDOW, kw), lambda *g: (rows(g), OFF_AK // kw + pp(g)))
    kp = pl.BlockSpec((WINDOW, kw), lambda *g: (prev(g), OFF_AK // kw + pp(g)))
    vc = pl.BlockSpec((WINDOW, kw), lambda *g: (rows(g), OFF_AV // kw + pp(g)))
    vp = pl.BlockSpec((WINDOW, kw), lambda *g: (prev(g), OFF_AV // kw + pp(g)))
    ag = pl.BlockSpec((WINDOW, qw), lambda *g: (rows(g), OFF_AG // qw + pp(g)))
    bias = pl.BlockSpec((PAIR_HEADS, WINDOW, 2 * WINDOW), lambda *g: (pp(g), 0, 0))
    sink = pl.BlockSpec((PAIR_HEADS, 8, LANES), lambda *g: (pp(g), 0, 0))
    return [q, kc, kp, vc, vp, ag, bias, sink], rows, pp


def _attn_operands(q_ref, kc_ref, kp_ref, vc_ref, vp_ref, ag_ref, bias_ref, sink_ref, j):
    lo, hi = j * HEAD_DIM, (j + 1) * HEAD_DIM
    k = jnp.concatenate([kp_ref[:, lo:hi], kc_ref[:, lo:hi]], axis=0)
    v = jnp.concatenate([vp_ref[:, lo:hi], vc_ref[:, lo:hi]], axis=0)
    heads = [j * GROUP + g for g in range(GROUP)]
    q = jnp.concatenate([q_ref[:, h * HEAD_DIM:(h + 1) * HEAD_DIM] for h in heads], axis=0)
    ag = jnp.concatenate([ag_ref[:, h * HEAD_DIM:(h + 1) * HEAD_DIM] for h in heads], axis=0)
    bias = jnp.concatenate([bias_ref[h] for h in heads], axis=0)
    sink = jnp.concatenate([jnp.broadcast_to(sink_ref[h, 0:1, 0:1], (WINDOW, 1)) for h in heads], axis=0)
    return q, k, v, ag, bias, sink


def _attn_fwd(proj, bias_tab, sinks_b, B_loc, S):
    T = B_loc * S
    nb = S // WINDOW
    n_pairs = ATTN_KV_HEADS // KV_PAIR
    in_specs, rows, pp = _attn_specs(B_loc, nb, lambda b, p, n: (b, p, n))

    def body(q_ref, kc_ref, kp_ref, vc_ref, vp_ref, ag_ref, bias_ref, sink_ref, ya_ref, p_ref, st_ref):
        mask = _attn_mask(pl.program_id(2))
        for j in range(KV_PAIR):
            out, p, rden, p_sink = _attn_group_fwd(
                mask, *_attn_operands(q_ref, kc_ref, kp_ref, vc_ref, vp_ref, ag_ref, bias_ref, sink_ref, j))
            for g in range(GROUP):
                h = j * GROUP + g
                blk = slice(g * WINDOW, (g + 1) * WINDOW)
                ya_ref[:, h * HEAD_DIM:(h + 1) * HEAD_DIM] = out[blk].astype(BF16)
                p_ref[:, h * 2 * WINDOW:(h + 1) * 2 * WINDOW] = p[blk]
                st_ref[:, h:h + 1] = rden[blk]
                st_ref[:, PAIR_HEADS + h:PAIR_HEADS + h + 1] = p_sink[blk]

    return pl.pallas_call(
        body, name="attn_fwd", grid=(B_loc, n_pairs, nb), in_specs=in_specs,
        out_specs=[pl.BlockSpec((WINDOW, PAIR_HEADS * HEAD_DIM), lambda *g: (rows(g), pp(g))),
                   pl.BlockSpec((WINDOW, PAIR_HEADS * 2 * WINDOW), lambda *g: (rows(g), pp(g))),
                   pl.BlockSpec((None, WINDOW, 2 * PAIR_HEADS), lambda *g: (pp(g), rows(g), 0))],
        out_shape=[jax.ShapeDtypeStruct((T, ATTN_WIDTH), BF16),
                   jax.ShapeDtypeStruct((T, ATTN_HEADS * 2 * WINDOW), BF16),
                   jax.ShapeDtypeStruct((n_pairs, T, 2 * PAIR_HEADS), F32)],
        compiler_params=_params(("parallel", "parallel", "parallel")),
    )(proj, proj, proj, proj, proj, proj, bias_tab, sinks_b)


def _attn_bwd(proj, probs, stats, dya, dproj, B_loc, S):
    nb = S // WINDOW
    n_pairs = ATTN_KV_HEADS // KV_PAIR
    qw, kw = PAIR_HEADS * HEAD_DIM, KV_PAIR * HEAD_DIM

    def at(off, width, back=0):
        return lambda b, i, p: ((b * nb + jnp.maximum(nb - 1 - i - back, 0)) * WINDOW,
                                pl.multiple_of(off + p * width, LANES))

    in_specs = [_window(WINDOW, qw, at(OFF_AQ, qw)), _window(WINDOW, kw, at(OFF_AK, kw)),
                _window(WINDOW, kw, at(OFF_AK, kw, 1)), _window(WINDOW, kw, at(OFF_AV, kw)),
                _window(WINDOW, kw, at(OFF_AV, kw, 1)), _window(WINDOW, qw, at(OFF_AG, qw)),
                pl.BlockSpec((WINDOW, PAIR_HEADS * 2 * WINDOW), lambda b, i, p: (b * nb + nb - 1 - i, p)),
                pl.BlockSpec((None, WINDOW, 2 * PAIR_HEADS), lambda b, i, p: (p, b * nb + nb - 1 - i, 0)),
                pl.BlockSpec((WINDOW, qw), lambda b, i, p: (b * nb + nb - 1 - i, p)), ANY]

    def body(q_ref, kc_ref, kp_ref, vc_ref, vp_ref, ag_ref, p_ref, st_ref, dya_ref, dproj_in,
             dproj_ref, dbias_ref, dsink_ref, dkc_ref, dvc_ref):
        b, i, p = pl.program_id(0), pl.program_id(1), pl.program_id(2)

        @pl.when((b == 0) & (i == 0) & (p == 0))
        def _():
            dbias_ref[...] = jnp.zeros_like(dbias_ref)
            dsink_ref[...] = jnp.zeros_like(dsink_ref)

        @pl.when(i == 0)
        def _():
            dkc_ref[p] = jnp.zeros((WINDOW, kw), F32)
            dvc_ref[p] = jnp.zeros((WINDOW, kw), F32)

        dk_carry, dv_carry = dkc_ref[p], dvc_ref[p]
        dqs, dags, dbiases, dsinks, dks, dvs = [], [], [], [], [], []
        for j in range(KV_PAIR):
            heads = [j * GROUP + g for g in range(GROUP)]
            lo, hi = j * HEAD_DIM, (j + 1) * HEAD_DIM
            stack = lambda parts: jnp.concatenate(parts, axis=0)
            k = stack([kp_ref[:, lo:hi], kc_ref[:, lo:hi]])
            v = stack([vp_ref[:, lo:hi], vc_ref[:, lo:hi]])
            q = stack([q_ref[:, h * HEAD_DIM:(h + 1) * HEAD_DIM] for h in heads])
            ag = stack([ag_ref[:, h * HEAD_DIM:(h + 1) * HEAD_DIM] for h in heads])
            dout = stack([dya_ref[:, h * HEAD_DIM:(h + 1) * HEAD_DIM].astype(F32) for h in heads])
            probs_j = stack([p_ref[:, h * 2 * WINDOW:(h + 1) * 2 * WINDOW] for h in heads])
            rden = stack([st_ref[:, h:h + 1] for h in heads])
            p_sink = stack([st_ref[:, PAIR_HEADS + h:PAIR_HEADS + h + 1] for h in heads])
            dq, dk, dv, dag, dbias, dsink = _attn_group_bwd(q, k, v, ag, probs_j, rden, p_sink, dout)
            lo, hi = j * HEAD_DIM, (j + 1) * HEAD_DIM
            dks.append((dk[WINDOW:] + dk_carry[:, lo:hi]).astype(BF16))
            dvs.append((dv[WINDOW:] + dv_carry[:, lo:hi]).astype(BF16))
            dkc_ref[p, :, lo:hi] = dk[:WINDOW]
            dvc_ref[p, :, lo:hi] = dv[:WINDOW]
            for g in range(GROUP):
                blk = slice(g * WINDOW, (g + 1) * WINDOW)
                dqs.append(dq[blk].astype(BF16))
                dags.append(dag[blk].astype(BF16))
                dbiases.append(dbias[blk])
                dsinks.append(jnp.broadcast_to(jnp.sum(dsink[blk], axis=0, keepdims=True), (8, LANES)))

        for pair in range(n_pairs):
            @pl.when(p == pair)
            def _():
                for j in range(KV_PAIR):
                    col = (pair * KV_PAIR + j) * HEAD_DIM
                    dproj_ref[:, OFF_AK + col:OFF_AK + col + HEAD_DIM] = dks[j]
                    dproj_ref[:, OFF_AV + col:OFF_AV + col + HEAD_DIM] = dvs[j]
                for hh in range(PAIR_HEADS):
                    h = pair * PAIR_HEADS + hh
                    dproj_ref[:, OFF_AQ + h * HEAD_DIM:OFF_AQ + (h + 1) * HEAD_DIM] = dqs[hh]
                    dproj_ref[:, OFF_AG + h * HEAD_DIM:OFF_AG + (h + 1) * HEAD_DIM] = dags[hh]
                    dbias_ref[h] += dbiases[hh]
                    dsink_ref[h] += dsinks[hh]

    return pl.pallas_call(
        body, name="attn_bwd", grid=(B_loc, nb, n_pairs), in_specs=in_specs,
        out_specs=[_window(WINDOW, OFF_HQ, lambda b, i, p: ((b * nb + nb - 1 - i) * WINDOW, 0)),
                   pl.BlockSpec((ATTN_HEADS, WINDOW, 2 * WINDOW), lambda b, i, p: (0, 0, 0)),
                   pl.BlockSpec((ATTN_HEADS, 8, LANES), lambda b, i, p: (0, 0, 0))],
        out_shape=[jax.ShapeDtypeStruct(dproj.shape, dproj.dtype),
                   jax.ShapeDtypeStruct((ATTN_HEADS, WINDOW, 2 * WINDOW), F32),
                   jax.ShapeDtypeStruct((ATTN_HEADS, 8, LANES), F32)],
        input_output_aliases={9: 0},
        scratch_shapes=[pltpu.VMEM((n_pairs, WINDOW, kw), F32), pltpu.VMEM((n_pairs, WINDOW, kw), F32)],
        compiler_params=_params(("arbitrary", "arbitrary", "arbitrary")),
    )(proj, proj, proj, proj, proj, proj, probs, stats, dya, dproj)


class _HgrnPre:
    def __init__(self, fr, qr, lb, g_scr):
        t = lax.broadcasted_iota(jnp.int32, (CHUNK, CHUNK), 0)
        s = lax.broadcasted_iota(jnp.int32, (CHUNK, CHUNK), 1)
        self.sg = jax.nn.sigmoid(fr)
        self.f = lb + (1.0 - lb) * self.sg
        g = jnp.dot((t >= s).astype(F32), jnp.log(self.f), precision=lax.Precision.HIGHEST, preferred_element_type=F32)
        g_scr[...] = g
        self.g = g
        self.row = lax.broadcasted_iota(jnp.int32, g.shape, 0)
        self.refs = [jnp.zeros((1, g.shape[1]), F32)] + [g_scr[pl.ds(i * SUB - 1, 1), :] for i in range(1, NSUB)]
        self.gend = g_scr[pl.ds(CHUNK - 1, 1), :]
        refrow = jnp.zeros_like(g)
        for i in range(1, NSUB):
            refrow = jnp.where(self.row >= i * SUB, self.refs[i], refrow)
        self.sigq = jax.nn.sigmoid(qr)
        self.qs = qr * self.sigq
        self.k = 1.0 - self.f
        self.eg = jnp.exp(g)
        self.eqd = jnp.exp(g - refrow)
        self.ekd = [jnp.exp(jnp.where(self.row < (i + 1) * SUB, self.refs[i] - g, 0.0)) for i in range(NSUB)]
        self.ekdec = jnp.exp(self.gend - g)
        self.qg = self.qs * self.eg
        self.qd = self.qs * self.eqd
        self.kd = [self.k * e for e in self.ekd]
        self.kdec = self.k * self.ekdec
        self.egend = jnp.exp(self.gend)


def _hgrn_pair_mask():
    t = lax.broadcasted_iota(jnp.int32, (CHUNK, NSUB * CHUNK), 0)
    col = lax.broadcasted_iota(jnp.int32, (CHUNK, NSUB * CHUNK), 1)
    return ((t // SUB) == (col // CHUNK)) & ((col % CHUNK) <= t)


def _hgrn_head_out(p, lanes, state_t, v, mask):
    qg, qd = p.qg[:, lanes], p.qd[:, lanes]
    kall = jnp.concatenate([kd[:, lanes] for kd in p.kd], axis=0)
    vst = jnp.concatenate([v] * NSUB, axis=0)
    am = jnp.where(mask, _bdot(qd, kall, NT_DIMS), 0.0)
    o = _bdot(qg, state_t, NT_DIMS) + _bdot(am, vst)
    return o, (qg, qd, kall, am, vst)


def _hgrn_fwd(proj, lb, gain, B_loc, S):
    T = B_loc * S
    nc = S // CHUNK
    nh = HGRN_HEADS

    def at(off):
        return lambda b, n: ((b * nc + n) * CHUNK, off)

    vec = pl.BlockSpec((1, HGRN_WIDTH), lambda b, n: (0, 0))

    def body(q_ref, f_ref, v_ref, hg_ref, lb_ref, gain_ref, yh_ref, st_ref, state_scr, g_scr):
        @pl.when(pl.program_id(1) == 0)
        def _():
            state_scr[...] = jnp.zeros_like(state_scr)

        p = _HgrnPre(f_ref[...], q_ref[...], lb_ref[...], g_scr)
        v = v_ref[...]
        gate = gain_ref[...] * jax.nn.silu(hg_ref[...])
        mask = _hgrn_pair_mask()
        for hd in range(nh):
            lanes = slice(hd * HGRN_DIM, (hd + 1) * HGRN_DIM)
            st = state_scr[hd]
            st_ref[hd] = st
            o, _ = _hgrn_head_out(p, lanes, st, v[:, lanes], mask)
            rs = lax.rsqrt(jnp.mean(o * o, axis=-1, keepdims=True) + NORM_EPS)
            yh_ref[:, lanes] = (o * rs * gate[:, lanes]).astype(BF16)
            state_scr[hd] = st * p.egend[:, lanes] + _bdot(v[:, lanes], p.kdec[:, lanes], TN_DIMS)

    return pl.pallas_call(
        body, name="hgrn_fwd", grid=(B_loc, nc),
        in_specs=[_window(CHUNK, HGRN_WIDTH, at(OFF_HQ)), _window(CHUNK, HGRN_WIDTH, at(OFF_HF)),
                  _window(CHUNK, HGRN_WIDTH, at(OFF_HI)), _window(CHUNK, HGRN_WIDTH, at(OFF_HG)), vec, vec],
        out_specs=[pl.BlockSpec((CHUNK, HGRN_WIDTH), lambda b, n: (b * nc + n, 0)),
                   pl.BlockSpec((None, None, nh, HGRN_DIM, HGRN_DIM), lambda b, n: (b, n, 0, 0, 0))],
        out_shape=[jax.ShapeDtypeStruct((T, HGRN_WIDTH), BF16),
                   jax.ShapeDtypeStruct((B_loc, nc, nh, HGRN_DIM, HGRN_DIM), F32)],
        scratch_shapes=[pltpu.VMEM((nh, HGRN_DIM, HGRN_DIM), F32), pltpu.VMEM((CHUNK, HGRN_WIDTH), F32)],
        compiler_params=_params(("parallel", "arbitrary")),
    )(proj, proj, proj, proj, lb, gain)


def _hgrn_bwd(proj, states, lb, gain, dyh, dproj, B_loc, S):
    nc = S // CHUNK
    nh = HGRN_HEADS

    def at(off):
        return lambda b, i: ((b * nc + nc - 1 - i) * CHUNK, off)

    vec = pl.BlockSpec((1, HGRN_WIDTH), lambda b, i: (0, 0))
    in_specs = [_window(CHUNK, HGRN_WIDTH, at(OFF_HQ)), _window(CHUNK, HGRN_WIDTH, at(OFF_HF)),
                _window(CHUNK, HGRN_WIDTH, at(OFF_HI)), _window(CHUNK, HGRN_WIDTH, at(OFF_HG)), vec, vec,
                pl.BlockSpec((None, None, nh, HGRN_DIM, HGRN_DIM), lambda b, i: (b, nc - 1 - i, 0, 0, 0)),
                pl.BlockSpec((CHUNK, HGRN_WIDTH), lambda b, i: (b * nc + nc - 1 - i, 0)), ANY]
    acc_spec = pl.BlockSpec((None, 1, HGRN_WIDTH), lambda b, i: (b, 0, 0))

    def body(q_ref, f_ref, v_ref, hg_ref, lb_ref, gain_ref, st_ref, dyh_ref, dproj_in,
             dproj_ref, dlb_ref, dgain_ref, dstate_scr, g_scr, dg_scr):
        dq_ref, df_ref, dv_ref, dhg_ref = [dproj_ref.at[:, pl.ds(k * HGRN_WIDTH, HGRN_WIDTH)] for k in range(4)]

        @pl.when(pl.program_id(1) == 0)
        def _():
            dstate_scr[...] = jnp.zeros_like(dstate_scr)
            dlb_ref[...] = jnp.zeros_like(dlb_ref)
            dgain_ref[...] = jnp.zeros_like(dgain_ref)

        qr, lb, gain, hg, v = q_ref[...], lb_ref[...], gain_ref[...], hg_ref[...], v_ref[...]
        p = _HgrnPre(f_ref[...], qr, lb, g_scr)
        sgh = jax.nn.sigmoid(hg)
        sil = hg * sgh
        dy = dyh_ref[...].astype(F32)
        mask = _hgrn_pair_mask()
        dqg, dqd, dkdec, dv, dhg, dgend, dgain = [], [], [], [], [], [], []
        dkd = [[] for _ in range(NSUB)]
        heads = [slice(hd * HGRN_DIM, (hd + 1) * HGRN_DIM) for hd in range(nh)]
        sts = [st_ref[hd] for hd in range(nh)]
        dnews = [dstate_scr[hd] for hd in range(nh)]
        fwd = [_hgrn_head_out(p, lanes, st, v[:, lanes], mask) for lanes, st in zip(heads, sts)]
        for lanes, st, dnew in zip(heads, sts, dnews):
            dkdec_h = _bdot(v[:, lanes], dnew)
            dkdec.append(dkdec_h)
            dgend.append(jnp.sum(dkdec_h * p.kdec[:, lanes], axis=0, keepdims=True)
                         + jnp.sum(dnew * st, axis=0, keepdims=True) * p.egend[:, lanes])
        dos = []
        gate_grad = sgh * (1.0 + hg * (1.0 - sgh))
        for lanes, (o, _) in zip(heads, fwd):
            rs = lax.rsqrt(jnp.mean(o * o, axis=-1, keepdims=True) + NORM_EPS)
            n = o * rs
            dyn = dy[:, lanes] * n
            dgain.append(jnp.sum(dyn * sil[:, lanes], axis=0, keepdims=True))
            dhg.append(dyn * gain[:, lanes] * gate_grad[:, lanes])
            dn = dy[:, lanes] * gain[:, lanes] * sil[:, lanes]
            dos.append(rs * (dn - n * jnp.mean(dn * n, axis=-1, keepdims=True)))
        drs = []
        for hd, (lanes, st, dnew, do, (_, (qg, qd, kall, am, vst))) in enumerate(zip(heads, sts, dnews, dos, fwd)):
            dqg.append(_bdot(do, st))
            dstate_scr[hd] = _bdot(do, qg, TN_DIMS) + dnew * p.egend[:, lanes]
            drs.append(jnp.where(mask, _bdot(do, vst, NT_DIMS), 0.0))
            dvst = _bdot(am, do, TN_DIMS)
            dv.append(sum(dvst[i * CHUNK:(i + 1) * CHUNK] for i in range(NSUB)) + _bdot(p.kdec[:, lanes], dnew, NT_DIMS))
        for dr, (_, (qg, qd, kall, am, vst)) in zip(drs, fwd):
            dqd.append(_bdot(dr, kall))
            dkall = _bdot(dr, qd, TN_DIMS)
            for i in range(NSUB):
                dkd[i].append(dkall[i * CHUNK:(i + 1) * CHUNK])

        wide = lambda parts: jnp.concatenate(parts, axis=1)
        dqg, dqd, dkdec = wide(dqg), wide(dqd), wide(dkdec)
        t2 = dqd * p.qd
        dg = dqg * p.qg + t2 - dkdec * p.kdec
        dk = dkdec * p.ekdec
        dg_scr[...] = jnp.zeros_like(dg_scr)
        for i in range(NSUB):
            dkd_i = wide(dkd[i])
            tk = jnp.where(p.row < (i + 1) * SUB, dkd_i * p.kd[i], 0.0)
            dg = dg - tk
            dk = dk + dkd_i * p.ekd[i]
            if i >= 1:
                in_blk = (p.row >= i * SUB) & (p.row < (i + 1) * SUB)
                dg_scr[pl.ds(i * SUB - 1, 1), :] = (jnp.sum(tk, axis=0, keepdims=True)
                                                    - jnp.sum(jnp.where(in_blk, t2, 0.0), axis=0, keepdims=True))
        dg_scr[pl.ds(CHUNK - 1, 1), :] = wide(dgend)
        t = lax.broadcasted_iota(jnp.int32, (CHUNK, CHUNK), 0)
        s = lax.broadcasted_iota(jnp.int32, (CHUNK, CHUNK), 1)
        dlogf = jnp.dot((t <= s).astype(F32), dg + dg_scr[...], precision=lax.Precision.HIGHEST, preferred_element_type=F32)
        df = dlogf / p.f - dk
        df_ref[...] = (df * (1.0 - lb) * p.sg * (1.0 - p.sg)).astype(BF16)
        dlb_ref[...] += jnp.sum(df * (1.0 - p.sg), axis=0, keepdims=True)
        dq_ref[...] = ((dqg * p.eg + dqd * p.eqd) * p.sigq * (1.0 + qr * (1.0 - p.sigq))).astype(BF16)
        dv_ref[...] = wide(dv).astype(BF16)
        dhg_ref[...] = wide(dhg).astype(BF16)
        dgain_ref[...] += wide(dgain)

    acc = jax.ShapeDtypeStruct((B_loc, 1, HGRN_WIDTH), F32)
    return pl.pallas_call(
        body, name="hgrn_bwd", grid=(B_loc, nc), in_specs=in_specs,
        out_specs=[_window(CHUNK, 4 * HGRN_WIDTH, at(OFF_HQ)), acc_spec, acc_spec],
        out_shape=[jax.ShapeDtypeStruct(dproj.shape, dproj.dtype), acc, acc],
        input_output_aliases={8: 0},
        scratch_shapes=[pltpu.VMEM((nh, HGRN_DIM, HGRN_DIM), F32), pltpu.VMEM((CHUNK, HGRN_WIDTH), F32),
                        pltpu.VMEM((CHUNK, HGRN_WIDTH), F32)],
        compiler_params=_params(("parallel", "arbitrary")),
    )(proj, proj, proj, proj, lb, gain, states, dyh, dproj)


def _adamw(w, g, m, v, name):
    R, C = w.shape
    tr = _pick(R, (128, 64, 32, 16, 8)) if C > 1024 else _pick(R, (512, 256, 128, 64, 32, 16, 8))

    def body(w_ref, g_ref, m_ref, v_ref, d_ref, nm_ref, nv_ref, g_out_ref):
        gv = g_ref[...]
        g_out_ref[...] = gv
        nm = ADAM_B1 * m_ref[...] + (1.0 - ADAM_B1) * gv
        nv = ADAM_B2 * v_ref[...] + (1.0 - ADAM_B2) * (gv * gv)
        m_hat = nm / (1.0 - ADAM_B1 ** ADAM_STEP)
        v_hat = nv / (1.0 - ADAM_B2 ** ADAM_STEP)
        d_ref[...] = -ADAM_LR * (m_hat / (jnp.sqrt(v_hat) + ADAM_EPS) + ADAM_WD * w_ref[...])
        nm_ref[...] = nm
        nv_ref[...] = nv

    blk = pl.BlockSpec((tr, C), lambda i: (i, 0))
    o = jax.ShapeDtypeStruct((R, C), F32)
    return pl.pallas_call(
        body, name=name, grid=(R // tr,), in_specs=[blk] * 4, out_specs=[blk] * 4, out_shape=[o, o, o, o],
        compiler_params=_params(("parallel",)),
    )(w, g, m, v)


ANY = pl.BlockSpec(memory_space=pl.ANY)
VMEM_SPEC = pl.BlockSpec(memory_space=pltpu.VMEM)


def _place():
    x, y, c = lax.axis_index("x"), lax.axis_index("y"), lax.axis_index("c")
    other_chips = [(1 - x, y), (x, 1 - y), (1 - x, 1 - y)]
    return x, y, c, other_chips


def _cast_into_full(w, ax, s_arr, name):
    R, C = w.shape
    tr = _pick(R, (256, 128))
    nr = R // tr

    def body(s_ref, w_ref, o_ref):
        o_ref[...] = w_ref[...].astype(BF16)

    if ax == 1:
        shape, o_map = (R, N_CHIPS * C), lambda i, s: (i, s[0])
    else:
        shape, o_map = (N_CHIPS * R, C), lambda i, s: (s[0] * nr + i, 0)
    return pl.pallas_call(
        body, name=name,
        grid_spec=pltpu.PrefetchScalarGridSpec(
            num_scalar_prefetch=1, grid=(nr,), in_specs=[pl.BlockSpec((tr, C), lambda i, s: (i, 0))],
            out_specs=pl.BlockSpec((tr, C), o_map)),
        out_shape=jax.ShapeDtypeStruct(shape, BF16),
        compiler_params=_params(("parallel",)),
    )(s_arr, w)


class _Gather:
    def __init__(self, fulls, shard_shapes, axes, tag):
        self.shapes, self.axes, self.tag, self.nw = shard_shapes, axes, tag, len(fulls)
        self.fulls, self.sems, self.token = list(fulls), {}, None

    def start(self, peers, after=None):
        nw, np_ = self.nw, len(peers)

        def body(*refs):
            ins, sems = refs[:nw], refs[nw + (after is not None):nw + (after is not None) + 2 * np_]
            for k, j in enumerate(peers):
                for cp in self._peer_copies(ins, sems[2 * k], sems[2 * k + 1], j):
                    cp.start()
            refs[-1][...] = jnp.zeros_like(refs[-1])

        out = pl.pallas_call(
            body, name="gather_start_%s_%s" % (self.tag, "".join(map(str, peers))),
            out_shape=(*[pltpu.SemaphoreType.DMA((nw,))] * (2 * np_),
                       *[pltpu.HBM(f.shape, f.dtype) for f in self.fulls], jax.ShapeDtypeStruct((8, LANES), F32)),
            in_specs=[HBM_SPEC] * nw + ([] if after is None else [ANY]),
            out_specs=(*[SEM_SPEC] * (2 * np_), *[HBM_SPEC] * nw, VMEM_SPEC),
            input_output_aliases={k: 2 * np_ + k for k in range(nw)},
            compiler_params=pltpu.CompilerParams(has_side_effects=DATAFLOW),
        )(*[pltpu.with_memory_space_constraint(f, pltpu.HBM) for f in self.fulls], *(() if after is None else (after,)))
        for k, j in enumerate(peers):
            self.sems[j] = (out[2 * k], out[2 * k + 1])
        self.fulls = list(out[2 * np_:2 * np_ + nw])
        self.token = out[-1]

    def _region(self, ref, i, t, half):
        R, C = self.shapes[i]
        hr = R // 2
        if self.axes[i] == 1:
            return ref.at[pl.ds(half * hr, hr), pl.ds(pl.multiple_of(t * C, LANES), C)]
        return ref.at[pl.ds(t * R + half * hr, hr), :]

    def _peer_copies(self, refs, send_sems, recv_sems, j):
        x, y, c, chips = _place()
        s = 2 * x + y
        return [pltpu.make_async_remote_copy(
            src_ref=self._region(refs[i], i, s, c), dst_ref=self._region(refs[i], i, s, c), send_sem=send_sems.at[i],
            recv_sem=recv_sems.at[i], device_id=(*chips[j], c), device_id_type=MESH) for i in range(self.nw)]

    def wait(self, peers, after):
        nw, np_ = self.nw, len(peers)

        def body(*refs):
            ins, sems = refs[:nw], refs[nw:nw + 2 * np_]
            for k, j in enumerate(peers):
                for cp in self._peer_copies(ins, sems[2 * k], sems[2 * k + 1], j):
                    cp.wait_send()
                    cp.wait_recv()

        sem_args = [s for j in peers for s in self.sems[j]]
        out = pl.pallas_call(
            body, name="gather_wait_%s_%s" % (self.tag, "".join(map(str, peers))),
            out_shape=tuple(pltpu.HBM(f.shape, f.dtype) for f in self.fulls),
            in_specs=[HBM_SPEC] * nw + [SEM_SPEC] * (2 * np_) + [ANY], out_specs=tuple([HBM_SPEC] * nw),
            input_output_aliases={k: k for k in range(nw)},
            compiler_params=pltpu.CompilerParams(has_side_effects=DATAFLOW),
        )(*self.fulls, *sem_args, after)
        self.fulls = list(out)

    def forward(self, peers):
        nw, np_ = self.nw, len(peers)

        def body(*refs):
            ins, outs = refs[:nw], refs[nw:2 * nw]
            send_sems, recv_sems = refs[2 * nw:]
            x, y, c, chips = _place()
            cps = []
            for i in range(nw):
                for k, j in enumerate(peers):
                    t = 2 * chips[j][0] + chips[j][1]
                    cp = pltpu.make_async_remote_copy(
                        src_ref=self._region(ins[i], i, t, c), dst_ref=self._region(outs[i], i, t, c),
                        send_sem=send_sems.at[i * np_ + k], recv_sem=recv_sems.at[i * np_ + k],
                        device_id=(x, y, 1 - c), device_id_type=MESH)
                    cp.start()
                    cps.append(cp)
            for cp in cps:
                cp.wait()

        out = pl.pallas_call(
            body, name="gather_forward_%s_%s" % (self.tag, "".join(map(str, peers))),
            in_specs=[ANY] * nw, out_specs=[ANY] * nw,
            out_shape=[jax.ShapeDtypeStruct(f.shape, f.dtype) for f in self.fulls],
            input_output_aliases={i: i for i in range(nw)},
            scratch_shapes=[pltpu.SemaphoreType.DMA((nw * np_,)), pltpu.SemaphoreType.DMA((nw * np_,))],
        )(*self.fulls)
        self.fulls = list(out)


def _matmul_slab(a, wfull, slab_arr, prev, after, name):
    M, K = a.shape
    N = wfull.shape[1]
    nslab = N // N_CHIPS
    tn = _pick(nslab, (2688, 896, 512, 384, 256, 128))
    tm = _pick(M, (512, 256, 128) if tn > 1024 else (1024, 512, 256, 128))
    per = nslab // tn
    extra = [e for e in (prev, after) if e is not None]

    def body(slab_ref, a_ref, b_ref, *rest):
        rest[len(extra)][...] = jnp.dot(a_ref[...], b_ref[...], preferred_element_type=F32)

    return pl.pallas_call(
        body, name=name,
        grid_spec=pltpu.PrefetchScalarGridSpec(
            num_scalar_prefetch=1, grid=(M // tm, per),
            in_specs=[pl.BlockSpec((tm, K), lambda i, j, sl: (i, 0)),
                      pl.BlockSpec((K, tn), lambda i, j, sl: (0, sl[0] * per + j))] + [ANY] * len(extra),
            out_specs=pl.BlockSpec((tm, tn), lambda i, j, sl: (i, sl[0] * per + j))),
        out_shape=jax.ShapeDtypeStruct((M, N), F32),
        input_output_aliases={} if prev is None else {3: 0},
        compiler_params=_params(("parallel", "arbitrary")),
    )(slab_arr, a, wfull, *extra)


def _exchange_sibling_halves(gs, name):
    nw = len(gs)

    def body(*refs):
        ins, outs = refs[:nw], refs[nw:2 * nw]
        send_sems, recv_sems = refs[2 * nw:]
        x, y, c, _ = _place()
        cps = []
        for i in range(nw):
            cp = pltpu.make_async_remote_copy(src_ref=ins[i].at[:, 1 - c], dst_ref=outs[i], send_sem=send_sems.at[i],
                                              recv_sem=recv_sems.at[i], device_id=(x, y, 1 - c), device_id_type=MESH)
            cp.start()
            cps.append(cp)
        for cp in cps:
            cp.wait()

    return pl.pallas_call(
        body, name=name, in_specs=[ANY] * nw, out_specs=[ANY] * nw,
        out_shape=[jax.ShapeDtypeStruct((g.shape[0],) + g.shape[2:], g.dtype) for g in gs],
        scratch_shapes=[pltpu.SemaphoreType.DMA((nw,)), pltpu.SemaphoreType.DMA((nw,))],
    )(*gs)


HBM_SPEC = pl.BlockSpec(memory_space=pltpu.HBM)
SEM_SPEC = pl.BlockSpec(memory_space=pltpu.SEMAPHORE)
DATAFLOW = pltpu.SideEffectType.DATAFLOW_SIDE_EFFECTING


def _chip_copies(ins, lands, send_sems, recv_sems):
    x, y, c, chips = _place()
    return [pltpu.make_async_remote_copy(
        src_ref=ins[i].at[2 * chip[0] + chip[1]], dst_ref=lands[i].at[j], send_sem=send_sems.at[i * 3 + j],
        recv_sem=recv_sems.at[i * 3 + j], device_id=(*chip, c), device_id_type=MESH)
        for i in range(len(ins)) for j, chip in enumerate(chips)]


def _chips_send_start(ss, name):
    nw = len(ss)
    lands = [pltpu.with_memory_space_constraint(lax.empty((N_CHIPS - 1,) + s.shape[1:], s.dtype), pltpu.HBM) for s in ss]

    def body(*refs):
        ins, land_refs = refs[:nw], refs[nw:2 * nw]
        send_sems, recv_sems = refs[2 * nw], refs[2 * nw + 1]
        token = refs[-1]
        for cp in _chip_copies(ins, land_refs, send_sems, recv_sems):
            cp.start()
        token[...] = jnp.zeros_like(token)

    n = 3 * nw
    out = pl.pallas_call(
        body, name=name,
        out_shape=(pltpu.SemaphoreType.DMA((n,)), pltpu.SemaphoreType.DMA((n,)),
                   *[pltpu.HBM(s.shape, s.dtype) for s in ss], *[pltpu.HBM(l.shape, l.dtype) for l in lands],
                   jax.ShapeDtypeStruct((8, LANES), F32)),
        in_specs=[HBM_SPEC] * (2 * nw), out_specs=(SEM_SPEC, SEM_SPEC, *[HBM_SPEC] * (2 * nw), VMEM_SPEC),
        input_output_aliases={k: 2 + k for k in range(2 * nw)},
        compiler_params=pltpu.CompilerParams(has_side_effects=DATAFLOW),
    )(*[pltpu.with_memory_space_constraint(s, pltpu.HBM) for s in ss], *lands)
    return out[0], out[1], list(out[2:2 + nw]), list(out[2 + nw:2 + 2 * nw]), out[-1]


def _chips_send_wait(send_sems, recv_sems, ss, lands, after, name):
    nw = len(ss)

    def body(*refs):
        ins, land_refs = refs[:nw], refs[nw:2 * nw]
        s_sems, r_sems = refs[2 * nw], refs[2 * nw + 1]
        for cp in _chip_copies(ins, land_refs, s_sems, r_sems):
            cp.wait_send()
            cp.wait_recv()

    out = pl.pallas_call(
        body, name=name,
        out_shape=(*[pltpu.HBM(s.shape, s.dtype) for s in ss], *[pltpu.HBM(l.shape, l.dtype) for l in lands]),
        in_specs=[HBM_SPEC] * (2 * nw) + [SEM_SPEC, SEM_SPEC, ANY], out_specs=tuple([HBM_SPEC] * (2 * nw)),
        input_output_aliases={k: k for k in range(2 * nw)},
        compiler_params=pltpu.CompilerParams(has_side_effects=DATAFLOW),
    )(*ss, *lands, send_sems, recv_sems, after)
    return list(out[nw:])


def _sum_small(pack):
    rows = pack.shape[0]

    def body(pack_ref, sum_ref, all_ref, send_sems, recv_sems):
        x, y, c, _ = _place()
        me = 4 * x + 2 * y + c
        all_ref[me] = pack_ref[...]
        cps = []
        for k in range(1, N_DEV):
            to = (1 - x if k & 4 else x, 1 - y if k & 2 else y, 1 - c if k & 1 else c)
            cp = pltpu.make_async_remote_copy(
                src_ref=pack_ref, dst_ref=all_ref.at[me], send_sem=send_sems.at[k - 1],
                recv_sem=recv_sems.at[k - 1], device_id=to, device_id_type=MESH)
            cp.start()
            cps.append(cp)
        for cp in cps:
            cp.wait()
        total = all_ref[0]
        for d in range(1, N_DEV):
            total = total + all_ref[d]
        sum_ref[...] = total

    return pl.pallas_call(
        body, name="sum_small_grads", in_specs=[VMEM_SPEC], out_specs=VMEM_SPEC,
        out_shape=jax.ShapeDtypeStruct(pack.shape, F32),
        scratch_shapes=[pltpu.VMEM((N_DEV, rows, LANES), F32), pltpu.SemaphoreType.DMA((N_DEV - 1,)),
                        pltpu.SemaphoreType.DMA((N_DEV - 1,))],
    )(pack)


def _share_with_sibling(fs, name):
    nw = len(fs)

    def body(*refs):
        ins, outs = refs[:nw], refs[nw:2 * nw]
        send_sems, recv_sems = refs[2 * nw:]
        x, y, c, _ = _place()
        cps = []
        for i in range(nw):
            cp = pltpu.make_async_remote_copy(src_ref=ins[i].at[c], dst_ref=outs[i].at[c], send_sem=send_sems.at[i],
                                              recv_sem=recv_sems.at[i], device_id=(x, y, 1 - c), device_id_type=MESH)
            cp.start()
            cps.append(cp)
        for cp in cps:
            cp.wait()

    return pl.pallas_call(
        body, name=name, in_specs=[ANY] * nw, out_specs=[ANY] * nw,
        out_shape=[jax.ShapeDtypeStruct(f.shape, f.dtype) for f in fs],
        input_output_aliases={i: i for i in range(nw)},
        scratch_shapes=[pltpu.SemaphoreType.DMA((nw,)), pltpu.SemaphoreType.DMA((nw,))],
    )(*fs)


def _sum_sibling(g, land, where_arr, name):
    P, Q = g.shape[-2:]
    tp = _pick(P, (256, 128, 64))

    def body(w_ref, g_ref, l_ref, s_ref):
        s_ref[...] = (g_ref[...].astype(F32) + l_ref[...].astype(F32)).astype(BF16)

    if g.ndim == 4:
        g_spec = pl.BlockSpec((None, None, tp, Q), lambda j, r, w: (w[1 + j], w[0], r, 0))
    else:
        g_spec = pl.BlockSpec((None, tp, Q), lambda j, r, w: (w[1 + j], r, 0))
    slab = pl.BlockSpec((None, tp, Q), lambda j, r, w: (w[1 + j], r, 0))
    return pl.pallas_call(
        body, name=name,
        grid_spec=pltpu.PrefetchScalarGridSpec(
            num_scalar_prefetch=1, grid=(N_CHIPS - 1, P // tp), in_specs=[g_spec, slab], out_specs=slab),
        out_shape=jax.ShapeDtypeStruct((N_CHIPS, P, Q), BF16),
        compiler_params=_params(("parallel", "parallel")),
    )(where_arr, g, land)


def _sum_chips(g, land, recv, sc_arr, name):
    P, Q = g.shape[-2:]
    tp = _pick(P, (256, 128, 64))

    def body(sc_ref, g_ref, l_ref, r_ref, f_ref):
        acc = g_ref[...].astype(F32) + l_ref[...].astype(F32)
        for j in range(N_CHIPS - 1):
            acc = acc + r_ref[j].astype(F32)
        f_ref[...] = acc

    if g.ndim == 4:
        g_spec = pl.BlockSpec((None, None, tp, Q), lambda r, sc: (sc[0], sc[1], r, 0))
    else:
        g_spec = pl.BlockSpec((None, tp, Q), lambda r, sc: (sc[0], r, 0))
    return pl.pallas_call(
        body, name=name,
        grid_spec=pltpu.PrefetchScalarGridSpec(
            num_scalar_prefetch=1, grid=(P // tp,),
            in_specs=[g_spec,
                      pl.BlockSpec((None, tp, Q), lambda r, sc: (sc[0], r, 0)),
                      pl.BlockSpec((N_CHIPS - 1, tp, Q), lambda r, sc: (0, r, 0))],
            out_specs=pl.BlockSpec((None, tp, Q), lambda r, sc: (sc[1], r, 0))),
        out_shape=jax.ShapeDtypeStruct((2, P, Q), F32),
        compiler_params=_params(("parallel",)),
    )(sc_arr, g, land, recv)


class _Reduction:
    def __init__(self, names, pieces, lands, flight):
        self.names, self.pieces, self.lands = names, pieces, lands
        self.send_sems, self.recv_sems, self.sums, self.zones, self.token = flight


def _reduce_start(pieces, lands, names, where_arr):
    tag = names[0] if len(names) == 1 else "branches"
    sums = [_sum_sibling(g, l, where_arr, "sum_sibling_" + nm) for g, l, nm in zip(pieces, lands, names)]
    return _Reduction(names, pieces, lands, _chips_send_start(sums, "grads_to_chips_start_" + tag))


def _wgrad_half(a, b, half_arr, name, after=None):
    K, M = a.shape
    nslab = b.shape[1] // N_CHIPS
    tm = M // 2
    tn = _pick(nslab, (896, 512, 384, 256, 128))
    per = nslab // tn
    assert 4 * K * (tm + tn) <= MATMUL_OPERAND_BYTES

    def body(h_ref, a_ref, b_ref, *rest):
        rest[-1][...] = lax.dot_general(a_ref[...], b_ref[...], TN_DIMS, preferred_element_type=F32).astype(BF16)

    return pl.pallas_call(
        body, name=name,
        grid_spec=pltpu.PrefetchScalarGridSpec(
            num_scalar_prefetch=1, grid=(b.shape[1] // tn,),
            in_specs=[pl.BlockSpec((K, tm), lambda j, h: (0, h[0])), pl.BlockSpec((K, tn), lambda j, h: (0, j))]
            + ([] if after is None else [ANY]),
            out_specs=pl.BlockSpec((None, tm, tn), lambda j, h: (j // per, 0, j % per))),
        out_shape=jax.ShapeDtypeStruct((N_CHIPS, tm, nslab), BF16),
        compiler_params=_params(("parallel",)),
    )(*((half_arr, a, b) if after is None else (half_arr, a, b, after)))


def _sibling_send_start(g, name):
    zone = pltpu.with_memory_space_constraint(lax.empty(g.shape, g.dtype), pltpu.HBM)

    def body(g_ref, zone_ref, send_sem, recv_sem, g_thru, zone_thru, token):
        x, y, c, _ = _place()
        pltpu.make_async_remote_copy(src_ref=g_ref, dst_ref=zone_ref, send_sem=send_sem, recv_sem=recv_sem,
                                     device_id=(x, y, 1 - c), device_id_type=MESH).start()
        token[...] = jnp.zeros_like(token)

    return pl.pallas_call(
        body, name=name,
        out_shape=(pltpu.SemaphoreType.DMA(()), pltpu.SemaphoreType.DMA(()), pltpu.HBM(g.shape, g.dtype),
                   pltpu.HBM(g.shape, g.dtype), jax.ShapeDtypeStruct((8, LANES), F32)),
        in_specs=[HBM_SPEC, HBM_SPEC], out_specs=(SEM_SPEC, SEM_SPEC, HBM_SPEC, HBM_SPEC, VMEM_SPEC),
        input_output_aliases={0: 2, 1: 3},
        compiler_params=pltpu.CompilerParams(has_side_effects=DATAFLOW),
    )(pltpu.with_memory_space_constraint(g, pltpu.HBM), zone)


def _sibling_send_wait(send_sem, recv_sem, g, zone, after, name):
    def body(g_ref, zone_ref, s_sem, r_sem, after_ref, g_out, zone_out):
        x, y, c, _ = _place()
        cp = pltpu.make_async_remote_copy(src_ref=g_ref, dst_ref=zone_ref, send_sem=s_sem, recv_sem=r_sem,
                                          device_id=(x, y, 1 - c), device_id_type=MESH)
        cp.wait_send()
        cp.wait_recv()

    return pl.pallas_call(
        body, name=name, out_shape=(pltpu.HBM(g.shape, g.dtype), pltpu.HBM(zone.shape, zone.dtype)),
        in_specs=[HBM_SPEC, HBM_SPEC, SEM_SPEC, SEM_SPEC, ANY], out_specs=(HBM_SPEC, HBM_SPEC),
        input_output_aliases={0: 0, 1: 1},
        compiler_params=pltpu.CompilerParams(has_side_effects=DATAFLOW),
    )(g, zone, send_sem, recv_sem, after)[1]


def _reduce_finish(red, after, sc_arr):
    tag = red.names[0] if len(red.names) == 1 else "branches"
    recvs = _chips_send_wait(red.send_sems, red.recv_sems, red.sums, red.zones, after, "grads_to_chips_wait_" + tag)
    halves = [_sum_chips(g, l, r, sc_arr, "sum_chips_" + nm) for g, l, r, nm in zip(red.pieces, red.lands, recvs, red.names)]
    return _share_with_sibling(halves, "grads_share_sibling_" + tag)


def _t5_bucket(dist):
    max_exact = REL_BUCKETS // 2
    d = jnp.maximum(dist, 0)
    df = jnp.maximum(d, 1).astype(F32)
    large = max_exact + (jnp.log(df / max_exact) / math.log(REL_MAX_DIST / max_exact)
                         * (REL_BUCKETS - max_exact)).astype(jnp.int32)
    large = jnp.minimum(large, REL_BUCKETS - 1)
    return jnp.where(d < max_exact, d, large)


def _bucket_table():
    qi = jnp.arange(WINDOW)[:, None]
    si = jnp.arange(2 * WINDOW)[None, :]
    return _t5_bucket(qi + WINDOW - si)


TILE_WORDS = 8 * LANES


def _tile_rows(shape):
    return -(-math.prod(shape) // TILE_WORDS) * 8


def _rows_of(a):
    flat = a.reshape(-1).astype(F32)
    n = _tile_rows(a.shape) * LANES
    return jnp.pad(flat, (0, n - flat.shape[0])).reshape(-1, LANES)


def _pack_rows(parts):
    return jnp.concatenate([_rows_of(p) for p in parts], axis=0)


def _unpack_rows(packed, shapes):
    out, at = [], 0
    for shp in shapes:
        n, nr = math.prod(shp), _tile_rows(shp)
        out.append(packed[at:at + nr].reshape(-1)[:n].reshape(shp))
        at += nr
    return out


def kernel(x, norm_pre, w_in, rel_bias, attn_sinks, lb_logits, hgrn_norm, w_branch_attn, w_branch_hgrn, w_out, norm_post, loss_target, m_norm_pre, m_w_in, m_rel_bias, m_attn_sinks, m_lb_logits, m_hgrn_norm, m_w_branch_attn, m_w_branch_hgrn, m_w_out, m_norm_post, v_norm_pre, v_w_in, v_rel_bias, v_attn_sinks, v_lb_logits, v_hgrn_norm, v_w_branch_attn, v_w_branch_hgrn, v_w_out, v_norm_post):
    B_loc, S, D = x.shape
    T = B_loc * S
    x2 = x.reshape(T, D)
    tgt2 = loss_target.reshape(T, D)
    my_x, my_y, my_c = lax.axis_index("x"), lax.axis_index("y"), lax.axis_index("c")

    c_arr = jnp.reshape(my_c, (1,)).astype(jnp.int32)
    s_arr = jnp.reshape(2 * my_x + my_y, (1,)).astype(jnp.int32)
    sc_arr = jnp.concatenate([s_arr, c_arr])
    shard_ws = [w_in[0], w_branch_attn[0], w_branch_hgrn[0], w_out[0]]
    shard_axes = (1, 1, 1, 0)
    names = ["w_in", "w_branch_attn", "w_branch_hgrn", "w_out"]
    placed = [_cast_into_full(w, ax, s_arr, "cast_" + nm) for w, ax, nm in zip(shard_ws, shard_axes, names)]
    peer_slabs = [jnp.reshape(t, (1,)).astype(jnp.int32)
                  for t in (2 * (1 - my_x) + my_y, 2 * my_x + 1 - my_y, 2 * (1 - my_x) + 1 - my_y)]

    buckets = _bucket_table()
    onehot = (buckets.reshape(-1)[:, None] == jnp.arange(REL_BUCKETS)[None, :]).astype(F32)
    bias_tab = jnp.dot(onehot, rel_bias.astype(F32), precision=lax.Precision.HIGHEST).T.reshape(ATTN_HEADS, WINDOW, 2 * WINDOW)
    sinks_b = jnp.broadcast_to(attn_sinks[0].astype(F32)[:, None, None], (ATTN_HEADS, 8, LANES))
    lb_fn = lambda l: jnp.cumsum(jax.nn.softmax(l.astype(F32), axis=0), axis=0)[:1]
    lb, lb_vjp = jax.vjp(lb_fn, lb_logits)
    gain_h = hgrn_norm[0].reshape(1, HGRN_WIDTH)

    h, rstd = _rmsnorm_fwd(x2, norm_pre)
    gather_in = _Gather(placed[:1], [shard_ws[0].shape], shard_axes[:1], "w_in")
    gather_in.start([0, 1])
    proj = _matmul_slab(h, gather_in.fulls[0], s_arr, None, gather_in.token, "in_proj_own")
    gather_in.wait([0, 1], proj)
    gather_in.forward([0, 1])
    gather_in.start([2])
    proj = _matmul_slab(h, gather_in.fulls[0], peer_slabs[0], proj, gather_in.token, "in_proj_peer0")
    proj = _matmul_slab(h, gather_in.fulls[0], peer_slabs[1], proj, None, "in_proj_peer1")
    gather_in.wait([2], proj)
    gather_in.forward([2])
    gather_rest = _Gather(placed[1:], [w.shape for w in shard_ws[1:]], shard_axes[1:], "rest")
    gather_rest.start([0, 1, 2], gather_in.fulls[0])
    proj = _matmul_slab(h, gather_in.fulls[0], peer_slabs[2], proj, gather_rest.token, "in_proj_peer2")
    win_f = gather_in.fulls[0]
    ya, attn_probs, attn_stats = _attn_fwd(proj, bias_tab, sinks_b, B_loc, S)
    yh, states = _hgrn_fwd(proj, lb, gain_h, B_loc, S)
    gather_rest.wait([0, 1, 2], yh)
    gather_rest.forward([0, 1, 2])
    wba_f, wbh_f, wout_f = gather_rest.fulls
    ua = _matmul(ya, wba_f, "nn", BF16, "branch_attn_proj")
    uh, merged = _branch_proj_merge(yh, wbh_f, proj, ua)
    yv = _matmul(merged, wout_f, "nn", F32, "out_proj")
    dy, dout, loss_p, gnpost_p = _post_loss(yv, x2, tgt2, norm_post)

    g_wout = _matmul(merged, dy, "tn", BF16, "out_proj_wgrad")
    d_ua, d_uh, dproj = _out_dgrad_merge_bwd(dy, wout_f, proj, ua, uh)
    g_wba = _matmul(ya, d_ua, "tn", BF16, "branch_attn_wgrad", slabs=N_CHIPS)
    g_wbh = _matmul(yh, d_uh, "tn", BF16, "branch_hgrn_wgrad", slabs=N_CHIPS)
    where_arr = jnp.concatenate([c_arr] + peer_slabs)
    late_pieces = [g.reshape(N_CHIPS, 2, -1, g.shape[-1]) for g in (g_wba, g_wbh, g_wout)]
    late = _reduce_start(late_pieces, _exchange_sibling_halves(late_pieces, "grads_to_sibling_branches"), names[1:], where_arr)
    d_ya = _matmul(d_ua, wba_f, "nt", BF16, "branch_attn_dgrad", after=late.token)
    d_yh = _matmul(d_uh, wbh_f, "nt", BF16, "branch_hgrn_dgrad", after=late.token)
    dproj, dbias_p, dsinks_p = _attn_bwd(proj, attn_probs, attn_stats, d_ya, dproj, B_loc, S)
    dproj, dlb_p, dgain_p = _hgrn_bwd(proj, states, lb, gain_h, d_yh, dproj, B_loc, S)
    g_give = _wgrad_half(h, dproj, 1 - c_arr, "in_proj_wgrad_sibling_half")
    s_sem, r_sem, g_give, zone, token = _sibling_send_start(g_give, "grads_to_sibling_start_w_in")
    g_keep = _wgrad_half(h, dproj, c_arr, "in_proj_wgrad_own_half", after=token)
    land = _sibling_send_wait(s_sem, r_sem, g_give, zone, g_keep, "grads_to_sibling_wait_w_in")
    last = _reduce_start([g_keep], [land], names[:1], where_arr)
    dh = _matmul(dproj, win_f, "nt", F32, "in_proj_dgrad", after=last.token)
    grad_x2, gnpre_p = _rmsnorm_bwd(dh, x2, rstd, norm_pre, dout)
    shared = _reduce_finish(last, grad_x2, sc_arr) + _reduce_finish(late, grad_x2, sc_arr)

    grelb_p = jnp.dot(dbias_p.reshape(ATTN_HEADS, -1), onehot, precision=lax.Precision.HIGHEST).T
    gsinks_p = dsinks_p[:, 0, 0]
    dlb_sum = jnp.sum(dlb_p, axis=0).reshape(1, HGRN_WIDTH)
    ghn_p = jnp.sum(dgain_p, axis=0).reshape(HGRN_HEADS, HGRN_DIM)
    small_parts = [gnpre_p, gnpost_p, grelb_p, gsinks_p, dlb_sum, ghn_p, loss_p]
    small_shapes = [p.shape for p in small_parts]
    pack_sum = _sum_small(_pack_rows(small_parts))
    big_w = [w_in, w_branch_attn, w_branch_hgrn, w_out]
    big_m = [m_w_in, m_w_branch_attn, m_w_branch_hgrn, m_w_out]
    big_v = [v_w_in, v_w_branch_attn, v_w_branch_hgrn, v_w_out]
    big = {}
    for nm, gs, w, m, v in zip(names, shared, big_w, big_m, big_v):
        shp = w.shape
        g2 = gs.reshape(shp[1], shp[2])
        d, nm_, nv_, g_out = _adamw(w[0], g2, m[0], v[0], "adamw_" + nm)
        big[nm] = tuple(a.reshape(shp) for a in (g_out, d, nm_, nv_))

    gnpre, gnpost, grelb, gsinks, dlb, ghn, loss = _unpack_rows(pack_sum, small_shapes)
    (g_lb_logits,) = lb_vjp(dlb)
    small_names = ["norm_pre", "rel_bias", "attn_sinks", "lb_logits", "hgrn_norm", "norm_post"]
    small_w = [norm_pre, rel_bias, attn_sinks, lb_logits, hgrn_norm, norm_post]
    small_m = [m_norm_pre, m_rel_bias, m_attn_sinks, m_lb_logits, m_hgrn_norm, m_norm_post]
    small_v = [v_norm_pre, v_rel_bias, v_attn_sinks, v_lb_logits, v_hgrn_norm, v_norm_post]
    small_g = [gnpre.reshape(norm_pre.shape), grelb.reshape(rel_bias.shape), gsinks.reshape(attn_sinks.shape),
               g_lb_logits.reshape(lb_logits.shape), ghn.reshape(hgrn_norm.shape), gnpost.reshape(norm_post.shape)]
    shapes = [w.shape for w in small_w]
    d_s, nm_s, nv_s, _ = _adamw(_pack_rows(small_w), _pack_rows(small_g), _pack_rows(small_m), _pack_rows(small_v),
                             "adamw_small")
    small = {}
    for nm, g, d, m_, v_ in zip(small_names, small_g, _unpack_rows(d_s, shapes), _unpack_rows(nm_s, shapes),
                                _unpack_rows(nv_s, shapes)):
        small[nm] = (g, d, m_, v_)

    allw = {**big, **small}
    order = ["norm_pre", "w_in", "rel_bias", "attn_sinks", "lb_logits", "hgrn_norm", "w_branch_attn", "w_branch_hgrn",
             "w_out", "norm_post"]
    outs = [loss.reshape(()), grad_x2.reshape(B_loc, S, D)]
    for k in range(4):
        outs += [allw[nm][k] for nm in order]
    return tuple(outs)
```

```python
import functools
import math

import jax
import jax.numpy as jnp
from jax import lax
from jax.experimental import pallas as pl
from jax.experimental.pallas import tpu as pltpu

F32 = jnp.float32
BF16 = jnp.bfloat16
MESH = pl.DeviceIdType.MESH

ATTN_HEADS = 16
ATTN_KV_HEADS = 4
HEAD_DIM = 64
GROUP = ATTN_HEADS // ATTN_KV_HEADS
WINDOW = 128
ATTN_WIDTH = ATTN_HEADS * HEAD_DIM
KV_WIDTH = ATTN_KV_HEADS * HEAD_DIM
HGRN_HEADS = 8
HGRN_DIM = 128
HGRN_WIDTH = HGRN_HEADS * HGRN_DIM
CHUNK = 64
SUB = 16
NSUB = CHUNK // SUB
REL_BUCKETS = 32
REL_MAX_DIST = 128
NORM_EPS = 1e-6
ADAM_LR = 0.001
ADAM_B1 = 0.9
ADAM_B2 = 0.999
ADAM_EPS = 1e-08
ADAM_WD = 0.01
ADAM_STEP = 10
LANES = 128
N_CHIPS = 4
N_DEV = 8
VMEM_LIMIT = 48 * 1024 * 1024
MATMUL_OPERAND_BYTES = 34 * 1024 * 1024
MATMUL_VMEM_BYTES = 44 * 1024 * 1024

OFF_AQ = 0
OFF_AK = OFF_AQ + ATTN_WIDTH
OFF_AV = OFF_AK + KV_WIDTH
OFF_AG = OFF_AV + KV_WIDTH
OFF_HQ = OFF_AG + ATTN_WIDTH
OFF_HF = OFF_HQ + HGRN_WIDTH
OFF_HI = OFF_HF + HGRN_WIDTH
OFF_HG = OFF_HI + HGRN_WIDTH
OFF_GA = OFF_HG + HGRN_WIDTH

NT_DIMS = (((1,), (1,)), ((), ()))
TN_DIMS = (((0,), (0,)), ((), ()))
NN_DIMS = (((1,), (0,)), ((), ()))


def _pick(n, cands):
    for c in cands:
        if n % c == 0:
            return c
    raise ValueError(f"no tile for {n} in {cands}")


def _params(sem):
    return pltpu.CompilerParams(dimension_semantics=sem, vmem_limit_bytes=VMEM_LIMIT)


def _bdot(a, b, dims=NN_DIMS):
    return lax.dot_general(a.astype(BF16), b.astype(BF16), dims, preferred_element_type=F32)


def _matmul(a, b, mode, out_dtype, name, slabs=1, after=None):
    if mode == "nn":
        (M, K), (K2, N) = a.shape, b.shape
    elif mode == "nt":
        (M, K), (N, K2) = a.shape, b.shape
    else:
        (K, M), (K2, N) = a.shape, b.shape
    assert K == K2
    nslab = N // slabs
    tm = _pick(M, (1024, 512, 256, 128))
    out_bytes = jnp.dtype(out_dtype).itemsize
    choices = []
    for tn in (2688, 1024, 896, 512, 384, 256, 128):
        for tk in (4096, 3584, 2048, 1792, 1536, 1024, 512, 256, 128):
            acc = 0 if tk == K else 4 * tm * tn
            if (nslab % tn == 0 and K % tk == 0 and 4 * tk * (tm + tn) <= MATMUL_OPERAND_BYTES
                    and 4 * tk * (tm + tn) + 2 * out_bytes * tm * tn + acc <= MATMUL_VMEM_BYTES):
                choices.append((K // tk > 1, -tn, tn, tk))
                break
    _, _, tn, tk = min(choices)
    nk = K // tk
    per = nslab // tn
    dims = {"nn": NN_DIMS, "nt": NT_DIMS, "tn": TN_DIMS}[mode]

    n_in = 2 if after is None else 3

    def body(*refs):
        a_ref, b_ref, o_ref, acc = refs[0], refs[1], refs[n_in], refs[n_in + 1:]
        part = lax.dot_general(a_ref[...], b_ref[...], dims, preferred_element_type=F32)
        if nk == 1:
            o_ref[...] = part.astype(o_ref.dtype)
            return
        acc_ref, = acc
        k = pl.program_id(2)

        @pl.when(k == 0)
        def _():
            acc_ref[...] = part

        @pl.when((k > 0) & (k < nk - 1))
        def _():
            acc_ref[...] += part

        @pl.when(k == nk - 1)
        def _():
            o_ref[...] = (acc_ref[...] + part).astype(o_ref.dtype)

    if mode == "tn":
        a_spec = pl.BlockSpec((tk, tm), lambda i, j, k: (k, i))
    else:
        a_spec = pl.BlockSpec((tm, tk), lambda i, j, k: (i, k))
    if mode == "nt":
        b_spec = pl.BlockSpec((tn, tk), lambda i, j, k: (j, k))
    else:
        b_spec = pl.BlockSpec((tk, tn), lambda i, j, k: (k, j))
    if slabs == 1:
        o_shape = jax.ShapeDtypeStruct((M, N), out_dtype)
        o_spec = pl.BlockSpec((tm, tn), lambda i, j, k: (i, j))
    else:
        o_shape = jax.ShapeDtypeStruct((slabs, M, nslab), out_dtype)
        o_spec = pl.BlockSpec((None, tm, tn), lambda i, j, k: (j // per, i, j % per))
    return pl.pallas_call(
        body, name=name, grid=(M // tm, N // tn, nk), in_specs=[a_spec, b_spec] + ([] if after is None else [ANY]),
        out_specs=o_spec, out_shape=o_shape,
        scratch_shapes=[pltpu.VMEM((tm, tn), F32)] if nk > 1 else [],
        compiler_params=_params(("parallel", "parallel", "arbitrary")),
    )(*((a, b) if after is None else (a, b, after)))


def _rmsnorm_fwd(x2, gain):
    T, D = x2.shape
    tr = _pick(T, (256, 128))

    def body(x_ref, g_ref, h_ref, r_ref):
        xv = x_ref[...]
        r = lax.rsqrt(jnp.mean(xv * xv, axis=-1, keepdims=True) + NORM_EPS)
        h_ref[...] = (xv * r * g_ref[...]).astype(BF16)
        r_ref[...] = r

    return pl.pallas_call(
        body, name="rmsnorm_pre_fwd", grid=(T // tr,),
        in_specs=[pl.BlockSpec((tr, D), lambda i: (i, 0)), pl.BlockSpec((1, D), lambda i: (0, 0))],
        out_specs=[pl.BlockSpec((tr, D), lambda i: (i, 0)), pl.BlockSpec((tr, 1), lambda i: (i, 0))],
        out_shape=[jax.ShapeDtypeStruct((T, D), BF16), jax.ShapeDtypeStruct((T, 1), F32)],
        compiler_params=_params(("parallel",)),
    )(x2, gain)


def _rmsnorm_bwd(dh, x2, rstd, gain, dout):
    T, D = x2.shape
    tr = _pick(T, (256, 128))

    def body(dh_ref, x_ref, r_ref, g_ref, do_ref, gx_ref, gg_ref):
        @pl.when(pl.program_id(0) == 0)
        def _():
            gg_ref[...] = jnp.zeros_like(gg_ref)

        n = x_ref[...] * r_ref[...]
        dhv = dh_ref[...]
        dn = dhv * g_ref[...]
        gx_ref[...] = do_ref[...] + r_ref[...] * (dn - n * jnp.mean(dn * n, axis=-1, keepdims=True))
        gg_ref[...] += jnp.sum(dhv * n, axis=0, keepdims=True)

    row = pl.BlockSpec((tr, D), lambda i: (i, 0))
    vec = pl.BlockSpec((1, D), lambda i: (0, 0))
    return pl.pallas_call(
        body, name="rmsnorm_pre_bwd", grid=(T // tr,),
        in_specs=[row, row, pl.BlockSpec((tr, 1), lambda i: (i, 0)), vec, row],
        out_specs=[row, vec],
        out_shape=[jax.ShapeDtypeStruct((T, D), F32), jax.ShapeDtypeStruct((1, D), F32)],
        compiler_params=_params(("arbitrary",)),
    )(dh, x2, rstd, gain, dout)


def _post_loss(yv, x2, tgt2, gain):
    T, D = x2.shape
    tr = _pick(T, (256, 128))

    def body(y_ref, x_ref, t_ref, g_ref, dy_ref, do_ref, loss_ref, gg_ref):
        @pl.when(pl.program_id(0) == 0)
        def _():
            gg_ref[...] = jnp.zeros_like(gg_ref)
            loss_ref[...] = jnp.zeros_like(loss_ref)

        yv_ = y_ref[...]
        r = lax.rsqrt(jnp.mean(yv_ * yv_, axis=-1, keepdims=True) + NORM_EPS)
        n = yv_ * r
        e = (x_ref[...] + n * g_ref[...]) - t_ref[...]
        loss_ref[...] += 0.5 * jnp.sum(jnp.mean(e * e, axis=-1, keepdims=True), axis=0, keepdims=True)
        dz = e / D
        do_ref[...] = dz
        gg_ref[...] += jnp.sum(dz * n, axis=0, keepdims=True)
        dn = dz * g_ref[...]
        dy_ref[...] = (r * (dn - n * jnp.mean(dn * n, axis=-1, keepdims=True))).astype(BF16)

    row = pl.BlockSpec((tr, D), lambda i: (i, 0))
    vec = pl.BlockSpec((1, D), lambda i: (0, 0))
    return pl.pallas_call(
        body, name="post_norm_loss", grid=(T // tr,),
        in_specs=[row, row, row, vec],
        out_specs=[row, row, pl.BlockSpec((1, 1), lambda i: (0, 0)), vec],
        out_shape=[jax.ShapeDtypeStruct((T, D), BF16), jax.ShapeDtypeStruct((T, D), F32),
                   jax.ShapeDtypeStruct((1, 1), F32), jax.ShapeDtypeStruct((1, D), F32)],
        compiler_params=_params(("arbitrary",)),
    )(yv, x2, tgt2, gain)


def _window(rows, cols, at):
    return pl.BlockSpec((pl.Element(rows), pl.Element(cols)), at)


def _gate_windows(tm, tn, D):
    return [_window(tm, tn, lambda i, j: (i * tm, pl.multiple_of(OFF_GA + j * tn, LANES))),
            _window(tm, tn, lambda i, j: (i * tm, pl.multiple_of(OFF_GA + D + j * tn, LANES)))]


def _branch_proj_merge(yh, wbh, proj, ua):
    T, K = yh.shape
    D = wbh.shape[1]
    tm, tn = _pick(T, (512, 256, 128)), _pick(D, (1024, 512, 256))

    def body(a_ref, b_ref, ga_ref, gh_ref, ua_ref, uh_ref, m_ref):
        uh = jnp.dot(a_ref[...], b_ref[...], preferred_element_type=F32).astype(BF16)
        uh_ref[...] = uh
        m_ref[...] = (jax.nn.sigmoid(ga_ref[...]) * ua_ref[...].astype(F32)
                      + jax.nn.sigmoid(gh_ref[...]) * uh.astype(F32)).astype(BF16)

    blk = pl.BlockSpec((tm, tn), lambda i, j: (i, j))
    o = jax.ShapeDtypeStruct((T, D), BF16)
    return pl.pallas_call(
        body, name="branch_hgrn_proj_merge", grid=(T // tm, D // tn),
        in_specs=[pl.BlockSpec((tm, K), lambda i, j: (i, 0)), pl.BlockSpec((K, tn), lambda i, j: (0, j))]
        + _gate_windows(tm, tn, D) + [blk],
        out_specs=[blk, blk], out_shape=[o, o],
        compiler_params=_params(("parallel", "arbitrary")),
    )(yh, wbh, proj, proj, ua)


def _out_dgrad_merge_bwd(dy, wout, proj, ua, uh):
    T, K = dy.shape
    D = wout.shape[0]
    tm, tn = _pick(T, (512, 256, 128)), _pick(D, (1024, 512, 256))
    nj = D // tn

    def body(a_ref, b_ref, ga_ref, gh_ref, ua_ref, uh_ref, dua_ref, duh_ref, dproj_ref):
        d = lax.dot_general(a_ref[...], b_ref[...], NT_DIMS, preferred_element_type=F32)
        sa = jax.nn.sigmoid(ga_ref[...])
        sh = jax.nn.sigmoid(gh_ref[...])
        dua_ref[...] = (d * sa).astype(BF16)
        duh_ref[...] = (d * sh).astype(BF16)
        dga = (d * ua_ref[...].astype(F32) * sa * (1.0 - sa)).astype(BF16)
        dgh = (d * uh_ref[...].astype(F32) * sh * (1.0 - sh)).astype(BF16)
        for jj in range(nj):
            @pl.when(pl.program_id(1) == jj)
            def _():
                dproj_ref[:, jj * tn:(jj + 1) * tn] = dga
                dproj_ref[:, D + jj * tn:D + (jj + 1) * tn] = dgh

    blk = pl.BlockSpec((tm, tn), lambda i, j: (i, j))
    o = jax.ShapeDtypeStruct((T, D), BF16)
    return pl.pallas_call(
        body, name="out_proj_dgrad_merge_bwd", grid=(T // tm, nj),
        in_specs=[pl.BlockSpec((tm, K), lambda i, j: (i, 0)), pl.BlockSpec((tn, K), lambda i, j: (j, 0))]
        + _gate_windows(tm, tn, D) + [blk, blk],
        out_specs=[blk, blk, _window(tm, 2 * D, lambda i, j: (i * tm, OFF_GA))],
        out_shape=[o, o, jax.ShapeDtypeStruct((T, proj.shape[1]), BF16)],
        compiler_params=_params(("parallel", "arbitrary")),
    )(dy, wout, proj, proj, ua, uh)


KV_PAIR = 2
PAIR_HEADS = KV_PAIR * GROUP


def _attn_mask(n):
    qi = lax.broadcasted_iota(jnp.int32, (WINDOW, 2 * WINDOW), 0)
    si = lax.broadcasted_iota(jnp.int32, (WINDOW, 2 * WINDOW), 1)
    dist = qi + WINDOW - si
    return (dist >= 0) & (dist < WINDOW) & ((si >= WINDOW) | (n > 0))


def _first_key_column(shape):
    return lax.broadcasted_iota(jnp.int32, shape, 1) == 0


def _attn_group_fwd(mask, q, k, v, ag, bias, sink):
    mask = jnp.concatenate([mask] * GROUP, axis=0)
    s = lax.dot_general(q.astype(BF16), k.astype(BF16), NT_DIMS, preferred_element_type=F32) * (HEAD_DIM ** -0.5)
    s = jnp.where(mask, s + bias, -1e30)
    m = jnp.maximum(jnp.max(s, axis=-1, keepdims=True), sink)
    p = jnp.exp(s - m)
    e_sink = jnp.exp(sink - m)
    rden = 1.0 / (jnp.sum(p, axis=-1, keepdims=True) + e_sink)
    probs = (p * rden).astype(BF16)
    o = jnp.dot(probs, v.astype(BF16), preferred_element_type=F32)
    kept = jnp.where(_first_key_column(probs.shape), (e_sink * rden).astype(BF16), probs)
    return o * jax.nn.silu(ag), kept


def _attn_group_bwd(q, k, v, ag, kept, dout):
    first = _first_key_column(kept.shape)
    p_sink = kept[:, 0:1].astype(F32)
    probs = jnp.where(first, jnp.zeros_like(kept), kept)
    qb, kb, vb = q.astype(BF16), k.astype(BF16), v.astype(BF16)
    o = jnp.dot(probs, vb, preferred_element_type=F32)
    sg = jax.nn.sigmoid(ag)
    d_o = dout * (ag * sg)
    dag = dout * o * (sg * (1.0 + ag * (1.0 - sg)))
    d_row = jnp.sum(d_o * o, axis=-1, keepdims=True)
    d_ob = d_o.astype(BF16)
    ds = probs.astype(F32) * (lax.dot_general(d_ob, vb, NT_DIMS, preferred_element_type=F32) - d_row)
    dsb = ds.astype(BF16)
    dq = jnp.dot(dsb, kb, preferred_element_type=F32) * (HEAD_DIM ** -0.5)
    dk = lax.dot_general(dsb, qb, TN_DIMS, preferred_element_type=F32) * (HEAD_DIM ** -0.5)
    dv = lax.dot_general(probs, d_ob, TN_DIMS, preferred_element_type=F32)
    return dq, dk, dv, dag, ds, -(p_sink * d_row)


def _attn_specs(B_loc, nb, order):
    qw = PAIR_HEADS * HEAD_DIM
    kw = KV_PAIR * HEAD_DIM

    def rows(g):
        b, p, n = order(*g)
        return b * nb + n

    def prev(g):
        b, p, n = order(*g)
        return b * nb + jnp.maximum(n - 1, 0)

    def pp(g):
        return order(*g)[1]

    q = pl.BlockSpec((WINDOW, qw), lambda *g: (rows(g), OFF_AQ // qw + pp(g)))
    kc = pl.BlockSpec((WINDOW, kw), lambda *g: (rows(g), OFF_AK // kw + pp(g)))
    kp = pl.BlockSpec((WINDOW, kw), lambda *g: (prev(g), OFF_AK // kw + pp(g)))
    vc = pl.BlockSpec((WINDOW, kw), lambda *g: (rows(g), OFF_AV // kw + pp(g)))
    vp = pl.BlockSpec((WINDOW, kw), lambda *g: (prev(g), OFF_AV // kw + pp(g)))
    ag = pl.BlockSpec((WINDOW, qw), lambda *g: (rows(g), OFF_AG // qw + pp(g)))
    bias = pl.BlockSpec((PAIR_HEADS, WINDOW, 2 * WINDOW), lambda *g: (pp(g), 0, 0))
    sink = pl.BlockSpec((PAIR_HEADS, 8, LANES), lambda *g: (pp(g), 0, 0))
    return [q, kc, kp, vc, vp, ag, bias, sink], rows, pp


def _attn_operands(q_ref, kc_ref, kp_ref, vc_ref, vp_ref, ag_ref, bias_ref, sink_ref, j):
    lo, hi = j * HEAD_DIM, (j + 1) * HEAD_DIM
    k = jnp.concatenate([kp_ref[:, lo:hi], kc_ref[:, lo:hi]], axis=0)
    v = jnp.concatenate([vp_ref[:, lo:hi], vc_ref[:, lo:hi]], axis=0)
    heads = [j * GROUP + g for g in range(GROUP)]
    q = jnp.concatenate([q_ref[:, h * HEAD_DIM:(h + 1) * HEAD_DIM] for h in heads], axis=0)
    ag = jnp.concatenate([ag_ref[:, h * HEAD_DIM:(h + 1) * HEAD_DIM] for h in heads], axis=0)
    bias = jnp.concatenate([bias_ref[h] for h in heads], axis=0)
    sink = jnp.concatenate([jnp.broadcast_to(sink_ref[h, 0:1, 0:1], (WINDOW, 1)) for h in heads], axis=0)
    return q, k, v, ag, bias, sink


def _attn_fwd(proj, bias_tab, sinks_b, B_loc, S):
    T = B_loc * S
    nb = S // WINDOW
    n_pairs = ATTN_KV_HEADS // KV_PAIR
    in_specs, rows, pp = _attn_specs(B_loc, nb, lambda b, p, n: (b, p, n))

    def body(q_ref, kc_ref, kp_ref, vc_ref, vp_ref, ag_ref, bias_ref, sink_ref, ya_ref, p_ref):
        mask = _attn_mask(pl.program_id(2))
        for j in range(KV_PAIR):
            out, kept = _attn_group_fwd(
                mask, *_attn_operands(q_ref, kc_ref, kp_ref, vc_ref, vp_ref, ag_ref, bias_ref, sink_ref, j))
            for g in range(GROUP):
                h = j * GROUP + g
                blk = slice(g * WINDOW, (g + 1) * WINDOW)
                ya_ref[:, h * HEAD_DIM:(h + 1) * HEAD_DIM] = out[blk].astype(BF16)
                p_ref[:, h * 2 * WINDOW:(h + 1) * 2 * WINDOW] = kept[blk]

    return pl.pallas_call(
        body, name="attn_fwd", grid=(B_loc, n_pairs, nb), in_specs=in_specs,
        out_specs=[pl.BlockSpec((WINDOW, PAIR_HEADS * HEAD_DIM), lambda *g: (rows(g), pp(g))),
                   pl.BlockSpec((WINDOW, PAIR_HEADS * 2 * WINDOW), lambda *g: (rows(g), pp(g)))],
        out_shape=[jax.ShapeDtypeStruct((T, ATTN_WIDTH), BF16),
                   jax.ShapeDtypeStruct((T, ATTN_HEADS * 2 * WINDOW), BF16)],
        compiler_params=_params(("parallel", "parallel", "parallel")),
    )(proj, proj, proj, proj, proj, proj, bias_tab, sinks_b)


def _attn_bwd(proj, probs, dya, dproj, B_loc, S):
    nb = S // WINDOW
    n_pairs = ATTN_KV_HEADS // KV_PAIR
    qw, kw = PAIR_HEADS * HEAD_DIM, KV_PAIR * HEAD_DIM

    def at(off, width, back=0):
        return lambda b, i, p: ((b * nb + jnp.maximum(nb - 1 - i - back, 0)) * WINDOW,
                                pl.multiple_of(off + p * width, LANES))

    in_specs = [_window(WINDOW, qw, at(OFF_AQ, qw)), _window(WINDOW, kw, at(OFF_AK, kw)),
                _window(WINDOW, kw, at(OFF_AK, kw, 1)), _window(WINDOW, kw, at(OFF_AV, kw)),
                _window(WINDOW, kw, at(OFF_AV, kw, 1)), _window(WINDOW, qw, at(OFF_AG, qw)),
                pl.BlockSpec((WINDOW, PAIR_HEADS * 2 * WINDOW), lambda b, i, p: (b * nb + nb - 1 - i, p)),
                pl.BlockSpec((WINDOW, qw), lambda b, i, p: (b * nb + nb - 1 - i, p)), ANY]

    def body(q_ref, kc_ref, kp_ref, vc_ref, vp_ref, ag_ref, p_ref, dya_ref, dproj_in,
             dproj_ref, dbias_ref, dsink_ref, dkc_ref, dvc_ref):
        b, i, p = pl.program_id(0), pl.program_id(1), pl.program_id(2)

        @pl.when((b == 0) & (i == 0) & (p == 0))
        def _():
            dbias_ref[...] = jnp.zeros_like(dbias_ref)
            dsink_ref[...] = jnp.zeros_like(dsink_ref)

        @pl.when(i == 0)
        def _():
            dkc_ref[p] = jnp.zeros((WINDOW, kw), F32)
            dvc_ref[p] = jnp.zeros((WINDOW, kw), F32)

        dk_carry, dv_carry = dkc_ref[p], dvc_ref[p]
        dqs, dags, dbiases, dsinks, dks, dvs = [], [], [], [], [], []
        for j in range(KV_PAIR):
            heads = [j * GROUP + g for g in range(GROUP)]
            lo, hi = j * HEAD_DIM, (j + 1) * HEAD_DIM
            stack = lambda parts: jnp.concatenate(parts, axis=0)
            k = stack([kp_ref[:, lo:hi], kc_ref[:, lo:hi]])
            v = stack([vp_ref[:, lo:hi], vc_ref[:, lo:hi]])
            q = stack([q_ref[:, h * HEAD_DIM:(h + 1) * HEAD_DIM] for h in heads])
            ag = stack([ag_ref[:, h * HEAD_DIM:(h + 1) * HEAD_DIM] for h in heads])
            dout = stack([dya_ref[:, h * HEAD_DIM:(h + 1) * HEAD_DIM].astype(F32) for h in heads])
            kept = stack([p_ref[:, h * 2 * WINDOW:(h + 1) * 2 * WINDOW] for h in heads])
            dq, dk, dv, dag, dbias, dsink = _attn_group_bwd(q, k, v, ag, kept, dout)
            lo, hi = j * HEAD_DIM, (j + 1) * HEAD_DIM
            dks.append((dk[WINDOW:] + dk_carry[:, lo:hi]).astype(BF16))
            dvs.append((dv[WINDOW:] + dv_carry[:, lo:hi]).astype(BF16))
            dkc_ref[p, :, lo:hi] = dk[:WINDOW]
            dvc_ref[p, :, lo:hi] = dv[:WINDOW]
            for g in range(GROUP):
                blk = slice(g * WINDOW, (g + 1) * WINDOW)
                dqs.append(dq[blk].astype(BF16))
                dags.append(dag[blk].astype(BF16))
                dbiases.append(dbias[blk])
                dsinks.append(jnp.broadcast_to(jnp.sum(dsink[blk], axis=0, keepdims=True), (8, LANES)))

        for pair in range(n_pairs):
            @pl.when(p == pair)
            def _():
                for j in range(KV_PAIR):
                    col = (pair * KV_PAIR + j) * HEAD_DIM
                    dproj_ref[:, OFF_AK + col:OFF_AK + col + HEAD_DIM] = dks[j]
                    dproj_ref[:, OFF_AV + col:OFF_AV + col + HEAD_DIM] = dvs[j]
                for hh in range(PAIR_HEADS):
                    h = pair * PAIR_HEADS + hh
                    dproj_ref[:, OFF_AQ + h * HEAD_DIM:OFF_AQ + (h + 1) * HEAD_DIM] = dqs[hh]
                    dproj_ref[:, OFF_AG + h * HEAD_DIM:OFF_AG + (h + 1) * HEAD_DIM] = dags[hh]
                    dbias_ref[h] += dbiases[hh]
                    dsink_ref[h] += dsinks[hh]

    return pl.pallas_call(
        body, name="attn_bwd", grid=(B_loc, nb, n_pairs), in_specs=in_specs,
        out_specs=[_window(WINDOW, OFF_HQ, lambda b, i, p: ((b * nb + nb - 1 - i) * WINDOW, 0)),
                   pl.BlockSpec((ATTN_HEADS, WINDOW, 2 * WINDOW), lambda b, i, p: (0, 0, 0)),
                   pl.BlockSpec((ATTN_HEADS, 8, LANES), lambda b, i, p: (0, 0, 0))],
        out_shape=[jax.ShapeDtypeStruct(dproj.shape, dproj.dtype),
                   jax.ShapeDtypeStruct((ATTN_HEADS, WINDOW, 2 * WINDOW), F32),
                   jax.ShapeDtypeStruct((ATTN_HEADS, 8, LANES), F32)],
        input_output_aliases={8: 0},
        scratch_shapes=[pltpu.VMEM((n_pairs, WINDOW, kw), F32), pltpu.VMEM((n_pairs, WINDOW, kw), F32)],
        compiler_params=_params(("arbitrary", "arbitrary", "arbitrary")),
    )(proj, proj, proj, proj, proj, proj, probs, dya, dproj)


class _HgrnPre:
    def __init__(self, fr, qr, lb, g_scr):
        t = lax.broadcasted_iota(jnp.int32, (CHUNK, CHUNK), 0)
        s = lax.broadcasted_iota(jnp.int32, (CHUNK, CHUNK), 1)
        self.sg = jax.nn.sigmoid(fr)
        self.f = lb + (1.0 - lb) * self.sg
        g = jnp.dot((t >= s).astype(F32), jnp.log(self.f), precision=lax.Precision.HIGHEST, preferred_element_type=F32)
        g_scr[...] = g
        self.g = g
        self.row = lax.broadcasted_iota(jnp.int32, g.shape, 0)
        self.refs = [jnp.zeros((1, g.shape[1]), F32)] + [g_scr[pl.ds(i * SUB - 1, 1), :] for i in range(1, NSUB)]
        self.gend = g_scr[pl.ds(CHUNK - 1, 1), :]
        refrow = jnp.zeros_like(g)
        for i in range(1, NSUB):
            refrow = jnp.where(self.row >= i * SUB, self.refs[i], refrow)
        self.sigq = jax.nn.sigmoid(qr)
        self.qs = qr * self.sigq
        self.k = 1.0 - self.f
        self.eg = jnp.exp(g)
        self.eqd = jnp.exp(g - refrow)
        self.ekd = [jnp.exp(jnp.where(self.row < (i + 1) * SUB, self.refs[i] - g, 0.0)) for i in range(NSUB)]
        self.ekdec = jnp.exp(self.gend - g)
        self.qg = self.qs * self.eg
        self.qd = self.qs * self.eqd
        self.kd = [self.k * e for e in self.ekd]
        self.kdec = self.k * self.ekdec
        self.egend = jnp.exp(self.gend)


def _hgrn_pair_mask():
    t = lax.broadcasted_iota(jnp.int32, (CHUNK, NSUB * CHUNK), 0)
    col = lax.broadcasted_iota(jnp.int32, (CHUNK, NSUB * CHUNK), 1)
    return ((t // SUB) == (col // CHUNK)) & ((col % CHUNK) <= t)


def _hgrn_head_out(p, lanes, state_t, v, mask):
    qg, qd = p.qg[:, lanes], p.qd[:, lanes]
    kall = jnp.concatenate([kd[:, lanes] for kd in p.kd], axis=0)
    vst = jnp.concatenate([v] * NSUB, axis=0)
    am = jnp.where(mask, _bdot(qd, kall, NT_DIMS), 0.0)
    o = _bdot(qg, state_t, NT_DIMS) + _bdot(am, vst)
    return o, (qg, qd, kall, am, vst)


def _hgrn_fwd(proj, lb, gain, B_loc, S):
    T = B_loc * S
    nc = S // CHUNK
    nh = HGRN_HEADS

    def at(off):
        return lambda b, n: ((b * nc + n) * CHUNK, off)

    vec = pl.BlockSpec((1, HGRN_WIDTH), lambda b, n: (0, 0))

    def body(q_ref, f_ref, v_ref, hg_ref, lb_ref, gain_ref, yh_ref, st_ref, state_scr, g_scr):
        @pl.when(pl.program_id(1) == 0)
        def _():
            state_scr[...] = jnp.zeros_like(state_scr)

        p = _HgrnPre(f_ref[...], q_ref[...], lb_ref[...], g_scr)
        v = v_ref[...]
        gate = gain_ref[...] * jax.nn.silu(hg_ref[...])
        mask = _hgrn_pair_mask()
        for hd in range(nh):
            lanes = slice(hd * HGRN_DIM, (hd + 1) * HGRN_DIM)
            st = state_scr[hd]
            st_ref[hd] = st
            o, _ = _hgrn_head_out(p, lanes, st, v[:, lanes], mask)
            rs = lax.rsqrt(jnp.mean(o * o, axis=-1, keepdims=True) + NORM_EPS)
            yh_ref[:, lanes] = (o * rs * gate[:, lanes]).astype(BF16)
            state_scr[hd] = st * p.egend[:, lanes] + _bdot(v[:, lanes], p.kdec[:, lanes], TN_DIMS)

    return pl.pallas_call(
        body, name="hgrn_fwd", grid=(B_loc, nc),
        in_specs=[_window(CHUNK, HGRN_WIDTH, at(OFF_HQ)), _window(CHUNK, HGRN_WIDTH, at(OFF_HF)),
                  _window(CHUNK, HGRN_WIDTH, at(OFF_HI)), _window(CHUNK, HGRN_WIDTH, at(OFF_HG)), vec, vec],
        out_specs=[pl.BlockSpec((CHUNK, HGRN_WIDTH), lambda b, n: (b * nc + n, 0)),
                   pl.BlockSpec((None, None, nh, HGRN_DIM, HGRN_DIM), lambda b, n: (b, n, 0, 0, 0))],
        out_shape=[jax.ShapeDtypeStruct((T, HGRN_WIDTH), BF16),
                   jax.ShapeDtypeStruct((B_loc, nc, nh, HGRN_DIM, HGRN_DIM), F32)],
        scratch_shapes=[pltpu.VMEM((nh, HGRN_DIM, HGRN_DIM), F32), pltpu.VMEM((CHUNK, HGRN_WIDTH), F32)],
        compiler_params=_params(("parallel", "arbitrary")),
    )(proj, proj, proj, proj, lb, gain)


def _hgrn_bwd(proj, states, lb, gain, dyh, dproj, B_loc, S):
    nc = S // CHUNK
    nh = HGRN_HEADS

    def at(off):
        return lambda b, i: ((b * nc + nc - 1 - i) * CHUNK, off)

    vec = pl.BlockSpec((1, HGRN_WIDTH), lambda b, i: (0, 0))
    in_specs = [_window(CHUNK, HGRN_WIDTH, at(OFF_HQ)), _window(CHUNK, HGRN_WIDTH, at(OFF_HF)),
                _window(CHUNK, HGRN_WIDTH, at(OFF_HI)), _window(CHUNK, HGRN_WIDTH, at(OFF_HG)), vec, vec,
                pl.BlockSpec((None, None, nh, HGRN_DIM, HGRN_DIM), lambda b, i: (b, nc - 1 - i, 0, 0, 0)),
                pl.BlockSpec((CHUNK, HGRN_WIDTH), lambda b, i: (b * nc + nc - 1 - i, 0)), ANY]
    acc_spec = pl.BlockSpec((None, 1, HGRN_WIDTH), lambda b, i: (b, 0, 0))

    def body(q_ref, f_ref, v_ref, hg_ref, lb_ref, gain_ref, st_ref, dyh_ref, dproj_in,
             dproj_ref, dlb_ref, dgain_ref, dstate_scr, g_scr, dg_scr):
        dq_ref, df_ref, dv_ref, dhg_ref = [dproj_ref.at[:, pl.ds(k * HGRN_WIDTH, HGRN_WIDTH)] for k in range(4)]

        @pl.when(pl.program_id(1) == 0)
        def _():
            dstate_scr[...] = jnp.zeros_like(dstate_scr)
            dlb_ref[...] = jnp.zeros_like(dlb_ref)
            dgain_ref[...] = jnp.zeros_like(dgain_ref)

        qr, lb, gain, hg, v = q_ref[...], lb_ref[...], gain_ref[...], hg_ref[...], v_ref[...]
        p = _HgrnPre(f_ref[...], qr, lb, g_scr)
        sgh = jax.nn.sigmoid(hg)
        sil = hg * sgh
        dy = dyh_ref[...].astype(F32)
        mask = _hgrn_pair_mask()
        dqg, dqd, dkdec, dv, dhg, dgend, dgain = [], [], [], [], [], [], []
        dkd = [[] for _ in range(NSUB)]
        heads = [slice(hd * HGRN_DIM, (hd + 1) * HGRN_DIM) for hd in range(nh)]
        sts = [st_ref[hd] for hd in range(nh)]
        dnews = [dstate_scr[hd] for hd in range(nh)]
        fwd = [_hgrn_head_out(p, lanes, st, v[:, lanes], mask) for lanes, st in zip(heads, sts)]
        for lanes, st, dnew in zip(heads, sts, dnews):
            dkdec_h = _bdot(v[:, lanes], dnew)
            dkdec.append(dkdec_h)
            dgend.append(jnp.sum(dkdec_h * p.kdec[:, lanes], axis=0, keepdims=True)
                         + jnp.sum(dnew * st, axis=0, keepdims=True) * p.egend[:, lanes])
        dos = []
        gate_grad = sgh * (1.0 + hg * (1.0 - sgh))
        for lanes, (o, _) in zip(heads, fwd):
            rs = lax.rsqrt(jnp.mean(o * o, axis=-1, keepdims=True) + NORM_EPS)
            n = o * rs
            dyn = dy[:, lanes] * n
            dgain.append(jnp.sum(dyn * sil[:, lanes], axis=0, keepdims=True))
            dhg.append(dyn * gain[:, lanes] * gate_grad[:, lanes])
            dn = dy[:, lanes] * gain[:, lanes] * sil[:, lanes]
            dos.append(rs * (dn - n * jnp.mean(dn * n, axis=-1, keepdims=True)))
        drs = []
        for hd, (lanes, st, dnew, do, (_, (qg, qd, kall, am, vst))) in enumerate(zip(heads, sts, dnews, dos, fwd)):
            dqg.append(_bdot(do, st))
            dstate_scr[hd] = _bdot(do, qg, TN_DIMS) + dnew * p.egend[:, lanes]
            drs.append(jnp.where(mask, _bdot(do, vst, NT_DIMS), 0.0))
            dvst = _bdot(am, do, TN_DIMS)
            dv.append(sum(dvst[i * CHUNK:(i + 1) * CHUNK] for i in range(NSUB)) + _bdot(p.kdec[:, lanes], dnew, NT_DIMS))
        for dr, (_, (qg, qd, kall, am, vst)) in zip(drs, fwd):
            dqd.append(_bdot(dr, kall))
            dkall = _bdot(dr, qd, TN_DIMS)
            for i in range(NSUB):
                dkd[i].append(dkall[i * CHUNK:(i + 1) * CHUNK])

        wide = lambda parts: jnp.concatenate(parts, axis=1)
        dqg, dqd, dkdec = wide(dqg), wide(dqd), wide(dkdec)
        t2 = dqd * p.qd
        dg = dqg * p.qg + t2 - dkdec * p.kdec
        dk = dkdec * p.ekdec
        dg_scr[...] = jnp.zeros_like(dg_scr)
        for i in range(NSUB):
            dkd_i = wide(dkd[i])
            tk = jnp.where(p.row < (i + 1) * SUB, dkd_i * p.kd[i], 0.0)
            dg = dg - tk
            dk = dk + dkd_i * p.ekd[i]
            if i >= 1:
                in_blk = (p.row >= i * SUB) & (p.row < (i + 1) * SUB)
                dg_scr[pl.ds(i * SUB - 1, 1), :] = (jnp.sum(tk, axis=0, keepdims=True)
                                                    - jnp.sum(jnp.where(in_blk, t2, 0.0), axis=0, keepdims=True))
        dg_scr[pl.ds(CHUNK - 1, 1), :] = wide(dgend)
        t = lax.broadcasted_iota(jnp.int32, (CHUNK, CHUNK), 0)
        s = lax.broadcasted_iota(jnp.int32, (CHUNK, CHUNK), 1)
        dlogf = jnp.dot((t <= s).astype(F32), dg + dg_scr[...], precision=lax.Precision.HIGHEST, preferred_element_type=F32)
        df = dlogf / p.f - dk
        df_ref[...] = (df * (1.0 - lb) * p.sg * (1.0 - p.sg)).astype(BF16)
        dlb_ref[...] += jnp.sum(df * (1.0 - p.sg), axis=0, keepdims=True)
        dq_ref[...] = ((dqg * p.eg + dqd * p.eqd) * p.sigq * (1.0 + qr * (1.0 - p.sigq))).astype(BF16)
        dv_ref[...] = wide(dv).astype(BF16)
        dhg_ref[...] = wide(dhg).astype(BF16)
        dgain_ref[...] += wide(dgain)

    acc = jax.ShapeDtypeStruct((B_loc, 1, HGRN_WIDTH), F32)
    return pl.pallas_call(
        body, name="hgrn_bwd", grid=(B_loc, nc), in_specs=in_specs,
        out_specs=[_window(CHUNK, 4 * HGRN_WIDTH, at(OFF_HQ)), acc_spec, acc_spec],
        out_shape=[jax.ShapeDtypeStruct(dproj.shape, dproj.dtype), acc, acc],
        input_output_aliases={8: 0},
        scratch_shapes=[pltpu.VMEM((nh, HGRN_DIM, HGRN_DIM), F32), pltpu.VMEM((CHUNK, HGRN_WIDTH), F32),
                        pltpu.VMEM((CHUNK, HGRN_WIDTH), F32)],
        compiler_params=_params(("parallel", "arbitrary")),
    )(proj, proj, proj, proj, lb, gain, states, dyh, dproj)


def _adamw(w, g, m, v, name):
    R, C = w.shape
    tr = _pick(R, (128, 64, 32, 16, 8)) if C > 1024 else _pick(R, (512, 256, 128, 64, 32, 16, 8))

    def body(w_ref, g_ref, m_ref, v_ref, d_ref, nm_ref, nv_ref, g_out_ref):
        gv = g_ref[...]
        g_out_ref[...] = gv
        nm = ADAM_B1 * m_ref[...] + (1.0 - ADAM_B1) * gv
        nv = ADAM_B2 * v_ref[...] + (1.0 - ADAM_B2) * (gv * gv)
        m_hat = nm / (1.0 - ADAM_B1 ** ADAM_STEP)
        v_hat = nv / (1.0 - ADAM_B2 ** ADAM_STEP)
        d_ref[...] = -ADAM_LR * (m_hat / (jnp.sqrt(v_hat) + ADAM_EPS) + ADAM_WD * w_ref[...])
        nm_ref[...] = nm
        nv_ref[...] = nv

    blk = pl.BlockSpec((tr, C), lambda i: (i, 0))
    o = jax.ShapeDtypeStruct((R, C), F32)
    return pl.pallas_call(
        body, name=name, grid=(R // tr,), in_specs=[blk] * 4, out_specs=[blk] * 4, out_shape=[o, o, o, o],
        compiler_params=_params(("parallel",)),
    )(w, g, m, v)


ANY = pl.BlockSpec(memory_space=pl.ANY)
VMEM_SPEC = pl.BlockSpec(memory_space=pltpu.VMEM)


def _place():
    x, y, c = lax.axis_index("x"), lax.axis_index("y"), lax.axis_index("c")
    other_chips = [(1 - x, y), (x, 1 - y), (1 - x, 1 - y)]
    return x, y, c, other_chips


def _cast_into_full(w, ax, s_arr, name):
    R, C = w.shape
    tr = _pick(R, (256, 128))
    nr = R // tr

    def body(s_ref, w_ref, o_ref):
        o_ref[...] = w_ref[...].astype(BF16)

    if ax == 1:
        shape, o_map = (R, N_CHIPS * C), lambda i, s: (i, s[0])
    else:
        shape, o_map = (N_CHIPS * R, C), lambda i, s: (s[0] * nr + i, 0)
    return pl.pallas_call(
        body, name=name,
        grid_spec=pltpu.PrefetchScalarGridSpec(
            num_scalar_prefetch=1, grid=(nr,), in_specs=[pl.BlockSpec((tr, C), lambda i, s: (i, 0))],
            out_specs=pl.BlockSpec((tr, C), o_map)),
        out_shape=jax.ShapeDtypeStruct(shape, BF16),
        compiler_params=_params(("parallel",)),
    )(s_arr, w)


class _Gather:
    def __init__(self, fulls, shard_shapes, axes, tag):
        self.shapes, self.axes, self.tag, self.nw = shard_shapes, axes, tag, len(fulls)
        self.fulls, self.sems, self.token = list(fulls), {}, None

    def start(self, peers, after=None):
        nw, np_ = self.nw, len(peers)

        def body(*refs):
            ins, sems = refs[:nw], refs[nw + (after is not None):nw + (after is not None) + 2 * np_]
            for k, j in enumerate(peers):
                for cp in self._peer_copies(ins, sems[2 * k], sems[2 * k + 1], j):
                    cp.start()
            refs[-1][...] = jnp.zeros_like(refs[-1])

        out = pl.pallas_call(
            body, name="gather_start_%s_%s" % (self.tag, "".join(map(str, peers))),
            out_shape=(*[pltpu.SemaphoreType.DMA((nw,))] * (2 * np_),
                       *[pltpu.HBM(f.shape, f.dtype) for f in self.fulls], jax.ShapeDtypeStruct((8, LANES), F32)),
            in_specs=[HBM_SPEC] * nw + ([] if after is None else [ANY]),
            out_specs=(*[SEM_SPEC] * (2 * np_), *[HBM_SPEC] * nw, VMEM_SPEC),
            input_output_aliases={k: 2 * np_ + k for k in range(nw)},
            compiler_params=pltpu.CompilerParams(has_side_effects=DATAFLOW),
        )(*[pltpu.with_memory_space_constraint(f, pltpu.HBM) for f in self.fulls], *(() if after is None else (after,)))
        for k, j in enumerate(peers):
            self.sems[j] = (out[2 * k], out[2 * k + 1])
        self.fulls = list(out[2 * np_:2 * np_ + nw])
        self.token = out[-1]

    def _region(self, ref, i, t, half):
        R, C = self.shapes[i]
        hr = R // 2
        if self.axes[i] == 1:
            return ref.at[pl.ds(half * hr, hr), pl.ds(pl.multiple_of(t * C, LANES), C)]
        return ref.at[pl.ds(t * R + half * hr, hr), :]

    def _peer_copies(self, refs, send_sems, recv_sems, j):
        x, y, c, chips = _place()
        s = 2 * x + y
        return [pltpu.make_async_remote_copy(
            src_ref=self._region(refs[i], i, s, c), dst_ref=self._region(refs[i], i, s, c), send_sem=send_sems.at[i],
            recv_sem=recv_sems.at[i], device_id=(*chips[j], c), device_id_type=MESH) for i in range(self.nw)]

    def wait(self, peers, after):
        nw, np_ = self.nw, len(peers)

        def body(*refs):
            ins, sems = refs[:nw], refs[nw:nw + 2 * np_]
            for k, j in enumerate(peers):
                for cp in self._peer_copies(ins, sems[2 * k], sems[2 * k + 1], j):
                    cp.wait_send()
                    cp.wait_recv()

        sem_args = [s for j in peers for s in self.sems[j]]
        out = pl.pallas_call(
            body, name="gather_wait_%s_%s" % (self.tag, "".join(map(str, peers))),
            out_shape=tuple(pltpu.HBM(f.shape, f.dtype) for f in self.fulls),
            in_specs=[HBM_SPEC] * nw + [SEM_SPEC] * (2 * np_) + [ANY], out_specs=tuple([HBM_SPEC] * nw),
            input_output_aliases={k: k for k in range(nw)},
            compiler_params=pltpu.CompilerParams(has_side_effects=DATAFLOW),
        )(*self.fulls, *sem_args, after)
        self.fulls = list(out)

    def forward(self, peers):
        nw, np_ = self.nw, len(peers)

        def body(*refs):
            ins, outs = refs[:nw], refs[nw:2 * nw]
            send_sems, recv_sems = refs[2 * nw:]
            x, y, c, chips = _place()
            cps = []
            for i in range(nw):
                for k, j in enumerate(peers):
                    t = 2 * chips[j][0] + chips[j][1]
                    cp = pltpu.make_async_remote_copy(
                        src_ref=self._region(ins[i], i, t, c), dst_ref=self._region(outs[i], i, t, c),
                        send_sem=send_sems.at[i * np_ + k], recv_sem=recv_sems.at[i * np_ + k],
                        device_id=(x, y, 1 - c), device_id_type=MESH)
                    cp.start()
                    cps.append(cp)
            for cp in cps:
                cp.wait()

        out = pl.pallas_call(
            body, name="gather_forward_%s_%s" % (self.tag, "".join(map(str, peers))),
            in_specs=[ANY] * nw, out_specs=[ANY] * nw,
            out_shape=[jax.ShapeDtypeStruct(f.shape, f.dtype) for f in self.fulls],
            input_output_aliases={i: i for i in range(nw)},
            scratch_shapes=[pltpu.SemaphoreType.DMA((nw * np_,)), pltpu.SemaphoreType.DMA((nw * np_,))],
        )(*self.fulls)
        self.fulls = list(out)


def _matmul_slab(a, wfull, slab_arr, prev, after, name):
    M, K = a.shape
    N = wfull.shape[1]
    nslab = N // N_CHIPS
    tn = _pick(nslab, (2688, 896, 512, 384, 256, 128))
    tm = _pick(M, (512, 256, 128) if tn > 1024 else (1024, 512, 256, 128))
    per = nslab // tn
    extra = [e for e in (prev, after) if e is not None]

    def body(slab_ref, a_ref, b_ref, *rest):
        rest[len(extra)][...] = jnp.dot(a_ref[...], b_ref[...], preferred_element_type=F32)

    return pl.pallas_call(
        body, name=name,
        grid_spec=pltpu.PrefetchScalarGridSpec(
            num_scalar_prefetch=1, grid=(M // tm, per),
            in_specs=[pl.BlockSpec((tm, K), lambda i, j, sl: (i, 0)),
                      pl.BlockSpec((K, tn), lambda i, j, sl: (0, sl[0] * per + j))] + [ANY] * len(extra),
            out_specs=pl.BlockSpec((tm, tn), lambda i, j, sl: (i, sl[0] * per + j))),
        out_shape=jax.ShapeDtypeStruct((M, N), F32),
        input_output_aliases={} if prev is None else {3: 0},
        compiler_params=_params(("parallel", "arbitrary")),
    )(slab_arr, a, wfull, *extra)


def _exchange_sibling_halves(gs, name):
    nw = len(gs)

    def body(*refs):
        ins, outs = refs[:nw], refs[nw:2 * nw]
        send_sems, recv_sems = refs[2 * nw:]
        x, y, c, _ = _place()
        cps = []
        for i in range(nw):
            cp = pltpu.make_async_remote_copy(src_ref=ins[i].at[:, 1 - c], dst_ref=outs[i], send_sem=send_sems.at[i],
                                              recv_sem=recv_sems.at[i], device_id=(x, y, 1 - c), device_id_type=MESH)
            cp.start()
            cps.append(cp)
        for cp in cps:
            cp.wait()

    return pl.pallas_call(
        body, name=name, in_specs=[ANY] * nw, out_specs=[ANY] * nw,
        out_shape=[jax.ShapeDtypeStruct((g.shape[0],) + g.shape[2:], g.dtype) for g in gs],
        scratch_shapes=[pltpu.SemaphoreType.DMA((nw,)), pltpu.SemaphoreType.DMA((nw,))],
    )(*gs)


HBM_SPEC = pl.BlockSpec(memory_space=pltpu.HBM)
SEM_SPEC = pl.BlockSpec(memory_space=pltpu.SEMAPHORE)
DATAFLOW = pltpu.SideEffectType.DATAFLOW_SIDE_EFFECTING


def _chip_copies(ins, lands, send_sems, recv_sems):
    x, y, c, chips = _place()
    return [pltpu.make_async_remote_copy(
        src_ref=ins[i].at[2 * chip[0] + chip[1]], dst_ref=lands[i].at[j], send_sem=send_sems.at[i * 3 + j],
        recv_sem=recv_sems.at[i * 3 + j], device_id=(*chip, c), device_id_type=MESH)
        for i in range(len(ins)) for j, chip in enumerate(chips)]


def _chips_send_start(ss, name):
    nw = len(ss)
    lands = [pltpu.with_memory_space_constraint(lax.empty((N_CHIPS - 1,) + s.shape[1:], s.dtype), pltpu.HBM) for s in ss]

    def body(*refs):
        ins, land_refs = refs[:nw], refs[nw:2 * nw]
        send_sems, recv_sems = refs[2 * nw], refs[2 * nw + 1]
        token = refs[-1]
        for cp in _chip_copies(ins, land_refs, send_sems, recv_sems):
            cp.start()
        token[...] = jnp.zeros_like(token)

    n = 3 * nw
    out = pl.pallas_call(
        body, name=name,
        out_shape=(pltpu.SemaphoreType.DMA((n,)), pltpu.SemaphoreType.DMA((n,)),
                   *[pltpu.HBM(s.shape, s.dtype) for s in ss], *[pltpu.HBM(l.shape, l.dtype) for l in lands],
                   jax.ShapeDtypeStruct((8, LANES), F32)),
        in_specs=[HBM_SPEC] * (2 * nw), out_specs=(SEM_SPEC, SEM_SPEC, *[HBM_SPEC] * (2 * nw), VMEM_SPEC),
        input_output_aliases={k: 2 + k for k in range(2 * nw)},
        compiler_params=pltpu.CompilerParams(has_side_effects=DATAFLOW),
    )(*[pltpu.with_memory_space_constraint(s, pltpu.HBM) for s in ss], *lands)
    return out[0], out[1], list(out[2:2 + nw]), list(out[2 + nw:2 + 2 * nw]), out[-1]


def _chips_send_wait(send_sems, recv_sems, ss, lands, after, name):
    nw = len(ss)

    def body(*refs):
        ins, land_refs = refs[:nw], refs[nw:2 * nw]
        s_sems, r_sems = refs[2 * nw], refs[2 * nw + 1]
        for cp in _chip_copies(ins, land_refs, s_sems, r_sems):
            cp.wait_send()
            cp.wait_recv()

    out = pl.pallas_call(
        body, name=name,
        out_shape=(*[pltpu.HBM(s.shape, s.dtype) for s in ss], *[pltpu.HBM(l.shape, l.dtype) for l in lands]),
        in_specs=[HBM_SPEC] * (2 * nw) + [SEM_SPEC, SEM_SPEC, ANY], out_specs=tuple([HBM_SPEC] * (2 * nw)),
        input_output_aliases={k: k for k in range(2 * nw)},
        compiler_params=pltpu.CompilerParams(has_side_effects=DATAFLOW),
    )(*ss, *lands, send_sems, recv_sems, after)
    return list(out[nw:])


def _sum_small(pack):
    rows = pack.shape[0]

    def body(pack_ref, sum_ref, all_ref, send_sems, recv_sems):
        x, y, c, _ = _place()
        me = 4 * x + 2 * y + c
        all_ref[me] = pack_ref[...]
        cps = []
        for k in range(1, N_DEV):
            to = (1 - x if k & 4 else x, 1 - y if k & 2 else y, 1 - c if k & 1 else c)
            cp = pltpu.make_async_remote_copy(
                src_ref=pack_ref, dst_ref=all_ref.at[me], send_sem=send_sems.at[k - 1],
                recv_sem=recv_sems.at[k - 1], device_id=to, device_id_type=MESH)
            cp.start()
            cps.append(cp)
        for cp in cps:
            cp.wait()
        total = all_ref[0]
        for d in range(1, N_DEV):
            total = total + all_ref[d]
        sum_ref[...] = total

    return pl.pallas_call(
        body, name="sum_small_grads", in_specs=[VMEM_SPEC], out_specs=VMEM_SPEC,
        out_shape=jax.ShapeDtypeStruct(pack.shape, F32),
        scratch_shapes=[pltpu.VMEM((N_DEV, rows, LANES), F32), pltpu.SemaphoreType.DMA((N_DEV - 1,)),
                        pltpu.SemaphoreType.DMA((N_DEV - 1,))],
    )(pack)


def _share_with_sibling(fs, name):
    nw = len(fs)

    def body(*refs):
        ins, outs = refs[:nw], refs[nw:2 * nw]
        send_sems, recv_sems = refs[2 * nw:]
        x, y, c, _ = _place()
        cps = []
        for i in range(nw):
            cp = pltpu.make_async_remote_copy(src_ref=ins[i].at[c], dst_ref=outs[i].at[c], send_sem=send_sems.at[i],
                                              recv_sem=recv_sems.at[i], device_id=(x, y, 1 - c), device_id_type=MESH)
            cp.start()
            cps.append(cp)
        for cp in cps:
            cp.wait()

    return pl.pallas_call(
        body, name=name, in_specs=[ANY] * nw, out_specs=[ANY] * nw,
        out_shape=[jax.ShapeDtypeStruct(f.shape, f.dtype) for f in fs],
        input_output_aliases={i: i for i in range(nw)},
        scratch_shapes=[pltpu.SemaphoreType.DMA((nw,)), pltpu.SemaphoreType.DMA((nw,))],
    )(*fs)


def _sum_sibling(g, land, where_arr, name):
    P, Q = g.shape[-2:]
    tp = _pick(P, (256, 128, 64))

    def body(w_ref, g_ref, l_ref, s_ref):
        s_ref[...] = (g_ref[...].astype(F32) + l_ref[...].astype(F32)).astype(BF16)

    if g.ndim == 4:
        g_spec = pl.BlockSpec((None, None, tp, Q), lambda j, r, w: (w[1 + j], w[0], r, 0))
    else:
        g_spec = pl.BlockSpec((None, tp, Q), lambda j, r, w: (w[1 + j], r, 0))
    slab = pl.BlockSpec((None, tp, Q), lambda j, r, w: (w[1 + j], r, 0))
    return pl.pallas_call(
        body, name=name,
        grid_spec=pltpu.PrefetchScalarGridSpec(
            num_scalar_prefetch=1, grid=(N_CHIPS - 1, P // tp), in_specs=[g_spec, slab], out_specs=slab),
        out_shape=jax.ShapeDtypeStruct((N_CHIPS, P, Q), BF16),
        compiler_params=_params(("parallel", "parallel")),
    )(where_arr, g, land)


def _sum_chips(g, land, recv, sc_arr, name):
    P, Q = g.shape[-2:]
    tp = _pick(P, (256, 128, 64))

    def body(sc_ref, g_ref, l_ref, r_ref, f_ref):
        acc = g_ref[...].astype(F32) + l_ref[...].astype(F32)
        for j in range(N_CHIPS - 1):
            acc = acc + r_ref[j].astype(F32)
        f_ref[...] = acc

    if g.ndim == 4:
        g_spec = pl.BlockSpec((None, None, tp, Q), lambda r, sc: (sc[0], sc[1], r, 0))
    else:
        g_spec = pl.BlockSpec((None, tp, Q), lambda r, sc: (sc[0], r, 0))
    return pl.pallas_call(
        body, name=name,
        grid_spec=pltpu.PrefetchScalarGridSpec(
            num_scalar_prefetch=1, grid=(P // tp,),
            in_specs=[g_spec,
                      pl.BlockSpec((None, tp, Q), lambda r, sc: (sc[0], r, 0)),
                      pl.BlockSpec((N_CHIPS - 1, tp, Q), lambda r, sc: (0, r, 0))],
            out_specs=pl.BlockSpec((None, tp, Q), lambda r, sc: (sc[1], r, 0))),
        out_shape=jax.ShapeDtypeStruct((2, P, Q), F32),
        compiler_params=_params(("parallel",)),
    )(sc_arr, g, land, recv)


class _Reduction:
    def __init__(self, names, pieces, lands, flight):
        self.names, self.pieces, self.lands = names, pieces, lands
        self.send_sems, self.recv_sems, self.sums, self.zones, self.token = flight


def _reduce_start(pieces, lands, names, where_arr):
    tag = names[0] if len(names) == 1 else "branches"
    sums = [_sum_sibling(g, l, where_arr, "sum_sibling_" + nm) for g, l, nm in zip(pieces, lands, names)]
    return _Reduction(names, pieces, lands, _chips_send_start(sums, "grads_to_chips_start_" + tag))


def _wgrad_half(a, b, half_arr, name, after=None):
    K, M = a.shape
    nslab = b.shape[1] // N_CHIPS
    tm = M // 2
    tn = _pick(nslab, (896, 512, 384, 256, 128))
    per = nslab // tn
    assert 4 * K * (tm + tn) <= MATMUL_OPERAND_BYTES

    def body(h_ref, a_ref, b_ref, *rest):
        rest[-1][...] = lax.dot_general(a_ref[...], b_ref[...], TN_DIMS, preferred_element_type=F32).astype(BF16)

    return pl.pallas_call(
        body, name=name,
        grid_spec=pltpu.PrefetchScalarGridSpec(
            num_scalar_prefetch=1, grid=(b.shape[1] // tn,),
            in_specs=[pl.BlockSpec((K, tm), lambda j, h: (0, h[0])), pl.BlockSpec((K, tn), lambda j, h: (0, j))]
            + ([] if after is None else [ANY]),
            out_specs=pl.BlockSpec((None, tm, tn), lambda j, h: (j // per, 0, j % per))),
        out_shape=jax.ShapeDtypeStruct((N_CHIPS, tm, nslab), BF16),
        compiler_params=_params(("parallel",)),
    )(*((half_arr, a, b) if after is None else (half_arr, a, b, after)))


def _sibling_send_start(g, name):
    zone = pltpu.with_memory_space_constraint(lax.empty(g.shape, g.dtype), pltpu.HBM)

    def body(g_ref, zone_ref, send_sem, recv_sem, g_thru, zone_thru, token):
        x, y, c, _ = _place()
        pltpu.make_async_remote_copy(src_ref=g_ref, dst_ref=zone_ref, send_sem=send_sem, recv_sem=recv_sem,
                                     device_id=(x, y, 1 - c), device_id_type=MESH).start()
        token[...] = jnp.zeros_like(token)

    return pl.pallas_call(
        body, name=name,
        out_shape=(pltpu.SemaphoreType.DMA(()), pltpu.SemaphoreType.DMA(()), pltpu.HBM(g.shape, g.dtype),
                   pltpu.HBM(g.shape, g.dtype), jax.ShapeDtypeStruct((8, LANES), F32)),
        in_specs=[HBM_SPEC, HBM_SPEC], out_specs=(SEM_SPEC, SEM_SPEC, HBM_SPEC, HBM_SPEC, VMEM_SPEC),
        input_output_aliases={0: 2, 1: 3},
        compiler_params=pltpu.CompilerParams(has_side_effects=DATAFLOW),
    )(pltpu.with_memory_space_constraint(g, pltpu.HBM), zone)


def _sibling_send_wait(send_sem, recv_sem, g, zone, after, name):
    def body(g_ref, zone_ref, s_sem, r_sem, after_ref, g_out, zone_out):
        x, y, c, _ = _place()
        cp = pltpu.make_async_remote_copy(src_ref=g_ref, dst_ref=zone_ref, send_sem=s_sem, recv_sem=r_sem,
                                          device_id=(x, y, 1 - c), device_id_type=MESH)
        cp.wait_send()
        cp.wait_recv()

    return pl.pallas_call(
        body, name=name, out_shape=(pltpu.HBM(g.shape, g.dtype), pltpu.HBM(zone.shape, zone.dtype)),
        in_specs=[HBM_SPEC, HBM_SPEC, SEM_SPEC, SEM_SPEC, ANY], out_specs=(HBM_SPEC, HBM_SPEC),
        input_output_aliases={0: 0, 1: 1},
        compiler_params=pltpu.CompilerParams(has_side_effects=DATAFLOW),
    )(g, zone, send_sem, recv_sem, after)[1]


def _reduce_finish(red, after, sc_arr):
    tag = red.names[0] if len(red.names) == 1 else "branches"
    recvs = _chips_send_wait(red.send_sems, red.recv_sems, red.sums, red.zones, after, "grads_to_chips_wait_" + tag)
    halves = [_sum_chips(g, l, r, sc_arr, "sum_chips_" + nm) for g, l, r, nm in zip(red.pieces, red.lands, recvs, red.names)]
    return _share_with_sibling(halves, "grads_share_sibling_" + tag)


def _t5_bucket(dist):
    max_exact = REL_BUCKETS // 2
    d = jnp.maximum(dist, 0)
    df = jnp.maximum(d, 1).astype(F32)
    large = max_exact + (jnp.log(df / max_exact) / math.log(REL_MAX_DIST / max_exact)
                         * (REL_BUCKETS - max_exact)).astype(jnp.int32)
    large = jnp.minimum(large, REL_BUCKETS - 1)
    return jnp.where(d < max_exact, d, large)


def _bucket_table():
    qi = jnp.arange(WINDOW)[:, None]
    si = jnp.arange(2 * WINDOW)[None, :]
    return _t5_bucket(qi + WINDOW - si)


TILE_WORDS = 8 * LANES


def _tile_rows(shape):
    return -(-math.prod(shape) // TILE_WORDS) * 8


def _rows_of(a):
    flat = a.reshape(-1).astype(F32)
    n = _tile_rows(a.shape) * LANES
    return jnp.pad(flat, (0, n - flat.shape[0])).reshape(-1, LANES)


def _pack_rows(parts):
    return jnp.concatenate([_rows_of(p) for p in parts], axis=0)


def _unpack_rows(packed, shapes):
    out, at = [], 0
    for shp in shapes:
        n, nr = math.prod(shp), _tile_rows(shp)
        out.append(packed[at:at + nr].reshape(-1)[:n].reshape(shp))
        at += nr
    return out


def kernel(x, norm_pre, w_in, rel_bias, attn_sinks, lb_logits, hgrn_norm, w_branch_attn, w_branch_hgrn, w_out, norm_post, loss_target, m_norm_pre, m_w_in, m_rel_bias, m_attn_sinks, m_lb_logits, m_hgrn_norm, m_w_branch_attn, m_w_branch_hgrn, m_w_out, m_norm_post, v_norm_pre, v_w_in, v_rel_bias, v_attn_sinks, v_lb_logits, v_hgrn_norm, v_w_branch_attn, v_w_branch_hgrn, v_w_out, v_norm_post):
    B_loc, S, D = x.shape
    T = B_loc * S
    x2 = x.reshape(T, D)
    tgt2 = loss_target.reshape(T, D)
    my_x, my_y, my_c = lax.axis_index("x"), lax.axis_index("y"), lax.axis_index("c")

    c_arr = jnp.reshape(my_c, (1,)).astype(jnp.int32)
    s_arr = jnp.reshape(2 * my_x + my_y, (1,)).astype(jnp.int32)
    sc_arr = jnp.concatenate([s_arr, c_arr])
    shard_ws = [w_in[0], w_branch_attn[0], w_branch_hgrn[0], w_out[0]]
    shard_axes = (1, 1, 1, 0)
    names = ["w_in", "w_branch_attn", "w_branch_hgrn", "w_out"]
    placed = [_cast_into_full(w, ax, s_arr, "cast_" + nm) for w, ax, nm in zip(shard_ws, shard_axes, names)]
    peer_slabs = [jnp.reshape(t, (1,)).astype(jnp.int32)
                  for t in (2 * (1 - my_x) + my_y, 2 * my_x + 1 - my_y, 2 * (1 - my_x) + 1 - my_y)]

    buckets = _bucket_table()
    onehot = (buckets.reshape(-1)[:, None] == jnp.arange(REL_BUCKETS)[None, :]).astype(F32)
    bias_tab = jnp.dot(onehot, rel_bias.astype(F32), precision=lax.Precision.HIGHEST).T.reshape(ATTN_HEADS, WINDOW, 2 * WINDOW)
    sinks_b = jnp.broadcast_to(attn_sinks[0].astype(F32)[:, None, None], (ATTN_HEADS, 8, LANES))
    lb_fn = lambda l: jnp.cumsum(jax.nn.softmax(l.astype(F32), axis=0), axis=0)[:1]
    lb, lb_vjp = jax.vjp(lb_fn, lb_logits)
    gain_h = hgrn_norm[0].reshape(1, HGRN_WIDTH)

    h, rstd = _rmsnorm_fwd(x2, norm_pre)
    gather_in = _Gather(placed[:1], [shard_ws[0].shape], shard_axes[:1], "w_in")
    gather_in.start([0, 1])
    proj = _matmul_slab(h, gather_in.fulls[0], s_arr, None, gather_in.token, "in_proj_own")
    gather_in.wait([0, 1], proj)
    gather_in.forward([0, 1])
    gather_in.start([2])
    proj = _matmul_slab(h, gather_in.fulls[0], peer_slabs[0], proj, gather_in.token, "in_proj_peer0")
    proj = _matmul_slab(h, gather_in.fulls[0], peer_slabs[1], proj, None, "in_proj_peer1")
    gather_in.wait([2], proj)
    gather_in.forward([2])
    gather_rest = _Gather(placed[1:], [w.shape for w in shard_ws[1:]], shard_axes[1:], "rest")
    gather_rest.start([0, 1, 2], gather_in.fulls[0])
    proj = _matmul_slab(h, gather_in.fulls[0], peer_slabs[2], proj, gather_rest.token, "in_proj_peer2")
    win_f = gather_in.fulls[0]
    ya, attn_probs = _attn_fwd(proj, bias_tab, sinks_b, B_loc, S)
    yh, states = _hgrn_fwd(proj, lb, gain_h, B_loc, S)
    gather_rest.wait([0, 1, 2], yh)
    gather_rest.forward([0, 1, 2])
    wba_f, wbh_f, wout_f = gather_rest.fulls
    ua = _matmul(ya, wba_f, "nn", BF16, "branch_attn_proj")
    uh, merged = _branch_proj_merge(yh, wbh_f, proj, ua)
    yv = _matmul(merged, wout_f, "nn", F32, "out_proj")
    dy, dout, loss_p, gnpost_p = _post_loss(yv, x2, tgt2, norm_post)

    g_wout = _matmul(merged, dy, "tn", BF16, "out_proj_wgrad")
    d_ua, d_uh, dproj = _out_dgrad_merge_bwd(dy, wout_f, proj, ua, uh)
    g_wba = _matmul(ya, d_ua, "tn", BF16, "branch_attn_wgrad", slabs=N_CHIPS)
    g_wbh = _matmul(yh, d_uh, "tn", BF16, "branch_hgrn_wgrad", slabs=N_CHIPS)
    where_arr = jnp.concatenate([c_arr] + peer_slabs)
    late_pieces = [g.reshape(N_CHIPS, 2, -1, g.shape[-1]) for g in (g_wba, g_wbh, g_wout)]
    late = _reduce_start(late_pieces, _exchange_sibling_halves(late_pieces, "grads_to_sibling_branches"), names[1:], where_arr)
    d_ya = _matmul(d_ua, wba_f, "nt", BF16, "branch_attn_dgrad", after=late.token)
    d_yh = _matmul(d_uh, wbh_f, "nt", BF16, "branch_hgrn_dgrad", after=late.token)
    dproj, dbias_p, dsinks_p = _attn_bwd(proj, attn_probs, d_ya, dproj, B_loc, S)
    dproj, dlb_p, dgain_p = _hgrn_bwd(proj, states, lb, gain_h, d_yh, dproj, B_loc, S)
    g_give = _wgrad_half(h, dproj, 1 - c_arr, "in_proj_wgrad_sibling_half")
    s_sem, r_sem, g_give, zone, token = _sibling_send_start(g_give, "grads_to_sibling_start_w_in")
    g_keep = _wgrad_half(h, dproj, c_arr, "in_proj_wgrad_own_half", after=token)
    land = _sibling_send_wait(s_sem, r_sem, g_give, zone, g_keep, "grads_to_sibling_wait_w_in")
    last = _reduce_start([g_keep], [land], names[:1], where_arr)
    dh = _matmul(dproj, win_f, "nt", F32, "in_proj_dgrad", after=last.token)
    grad_x2, gnpre_p = _rmsnorm_bwd(dh, x2, rstd, norm_pre, dout)
    shared = _reduce_finish(last, grad_x2, sc_arr) + _reduce_finish(late, grad_x2, sc_arr)

    grelb_p = jnp.dot(dbias_p.reshape(ATTN_HEADS, -1), onehot, precision=lax.Precision.HIGHEST).T
    gsinks_p = dsinks_p[:, 0, 0]
    dlb_sum = jnp.sum(dlb_p, axis=0).reshape(1, HGRN_WIDTH)
    ghn_p = jnp.sum(dgain_p, axis=0).reshape(HGRN_HEADS, HGRN_DIM)
    small_parts = [gnpre_p, gnpost_p, grelb_p, gsinks_p, dlb_sum, ghn_p, loss_p]
    small_shapes = [p.shape for p in small_parts]
    pack_sum = _sum_small(_pack_rows(small_parts))
    big_w = [w_in, w_branch_attn, w_branch_hgrn, w_out]
    big_m = [m_w_in, m_w_branch_attn, m_w_branch_hgrn, m_w_out]
    big_v = [v_w_in, v_w_branch_attn, v_w_branch_hgrn, v_w_out]
    big = {}
    for nm, gs, w, m, v in zip(names, shared, big_w, big_m, big_v):
        shp = w.shape
        g2 = gs.reshape(shp[1], shp[2])
        d, nm_, nv_, g_out = _adamw(w[0], g2, m[0], v[0], "adamw_" + nm)
        big[nm] = tuple(a.reshape(shp) for a in (g_out, d, nm_, nv_))

    gnpre, gnpost, grelb, gsinks, dlb, ghn, loss = _unpack_rows(pack_sum, small_shapes)
    (g_lb_logits,) = lb_vjp(dlb)
    small_names = ["norm_pre", "rel_bias", "attn_sinks", "lb_logits", "hgrn_norm", "norm_post"]
    small_w = [norm_pre, rel_bias, attn_sinks, lb_logits, hgrn_norm, norm_post]
    small_m = [m_norm_pre, m_rel_bias, m_attn_sinks, m_lb_logits, m_hgrn_norm, m_norm_post]
    small_v = [v_norm_pre, v_rel_bias, v_attn_sinks, v_lb_logits, v_hgrn_norm, v_norm_post]
    small_g = [gnpre.reshape(norm_pre.shape), grelb.reshape(rel_bias.shape), gsinks.reshape(attn_sinks.shape),
               g_lb_logits.reshape(lb_logits.shape), ghn.reshape(hgrn_norm.shape), gnpost.reshape(norm_post.shape)]
    shapes = [w.shape for w in small_w]
    d_s, nm_s, nv_s, _ = _adamw(_pack_rows(small_w), _pack_rows(small_g), _pack_rows(small_m), _pack_rows(small_v),
                             "adamw_small")
    small = {}
    for nm, g, d, m_, v_ in zip(small_names, small_g, _unpack_rows(d_s, shapes), _unpack_rows(nm_s, shapes),
                                _unpack_rows(nv_s, shapes)):
        small[nm] = (g, d, m_, v_)

    allw = {**big, **small}
    order = ["norm_pre", "w_in", "rel_bias", "attn_sinks", "lb_logits", "hgrn_norm", "w_branch_attn", "w_branch_hgrn",
             "w_out", "norm_post"]
    outs = [loss.reshape(()), grad_x2.reshape(B_loc, S, D)]
    for k in range(4):
        outs += [allw[nm][k] for nm in order]
    return tuple(outs)
```

```python
import functools
import math

import jax
import jax.numpy as jnp
from jax import lax
from jax.experimental import pallas as pl
from jax.experimental.pallas import tpu as pltpu

F32 = jnp.float32
BF16 = jnp.bfloat16
MESH = pl.DeviceIdType.MESH

ATTN_HEADS = 16
ATTN_KV_HEADS = 4
HEAD_DIM = 64
GROUP = ATTN_HEADS // ATTN_KV_HEADS
WINDOW = 128
ATTN_WIDTH = ATTN_HEADS * HEAD_DIM
KV_WIDTH = ATTN_KV_HEADS * HEAD_DIM
HGRN_HEADS = 8
HGRN_DIM = 128
HGRN_WIDTH = HGRN_HEADS * HGRN_DIM
CHUNK = 64
SUB = 16
NSUB = CHUNK // SUB
REL_BUCKETS = 32
REL_MAX_DIST = 128
NORM_EPS = 1e-6
ADAM_LR = 0.001
ADAM_B1 = 0.9
ADAM_B2 = 0.999
ADAM_EPS = 1e-08
ADAM_WD = 0.01
ADAM_STEP = 10
LANES = 128
N_CHIPS = 4
N_DEV = 8
VMEM_LIMIT = 48 * 1024 * 1024
MATMUL_OPERAND_BYTES = 34 * 1024 * 1024
MATMUL_VMEM_BYTES = 44 * 1024 * 1024

OFF_AQ = 0
OFF_AK = OFF_AQ + ATTN_WIDTH
OFF_AV = OFF_AK + KV_WIDTH
OFF_AG = OFF_AV + KV_WIDTH
OFF_HQ = OFF_AG + ATTN_WIDTH
OFF_HF = OFF_HQ + HGRN_WIDTH
OFF_HI = OFF_HF + HGRN_WIDTH
OFF_HG = OFF_HI + HGRN_WIDTH
OFF_GA = OFF_HG + HGRN_WIDTH

NT_DIMS = (((1,), (1,)), ((), ()))
TN_DIMS = (((0,), (0,)), ((), ()))
NN_DIMS = (((1,), (0,)), ((), ()))


def _pick(n, cands):
    for c in cands:
        if n % c == 0:
            return c
    raise ValueError(f"no tile for {n} in {cands}")


def _params(sem):
    return pltpu.CompilerParams(dimension_semantics=sem, vmem_limit_bytes=VMEM_LIMIT)


def _bdot(a, b, dims=NN_DIMS):
    return lax.dot_general(a.astype(BF16), b.astype(BF16), dims, preferred_element_type=F32)


def _matmul(a, b, mode, out_dtype, name, slabs=1, after=None):
    if mode == "nn":
        (M, K), (K2, N) = a.shape, b.shape
    elif mode == "nt":
        (M, K), (N, K2) = a.shape, b.shape
    else:
        (K, M), (K2, N) = a.shape, b.shape
    assert K == K2
    nslab = N // slabs
    tm = _pick(M, (1024, 512, 256, 128))
    out_bytes = jnp.dtype(out_dtype).itemsize
    choices = []
    for tn in (2688, 1024, 896, 512, 384, 256, 128):
        for tk in (4096, 3584, 2048, 1792, 1536, 1024, 512, 256, 128):
            acc = 0 if tk == K else 4 * tm * tn
            if (nslab % tn == 0 and K % tk == 0 and 4 * tk * (tm + tn) <= MATMUL_OPERAND_BYTES
                    and 4 * tk * (tm + tn) + 2 * out_bytes * tm * tn + acc <= MATMUL_VMEM_BYTES):
                choices.append((K // tk > 1, -tn, tn, tk))
                break
    _, _, tn, tk = min(choices)
    nk = K // tk
    per = nslab // tn
    dims = {"nn": NN_DIMS, "nt": NT_DIMS, "tn": TN_DIMS}[mode]

    n_in = 2 if after is None else 3

    def body(*refs):
        a_ref, b_ref, o_ref, acc = refs[0], refs[1], refs[n_in], refs[n_in + 1:]
        part = lax.dot_general(a_ref[...], b_ref[...], dims, preferred_element_type=F32)
        if nk == 1:
            o_ref[...] = part.astype(o_ref.dtype)
            return
        acc_ref, = acc
        k = pl.program_id(2)

        @pl.when(k == 0)
        def _():
            acc_ref[...] = part

        @pl.when((k > 0) & (k < nk - 1))
        def _():
            acc_ref[...] += part

        @pl.when(k == nk - 1)
        def _():
            o_ref[...] = (acc_ref[...] + part).astype(o_ref.dtype)

    if mode == "tn":
        a_spec = pl.BlockSpec((tk, tm), lambda i, j, k: (k, i))
    else:
        a_spec = pl.BlockSpec((tm, tk), lambda i, j, k: (i, k))
    if mode == "nt":
        b_spec = pl.BlockSpec((tn, tk), lambda i, j, k: (j, k))
    else:
        b_spec = pl.BlockSpec((tk, tn), lambda i, j, k: (k, j))
    if slabs == 1:
        o_shape = jax.ShapeDtypeStruct((M, N), out_dtype)
        o_spec = pl.BlockSpec((tm, tn), lambda i, j, k: (i, j))
    else:
        o_shape = jax.ShapeDtypeStruct((slabs, M, nslab), out_dtype)
        o_spec = pl.BlockSpec((None, tm, tn), lambda i, j, k: (j // per, i, j % per))
    return pl.pallas_call(
        body, name=name, grid=(M // tm, N // tn, nk), in_specs=[a_spec, b_spec] + ([] if after is None else [ANY]),
        out_specs=o_spec, out_shape=o_shape,
        scratch_shapes=[pltpu.VMEM((tm, tn), F32)] if nk > 1 else [],
        compiler_params=_params(("parallel", "parallel", "arbitrary")),
    )(*((a, b) if after is None else (a, b, after)))


def _rmsnorm_fwd(x2, gain):
    T, D = x2.shape
    tr = _pick(T, (256, 128))

    def body(x_ref, g_ref, h_ref, r_ref):
        xv = x_ref[...]
        r = lax.rsqrt(jnp.mean(xv * xv, axis=-1, keepdims=True) + NORM_EPS)
        h_ref[...] = (xv * r * g_ref[...]).astype(BF16)
        r_ref[...] = r

    return pl.pallas_call(
        body, name="rmsnorm_pre_fwd", grid=(T // tr,),
        in_specs=[pl.BlockSpec((tr, D), lambda i: (i, 0)), pl.BlockSpec((1, D), lambda i: (0, 0))],
        out_specs=[pl.BlockSpec((tr, D), lambda i: (i, 0)), pl.BlockSpec((tr, 1), lambda i: (i, 0))],
        out_shape=[jax.ShapeDtypeStruct((T, D), BF16), jax.ShapeDtypeStruct((T, 1), F32)],
        compiler_params=_params(("parallel",)),
    )(x2, gain)


def _rmsnorm_bwd(dh, x2, rstd, gain, dout):
    T, D = x2.shape
    tr = _pick(T, (256, 128))

    def body(dh_ref, x_ref, r_ref, g_ref, do_ref, gx_ref, gg_ref):
        @pl.when(pl.program_id(0) == 0)
        def _():
            gg_ref[...] = jnp.zeros_like(gg_ref)

        n = x_ref[...] * r_ref[...]
        dhv = dh_ref[...]
        dn = dhv * g_ref[...]
        gx_ref[...] = do_ref[...] + r_ref[...] * (dn - n * jnp.mean(dn * n, axis=-1, keepdims=True))
        gg_ref[...] += jnp.sum(dhv * n, axis=0, keepdims=True)

    row = pl.BlockSpec((tr, D), lambda i: (i, 0))
    vec = pl.BlockSpec((1, D), lambda i: (0, 0))
    return pl.pallas_call(
        body, name="rmsnorm_pre_bwd", grid=(T // tr,),
        in_specs=[row, row, pl.BlockSpec((tr, 1), lambda i: (i, 0)), vec, row],
        out_specs=[row, vec],
        out_shape=[jax.ShapeDtypeStruct((T, D), F32), jax.ShapeDtypeStruct((1, D), F32)],
        compiler_params=_params(("arbitrary",)),
    )(dh, x2, rstd, gain, dout)


def _post_loss(yv, x2, tgt2, gain):
    T, D = x2.shape
    tr = _pick(T, (256, 128))

    def body(y_ref, x_ref, t_ref, g_ref, dy_ref, do_ref, loss_ref, gg_ref):
        @pl.when(pl.program_id(0) == 0)
        def _():
            gg_ref[...] = jnp.zeros_like(gg_ref)
            loss_ref[...] = jnp.zeros_like(loss_ref)

        yv_ = y_ref[...]
        r = lax.rsqrt(jnp.mean(yv_ * yv_, axis=-1, keepdims=True) + NORM_EPS)
        n = yv_ * r
        e = (x_ref[...] + n * g_ref[...]) - t_ref[...]
        loss_ref[...] += 0.5 * jnp.sum(jnp.mean(e * e, axis=-1, keepdims=True), axis=0, keepdims=True)
        dz = e / D
        do_ref[...] = dz
        gg_ref[...] += jnp.sum(dz * n, axis=0, keepdims=True)
        dn = dz * g_ref[...]
        dy_ref[...] = (r * (dn - n * jnp.mean(dn * n, axis=-1, keepdims=True))).astype(BF16)

    row = pl.BlockSpec((tr, D), lambda i: (i, 0))
    vec = pl.BlockSpec((1, D), lambda i: (0, 0))
    return pl.pallas_call(
        body, name="post_norm_loss", grid=(T // tr,),
        in_specs=[row, row, row, vec],
        out_specs=[row, row, pl.BlockSpec((1, 1), lambda i: (0, 0)), vec],
        out_shape=[jax.ShapeDtypeStruct((T, D), BF16), jax.ShapeDtypeStruct((T, D), F32),
                   jax.ShapeDtypeStruct((1, 1), F32), jax.ShapeDtypeStruct((1, D), F32)],
        compiler_params=_params(("arbitrary",)),
    )(yv, x2, tgt2, gain)


def _window(rows, cols, at):
    return pl.BlockSpec((pl.Element(rows), pl.Element(cols)), at)


def _gate_windows(tm, tn, D):
    return [_window(tm, tn, lambda i, j: (i * tm, pl.multiple_of(OFF_GA + j * tn, LANES))),
            _window(tm, tn, lambda i, j: (i * tm, pl.multiple_of(OFF_GA + D + j * tn, LANES)))]


def _branch_proj_merge(yh, wbh, proj, ua):
    T, K = yh.shape
    D = wbh.shape[1]
    tm, tn = _pick(T, (512, 256, 128)), _pick(D, (1024, 512, 256))

    def body(a_ref, b_ref, ga_ref, gh_ref, ua_ref, uh_ref, m_ref):
        uh = jnp.dot(a_ref[...], b_ref[...], preferred_element_type=F32).astype(BF16)
        uh_ref[...] = uh
        m_ref[...] = (jax.nn.sigmoid(ga_ref[...]) * ua_ref[...].astype(F32)
                      + jax.nn.sigmoid(gh_ref[...]) * uh.astype(F32)).astype(BF16)

    blk = pl.BlockSpec((tm, tn), lambda i, j: (i, j))
    o = jax.ShapeDtypeStruct((T, D), BF16)
    return pl.pallas_call(
        body, name="branch_hgrn_proj_merge", grid=(T // tm, D // tn),
        in_specs=[pl.BlockSpec((tm, K), lambda i, j: (i, 0)), pl.BlockSpec((K, tn), lambda i, j: (0, j))]
        + _gate_windows(tm, tn, D) + [blk],
        out_specs=[blk, blk], out_shape=[o, o],
        compiler_params=_params(("parallel", "arbitrary")),
    )(yh, wbh, proj, proj, ua)


def _out_dgrad_merge_bwd(dy, wout, proj, ua, uh):
    T, K = dy.shape
    D = wout.shape[0]
    tm, tn = _pick(T, (512, 256, 128)), _pick(D, (1024, 512, 256))
    nj = D // tn

    def body(a_ref, b_ref, ga_ref, gh_ref, ua_ref, uh_ref, dua_ref, duh_ref, dproj_ref):
        d = lax.dot_general(a_ref[...], b_ref[...], NT_DIMS, preferred_element_type=F32)
        sa = jax.nn.sigmoid(ga_ref[...])
        sh = jax.nn.sigmoid(gh_ref[...])
        dua_ref[...] = (d * sa).astype(BF16)
        duh_ref[...] = (d * sh).astype(BF16)
        dga = (d * ua_ref[...].astype(F32) * sa * (1.0 - sa)).astype(BF16)
        dgh = (d * uh_ref[...].astype(F32) * sh * (1.0 - sh)).astype(BF16)
        for jj in range(nj):
            @pl.when(pl.program_id(1) == jj)
            def _():
                dproj_ref[:, jj * tn:(jj + 1) * tn] = dga
                dproj_ref[:, D + jj * tn:D + (jj + 1) * tn] = dgh

    blk = pl.BlockSpec((tm, tn), lambda i, j: (i, j))
    o = jax.ShapeDtypeStruct((T, D), BF16)
    return pl.pallas_call(
        body, name="out_proj_dgrad_merge_bwd", grid=(T // tm, nj),
        in_specs=[pl.BlockSpec((tm, K), lambda i, j: (i, 0)), pl.BlockSpec((tn, K), lambda i, j: (j, 0))]
        + _gate_windows(tm, tn, D) + [blk, blk],
        out_specs=[blk, blk, _window(tm, 2 * D, lambda i, j: (i * tm, OFF_GA))],
        out_shape=[o, o, jax.ShapeDtypeStruct((T, proj.shape[1]), BF16)],
        compiler_params=_params(("parallel", "arbitrary")),
    )(dy, wout, proj, proj, ua, uh)


KV_PAIR = 2
PAIR_HEADS = KV_PAIR * GROUP


def _attn_mask(n):
    qi = lax.broadcasted_iota(jnp.int32, (WINDOW, 2 * WINDOW), 0)
    si = lax.broadcasted_iota(jnp.int32, (WINDOW, 2 * WINDOW), 1)
    dist = qi + WINDOW - si
    return (dist >= 0) & (dist < WINDOW) & ((si >= WINDOW) | (n > 0))


def _first_key_column(shape):
    return lax.broadcasted_iota(jnp.int32, shape, 1) == 0


def _attn_group_fwd(mask, q, k, v, ag, bias, sink):
    mask = jnp.concatenate([mask] * GROUP, axis=0)
    s = lax.dot_general(q.astype(BF16), k.astype(BF16), NT_DIMS, preferred_element_type=F32) * (HEAD_DIM ** -0.5)
    s = jnp.where(mask, s + bias, -1e30)
    m = jnp.maximum(jnp.max(s, axis=-1, keepdims=True), sink)
    p = jnp.exp(s - m)
    e_sink = jnp.exp(sink - m)
    den = jnp.sum(p, axis=-1, keepdims=True) + e_sink
    pb = p.astype(BF16)
    o = jnp.dot(pb, v.astype(BF16), preferred_element_type=F32)
    kept = jnp.where(_first_key_column(pb.shape), e_sink.astype(BF16), pb)
    return o * (jax.nn.silu(ag) / den), kept


def _attn_group_bwd(q, k, v, ag, kept, dout):
    e_sink = kept[:, 0:1].astype(F32)
    pb = jnp.where(_first_key_column(kept.shape), jnp.zeros_like(kept), kept)
    pf = pb.astype(F32)
    rden = 1.0 / (jnp.sum(pf, axis=-1, keepdims=True) + e_sink)
    qb, kb, vb = q.astype(BF16), k.astype(BF16), v.astype(BF16)
    o = jnp.dot(pb, vb, preferred_element_type=F32) * rden
    sg = jax.nn.sigmoid(ag)
    d_o = dout * (ag * sg)
    dag = dout * o * (sg * (1.0 + ag * (1.0 - sg)))
    d_row = jnp.sum(d_o * o, axis=-1, keepdims=True)
    ds = (pf * rden) * (lax.dot_general(d_o.astype(BF16), vb, NT_DIMS, preferred_element_type=F32) - d_row)
    dsb = ds.astype(BF16)
    dq = jnp.dot(dsb, kb, preferred_element_type=F32) * (HEAD_DIM ** -0.5)
    dk = lax.dot_general(dsb, qb, TN_DIMS, preferred_element_type=F32) * (HEAD_DIM ** -0.5)
    dv = lax.dot_general(pb, (d_o * rden).astype(BF16), TN_DIMS, preferred_element_type=F32)
    return dq, dk, dv, dag, ds, -(e_sink * rden * d_row)


def _attn_specs(B_loc, nb, order):
    qw = PAIR_HEADS * HEAD_DIM
    kw = KV_PAIR * HEAD_DIM

    def rows(g):
        b, p, n = order(*g)
        return b * nb + n

    def prev(g):
        b, p, n = order(*g)
        return b * nb + jnp.maximum(n - 1, 0)

    def pp(g):
        return order(*g)[1]

    q = pl.BlockSpec((WINDOW, qw), lambda *g: (rows(g), OFF_AQ // qw + pp(g)))
    kc = pl.BlockSpec((WINDOW, kw), lambda *g: (rows(g), OFF_AK // kw + pp(g)))
    kp = pl.BlockSpec((WINDOW, kw), lambda *g: (prev(g), OFF_AK // kw + pp(g)))
    vc = pl.BlockSpec((WINDOW, kw), lambda *g: (rows(g), OFF_AV // kw + pp(g)))
    vp = pl.BlockSpec((WINDOW, kw), lambda *g: (prev(g), OFF_AV // kw + pp(g)))
    ag = pl.BlockSpec((WINDOW, qw), lambda *g: (rows(g), OFF_AG // qw + pp(g)))
    bias = pl.BlockSpec((PAIR_HEADS, WINDOW, 2 * WINDOW), lambda *g: (pp(g), 0, 0))
    sink = pl.BlockSpec((PAIR_HEADS, 8, LANES), lambda *g: (pp(g), 0, 0))
    return [q, kc, kp, vc, vp, ag, bias, sink], rows, pp


def _attn_operands(q_ref, kc_ref, kp_ref, vc_ref, vp_ref, ag_ref, bias_ref, sink_ref, j):
    lo, hi = j * HEAD_DIM, (j + 1) * HEAD_DIM
    k = jnp.concatenate([kp_ref[:, lo:hi], kc_ref[:, lo:hi]], axis=0)
    v = jnp.concatenate([vp_ref[:, lo:hi], vc_ref[:, lo:hi]], axis=0)
    heads = [j * GROUP + g for g in range(GROUP)]
    q = jnp.concatenate([q_ref[:, h * HEAD_DIM:(h + 1) * HEAD_DIM] for h in heads], axis=0)
    ag = jnp.concatenate([ag_ref[:, h * HEAD_DIM:(h + 1) * HEAD_DIM] for h in heads], axis=0)
    bias = jnp.concatenate([bias_ref[h] for h in heads], axis=0)
    sink = jnp.concatenate([jnp.broadcast_to(sink_ref[h, 0:1, 0:1], (WINDOW, 1)) for h in heads], axis=0)
    return q, k, v, ag, bias, sink


def _attn_fwd(proj, bias_tab, sinks_b, B_loc, S):
    T = B_loc * S
    nb = S // WINDOW
    n_pairs = ATTN_KV_HEADS // KV_PAIR
    in_specs, rows, pp = _attn_specs(B_loc, nb, lambda b, p, n: (b, p, n))

    def body(q_ref, kc_ref, kp_ref, vc_ref, vp_ref, ag_ref, bias_ref, sink_ref, ya_ref, p_ref):
        mask = _attn_mask(pl.program_id(2))
        for j in range(KV_PAIR):
            out, kept = _attn_group_fwd(
                mask, *_attn_operands(q_ref, kc_ref, kp_ref, vc_ref, vp_ref, ag_ref, bias_ref, sink_ref, j))
            for g in range(GROUP):
                h = j * GROUP + g
                blk = slice(g * WINDOW, (g + 1) * WINDOW)
                ya_ref[:, h * HEAD_DIM:(h + 1) * HEAD_DIM] = out[blk].astype(BF16)
                p_ref[:, h * 2 * WINDOW:(h + 1) * 2 * WINDOW] = kept[blk]

    return pl.pallas_call(
        body, name="attn_fwd", grid=(B_loc, n_pairs, nb), in_specs=in_specs,
        out_specs=[pl.BlockSpec((WINDOW, PAIR_HEADS * HEAD_DIM), lambda *g: (rows(g), pp(g))),
                   pl.BlockSpec((WINDOW, PAIR_HEADS * 2 * WINDOW), lambda *g: (rows(g), pp(g)))],
        out_shape=[jax.ShapeDtypeStruct((T, ATTN_WIDTH), BF16),
                   jax.ShapeDtypeStruct((T, ATTN_HEADS * 2 * WINDOW), BF16)],
        compiler_params=_params(("parallel", "parallel", "parallel")),
    )(proj, proj, proj, proj, proj, proj, bias_tab, sinks_b)


def _attn_bwd(proj, probs, dya, dproj, B_loc, S):
    nb = S // WINDOW
    n_pairs = ATTN_KV_HEADS // KV_PAIR
    qw, kw = PAIR_HEADS * HEAD_DIM, KV_PAIR * HEAD_DIM

    def at(off, width, back=0):
        return lambda b, i, p: ((b * nb + jnp.maximum(nb - 1 - i - back, 0)) * WINDOW,
                                pl.multiple_of(off + p * width, LANES))

    in_specs = [_window(WINDOW, qw, at(OFF_AQ, qw)), _window(WINDOW, kw, at(OFF_AK, kw)),
                _window(WINDOW, kw, at(OFF_AK, kw, 1)), _window(WINDOW, kw, at(OFF_AV, kw)),
                _window(WINDOW, kw, at(OFF_AV, kw, 1)), _window(WINDOW, qw, at(OFF_AG, qw)),
                pl.BlockSpec((WINDOW, PAIR_HEADS * 2 * WINDOW), lambda b, i, p: (b * nb + nb - 1 - i, p)),
                pl.BlockSpec((WINDOW, qw), lambda b, i, p: (b * nb + nb - 1 - i, p)), ANY]

    def body(q_ref, kc_ref, kp_ref, vc_ref, vp_ref, ag_ref, p_ref, dya_ref, dproj_in,
             dproj_ref, dbias_ref, dsink_ref, dkc_ref, dvc_ref):
        b, i, p = pl.program_id(0), pl.program_id(1), pl.program_id(2)

        @pl.when((b == 0) & (i == 0) & (p == 0))
        def _():
            dbias_ref[...] = jnp.zeros_like(dbias_ref)
            dsink_ref[...] = jnp.zeros_like(dsink_ref)

        @pl.when(i == 0)
        def _():
            dkc_ref[p] = jnp.zeros((WINDOW, kw), F32)
            dvc_ref[p] = jnp.zeros((WINDOW, kw), F32)

        dk_carry, dv_carry = dkc_ref[p], dvc_ref[p]
        dqs, dags, dbiases, dsinks, dks, dvs = [], [], [], [], [], []
        for j in range(KV_PAIR):
            heads = [j * GROUP + g for g in range(GROUP)]
            lo, hi = j * HEAD_DIM, (j + 1) * HEAD_DIM
            stack = lambda parts: jnp.concatenate(parts, axis=0)
            k = stack([kp_ref[:, lo:hi], kc_ref[:, lo:hi]])
            v = stack([vp_ref[:, lo:hi], vc_ref[:, lo:hi]])
            q = stack([q_ref[:, h * HEAD_DIM:(h + 1) * HEAD_DIM] for h in heads])
            ag = stack([ag_ref[:, h * HEAD_DIM:(h + 1) * HEAD_DIM] for h in heads])
            dout = stack([dya_ref[:, h * HEAD_DIM:(h + 1) * HEAD_DIM].astype(F32) for h in heads])
            kept = stack([p_ref[:, h * 2 * WINDOW:(h + 1) * 2 * WINDOW] for h in heads])
            dq, dk, dv, dag, dbias, dsink = _attn_group_bwd(q, k, v, ag, kept, dout)
            lo, hi = j * HEAD_DIM, (j + 1) * HEAD_DIM
            dks.append((dk[WINDOW:] + dk_carry[:, lo:hi]).astype(BF16))
            dvs.append((dv[WINDOW:] + dv_carry[:, lo:hi]).astype(BF16))
            dkc_ref[p, :, lo:hi] = dk[:WINDOW]
            dvc_ref[p, :, lo:hi] = dv[:WINDOW]
            for g in range(GROUP):
                blk = slice(g * WINDOW, (g + 1) * WINDOW)
                dqs.append(dq[blk].astype(BF16))
                dags.append(dag[blk].astype(BF16))
                dbiases.append(dbias[blk])
                dsinks.append(jnp.broadcast_to(jnp.sum(dsink[blk], axis=0, keepdims=True), (8, LANES)))

        for pair in range(n_pairs):
            @pl.when(p == pair)
            def _():
                for j in range(KV_PAIR):
                    col = (pair * KV_PAIR + j) * HEAD_DIM
                    dproj_ref[:, OFF_AK + col:OFF_AK + col + HEAD_DIM] = dks[j]
                    dproj_ref[:, OFF_AV + col:OFF_AV + col + HEAD_DIM] = dvs[j]
                for hh in range(PAIR_HEADS):
                    h = pair * PAIR_HEADS + hh
                    dproj_ref[:, OFF_AQ + h * HEAD_DIM:OFF_AQ + (h + 1) * HEAD_DIM] = dqs[hh]
                    dproj_ref[:, OFF_AG + h * HEAD_DIM:OFF_AG + (h + 1) * HEAD_DIM] = dags[hh]
                    dbias_ref[h] += dbiases[hh]
                    dsink_ref[h] += dsinks[hh]

    return pl.pallas_call(
        body, name="attn_bwd", grid=(B_loc, nb, n_pairs), in_specs=in_specs,
        out_specs=[_window(WINDOW, OFF_HQ, lambda b, i, p: ((b * nb + nb - 1 - i) * WINDOW, 0)),
                   pl.BlockSpec((ATTN_HEADS, WINDOW, 2 * WINDOW), lambda b, i, p: (0, 0, 0)),
                   pl.BlockSpec((ATTN_HEADS, 8, LANES), lambda b, i, p: (0, 0, 0))],
        out_shape=[jax.ShapeDtypeStruct(dproj.shape, dproj.dtype),
                   jax.ShapeDtypeStruct((ATTN_HEADS, WINDOW, 2 * WINDOW), F32),
                   jax.ShapeDtypeStruct((ATTN_HEADS, 8, LANES), F32)],
        input_output_aliases={8: 0},
        scratch_shapes=[pltpu.VMEM((n_pairs, WINDOW, kw), F32), pltpu.VMEM((n_pairs, WINDOW, kw), F32)],
        compiler_params=_params(("arbitrary", "arbitrary", "arbitrary")),
    )(proj, proj, proj, proj, proj, proj, probs, dya, dproj)


class _HgrnPre:
    def __init__(self, fr, qr, lb, g_scr):
        t = lax.broadcasted_iota(jnp.int32, (CHUNK, CHUNK), 0)
        s = lax.broadcasted_iota(jnp.int32, (CHUNK, CHUNK), 1)
        self.sg = jax.nn.sigmoid(fr)
        self.f = lb + (1.0 - lb) * self.sg
        g = jnp.dot((t >= s).astype(F32), jnp.log(self.f), precision=lax.Precision.HIGHEST, preferred_element_type=F32)
        g_scr[...] = g
        self.g = g
        self.row = lax.broadcasted_iota(jnp.int32, g.shape, 0)
        self.refs = [jnp.zeros((1, g.shape[1]), F32)] + [g_scr[pl.ds(i * SUB - 1, 1), :] for i in range(1, NSUB)]
        self.gend = g_scr[pl.ds(CHUNK - 1, 1), :]
        refrow = jnp.zeros_like(g)
        for i in range(1, NSUB):
            refrow = jnp.where(self.row >= i * SUB, self.refs[i], refrow)
        self.sigq = jax.nn.sigmoid(qr)
        self.qs = qr * self.sigq
        self.k = 1.0 - self.f
        self.eg = jnp.exp(g)
        self.eqd = jnp.exp(g - refrow)
        self.ekd = [jnp.exp(jnp.where(self.row < (i + 1) * SUB, self.refs[i] - g, 0.0)) for i in range(NSUB)]
        self.ekdec = jnp.exp(self.gend - g)
        self.qg = self.qs * self.eg
        self.qd = self.qs * self.eqd
        self.kd = [self.k * e for e in self.ekd]
        self.kdec = self.k * self.ekdec
        self.egend = jnp.exp(self.gend)


def _hgrn_pair_mask():
    t = lax.broadcasted_iota(jnp.int32, (CHUNK, NSUB * CHUNK), 0)
    col = lax.broadcasted_iota(jnp.int32, (CHUNK, NSUB * CHUNK), 1)
    return ((t // SUB) == (col // CHUNK)) & ((col % CHUNK) <= t)


def _hgrn_head_out(p, lanes, state_t, v, mask):
    qg, qd = p.qg[:, lanes], p.qd[:, lanes]
    kall = jnp.concatenate([kd[:, lanes] for kd in p.kd], axis=0)
    vst = jnp.concatenate([v] * NSUB, axis=0)
    am = jnp.where(mask, _bdot(qd, kall, NT_DIMS), 0.0)
    o = _bdot(qg, state_t, NT_DIMS) + _bdot(am, vst)
    return o, (qg, qd, kall, am, vst)


def _hgrn_fwd(proj, lb, gain, B_loc, S):
    T = B_loc * S
    nc = S // CHUNK
    nh = HGRN_HEADS

    def at(off):
        return lambda b, n: ((b * nc + n) * CHUNK, off)

    vec = pl.BlockSpec((1, HGRN_WIDTH), lambda b, n: (0, 0))

    def body(q_ref, f_ref, v_ref, hg_ref, lb_ref, gain_ref, yh_ref, st_ref, state_scr, g_scr):
        @pl.when(pl.program_id(1) == 0)
        def _():
            state_scr[...] = jnp.zeros_like(state_scr)

        p = _HgrnPre(f_ref[...], q_ref[...], lb_ref[...], g_scr)
        v = v_ref[...]
        gate = gain_ref[...] * jax.nn.silu(hg_ref[...])
        mask = _hgrn_pair_mask()
        for hd in range(nh):
            lanes = slice(hd * HGRN_DIM, (hd + 1) * HGRN_DIM)
            st = state_scr[hd]
            st_ref[hd] = st
            o, _ = _hgrn_head_out(p, lanes, st, v[:, lanes], mask)
            rs = lax.rsqrt(jnp.mean(o * o, axis=-1, keepdims=True) + NORM_EPS)
            yh_ref[:, lanes] = (o * rs * gate[:, lanes]).astype(BF16)
            state_scr[hd] = st * p.egend[:, lanes] + _bdot(v[:, lanes], p.kdec[:, lanes], TN_DIMS)

    return pl.pallas_call(
        body, name="hgrn_fwd", grid=(B_loc, nc),
        in_specs=[_window(CHUNK, HGRN_WIDTH, at(OFF_HQ)), _window(CHUNK, HGRN_WIDTH, at(OFF_HF)),
                  _window(CHUNK, HGRN_WIDTH, at(OFF_HI)), _window(CHUNK, HGRN_WIDTH, at(OFF_HG)), vec, vec],
        out_specs=[pl.BlockSpec((CHUNK, HGRN_WIDTH), lambda b, n: (b * nc + n, 0)),
                   pl.BlockSpec((None, None, nh, HGRN_DIM, HGRN_DIM), lambda b, n: (b, n, 0, 0, 0))],
        out_shape=[jax.ShapeDtypeStruct((T, HGRN_WIDTH), BF16),
                   jax.ShapeDtypeStruct((B_loc, nc, nh, HGRN_DIM, HGRN_DIM), F32)],
        scratch_shapes=[pltpu.VMEM((nh, HGRN_DIM, HGRN_DIM), F32), pltpu.VMEM((CHUNK, HGRN_WIDTH), F32)],
        compiler_params=_params(("parallel", "arbitrary")),
    )(proj, proj, proj, proj, lb, gain)


def _hgrn_bwd(proj, states, lb, gain, dyh, dproj, B_loc, S):
    nc = S // CHUNK
    nh = HGRN_HEADS

    def at(off):
        return lambda b, i: ((b * nc + nc - 1 - i) * CHUNK, off)

    vec = pl.BlockSpec((1, HGRN_WIDTH), lambda b, i: (0, 0))
    in_specs = [_window(CHUNK, HGRN_WIDTH, at(OFF_HQ)), _window(CHUNK, HGRN_WIDTH, at(OFF_HF)),
                _window(CHUNK, HGRN_WIDTH, at(OFF_HI)), _window(CHUNK, HGRN_WIDTH, at(OFF_HG)), vec, vec,
                pl.BlockSpec((None, None, nh, HGRN_DIM, HGRN_DIM), lambda b, i: (b, nc - 1 - i, 0, 0, 0)),
                pl.BlockSpec((CHUNK, HGRN_WIDTH), lambda b, i: (b * nc + nc - 1 - i, 0)), ANY]
    acc_spec = pl.BlockSpec((None, 1, HGRN_WIDTH), lambda b, i: (b, 0, 0))

    def body(q_ref, f_ref, v_ref, hg_ref, lb_ref, gain_ref, st_ref, dyh_ref, dproj_in,
             dproj_ref, dlb_ref, dgain_ref, dstate_scr, g_scr, dg_scr):
        dq_ref, df_ref, dv_ref, dhg_ref = [dproj_ref.at[:, pl.ds(k * HGRN_WIDTH, HGRN_WIDTH)] for k in range(4)]

        @pl.when(pl.program_id(1) == 0)
        def _():
            dstate_scr[...] = jnp.zeros_like(dstate_scr)
            dlb_ref[...] = jnp.zeros_like(dlb_ref)
            dgain_ref[...] = jnp.zeros_like(dgain_ref)

        qr, lb, gain, hg, v = q_ref[...], lb_ref[...], gain_ref[...], hg_ref[...], v_ref[...]
        p = _HgrnPre(f_ref[...], qr, lb, g_scr)
        sgh = jax.nn.sigmoid(hg)
        sil = hg * sgh
        dy = dyh_ref[...].astype(F32)
        mask = _hgrn_pair_mask()
        dqg, dqd, dkdec, dv, dhg, dgend, dgain = [], [], [], [], [], [], []
        dkd = [[] for _ in range(NSUB)]
        heads = [slice(hd * HGRN_DIM, (hd + 1) * HGRN_DIM) for hd in range(nh)]
        sts = [st_ref[hd] for hd in range(nh)]
        dnews = [dstate_scr[hd] for hd in range(nh)]
        fwd = [_hgrn_head_out(p, lanes, st, v[:, lanes], mask) for lanes, st in zip(heads, sts)]
        for lanes, st, dnew in zip(heads, sts, dnews):
            dkdec_h = _bdot(v[:, lanes], dnew)
            dkdec.append(dkdec_h)
            dgend.append(jnp.sum(dkdec_h * p.kdec[:, lanes], axis=0, keepdims=True)
                         + jnp.sum(dnew * st, axis=0, keepdims=True) * p.egend[:, lanes])
        dos = []
        gate_grad = sgh * (1.0 + hg * (1.0 - sgh))
        for lanes, (o, _) in zip(heads, fwd):
            rs = lax.rsqrt(jnp.mean(o * o, axis=-1, keepdims=True) + NORM_EPS)
            n = o * rs
            dyn = dy[:, lanes] * n
            dgain.append(jnp.sum(dyn * sil[:, lanes], axis=0, keepdims=True))
            dhg.append(dyn * gain[:, lanes] * gate_grad[:, lanes])
            dn = dy[:, lanes] * gain[:, lanes] * sil[:, lanes]
            dos.append(rs * (dn - n * jnp.mean(dn * n, axis=-1, keepdims=True)))
        drs = []
        for hd, (lanes, st, dnew, do, (_, (qg, qd, kall, am, vst))) in enumerate(zip(heads, sts, dnews, dos, fwd)):
            dqg.append(_bdot(do, st))
            dstate_scr[hd] = _bdot(do, qg, TN_DIMS) + dnew * p.egend[:, lanes]
            drs.append(jnp.where(mask, _bdot(do, vst, NT_DIMS), 0.0))
            dvst = _bdot(am, do, TN_DIMS)
            dv.append(sum(dvst[i * CHUNK:(i + 1) * CHUNK] for i in range(NSUB)) + _bdot(p.kdec[:, lanes], dnew, NT_DIMS))
        for dr, (_, (qg, qd, kall, am, vst)) in zip(drs, fwd):
            dqd.append(_bdot(dr, kall))
            dkall = _bdot(dr, qd, TN_DIMS)
            for i in range(NSUB):
                dkd[i].append(dkall[i * CHUNK:(i + 1) * CHUNK])

        wide = lambda parts: jnp.concatenate(parts, axis=1)
        dqg, dqd, dkdec = wide(dqg), wide(dqd), wide(dkdec)
        t2 = dqd * p.qd
        dg = dqg * p.qg + t2 - dkdec * p.kdec
        dk = dkdec * p.ekdec
        dg_scr[...] = jnp.zeros_like(dg_scr)
        for i in range(NSUB):
            dkd_i = wide(dkd[i])
            tk = jnp.where(p.row < (i + 1) * SUB, dkd_i * p.kd[i], 0.0)
            dg = dg - tk
            dk = dk + dkd_i * p.ekd[i]
            if i >= 1:
                in_blk = (p.row >= i * SUB) & (p.row < (i + 1) * SUB)
                dg_scr[pl.ds(i * SUB - 1, 1), :] = (jnp.sum(tk, axis=0, keepdims=True)
                                                    - jnp.sum(jnp.where(in_blk, t2, 0.0), axis=0, keepdims=True))
        dg_scr[pl.ds(CHUNK - 1, 1), :] = wide(dgend)
        t = lax.broadcasted_iota(jnp.int32, (CHUNK, CHUNK), 0)
        s = lax.broadcasted_iota(jnp.int32, (CHUNK, CHUNK), 1)
        dlogf = jnp.dot((t <= s).astype(F32), dg + dg_scr[...], precision=lax.Precision.HIGHEST, preferred_element_type=F32)
        df = dlogf / p.f - dk
        df_ref[...] = (df * (1.0 - lb) * p.sg * (1.0 - p.sg)).astype(BF16)
        dlb_ref[...] += jnp.sum(df * (1.0 - p.sg), axis=0, keepdims=True)
        dq_ref[...] = ((dqg * p.eg + dqd * p.eqd) * p.sigq * (1.0 + qr * (1.0 - p.sigq))).astype(BF16)
        dv_ref[...] = wide(dv).astype(BF16)
        dhg_ref[...] = wide(dhg).astype(BF16)
        dgain_ref[...] += wide(dgain)

    acc = jax.ShapeDtypeStruct((B_loc, 1, HGRN_WIDTH), F32)
    return pl.pallas_call(
        body, name="hgrn_bwd", grid=(B_loc, nc), in_specs=in_specs,
        out_specs=[_window(CHUNK, 4 * HGRN_WIDTH, at(OFF_HQ)), acc_spec, acc_spec],
        out_shape=[jax.ShapeDtypeStruct(dproj.shape, dproj.dtype), acc, acc],
        input_output_aliases={8: 0},
        scratch_shapes=[pltpu.VMEM((nh, HGRN_DIM, HGRN_DIM), F32), pltpu.VMEM((CHUNK, HGRN_WIDTH), F32),
                        pltpu.VMEM((CHUNK, HGRN_WIDTH), F32)],
        compiler_params=_params(("parallel", "arbitrary")),
    )(proj, proj, proj, proj, lb, gain, states, dyh, dproj)


def _adamw(w, g, m, v, name):
    R, C = w.shape
    tr = _pick(R, (128, 64, 32, 16, 8)) if C > 1024 else _pick(R, (512, 256, 128, 64, 32, 16, 8))

    def body(w_ref, g_ref, m_ref, v_ref, d_ref, nm_ref, nv_ref, g_out_ref):
        gv = g_ref[...]
        g_out_ref[...] = gv
        nm = ADAM_B1 * m_ref[...] + (1.0 - ADAM_B1) * gv
        nv = ADAM_B2 * v_ref[...] + (1.0 - ADAM_B2) * (gv * gv)
        m_hat = nm / (1.0 - ADAM_B1 ** ADAM_STEP)
        v_hat = nv / (1.0 - ADAM_B2 ** ADAM_STEP)
        d_ref[...] = -ADAM_LR * (m_hat / (jnp.sqrt(v_hat) + ADAM_EPS) + ADAM_WD * w_ref[...])
        nm_ref[...] = nm
        nv_ref[...] = nv

    blk = pl.BlockSpec((tr, C), lambda i: (i, 0))
    o = jax.ShapeDtypeStruct((R, C), F32)
    return pl.pallas_call(
        body, name=name, grid=(R // tr,), in_specs=[blk] * 4, out_specs=[blk] * 4, out_shape=[o, o, o, o],
        compiler_params=_params(("parallel",)),
    )(w, g, m, v)


ANY = pl.BlockSpec(memory_space=pl.ANY)
VMEM_SPEC = pl.BlockSpec(memory_space=pltpu.VMEM)


def _place():
    x, y, c = lax.axis_index("x"), lax.axis_index("y"), lax.axis_index("c")
    other_chips = [(1 - x, y), (x, 1 - y), (1 - x, 1 - y)]
    return x, y, c, other_chips


def _cast_into_full(w, ax, s_arr, name):
    R, C = w.shape
    tr = _pick(R, (256, 128))
    nr = R // tr

    def body(s_ref, w_ref, o_ref):
        o_ref[...] = w_ref[...].astype(BF16)

    if ax == 1:
        shape, o_map = (R, N_CHIPS * C), lambda i, s: (i, s[0])
    else:
        shape, o_map = (N_CHIPS * R, C), lambda i, s: (s[0] * nr + i, 0)
    return pl.pallas_call(
        body, name=name,
        grid_spec=pltpu.PrefetchScalarGridSpec(
            num_scalar_prefetch=1, grid=(nr,), in_specs=[pl.BlockSpec((tr, C), lambda i, s: (i, 0))],
            out_specs=pl.BlockSpec((tr, C), o_map)),
        out_shape=jax.ShapeDtypeStruct(shape, BF16),
        compiler_params=_params(("parallel",)),
    )(s_arr, w)


class _Gather:
    def __init__(self, fulls, shard_shapes, axes, tag):
        self.shapes, self.axes, self.tag, self.nw = shard_shapes, axes, tag, len(fulls)
        self.fulls, self.sems, self.token = list(fulls), {}, None

    def start(self, peers, after=None):
        nw, np_ = self.nw, len(peers)

        def body(*refs):
            ins, sems = refs[:nw], refs[nw + (after is not None):nw + (after is not None) + 2 * np_]
            for k, j in enumerate(peers):
                for cp in self._peer_copies(ins, sems[2 * k], sems[2 * k + 1], j):
                    cp.start()
            refs[-1][...] = jnp.zeros_like(refs[-1])

        out = pl.pallas_call(
            body, name="gather_start_%s_%s" % (self.tag, "".join(map(str, peers))),
            out_shape=(*[pltpu.SemaphoreType.DMA((nw,))] * (2 * np_),
                       *[pltpu.HBM(f.shape, f.dtype) for f in self.fulls], jax.ShapeDtypeStruct((8, LANES), F32)),
            in_specs=[HBM_SPEC] * nw + ([] if after is None else [ANY]),
            out_specs=(*[SEM_SPEC] * (2 * np_), *[HBM_SPEC] * nw, VMEM_SPEC),
            input_output_aliases={k: 2 * np_ + k for k in range(nw)},
            compiler_params=pltpu.CompilerParams(has_side_effects=DATAFLOW),
        )(*[pltpu.with_memory_space_constraint(f, pltpu.HBM) for f in self.fulls], *(() if after is None else (after,)))
        for k, j in enumerate(peers):
            self.sems[j] = (out[2 * k], out[2 * k + 1])
        self.fulls = list(out[2 * np_:2 * np_ + nw])
        self.token = out[-1]

    def _region(self, ref, i, t, half):
        R, C = self.shapes[i]
        hr = R // 2
        if self.axes[i] == 1:
            return ref.at[pl.ds(half * hr, hr), pl.ds(pl.multiple_of(t * C, LANES), C)]
        return ref.at[pl.ds(t * R + half * hr, hr), :]

    def _peer_copies(self, refs, send_sems, recv_sems, j):
        x, y, c, chips = _place()
        s = 2 * x + y
        return [pltpu.make_async_remote_copy(
            src_ref=self._region(refs[i], i, s, c), dst_ref=self._region(refs[i], i, s, c), send_sem=send_sems.at[i],
            recv_sem=recv_sems.at[i], device_id=(*chips[j], c), device_id_type=MESH) for i in range(self.nw)]

    def wait(self, peers, after):
        nw, np_ = self.nw, len(peers)

        def body(*refs):
            ins, sems = refs[:nw], refs[nw:nw + 2 * np_]
            for k, j in enumerate(peers):
                for cp in self._peer_copies(ins, sems[2 * k], sems[2 * k + 1], j):
                    cp.wait_send()
                    cp.wait_recv()

        sem_args = [s for j in peers for s in self.sems[j]]
        out = pl.pallas_call(
            body, name="gather_wait_%s_%s" % (self.tag, "".join(map(str, peers))),
            out_shape=tuple(pltpu.HBM(f.shape, f.dtype) for f in self.fulls),
            in_specs=[HBM_SPEC] * nw + [SEM_SPEC] * (2 * np_) + [ANY], out_specs=tuple([HBM_SPEC] * nw),
            input_output_aliases={k: k for k in range(nw)},
            compiler_params=pltpu.CompilerParams(has_side_effects=DATAFLOW),
        )(*self.fulls, *sem_args, after)
        self.fulls = list(out)

    def forward(self, peers):
        nw, np_ = self.nw, len(peers)

        def body(*refs):
            ins, outs = refs[:nw], refs[nw:2 * nw]
            send_sems, recv_sems = refs[2 * nw:]
            x, y, c, chips = _place()
            cps = []
            for i in range(nw):
                for k, j in enumerate(peers):
                    t = 2 * chips[j][0] + chips[j][1]
                    cp = pltpu.make_async_remote_copy(
                        src_ref=self._region(ins[i], i, t, c), dst_ref=self._region(outs[i], i, t, c),
                        send_sem=send_sems.at[i * np_ + k], recv_sem=recv_sems.at[i * np_ + k],
                        device_id=(x, y, 1 - c), device_id_type=MESH)
                    cp.start()
                    cps.append(cp)
            for cp in cps:
                cp.wait()

        out = pl.pallas_call(
            body, name="gather_forward_%s_%s" % (self.tag, "".join(map(str, peers))),
            in_specs=[ANY] * nw, out_specs=[ANY] * nw,
            out_shape=[jax.ShapeDtypeStruct(f.shape, f.dtype) for f in self.fulls],
            input_output_aliases={i: i for i in range(nw)},
            scratch_shapes=[pltpu.SemaphoreType.DMA((nw * np_,)), pltpu.SemaphoreType.DMA((nw * np_,))],
        )(*self.fulls)
        self.fulls = list(out)


def _matmul_slab(a, wfull, slab_arr, prev, after, name):
    M, K = a.shape
    N = wfull.shape[1]
    nslab = N // N_CHIPS
    tn = _pick(nslab, (2688, 896, 512, 384, 256, 128))
    tm = _pick(M, (512, 256, 128) if tn > 1024 else (1024, 512, 256, 128))
    per = nslab // tn
    extra = [e for e in (prev, after) if e is not None]

    def body(slab_ref, a_ref, b_ref, *rest):
        rest[len(extra)][...] = jnp.dot(a_ref[...], b_ref[...], preferred_element_type=F32)

    return pl.pallas_call(
        body, name=name,
        grid_spec=pltpu.PrefetchScalarGridSpec(
            num_scalar_prefetch=1, grid=(M // tm, per),
            in_specs=[pl.BlockSpec((tm, K), lambda i, j, sl: (i, 0)),
                      pl.BlockSpec((K, tn), lambda i, j, sl: (0, sl[0] * per + j))] + [ANY] * len(extra),
            out_specs=pl.BlockSpec((tm, tn), lambda i, j, sl: (i, sl[0] * per + j))),
        out_shape=jax.ShapeDtypeStruct((M, N), F32),
        input_output_aliases={} if prev is None else {3: 0},
        compiler_params=_params(("parallel", "arbitrary")),
    )(slab_arr, a, wfull, *extra)


def _exchange_sibling_halves(gs, name):
    nw = len(gs)

    def body(*refs):
        ins, outs = refs[:nw], refs[nw:2 * nw]
        send_sems, recv_sems = refs[2 * nw:]
        x, y, c, _ = _place()
        cps = []
        for i in range(nw):
            cp = pltpu.make_async_remote_copy(src_ref=ins[i].at[:, 1 - c], dst_ref=outs[i], send_sem=send_sems.at[i],
                                              recv_sem=recv_sems.at[i], device_id=(x, y, 1 - c), device_id_type=MESH)
            cp.start()
            cps.append(cp)
        for cp in cps:
            cp.wait()

    return pl.pallas_call(
        body, name=name, in_specs=[ANY] * nw, out_specs=[ANY] * nw,
        out_shape=[jax.ShapeDtypeStruct((g.shape[0],) + g.shape[2:], g.dtype) for g in gs],
        scratch_shapes=[pltpu.SemaphoreType.DMA((nw,)), pltpu.SemaphoreType.DMA((nw,))],
    )(*gs)


HBM_SPEC = pl.BlockSpec(memory_space=pltpu.HBM)
SEM_SPEC = pl.BlockSpec(memory_space=pltpu.SEMAPHORE)
DATAFLOW = pltpu.SideEffectType.DATAFLOW_SIDE_EFFECTING


def _chip_copies(ins, lands, send_sems, recv_sems):
    x, y, c, chips = _place()
    return [pltpu.make_async_remote_copy(
        src_ref=ins[i].at[2 * chip[0] + chip[1]], dst_ref=lands[i].at[j], send_sem=send_sems.at[i * 3 + j],
        recv_sem=recv_sems.at[i * 3 + j], device_id=(*chip, c), device_id_type=MESH)
        for i in range(len(ins)) for j, chip in enumerate(chips)]


def _chips_send_start(ss, name):
    nw = len(ss)
    lands = [pltpu.with_memory_space_constraint(lax.empty((N_CHIPS - 1,) + s.shape[1:], s.dtype), pltpu.HBM) for s in ss]

    def body(*refs):
        ins, land_refs = refs[:nw], refs[nw:2 * nw]
        send_sems, recv_sems = refs[2 * nw], refs[2 * nw + 1]
        token = refs[-1]
        for cp in _chip_copies(ins, land_refs, send_sems, recv_sems):
            cp.start()
        token[...] = jnp.zeros_like(token)

    n = 3 * nw
    out = pl.pallas_call(
        body, name=name,
        out_shape=(pltpu.SemaphoreType.DMA((n,)), pltpu.SemaphoreType.DMA((n,)),
                   *[pltpu.HBM(s.shape, s.dtype) for s in ss], *[pltpu.HBM(l.shape, l.dtype) for l in lands],
                   jax.ShapeDtypeStruct((8, LANES), F32)),
        in_specs=[HBM_SPEC] * (2 * nw), out_specs=(SEM_SPEC, SEM_SPEC, *[HBM_SPEC] * (2 * nw), VMEM_SPEC),
        input_output_aliases={k: 2 + k for k in range(2 * nw)},
        compiler_params=pltpu.CompilerParams(has_side_effects=DATAFLOW),
    )(*[pltpu.with_memory_space_constraint(s, pltpu.HBM) for s in ss], *lands)
    return out[0], out[1], list(out[2:2 + nw]), list(out[2 + nw:2 + 2 * nw]), out[-1]


def _chips_send_wait(send_sems, recv_sems, ss, lands, after, name):
    nw = len(ss)

    def body(*refs):
        ins, land_refs = refs[:nw], refs[nw:2 * nw]
        s_sems, r_sems = refs[2 * nw], refs[2 * nw + 1]
        for cp in _chip_copies(ins, land_refs, s_sems, r_sems):
            cp.wait_send()
            cp.wait_recv()

    out = pl.pallas_call(
        body, name=name,
        out_shape=(*[pltpu.HBM(s.shape, s.dtype) for s in ss], *[pltpu.HBM(l.shape, l.dtype) for l in lands]),
        in_specs=[HBM_SPEC] * (2 * nw) + [SEM_SPEC, SEM_SPEC, ANY], out_specs=tuple([HBM_SPEC] * (2 * nw)),
        input_output_aliases={k: k for k in range(2 * nw)},
        compiler_params=pltpu.CompilerParams(has_side_effects=DATAFLOW),
    )(*ss, *lands, send_sems, recv_sems, after)
    return list(out[nw:])


def _sum_small(pack):
    rows = pack.shape[0]

    def body(pack_ref, sum_ref, all_ref, send_sems, recv_sems):
        x, y, c, _ = _place()
        me = 4 * x + 2 * y + c
        all_ref[me] = pack_ref[...]
        cps = []
        for k in range(1, N_DEV):
            to = (1 - x if k & 4 else x, 1 - y if k & 2 else y, 1 - c if k & 1 else c)
            cp = pltpu.make_async_remote_copy(
                src_ref=pack_ref, dst_ref=all_ref.at[me], send_sem=send_sems.at[k - 1],
                recv_sem=recv_sems.at[k - 1], device_id=to, device_id_type=MESH)
            cp.start()
            cps.append(cp)
        for cp in cps:
            cp.wait()
        total = all_ref[0]
        for d in range(1, N_DEV):
            total = total + all_ref[d]
        sum_ref[...] = total

    return pl.pallas_call(
        body, name="sum_small_grads", in_specs=[VMEM_SPEC], out_specs=VMEM_SPEC,
        out_shape=jax.ShapeDtypeStruct(pack.shape, F32),
        scratch_shapes=[pltpu.VMEM((N_DEV, rows, LANES), F32), pltpu.SemaphoreType.DMA((N_DEV - 1,)),
                        pltpu.SemaphoreType.DMA((N_DEV - 1,))],
    )(pack)


def _share_with_sibling(fs, name):
    nw = len(fs)

    def body(*refs):
        ins, outs = refs[:nw], refs[nw:2 * nw]
        send_sems, recv_sems = refs[2 * nw:]
        x, y, c, _ = _place()
        cps = []
        for i in range(nw):
            cp = pltpu.make_async_remote_copy(src_ref=ins[i].at[c], dst_ref=outs[i].at[c], send_sem=send_sems.at[i],
                                              recv_sem=recv_sems.at[i], device_id=(x, y, 1 - c), device_id_type=MESH)
            cp.start()
            cps.append(cp)
        for cp in cps:
            cp.wait()

    return pl.pallas_call(
        body, name=name, in_specs=[ANY] * nw, out_specs=[ANY] * nw,
        out_shape=[jax.ShapeDtypeStruct(f.shape, f.dtype) for f in fs],
        input_output_aliases={i: i for i in range(nw)},
        scratch_shapes=[pltpu.SemaphoreType.DMA((nw,)), pltpu.SemaphoreType.DMA((nw,))],
    )(*fs)


def _sum_sibling(g, land, where_arr, name):
    P, Q = g.shape[-2:]
    tp = _pick(P, (256, 128, 64))

    def body(w_ref, g_ref, l_ref, s_ref):
        s_ref[...] = (g_ref[...].astype(F32) + l_ref[...].astype(F32)).astype(BF16)

    if g.ndim == 4:
        g_spec = pl.BlockSpec((None, None, tp, Q), lambda j, r, w: (w[1 + j], w[0], r, 0))
    else:
        g_spec = pl.BlockSpec((None, tp, Q), lambda j, r, w: (w[1 + j], r, 0))
    slab = pl.BlockSpec((None, tp, Q), lambda j, r, w: (w[1 + j], r, 0))
    return pl.pallas_call(
        body, name=name,
        grid_spec=pltpu.PrefetchScalarGridSpec(
            num_scalar_prefetch=1, grid=(N_CHIPS - 1, P // tp), in_specs=[g_spec, slab], out_specs=slab),
        out_shape=jax.ShapeDtypeStruct((N_CHIPS, P, Q), BF16),
        compiler_params=_params(("parallel", "parallel")),
    )(where_arr, g, land)


def _sum_chips(g, land, recv, sc_arr, name):
    P, Q = g.shape[-2:]
    tp = _pick(P, (256, 128, 64))

    def body(sc_ref, g_ref, l_ref, r_ref, f_ref):
        acc = g_ref[...].astype(F32) + l_ref[...].astype(F32)
        for j in range(N_CHIPS - 1):
            acc = acc + r_ref[j].astype(F32)
        f_ref[...] = acc

    if g.ndim == 4:
        g_spec = pl.BlockSpec((None, None, tp, Q), lambda r, sc: (sc[0], sc[1], r, 0))
    else:
        g_spec = pl.BlockSpec((None, tp, Q), lambda r, sc: (sc[0], r, 0))
    return pl.pallas_call(
        body, name=name,
        grid_spec=pltpu.PrefetchScalarGridSpec(
            num_scalar_prefetch=1, grid=(P // tp,),
            in_specs=[g_spec,
                      pl.BlockSpec((None, tp, Q), lambda r, sc: (sc[0], r, 0)),
                      pl.BlockSpec((N_CHIPS - 1, tp, Q), lambda r, sc: (0, r, 0))],
            out_specs=pl.BlockSpec((None, tp, Q), lambda r, sc: (sc[1], r, 0))),
        out_shape=jax.ShapeDtypeStruct((2, P, Q), F32),
        compiler_params=_params(("parallel",)),
    )(sc_arr, g, land, recv)


class _Reduction:
    def __init__(self, names, pieces, lands, flight):
        self.names, self.pieces, self.lands = names, pieces, lands
        self.send_sems, self.recv_sems, self.sums, self.zones, self.token = flight


def _reduce_start(pieces, lands, names, where_arr):
    tag = names[0] if len(names) == 1 else "branches"
    sums = [_sum_sibling(g, l, where_arr, "sum_sibling_" + nm) for g, l, nm in zip(pieces, lands, names)]
    return _Reduction(names, pieces, lands, _chips_send_start(sums, "grads_to_chips_start_" + tag))


def _wgrad_half(a, b, half_arr, name, after=None):
    K, M = a.shape
    nslab = b.shape[1] // N_CHIPS
    tm = M // 2
    tn = _pick(nslab, (896, 512, 384, 256, 128))
    per = nslab // tn
    assert 4 * K * (tm + tn) <= MATMUL_OPERAND_BYTES

    def body(h_ref, a_ref, b_ref, *rest):
        rest[-1][...] = lax.dot_general(a_ref[...], b_ref[...], TN_DIMS, preferred_element_type=F32).astype(BF16)

    return pl.pallas_call(
        body, name=name,
        grid_spec=pltpu.PrefetchScalarGridSpec(
            num_scalar_prefetch=1, grid=(b.shape[1] // tn,),
            in_specs=[pl.BlockSpec((K, tm), lambda j, h: (0, h[0])), pl.BlockSpec((K, tn), lambda j, h: (0, j))]
            + ([] if after is None else [ANY]),
            out_specs=pl.BlockSpec((None, tm, tn), lambda j, h: (j // per, 0, j % per))),
        out_shape=jax.ShapeDtypeStruct((N_CHIPS, tm, nslab), BF16),
        compiler_params=_params(("parallel",)),
    )(*((half_arr, a, b) if after is None else (half_arr, a, b, after)))


def _sibling_send_start(g, name):
    zone = pltpu.with_memory_space_constraint(lax.empty(g.shape, g.dtype), pltpu.HBM)

    def body(g_ref, zone_ref, send_sem, recv_sem, g_thru, zone_thru, token):
        x, y, c, _ = _place()
        pltpu.make_async_remote_copy(src_ref=g_ref, dst_ref=zone_ref, send_sem=send_sem, recv_sem=recv_sem,
                                     device_id=(x, y, 1 - c), device_id_type=MESH).start()
        token[...] = jnp.zeros_like(token)

    return pl.pallas_call(
        body, name=name,
        out_shape=(pltpu.SemaphoreType.DMA(()), pltpu.SemaphoreType.DMA(()), pltpu.HBM(g.shape, g.dtype),
                   pltpu.HBM(g.shape, g.dtype), jax.ShapeDtypeStruct((8, LANES), F32)),
        in_specs=[HBM_SPEC, HBM_SPEC], out_specs=(SEM_SPEC, SEM_SPEC, HBM_SPEC, HBM_SPEC, VMEM_SPEC),
        input_output_aliases={0: 2, 1: 3},
        compiler_params=pltpu.CompilerParams(has_side_effects=DATAFLOW),
    )(pltpu.with_memory_space_constraint(g, pltpu.HBM), zone)


def _sibling_send_wait(send_sem, recv_sem, g, zone, after, name):
    def body(g_ref, zone_ref, s_sem, r_sem, after_ref, g_out, zone_out):
        x, y, c, _ = _place()
        cp = pltpu.make_async_remote_copy(src_ref=g_ref, dst_ref=zone_ref, send_sem=s_sem, recv_sem=r_sem,
                                          device_id=(x, y, 1 - c), device_id_type=MESH)
        cp.wait_send()
        cp.wait_recv()

    return pl.pallas_call(
        body, name=name, out_shape=(pltpu.HBM(g.shape, g.dtype), pltpu.HBM(zone.shape, zone.dtype)),
        in_specs=[HBM_SPEC, HBM_SPEC, SEM_SPEC, SEM_SPEC, ANY], out_specs=(HBM_SPEC, HBM_SPEC),
        input_output_aliases={0: 0, 1: 1},
        compiler_params=pltpu.CompilerParams(has_side_effects=DATAFLOW),
    )(g, zone, send_sem, recv_sem, after)[1]


def _reduce_finish(red, after, sc_arr):
    tag = red.names[0] if len(red.names) == 1 else "branches"
    recvs = _chips_send_wait(red.send_sems, red.recv_sems, red.sums, red.zones, after, "grads_to_chips_wait_" + tag)
    halves = [_sum_chips(g, l, r, sc_arr, "sum_chips_" + nm) for g, l, r, nm in zip(red.pieces, red.lands, recvs, red.names)]
    return _share_with_sibling(halves, "grads_share_sibling_" + tag)


def _t5_bucket(dist):
    max_exact = REL_BUCKETS // 2
    d = jnp.maximum(dist, 0)
    df = jnp.maximum(d, 1).astype(F32)
    large = max_exact + (jnp.log(df / max_exact) / math.log(REL_MAX_DIST / max_exact)
                         * (REL_BUCKETS - max_exact)).astype(jnp.int32)
    large = jnp.minimum(large, REL_BUCKETS - 1)
    return jnp.where(d < max_exact, d, large)


def _bucket_table():
    qi = jnp.arange(WINDOW)[:, None]
    si = jnp.arange(2 * WINDOW)[None, :]
    return _t5_bucket(qi + WINDOW - si)


TILE_WORDS = 8 * LANES


def _tile_rows(shape):
    return -(-math.prod(shape) // TILE_WORDS) * 8


def _rows_of(a):
    flat = a.reshape(-1).astype(F32)
    n = _tile_rows(a.shape) * LANES
    return jnp.pad(flat, (0, n - flat.shape[0])).reshape(-1, LANES)


def _pack_rows(parts):
    return jnp.concatenate([_rows_of(p) for p in parts], axis=0)


def _unpack_rows(packed, shapes):
    out, at = [], 0
    for shp in shapes:
        n, nr = math.prod(shp), _tile_rows(shp)
        out.append(packed[at:at + nr].reshape(-1)[:n].reshape(shp))
        at += nr
    return out


def kernel(x, norm_pre, w_in, rel_bias, attn_sinks, lb_logits, hgrn_norm, w_branch_attn, w_branch_hgrn, w_out, norm_post, loss_target, m_norm_pre, m_w_in, m_rel_bias, m_attn_sinks, m_lb_logits, m_hgrn_norm, m_w_branch_attn, m_w_branch_hgrn, m_w_out, m_norm_post, v_norm_pre, v_w_in, v_rel_bias, v_attn_sinks, v_lb_logits, v_hgrn_norm, v_w_branch_attn, v_w_branch_hgrn, v_w_out, v_norm_post):
    B_loc, S, D = x.shape
    T = B_loc * S
    x2 = x.reshape(T, D)
    tgt2 = loss_target.reshape(T, D)
    my_x, my_y, my_c = lax.axis_index("x"), lax.axis_index("y"), lax.axis_index("c")

    c_arr = jnp.reshape(my_c, (1,)).astype(jnp.int32)
    s_arr = jnp.reshape(2 * my_x + my_y, (1,)).astype(jnp.int32)
    sc_arr = jnp.concatenate([s_arr, c_arr])
    shard_ws = [w_in[0], w_branch_attn[0], w_branch_hgrn[0], w_out[0]]
    shard_axes = (1, 1, 1, 0)
    names = ["w_in", "w_branch_attn", "w_branch_hgrn", "w_out"]
    placed = [_cast_into_full(w, ax, s_arr, "cast_" + nm) for w, ax, nm in zip(shard_ws, shard_axes, names)]
    peer_slabs = [jnp.reshape(t, (1,)).astype(jnp.int32)
                  for t in (2 * (1 - my_x) + my_y, 2 * my_x + 1 - my_y, 2 * (1 - my_x) + 1 - my_y)]

    buckets = _bucket_table()
    onehot = (buckets.reshape(-1)[:, None] == jnp.arange(REL_BUCKETS)[None, :]).astype(F32)
    bias_tab = jnp.dot(onehot, rel_bias.astype(F32), precision=lax.Precision.HIGHEST).T.reshape(ATTN_HEADS, WINDOW, 2 * WINDOW)
    sinks_b = jnp.broadcast_to(attn_sinks[0].astype(F32)[:, None, None], (ATTN_HEADS, 8, LANES))
    lb_fn = lambda l: jnp.cumsum(jax.nn.softmax(l.astype(F32), axis=0), axis=0)[:1]
    lb, lb_vjp = jax.vjp(lb_fn, lb_logits)
    gain_h = hgrn_norm[0].reshape(1, HGRN_WIDTH)

    h, rstd = _rmsnorm_fwd(x2, norm_pre)
    gather_in = _Gather(placed[:1], [shard_ws[0].shape], shard_axes[:1], "w_in")
    gather_in.start([0, 1])
    proj = _matmul_slab(h, gather_in.fulls[0], s_arr, None, gather_in.token, "in_proj_own")
    gather_in.wait([0, 1], proj)
    gather_in.forward([0, 1])
    gather_in.start([2])
    proj = _matmul_slab(h, gather_in.fulls[0], peer_slabs[0], proj, gather_in.token, "in_proj_peer0")
    proj = _matmul_slab(h, gather_in.fulls[0], peer_slabs[1], proj, None, "in_proj_peer1")
    gather_in.wait([2], proj)
    gather_in.forward([2])
    gather_rest = _Gather(placed[1:], [w.shape for w in shard_ws[1:]], shard_axes[1:], "rest")
    gather_rest.start([0, 1, 2], gather_in.fulls[0])
    proj = _matmul_slab(h, gather_in.fulls[0], peer_slabs[2], proj, gather_rest.token, "in_proj_peer2")
    win_f = gather_in.fulls[0]
    ya, attn_probs = _attn_fwd(proj, bias_tab, sinks_b, B_loc, S)
    yh, states = _hgrn_fwd(proj, lb, gain_h, B_loc, S)
    gather_rest.wait([0, 1, 2], yh)
    gather_rest.forward([0, 1, 2])
    wba_f, wbh_f, wout_f = gather_rest.fulls
    ua = _matmul(ya, wba_f, "nn", BF16, "branch_attn_proj")
    uh, merged = _branch_proj_merge(yh, wbh_f, proj, ua)
    yv = _matmul(merged, wout_f, "nn", F32, "out_proj")
    dy, dout, loss_p, gnpost_p = _post_loss(yv, x2, tgt2, norm_post)

    g_wout = _matmul(merged, dy, "tn", BF16, "out_proj_wgrad")
    d_ua, d_uh, dproj = _out_dgrad_merge_bwd(dy, wout_f, proj, ua, uh)
    g_wba = _matmul(ya, d_ua, "tn", BF16, "branch_attn_wgrad", slabs=N_CHIPS)
    g_wbh = _matmul(yh, d_uh, "tn", BF16, "branch_hgrn_wgrad", slabs=N_CHIPS)
    where_arr = jnp.concatenate([c_arr] + peer_slabs)
    late_pieces = [g.reshape(N_CHIPS, 2, -1, g.shape[-1]) for g in (g_wba, g_wbh, g_wout)]
    late = _reduce_start(late_pieces, _exchange_sibling_halves(late_pieces, "grads_to_sibling_branches"), names[1:], where_arr)
    d_ya = _matmul(d_ua, wba_f, "nt", BF16, "branch_attn_dgrad", after=late.token)
    d_yh = _matmul(d_uh, wbh_f, "nt", BF16, "branch_hgrn_dgrad", after=late.token)
    dproj, dbias_p, dsinks_p = _attn_bwd(proj, attn_probs, d_ya, dproj, B_loc, S)
    dproj, dlb_p, dgain_p = _hgrn_bwd(proj, states, lb, gain_h, d_yh, dproj, B_loc, S)
    g_give = _wgrad_half(h, dproj, 1 - c_arr, "in_proj_wgrad_sibling_half")
    s_sem, r_sem, g_give, zone, token = _sibling_send_start(g_give, "grads_to_sibling_start_w_in")
    g_keep = _wgrad_half(h, dproj, c_arr, "in_proj_wgrad_own_half", after=token)
    land = _sibling_send_wait(s_sem, r_sem, g_give, zone, g_keep, "grads_to_sibling_wait_w_in")
    last = _reduce_start([g_keep], [land], names[:1], where_arr)
    dh = _matmul(dproj, win_f, "nt", F32, "in_proj_dgrad", after=last.token)
    grad_x2, gnpre_p = _rmsnorm_bwd(dh, x2, rstd, norm_pre, dout)
    shared = _reduce_finish(last, grad_x2, sc_arr) + _reduce_finish(late, grad_x2, sc_arr)

    grelb_p = jnp.dot(dbias_p.reshape(ATTN_HEADS, -1), onehot, precision=lax.Precision.HIGHEST).T
    gsinks_p = dsinks_p[:, 0, 0]
    dlb_sum = jnp.sum(dlb_p, axis=0).reshape(1, HGRN_WIDTH)
    ghn_p = jnp.sum(dgain_p, axis=0).reshape(HGRN_HEADS, HGRN_DIM)
    small_parts = [gnpre_p, gnpost_p, grelb_p, gsinks_p, dlb_sum, ghn_p, loss_p]
    small_shapes = [p.shape for p in small_parts]
    pack_sum = _sum_small(_pack_rows(small_parts))
    big_w = [w_in, w_branch_attn, w_branch_hgrn, w_out]
    big_m = [m_w_in, m_w_branch_attn, m_w_branch_hgrn, m_w_out]
    big_v = [v_w_in, v_w_branch_attn, v_w_branch_hgrn, v_w_out]
    big = {}
    for nm, gs, w, m, v in zip(names, shared, big_w, big_m, big_v):
        shp = w.shape
        g2 = gs.reshape(shp[1], shp[2])
        d, nm_, nv_, g_out = _adamw(w[0], g2, m[0], v[0], "adamw_" + nm)
        big[nm] = tuple(a.reshape(shp) for a in (g_out, d, nm_, nv_))

    gnpre, gnpost, grelb, gsinks, dlb, ghn, loss = _unpack_rows(pack_sum, small_shapes)
    (g_lb_logits,) = lb_vjp(dlb)
    small_names = ["norm_pre", "rel_bias", "attn_sinks", "lb_logits", "hgrn_norm", "norm_post"]
    small_w = [norm_pre, rel_bias, attn_sinks, lb_logits, hgrn_norm, norm_post]
    small_m = [m_norm_pre, m_rel_bias, m_attn_sinks, m_lb_logits, m_hgrn_norm, m_norm_post]
    small_v = [v_norm_pre, v_rel_bias, v_attn_sinks, v_lb_logits, v_hgrn_norm, v_norm_post]
    small_g = [gnpre.reshape(norm_pre.shape), grelb.reshape(rel_bias.shape), gsinks.reshape(attn_sinks.shape),
               g_lb_logits.reshape(lb_logits.shape), ghn.reshape(hgrn_norm.shape), gnpost.reshape(norm_post.shape)]
    shapes = [w.shape for w in small_w]
    d_s, nm_s, nv_s, _ = _adamw(_pack_rows(small_w), _pack_rows(small_g), _pack_rows(small_m), _pack_rows(small_v),
                             "adamw_small")
    small = {}
    for nm, g, d, m_, v_ in zip(small_names, small_g, _unpack_rows(d_s, shapes), _unpack_rows(nm_s, shapes),
                                _unpack_rows(nv_s, shapes)):
        small[nm] = (g, d, m_, v_)

    allw = {**big, **small}
    order = ["norm_pre", "w_in", "rel_bias", "attn_sinks", "lb_logits", "hgrn_norm", "w_branch_attn", "w_branch_hgrn",
             "w_out", "norm_post"]
    outs = [loss.reshape(()), grad_x2.reshape(B_loc, S, D)]
    for k in range(4):
        outs += [allw[nm][k] for nm in order]
    return tuple(outs)
```

```python
import math

import jax
import jax.numpy as jnp
from jax import lax
from jax.experimental import pallas as pl
from jax.experimental.pallas import tpu as pltpu

F32 = jnp.float32
BF16 = jnp.bfloat16
MESH = pl.DeviceIdType.MESH

ATTN_HEADS = 16
ATTN_KV_HEADS = 4
HEAD_DIM = 64
GROUP = ATTN_HEADS // ATTN_KV_HEADS
WINDOW = 128
ATTN_WIDTH = ATTN_HEADS * HEAD_DIM
KV_WIDTH = ATTN_KV_HEADS * HEAD_DIM
HGRN_HEADS = 8
HGRN_DIM = 128
HGRN_WIDTH = HGRN_HEADS * HGRN_DIM
CHUNK = 64
SUB = 16
NSUB = CHUNK // SUB
REL_BUCKETS = 32
REL_MAX_DIST = 128
NORM_EPS = 1e-6
ADAM_LR = 0.001
ADAM_B1 = 0.9
ADAM_B2 = 0.999
ADAM_EPS = 1e-08
ADAM_WD = 0.01
ADAM_STEP = 10
LANES = 128
N_CHIPS = 4
N_DEV = 8
VMEM_LIMIT = 48 * 1024 * 1024
MATMUL_OPERAND_BYTES = 34 * 1024 * 1024
MATMUL_VMEM_BYTES = 44 * 1024 * 1024

OFF_AQ = 0
OFF_AK = OFF_AQ + ATTN_WIDTH
OFF_AV = OFF_AK + KV_WIDTH
OFF_AG = OFF_AV + KV_WIDTH
OFF_HQ = OFF_AG + ATTN_WIDTH
OFF_HF = OFF_HQ + HGRN_WIDTH
OFF_HI = OFF_HF + HGRN_WIDTH
OFF_HG = OFF_HI + HGRN_WIDTH
OFF_GA = OFF_HG + HGRN_WIDTH

NT_DIMS = (((1,), (1,)), ((), ()))
TN_DIMS = (((0,), (0,)), ((), ()))
NN_DIMS = (((1,), (0,)), ((), ()))


def _pick(n, cands):
    for c in cands:
        if n % c == 0:
            return c
    raise ValueError(f"no tile for {n} in {cands}")


def _params(sem):
    return pltpu.CompilerParams(dimension_semantics=sem, vmem_limit_bytes=VMEM_LIMIT)


def _bdot(a, b, dims=NN_DIMS):
    return lax.dot_general(a.astype(BF16), b.astype(BF16), dims, preferred_element_type=F32)


def _matmul(a, b, mode, out_dtype, name, slabs=1, after=None):
    if mode == "nn":
        (M, K), (K2, N) = a.shape, b.shape
    elif mode == "nt":
        (M, K), (N, K2) = a.shape, b.shape
    else:
        (K, M), (K2, N) = a.shape, b.shape
    assert K == K2
    nslab = N // slabs
    tm = _pick(M, (1024, 512, 256, 128))
    out_bytes = jnp.dtype(out_dtype).itemsize
    choices = []
    for tn in (2688, 1024, 896, 512, 384, 256, 128):
        for tk in (4096, 3584, 2048, 1792, 1536, 1024, 512, 256, 128):
            acc = 0 if tk == K else 4 * tm * tn
            if (nslab % tn == 0 and K % tk == 0 and 4 * tk * (tm + tn) <= MATMUL_OPERAND_BYTES
                    and 4 * tk * (tm + tn) + 2 * out_bytes * tm * tn + acc <= MATMUL_VMEM_BYTES):
                choices.append((K // tk > 1, -tn, tn, tk))
                break
    _, _, tn, tk = min(choices)
    nk = K // tk
    per = nslab // tn
    dims = {"nn": NN_DIMS, "nt": NT_DIMS, "tn": TN_DIMS}[mode]

    n_in = 2 if after is None else 3

    def body(*refs):
        a_ref, b_ref, o_ref, acc = refs[0], refs[1], refs[n_in], refs[n_in + 1:]
        part = lax.dot_general(a_ref[...], b_ref[...], dims, preferred_element_type=F32)
        if nk == 1:
            o_ref[...] = part.astype(o_ref.dtype)
            return
        acc_ref, = acc
        k = pl.program_id(2)

        @pl.when(k == 0)
        def _():
            acc_ref[...] = part

        @pl.when((k > 0) & (k < nk - 1))
        def _():
            acc_ref[...] += part

        @pl.when(k == nk - 1)
        def _():
            o_ref[...] = (acc_ref[...] + part).astype(o_ref.dtype)

    if mode == "tn":
        a_spec = pl.BlockSpec((tk, tm), lambda i, j, k: (k, i))
    else:
        a_spec = pl.BlockSpec((tm, tk), lambda i, j, k: (i, k))
    if mode == "nt":
        b_spec = pl.BlockSpec((tn, tk), lambda i, j, k: (j, k))
    else:
        b_spec = pl.BlockSpec((tk, tn), lambda i, j, k: (k, j))
    if slabs == 1:
        o_shape = jax.ShapeDtypeStruct((M, N), out_dtype)
        o_spec = pl.BlockSpec((tm, tn), lambda i, j, k: (i, j))
    else:
        o_shape = jax.ShapeDtypeStruct((slabs, M, nslab), out_dtype)
        o_spec = pl.BlockSpec((None, tm, tn), lambda i, j, k: (j // per, i, j % per))
    return pl.pallas_call(
        body, name=name, grid=(M // tm, N // tn, nk), in_specs=[a_spec, b_spec] + ([] if after is None else [ANY]),
        out_specs=o_spec, out_shape=o_shape,
        scratch_shapes=[pltpu.VMEM((tm, tn), F32)] if nk > 1 else [],
        compiler_params=_params(("parallel", "parallel", "arbitrary")),
    )(*((a, b) if after is None else (a, b, after)))


def _rmsnorm_fwd(x2, gain):
    T, D = x2.shape
    tr = _pick(T, (256, 128))

    def body(x_ref, g_ref, h_ref, r_ref):
        xv = x_ref[...]
        r = lax.rsqrt(jnp.mean(xv * xv, axis=-1, keepdims=True) + NORM_EPS)
        h_ref[...] = (xv * r * g_ref[...]).astype(BF16)
        r_ref[...] = r

    return pl.pallas_call(
        body, name="rmsnorm_pre_fwd", grid=(T // tr,),
        in_specs=[pl.BlockSpec((tr, D), lambda i: (i, 0)), pl.BlockSpec((1, D), lambda i: (0, 0))],
        out_specs=[pl.BlockSpec((tr, D), lambda i: (i, 0)), pl.BlockSpec((tr, 1), lambda i: (i, 0))],
        out_shape=[jax.ShapeDtypeStruct((T, D), BF16), jax.ShapeDtypeStruct((T, 1), F32)],
        compiler_params=_params(("parallel",)),
    )(x2, gain)


def _rmsnorm_bwd(dh, x2, rstd, gain, dout):
    T, D = x2.shape
    tr = _pick(T, (256, 128))

    def body(dh_ref, x_ref, r_ref, g_ref, do_ref, gx_ref, gg_ref):
        @pl.when(pl.program_id(0) == 0)
        def _():
            gg_ref[...] = jnp.zeros_like(gg_ref)

        n = x_ref[...] * r_ref[...]
        dhv = dh_ref[...].astype(F32)
        dn = dhv * g_ref[...]
        gx_ref[...] = do_ref[...] + r_ref[...] * (dn - n * jnp.mean(dn * n, axis=-1, keepdims=True))
        gg_ref[...] += jnp.sum(dhv * n, axis=0, keepdims=True)

    row = pl.BlockSpec((tr, D), lambda i: (i, 0))
    vec = pl.BlockSpec((1, D), lambda i: (0, 0))
    return pl.pallas_call(
        body, name="rmsnorm_pre_bwd", grid=(T // tr,),
        in_specs=[row, row, pl.BlockSpec((tr, 1), lambda i: (i, 0)), vec, row],
        out_specs=[row, vec],
        out_shape=[jax.ShapeDtypeStruct((T, D), F32), jax.ShapeDtypeStruct((1, D), F32)],
        compiler_params=_params(("arbitrary",)),
    )(dh, x2, rstd, gain, dout)


def _post_loss(yv, x2, tgt2, gain):
    T, D = x2.shape
    tr = _pick(T, (256, 128))

    def body(y_ref, x_ref, t_ref, g_ref, dy_ref, do_ref, loss_ref, gg_ref):
        @pl.when(pl.program_id(0) == 0)
        def _():
            gg_ref[...] = jnp.zeros_like(gg_ref)
            loss_ref[...] = jnp.zeros_like(loss_ref)

        yv_ = y_ref[...].astype(F32)
        r = lax.rsqrt(jnp.mean(yv_ * yv_, axis=-1, keepdims=True) + NORM_EPS)
        n = yv_ * r
        e = (x_ref[...] + n * g_ref[...]) - t_ref[...]
        loss_ref[...] += 0.5 * jnp.sum(jnp.mean(e * e, axis=-1, keepdims=True), axis=0, keepdims=True)
        dz = e / D
        do_ref[...] = dz
        gg_ref[...] += jnp.sum(dz * n, axis=0, keepdims=True)
        dn = dz * g_ref[...]
        dy_ref[...] = (r * (dn - n * jnp.mean(dn * n, axis=-1, keepdims=True))).astype(BF16)

    row = pl.BlockSpec((tr, D), lambda i: (i, 0))
    vec = pl.BlockSpec((1, D), lambda i: (0, 0))
    return pl.pallas_call(
        body, name="post_norm_loss", grid=(T // tr,),
        in_specs=[row, row, row, vec],
        out_specs=[row, row, pl.BlockSpec((1, 1), lambda i: (0, 0)), vec],
        out_shape=[jax.ShapeDtypeStruct((T, D), BF16), jax.ShapeDtypeStruct((T, D), F32),
                   jax.ShapeDtypeStruct((1, 1), F32), jax.ShapeDtypeStruct((1, D), F32)],
        compiler_params=_params(("arbitrary",)),
    )(yv, x2, tgt2, gain)


def _window(rows, cols, at):
    return pl.BlockSpec((pl.Element(rows), pl.Element(cols)), at)


def _gate_windows(tm, tn, D):
    return [_window(tm, tn, lambda i, j: (i * tm, pl.multiple_of(OFF_GA + j * tn, LANES))),
            _window(tm, tn, lambda i, j: (i * tm, pl.multiple_of(OFF_GA + D + j * tn, LANES)))]


def _branch_proj_merge(yh, wbh, proj, ua):
    T, K = yh.shape
    D = wbh.shape[1]
    tm, tn = _pick(T, (512, 256, 128)), _pick(D, (1024, 512, 256))

    def body(a_ref, b_ref, ga_ref, gh_ref, ua_ref, uh_ref, m_ref):
        uh = jnp.dot(a_ref[...], b_ref[...], preferred_element_type=F32).astype(BF16)
        uh_ref[...] = uh
        m_ref[...] = (jax.nn.sigmoid(ga_ref[...]) * ua_ref[...].astype(F32)
                      + jax.nn.sigmoid(gh_ref[...]) * uh.astype(F32)).astype(BF16)

    blk = pl.BlockSpec((tm, tn), lambda i, j: (i, j))
    o = jax.ShapeDtypeStruct((T, D), BF16)
    return pl.pallas_call(
        body, name="branch_hgrn_proj_merge", grid=(T // tm, D // tn),
        in_specs=[pl.BlockSpec((tm, K), lambda i, j: (i, 0)), pl.BlockSpec((K, tn), lambda i, j: (0, j))]
        + _gate_windows(tm, tn, D) + [blk],
        out_specs=[blk, blk], out_shape=[o, o],
        compiler_params=_params(("parallel", "arbitrary")),
    )(yh, wbh, proj, proj, ua)


def _out_dgrad_merge_bwd(dy, wout, proj, ua, uh):
    T, K = dy.shape
    D = wout.shape[0]
    tm, tn = _pick(T, (512, 256, 128)), _pick(D, (1024, 512, 256))
    nj = D // tn

    def body(a_ref, b_ref, ga_ref, gh_ref, ua_ref, uh_ref, dua_ref, duh_ref, dproj_ref):
        d = lax.dot_general(a_ref[...], b_ref[...], NT_DIMS, preferred_element_type=F32)
        sa = jax.nn.sigmoid(ga_ref[...])
        sh = jax.nn.sigmoid(gh_ref[...])
        dua_ref[...] = (d * sa).astype(BF16)
        duh_ref[...] = (d * sh).astype(BF16)
        dga = (d * ua_ref[...].astype(F32) * sa * (1.0 - sa)).astype(BF16)
        dgh = (d * uh_ref[...].astype(F32) * sh * (1.0 - sh)).astype(BF16)
        for jj in range(nj):
            @pl.when(pl.program_id(1) == jj)
            def _():
                dproj_ref[:, jj * tn:(jj + 1) * tn] = dga
                dproj_ref[:, D + jj * tn:D + (jj + 1) * tn] = dgh

    blk = pl.BlockSpec((tm, tn), lambda i, j: (i, j))
    o = jax.ShapeDtypeStruct((T, D), BF16)
    return pl.pallas_call(
        body, name="out_proj_dgrad_merge_bwd", grid=(T // tm, nj),
        in_specs=[pl.BlockSpec((tm, K), lambda i, j: (i, 0)), pl.BlockSpec((tn, K), lambda i, j: (j, 0))]
        + _gate_windows(tm, tn, D) + [blk, blk],
        out_specs=[blk, blk, _window(tm, 2 * D, lambda i, j: (i * tm, OFF_GA))],
        out_shape=[o, o, jax.ShapeDtypeStruct((T, proj.shape[1]), BF16)],
        compiler_params=_params(("parallel", "arbitrary")),
    )(dy, wout, proj, proj, ua, uh)


KV_PAIR = 2
PAIR_HEADS = KV_PAIR * GROUP


def _attn_mask(n):
    qi = lax.broadcasted_iota(jnp.int32, (WINDOW, 2 * WINDOW), 0)
    si = lax.broadcasted_iota(jnp.int32, (WINDOW, 2 * WINDOW), 1)
    dist = qi + WINDOW - si
    return (dist >= 0) & (dist < WINDOW) & ((si >= WINDOW) | (n > 0))


def _first_key_column(shape):
    return lax.broadcasted_iota(jnp.int32, shape, 1) == 0


def _attn_group_fwd(mask, q, k, v, ag, bias, sink):
    mask = jnp.concatenate([mask] * GROUP, axis=0)
    s = lax.dot_general(q.astype(BF16), k.astype(BF16), NT_DIMS, preferred_element_type=F32) * (HEAD_DIM ** -0.5)
    s = jnp.where(mask, s + bias, -1e30)
    m = jnp.maximum(jnp.max(s, axis=-1, keepdims=True), sink)
    p = jnp.exp(s - m)
    e_sink = jnp.exp(sink - m)
    rden = 1.0 / (jnp.sum(p, axis=-1, keepdims=True) + e_sink)
    probs = (p * rden).astype(BF16)
    o = jnp.dot(probs, v.astype(BF16), preferred_element_type=F32)
    kept = jnp.where(_first_key_column(probs.shape), (e_sink * rden).astype(BF16), probs)
    return o * jax.nn.silu(ag), kept


def _attn_group_bwd(q, k, v, ag, kept, dout):
    first = _first_key_column(kept.shape)
    p_sink = kept[:, 0:1].astype(F32)
    probs = jnp.where(first, jnp.zeros_like(kept), kept)
    qb, kb, vb = q.astype(BF16), k.astype(BF16), v.astype(BF16)
    o = jnp.dot(probs, vb, preferred_element_type=F32)
    sg = jax.nn.sigmoid(ag)
    d_o = dout * (ag * sg)
    dag = dout * o * (sg * (1.0 + ag * (1.0 - sg)))
    d_row = jnp.sum(d_o * o, axis=-1, keepdims=True)
    d_ob = d_o.astype(BF16)
    ds = probs.astype(F32) * (lax.dot_general(d_ob, vb, NT_DIMS, preferred_element_type=F32) - d_row)
    dsb = ds.astype(BF16)
    dq = jnp.dot(dsb, kb, preferred_element_type=F32) * (HEAD_DIM ** -0.5)
    dk = lax.dot_general(dsb, qb, TN_DIMS, preferred_element_type=F32) * (HEAD_DIM ** -0.5)
    dv = lax.dot_general(probs, d_ob, TN_DIMS, preferred_element_type=F32)
    return dq, dk, dv, dag, ds, -(p_sink * d_row)


def _attn_specs(B_loc, nb, order):
    qw = PAIR_HEADS * HEAD_DIM
    kw = KV_PAIR * HEAD_DIM

    def rows(g):
        b, p, n = order(*g)
        return b * nb + n

    def prev(g):
        b, p, n = order(*g)
        return b * nb + jnp.maximum(n - 1, 0)

    def pp(g):
        return order(*g)[1]

    q = pl.BlockSpec((WINDOW, qw), lambda *g: (rows(g), OFF_AQ // qw + pp(g)))
    kc = pl.BlockSpec((WINDOW, kw), lambda *g: (rows(g), OFF_AK // kw + pp(g)))
    kp = pl.BlockSpec((WINDOW, kw), lambda *g: (prev(g), OFF_AK // kw + pp(g)))
    vc = pl.BlockSpec((WINDOW, kw), lambda *g: (rows(g), OFF_AV // kw + pp(g)))
    vp = pl.BlockSpec((WINDOW, kw), lambda *g: (prev(g), OFF_AV // kw + pp(g)))
    ag = pl.BlockSpec((WINDOW, qw), lambda *g: (rows(g), OFF_AG // qw + pp(g)))
    bias = pl.BlockSpec((PAIR_HEADS, WINDOW, 2 * WINDOW), lambda *g: (pp(g), 0, 0))
    sink = pl.BlockSpec((PAIR_HEADS, 8, LANES), lambda *g: (pp(g), 0, 0))
    return [q, kc, kp, vc, vp, ag, bias, sink], rows, pp


def _attn_operands(q_ref, kc_ref, kp_ref, vc_ref, vp_ref, ag_ref, bias_ref, sink_ref, j):
    lo, hi = j * HEAD_DIM, (j + 1) * HEAD_DIM
    k = jnp.concatenate([kp_ref[:, lo:hi], kc_ref[:, lo:hi]], axis=0)
    v = jnp.concatenate([vp_ref[:, lo:hi], vc_ref[:, lo:hi]], axis=0)
    heads = [j * GROUP + g for g in range(GROUP)]
    q = jnp.concatenate([q_ref[:, h * HEAD_DIM:(h + 1) * HEAD_DIM] for h in heads], axis=0)
    ag = jnp.concatenate([ag_ref[:, h * HEAD_DIM:(h + 1) * HEAD_DIM] for h in heads], axis=0)
    bias = jnp.concatenate([bias_ref[h] for h in heads], axis=0)
    sink = jnp.concatenate([jnp.broadcast_to(sink_ref[h, 0:1, 0:1], (WINDOW, 1)) for h in heads], axis=0)
    return q, k, v, ag, bias, sink


def _attn_fwd(proj, bias_tab, sinks_b, B_loc, S):
    T = B_loc * S
    nb = S // WINDOW
    n_pairs = ATTN_KV_HEADS // KV_PAIR
    in_specs, rows, pp = _attn_specs(B_loc, nb, lambda b, p, n: (b, p, n))

    def body(q_ref, kc_ref, kp_ref, vc_ref, vp_ref, ag_ref, bias_ref, sink_ref, ya_ref, p_ref):
        mask = _attn_mask(pl.program_id(2))
        for j in range(KV_PAIR):
            out, kept = _attn_group_fwd(
                mask, *_attn_operands(q_ref, kc_ref, kp_ref, vc_ref, vp_ref, ag_ref, bias_ref, sink_ref, j))
            for g in range(GROUP):
                h = j * GROUP + g
                blk = slice(g * WINDOW, (g + 1) * WINDOW)
                ya_ref[:, h * HEAD_DIM:(h + 1) * HEAD_DIM] = out[blk].astype(BF16)
                p_ref[:, h * 2 * WINDOW:(h + 1) * 2 * WINDOW] = kept[blk]

    return pl.pallas_call(
        body, name="attn_fwd", grid=(B_loc, n_pairs, nb), in_specs=in_specs,
        out_specs=[pl.BlockSpec((WINDOW, PAIR_HEADS * HEAD_DIM), lambda *g: (rows(g), pp(g))),
                   pl.BlockSpec((WINDOW, PAIR_HEADS * 2 * WINDOW), lambda *g: (rows(g), pp(g)))],
        out_shape=[jax.ShapeDtypeStruct((T, ATTN_WIDTH), BF16),
                   jax.ShapeDtypeStruct((T, ATTN_HEADS * 2 * WINDOW), BF16)],
        compiler_params=_params(("parallel", "parallel", "parallel")),
    )(proj, proj, proj, proj, proj, proj, bias_tab, sinks_b)


def _attn_bwd(proj, probs, dya, dproj, B_loc, S):
    nb = S // WINDOW
    n_pairs = ATTN_KV_HEADS // KV_PAIR
    qw, kw = PAIR_HEADS * HEAD_DIM, KV_PAIR * HEAD_DIM

    def at(off, width, back=0):
        return lambda b, i, p: ((b * nb + jnp.maximum(nb - 1 - i - back, 0)) * WINDOW,
                                pl.multiple_of(off + p * width, LANES))

    in_specs = [_window(WINDOW, qw, at(OFF_AQ, qw)), _window(WINDOW, kw, at(OFF_AK, kw)),
                _window(WINDOW, kw, at(OFF_AK, kw, 1)), _window(WINDOW, kw, at(OFF_AV, kw)),
                _window(WINDOW, kw, at(OFF_AV, kw, 1)), _window(WINDOW, qw, at(OFF_AG, qw)),
                pl.BlockSpec((WINDOW, PAIR_HEADS * 2 * WINDOW), lambda b, i, p: (b * nb + nb - 1 - i, p)),
                pl.BlockSpec((WINDOW, qw), lambda b, i, p: (b * nb + nb - 1 - i, p)), ANY]

    def body(q_ref, kc_ref, kp_ref, vc_ref, vp_ref, ag_ref, p_ref, dya_ref, dproj_in,
             dproj_ref, dbias_ref, dsink_ref, dkc_ref, dvc_ref):
        b, i, p = pl.program_id(0), pl.program_id(1), pl.program_id(2)

        @pl.when((b == 0) & (i == 0) & (p == 0))
        def _():
            dbias_ref[...] = jnp.zeros_like(dbias_ref)
            dsink_ref[...] = jnp.zeros_like(dsink_ref)

        @pl.when(i == 0)
        def _():
            dkc_ref[p] = jnp.zeros((WINDOW, kw), F32)
            dvc_ref[p] = jnp.zeros((WINDOW, kw), F32)

        dk_carry, dv_carry = dkc_ref[p], dvc_ref[p]
        dqs, dags, dbiases, dsinks, dks, dvs = [], [], [], [], [], []
        for j in range(KV_PAIR):
            heads = [j * GROUP + g for g in range(GROUP)]
            lo, hi = j * HEAD_DIM, (j + 1) * HEAD_DIM
            stack = lambda parts: jnp.concatenate(parts, axis=0)
            k = stack([kp_ref[:, lo:hi], kc_ref[:, lo:hi]])
            v = stack([vp_ref[:, lo:hi], vc_ref[:, lo:hi]])
            q = stack([q_ref[:, h * HEAD_DIM:(h + 1) * HEAD_DIM] for h in heads])
            ag = stack([ag_ref[:, h * HEAD_DIM:(h + 1) * HEAD_DIM] for h in heads])
            dout = stack([dya_ref[:, h * HEAD_DIM:(h + 1) * HEAD_DIM].astype(F32) for h in heads])
            kept = stack([p_ref[:, h * 2 * WINDOW:(h + 1) * 2 * WINDOW] for h in heads])
            dq, dk, dv, dag, dbias, dsink = _attn_group_bwd(q, k, v, ag, kept, dout)
            dks.append((dk[WINDOW:] + dk_carry[:, lo:hi]).astype(BF16))
            dvs.append((dv[WINDOW:] + dv_carry[:, lo:hi]).astype(BF16))
            dkc_ref[p, :, lo:hi] = dk[:WINDOW]
            dvc_ref[p, :, lo:hi] = dv[:WINDOW]
            for g in range(GROUP):
                blk = slice(g * WINDOW, (g + 1) * WINDOW)
                dqs.append(dq[blk].astype(BF16))
                dags.append(dag[blk].astype(BF16))
                dbiases.append(dbias[blk])
                dsinks.append(jnp.broadcast_to(jnp.sum(dsink[blk], axis=0, keepdims=True), (8, LANES)))

        for pair in range(n_pairs):
            @pl.when(p == pair)
            def _():
                for j in range(KV_PAIR):
                    col = (pair * KV_PAIR + j) * HEAD_DIM
                    dproj_ref[:, OFF_AK + col:OFF_AK + col + HEAD_DIM] = dks[j]
                    dproj_ref[:, OFF_AV + col:OFF_AV + col + HEAD_DIM] = dvs[j]
                for hh in range(PAIR_HEADS):
                    h = pair * PAIR_HEADS + hh
                    dproj_ref[:, OFF_AQ + h * HEAD_DIM:OFF_AQ + (h + 1) * HEAD_DIM] = dqs[hh]
                    dproj_ref[:, OFF_AG + h * HEAD_DIM:OFF_AG + (h + 1) * HEAD_DIM] = dags[hh]
                    dbias_ref[h] += dbiases[hh]
                    dsink_ref[h] += dsinks[hh]

    return pl.pallas_call(
        body, name="attn_bwd", grid=(B_loc, nb, n_pairs), in_specs=in_specs,
        out_specs=[_window(WINDOW, OFF_HQ, lambda b, i, p: ((b * nb + nb - 1 - i) * WINDOW, 0)),
                   pl.BlockSpec((ATTN_HEADS, WINDOW, 2 * WINDOW), lambda b, i, p: (0, 0, 0)),
                   pl.BlockSpec((ATTN_HEADS, 8, LANES), lambda b, i, p: (0, 0, 0))],
        out_shape=[jax.ShapeDtypeStruct(dproj.shape, dproj.dtype),
                   jax.ShapeDtypeStruct((ATTN_HEADS, WINDOW, 2 * WINDOW), F32),
                   jax.ShapeDtypeStruct((ATTN_HEADS, 8, LANES), F32)],
        input_output_aliases={8: 0},
        scratch_shapes=[pltpu.VMEM((n_pairs, WINDOW, kw), F32), pltpu.VMEM((n_pairs, WINDOW, kw), F32)],
        compiler_params=_params(("arbitrary", "arbitrary", "arbitrary")),
    )(proj, proj, proj, proj, proj, proj, probs, dya, dproj)


class _HgrnPre:
    def __init__(self, fr, qr, lb, g_scr):
        t = lax.broadcasted_iota(jnp.int32, (CHUNK, CHUNK), 0)
        s = lax.broadcasted_iota(jnp.int32, (CHUNK, CHUNK), 1)
        self.sg = jax.nn.sigmoid(fr)
        self.f = lb + (1.0 - lb) * self.sg
        g = jnp.dot((t >= s).astype(F32), jnp.log(self.f), precision=lax.Precision.HIGHEST, preferred_element_type=F32)
        g_scr[...] = g
        self.g = g
        self.row = lax.broadcasted_iota(jnp.int32, g.shape, 0)
        self.refs = [jnp.zeros((1, g.shape[1]), F32)] + [g_scr[pl.ds(i * SUB - 1, 1), :] for i in range(1, NSUB)]
        self.gend = g_scr[pl.ds(CHUNK - 1, 1), :]
        refrow = jnp.zeros_like(g)
        for i in range(1, NSUB):
            refrow = jnp.where(self.row >= i * SUB, self.refs[i], refrow)
        self.sigq = jax.nn.sigmoid(qr)
        self.qs = qr * self.sigq
        self.k = 1.0 - self.f
        self.eg = jnp.exp(g)
        self.eqd = jnp.exp(g - refrow)
        self.ekd = [jnp.exp(jnp.where(self.row < (i + 1) * SUB, self.refs[i] - g, 0.0)) for i in range(NSUB)]
        self.ekdec = jnp.exp(self.gend - g)
        self.qg = self.qs * self.eg
        self.qd = self.qs * self.eqd
        self.kd = [self.k * e for e in self.ekd]
        self.kdec = self.k * self.ekdec
        self.egend = jnp.exp(self.gend)


def _hgrn_pair_mask():
    t = lax.broadcasted_iota(jnp.int32, (CHUNK, NSUB * CHUNK), 0)
    col = lax.broadcasted_iota(jnp.int32, (CHUNK, NSUB * CHUNK), 1)
    return ((t // SUB) == (col // CHUNK)) & ((col % CHUNK) <= t)


def _hgrn_head_out(p, lanes, state_t, v, mask):
    qg, qd = p.qg[:, lanes], p.qd[:, lanes]
    kall = jnp.concatenate([kd[:, lanes] for kd in p.kd], axis=0)
    vst = jnp.concatenate([v] * NSUB, axis=0)
    am = jnp.where(mask, _bdot(qd, kall, NT_DIMS), 0.0)
    o = _bdot(qg, state_t, NT_DIMS) + _bdot(am, vst)
    return o, (qg, qd, kall, am, vst)


def _hgrn_fwd(proj, lb, gain, B_loc, S):
    T = B_loc * S
    nc = S // CHUNK
    nh = HGRN_HEADS

    def at(off):
        return lambda b, n: ((b * nc + n) * CHUNK, off)

    vec = pl.BlockSpec((1, HGRN_WIDTH), lambda b, n: (0, 0))

    def body(q_ref, f_ref, v_ref, hg_ref, lb_ref, gain_ref, yh_ref, st_ref, state_scr, g_scr):
        @pl.when(pl.program_id(1) == 0)
        def _():
            state_scr[...] = jnp.zeros_like(state_scr)

        p = _HgrnPre(f_ref[...], q_ref[...], lb_ref[...], g_scr)
        v = v_ref[...]
        gate = gain_ref[...] * jax.nn.silu(hg_ref[...])
        mask = _hgrn_pair_mask()
        for hd in range(nh):
            lanes = slice(hd * HGRN_DIM, (hd + 1) * HGRN_DIM)
            st = state_scr[hd]
            st_ref[hd] = st
            o, _ = _hgrn_head_out(p, lanes, st, v[:, lanes], mask)
            rs = lax.rsqrt(jnp.mean(o * o, axis=-1, keepdims=True) + NORM_EPS)
            yh_ref[:, lanes] = (o * rs * gate[:, lanes]).astype(BF16)
            state_scr[hd] = st * p.egend[:, lanes] + _bdot(v[:, lanes], p.kdec[:, lanes], TN_DIMS)

    return pl.pallas_call(
        body, name="hgrn_fwd", grid=(B_loc, nc),
        in_specs=[_window(CHUNK, HGRN_WIDTH, at(OFF_HQ)), _window(CHUNK, HGRN_WIDTH, at(OFF_HF)),
                  _window(CHUNK, HGRN_WIDTH, at(OFF_HI)), _window(CHUNK, HGRN_WIDTH, at(OFF_HG)), vec, vec],
        out_specs=[pl.BlockSpec((CHUNK, HGRN_WIDTH), lambda b, n: (b * nc + n, 0)),
                   pl.BlockSpec((None, None, nh, HGRN_DIM, HGRN_DIM), lambda b, n: (b, n, 0, 0, 0))],
        out_shape=[jax.ShapeDtypeStruct((T, HGRN_WIDTH), BF16),
                   jax.ShapeDtypeStruct((B_loc, nc, nh, HGRN_DIM, HGRN_DIM), F32)],
        scratch_shapes=[pltpu.VMEM((nh, HGRN_DIM, HGRN_DIM), F32), pltpu.VMEM((CHUNK, HGRN_WIDTH), F32)],
        compiler_params=_params(("parallel", "arbitrary")),
    )(proj, proj, proj, proj, lb, gain)


def _hgrn_bwd(proj, states, lb, gain, dyh, dproj, B_loc, S):
    nc = S // CHUNK
    nh = HGRN_HEADS

    def at(off):
        return lambda b, i: ((b * nc + nc - 1 - i) * CHUNK, off)

    vec = pl.BlockSpec((1, HGRN_WIDTH), lambda b, i: (0, 0))
    in_specs = [_window(CHUNK, HGRN_WIDTH, at(OFF_HQ)), _window(CHUNK, HGRN_WIDTH, at(OFF_HF)),
                _window(CHUNK, HGRN_WIDTH, at(OFF_HI)), _window(CHUNK, HGRN_WIDTH, at(OFF_HG)), vec, vec,
                pl.BlockSpec((None, None, nh, HGRN_DIM, HGRN_DIM), lambda b, i: (b, nc - 1 - i, 0, 0, 0)),
                pl.BlockSpec((CHUNK, HGRN_WIDTH), lambda b, i: (b * nc + nc - 1 - i, 0)), ANY]
    acc_spec = pl.BlockSpec((None, 1, HGRN_WIDTH), lambda b, i: (b, 0, 0))

    def body(q_ref, f_ref, v_ref, hg_ref, lb_ref, gain_ref, st_ref, dyh_ref, dproj_in,
             dproj_ref, dlb_ref, dgain_ref, dstate_scr, g_scr, dg_scr):
        dq_ref, df_ref, dv_ref, dhg_ref = [dproj_ref.at[:, pl.ds(k * HGRN_WIDTH, HGRN_WIDTH)] for k in range(4)]

        @pl.when(pl.program_id(1) == 0)
        def _():
            dstate_scr[...] = jnp.zeros_like(dstate_scr)
            dlb_ref[...] = jnp.zeros_like(dlb_ref)
            dgain_ref[...] = jnp.zeros_like(dgain_ref)

        qr, lb, gain, hg, v = q_ref[...], lb_ref[...], gain_ref[...], hg_ref[...], v_ref[...]
        p = _HgrnPre(f_ref[...], qr, lb, g_scr)
        sgh = jax.nn.sigmoid(hg)
        sil = hg * sgh
        dy = dyh_ref[...].astype(F32)
        mask = _hgrn_pair_mask()
        dqg, dqd, dkdec, dv, dhg, dgend, dgain = [], [], [], [], [], [], []
        dkd = [[] for _ in range(NSUB)]
        heads = [slice(hd * HGRN_DIM, (hd + 1) * HGRN_DIM) for hd in range(nh)]
        sts = [st_ref[hd] for hd in range(nh)]
        dnews = [dstate_scr[hd] for hd in range(nh)]
        fwd = [_hgrn_head_out(p, lanes, st, v[:, lanes], mask) for lanes, st in zip(heads, sts)]
        for lanes, st, dnew in zip(heads, sts, dnews):
            dkdec_h = _bdot(v[:, lanes], dnew)
            dkdec.append(dkdec_h)
            dgend.append(jnp.sum(dkdec_h * p.kdec[:, lanes], axis=0, keepdims=True)
                         + jnp.sum(dnew * st, axis=0, keepdims=True) * p.egend[:, lanes])
        dos = []
        gate_grad = sgh * (1.0 + hg * (1.0 - sgh))
        for lanes, (o, _) in zip(heads, fwd):
            rs = lax.rsqrt(jnp.mean(o * o, axis=-1, keepdims=True) + NORM_EPS)
            n = o * rs
            dyn = dy[:, lanes] * n
            dgain.append(jnp.sum(dyn * sil[:, lanes], axis=0, keepdims=True))
            dhg.append(dyn * gain[:, lanes] * gate_grad[:, lanes])
            dn = dy[:, lanes] * gain[:, lanes] * sil[:, lanes]
            dos.append(rs * (dn - n * jnp.mean(dn * n, axis=-1, keepdims=True)))
        drs = []
        for hd, (lanes, st, dnew, do, (_, (qg, qd, kall, am, vst))) in enumerate(zip(heads, sts, dnews, dos, fwd)):
            dqg.append(_bdot(do, st))
            dstate_scr[hd] = _bdot(do, qg, TN_DIMS) + dnew * p.egend[:, lanes]
            drs.append(jnp.where(mask, _bdot(do, vst, NT_DIMS), 0.0))
            dvst = _bdot(am, do, TN_DIMS)
            dv.append(sum(dvst[i * CHUNK:(i + 1) * CHUNK] for i in range(NSUB)) + _bdot(p.kdec[:, lanes], dnew, NT_DIMS))
        for dr, (_, (qg, qd, kall, am, vst)) in zip(drs, fwd):
            dqd.append(_bdot(dr, kall))
            dkall = _bdot(dr, qd, TN_DIMS)
            for i in range(NSUB):
                dkd[i].append(dkall[i * CHUNK:(i + 1) * CHUNK])

        wide = lambda parts: jnp.concatenate(parts, axis=1)
        dqg, dqd, dkdec = wide(dqg), wide(dqd), wide(dkdec)
        t2 = dqd * p.qd
        dg = dqg * p.qg + t2 - dkdec * p.kdec
        dk = dkdec * p.ekdec
        dg_scr[...] = jnp.zeros_like(dg_scr)
        for i in range(NSUB):
            dkd_i = wide(dkd[i])
            tk = jnp.where(p.row < (i + 1) * SUB, dkd_i * p.kd[i], 0.0)
            dg = dg - tk
            dk = dk + dkd_i * p.ekd[i]
            if i >= 1:
                in_blk = (p.row >= i * SUB) & (p.row < (i + 1) * SUB)
                dg_scr[pl.ds(i * SUB - 1, 1), :] = (jnp.sum(tk, axis=0, keepdims=True)
                                                    - jnp.sum(jnp.where(in_blk, t2, 0.0), axis=0, keepdims=True))
        dg_scr[pl.ds(CHUNK - 1, 1), :] = wide(dgend)
        t = lax.broadcasted_iota(jnp.int32, (CHUNK, CHUNK), 0)
        s = lax.broadcasted_iota(jnp.int32, (CHUNK, CHUNK), 1)
        dlogf = jnp.dot((t <= s).astype(F32), dg + dg_scr[...], precision=lax.Precision.HIGHEST, preferred_element_type=F32)
        df = dlogf / p.f - dk
        df_ref[...] = (df * (1.0 - lb) * p.sg * (1.0 - p.sg)).astype(BF16)
        dlb_ref[...] += jnp.sum(df * (1.0 - p.sg), axis=0, keepdims=True)
        dq_ref[...] = ((dqg * p.eg + dqd * p.eqd) * p.sigq * (1.0 + qr * (1.0 - p.sigq))).astype(BF16)
        dv_ref[...] = wide(dv).astype(BF16)
        dhg_ref[...] = wide(dhg).astype(BF16)
        dgain_ref[...] += wide(dgain)

    acc = jax.ShapeDtypeStruct((B_loc, 1, HGRN_WIDTH), F32)
    return pl.pallas_call(
        body, name="hgrn_bwd", grid=(B_loc, nc), in_specs=in_specs,
        out_specs=[_window(CHUNK, 4 * HGRN_WIDTH, at(OFF_HQ)), acc_spec, acc_spec],
        out_shape=[jax.ShapeDtypeStruct(dproj.shape, dproj.dtype), acc, acc],
        input_output_aliases={8: 0},
        scratch_shapes=[pltpu.VMEM((nh, HGRN_DIM, HGRN_DIM), F32), pltpu.VMEM((CHUNK, HGRN_WIDTH), F32),
                        pltpu.VMEM((CHUNK, HGRN_WIDTH), F32)],
        compiler_params=_params(("parallel", "arbitrary")),
    )(proj, proj, proj, proj, lb, gain, states, dyh, dproj)


def _adamw(w, g, m, v, name):
    R, C = w.shape
    tr = _pick(R, (128, 64, 32, 16, 8)) if C > 1024 else _pick(R, (512, 256, 128, 64, 32, 16, 8))

    def body(w_ref, g_ref, m_ref, v_ref, d_ref, nm_ref, nv_ref, g_out_ref):
        gv = g_ref[...]
        g_out_ref[...] = gv
        nm = ADAM_B1 * m_ref[...] + (1.0 - ADAM_B1) * gv
        nv = ADAM_B2 * v_ref[...] + (1.0 - ADAM_B2) * (gv * gv)
        m_hat = nm / (1.0 - ADAM_B1 ** ADAM_STEP)
        v_hat = nv / (1.0 - ADAM_B2 ** ADAM_STEP)
        d_ref[...] = -ADAM_LR * (m_hat / (jnp.sqrt(v_hat) + ADAM_EPS) + ADAM_WD * w_ref[...])
        nm_ref[...] = nm
        nv_ref[...] = nv

    blk = pl.BlockSpec((tr, C), lambda i: (i, 0))
    o = jax.ShapeDtypeStruct((R, C), F32)
    return pl.pallas_call(
        body, name=name, grid=(R // tr,), in_specs=[blk] * 4, out_specs=[blk] * 4, out_shape=[o, o, o, o],
        compiler_params=_params(("parallel",)),
    )(w, g, m, v)


ANY = pl.BlockSpec(memory_space=pl.ANY)
VMEM_SPEC = pl.BlockSpec(memory_space=pltpu.VMEM)


def _place():
    x, y, c = lax.axis_index("x"), lax.axis_index("y"), lax.axis_index("c")
    other_chips = [(1 - x, y), (x, 1 - y), (1 - x, 1 - y)]
    return x, y, c, other_chips


def _cast_into_full(w, ax, s_arr, name):
    R, C = w.shape
    tr = _pick(R, (256, 128))
    nr = R // tr

    def body(s_ref, w_ref, o_ref):
        o_ref[...] = w_ref[...].astype(BF16)

    if ax == 1:
        shape, o_map = (R, N_CHIPS * C), lambda i, s: (i, s[0])
    else:
        shape, o_map = (N_CHIPS * R, C), lambda i, s: (s[0] * nr + i, 0)
    return pl.pallas_call(
        body, name=name,
        grid_spec=pltpu.PrefetchScalarGridSpec(
            num_scalar_prefetch=1, grid=(nr,), in_specs=[pl.BlockSpec((tr, C), lambda i, s: (i, 0))],
            out_specs=pl.BlockSpec((tr, C), o_map)),
        out_shape=jax.ShapeDtypeStruct(shape, BF16),
        compiler_params=_params(("parallel",)),
    )(s_arr, w)


class _Gather:
    def __init__(self, fulls, shard_shapes, axes, tag):
        self.shapes, self.axes, self.tag, self.nw = shard_shapes, axes, tag, len(fulls)
        self.fulls, self.sems, self.token = list(fulls), {}, None

    def start(self, peers, after=None):
        nw, np_ = self.nw, len(peers)

        def body(*refs):
            ins, sems = refs[:nw], refs[nw + (after is not None):nw + (after is not None) + 2 * np_]
            for k, j in enumerate(peers):
                for cp in self._peer_copies(ins, sems[2 * k], sems[2 * k + 1], j):
                    cp.start()
            refs[-1][...] = jnp.zeros_like(refs[-1])

        out = pl.pallas_call(
            body, name="gather_start_%s_%s" % (self.tag, "".join(map(str, peers))),
            out_shape=(*[pltpu.SemaphoreType.DMA((nw,))] * (2 * np_),
                       *[pltpu.HBM(f.shape, f.dtype) for f in self.fulls], jax.ShapeDtypeStruct((8, LANES), F32)),
            in_specs=[HBM_SPEC] * nw + ([] if after is None else [ANY]),
            out_specs=(*[SEM_SPEC] * (2 * np_), *[HBM_SPEC] * nw, VMEM_SPEC),
            input_output_aliases={k: 2 * np_ + k for k in range(nw)},
            compiler_params=pltpu.CompilerParams(has_side_effects=DATAFLOW),
        )(*[pltpu.with_memory_space_constraint(f, pltpu.HBM) for f in self.fulls], *(() if after is None else (after,)))
        for k, j in enumerate(peers):
            self.sems[j] = (out[2 * k], out[2 * k + 1])
        self.fulls = list(out[2 * np_:2 * np_ + nw])
        self.token = out[-1]

    def _region(self, ref, i, t, half):
        R, C = self.shapes[i]
        hr = R // 2
        if self.axes[i] == 1:
            return ref.at[pl.ds(half * hr, hr), pl.ds(pl.multiple_of(t * C, LANES), C)]
        return ref.at[pl.ds(t * R + half * hr, hr), :]

    def _peer_copies(self, refs, send_sems, recv_sems, j):
        x, y, c, chips = _place()
        s = 2 * x + y
        return [pltpu.make_async_remote_copy(
            src_ref=self._region(refs[i], i, s, c), dst_ref=self._region(refs[i], i, s, c), send_sem=send_sems.at[i],
            recv_sem=recv_sems.at[i], device_id=(*chips[j], c), device_id_type=MESH) for i in range(self.nw)]

    def wait(self, peers, after):
        nw, np_ = self.nw, len(peers)

        def body(*refs):
            ins, sems = refs[:nw], refs[nw:nw + 2 * np_]
            for k, j in enumerate(peers):
                for cp in self._peer_copies(ins, sems[2 * k], sems[2 * k + 1], j):
                    cp.wait_send()
                    cp.wait_recv()

        sem_args = [s for j in peers for s in self.sems[j]]
        out = pl.pallas_call(
            body, name="gather_wait_%s_%s" % (self.tag, "".join(map(str, peers))),
            out_shape=tuple(pltpu.HBM(f.shape, f.dtype) for f in self.fulls),
            in_specs=[HBM_SPEC] * nw + [SEM_SPEC] * (2 * np_) + [ANY], out_specs=tuple([HBM_SPEC] * nw),
            input_output_aliases={k: k for k in range(nw)},
            compiler_params=pltpu.CompilerParams(has_side_effects=DATAFLOW),
        )(*self.fulls, *sem_args, after)
        self.fulls = list(out)

    def forward(self, peers):
        nw, np_ = self.nw, len(peers)

        def body(*refs):
            ins, outs = refs[:nw], refs[nw:2 * nw]
            send_sems, recv_sems = refs[2 * nw:]
            x, y, c, chips = _place()
            cps = []
            for i in range(nw):
                for k, j in enumerate(peers):
                    t = 2 * chips[j][0] + chips[j][1]
                    cp = pltpu.make_async_remote_copy(
                        src_ref=self._region(ins[i], i, t, c), dst_ref=self._region(outs[i], i, t, c),
                        send_sem=send_sems.at[i * np_ + k], recv_sem=recv_sems.at[i * np_ + k],
                        device_id=(x, y, 1 - c), device_id_type=MESH)
                    cp.start()
                    cps.append(cp)
            for cp in cps:
                cp.wait()

        out = pl.pallas_call(
            body, name="gather_forward_%s_%s" % (self.tag, "".join(map(str, peers))),
            in_specs=[ANY] * nw, out_specs=[ANY] * nw,
            out_shape=[jax.ShapeDtypeStruct(f.shape, f.dtype) for f in self.fulls],
            input_output_aliases={i: i for i in range(nw)},
            scratch_shapes=[pltpu.SemaphoreType.DMA((nw * np_,)), pltpu.SemaphoreType.DMA((nw * np_,))],
        )(*self.fulls)
        self.fulls = list(out)


def _matmul_slab(a, wfull, slab_arr, prev, after, name):
    M, K = a.shape
    N = wfull.shape[1]
    nslab = N // N_CHIPS
    tn = _pick(nslab, (2688, 896, 512, 384, 256, 128))
    tm = _pick(M, (512, 256, 128) if tn > 1024 else (1024, 512, 256, 128))
    per = nslab // tn
    extra = [e for e in (prev, after) if e is not None]

    def body(slab_ref, a_ref, b_ref, *rest):
        rest[len(extra)][...] = jnp.dot(a_ref[...], b_ref[...], preferred_element_type=F32)

    return pl.pallas_call(
        body, name=name,
        grid_spec=pltpu.PrefetchScalarGridSpec(
            num_scalar_prefetch=1, grid=(M // tm, per),
            in_specs=[pl.BlockSpec((tm, K), lambda i, j, sl: (i, 0)),
                      pl.BlockSpec((K, tn), lambda i, j, sl: (0, sl[0] * per + j))] + [ANY] * len(extra),
            out_specs=pl.BlockSpec((tm, tn), lambda i, j, sl: (i, sl[0] * per + j))),
        out_shape=jax.ShapeDtypeStruct((M, N), F32),
        input_output_aliases={} if prev is None else {3: 0},
        compiler_params=_params(("parallel", "arbitrary")),
    )(slab_arr, a, wfull, *extra)


def _exchange_sibling_halves(gs, name):
    nw = len(gs)

    def body(*refs):
        ins, outs = refs[:nw], refs[nw:2 * nw]
        send_sems, recv_sems = refs[2 * nw:]
        x, y, c, _ = _place()
        cps = []
        for i in range(nw):
            cp = pltpu.make_async_remote_copy(src_ref=ins[i].at[:, 1 - c], dst_ref=outs[i], send_sem=send_sems.at[i],
                                              recv_sem=recv_sems.at[i], device_id=(x, y, 1 - c), device_id_type=MESH)
            cp.start()
            cps.append(cp)
        for cp in cps:
            cp.wait()

    return pl.pallas_call(
        body, name=name, in_specs=[ANY] * nw, out_specs=[ANY] * nw,
        out_shape=[jax.ShapeDtypeStruct((g.shape[0],) + g.shape[2:], g.dtype) for g in gs],
        scratch_shapes=[pltpu.SemaphoreType.DMA((nw,)), pltpu.SemaphoreType.DMA((nw,))],
    )(*gs)


HBM_SPEC = pl.BlockSpec(memory_space=pltpu.HBM)
SEM_SPEC = pl.BlockSpec(memory_space=pltpu.SEMAPHORE)
DATAFLOW = pltpu.SideEffectType.DATAFLOW_SIDE_EFFECTING


def _chip_copies(ins, lands, send_sems, recv_sems):
    x, y, c, chips = _place()
    return [pltpu.make_async_remote_copy(
        src_ref=ins[i].at[2 * chip[0] + chip[1]], dst_ref=lands[i].at[j], send_sem=send_sems.at[i * 3 + j],
        recv_sem=recv_sems.at[i * 3 + j], device_id=(*chip, c), device_id_type=MESH)
        for i in range(len(ins)) for j, chip in enumerate(chips)]


def _chips_send_start(ss, name):
    nw = len(ss)
    lands = [pltpu.with_memory_space_constraint(lax.empty((N_CHIPS - 1,) + s.shape[1:], s.dtype), pltpu.HBM) for s in ss]

    def body(*refs):
        ins, land_refs = refs[:nw], refs[nw:2 * nw]
        send_sems, recv_sems = refs[2 * nw], refs[2 * nw + 1]
        token = refs[-1]
        for cp in _chip_copies(ins, land_refs, send_sems, recv_sems):
            cp.start()
        token[...] = jnp.zeros_like(token)

    n = 3 * nw
    out = pl.pallas_call(
        body, name=name,
        out_shape=(pltpu.SemaphoreType.DMA((n,)), pltpu.SemaphoreType.DMA((n,)),
                   *[pltpu.HBM(s.shape, s.dtype) for s in ss], *[pltpu.HBM(l.shape, l.dtype) for l in lands],
                   jax.ShapeDtypeStruct((8, LANES), F32)),
        in_specs=[HBM_SPEC] * (2 * nw), out_specs=(SEM_SPEC, SEM_SPEC, *[HBM_SPEC] * (2 * nw), VMEM_SPEC),
        input_output_aliases={k: 2 + k for k in range(2 * nw)},
        compiler_params=pltpu.CompilerParams(has_side_effects=DATAFLOW),
    )(*[pltpu.with_memory_space_constraint(s, pltpu.HBM) for s in ss], *lands)
    return out[0], out[1], list(out[2:2 + nw]), list(out[2 + nw:2 + 2 * nw]), out[-1]


def _chips_send_wait(send_sems, recv_sems, ss, lands, after, name):
    nw = len(ss)

    def body(*refs):
        ins, land_refs = refs[:nw], refs[nw:2 * nw]
        s_sems, r_sems = refs[2 * nw], refs[2 * nw + 1]
        for cp in _chip_copies(ins, land_refs, s_sems, r_sems):
            cp.wait_send()
            cp.wait_recv()

    out = pl.pallas_call(
        body, name=name,
        out_shape=(*[pltpu.HBM(s.shape, s.dtype) for s in ss], *[pltpu.HBM(l.shape, l.dtype) for l in lands]),
        in_specs=[HBM_SPEC] * (2 * nw) + [SEM_SPEC, SEM_SPEC, ANY], out_specs=tuple([HBM_SPEC] * (2 * nw)),
        input_output_aliases={k: k for k in range(2 * nw)},
        compiler_params=pltpu.CompilerParams(has_side_effects=DATAFLOW),
    )(*ss, *lands, send_sems, recv_sems, after)
    return list(out[nw:])


def _sum_small(pack):
    rows = pack.shape[0]

    def body(pack_ref, sum_ref, all_ref, send_sems, recv_sems):
        x, y, c, _ = _place()
        me = 4 * x + 2 * y + c
        all_ref[me] = pack_ref[...]
        cps = []
        for k in range(1, N_DEV):
            to = (1 - x if k & 4 else x, 1 - y if k & 2 else y, 1 - c if k & 1 else c)
            cp = pltpu.make_async_remote_copy(
                src_ref=pack_ref, dst_ref=all_ref.at[me], send_sem=send_sems.at[k - 1],
                recv_sem=recv_sems.at[k - 1], device_id=to, device_id_type=MESH)
            cp.start()
            cps.append(cp)
        for cp in cps:
            cp.wait()
        total = all_ref[0]
        for d in range(1, N_DEV):
            total = total + all_ref[d]
        sum_ref[...] = total

    return pl.pallas_call(
        body, name="sum_small_grads", in_specs=[VMEM_SPEC], out_specs=VMEM_SPEC,
        out_shape=jax.ShapeDtypeStruct(pack.shape, F32),
        scratch_shapes=[pltpu.VMEM((N_DEV, rows, LANES), F32), pltpu.SemaphoreType.DMA((N_DEV - 1,)),
                        pltpu.SemaphoreType.DMA((N_DEV - 1,))],
    )(pack)


def _share_with_sibling(fs, name):
    nw = len(fs)

    def body(*refs):
        ins, outs = refs[:nw], refs[nw:2 * nw]
        send_sems, recv_sems = refs[2 * nw:]
        x, y, c, _ = _place()
        cps = []
        for i in range(nw):
            cp = pltpu.make_async_remote_copy(src_ref=ins[i].at[c], dst_ref=outs[i].at[c], send_sem=send_sems.at[i],
                                              recv_sem=recv_sems.at[i], device_id=(x, y, 1 - c), device_id_type=MESH)
            cp.start()
            cps.append(cp)
        for cp in cps:
            cp.wait()

    return pl.pallas_call(
        body, name=name, in_specs=[ANY] * nw, out_specs=[ANY] * nw,
        out_shape=[jax.ShapeDtypeStruct(f.shape, f.dtype) for f in fs],
        input_output_aliases={i: i for i in range(nw)},
        scratch_shapes=[pltpu.SemaphoreType.DMA((nw,)), pltpu.SemaphoreType.DMA((nw,))],
    )(*fs)


def _sum_sibling(g, land, where_arr, name):
    P, Q = g.shape[-2:]
    tp = _pick(P, (256, 128, 64))

    def body(w_ref, g_ref, l_ref, s_ref):
        s_ref[...] = (g_ref[...].astype(F32) + l_ref[...].astype(F32)).astype(BF16)

    if g.ndim == 4:
        g_spec = pl.BlockSpec((None, None, tp, Q), lambda j, r, w: (w[1 + j], w[0], r, 0))
    else:
        g_spec = pl.BlockSpec((None, tp, Q), lambda j, r, w: (w[1 + j], r, 0))
    slab = pl.BlockSpec((None, tp, Q), lambda j, r, w: (w[1 + j], r, 0))
    return pl.pallas_call(
        body, name=name,
        grid_spec=pltpu.PrefetchScalarGridSpec(
            num_scalar_prefetch=1, grid=(N_CHIPS - 1, P // tp), in_specs=[g_spec, slab], out_specs=slab),
        out_shape=jax.ShapeDtypeStruct((N_CHIPS, P, Q), BF16),
        compiler_params=_params(("parallel", "parallel")),
    )(where_arr, g, land)


def _sum_chips(g, land, recv, sc_arr, name):
    P, Q = g.shape[-2:]
    tp = _pick(P, (256, 128, 64))

    def body(sc_ref, g_ref, l_ref, r_ref, f_ref):
        acc = g_ref[...].astype(F32) + l_ref[...].astype(F32)
        for j in range(N_CHIPS - 1):
            acc = acc + r_ref[j].astype(F32)
        f_ref[...] = acc

    if g.ndim == 4:
        g_spec = pl.BlockSpec((None, None, tp, Q), lambda r, sc: (sc[0], sc[1], r, 0))
    else:
        g_spec = pl.BlockSpec((None, tp, Q), lambda r, sc: (sc[0], r, 0))
    return pl.pallas_call(
        body, name=name,
        grid_spec=pltpu.PrefetchScalarGridSpec(
            num_scalar_prefetch=1, grid=(P // tp,),
            in_specs=[g_spec,
                      pl.BlockSpec((None, tp, Q), lambda r, sc: (sc[0], r, 0)),
                      pl.BlockSpec((N_CHIPS - 1, tp, Q), lambda r, sc: (0, r, 0))],
            out_specs=pl.BlockSpec((None, tp, Q), lambda r, sc: (sc[1], r, 0))),
        out_shape=jax.ShapeDtypeStruct((2, P, Q), F32),
        compiler_params=_params(("parallel",)),
    )(sc_arr, g, land, recv)


class _Reduction:
    def __init__(self, names, pieces, lands, flight):
        self.names, self.pieces, self.lands = names, pieces, lands
        self.send_sems, self.recv_sems, self.sums, self.zones, self.token = flight


def _reduce_start(pieces, lands, names, where_arr):
    tag = names[0] if len(names) == 1 else "branches"
    sums = [_sum_sibling(g, l, where_arr, "sum_sibling_" + nm) for g, l, nm in zip(pieces, lands, names)]
    return _Reduction(names, pieces, lands, _chips_send_start(sums, "grads_to_chips_start_" + tag))


def _wgrad_half(a, b, half_arr, name, after=None):
    K, M = a.shape
    nslab = b.shape[1] // N_CHIPS
    tm = M // 2
    tn = _pick(nslab, (896, 512, 384, 256, 128))
    per = nslab // tn
    assert 4 * K * (tm + tn) <= MATMUL_OPERAND_BYTES

    def body(h_ref, a_ref, b_ref, *rest):
        rest[-1][...] = lax.dot_general(a_ref[...], b_ref[...], TN_DIMS, preferred_element_type=F32).astype(BF16)

    return pl.pallas_call(
        body, name=name,
        grid_spec=pltpu.PrefetchScalarGridSpec(
            num_scalar_prefetch=1, grid=(b.shape[1] // tn,),
            in_specs=[pl.BlockSpec((K, tm), lambda j, h: (0, h[0])), pl.BlockSpec((K, tn), lambda j, h: (0, j))]
            + ([] if after is None else [ANY]),
            out_specs=pl.BlockSpec((None, tm, tn), lambda j, h: (j // per, 0, j % per))),
        out_shape=jax.ShapeDtypeStruct((N_CHIPS, tm, nslab), BF16),
        compiler_params=_params(("parallel",)),
    )(*((half_arr, a, b) if after is None else (half_arr, a, b, after)))


def _sibling_send_start(g, name):
    zone = pltpu.with_memory_space_constraint(lax.empty(g.shape, g.dtype), pltpu.HBM)

    def body(g_ref, zone_ref, send_sem, recv_sem, g_thru, zone_thru, token):
        x, y, c, _ = _place()
        pltpu.make_async_remote_copy(src_ref=g_ref, dst_ref=zone_ref, send_sem=send_sem, recv_sem=recv_sem,
                                     device_id=(x, y, 1 - c), device_id_type=MESH).start()
        token[...] = jnp.zeros_like(token)

    return pl.pallas_call(
        body, name=name,
        out_shape=(pltpu.SemaphoreType.DMA(()), pltpu.SemaphoreType.DMA(()), pltpu.HBM(g.shape, g.dtype),
                   pltpu.HBM(g.shape, g.dtype), jax.ShapeDtypeStruct((8, LANES), F32)),
        in_specs=[HBM_SPEC, HBM_SPEC], out_specs=(SEM_SPEC, SEM_SPEC, HBM_SPEC, HBM_SPEC, VMEM_SPEC),
        input_output_aliases={0: 2, 1: 3},
        compiler_params=pltpu.CompilerParams(has_side_effects=DATAFLOW),
    )(pltpu.with_memory_space_constraint(g, pltpu.HBM), zone)


def _sibling_send_wait(send_sem, recv_sem, g, zone, after, name):
    def body(g_ref, zone_ref, s_sem, r_sem, after_ref, g_out, zone_out):
        x, y, c, _ = _place()
        cp = pltpu.make_async_remote_copy(src_ref=g_ref, dst_ref=zone_ref, send_sem=s_sem, recv_sem=r_sem,
                                          device_id=(x, y, 1 - c), device_id_type=MESH)
        cp.wait_send()
        cp.wait_recv()

    return pl.pallas_call(
        body, name=name, out_shape=(pltpu.HBM(g.shape, g.dtype), pltpu.HBM(zone.shape, zone.dtype)),
        in_specs=[HBM_SPEC, HBM_SPEC, SEM_SPEC, SEM_SPEC, ANY], out_specs=(HBM_SPEC, HBM_SPEC),
        input_output_aliases={0: 0, 1: 1},
        compiler_params=pltpu.CompilerParams(has_side_effects=DATAFLOW),
    )(g, zone, send_sem, recv_sem, after)[1]


def _reduce_finish(red, after, sc_arr):
    tag = red.names[0] if len(red.names) == 1 else "branches"
    recvs = _chips_send_wait(red.send_sems, red.recv_sems, red.sums, red.zones, after, "grads_to_chips_wait_" + tag)
    halves = [_sum_chips(g, l, r, sc_arr, "sum_chips_" + nm) for g, l, r, nm in zip(red.pieces, red.lands, recvs, red.names)]
    return _share_with_sibling(halves, "grads_share_sibling_" + tag)


def _t5_bucket(dist):
    max_exact = REL_BUCKETS // 2
    d = jnp.maximum(dist, 0)
    df = jnp.maximum(d, 1).astype(F32)
    large = max_exact + (jnp.log(df / max_exact) / math.log(REL_MAX_DIST / max_exact)
                         * (REL_BUCKETS - max_exact)).astype(jnp.int32)
    large = jnp.minimum(large, REL_BUCKETS - 1)
    return jnp.where(d < max_exact, d, large)


def _bucket_table():
    qi = jnp.arange(WINDOW)[:, None]
    si = jnp.arange(2 * WINDOW)[None, :]
    return _t5_bucket(qi + WINDOW - si)


TILE_WORDS = 8 * LANES


def _tile_rows(shape):
    return -(-math.prod(shape) // TILE_WORDS) * 8


def _rows_of(a):
    flat = a.reshape(-1).astype(F32)
    n = _tile_rows(a.shape) * LANES
    return jnp.pad(flat, (0, n - flat.shape[0])).reshape(-1, LANES)


def _pack_rows(parts):
    return jnp.concatenate([_rows_of(p) for p in parts], axis=0)


def _unpack_rows(packed, shapes):
    out, at = [], 0
    for shp in shapes:
        n, nr = math.prod(shp), _tile_rows(shp)
        out.append(packed[at:at + nr].reshape(-1)[:n].reshape(shp))
        at += nr
    return out


def kernel(x, norm_pre, w_in, rel_bias, attn_sinks, lb_logits, hgrn_norm, w_branch_attn, w_branch_hgrn, w_out, norm_post, loss_target, m_norm_pre, m_w_in, m_rel_bias, m_attn_sinks, m_lb_logits, m_hgrn_norm, m_w_branch_attn, m_w_branch_hgrn, m_w_out, m_norm_post, v_norm_pre, v_w_in, v_rel_bias, v_attn_sinks, v_lb_logits, v_hgrn_norm, v_w_branch_attn, v_w_branch_hgrn, v_w_out, v_norm_post):
    B_loc, S, D = x.shape
    T = B_loc * S
    x2 = x.reshape(T, D)
    tgt2 = loss_target.reshape(T, D)
    my_x, my_y, my_c = lax.axis_index("x"), lax.axis_index("y"), lax.axis_index("c")

    c_arr = jnp.reshape(my_c, (1,)).astype(jnp.int32)
    s_arr = jnp.reshape(2 * my_x + my_y, (1,)).astype(jnp.int32)
    sc_arr = jnp.concatenate([s_arr, c_arr])
    shard_ws = [w_in[0], w_branch_attn[0], w_branch_hgrn[0], w_out[0]]
    shard_axes = (1, 1, 1, 0)
    names = ["w_in", "w_branch_attn", "w_branch_hgrn", "w_out"]
    placed = [_cast_into_full(w, ax, s_arr, "cast_" + nm) for w, ax, nm in zip(shard_ws, shard_axes, names)]
    peer_slabs = [jnp.reshape(t, (1,)).astype(jnp.int32)
                  for t in (2 * (1 - my_x) + my_y, 2 * my_x + 1 - my_y, 2 * (1 - my_x) + 1 - my_y)]

    buckets = _bucket_table()
    onehot = (buckets.reshape(-1)[:, None] == jnp.arange(REL_BUCKETS)[None, :]).astype(F32)
    bias_tab = jnp.dot(onehot, rel_bias.astype(F32), precision=lax.Precision.HIGHEST).T.reshape(ATTN_HEADS, WINDOW, 2 * WINDOW)
    sinks_b = jnp.broadcast_to(attn_sinks[0].astype(F32)[:, None, None], (ATTN_HEADS, 8, LANES))
    lb_fn = lambda l: jnp.cumsum(jax.nn.softmax(l.astype(F32), axis=0), axis=0)[:1]
    lb, lb_vjp = jax.vjp(lb_fn, lb_logits)
    gain_h = hgrn_norm[0].reshape(1, HGRN_WIDTH)

    h, rstd = _rmsnorm_fwd(x2, norm_pre)
    gather_in = _Gather(placed[:1], [shard_ws[0].shape], shard_axes[:1], "w_in")
    gather_in.start([0, 1])
    proj = _matmul_slab(h, gather_in.fulls[0], s_arr, None, gather_in.token, "in_proj_own")
    gather_in.wait([0, 1], proj)
    gather_in.forward([0, 1])
    gather_in.start([2])
    proj = _matmul_slab(h, gather_in.fulls[0], peer_slabs[0], proj, gather_in.token, "in_proj_peer0")
    proj = _matmul_slab(h, gather_in.fulls[0], peer_slabs[1], proj, None, "in_proj_peer1")
    gather_in.wait([2], proj)
    gather_in.forward([2])
    gather_rest = _Gather(placed[1:], [w.shape for w in shard_ws[1:]], shard_axes[1:], "rest")
    gather_rest.start([0, 1, 2], gather_in.fulls[0])
    proj = _matmul_slab(h, gather_in.fulls[0], peer_slabs[2], proj, gather_rest.token, "in_proj_peer2")
    win_f = gather_in.fulls[0]
    ya, attn_probs = _attn_fwd(proj, bias_tab, sinks_b, B_loc, S)
    yh, states = _hgrn_fwd(proj, lb, gain_h, B_loc, S)
    gather_rest.wait([0, 1, 2], yh)
    gather_rest.forward([0, 1, 2])
    wba_f, wbh_f, wout_f = gather_rest.fulls
    ua = _matmul(ya, wba_f, "nn", BF16, "branch_attn_proj")
    uh, merged = _branch_proj_merge(yh, wbh_f, proj, ua)
    yv = _matmul(merged, wout_f, "nn", BF16, "out_proj")
    dy, dout, loss_p, gnpost_p = _post_loss(yv, x2, tgt2, norm_post)

    g_wout = _matmul(merged, dy, "tn", BF16, "out_proj_wgrad")
    d_ua, d_uh, dproj = _out_dgrad_merge_bwd(dy, wout_f, proj, ua, uh)
    g_wba = _matmul(ya, d_ua, "tn", BF16, "branch_attn_wgrad", slabs=N_CHIPS)
    g_wbh = _matmul(yh, d_uh, "tn", BF16, "branch_hgrn_wgrad", slabs=N_CHIPS)
    where_arr = jnp.concatenate([c_arr] + peer_slabs)
    late_pieces = [g.reshape(N_CHIPS, 2, -1, g.shape[-1]) for g in (g_wba, g_wbh, g_wout)]
    late = _reduce_start(late_pieces, _exchange_sibling_halves(late_pieces, "grads_to_sibling_branches"), names[1:], where_arr)
    d_ya = _matmul(d_ua, wba_f, "nt", BF16, "branch_attn_dgrad", after=late.token)
    d_yh = _matmul(d_uh, wbh_f, "nt", BF16, "branch_hgrn_dgrad", after=late.token)
    dproj, dbias_p, dsinks_p = _attn_bwd(proj, attn_probs, d_ya, dproj, B_loc, S)
    dproj, dlb_p, dgain_p = _hgrn_bwd(proj, states, lb, gain_h, d_yh, dproj, B_loc, S)
    g_give = _wgrad_half(h, dproj, 1 - c_arr, "in_proj_wgrad_sibling_half")
    s_sem, r_sem, g_give, zone, token = _sibling_send_start(g_give, "grads_to_sibling_start_w_in")
    g_keep = _wgrad_half(h, dproj, c_arr, "in_proj_wgrad_own_half", after=token)
    land = _sibling_send_wait(s_sem, r_sem, g_give, zone, g_keep, "grads_to_sibling_wait_w_in")
    last = _reduce_start([g_keep], [land], names[:1], where_arr)
    dh = _matmul(dproj, win_f, "nt", BF16, "in_proj_dgrad", after=last.token)
    grad_x2, gnpre_p = _rmsnorm_bwd(dh, x2, rstd, norm_pre, dout)
    shared = _reduce_finish(last, grad_x2, sc_arr) + _reduce_finish(late, grad_x2, sc_arr)

    grelb_p = jnp.dot(dbias_p.reshape(ATTN_HEADS, -1), onehot, precision=lax.Precision.HIGHEST).T
    gsinks_p = dsinks_p[:, 0, 0]
    dlb_sum = jnp.sum(dlb_p, axis=0).reshape(1, HGRN_WIDTH)
    ghn_p = jnp.sum(dgain_p, axis=0).reshape(HGRN_HEADS, HGRN_DIM)
    small_parts = [gnpre_p, gnpost_p, grelb_p, gsinks_p, dlb_sum, ghn_p, loss_p]
    small_shapes = [p.shape for p in small_parts]
    pack_sum = _sum_small(_pack_rows(small_parts))
    big_w = [w_in, w_branch_attn, w_branch_hgrn, w_out]
    big_m = [m_w_in, m_w_branch_attn, m_w_branch_hgrn, m_w_out]
    big_v = [v_w_in, v_w_branch_attn, v_w_branch_hgrn, v_w_out]
    big = {}
    for nm, gs, w, m, v in zip(names, shared, big_w, big_m, big_v):
        shp = w.shape
        g2 = gs.reshape(shp[1], shp[2])
        d, nm_, nv_, g_out = _adamw(w[0], g2, m[0], v[0], "adamw_" + nm)
        big[nm] = tuple(a.reshape(shp) for a in (g_out, d, nm_, nv_))

    gnpre, gnpost, grelb, gsinks, dlb, ghn, loss = _unpack_rows(pack_sum, small_shapes)
    (g_lb_logits,) = lb_vjp(dlb)
    small_names = ["norm_pre", "rel_bias", "attn_sinks", "lb_logits", "hgrn_norm", "norm_post"]
    small_w = [norm_pre, rel_bias, attn_sinks, lb_logits, hgrn_norm, norm_post]
    small_m = [m_norm_pre, m_rel_bias, m_attn_sinks, m_lb_logits, m_hgrn_norm, m_norm_post]
    small_v = [v_norm_pre, v_rel_bias, v_attn_sinks, v_lb_logits, v_hgrn_norm, v_norm_post]
    small_g = [gnpre.reshape(norm_pre.shape), grelb.reshape(rel_bias.shape), gsinks.reshape(attn_sinks.shape),
               g_lb_logits.reshape(lb_logits.shape), ghn.reshape(hgrn_norm.shape), gnpost.reshape(norm_post.shape)]
    shapes = [w.shape for w in small_w]
    d_s, nm_s, nv_s, _ = _adamw(_pack_rows(small_w), _pack_rows(small_g), _pack_rows(small_m), _pack_rows(small_v),
                             "adamw_small")
    small = {}
    for nm, g, d, m_, v_ in zip(small_names, small_g, _unpack_rows(d_s, shapes), _unpack_rows(nm_s, shapes),
                                _unpack_rows(nv_s, shapes)):
        small[nm] = (g, d, m_, v_)

    allw = {**big, **small}
    order = ["norm_pre", "w_in", "rel_bias", "attn_sinks", "lb_logits", "hgrn_norm", "w_branch_attn", "w_branch_hgrn",
             "w_out", "norm_post"]
    outs = [loss.reshape(()), grad_x2.reshape(B_loc, S, D)]
    for k in range(4):
        outs += [allw[nm][k] for nm in order]
    return tuple(outs)
```

```python
import math

import jax
import jax.numpy as jnp
from jax import lax
from jax.experimental import pallas as pl
from jax.experimental.pallas import tpu as pltpu

F32 = jnp.float32
BF16 = jnp.bfloat16
MESH = pl.DeviceIdType.MESH

ATTN_HEADS = 16
ATTN_KV_HEADS = 4
HEAD_DIM = 64
GROUP = ATTN_HEADS // ATTN_KV_HEADS
WINDOW = 128
ATTN_WIDTH = ATTN_HEADS * HEAD_DIM
KV_WIDTH = ATTN_KV_HEADS * HEAD_DIM
HGRN_HEADS = 8
HGRN_DIM = 128
HGRN_WIDTH = HGRN_HEADS * HGRN_DIM
CHUNK = 64
SUB = 16
NSUB = CHUNK // SUB
REL_BUCKETS = 32
REL_MAX_DIST = 128
NORM_EPS = 1e-6
ADAM_LR = 0.001
ADAM_B1 = 0.9
ADAM_B2 = 0.999
ADAM_EPS = 1e-08
ADAM_WD = 0.01
ADAM_STEP = 10
LANES = 128
N_CHIPS = 4
N_DEV = 8
VMEM_LIMIT = 48 * 1024 * 1024
MATMUL_OPERAND_BYTES = 34 * 1024 * 1024
MATMUL_VMEM_BYTES = 44 * 1024 * 1024

OFF_AQ = 0
OFF_AK = OFF_AQ + ATTN_WIDTH
OFF_AV = OFF_AK + KV_WIDTH
OFF_AG = OFF_AV + KV_WIDTH
OFF_HQ = OFF_AG + ATTN_WIDTH
OFF_HF = OFF_HQ + HGRN_WIDTH
OFF_HI = OFF_HF + HGRN_WIDTH
OFF_HG = OFF_HI + HGRN_WIDTH
OFF_GA = OFF_HG + HGRN_WIDTH

NT_DIMS = (((1,), (1,)), ((), ()))
TN_DIMS = (((0,), (0,)), ((), ()))
NN_DIMS = (((1,), (0,)), ((), ()))


def _pick(n, cands):
    for c in cands:
        if n % c == 0:
            return c
    raise ValueError(f"no tile for {n} in {cands}")


def _params(sem):
    return pltpu.CompilerParams(dimension_semantics=sem, vmem_limit_bytes=VMEM_LIMIT)


def _bdot(a, b, dims=NN_DIMS):
    return lax.dot_general(a.astype(BF16), b.astype(BF16), dims, preferred_element_type=F32)


def _matmul(a, b, mode, out_dtype, name, slabs=1, after=None):
    if mode == "nn":
        (M, K), (K2, N) = a.shape, b.shape
    elif mode == "nt":
        (M, K), (N, K2) = a.shape, b.shape
    else:
        (K, M), (K2, N) = a.shape, b.shape
    assert K == K2
    nslab = N // slabs
    tm = _pick(M, (1024, 512, 256, 128))
    out_bytes = jnp.dtype(out_dtype).itemsize
    choices = []
    for tn in (2688, 1024, 896, 512, 384, 256, 128):
        for tk in (4096, 3584, 2048, 1792, 1536, 1024, 512, 256, 128):
            acc = 0 if tk == K else 4 * tm * tn
            if (nslab % tn == 0 and K % tk == 0 and 4 * tk * (tm + tn) <= MATMUL_OPERAND_BYTES
                    and 4 * tk * (tm + tn) + 2 * out_bytes * tm * tn + acc <= MATMUL_VMEM_BYTES):
                choices.append((K // tk > 1, -tn, tn, tk))
                break
    _, _, tn, tk = min(choices)
    nk = K // tk
    per = nslab // tn
    dims = {"nn": NN_DIMS, "nt": NT_DIMS, "tn": TN_DIMS}[mode]

    n_in = 2 if after is None else 3

    def body(*refs):
        a_ref, b_ref, o_ref, acc = refs[0], refs[1], refs[n_in], refs[n_in + 1:]
        part = lax.dot_general(a_ref[...], b_ref[...], dims, preferred_element_type=F32)
        if nk == 1:
            o_ref[...] = part.astype(o_ref.dtype)
            return
        acc_ref, = acc
        k = pl.program_id(2)

        @pl.when(k == 0)
        def _():
            acc_ref[...] = part

        @pl.when((k > 0) & (k < nk - 1))
        def _():
            acc_ref[...] += part

        @pl.when(k == nk - 1)
        def _():
            o_ref[...] = (acc_ref[...] + part).astype(o_ref.dtype)

    if mode == "tn":
        a_spec = pl.BlockSpec((tk, tm), lambda i, j, k: (k, i))
    else:
        a_spec = pl.BlockSpec((tm, tk), lambda i, j, k: (i, k))
    if mode == "nt":
        b_spec = pl.BlockSpec((tn, tk), lambda i, j, k: (j, k))
    else:
        b_spec = pl.BlockSpec((tk, tn), lambda i, j, k: (k, j))
    if slabs == 1:
        o_shape = jax.ShapeDtypeStruct((M, N), out_dtype)
        o_spec = pl.BlockSpec((tm, tn), lambda i, j, k: (i, j))
    else:
        o_shape = jax.ShapeDtypeStruct((slabs, M, nslab), out_dtype)
        o_spec = pl.BlockSpec((None, tm, tn), lambda i, j, k: (j // per, i, j % per))
    return pl.pallas_call(
        body, name=name, grid=(M // tm, N // tn, nk), in_specs=[a_spec, b_spec] + ([] if after is None else [ANY]),
        out_specs=o_spec, out_shape=o_shape,
        scratch_shapes=[pltpu.VMEM((tm, tn), F32)] if nk > 1 else [],
        compiler_params=_params(("parallel", "parallel", "arbitrary")),
    )(*((a, b) if after is None else (a, b, after)))


def _rmsnorm_fwd(x2, gain):
    T, D = x2.shape
    tr = _pick(T, (256, 128))

    def body(x_ref, g_ref, h_ref, r_ref):
        xv = x_ref[...]
        r = lax.rsqrt(jnp.mean(xv * xv, axis=-1, keepdims=True) + NORM_EPS)
        h_ref[...] = (xv * r * g_ref[...]).astype(BF16)
        r_ref[...] = r

    return pl.pallas_call(
        body, name="rmsnorm_pre_fwd", grid=(T // tr,),
        in_specs=[pl.BlockSpec((tr, D), lambda i: (i, 0)), pl.BlockSpec((1, D), lambda i: (0, 0))],
        out_specs=[pl.BlockSpec((tr, D), lambda i: (i, 0)), pl.BlockSpec((tr, 1), lambda i: (i, 0))],
        out_shape=[jax.ShapeDtypeStruct((T, D), BF16), jax.ShapeDtypeStruct((T, 1), F32)],
        compiler_params=_params(("parallel",)),
    )(x2, gain)


def _rmsnorm_bwd(dh, x2, rstd, gain, dout):
    T, D = x2.shape
    tr = _pick(T, (256, 128))

    def body(dh_ref, x_ref, r_ref, g_ref, do_ref, gx_ref, gg_ref):
        @pl.when(pl.program_id(0) == 0)
        def _():
            gg_ref[...] = jnp.zeros_like(gg_ref)

        n = x_ref[...] * r_ref[...]
        dhv = dh_ref[...].astype(F32)
        dn = dhv * g_ref[...]
        gx_ref[...] = do_ref[...] + r_ref[...] * (dn - n * jnp.mean(dn * n, axis=-1, keepdims=True))
        gg_ref[...] += jnp.sum(dhv * n, axis=0, keepdims=True)

    row = pl.BlockSpec((tr, D), lambda i: (i, 0))
    vec = pl.BlockSpec((1, D), lambda i: (0, 0))
    return pl.pallas_call(
        body, name="rmsnorm_pre_bwd", grid=(T // tr,),
        in_specs=[row, row, pl.BlockSpec((tr, 1), lambda i: (i, 0)), vec, row],
        out_specs=[row, vec],
        out_shape=[jax.ShapeDtypeStruct((T, D), F32), jax.ShapeDtypeStruct((1, D), F32)],
        compiler_params=_params(("arbitrary",)),
    )(dh, x2, rstd, gain, dout)


def _post_loss(yv, x2, tgt2, gain):
    T, D = x2.shape
    tr = _pick(T, (256, 128))

    def body(y_ref, x_ref, t_ref, g_ref, dy_ref, do_ref, loss_ref, gg_ref):
        @pl.when(pl.program_id(0) == 0)
        def _():
            gg_ref[...] = jnp.zeros_like(gg_ref)
            loss_ref[...] = jnp.zeros_like(loss_ref)

        yv_ = y_ref[...].astype(F32)
        r = lax.rsqrt(jnp.mean(yv_ * yv_, axis=-1, keepdims=True) + NORM_EPS)
        n = yv_ * r
        e = (x_ref[...] + n * g_ref[...]) - t_ref[...]
        loss_ref[...] += 0.5 * jnp.sum(jnp.mean(e * e, axis=-1, keepdims=True), axis=0, keepdims=True)
        dz = e / D
        do_ref[...] = dz
        gg_ref[...] += jnp.sum(dz * n, axis=0, keepdims=True)
        dn = dz * g_ref[...]
        dy_ref[...] = (r * (dn - n * jnp.mean(dn * n, axis=-1, keepdims=True))).astype(BF16)

    row = pl.BlockSpec((tr, D), lambda i: (i, 0))
    vec = pl.BlockSpec((1, D), lambda i: (0, 0))
    return pl.pallas_call(
        body, name="post_norm_loss", grid=(T // tr,),
        in_specs=[row, row, row, vec],
        out_specs=[row, row, pl.BlockSpec((1, 1), lambda i: (0, 0)), vec],
        out_shape=[jax.ShapeDtypeStruct((T, D), BF16), jax.ShapeDtypeStruct((T, D), F32),
                   jax.ShapeDtypeStruct((1, 1), F32), jax.ShapeDtypeStruct((1, D), F32)],
        compiler_params=_params(("arbitrary",)),
    )(yv, x2, tgt2, gain)


def _window(rows, cols, at):
    return pl.BlockSpec((pl.Element(rows), pl.Element(cols)), at)


def _gate_windows(tm, tn, D):
    return [_window(tm, tn, lambda i, j: (i * tm, pl.multiple_of(OFF_GA + j * tn, LANES))),
            _window(tm, tn, lambda i, j: (i * tm, pl.multiple_of(OFF_GA + D + j * tn, LANES)))]


def _branch_proj_merge(yh, wbh, proj, ua):
    T, K = yh.shape
    D = wbh.shape[1]
    tm, tn = _pick(T, (512, 256, 128)), _pick(D, (1024, 512, 256))

    def body(a_ref, b_ref, ga_ref, gh_ref, ua_ref, uh_ref, m_ref):
        uh = jnp.dot(a_ref[...], b_ref[...], preferred_element_type=F32).astype(BF16)
        uh_ref[...] = uh
        m_ref[...] = (jax.nn.sigmoid(ga_ref[...]) * ua_ref[...].astype(F32)
                      + jax.nn.sigmoid(gh_ref[...]) * uh.astype(F32)).astype(BF16)

    blk = pl.BlockSpec((tm, tn), lambda i, j: (i, j))
    o = jax.ShapeDtypeStruct((T, D), BF16)
    return pl.pallas_call(
        body, name="branch_hgrn_proj_merge", grid=(T // tm, D // tn),
        in_specs=[pl.BlockSpec((tm, K), lambda i, j: (i, 0)), pl.BlockSpec((K, tn), lambda i, j: (0, j))]
        + _gate_windows(tm, tn, D) + [blk],
        out_specs=[blk, blk], out_shape=[o, o],
        compiler_params=_params(("parallel", "arbitrary")),
    )(yh, wbh, proj, proj, ua)


def _out_dgrad_merge_bwd(dy, wout, proj, ua, uh):
    T, K = dy.shape
    D = wout.shape[0]
    tm, tn = _pick(T, (512, 256, 128)), _pick(D, (1024, 512, 256))
    nj = D // tn

    def body(a_ref, b_ref, ga_ref, gh_ref, ua_ref, uh_ref, dua_ref, duh_ref, dproj_ref):
        d = lax.dot_general(a_ref[...], b_ref[...], NT_DIMS, preferred_element_type=F32)
        sa = jax.nn.sigmoid(ga_ref[...])
        sh = jax.nn.sigmoid(gh_ref[...])
        dua_ref[...] = (d * sa).astype(BF16)
        duh_ref[...] = (d * sh).astype(BF16)
        dga = (d * ua_ref[...].astype(F32) * sa * (1.0 - sa)).astype(BF16)
        dgh = (d * uh_ref[...].astype(F32) * sh * (1.0 - sh)).astype(BF16)
        for jj in range(nj):
            @pl.when(pl.program_id(1) == jj)
            def _():
                dproj_ref[:, jj * tn:(jj + 1) * tn] = dga
                dproj_ref[:, D + jj * tn:D + (jj + 1) * tn] = dgh

    blk = pl.BlockSpec((tm, tn), lambda i, j: (i, j))
    o = jax.ShapeDtypeStruct((T, D), BF16)
    return pl.pallas_call(
        body, name="out_proj_dgrad_merge_bwd", grid=(T // tm, nj),
        in_specs=[pl.BlockSpec((tm, K), lambda i, j: (i, 0)), pl.BlockSpec((tn, K), lambda i, j: (j, 0))]
        + _gate_windows(tm, tn, D) + [blk, blk],
        out_specs=[blk, blk, _window(tm, 2 * D, lambda i, j: (i * tm, OFF_GA))],
        out_shape=[o, o, jax.ShapeDtypeStruct((T, proj.shape[1]), BF16)],
        compiler_params=_params(("parallel", "arbitrary")),
    )(dy, wout, proj, proj, ua, uh)


KV_PAIR = 2
PAIR_HEADS = KV_PAIR * GROUP


def _attn_mask(n):
    qi = lax.broadcasted_iota(jnp.int32, (WINDOW, 2 * WINDOW), 0)
    si = lax.broadcasted_iota(jnp.int32, (WINDOW, 2 * WINDOW), 1)
    dist = qi + WINDOW - si
    return (dist >= 0) & (dist < WINDOW) & ((si >= WINDOW) | (n > 0))


def _first_key_column(shape):
    return lax.broadcasted_iota(jnp.int32, shape, 1) == 0


def _attn_group_fwd(mask, q, k, v, ag, bias, sink):
    mask = jnp.concatenate([mask] * GROUP, axis=0)
    s = lax.dot_general(q.astype(BF16), k.astype(BF16), NT_DIMS, preferred_element_type=F32) * (HEAD_DIM ** -0.5)
    s = jnp.where(mask, s + bias, -1e30)
    m = jnp.maximum(jnp.max(s, axis=-1, keepdims=True), sink)
    p = jnp.exp(s - m)
    e_sink = jnp.exp(sink - m)
    den = jnp.sum(p, axis=-1, keepdims=True) + e_sink
    pb = p.astype(BF16)
    o = jnp.dot(pb, v.astype(BF16), preferred_element_type=F32)
    return o * (jax.nn.silu(ag) / den), pb, e_sink.astype(BF16)


def _attn_group_bwd(q, k, v, ag, kept, dout):
    e_sink = kept[:, 0:1].astype(F32)
    pb = jnp.where(_first_key_column(kept.shape), jnp.zeros_like(kept), kept)
    pf = pb.astype(F32)
    rden = 1.0 / (jnp.sum(pf, axis=-1, keepdims=True) + e_sink)
    qb, kb, vb = q.astype(BF16), k.astype(BF16), v.astype(BF16)
    o = jnp.dot(pb, vb, preferred_element_type=F32) * rden
    sg = jax.nn.sigmoid(ag)
    d_o = dout * (ag * sg)
    dag = dout * o * (sg * (1.0 + ag * (1.0 - sg)))
    d_row = jnp.sum(d_o * o, axis=-1, keepdims=True)
    ds = (pf * rden) * (lax.dot_general(d_o.astype(BF16), vb, NT_DIMS, preferred_element_type=F32) - d_row)
    dsb = ds.astype(BF16)
    dq = jnp.dot(dsb, kb, preferred_element_type=F32) * (HEAD_DIM ** -0.5)
    dk = lax.dot_general(dsb, qb, TN_DIMS, preferred_element_type=F32) * (HEAD_DIM ** -0.5)
    dv = lax.dot_general(pb, (d_o * rden).astype(BF16), TN_DIMS, preferred_element_type=F32)
    return dq, dk, dv, dag, ds, -(e_sink * rden * d_row)


def _attn_specs(B_loc, nb, order):
    qw = PAIR_HEADS * HEAD_DIM
    kw = KV_PAIR * HEAD_DIM

    def rows(g):
        b, p, n = order(*g)
        return b * nb + n

    def prev(g):
        b, p, n = order(*g)
        return b * nb + jnp.maximum(n - 1, 0)

    def pp(g):
        return order(*g)[1]

    q = pl.BlockSpec((WINDOW, qw), lambda *g: (rows(g), OFF_AQ // qw + pp(g)))
    kc = pl.BlockSpec((WINDOW, kw), lambda *g: (rows(g), OFF_AK // kw + pp(g)))
    kp = pl.BlockSpec((WINDOW, kw), lambda *g: (prev(g), OFF_AK // kw + pp(g)))
    vc = pl.BlockSpec((WINDOW, kw), lambda *g: (rows(g), OFF_AV // kw + pp(g)))
    vp = pl.BlockSpec((WINDOW, kw), lambda *g: (prev(g), OFF_AV // kw + pp(g)))
    ag = pl.BlockSpec((WINDOW, qw), lambda *g: (rows(g), OFF_AG // qw + pp(g)))
    bias = pl.BlockSpec((PAIR_HEADS, WINDOW, 2 * WINDOW), lambda *g: (pp(g), 0, 0))
    sink = pl.BlockSpec((PAIR_HEADS, 8, LANES), lambda *g: (pp(g), 0, 0))
    return [q, kc, kp, vc, vp, ag, bias, sink], rows, pp


def _attn_operands(q_ref, kc_ref, kp_ref, vc_ref, vp_ref, ag_ref, bias_ref, sink_ref, j):
    lo, hi = j * HEAD_DIM, (j + 1) * HEAD_DIM
    k = jnp.concatenate([kp_ref[:, lo:hi], kc_ref[:, lo:hi]], axis=0)
    v = jnp.concatenate([vp_ref[:, lo:hi], vc_ref[:, lo:hi]], axis=0)
    heads = [j * GROUP + g for g in range(GROUP)]
    q = jnp.concatenate([q_ref[:, h * HEAD_DIM:(h + 1) * HEAD_DIM] for h in heads], axis=0)
    ag = jnp.concatenate([ag_ref[:, h * HEAD_DIM:(h + 1) * HEAD_DIM] for h in heads], axis=0)
    bias = jnp.concatenate([bias_ref[h] for h in heads], axis=0)
    sink = jnp.concatenate([jnp.broadcast_to(sink_ref[h, 0:1, 0:1], (WINDOW, 1)) for h in heads], axis=0)
    return q, k, v, ag, bias, sink


def _attn_fwd(proj, bias_tab, sinks_b, B_loc, S):
    T = B_loc * S
    nb = S // WINDOW
    n_pairs = ATTN_KV_HEADS // KV_PAIR
    in_specs, rows, pp = _attn_specs(B_loc, nb, lambda b, p, n: (b, p, n))

    def body(q_ref, kc_ref, kp_ref, vc_ref, vp_ref, ag_ref, bias_ref, sink_ref, ya_ref, p_ref):
        mask = _attn_mask(pl.program_id(2))
        for j in range(KV_PAIR):
            out, probs, e_sink = _attn_group_fwd(
                mask, *_attn_operands(q_ref, kc_ref, kp_ref, vc_ref, vp_ref, ag_ref, bias_ref, sink_ref, j))
            for g in range(GROUP):
                h = j * GROUP + g
                blk = slice(g * WINDOW, (g + 1) * WINDOW)
                ya_ref[:, h * HEAD_DIM:(h + 1) * HEAD_DIM] = out[blk].astype(BF16)
                p_ref[:, h * 2 * WINDOW:(h + 1) * 2 * WINDOW] = probs[blk]
                p_ref[:, h * 2 * WINDOW:h * 2 * WINDOW + 1] = e_sink[blk]

    return pl.pallas_call(
        body, name="attn_fwd", grid=(B_loc, n_pairs, nb), in_specs=in_specs,
        out_specs=[pl.BlockSpec((WINDOW, PAIR_HEADS * HEAD_DIM), lambda *g: (rows(g), pp(g))),
                   pl.BlockSpec((WINDOW, PAIR_HEADS * 2 * WINDOW), lambda *g: (rows(g), pp(g)))],
        out_shape=[jax.ShapeDtypeStruct((T, ATTN_WIDTH), BF16),
                   jax.ShapeDtypeStruct((T, ATTN_HEADS * 2 * WINDOW), BF16)],
        compiler_params=_params(("parallel", "parallel", "parallel")),
    )(proj, proj, proj, proj, proj, proj, bias_tab, sinks_b)


def _attn_bwd(proj, probs, dya, dproj, B_loc, S):
    nb = S // WINDOW
    n_pairs = ATTN_KV_HEADS // KV_PAIR
    qw, kw = PAIR_HEADS * HEAD_DIM, KV_PAIR * HEAD_DIM

    def at(off, width, back=0):
        return lambda b, i, p: ((b * nb + jnp.maximum(nb - 1 - i - back, 0)) * WINDOW,
                                pl.multiple_of(off + p * width, LANES))

    in_specs = [_window(WINDOW, qw, at(OFF_AQ, qw)), _window(WINDOW, kw, at(OFF_AK, kw)),
                _window(WINDOW, kw, at(OFF_AK, kw, 1)), _window(WINDOW, kw, at(OFF_AV, kw)),
                _window(WINDOW, kw, at(OFF_AV, kw, 1)), _window(WINDOW, qw, at(OFF_AG, qw)),
                pl.BlockSpec((WINDOW, PAIR_HEADS * 2 * WINDOW), lambda b, i, p: (b * nb + nb - 1 - i, p)),
                pl.BlockSpec((WINDOW, qw), lambda b, i, p: (b * nb + nb - 1 - i, p)), ANY]

    def body(q_ref, kc_ref, kp_ref, vc_ref, vp_ref, ag_ref, p_ref, dya_ref, dproj_in,
             dproj_ref, dbias_ref, dsink_ref, dkc_ref, dvc_ref):
        b, i, p = pl.program_id(0), pl.program_id(1), pl.program_id(2)

        @pl.when((b == 0) & (i == 0) & (p == 0))
        def _():
            dbias_ref[...] = jnp.zeros_like(dbias_ref)
            dsink_ref[...] = jnp.zeros_like(dsink_ref)

        @pl.when(i == 0)
        def _():
            dkc_ref[p] = jnp.zeros((WINDOW, kw), F32)
            dvc_ref[p] = jnp.zeros((WINDOW, kw), F32)

        dk_carry, dv_carry = dkc_ref[p], dvc_ref[p]
        dqs, dags, dbiases, dsinks, dks, dvs = [], [], [], [], [], []
        for j in range(KV_PAIR):
            heads = [j * GROUP + g for g in range(GROUP)]
            lo, hi = j * HEAD_DIM, (j + 1) * HEAD_DIM
            stack = lambda parts: jnp.concatenate(parts, axis=0)
            k = stack([kp_ref[:, lo:hi], kc_ref[:, lo:hi]])
            v = stack([vp_ref[:, lo:hi], vc_ref[:, lo:hi]])
            q = stack([q_ref[:, h * HEAD_DIM:(h + 1) * HEAD_DIM] for h in heads])
            ag = stack([ag_ref[:, h * HEAD_DIM:(h + 1) * HEAD_DIM] for h in heads])
            dout = stack([dya_ref[:, h * HEAD_DIM:(h + 1) * HEAD_DIM].astype(F32) for h in heads])
            kept = stack([p_ref[:, h * 2 * WINDOW:(h + 1) * 2 * WINDOW] for h in heads])
            dq, dk, dv, dag, dbias, dsink = _attn_group_bwd(q, k, v, ag, kept, dout)
            dks.append((dk[WINDOW:] + dk_carry[:, lo:hi]).astype(BF16))
            dvs.append((dv[WINDOW:] + dv_carry[:, lo:hi]).astype(BF16))
            dkc_ref[p, :, lo:hi] = dk[:WINDOW]
            dvc_ref[p, :, lo:hi] = dv[:WINDOW]
            for g in range(GROUP):
                blk = slice(g * WINDOW, (g + 1) * WINDOW)
                dqs.append(dq[blk].astype(BF16))
                dags.append(dag[blk].astype(BF16))
                dbiases.append(dbias[blk])
                dsinks.append(jnp.broadcast_to(jnp.sum(dsink[blk], axis=0, keepdims=True), (8, LANES)))

        for pair in range(n_pairs):
            @pl.when(p == pair)
            def _():
                for j in range(KV_PAIR):
                    col = (pair * KV_PAIR + j) * HEAD_DIM
                    dproj_ref[:, OFF_AK + col:OFF_AK + col + HEAD_DIM] = dks[j]
                    dproj_ref[:, OFF_AV + col:OFF_AV + col + HEAD_DIM] = dvs[j]
                for hh in range(PAIR_HEADS):
                    h = pair * PAIR_HEADS + hh
                    dproj_ref[:, OFF_AQ + h * HEAD_DIM:OFF_AQ + (h + 1) * HEAD_DIM] = dqs[hh]
                    dproj_ref[:, OFF_AG + h * HEAD_DIM:OFF_AG + (h + 1) * HEAD_DIM] = dags[hh]
                    dbias_ref[h] += dbiases[hh]
                    dsink_ref[h] += dsinks[hh]

    return pl.pallas_call(
        body, name="attn_bwd", grid=(B_loc, nb, n_pairs), in_specs=in_specs,
        out_specs=[_window(WINDOW, OFF_HQ, lambda b, i, p: ((b * nb + nb - 1 - i) * WINDOW, 0)),
                   pl.BlockSpec((ATTN_HEADS, WINDOW, 2 * WINDOW), lambda b, i, p: (0, 0, 0)),
                   pl.BlockSpec((ATTN_HEADS, 8, LANES), lambda b, i, p: (0, 0, 0))],
        out_shape=[jax.ShapeDtypeStruct(dproj.shape, dproj.dtype),
                   jax.ShapeDtypeStruct((ATTN_HEADS, WINDOW, 2 * WINDOW), F32),
                   jax.ShapeDtypeStruct((ATTN_HEADS, 8, LANES), F32)],
        input_output_aliases={8: 0},
        scratch_shapes=[pltpu.VMEM((n_pairs, WINDOW, kw), F32), pltpu.VMEM((n_pairs, WINDOW, kw), F32)],
        compiler_params=_params(("arbitrary", "arbitrary", "arbitrary")),
    )(proj, proj, proj, proj, proj, proj, probs, dya, dproj)


class _HgrnPre:
    def __init__(self, fr, qr, lb, g_scr):
        t = lax.broadcasted_iota(jnp.int32, (CHUNK, CHUNK), 0)
        s = lax.broadcasted_iota(jnp.int32, (CHUNK, CHUNK), 1)
        self.sg = jax.nn.sigmoid(fr)
        self.f = lb + (1.0 - lb) * self.sg
        g = jnp.dot((t >= s).astype(F32), jnp.log(self.f), precision=lax.Precision.HIGHEST, preferred_element_type=F32)
        g_scr[...] = g
        self.g = g
        self.row = lax.broadcasted_iota(jnp.int32, g.shape, 0)
        self.refs = [jnp.zeros((1, g.shape[1]), F32)] + [g_scr[pl.ds(i * SUB - 1, 1), :] for i in range(1, NSUB)]
        self.gend = g_scr[pl.ds(CHUNK - 1, 1), :]
        refrow = jnp.zeros_like(g)
        for i in range(1, NSUB):
            refrow = jnp.where(self.row >= i * SUB, self.refs[i], refrow)
        self.sigq = jax.nn.sigmoid(qr)
        self.qs = qr * self.sigq
        self.k = 1.0 - self.f
        self.eg = jnp.exp(g)
        self.eqd = jnp.exp(g - refrow)
        self.ekd = [jnp.exp(jnp.where(self.row < (i + 1) * SUB, self.refs[i] - g, 0.0)) for i in range(NSUB)]
        self.ekdec = jnp.exp(self.gend - g)
        self.qg = self.qs * self.eg
        self.qd = self.qs * self.eqd
        self.kd = [self.k * e for e in self.ekd]
        self.kdec = self.k * self.ekdec
        self.egend = jnp.exp(self.gend)


def _hgrn_pair_mask():
    t = lax.broadcasted_iota(jnp.int32, (CHUNK, NSUB * CHUNK), 0)
    col = lax.broadcasted_iota(jnp.int32, (CHUNK, NSUB * CHUNK), 1)
    return ((t // SUB) == (col // CHUNK)) & ((col % CHUNK) <= t)


def _hgrn_head_out(p, lanes, state_t, v, mask):
    qg, qd = p.qg[:, lanes], p.qd[:, lanes]
    kall = jnp.concatenate([kd[:, lanes] for kd in p.kd], axis=0)
    vst = jnp.concatenate([v] * NSUB, axis=0)
    am = jnp.where(mask, _bdot(qd, kall, NT_DIMS), 0.0)
    o = _bdot(qg, state_t, NT_DIMS) + _bdot(am, vst)
    return o, (qg, qd, kall, am, vst)


def _hgrn_fwd(proj, lb, gain, B_loc, S):
    T = B_loc * S
    nc = S // CHUNK
    nh = HGRN_HEADS

    def at(off):
        return lambda b, n: ((b * nc + n) * CHUNK, off)

    vec = pl.BlockSpec((1, HGRN_WIDTH), lambda b, n: (0, 0))

    def body(q_ref, f_ref, v_ref, hg_ref, lb_ref, gain_ref, yh_ref, st_ref, state_scr, g_scr):
        @pl.when(pl.program_id(1) == 0)
        def _():
            state_scr[...] = jnp.zeros_like(state_scr)

        p = _HgrnPre(f_ref[...], q_ref[...], lb_ref[...], g_scr)
        v = v_ref[...]
        gate = gain_ref[...] * jax.nn.silu(hg_ref[...])
        mask = _hgrn_pair_mask()
        for hd in range(nh):
            lanes = slice(hd * HGRN_DIM, (hd + 1) * HGRN_DIM)
            st = state_scr[hd]
            st_ref[hd] = st
            o, _ = _hgrn_head_out(p, lanes, st, v[:, lanes], mask)
            rs = lax.rsqrt(jnp.mean(o * o, axis=-1, keepdims=True) + NORM_EPS)
            yh_ref[:, lanes] = (o * rs * gate[:, lanes]).astype(BF16)
            state_scr[hd] = st * p.egend[:, lanes] + _bdot(v[:, lanes], p.kdec[:, lanes], TN_DIMS)

    return pl.pallas_call(
        body, name="hgrn_fwd", grid=(B_loc, nc),
        in_specs=[_window(CHUNK, HGRN_WIDTH, at(OFF_HQ)), _window(CHUNK, HGRN_WIDTH, at(OFF_HF)),
                  _window(CHUNK, HGRN_WIDTH, at(OFF_HI)), _window(CHUNK, HGRN_WIDTH, at(OFF_HG)), vec, vec],
        out_specs=[pl.BlockSpec((CHUNK, HGRN_WIDTH), lambda b, n: (b * nc + n, 0)),
                   pl.BlockSpec((None, None, nh, HGRN_DIM, HGRN_DIM), lambda b, n: (b, n, 0, 0, 0))],
        out_shape=[jax.ShapeDtypeStruct((T, HGRN_WIDTH), BF16),
                   jax.ShapeDtypeStruct((B_loc, nc, nh, HGRN_DIM, HGRN_DIM), F32)],
        scratch_shapes=[pltpu.VMEM((nh, HGRN_DIM, HGRN_DIM), F32), pltpu.VMEM((CHUNK, HGRN_WIDTH), F32)],
        compiler_params=_params(("parallel", "arbitrary")),
    )(proj, proj, proj, proj, lb, gain)


def _hgrn_bwd(proj, states, lb, gain, dyh, dproj, B_loc, S):
    nc = S // CHUNK
    nh = HGRN_HEADS

    def at(off):
        return lambda b, i: ((b * nc + nc - 1 - i) * CHUNK, off)

    vec = pl.BlockSpec((1, HGRN_WIDTH), lambda b, i: (0, 0))
    in_specs = [_window(CHUNK, HGRN_WIDTH, at(OFF_HQ)), _window(CHUNK, HGRN_WIDTH, at(OFF_HF)),
                _window(CHUNK, HGRN_WIDTH, at(OFF_HI)), _window(CHUNK, HGRN_WIDTH, at(OFF_HG)), vec, vec,
                pl.BlockSpec((None, None, nh, HGRN_DIM, HGRN_DIM), lambda b, i: (b, nc - 1 - i, 0, 0, 0)),
                pl.BlockSpec((CHUNK, HGRN_WIDTH), lambda b, i: (b * nc + nc - 1 - i, 0)), ANY]
    acc_spec = pl.BlockSpec((None, 1, HGRN_WIDTH), lambda b, i: (b, 0, 0))

    def body(q_ref, f_ref, v_ref, hg_ref, lb_ref, gain_ref, st_ref, dyh_ref, dproj_in,
             dproj_ref, dlb_ref, dgain_ref, dstate_scr, g_scr, dg_scr):
        dq_ref, df_ref, dv_ref, dhg_ref = [dproj_ref.at[:, pl.ds(k * HGRN_WIDTH, HGRN_WIDTH)] for k in range(4)]

        @pl.when(pl.program_id(1) == 0)
        def _():
            dstate_scr[...] = jnp.zeros_like(dstate_scr)
            dlb_ref[...] = jnp.zeros_like(dlb_ref)
            dgain_ref[...] = jnp.zeros_like(dgain_ref)

        qr, lb, gain, hg, v = q_ref[...], lb_ref[...], gain_ref[...], hg_ref[...], v_ref[...]
        p = _HgrnPre(f_ref[...], qr, lb, g_scr)
        sgh = jax.nn.sigmoid(hg)
        sil = hg * sgh
        dy = dyh_ref[...].astype(F32)
        mask = _hgrn_pair_mask()
        dqg, dqd, dkdec, dv, dhg, dgend, dgain = [], [], [], [], [], [], []
        dkd = [[] for _ in range(NSUB)]
        heads = [slice(hd * HGRN_DIM, (hd + 1) * HGRN_DIM) for hd in range(nh)]
        sts = [st_ref[hd] for hd in range(nh)]
        dnews = [dstate_scr[hd] for hd in range(nh)]
        fwd = [_hgrn_head_out(p, lanes, st, v[:, lanes], mask) for lanes, st in zip(heads, sts)]
        for lanes, st, dnew in zip(heads, sts, dnews):
            dkdec_h = _bdot(v[:, lanes], dnew)
            dkdec.append(dkdec_h)
            dgend.append(jnp.sum(dkdec_h * p.kdec[:, lanes], axis=0, keepdims=True)
                         + jnp.sum(dnew * st, axis=0, keepdims=True) * p.egend[:, lanes])
        dos = []
        gate_grad = sgh * (1.0 + hg * (1.0 - sgh))
        for lanes, (o, _) in zip(heads, fwd):
            rs = lax.rsqrt(jnp.mean(o * o, axis=-1, keepdims=True) + NORM_EPS)
            n = o * rs
            dyn = dy[:, lanes] * n
            dgain.append(jnp.sum(dyn * sil[:, lanes], axis=0, keepdims=True))
            dhg.append(dyn * gain[:, lanes] * gate_grad[:, lanes])
            dn = dy[:, lanes] * gain[:, lanes] * sil[:, lanes]
            dos.append(rs * (dn - n * jnp.mean(dn * n, axis=-1, keepdims=True)))
        drs = []
        for hd, (lanes, st, dnew, do, (_, (qg, qd, kall, am, vst))) in enumerate(zip(heads, sts, dnews, dos, fwd)):
            dqg.append(_bdot(do, st))
            dstate_scr[hd] = _bdot(do, qg, TN_DIMS) + dnew * p.egend[:, lanes]
            drs.append(jnp.where(mask, _bdot(do, vst, NT_DIMS), 0.0))
            dvst = _bdot(am, do, TN_DIMS)
            dv.append(sum(dvst[i * CHUNK:(i + 1) * CHUNK] for i in range(NSUB)) + _bdot(p.kdec[:, lanes], dnew, NT_DIMS))
        for dr, (_, (qg, qd, kall, am, vst)) in zip(drs, fwd):
            dqd.append(_bdot(dr, kall))
            dkall = _bdot(dr, qd, TN_DIMS)
            for i in range(NSUB):
                dkd[i].append(dkall[i * CHUNK:(i + 1) * CHUNK])

        wide = lambda parts: jnp.concatenate(parts, axis=1)
        dqg, dqd, dkdec = wide(dqg), wide(dqd), wide(dkdec)
        t2 = dqd * p.qd
        dg = dqg * p.qg + t2 - dkdec * p.kdec
        dk = dkdec * p.ekdec
        dg_scr[...] = jnp.zeros_like(dg_scr)
        for i in range(NSUB):
            dkd_i = wide(dkd[i])
            tk = jnp.where(p.row < (i + 1) * SUB, dkd_i * p.kd[i], 0.0)
            dg = dg - tk
            dk = dk + dkd_i * p.ekd[i]
            if i >= 1:
                in_blk = (p.row >= i * SUB) & (p.row < (i + 1) * SUB)
                dg_scr[pl.ds(i * SUB - 1, 1), :] = (jnp.sum(tk, axis=0, keepdims=True)
                                                    - jnp.sum(jnp.where(in_blk, t2, 0.0), axis=0, keepdims=True))
        dg_scr[pl.ds(CHUNK - 1, 1), :] = wide(dgend)
        t = lax.broadcasted_iota(jnp.int32, (CHUNK, CHUNK), 0)
        s = lax.broadcasted_iota(jnp.int32, (CHUNK, CHUNK), 1)
        dlogf = jnp.dot((t <= s).astype(F32), dg + dg_scr[...], precision=lax.Precision.HIGHEST, preferred_element_type=F32)
        df = dlogf / p.f - dk
        df_ref[...] = (df * (1.0 - lb) * p.sg * (1.0 - p.sg)).astype(BF16)
        dlb_ref[...] += jnp.sum(df * (1.0 - p.sg), axis=0, keepdims=True)
        dq_ref[...] = ((dqg * p.eg + dqd * p.eqd) * p.sigq * (1.0 + qr * (1.0 - p.sigq))).astype(BF16)
        dv_ref[...] = wide(dv).astype(BF16)
        dhg_ref[...] = wide(dhg).astype(BF16)
        dgain_ref[...] += wide(dgain)

    acc = jax.ShapeDtypeStruct((B_loc, 1, HGRN_WIDTH), F32)
    return pl.pallas_call(
        body, name="hgrn_bwd", grid=(B_loc, nc), in_specs=in_specs,
        out_specs=[_window(CHUNK, 4 * HGRN_WIDTH, at(OFF_HQ)), acc_spec, acc_spec],
        out_shape=[jax.ShapeDtypeStruct(dproj.shape, dproj.dtype), acc, acc],
        input_output_aliases={8: 0},
        scratch_shapes=[pltpu.VMEM((nh, HGRN_DIM, HGRN_DIM), F32), pltpu.VMEM((CHUNK, HGRN_WIDTH), F32),
                        pltpu.VMEM((CHUNK, HGRN_WIDTH), F32)],
        compiler_params=_params(("parallel", "arbitrary")),
    )(proj, proj, proj, proj, lb, gain, states, dyh, dproj)


def _adamw(w, g, m, v, name):
    R, C = w.shape
    tr = _pick(R, (128, 64, 32, 16, 8)) if C > 1024 else _pick(R, (512, 256, 128, 64, 32, 16, 8))

    def body(w_ref, g_ref, m_ref, v_ref, d_ref, nm_ref, nv_ref, g_out_ref):
        gv = g_ref[...]
        g_out_ref[...] = gv
        nm = ADAM_B1 * m_ref[...] + (1.0 - ADAM_B1) * gv
        nv = ADAM_B2 * v_ref[...] + (1.0 - ADAM_B2) * (gv * gv)
        m_hat = nm / (1.0 - ADAM_B1 ** ADAM_STEP)
        v_hat = nv / (1.0 - ADAM_B2 ** ADAM_STEP)
        d_ref[...] = -ADAM_LR * (m_hat / (jnp.sqrt(v_hat) + ADAM_EPS) + ADAM_WD * w_ref[...])
        nm_ref[...] = nm
        nv_ref[...] = nv

    blk = pl.BlockSpec((tr, C), lambda i: (i, 0))
    o = jax.ShapeDtypeStruct((R, C), F32)
    return pl.pallas_call(
        body, name=name, grid=(R // tr,), in_specs=[blk] * 4, out_specs=[blk] * 4, out_shape=[o, o, o, o],
        compiler_params=_params(("parallel",)),
    )(w, g, m, v)


ANY = pl.BlockSpec(memory_space=pl.ANY)
VMEM_SPEC = pl.BlockSpec(memory_space=pltpu.VMEM)


def _place():
    x, y, c = lax.axis_index("x"), lax.axis_index("y"), lax.axis_index("c")
    other_chips = [(1 - x, y), (x, 1 - y), (1 - x, 1 - y)]
    return x, y, c, other_chips


def _cast_into_full(w, ax, s_arr, name):
    R, C = w.shape
    tr = _pick(R, (256, 128))
    nr = R // tr

    def body(s_ref, w_ref, o_ref):
        o_ref[...] = w_ref[...].astype(BF16)

    if ax == 1:
        shape, o_map = (R, N_CHIPS * C), lambda i, s: (i, s[0])
    else:
        shape, o_map = (N_CHIPS * R, C), lambda i, s: (s[0] * nr + i, 0)
    return pl.pallas_call(
        body, name=name,
        grid_spec=pltpu.PrefetchScalarGridSpec(
            num_scalar_prefetch=1, grid=(nr,), in_specs=[pl.BlockSpec((tr, C), lambda i, s: (i, 0))],
            out_specs=pl.BlockSpec((tr, C), o_map)),
        out_shape=jax.ShapeDtypeStruct(shape, BF16),
        compiler_params=_params(("parallel",)),
    )(s_arr, w)


class _Gather:
    def __init__(self, fulls, shard_shapes, axes, tag):
        self.shapes, self.axes, self.tag, self.nw = shard_shapes, axes, tag, len(fulls)
        self.fulls, self.sems, self.token = list(fulls), {}, None

    def start(self, peers, after=None):
        nw, np_ = self.nw, len(peers)

        def body(*refs):
            ins, sems = refs[:nw], refs[nw + (after is not None):nw + (after is not None) + 2 * np_]
            for k, j in enumerate(peers):
                for cp in self._peer_copies(ins, sems[2 * k], sems[2 * k + 1], j):
                    cp.start()
            refs[-1][...] = jnp.zeros_like(refs[-1])

        out = pl.pallas_call(
            body, name="gather_start_%s_%s" % (self.tag, "".join(map(str, peers))),
            out_shape=(*[pltpu.SemaphoreType.DMA((nw,))] * (2 * np_),
                       *[pltpu.HBM(f.shape, f.dtype) for f in self.fulls], jax.ShapeDtypeStruct((8, LANES), F32)),
            in_specs=[HBM_SPEC] * nw + ([] if after is None else [ANY]),
            out_specs=(*[SEM_SPEC] * (2 * np_), *[HBM_SPEC] * nw, VMEM_SPEC),
            input_output_aliases={k: 2 * np_ + k for k in range(nw)},
            compiler_params=pltpu.CompilerParams(has_side_effects=DATAFLOW),
        )(*[pltpu.with_memory_space_constraint(f, pltpu.HBM) for f in self.fulls], *(() if after is None else (after,)))
        for k, j in enumerate(peers):
            self.sems[j] = (out[2 * k], out[2 * k + 1])
        self.fulls = list(out[2 * np_:2 * np_ + nw])
        self.token = out[-1]

    def _region(self, ref, i, t, half):
        R, C = self.shapes[i]
        hr = R // 2
        if self.axes[i] == 1:
            return ref.at[pl.ds(half * hr, hr), pl.ds(pl.multiple_of(t * C, LANES), C)]
        return ref.at[pl.ds(t * R + half * hr, hr), :]

    def _peer_copies(self, refs, send_sems, recv_sems, j):
        x, y, c, chips = _place()
        s = 2 * x + y
        return [pltpu.make_async_remote_copy(
            src_ref=self._region(refs[i], i, s, c), dst_ref=self._region(refs[i], i, s, c), send_sem=send_sems.at[i],
            recv_sem=recv_sems.at[i], device_id=(*chips[j], c), device_id_type=MESH) for i in range(self.nw)]

    def wait(self, peers, after):
        nw, np_ = self.nw, len(peers)

        def body(*refs):
            ins, sems = refs[:nw], refs[nw:nw + 2 * np_]
            for k, j in enumerate(peers):
                for cp in self._peer_copies(ins, sems[2 * k], sems[2 * k + 1], j):
                    cp.wait_send()
                    cp.wait_recv()

        sem_args = [s for j in peers for s in self.sems[j]]
        out = pl.pallas_call(
            body, name="gather_wait_%s_%s" % (self.tag, "".join(map(str, peers))),
            out_shape=tuple(pltpu.HBM(f.shape, f.dtype) for f in self.fulls),
            in_specs=[HBM_SPEC] * nw + [SEM_SPEC] * (2 * np_) + [ANY], out_specs=tuple([HBM_SPEC] * nw),
            input_output_aliases={k: k for k in range(nw)},
            compiler_params=pltpu.CompilerParams(has_side_effects=DATAFLOW),
        )(*self.fulls, *sem_args, after)
        self.fulls = list(out)

    def forward(self, peers):
        nw, np_ = self.nw, len(peers)

        def body(*refs):
            ins, outs = refs[:nw], refs[nw:2 * nw]
            send_sems, recv_sems = refs[2 * nw:]
            x, y, c, chips = _place()
            cps = []
            for i in range(nw):
                for k, j in enumerate(peers):
                    t = 2 * chips[j][0] + chips[j][1]
                    cp = pltpu.make_async_remote_copy(
                        src_ref=self._region(ins[i], i, t, c), dst_ref=self._region(outs[i], i, t, c),
                        send_sem=send_sems.at[i * np_ + k], recv_sem=recv_sems.at[i * np_ + k],
                        device_id=(x, y, 1 - c), device_id_type=MESH)
                    cp.start()
                    cps.append(cp)
            for cp in cps:
                cp.wait()

        out = pl.pallas_call(
            body, name="gather_forward_%s_%s" % (self.tag, "".join(map(str, peers))),
            in_specs=[ANY] * nw, out_specs=[ANY] * nw,
            out_shape=[jax.ShapeDtypeStruct(f.shape, f.dtype) for f in self.fulls],
            input_output_aliases={i: i for i in range(nw)},
            scratch_shapes=[pltpu.SemaphoreType.DMA((nw * np_,)), pltpu.SemaphoreType.DMA((nw * np_,))],
        )(*self.fulls)
        self.fulls = list(out)


def _matmul_slab(a, wfull, slab_arr, prev, after, name):
    M, K = a.shape
    N = wfull.shape[1]
    nslab = N // N_CHIPS
    tn = _pick(nslab, (2688, 896, 512, 384, 256, 128))
    tm = _pick(M, (512, 256, 128) if tn > 1024 else (1024, 512, 256, 128))
    per = nslab // tn
    extra = [e for e in (prev, after) if e is not None]

    def body(slab_ref, a_ref, b_ref, *rest):
        rest[len(extra)][...] = jnp.dot(a_ref[...], b_ref[...], preferred_element_type=F32)

    return pl.pallas_call(
        body, name=name,
        grid_spec=pltpu.PrefetchScalarGridSpec(
            num_scalar_prefetch=1, grid=(M // tm, per),
            in_specs=[pl.BlockSpec((tm, K), lambda i, j, sl: (i, 0)),
                      pl.BlockSpec((K, tn), lambda i, j, sl: (0, sl[0] * per + j))] + [ANY] * len(extra),
            out_specs=pl.BlockSpec((tm, tn), lambda i, j, sl: (i, sl[0] * per + j))),
        out_shape=jax.ShapeDtypeStruct((M, N), F32),
        input_output_aliases={} if prev is None else {3: 0},
        compiler_params=_params(("parallel", "arbitrary")),
    )(slab_arr, a, wfull, *extra)


def _exchange_sibling_halves(gs, name):
    nw = len(gs)

    def body(*refs):
        ins, outs = refs[:nw], refs[nw:2 * nw]
        send_sems, recv_sems = refs[2 * nw:]
        x, y, c, _ = _place()
        cps = []
        for i in range(nw):
            cp = pltpu.make_async_remote_copy(src_ref=ins[i].at[:, 1 - c], dst_ref=outs[i], send_sem=send_sems.at[i],
                                              recv_sem=recv_sems.at[i], device_id=(x, y, 1 - c), device_id_type=MESH)
            cp.start()
            cps.append(cp)
        for cp in cps:
            cp.wait()

    return pl.pallas_call(
        body, name=name, in_specs=[ANY] * nw, out_specs=[ANY] * nw,
        out_shape=[jax.ShapeDtypeStruct((g.shape[0],) + g.shape[2:], g.dtype) for g in gs],
        scratch_shapes=[pltpu.SemaphoreType.DMA((nw,)), pltpu.SemaphoreType.DMA((nw,))],
    )(*gs)


HBM_SPEC = pl.BlockSpec(memory_space=pltpu.HBM)
SEM_SPEC = pl.BlockSpec(memory_space=pltpu.SEMAPHORE)
DATAFLOW = pltpu.SideEffectType.DATAFLOW_SIDE_EFFECTING


def _chip_copies(ins, lands, send_sems, recv_sems):
    x, y, c, chips = _place()
    return [pltpu.make_async_remote_copy(
        src_ref=ins[i].at[2 * chip[0] + chip[1]], dst_ref=lands[i].at[j], send_sem=send_sems.at[i * 3 + j],
        recv_sem=recv_sems.at[i * 3 + j], device_id=(*chip, c), device_id_type=MESH)
        for i in range(len(ins)) for j, chip in enumerate(chips)]


def _chips_send_start(ss, name):
    nw = len(ss)
    lands = [pltpu.with_memory_space_constraint(lax.empty((N_CHIPS - 1,) + s.shape[1:], s.dtype), pltpu.HBM) for s in ss]

    def body(*refs):
        ins, land_refs = refs[:nw], refs[nw:2 * nw]
        send_sems, recv_sems = refs[2 * nw], refs[2 * nw + 1]
        token = refs[-1]
        for cp in _chip_copies(ins, land_refs, send_sems, recv_sems):
            cp.start()
        token[...] = jnp.zeros_like(token)

    n = 3 * nw
    out = pl.pallas_call(
        body, name=name,
        out_shape=(pltpu.SemaphoreType.DMA((n,)), pltpu.SemaphoreType.DMA((n,)),
                   *[pltpu.HBM(s.shape, s.dtype) for s in ss], *[pltpu.HBM(l.shape, l.dtype) for l in lands],
                   jax.ShapeDtypeStruct((8, LANES), F32)),
        in_specs=[HBM_SPEC] * (2 * nw), out_specs=(SEM_SPEC, SEM_SPEC, *[HBM_SPEC] * (2 * nw), VMEM_SPEC),
        input_output_aliases={k: 2 + k for k in range(2 * nw)},
        compiler_params=pltpu.CompilerParams(has_side_effects=DATAFLOW),
    )(*[pltpu.with_memory_space_constraint(s, pltpu.HBM) for s in ss], *lands)
    return out[0], out[1], list(out[2:2 + nw]), list(out[2 + nw:2 + 2 * nw]), out[-1]


def _chips_send_wait(send_sems, recv_sems, ss, lands, after, name):
    nw = len(ss)

    def body(*refs):
        ins, land_refs = refs[:nw], refs[nw:2 * nw]
        s_sems, r_sems = refs[2 * nw], refs[2 * nw + 1]
        for cp in _chip_copies(ins, land_refs, s_sems, r_sems):
            cp.wait_send()
            cp.wait_recv()

    out = pl.pallas_call(
        body, name=name,
        out_shape=(*[pltpu.HBM(s.shape, s.dtype) for s in ss], *[pltpu.HBM(l.shape, l.dtype) for l in lands]),
        in_specs=[HBM_SPEC] * (2 * nw) + [SEM_SPEC, SEM_SPEC, ANY], out_specs=tuple([HBM_SPEC] * (2 * nw)),
        input_output_aliases={k: k for k in range(2 * nw)},
        compiler_params=pltpu.CompilerParams(has_side_effects=DATAFLOW),
    )(*ss, *lands, send_sems, recv_sems, after)
    return list(out[nw:])


def _sum_small(pack):
    rows = pack.shape[0]

    def body(pack_ref, sum_ref, all_ref, send_sems, recv_sems):
        x, y, c, _ = _place()
        me = 4 * x + 2 * y + c
        all_ref[me] = pack_ref[...]
        cps = []
        for k in range(1, N_DEV):
            to = (1 - x if k & 4 else x, 1 - y if k & 2 else y, 1 - c if k & 1 else c)
            cp = pltpu.make_async_remote_copy(
                src_ref=pack_ref, dst_ref=all_ref.at[me], send_sem=send_sems.at[k - 1],
                recv_sem=recv_sems.at[k - 1], device_id=to, device_id_type=MESH)
            cp.start()
            cps.append(cp)
        for cp in cps:
            cp.wait()
        total = all_ref[0]
        for d in range(1, N_DEV):
            total = total + all_ref[d]
        sum_ref[...] = total

    return pl.pallas_call(
        body, name="sum_small_grads", in_specs=[VMEM_SPEC], out_specs=VMEM_SPEC,
        out_shape=jax.ShapeDtypeStruct(pack.shape, F32),
        scratch_shapes=[pltpu.VMEM((N_DEV, rows, LANES), F32), pltpu.SemaphoreType.DMA((N_DEV - 1,)),
                        pltpu.SemaphoreType.DMA((N_DEV - 1,))],
    )(pack)


def _share_with_sibling(fs, name):
    nw = len(fs)

    def body(*refs):
        ins, outs = refs[:nw], refs[nw:2 * nw]
        send_sems, recv_sems = refs[2 * nw:]
        x, y, c, _ = _place()
        cps = []
        for i in range(nw):
            cp = pltpu.make_async_remote_copy(src_ref=ins[i].at[c], dst_ref=outs[i].at[c], send_sem=send_sems.at[i],
                                              recv_sem=recv_sems.at[i], device_id=(x, y, 1 - c), device_id_type=MESH)
            cp.start()
            cps.append(cp)
        for cp in cps:
            cp.wait()

    return pl.pallas_call(
        body, name=name, in_specs=[ANY] * nw, out_specs=[ANY] * nw,
        out_shape=[jax.ShapeDtypeStruct(f.shape, f.dtype) for f in fs],
        input_output_aliases={i: i for i in range(nw)},
        scratch_shapes=[pltpu.SemaphoreType.DMA((nw,)), pltpu.SemaphoreType.DMA((nw,))],
    )(*fs)


def _sum_sibling(g, land, where_arr, name):
    P, Q = g.shape[-2:]
    tp = _pick(P, (256, 128, 64))

    def body(w_ref, g_ref, l_ref, s_ref):
        s_ref[...] = (g_ref[...].astype(F32) + l_ref[...].astype(F32)).astype(BF16)

    if g.ndim == 4:
        g_spec = pl.BlockSpec((None, None, tp, Q), lambda j, r, w: (w[1 + j], w[0], r, 0))
    else:
        g_spec = pl.BlockSpec((None, tp, Q), lambda j, r, w: (w[1 + j], r, 0))
    slab = pl.BlockSpec((None, tp, Q), lambda j, r, w: (w[1 + j], r, 0))
    return pl.pallas_call(
        body, name=name,
        grid_spec=pltpu.PrefetchScalarGridSpec(
            num_scalar_prefetch=1, grid=(N_CHIPS - 1, P // tp), in_specs=[g_spec, slab], out_specs=slab),
        out_shape=jax.ShapeDtypeStruct((N_CHIPS, P, Q), BF16),
        compiler_params=_params(("parallel", "parallel")),
    )(where_arr, g, land)


def _sum_chips(g, land, recv, sc_arr, name):
    P, Q = g.shape[-2:]
    tp = _pick(P, (256, 128, 64))

    def body(sc_ref, g_ref, l_ref, r_ref, f_ref):
        acc = g_ref[...].astype(F32) + l_ref[...].astype(F32)
        for j in range(N_CHIPS - 1):
            acc = acc + r_ref[j].astype(F32)
        f_ref[...] = acc

    if g.ndim == 4:
        g_spec = pl.BlockSpec((None, None, tp, Q), lambda r, sc: (sc[0], sc[1], r, 0))
    else:
        g_spec = pl.BlockSpec((None, tp, Q), lambda r, sc: (sc[0], r, 0))
    return pl.pallas_call(
        body, name=name,
        grid_spec=pltpu.PrefetchScalarGridSpec(
            num_scalar_prefetch=1, grid=(P // tp,),
            in_specs=[g_spec,
                      pl.BlockSpec((None, tp, Q), lambda r, sc: (sc[0], r, 0)),
                      pl.BlockSpec((N_CHIPS - 1, tp, Q), lambda r, sc: (0, r, 0))],
            out_specs=pl.BlockSpec((None, tp, Q), lambda r, sc: (sc[1], r, 0))),
        out_shape=jax.ShapeDtypeStruct((2, P, Q), F32),
        compiler_params=_params(("parallel",)),
    )(sc_arr, g, land, recv)


class _Reduction:
    def __init__(self, names, pieces, lands, flight):
        self.names, self.pieces, self.lands = names, pieces, lands
        self.send_sems, self.recv_sems, self.sums, self.zones, self.token = flight


def _reduce_start(pieces, lands, names, where_arr):
    tag = names[0] if len(names) == 1 else "branches"
    sums = [_sum_sibling(g, l, where_arr, "sum_sibling_" + nm) for g, l, nm in zip(pieces, lands, names)]
    return _Reduction(names, pieces, lands, _chips_send_start(sums, "grads_to_chips_start_" + tag))


def _wgrad_half(a, b, half_arr, name, after=None):
    K, M = a.shape
    nslab = b.shape[1] // N_CHIPS
    tm = M // 2
    tn = _pick(nslab, (896, 512, 384, 256, 128))
    per = nslab // tn
    assert 4 * K * (tm + tn) <= MATMUL_OPERAND_BYTES

    def body(h_ref, a_ref, b_ref, *rest):
        rest[-1][...] = lax.dot_general(a_ref[...], b_ref[...], TN_DIMS, preferred_element_type=F32).astype(BF16)

    return pl.pallas_call(
        body, name=name,
        grid_spec=pltpu.PrefetchScalarGridSpec(
            num_scalar_prefetch=1, grid=(b.shape[1] // tn,),
            in_specs=[pl.BlockSpec((K, tm), lambda j, h: (0, h[0])), pl.BlockSpec((K, tn), lambda j, h: (0, j))]
            + ([] if after is None else [ANY]),
            out_specs=pl.BlockSpec((None, tm, tn), lambda j, h: (j // per, 0, j % per))),
        out_shape=jax.ShapeDtypeStruct((N_CHIPS, tm, nslab), BF16),
        compiler_params=_params(("parallel",)),
    )(*((half_arr, a, b) if after is None else (half_arr, a, b, after)))


def _sibling_send_start(g, name):
    zone = pltpu.with_memory_space_constraint(lax.empty(g.shape, g.dtype), pltpu.HBM)

    def body(g_ref, zone_ref, send_sem, recv_sem, g_thru, zone_thru, token):
        x, y, c, _ = _place()
        pltpu.make_async_remote_copy(src_ref=g_ref, dst_ref=zone_ref, send_sem=send_sem, recv_sem=recv_sem,
                                     device_id=(x, y, 1 - c), device_id_type=MESH).start()
        token[...] = jnp.zeros_like(token)

    return pl.pallas_call(
        body, name=name,
        out_shape=(pltpu.SemaphoreType.DMA(()), pltpu.SemaphoreType.DMA(()), pltpu.HBM(g.shape, g.dtype),
                   pltpu.HBM(g.shape, g.dtype), jax.ShapeDtypeStruct((8, LANES), F32)),
        in_specs=[HBM_SPEC, HBM_SPEC], out_specs=(SEM_SPEC, SEM_SPEC, HBM_SPEC, HBM_SPEC, VMEM_SPEC),
        input_output_aliases={0: 2, 1: 3},
        compiler_params=pltpu.CompilerParams(has_side_effects=DATAFLOW),
    )(pltpu.with_memory_space_constraint(g, pltpu.HBM), zone)


def _sibling_send_wait(send_sem, recv_sem, g, zone, after, name):
    def body(g_ref, zone_ref, s_sem, r_sem, after_ref, g_out, zone_out):
        x, y, c, _ = _place()
        cp = pltpu.make_async_remote_copy(src_ref=g_ref, dst_ref=zone_ref, send_sem=s_sem, recv_sem=r_sem,
                                          device_id=(x, y, 1 - c), device_id_type=MESH)
        cp.wait_send()
        cp.wait_recv()

    return pl.pallas_call(
        body, name=name, out_shape=(pltpu.HBM(g.shape, g.dtype), pltpu.HBM(zone.shape, zone.dtype)),
        in_specs=[HBM_SPEC, HBM_SPEC, SEM_SPEC, SEM_SPEC, ANY], out_specs=(HBM_SPEC, HBM_SPEC),
        input_output_aliases={0: 0, 1: 1},
        compiler_params=pltpu.CompilerParams(has_side_effects=DATAFLOW),
    )(g, zone, send_sem, recv_sem, after)[1]


def _reduce_finish(red, after, sc_arr):
    tag = red.names[0] if len(red.names) == 1 else "branches"
    recvs = _chips_send_wait(red.send_sems, red.recv_sems, red.sums, red.zones, after, "grads_to_chips_wait_" + tag)
    halves = [_sum_chips(g, l, r, sc_arr, "sum_chips_" + nm) for g, l, r, nm in zip(red.pieces, red.lands, recvs, red.names)]
    return _share_with_sibling(halves, "grads_share_sibling_" + tag)


def _t5_bucket(dist):
    max_exact = REL_BUCKETS // 2
    d = jnp.maximum(dist, 0)
    df = jnp.maximum(d, 1).astype(F32)
    large = max_exact + (jnp.log(df / max_exact) / math.log(REL_MAX_DIST / max_exact)
                         * (REL_BUCKETS - max_exact)).astype(jnp.int32)
    large = jnp.minimum(large, REL_BUCKETS - 1)
    return jnp.where(d < max_exact, d, large)


def _bucket_table():
    qi = jnp.arange(WINDOW)[:, None]
    si = jnp.arange(2 * WINDOW)[None, :]
    return _t5_bucket(qi + WINDOW - si)


TILE_WORDS = 8 * LANES


def _tile_rows(shape):
    return -(-math.prod(shape) // TILE_WORDS) * 8


def _rows_of(a):
    flat = a.reshape(-1).astype(F32)
    n = _tile_rows(a.shape) * LANES
    return jnp.pad(flat, (0, n - flat.shape[0])).reshape(-1, LANES)


def _pack_rows(parts):
    return jnp.concatenate([_rows_of(p) for p in parts], axis=0)


def _unpack_rows(packed, shapes):
    out, at = [], 0
    for shp in shapes:
        n, nr = math.prod(shp), _tile_rows(shp)
        out.append(packed[at:at + nr].reshape(-1)[:n].reshape(shp))
        at += nr
    return out


def kernel(x, norm_pre, w_in, rel_bias, attn_sinks, lb_logits, hgrn_norm, w_branch_attn, w_branch_hgrn, w_out, norm_post, loss_target, m_norm_pre, m_w_in, m_rel_bias, m_attn_sinks, m_lb_logits, m_hgrn_norm, m_w_branch_attn, m_w_branch_hgrn, m_w_out, m_norm_post, v_norm_pre, v_w_in, v_rel_bias, v_attn_sinks, v_lb_logits, v_hgrn_norm, v_w_branch_attn, v_w_branch_hgrn, v_w_out, v_norm_post):
    B_loc, S, D = x.shape
    T = B_loc * S
    x2 = x.reshape(T, D)
    tgt2 = loss_target.reshape(T, D)
    my_x, my_y, my_c = lax.axis_index("x"), lax.axis_index("y"), lax.axis_index("c")

    c_arr = jnp.reshape(my_c, (1,)).astype(jnp.int32)
    s_arr = jnp.reshape(2 * my_x + my_y, (1,)).astype(jnp.int32)
    sc_arr = jnp.concatenate([s_arr, c_arr])
    shard_ws = [w_in[0], w_branch_attn[0], w_branch_hgrn[0], w_out[0]]
    shard_axes = (1, 1, 1, 0)
    names = ["w_in", "w_branch_attn", "w_branch_hgrn", "w_out"]
    placed = [_cast_into_full(w, ax, s_arr, "cast_" + nm) for w, ax, nm in zip(shard_ws, shard_axes, names)]
    peer_slabs = [jnp.reshape(t, (1,)).astype(jnp.int32)
                  for t in (2 * (1 - my_x) + my_y, 2 * my_x + 1 - my_y, 2 * (1 - my_x) + 1 - my_y)]

    buckets = _bucket_table()
    onehot = (buckets.reshape(-1)[:, None] == jnp.arange(REL_BUCKETS)[None, :]).astype(F32)
    bias_tab = jnp.dot(onehot, rel_bias.astype(F32), precision=lax.Precision.HIGHEST).T.reshape(ATTN_HEADS, WINDOW, 2 * WINDOW)
    sinks_b = jnp.broadcast_to(attn_sinks[0].astype(F32)[:, None, None], (ATTN_HEADS, 8, LANES))
    lb_fn = lambda l: jnp.cumsum(jax.nn.softmax(l.astype(F32), axis=0), axis=0)[:1]
    lb, lb_vjp = jax.vjp(lb_fn, lb_logits)
    gain_h = hgrn_norm[0].reshape(1, HGRN_WIDTH)

    h, rstd = _rmsnorm_fwd(x2, norm_pre)
    gather_in = _Gather(placed[:1], [shard_ws[0].shape], shard_axes[:1], "w_in")
    gather_in.start([0, 1])
    proj = _matmul_slab(h, gather_in.fulls[0], s_arr, None, gather_in.token, "in_proj_own")
    gather_in.wait([0, 1], proj)
    gather_in.forward([0, 1])
    gather_in.start([2])
    proj = _matmul_slab(h, gather_in.fulls[0], peer_slabs[0], proj, gather_in.token, "in_proj_peer0")
    proj = _matmul_slab(h, gather_in.fulls[0], peer_slabs[1], proj, None, "in_proj_peer1")
    gather_in.wait([2], proj)
    gather_in.forward([2])
    gather_rest = _Gather(placed[1:], [w.shape for w in shard_ws[1:]], shard_axes[1:], "rest")
    gather_rest.start([0, 1, 2], gather_in.fulls[0])
    proj = _matmul_slab(h, gather_in.fulls[0], peer_slabs[2], proj, gather_rest.token, "in_proj_peer2")
    win_f = gather_in.fulls[0]
    ya, attn_probs = _attn_fwd(proj, bias_tab, sinks_b, B_loc, S)
    yh, states = _hgrn_fwd(proj, lb, gain_h, B_loc, S)
    gather_rest.wait([0, 1, 2], yh)
    gather_rest.forward([0, 1, 2])
    wba_f, wbh_f, wout_f = gather_rest.fulls
    ua = _matmul(ya, wba_f, "nn", BF16, "branch_attn_proj")
    uh, merged = _branch_proj_merge(yh, wbh_f, proj, ua)
    yv = _matmul(merged, wout_f, "nn", BF16, "out_proj")
    dy, dout, loss_p, gnpost_p = _post_loss(yv, x2, tgt2, norm_post)

    g_wout = _matmul(merged, dy, "tn", BF16, "out_proj_wgrad")
    d_ua, d_uh, dproj = _out_dgrad_merge_bwd(dy, wout_f, proj, ua, uh)
    g_wba = _matmul(ya, d_ua, "tn", BF16, "branch_attn_wgrad", slabs=N_CHIPS)
    g_wbh = _matmul(yh, d_uh, "tn", BF16, "branch_hgrn_wgrad", slabs=N_CHIPS)
    where_arr = jnp.concatenate([c_arr] + peer_slabs)
    late_pieces = [g.reshape(N_CHIPS, 2, -1, g.shape[-1]) for g in (g_wba, g_wbh, g_wout)]
    late = _reduce_start(late_pieces, _exchange_sibling_halves(late_pieces, "grads_to_sibling_branches"), names[1:], where_arr)
    d_ya = _matmul(d_ua, wba_f, "nt", BF16, "branch_attn_dgrad", after=late.token)
    d_yh = _matmul(d_uh, wbh_f, "nt", BF16, "branch_hgrn_dgrad", after=late.token)
    dproj, dbias_p, dsinks_p = _attn_bwd(proj, attn_probs, d_ya, dproj, B_loc, S)
    dproj, dlb_p, dgain_p = _hgrn_bwd(proj, states, lb, gain_h, d_yh, dproj, B_loc, S)
    g_give = _wgrad_half(h, dproj, 1 - c_arr, "in_proj_wgrad_sibling_half")
    s_sem, r_sem, g_give, zone, token = _sibling_send_start(g_give, "grads_to_sibling_start_w_in")
    g_keep = _wgrad_half(h, dproj, c_arr, "in_proj_wgrad_own_half", after=token)
    land = _sibling_send_wait(s_sem, r_sem, g_give, zone, g_keep, "grads_to_sibling_wait_w_in")
    last = _reduce_start([g_keep], [land], names[:1], where_arr)
    dh = _matmul(dproj, win_f, "nt", BF16, "in_proj_dgrad", after=last.token)
    grad_x2, gnpre_p = _rmsnorm_bwd(dh, x2, rstd, norm_pre, dout)
    shared = _reduce_finish(last, grad_x2, sc_arr) + _reduce_finish(late, grad_x2, sc_arr)

    grelb_p = jnp.dot(dbias_p.reshape(ATTN_HEADS, -1), onehot, precision=lax.Precision.HIGHEST).T
    gsinks_p = dsinks_p[:, 0, 0]
    dlb_sum = jnp.sum(dlb_p, axis=0).reshape(1, HGRN_WIDTH)
    ghn_p = jnp.sum(dgain_p, axis=0).reshape(HGRN_HEADS, HGRN_DIM)
    small_parts = [gnpre_p, gnpost_p, grelb_p, gsinks_p, dlb_sum, ghn_p, loss_p]
    small_shapes = [p.shape for p in small_parts]
    pack_sum = _sum_small(_pack_rows(small_parts))
    big_w = [w_in, w_branch_attn, w_branch_hgrn, w_out]
    big_m = [m_w_in, m_w_branch_attn, m_w_branch_hgrn, m_w_out]
    big_v = [v_w_in, v_w_branch_attn, v_w_branch_hgrn, v_w_out]
    big = {}
    for nm, gs, w, m, v in zip(names, shared, big_w, big_m, big_v):
        shp = w.shape
        g2 = gs.reshape(shp[1], shp[2])
        d, nm_, nv_, g_out = _adamw(w[0], g2, m[0], v[0], "adamw_" + nm)
        big[nm] = tuple(a.reshape(shp) for a in (g_out, d, nm_, nv_))

    gnpre, gnpost, grelb, gsinks, dlb, ghn, loss = _unpack_rows(pack_sum, small_shapes)
    (g_lb_logits,) = lb_vjp(dlb)
    small_names = ["norm_pre", "rel_bias", "attn_sinks", "lb_logits", "hgrn_norm", "norm_post"]
    small_w = [norm_pre, rel_bias, attn_sinks, lb_logits, hgrn_norm, norm_post]
    small_m = [m_norm_pre, m_rel_bias, m_attn_sinks, m_lb_logits, m_hgrn_norm, m_norm_post]
    small_v = [v_norm_pre, v_rel_bias, v_attn_sinks, v_lb_logits, v_hgrn_norm, v_norm_post]
    small_g = [gnpre.reshape(norm_pre.shape), grelb.reshape(rel_bias.shape), gsinks.reshape(attn_sinks.shape),
               g_lb_logits.reshape(lb_logits.shape), ghn.reshape(hgrn_norm.shape), gnpost.reshape(norm_post.shape)]
    shapes = [w.shape for w in small_w]
    d_s, nm_s, nv_s, _ = _adamw(_pack_rows(small_w), _pack_rows(small_g), _pack_rows(small_m), _pack_rows(small_v),
                             "adamw_small")
    small = {}
    for nm, g, d, m_, v_ in zip(small_names, small_g, _unpack_rows(d_s, shapes), _unpack_rows(nm_s, shapes),
                                _unpack_rows(nv_s, shapes)):
        small[nm] = (g, d, m_, v_)

    allw = {**big, **small}
    order = ["norm_pre", "w_in", "rel_bias", "attn_sinks", "lb_logits", "hgrn_norm", "w_branch_attn", "w_branch_hgrn",
             "w_out", "norm_post"]
    outs = [loss.reshape(()), grad_x2.reshape(B_loc, S, D)]
    for k in range(4):
        outs += [allw[nm][k] for nm in order]
    return tuple(outs)
```

```python
import math

import jax
import jax.numpy as jnp
from jax import lax
from jax.experimental import pallas as pl
from jax.experimental.pallas import tpu as pltpu

F32 = jnp.float32
BF16 = jnp.bfloat16
MESH = pl.DeviceIdType.MESH

ATTN_HEADS = 16
ATTN_KV_HEADS = 4
HEAD_DIM = 64
GROUP = ATTN_HEADS // ATTN_KV_HEADS
WINDOW = 128
ATTN_WIDTH = ATTN_HEADS * HEAD_DIM
KV_WIDTH = ATTN_KV_HEADS * HEAD_DIM
HGRN_HEADS = 8
HGRN_DIM = 128
HGRN_WIDTH = HGRN_HEADS * HGRN_DIM
CHUNK = 64
SUB = 16
NSUB = CHUNK // SUB
REL_BUCKETS = 32
REL_MAX_DIST = 128
NORM_EPS = 1e-6
ADAM_LR = 0.001
ADAM_B1 = 0.9
ADAM_B2 = 0.999
ADAM_EPS = 1e-08
ADAM_WD = 0.01
ADAM_STEP = 10
LANES = 128
N_CHIPS = 4
N_DEV = 8
VMEM_LIMIT = 48 * 1024 * 1024
MATMUL_OPERAND_BYTES = 34 * 1024 * 1024
MATMUL_VMEM_BYTES = 44 * 1024 * 1024

OFF_AQ = 0
OFF_AK = OFF_AQ + ATTN_WIDTH
OFF_AV = OFF_AK + KV_WIDTH
OFF_AG = OFF_AV + KV_WIDTH
OFF_HQ = OFF_AG + ATTN_WIDTH
OFF_HF = OFF_HQ + HGRN_WIDTH
OFF_HI = OFF_HF + HGRN_WIDTH
OFF_HG = OFF_HI + HGRN_WIDTH
OFF_GA = OFF_HG + HGRN_WIDTH

NT_DIMS = (((1,), (1,)), ((), ()))
TN_DIMS = (((0,), (0,)), ((), ()))
NN_DIMS = (((1,), (0,)), ((), ()))


def _pick(n, cands):
    for c in cands:
        if n % c == 0:
            return c
    raise ValueError(f"no tile for {n} in {cands}")


def _params(sem):
    return pltpu.CompilerParams(dimension_semantics=sem, vmem_limit_bytes=VMEM_LIMIT)


def _bdot(a, b, dims=NN_DIMS):
    return lax.dot_general(a.astype(BF16), b.astype(BF16), dims, preferred_element_type=F32)


def _matmul(a, b, mode, out_dtype, name, slabs=1, after=None):
    if mode == "nn":
        (M, K), (K2, N) = a.shape, b.shape
    elif mode == "nt":
        (M, K), (N, K2) = a.shape, b.shape
    else:
        (K, M), (K2, N) = a.shape, b.shape
    assert K == K2
    nslab = N // slabs
    tm = _pick(M, (1024, 512, 256, 128))
    out_bytes = jnp.dtype(out_dtype).itemsize
    choices = []
    for tn in (2688, 1024, 896, 512, 384, 256, 128):
        for tk in (4096, 3584, 2048, 1792, 1536, 1024, 512, 256, 128):
            acc = 0 if tk == K else 4 * tm * tn
            if (nslab % tn == 0 and K % tk == 0 and 4 * tk * (tm + tn) <= MATMUL_OPERAND_BYTES
                    and 4 * tk * (tm + tn) + 2 * out_bytes * tm * tn + acc <= MATMUL_VMEM_BYTES):
                choices.append((K // tk > 1, -tn, tn, tk))
                break
    _, _, tn, tk = min(choices)
    nk = K // tk
    per = nslab // tn
    dims = {"nn": NN_DIMS, "nt": NT_DIMS, "tn": TN_DIMS}[mode]

    n_in = 2 if after is None else 3

    def body(*refs):
        a_ref, b_ref, o_ref, acc = refs[0], refs[1], refs[n_in], refs[n_in + 1:]
        part = lax.dot_general(a_ref[...], b_ref[...], dims, preferred_element_type=F32)
        if nk == 1:
            o_ref[...] = part.astype(o_ref.dtype)
            return
        acc_ref, = acc
        k = pl.program_id(2)

        @pl.when(k == 0)
        def _():
            acc_ref[...] = part

        @pl.when((k > 0) & (k < nk - 1))
        def _():
            acc_ref[...] += part

        @pl.when(k == nk - 1)
        def _():
            o_ref[...] = (acc_ref[...] + part).astype(o_ref.dtype)

    if mode == "tn":
        a_spec = pl.BlockSpec((tk, tm), lambda i, j, k: (k, i))
    else:
        a_spec = pl.BlockSpec((tm, tk), lambda i, j, k: (i, k))
    if mode == "nt":
        b_spec = pl.BlockSpec((tn, tk), lambda i, j, k: (j, k))
    else:
        b_spec = pl.BlockSpec((tk, tn), lambda i, j, k: (k, j))
    if slabs == 1:
        o_shape = jax.ShapeDtypeStruct((M, N), out_dtype)
        o_spec = pl.BlockSpec((tm, tn), lambda i, j, k: (i, j))
    else:
        o_shape = jax.ShapeDtypeStruct((slabs, M, nslab), out_dtype)
        o_spec = pl.BlockSpec((None, tm, tn), lambda i, j, k: (j // per, i, j % per))
    return pl.pallas_call(
        body, name=name, grid=(M // tm, N // tn, nk), in_specs=[a_spec, b_spec] + ([] if after is None else [ANY]),
        out_specs=o_spec, out_shape=o_shape,
        scratch_shapes=[pltpu.VMEM((tm, tn), F32)] if nk > 1 else [],
        compiler_params=_params(("parallel", "parallel", "arbitrary")),
    )(*((a, b) if after is None else (a, b, after)))


def _rmsnorm_fwd(x2, gain):
    T, D = x2.shape
    tr = _pick(T, (256, 128))

    def body(x_ref, g_ref, h_ref, r_ref):
        xv = x_ref[...]
        r = lax.rsqrt(jnp.mean(xv * xv, axis=-1, keepdims=True) + NORM_EPS)
        h_ref[...] = (xv * r * g_ref[...]).astype(BF16)
        r_ref[...] = r

    return pl.pallas_call(
        body, name="rmsnorm_pre_fwd", grid=(T // tr,),
        in_specs=[pl.BlockSpec((tr, D), lambda i: (i, 0)), pl.BlockSpec((1, D), lambda i: (0, 0))],
        out_specs=[pl.BlockSpec((tr, D), lambda i: (i, 0)), pl.BlockSpec((tr, 1), lambda i: (i, 0))],
        out_shape=[jax.ShapeDtypeStruct((T, D), BF16), jax.ShapeDtypeStruct((T, 1), F32)],
        compiler_params=_params(("parallel",)),
    )(x2, gain)


def _rmsnorm_bwd(dh, x2, rstd, gain, dout):
    T, D = x2.shape
    tr = _pick(T, (256, 128))

    def body(dh_ref, x_ref, r_ref, g_ref, do_ref, gx_ref, gg_ref):
        @pl.when(pl.program_id(0) == 0)
        def _():
            gg_ref[...] = jnp.zeros_like(gg_ref)

        n = x_ref[...] * r_ref[...]
        dhv = dh_ref[...].astype(F32)
        dn = dhv * g_ref[...]
        gx_ref[...] = do_ref[...] + r_ref[...] * (dn - n * jnp.mean(dn * n, axis=-1, keepdims=True))
        gg_ref[...] += jnp.sum(dhv * n, axis=0, keepdims=True)

    row = pl.BlockSpec((tr, D), lambda i: (i, 0))
    vec = pl.BlockSpec((1, D), lambda i: (0, 0))
    return pl.pallas_call(
        body, name="rmsnorm_pre_bwd", grid=(T // tr,),
        in_specs=[row, row, pl.BlockSpec((tr, 1), lambda i: (i, 0)), vec, row],
        out_specs=[row, vec],
        out_shape=[jax.ShapeDtypeStruct((T, D), F32), jax.ShapeDtypeStruct((1, D), F32)],
        compiler_params=_params(("arbitrary",)),
    )(dh, x2, rstd, gain, dout)


def _post_loss(yv, x2, tgt2, gain):
    T, D = x2.shape
    tr = _pick(T, (256, 128))

    def body(y_ref, x_ref, t_ref, g_ref, dy_ref, do_ref, loss_ref, gg_ref):
        @pl.when(pl.program_id(0) == 0)
        def _():
            gg_ref[...] = jnp.zeros_like(gg_ref)
            loss_ref[...] = jnp.zeros_like(loss_ref)

        yv_ = y_ref[...].astype(F32)
        r = lax.rsqrt(jnp.mean(yv_ * yv_, axis=-1, keepdims=True) + NORM_EPS)
        n = yv_ * r
        e = (x_ref[...] + n * g_ref[...]) - t_ref[...]
        loss_ref[...] += 0.5 * jnp.sum(jnp.mean(e * e, axis=-1, keepdims=True), axis=0, keepdims=True)
        dz = e / D
        do_ref[...] = dz
        gg_ref[...] += jnp.sum(dz * n, axis=0, keepdims=True)
        dn = dz * g_ref[...]
        dy_ref[...] = (r * (dn - n * jnp.mean(dn * n, axis=-1, keepdims=True))).astype(BF16)

    row = pl.BlockSpec((tr, D), lambda i: (i, 0))
    vec = pl.BlockSpec((1, D), lambda i: (0, 0))
    return pl.pallas_call(
        body, name="post_norm_loss", grid=(T // tr,),
        in_specs=[row, row, row, vec],
        out_specs=[row, row, pl.BlockSpec((1, 1), lambda i: (0, 0)), vec],
        out_shape=[jax.ShapeDtypeStruct((T, D), BF16), jax.ShapeDtypeStruct((T, D), F32),
                   jax.ShapeDtypeStruct((1, 1), F32), jax.ShapeDtypeStruct((1, D), F32)],
        compiler_params=_params(("arbitrary",)),
    )(yv, x2, tgt2, gain)


def _window(rows, cols, at):
    return pl.BlockSpec((pl.Element(rows), pl.Element(cols)), at)


def _gate_windows(tm, tn, D):
    return [_window(tm, tn, lambda i, j: (i * tm, pl.multiple_of(OFF_GA + j * tn, LANES))),
            _window(tm, tn, lambda i, j: (i * tm, pl.multiple_of(OFF_GA + D + j * tn, LANES)))]


def _branch_proj_merge(yh, wbh, proj, ua):
    T, K = yh.shape
    D = wbh.shape[1]
    tm, tn = _pick(T, (512, 256, 128)), _pick(D, (1024, 512, 256))

    def body(a_ref, b_ref, ga_ref, gh_ref, ua_ref, uh_ref, m_ref):
        uh = jnp.dot(a_ref[...], b_ref[...], preferred_element_type=F32).astype(BF16)
        uh_ref[...] = uh
        m_ref[...] = (jax.nn.sigmoid(ga_ref[...]) * ua_ref[...].astype(F32)
                      + jax.nn.sigmoid(gh_ref[...]) * uh.astype(F32)).astype(BF16)

    blk = pl.BlockSpec((tm, tn), lambda i, j: (i, j))
    o = jax.ShapeDtypeStruct((T, D), BF16)
    return pl.pallas_call(
        body, name="branch_hgrn_proj_merge", grid=(T // tm, D // tn),
        in_specs=[pl.BlockSpec((tm, K), lambda i, j: (i, 0)), pl.BlockSpec((K, tn), lambda i, j: (0, j))]
        + _gate_windows(tm, tn, D) + [blk],
        out_specs=[blk, blk], out_shape=[o, o],
        compiler_params=_params(("parallel", "arbitrary")),
    )(yh, wbh, proj, proj, ua)


def _out_dgrad_merge_bwd(dy, wout, proj, ua, uh):
    T, K = dy.shape
    D = wout.shape[0]
    tm, tn = _pick(T, (512, 256, 128)), _pick(D, (1024, 512, 256))
    nj = D // tn

    def body(a_ref, b_ref, ga_ref, gh_ref, ua_ref, uh_ref, dua_ref, duh_ref, dproj_ref):
        d = lax.dot_general(a_ref[...], b_ref[...], NT_DIMS, preferred_element_type=F32)
        sa = jax.nn.sigmoid(ga_ref[...])
        sh = jax.nn.sigmoid(gh_ref[...])
        dua_ref[...] = (d * sa).astype(BF16)
        duh_ref[...] = (d * sh).astype(BF16)
        dga = (d * ua_ref[...].astype(F32) * sa * (1.0 - sa)).astype(BF16)
        dgh = (d * uh_ref[...].astype(F32) * sh * (1.0 - sh)).astype(BF16)
        for jj in range(nj):
            @pl.when(pl.program_id(1) == jj)
            def _():
                dproj_ref[:, jj * tn:(jj + 1) * tn] = dga
                dproj_ref[:, D + jj * tn:D + (jj + 1) * tn] = dgh

    blk = pl.BlockSpec((tm, tn), lambda i, j: (i, j))
    o = jax.ShapeDtypeStruct((T, D), BF16)
    return pl.pallas_call(
        body, name="out_proj_dgrad_merge_bwd", grid=(T // tm, nj),
        in_specs=[pl.BlockSpec((tm, K), lambda i, j: (i, 0)), pl.BlockSpec((tn, K), lambda i, j: (j, 0))]
        + _gate_windows(tm, tn, D) + [blk, blk],
        out_specs=[blk, blk, _window(tm, 2 * D, lambda i, j: (i * tm, OFF_GA))],
        out_shape=[o, o, jax.ShapeDtypeStruct((T, proj.shape[1]), BF16)],
        compiler_params=_params(("parallel", "arbitrary")),
    )(dy, wout, proj, proj, ua, uh)


KV_PAIR = 2
PAIR_HEADS = KV_PAIR * GROUP


def _attn_mask(n):
    qi = lax.broadcasted_iota(jnp.int32, (WINDOW, 2 * WINDOW), 0)
    si = lax.broadcasted_iota(jnp.int32, (WINDOW, 2 * WINDOW), 1)
    dist = qi + WINDOW - si
    return (dist >= 0) & (dist < WINDOW) & ((si >= WINDOW) | (n > 0))


def _first_key_column(shape):
    return lax.broadcasted_iota(jnp.int32, shape, 1) == 0


def _attn_group_fwd(mask, q, k, v, ag, bias, sink):
    mask = jnp.concatenate([mask] * GROUP, axis=0)
    s = lax.dot_general(q.astype(BF16), k.astype(BF16), NT_DIMS, preferred_element_type=F32) * (HEAD_DIM ** -0.5)
    s = jnp.where(mask, s + bias, -1e30)
    m = jnp.maximum(jnp.max(s, axis=-1, keepdims=True), sink)
    p = jnp.exp(s - m)
    e_sink = jnp.exp(sink - m)
    den = jnp.sum(p, axis=-1, keepdims=True) + e_sink
    pb = p.astype(BF16)
    o = jnp.dot(pb, v.astype(BF16), preferred_element_type=F32)
    return o * (jax.nn.silu(ag) / den), pb, e_sink.astype(BF16)


def _attn_group_bwd(q, k, v, ag, kept, dout):
    e_sink = kept[:, 0:1].astype(F32)
    pb = jnp.where(_first_key_column(kept.shape), jnp.zeros_like(kept), kept)
    pf = pb.astype(F32)
    rden = 1.0 / (jnp.sum(pf, axis=-1, keepdims=True) + e_sink)
    qb, kb, vb = q.astype(BF16), k.astype(BF16), v.astype(BF16)
    o = jnp.dot(pb, vb, preferred_element_type=F32) * rden
    sg = jax.nn.sigmoid(ag)
    d_o = dout * (ag * sg)
    dag = dout * o * (sg * (1.0 + ag * (1.0 - sg)))
    d_row = jnp.sum(d_o * o, axis=-1, keepdims=True)
    ds = (pf * rden) * (lax.dot_general(d_o.astype(BF16), vb, NT_DIMS, preferred_element_type=F32) - d_row)
    dsb = ds.astype(BF16)
    dq = jnp.dot(dsb, kb, preferred_element_type=F32) * (HEAD_DIM ** -0.5)
    dk = lax.dot_general(dsb, qb, TN_DIMS, preferred_element_type=F32) * (HEAD_DIM ** -0.5)
    dv = lax.dot_general(pb, (d_o * rden).astype(BF16), TN_DIMS, preferred_element_type=F32)
    return dq, dk, dv, dag, ds, -(e_sink * rden * d_row)


def _attn_specs(B_loc, nb, order):
    qw = PAIR_HEADS * HEAD_DIM
    kw = KV_PAIR * HEAD_DIM

    def rows(g):
        b, p, n = order(*g)
        return b * nb + n

    def prev(g):
        b, p, n = order(*g)
        return b * nb + jnp.maximum(n - 1, 0)

    def pp(g):
        return order(*g)[1]

    q = pl.BlockSpec((WINDOW, qw), lambda *g: (rows(g), OFF_AQ // qw + pp(g)))
    kc = pl.BlockSpec((WINDOW, kw), lambda *g: (rows(g), OFF_AK // kw + pp(g)))
    kp = pl.BlockSpec((WINDOW, kw), lambda *g: (prev(g), OFF_AK // kw + pp(g)))
    vc = pl.BlockSpec((WINDOW, kw), lambda *g: (rows(g), OFF_AV // kw + pp(g)))
    vp = pl.BlockSpec((WINDOW, kw), lambda *g: (prev(g), OFF_AV // kw + pp(g)))
    ag = pl.BlockSpec((WINDOW, qw), lambda *g: (rows(g), OFF_AG // qw + pp(g)))
    bias = pl.BlockSpec((PAIR_HEADS, WINDOW, 2 * WINDOW), lambda *g: (pp(g), 0, 0))
    sink = pl.BlockSpec((PAIR_HEADS, 8, LANES), lambda *g: (pp(g), 0, 0))
    return [q, kc, kp, vc, vp, ag, bias, sink], rows, pp


def _attn_operands(q_ref, kc_ref, kp_ref, vc_ref, vp_ref, ag_ref, bias_ref, sink_ref, j):
    lo, hi = j * HEAD_DIM, (j + 1) * HEAD_DIM
    k = jnp.concatenate([kp_ref[:, lo:hi], kc_ref[:, lo:hi]], axis=0)
    v = jnp.concatenate([vp_ref[:, lo:hi], vc_ref[:, lo:hi]], axis=0)
    heads = [j * GROUP + g for g in range(GROUP)]
    q = jnp.concatenate([q_ref[:, h * HEAD_DIM:(h + 1) * HEAD_DIM] for h in heads], axis=0)
    ag = jnp.concatenate([ag_ref[:, h * HEAD_DIM:(h + 1) * HEAD_DIM] for h in heads], axis=0)
    bias = jnp.concatenate([bias_ref[h] for h in heads], axis=0)
    sink = jnp.concatenate([jnp.broadcast_to(sink_ref[h, 0:1, 0:1], (WINDOW, 1)) for h in heads], axis=0)
    return q, k, v, ag, bias, sink


def _attn_fwd(proj, bias_tab, sinks_b, B_loc, S):
    T = B_loc * S
    nb = S // WINDOW
    n_pairs = ATTN_KV_HEADS // KV_PAIR
    in_specs, rows, pp = _attn_specs(B_loc, nb, lambda b, p, n: (b, p, n))

    def body(q_ref, kc_ref, kp_ref, vc_ref, vp_ref, ag_ref, bias_ref, sink_ref, ya_ref, p_ref):
        mask = _attn_mask(pl.program_id(2))
        for j in range(KV_PAIR):
            out, probs, e_sink = _attn_group_fwd(
                mask, *_attn_operands(q_ref, kc_ref, kp_ref, vc_ref, vp_ref, ag_ref, bias_ref, sink_ref, j))
            for g in range(GROUP):
                h = j * GROUP + g
                blk = slice(g * WINDOW, (g + 1) * WINDOW)
                ya_ref[:, h * HEAD_DIM:(h + 1) * HEAD_DIM] = out[blk].astype(BF16)
                p_ref[:, h * 2 * WINDOW:(h + 1) * 2 * WINDOW] = probs[blk]
                p_ref[:, h * 2 * WINDOW:h * 2 * WINDOW + 1] = e_sink[blk]

    return pl.pallas_call(
        body, name="attn_fwd", grid=(B_loc, n_pairs, nb), in_specs=in_specs,
        out_specs=[pl.BlockSpec((WINDOW, PAIR_HEADS * HEAD_DIM), lambda *g: (rows(g), pp(g))),
                   pl.BlockSpec((WINDOW, PAIR_HEADS * 2 * WINDOW), lambda *g: (rows(g), pp(g)))],
        out_shape=[jax.ShapeDtypeStruct((T, ATTN_WIDTH), BF16),
                   jax.ShapeDtypeStruct((T, ATTN_HEADS * 2 * WINDOW), BF16)],
        compiler_params=_params(("parallel", "parallel", "parallel")),
    )(proj, proj, proj, proj, proj, proj, bias_tab, sinks_b)


def _attn_bwd(proj, probs, dya, dproj, B_loc, S):
    nb = S // WINDOW
    n_pairs = ATTN_KV_HEADS // KV_PAIR
    qw, kw = PAIR_HEADS * HEAD_DIM, KV_PAIR * HEAD_DIM

    def at(off, width, back=0):
        return lambda b, i, p: ((b * nb + jnp.maximum(nb - 1 - i - back, 0)) * WINDOW,
                                pl.multiple_of(off + p * width, LANES))

    in_specs = [_window(WINDOW, qw, at(OFF_AQ, qw)), _window(WINDOW, kw, at(OFF_AK, kw)),
                _window(WINDOW, kw, at(OFF_AK, kw, 1)), _window(WINDOW, kw, at(OFF_AV, kw)),
                _window(WINDOW, kw, at(OFF_AV, kw, 1)), _window(WINDOW, qw, at(OFF_AG, qw)),
                pl.BlockSpec((WINDOW, PAIR_HEADS * 2 * WINDOW), lambda b, i, p: (b * nb + nb - 1 - i, p)),
                pl.BlockSpec((WINDOW, qw), lambda b, i, p: (b * nb + nb - 1 - i, p)), ANY]

    def body(q_ref, kc_ref, kp_ref, vc_ref, vp_ref, ag_ref, p_ref, dya_ref, dproj_in,
             dproj_ref, dbias_ref, dsink_ref, dkc_ref, dvc_ref):
        b, i, p = pl.program_id(0), pl.program_id(1), pl.program_id(2)

        @pl.when((b == 0) & (i == 0) & (p == 0))
        def _():
            dbias_ref[...] = jnp.zeros_like(dbias_ref)
            dsink_ref[...] = jnp.zeros_like(dsink_ref)

        @pl.when(i == 0)
        def _():
            dkc_ref[p] = jnp.zeros((WINDOW, kw), F32)
            dvc_ref[p] = jnp.zeros((WINDOW, kw), F32)

        dk_carry, dv_carry = dkc_ref[p], dvc_ref[p]
        dqs, dags, dbiases, dsinks, dks, dvs = [], [], [], [], [], []
        for j in range(KV_PAIR):
            heads = [j * GROUP + g for g in range(GROUP)]
            lo, hi = j * HEAD_DIM, (j + 1) * HEAD_DIM
            stack = lambda parts: jnp.concatenate(parts, axis=0)
            k = stack([kp_ref[:, lo:hi], kc_ref[:, lo:hi]])
            v = stack([vp_ref[:, lo:hi], vc_ref[:, lo:hi]])
            q = stack([q_ref[:, h * HEAD_DIM:(h + 1) * HEAD_DIM] for h in heads])
            ag = stack([ag_ref[:, h * HEAD_DIM:(h + 1) * HEAD_DIM] for h in heads])
            dout = stack([dya_ref[:, h * HEAD_DIM:(h + 1) * HEAD_DIM].astype(F32) for h in heads])
            kept = stack([p_ref[:, h * 2 * WINDOW:(h + 1) * 2 * WINDOW] for h in heads])
            dq, dk, dv, dag, dbias, dsink = _attn_group_bwd(q, k, v, ag, kept, dout)
            dks.append((dk[WINDOW:] + dk_carry[:, lo:hi]).astype(BF16))
            dvs.append((dv[WINDOW:] + dv_carry[:, lo:hi]).astype(BF16))
            dkc_ref[p, :, lo:hi] = dk[:WINDOW]
            dvc_ref[p, :, lo:hi] = dv[:WINDOW]
            for g in range(GROUP):
                blk = slice(g * WINDOW, (g + 1) * WINDOW)
                dqs.append(dq[blk].astype(BF16))
                dags.append(dag[blk].astype(BF16))
                dbiases.append(dbias[blk])
                dsinks.append(jnp.broadcast_to(jnp.sum(dsink[blk], axis=0, keepdims=True), (8, LANES)))

        for pair in range(n_pairs):
            @pl.when(p == pair)
            def _():
                for j in range(KV_PAIR):
                    col = (pair * KV_PAIR + j) * HEAD_DIM
                    dproj_ref[:, OFF_AK + col:OFF_AK + col + HEAD_DIM] = dks[j]
                    dproj_ref[:, OFF_AV + col:OFF_AV + col + HEAD_DIM] = dvs[j]
                for hh in range(PAIR_HEADS):
                    h = pair * PAIR_HEADS + hh
                    dproj_ref[:, OFF_AQ + h * HEAD_DIM:OFF_AQ + (h + 1) * HEAD_DIM] = dqs[hh]
                    dproj_ref[:, OFF_AG + h * HEAD_DIM:OFF_AG + (h + 1) * HEAD_DIM] = dags[hh]
                    dbias_ref[h] += dbiases[hh]
                    dsink_ref[h] += dsinks[hh]

    return pl.pallas_call(
        body, name="attn_bwd", grid=(B_loc, nb, n_pairs), in_specs=in_specs,
        out_specs=[_window(WINDOW, OFF_HQ, lambda b, i, p: ((b * nb + nb - 1 - i) * WINDOW, 0)),
                   pl.BlockSpec((ATTN_HEADS, WINDOW, 2 * WINDOW), lambda b, i, p: (0, 0, 0)),
                   pl.BlockSpec((ATTN_HEADS, 8, LANES), lambda b, i, p: (0, 0, 0))],
        out_shape=[jax.ShapeDtypeStruct(dproj.shape, dproj.dtype),
                   jax.ShapeDtypeStruct((ATTN_HEADS, WINDOW, 2 * WINDOW), F32),
                   jax.ShapeDtypeStruct((ATTN_HEADS, 8, LANES), F32)],
        input_output_aliases={8: 0},
        scratch_shapes=[pltpu.VMEM((n_pairs, WINDOW, kw), F32), pltpu.VMEM((n_pairs, WINDOW, kw), F32)],
        compiler_params=_params(("arbitrary", "arbitrary", "arbitrary")),
    )(proj, proj, proj, proj, proj, proj, probs, dya, dproj)


class _HgrnPre:
    def __init__(self, fr, qr, lb, g_scr):
        t = lax.broadcasted_iota(jnp.int32, (CHUNK, CHUNK), 0)
        s = lax.broadcasted_iota(jnp.int32, (CHUNK, CHUNK), 1)
        self.sg = jax.nn.sigmoid(fr)
        self.f = lb + (1.0 - lb) * self.sg
        g = jnp.dot((t >= s).astype(F32), jnp.log(self.f), precision=lax.Precision.HIGHEST, preferred_element_type=F32)
        g_scr[...] = g
        self.g = g
        self.row = lax.broadcasted_iota(jnp.int32, g.shape, 0)
        self.refs = [jnp.zeros((1, g.shape[1]), F32)] + [g_scr[pl.ds(i * SUB - 1, 1), :] for i in range(1, NSUB)]
        self.gend = g_scr[pl.ds(CHUNK - 1, 1), :]
        refrow = jnp.zeros_like(g)
        for i in range(1, NSUB):
            refrow = jnp.where(self.row >= i * SUB, self.refs[i], refrow)
        self.sigq = jax.nn.sigmoid(qr)
        self.qs = qr * self.sigq
        self.k = 1.0 - self.f
        self.eg = jnp.exp(g)
        self.eqd = jnp.exp(g - refrow)
        self.ekd = [jnp.exp(jnp.where(self.row < (i + 1) * SUB, self.refs[i] - g, 0.0)) for i in range(NSUB)]
        self.ekdec = jnp.exp(self.gend - g)
        self.qg = self.qs * self.eg
        self.qd = self.qs * self.eqd
        self.kd = [self.k * e for e in self.ekd]
        self.kdec = self.k * self.ekdec
        self.egend = jnp.exp(self.gend)


def _hgrn_pair_mask():
    t = lax.broadcasted_iota(jnp.int32, (CHUNK, NSUB * CHUNK), 0)
    col = lax.broadcasted_iota(jnp.int32, (CHUNK, NSUB * CHUNK), 1)
    return ((t // SUB) == (col // CHUNK)) & ((col % CHUNK) <= t)


def _hgrn_head_out(p, lanes, state_t, v, mask):
    qg, qd = p.qg[:, lanes], p.qd[:, lanes]
    kall = jnp.concatenate([kd[:, lanes] for kd in p.kd], axis=0)
    vst = jnp.concatenate([v] * NSUB, axis=0)
    am = jnp.where(mask, _bdot(qd, kall, NT_DIMS), 0.0)
    o = _bdot(qg, state_t, NT_DIMS) + _bdot(am, vst)
    return o, (qg, qd, kall, am, vst)


def _hgrn_fwd(proj, lb, gain, B_loc, S):
    T = B_loc * S
    nc = S // CHUNK
    nh = HGRN_HEADS

    def at(off):
        return lambda b, n: ((b * nc + n) * CHUNK, off)

    vec = pl.BlockSpec((1, HGRN_WIDTH), lambda b, n: (0, 0))

    def body(q_ref, f_ref, v_ref, hg_ref, lb_ref, gain_ref, yh_ref, st_ref, state_scr, g_scr):
        @pl.when(pl.program_id(1) == 0)
        def _():
            state_scr[...] = jnp.zeros_like(state_scr)

        p = _HgrnPre(f_ref[...], q_ref[...], lb_ref[...], g_scr)
        v = v_ref[...]
        gate = gain_ref[...] * jax.nn.silu(hg_ref[...])
        mask = _hgrn_pair_mask()
        for hd in range(nh):
            lanes = slice(hd * HGRN_DIM, (hd + 1) * HGRN_DIM)
            st = state_scr[hd]
            st_ref[hd] = st
            o, _ = _hgrn_head_out(p, lanes, st, v[:, lanes], mask)
            rs = lax.rsqrt(jnp.mean(o * o, axis=-1, keepdims=True) + NORM_EPS)
            yh_ref[:, lanes] = (o * rs * gate[:, lanes]).astype(BF16)
            state_scr[hd] = st * p.egend[:, lanes] + _bdot(v[:, lanes], p.kdec[:, lanes], TN_DIMS)

    return pl.pallas_call(
        body, name="hgrn_fwd", grid=(B_loc, nc),
        in_specs=[_window(CHUNK, HGRN_WIDTH, at(OFF_HQ)), _window(CHUNK, HGRN_WIDTH, at(OFF_HF)),
                  _window(CHUNK, HGRN_WIDTH, at(OFF_HI)), _window(CHUNK, HGRN_WIDTH, at(OFF_HG)), vec, vec],
        out_specs=[pl.BlockSpec((CHUNK, HGRN_WIDTH), lambda b, n: (b * nc + n, 0)),
                   pl.BlockSpec((None, None, nh, HGRN_DIM, HGRN_DIM), lambda b, n: (b, n, 0, 0, 0))],
        out_shape=[jax.ShapeDtypeStruct((T, HGRN_WIDTH), BF16),
                   jax.ShapeDtypeStruct((B_loc, nc, nh, HGRN_DIM, HGRN_DIM), F32)],
        scratch_shapes=[pltpu.VMEM((nh, HGRN_DIM, HGRN_DIM), F32), pltpu.VMEM((CHUNK, HGRN_WIDTH), F32)],
        compiler_params=_params(("parallel", "arbitrary")),
    )(proj, proj, proj, proj, lb, gain)


def _hgrn_bwd(proj, states, lb, gain, dyh, dproj, B_loc, S):
    nc = S // CHUNK
    nh = HGRN_HEADS

    def at(off):
        return lambda b, i: ((b * nc + nc - 1 - i) * CHUNK, off)

    vec = pl.BlockSpec((1, HGRN_WIDTH), lambda b, i: (0, 0))
    in_specs = [_window(CHUNK, HGRN_WIDTH, at(OFF_HQ)), _window(CHUNK, HGRN_WIDTH, at(OFF_HF)),
                _window(CHUNK, HGRN_WIDTH, at(OFF_HI)), _window(CHUNK, HGRN_WIDTH, at(OFF_HG)), vec, vec,
                pl.BlockSpec((None, None, nh, HGRN_DIM, HGRN_DIM), lambda b, i: (b, nc - 1 - i, 0, 0, 0)),
                pl.BlockSpec((CHUNK, HGRN_WIDTH), lambda b, i: (b * nc + nc - 1 - i, 0)), ANY]
    acc_spec = pl.BlockSpec((None, 1, HGRN_WIDTH), lambda b, i: (b, 0, 0))

    def body(q_ref, f_ref, v_ref, hg_ref, lb_ref, gain_ref, st_ref, dyh_ref, dproj_in,
             dproj_ref, dlb_ref, dgain_ref, dstate_scr, g_scr, dg_scr):
        dq_ref, df_ref, dv_ref, dhg_ref = [dproj_ref.at[:, pl.ds(k * HGRN_WIDTH, HGRN_WIDTH)] for k in range(4)]

        @pl.when(pl.program_id(1) == 0)
        def _():
            dstate_scr[...] = jnp.zeros_like(dstate_scr)
            dlb_ref[...] = jnp.zeros_like(dlb_ref)
            dgain_ref[...] = jnp.zeros_like(dgain_ref)

        qr, lb, gain, hg, v = q_ref[...], lb_ref[...], gain_ref[...], hg_ref[...], v_ref[...]
        p = _HgrnPre(f_ref[...], qr, lb, g_scr)
        sgh = jax.nn.sigmoid(hg)
        sil = hg * sgh
        dy = dyh_ref[...].astype(F32)
        mask = _hgrn_pair_mask()
        dqg, dqd, dkdec, dv, dhg, dgend, dgain = [], [], [], [], [], [], []
        dkd = [[] for _ in range(NSUB)]
        heads = [slice(hd * HGRN_DIM, (hd + 1) * HGRN_DIM) for hd in range(nh)]
        sts = [st_ref[hd] for hd in range(nh)]
        dnews = [dstate_scr[hd] for hd in range(nh)]
        fwd = [_hgrn_head_out(p, lanes, st, v[:, lanes], mask) for lanes, st in zip(heads, sts)]
        for lanes, st, dnew in zip(heads, sts, dnews):
            dkdec_h = _bdot(v[:, lanes], dnew)
            dkdec.append(dkdec_h)
            dgend.append(jnp.sum(dkdec_h * p.kdec[:, lanes], axis=0, keepdims=True)
                         + jnp.sum(dnew * st, axis=0, keepdims=True) * p.egend[:, lanes])
        dos = []
        gate_grad = sgh * (1.0 + hg * (1.0 - sgh))
        for lanes, (o, _) in zip(heads, fwd):
            rs = lax.rsqrt(jnp.mean(o * o, axis=-1, keepdims=True) + NORM_EPS)
            n = o * rs
            dyn = dy[:, lanes] * n
            dgain.append(jnp.sum(dyn * sil[:, lanes], axis=0, keepdims=True))
            dhg.append(dyn * gain[:, lanes] * gate_grad[:, lanes])
            dn = dy[:, lanes] * gain[:, lanes] * sil[:, lanes]
            dos.append(rs * (dn - n * jnp.mean(dn * n, axis=-1, keepdims=True)))
        drs = []
        for hd, (lanes, st, dnew, do, (_, (qg, qd, kall, am, vst))) in enumerate(zip(heads, sts, dnews, dos, fwd)):
            dqg.append(_bdot(do, st))
            dstate_scr[hd] = _bdot(do, qg, TN_DIMS) + dnew * p.egend[:, lanes]
            drs.append(jnp.where(mask, _bdot(do, vst, NT_DIMS), 0.0))
            dvst = _bdot(am, do, TN_DIMS)
            dv.append(sum(dvst[i * CHUNK:(i + 1) * CHUNK] for i in range(NSUB)) + _bdot(p.kdec[:, lanes], dnew, NT_DIMS))
        for dr, (_, (qg, qd, kall, am, vst)) in zip(drs, fwd):
            dqd.append(_bdot(dr, kall))
            dkall = _bdot(dr, qd, TN_DIMS)
            for i in range(NSUB):
                dkd[i].append(dkall[i * CHUNK:(i + 1) * CHUNK])

        wide = lambda parts: jnp.concatenate(parts, axis=1)
        dqg, dqd, dkdec = wide(dqg), wide(dqd), wide(dkdec)
        t2 = dqd * p.qd
        dg = dqg * p.qg + t2 - dkdec * p.kdec
        dk = dkdec * p.ekdec
        dg_scr[...] = jnp.zeros_like(dg_scr)
        for i in range(NSUB):
            dkd_i = wide(dkd[i])
            tk = jnp.where(p.row < (i + 1) * SUB, dkd_i * p.kd[i], 0.0)
            dg = dg - tk
            dk = dk + dkd_i * p.ekd[i]
            if i >= 1:
                in_blk = (p.row >= i * SUB) & (p.row < (i + 1) * SUB)
                dg_scr[pl.ds(i * SUB - 1, 1), :] = (jnp.sum(tk, axis=0, keepdims=True)
                                                    - jnp.sum(jnp.where(in_blk, t2, 0.0), axis=0, keepdims=True))
        dg_scr[pl.ds(CHUNK - 1, 1), :] = wide(dgend)
        t = lax.broadcasted_iota(jnp.int32, (CHUNK, CHUNK), 0)
        s = lax.broadcasted_iota(jnp.int32, (CHUNK, CHUNK), 1)
        dlogf = jnp.dot((t <= s).astype(F32), dg + dg_scr[...], precision=lax.Precision.HIGHEST, preferred_element_type=F32)
        df = dlogf / p.f - dk
        df_ref[...] = (df * (1.0 - lb) * p.sg * (1.0 - p.sg)).astype(BF16)
        dlb_ref[...] += jnp.sum(df * (1.0 - p.sg), axis=0, keepdims=True)
        dq_ref[...] = ((dqg * p.eg + dqd * p.eqd) * p.sigq * (1.0 + qr * (1.0 - p.sigq))).astype(BF16)
        dv_ref[...] = wide(dv).astype(BF16)
        dhg_ref[...] = wide(dhg).astype(BF16)
        dgain_ref[...] += wide(dgain)

    acc = jax.ShapeDtypeStruct((B_loc, 1, HGRN_WIDTH), F32)
    return pl.pallas_call(
        body, name="hgrn_bwd", grid=(B_loc, nc), in_specs=in_specs,
        out_specs=[_window(CHUNK, 4 * HGRN_WIDTH, at(OFF_HQ)), acc_spec, acc_spec],
        out_shape=[jax.ShapeDtypeStruct(dproj.shape, dproj.dtype), acc, acc],
        input_output_aliases={8: 0},
        scratch_shapes=[pltpu.VMEM((nh, HGRN_DIM, HGRN_DIM), F32), pltpu.VMEM((CHUNK, HGRN_WIDTH), F32),
                        pltpu.VMEM((CHUNK, HGRN_WIDTH), F32)],
        compiler_params=_params(("parallel", "arbitrary")),
    )(proj, proj, proj, proj, lb, gain, states, dyh, dproj)


def _adamw(w, g, m, v, name):
    R, C = w.shape
    tr = _pick(R, (128, 64, 32, 16, 8)) if C > 1024 else _pick(R, (512, 256, 128, 64, 32, 16, 8))

    def body(w_ref, g_ref, m_ref, v_ref, d_ref, nm_ref, nv_ref, g_out_ref):
        gv = g_ref[...]
        g_out_ref[...] = gv
        nm = ADAM_B1 * m_ref[...] + (1.0 - ADAM_B1) * gv
        nv = ADAM_B2 * v_ref[...] + (1.0 - ADAM_B2) * (gv * gv)
        m_hat = nm / (1.0 - ADAM_B1 ** ADAM_STEP)
        v_hat = nv / (1.0 - ADAM_B2 ** ADAM_STEP)
        d_ref[...] = -ADAM_LR * (m_hat / (jnp.sqrt(v_hat) + ADAM_EPS) + ADAM_WD * w_ref[...])
        nm_ref[...] = nm
        nv_ref[...] = nv

    blk = pl.BlockSpec((tr, C), lambda i: (i, 0))
    o = jax.ShapeDtypeStruct((R, C), F32)
    return pl.pallas_call(
        body, name=name, grid=(R // tr,), in_specs=[blk] * 4, out_specs=[blk] * 4, out_shape=[o, o, o, o],
        compiler_params=_params(("parallel",)),
    )(w, g, m, v)


ANY = pl.BlockSpec(memory_space=pl.ANY)
VMEM_SPEC = pl.BlockSpec(memory_space=pltpu.VMEM)


def _place():
    x, y, c = lax.axis_index("x"), lax.axis_index("y"), lax.axis_index("c")
    other_chips = [(1 - x, y), (x, 1 - y), (1 - x, 1 - y)]
    return x, y, c, other_chips


def _cast_into_full(w, ax, s_arr, name):
    R, C = w.shape
    tr = _pick(R, (256, 128))
    nr = R // tr

    def body(s_ref, w_ref, o_ref):
        o_ref[...] = w_ref[...].astype(BF16)

    if ax == 1:
        shape, o_map = (R, N_CHIPS * C), lambda i, s: (i, s[0])
    else:
        shape, o_map = (N_CHIPS * R, C), lambda i, s: (s[0] * nr + i, 0)
    return pl.pallas_call(
        body, name=name,
        grid_spec=pltpu.PrefetchScalarGridSpec(
            num_scalar_prefetch=1, grid=(nr,), in_specs=[pl.BlockSpec((tr, C), lambda i, s: (i, 0))],
            out_specs=pl.BlockSpec((tr, C), o_map)),
        out_shape=jax.ShapeDtypeStruct(shape, BF16),
        compiler_params=_params(("parallel",)),
    )(s_arr, w)


class _Gather:
    def __init__(self, fulls, shard_shapes, axes, tag):
        self.shapes, self.axes, self.tag, self.nw = shard_shapes, axes, tag, len(fulls)
        self.fulls, self.sems, self.token = list(fulls), {}, None

    def start(self, peers, after=None):
        nw, np_ = self.nw, len(peers)

        def body(*refs):
            ins, sems = refs[:nw], refs[nw + (after is not None):nw + (after is not None) + 2 * np_]
            for k, j in enumerate(peers):
                for cp in self._peer_copies(ins, sems[2 * k], sems[2 * k + 1], j):
                    cp.start()
            refs[-1][...] = jnp.zeros_like(refs[-1])

        out = pl.pallas_call(
            body, name="gather_start_%s_%s" % (self.tag, "".join(map(str, peers))),
            out_shape=(*[pltpu.SemaphoreType.DMA((nw,))] * (2 * np_),
                       *[pltpu.HBM(f.shape, f.dtype) for f in self.fulls], jax.ShapeDtypeStruct((8, LANES), F32)),
            in_specs=[HBM_SPEC] * nw + ([] if after is None else [ANY]),
            out_specs=(*[SEM_SPEC] * (2 * np_), *[HBM_SPEC] * nw, VMEM_SPEC),
            input_output_aliases={k: 2 * np_ + k for k in range(nw)},
            compiler_params=pltpu.CompilerParams(has_side_effects=DATAFLOW),
        )(*[pltpu.with_memory_space_constraint(f, pltpu.HBM) for f in self.fulls], *(() if after is None else (after,)))
        for k, j in enumerate(peers):
            self.sems[j] = (out[2 * k], out[2 * k + 1])
        self.fulls = list(out[2 * np_:2 * np_ + nw])
        self.token = out[-1]

    def _region(self, ref, i, t, half):
        R, C = self.shapes[i]
        hr = R // 2
        if self.axes[i] == 1:
            return ref.at[pl.ds(half * hr, hr), pl.ds(pl.multiple_of(t * C, LANES), C)]
        return ref.at[pl.ds(t * R + half * hr, hr), :]

    def _peer_copies(self, refs, send_sems, recv_sems, j):
        x, y, c, chips = _place()
        s = 2 * x + y
        return [pltpu.make_async_remote_copy(
            src_ref=self._region(refs[i], i, s, c), dst_ref=self._region(refs[i], i, s, c), send_sem=send_sems.at[i],
            recv_sem=recv_sems.at[i], device_id=(*chips[j], c), device_id_type=MESH) for i in range(self.nw)]

    def wait(self, peers, after):
        nw, np_ = self.nw, len(peers)

        def body(*refs):
            ins, sems = refs[:nw], refs[nw:nw + 2 * np_]
            for k, j in enumerate(peers):
                for cp in self._peer_copies(ins, sems[2 * k], sems[2 * k + 1], j):
                    cp.wait_send()
                    cp.wait_recv()

        sem_args = [s for j in peers for s in self.sems[j]]
        out = pl.pallas_call(
            body, name="gather_wait_%s_%s" % (self.tag, "".join(map(str, peers))),
            out_shape=tuple(pltpu.HBM(f.shape, f.dtype) for f in self.fulls),
            in_specs=[HBM_SPEC] * nw + [SEM_SPEC] * (2 * np_) + [ANY], out_specs=tuple([HBM_SPEC] * nw),
            input_output_aliases={k: k for k in range(nw)},
            compiler_params=pltpu.CompilerParams(has_side_effects=DATAFLOW),
        )(*self.fulls, *sem_args, after)
        self.fulls = list(out)

    def forward(self, peers):
        nw, np_ = self.nw, len(peers)

        def body(*refs):
            ins, outs = refs[:nw], refs[nw:2 * nw]
            send_sems, recv_sems = refs[2 * nw:]
            x, y, c, chips = _place()
            cps = []
            for i in range(nw):
                for k, j in enumerate(peers):
                    t = 2 * chips[j][0] + chips[j][1]
                    cp = pltpu.make_async_remote_copy(
                        src_ref=self._region(ins[i], i, t, c), dst_ref=self._region(outs[i], i, t, c),
                        send_sem=send_sems.at[i * np_ + k], recv_sem=recv_sems.at[i * np_ + k],
                        device_id=(x, y, 1 - c), device_id_type=MESH)
                    cp.start()
                    cps.append(cp)
            for cp in cps:
                cp.wait()

        out = pl.pallas_call(
            body, name="gather_forward_%s_%s" % (self.tag, "".join(map(str, peers))),
            in_specs=[ANY] * nw, out_specs=[ANY] * nw,
            out_shape=[jax.ShapeDtypeStruct(f.shape, f.dtype) for f in self.fulls],
            input_output_aliases={i: i for i in range(nw)},
            scratch_shapes=[pltpu.SemaphoreType.DMA((nw * np_,)), pltpu.SemaphoreType.DMA((nw * np_,))],
        )(*self.fulls)
        self.fulls = list(out)


def _matmul_slab(a, wfull, slab_arr, prev, after, name):
    M, K = a.shape
    N = wfull.shape[1]
    nslab = N // N_CHIPS
    tn = _pick(nslab, (2688, 896, 512, 384, 256, 128))
    tm = _pick(M, (512, 256, 128) if tn > 1024 else (1024, 512, 256, 128))
    per = nslab // tn
    extra = [e for e in (prev, after) if e is not None]

    def body(slab_ref, a_ref, b_ref, *rest):
        rest[len(extra)][...] = jnp.dot(a_ref[...], b_ref[...], preferred_element_type=F32)

    return pl.pallas_call(
        body, name=name,
        grid_spec=pltpu.PrefetchScalarGridSpec(
            num_scalar_prefetch=1, grid=(M // tm, per),
            in_specs=[pl.BlockSpec((tm, K), lambda i, j, sl: (i, 0)),
                      pl.BlockSpec((K, tn), lambda i, j, sl: (0, sl[0] * per + j))] + [ANY] * len(extra),
            out_specs=pl.BlockSpec((tm, tn), lambda i, j, sl: (i, sl[0] * per + j))),
        out_shape=jax.ShapeDtypeStruct((M, N), F32),
        input_output_aliases={} if prev is None else {3: 0},
        compiler_params=_params(("parallel", "arbitrary")),
    )(slab_arr, a, wfull, *extra)


def _exchange_sibling_halves(gs, name):
    nw = len(gs)

    def body(*refs):
        ins, outs = refs[:nw], refs[nw:2 * nw]
        send_sems, recv_sems = refs[2 * nw:]
        x, y, c, _ = _place()
        cps = []
        for i in range(nw):
            cp = pltpu.make_async_remote_copy(src_ref=ins[i].at[:, 1 - c], dst_ref=outs[i], send_sem=send_sems.at[i],
                                              recv_sem=recv_sems.at[i], device_id=(x, y, 1 - c), device_id_type=MESH)
            cp.start()
            cps.append(cp)
        for cp in cps:
            cp.wait()

    return pl.pallas_call(
        body, name=name, in_specs=[ANY] * nw, out_specs=[ANY] * nw,
        out_shape=[jax.ShapeDtypeStruct((g.shape[0],) + g.shape[2:], g.dtype) for g in gs],
        scratch_shapes=[pltpu.SemaphoreType.DMA((nw,)), pltpu.SemaphoreType.DMA((nw,))],
    )(*gs)


HBM_SPEC = pl.BlockSpec(memory_space=pltpu.HBM)
SEM_SPEC = pl.BlockSpec(memory_space=pltpu.SEMAPHORE)
DATAFLOW = pltpu.SideEffectType.DATAFLOW_SIDE_EFFECTING


def _chip_copies(ins, lands, send_sems, recv_sems):
    x, y, c, chips = _place()
    return [pltpu.make_async_remote_copy(
        src_ref=ins[i].at[2 * chip[0] + chip[1]], dst_ref=lands[i].at[j], send_sem=send_sems.at[i * 3 + j],
        recv_sem=recv_sems.at[i * 3 + j], device_id=(*chip, c), device_id_type=MESH)
        for i in range(len(ins)) for j, chip in enumerate(chips)]


def _chips_send_start(ss, name):
    nw = len(ss)
    lands = [pltpu.with_memory_space_constraint(lax.empty((N_CHIPS - 1,) + s.shape[1:], s.dtype), pltpu.HBM) for s in ss]

    def body(*refs):
        ins, land_refs = refs[:nw], refs[nw:2 * nw]
        send_sems, recv_sems = refs[2 * nw], refs[2 * nw + 1]
        token = refs[-1]
        for cp in _chip_copies(ins, land_refs, send_sems, recv_sems):
            cp.start()
        token[...] = jnp.zeros_like(token)

    n = 3 * nw
    out = pl.pallas_call(
        body, name=name,
        out_shape=(pltpu.SemaphoreType.DMA((n,)), pltpu.SemaphoreType.DMA((n,)),
                   *[pltpu.HBM(s.shape, s.dtype) for s in ss], *[pltpu.HBM(l.shape, l.dtype) for l in lands],
                   jax.ShapeDtypeStruct((8, LANES), F32)),
        in_specs=[HBM_SPEC] * (2 * nw), out_specs=(SEM_SPEC, SEM_SPEC, *[HBM_SPEC] * (2 * nw), VMEM_SPEC),
        input_output_aliases={k: 2 + k for k in range(2 * nw)},
        compiler_params=pltpu.CompilerParams(has_side_effects=DATAFLOW),
    )(*[pltpu.with_memory_space_constraint(s, pltpu.HBM) for s in ss], *lands)
    return out[0], out[1], list(out[2:2 + nw]), list(out[2 + nw:2 + 2 * nw]), out[-1]


def _chips_send_wait(send_sems, recv_sems, ss, lands, after, name):
    nw = len(ss)

    def body(*refs):
        ins, land_refs = refs[:nw], refs[nw:2 * nw]
        s_sems, r_sems = refs[2 * nw], refs[2 * nw + 1]
        for cp in _chip_copies(ins, land_refs, s_sems, r_sems):
            cp.wait_send()
            cp.wait_recv()

    out = pl.pallas_call(
        body, name=name,
        out_shape=(*[pltpu.HBM(s.shape, s.dtype) for s in ss], *[pltpu.HBM(l.shape, l.dtype) for l in lands]),
        in_specs=[HBM_SPEC] * (2 * nw) + [SEM_SPEC, SEM_SPEC, ANY], out_specs=tuple([HBM_SPEC] * (2 * nw)),
        input_output_aliases={k: k for k in range(2 * nw)},
        compiler_params=pltpu.CompilerParams(has_side_effects=DATAFLOW),
    )(*ss, *lands, send_sems, recv_sems, after)
    return list(out[nw:])


def _share_and_sum_small(fs, pack, name):
    nw = len(fs)
    rows = pack.shape[0]

    def body(*refs):
        ins, pack_ref = refs[:nw], refs[nw]
        outs, sum_ref = refs[nw + 1:2 * nw + 1], refs[2 * nw + 1]
        all_ref, send_sems, recv_sems = refs[2 * nw + 2:]
        x, y, c, _ = _place()
        me = 4 * x + 2 * y + c
        cps = []
        for i in range(nw):
            cp = pltpu.make_async_remote_copy(src_ref=ins[i].at[c], dst_ref=outs[i].at[c], send_sem=send_sems.at[i],
                                              recv_sem=recv_sems.at[i], device_id=(x, y, 1 - c), device_id_type=MESH)
            cp.start()
            cps.append(cp)
        all_ref[me] = pack_ref[...]
        for k in range(1, N_DEV):
            to = (1 - x if k & 4 else x, 1 - y if k & 2 else y, 1 - c if k & 1 else c)
            cp = pltpu.make_async_remote_copy(
                src_ref=pack_ref, dst_ref=all_ref.at[me], send_sem=send_sems.at[nw + k - 1],
                recv_sem=recv_sems.at[nw + k - 1], device_id=to, device_id_type=MESH)
            cp.start()
            cps.append(cp)
        for cp in cps:
            cp.wait()
        total = all_ref[0]
        for d in range(1, N_DEV):
            total = total + all_ref[d]
        sum_ref[...] = total

    n_sems = nw + N_DEV - 1
    out = pl.pallas_call(
        body, name=name, in_specs=[ANY] * nw + [VMEM_SPEC], out_specs=[ANY] * nw + [VMEM_SPEC],
        out_shape=[jax.ShapeDtypeStruct(f.shape, f.dtype) for f in fs] + [jax.ShapeDtypeStruct(pack.shape, F32)],
        input_output_aliases={i: i for i in range(nw)},
        scratch_shapes=[pltpu.VMEM((N_DEV, rows, LANES), F32), pltpu.SemaphoreType.DMA((n_sems,)),
                        pltpu.SemaphoreType.DMA((n_sems,))],
    )(*fs, pack)
    return list(out[:nw]), out[nw]


def _sum_sibling(g, land, where_arr, name):
    P, Q = g.shape[-2:]
    tp = _pick(P, (256, 128, 64))

    def body(w_ref, g_ref, l_ref, s_ref):
        s_ref[...] = (g_ref[...].astype(F32) + l_ref[...].astype(F32)).astype(BF16)

    if g.ndim == 4:
        g_spec = pl.BlockSpec((None, None, tp, Q), lambda j, r, w: (w[1 + j], w[0], r, 0))
    else:
        g_spec = pl.BlockSpec((None, tp, Q), lambda j, r, w: (w[1 + j], r, 0))
    slab = pl.BlockSpec((None, tp, Q), lambda j, r, w: (w[1 + j], r, 0))
    return pl.pallas_call(
        body, name=name,
        grid_spec=pltpu.PrefetchScalarGridSpec(
            num_scalar_prefetch=1, grid=(N_CHIPS - 1, P // tp), in_specs=[g_spec, slab], out_specs=slab),
        out_shape=jax.ShapeDtypeStruct((N_CHIPS, P, Q), BF16),
        compiler_params=_params(("parallel", "parallel")),
    )(where_arr, g, land)


def _sum_chips(g, land, recv, sc_arr, name):
    P, Q = g.shape[-2:]
    tp = _pick(P, (256, 128, 64))

    def body(sc_ref, g_ref, l_ref, r_ref, f_ref):
        acc = g_ref[...].astype(F32) + l_ref[...].astype(F32)
        for j in range(N_CHIPS - 1):
            acc = acc + r_ref[j].astype(F32)
        f_ref[...] = acc

    if g.ndim == 4:
        g_spec = pl.BlockSpec((None, None, tp, Q), lambda r, sc: (sc[0], sc[1], r, 0))
    else:
        g_spec = pl.BlockSpec((None, tp, Q), lambda r, sc: (sc[0], r, 0))
    return pl.pallas_call(
        body, name=name,
        grid_spec=pltpu.PrefetchScalarGridSpec(
            num_scalar_prefetch=1, grid=(P // tp,),
            in_specs=[g_spec,
                      pl.BlockSpec((None, tp, Q), lambda r, sc: (sc[0], r, 0)),
                      pl.BlockSpec((N_CHIPS - 1, tp, Q), lambda r, sc: (0, r, 0))],
            out_specs=pl.BlockSpec((None, tp, Q), lambda r, sc: (sc[1], r, 0))),
        out_shape=jax.ShapeDtypeStruct((2, P, Q), F32),
        compiler_params=_params(("parallel",)),
    )(sc_arr, g, land, recv)


class _Reduction:
    def __init__(self, names, pieces, lands, flight):
        self.names, self.pieces, self.lands = names, pieces, lands
        self.send_sems, self.recv_sems, self.sums, self.zones, self.token = flight


def _reduce_start(pieces, lands, names, where_arr):
    tag = names[0] if len(names) == 1 else "branches"
    sums = [_sum_sibling(g, l, where_arr, "sum_sibling_" + nm) for g, l, nm in zip(pieces, lands, names)]
    return _Reduction(names, pieces, lands, _chips_send_start(sums, "grads_to_chips_start_" + tag))


def _wgrad_half(a, b, half_arr, name, after=None):
    K, M = a.shape
    nslab = b.shape[1] // N_CHIPS
    tm = M // 2
    tn = _pick(nslab, (896, 512, 384, 256, 128))
    per = nslab // tn
    assert 4 * K * (tm + tn) <= MATMUL_OPERAND_BYTES

    def body(h_ref, a_ref, b_ref, *rest):
        rest[-1][...] = lax.dot_general(a_ref[...], b_ref[...], TN_DIMS, preferred_element_type=F32).astype(BF16)

    return pl.pallas_call(
        body, name=name,
        grid_spec=pltpu.PrefetchScalarGridSpec(
            num_scalar_prefetch=1, grid=(b.shape[1] // tn,),
            in_specs=[pl.BlockSpec((K, tm), lambda j, h: (0, h[0])), pl.BlockSpec((K, tn), lambda j, h: (0, j))]
            + ([] if after is None else [ANY]),
            out_specs=pl.BlockSpec((None, tm, tn), lambda j, h: (j // per, 0, j % per))),
        out_shape=jax.ShapeDtypeStruct((N_CHIPS, tm, nslab), BF16),
        compiler_params=_params(("parallel",)),
    )(*((half_arr, a, b) if after is None else (half_arr, a, b, after)))


def _sibling_send_start(g, name):
    zone = pltpu.with_memory_space_constraint(lax.empty(g.shape, g.dtype), pltpu.HBM)

    def body(g_ref, zone_ref, send_sem, recv_sem, g_thru, zone_thru, token):
        x, y, c, _ = _place()
        pltpu.make_async_remote_copy(src_ref=g_ref, dst_ref=zone_ref, send_sem=send_sem, recv_sem=recv_sem,
                                     device_id=(x, y, 1 - c), device_id_type=MESH).start()
        token[...] = jnp.zeros_like(token)

    return pl.pallas_call(
        body, name=name,
        out_shape=(pltpu.SemaphoreType.DMA(()), pltpu.SemaphoreType.DMA(()), pltpu.HBM(g.shape, g.dtype),
                   pltpu.HBM(g.shape, g.dtype), jax.ShapeDtypeStruct((8, LANES), F32)),
        in_specs=[HBM_SPEC, HBM_SPEC], out_specs=(SEM_SPEC, SEM_SPEC, HBM_SPEC, HBM_SPEC, VMEM_SPEC),
        input_output_aliases={0: 2, 1: 3},
        compiler_params=pltpu.CompilerParams(has_side_effects=DATAFLOW),
    )(pltpu.with_memory_space_constraint(g, pltpu.HBM), zone)


def _sibling_send_wait(send_sem, recv_sem, g, zone, after, name):
    def body(g_ref, zone_ref, s_sem, r_sem, after_ref, g_out, zone_out):
        x, y, c, _ = _place()
        cp = pltpu.make_async_remote_copy(src_ref=g_ref, dst_ref=zone_ref, send_sem=s_sem, recv_sem=r_sem,
                                          device_id=(x, y, 1 - c), device_id_type=MESH)
        cp.wait_send()
        cp.wait_recv()

    return pl.pallas_call(
        body, name=name, out_shape=(pltpu.HBM(g.shape, g.dtype), pltpu.HBM(zone.shape, zone.dtype)),
        in_specs=[HBM_SPEC, HBM_SPEC, SEM_SPEC, SEM_SPEC, ANY], out_specs=(HBM_SPEC, HBM_SPEC),
        input_output_aliases={0: 0, 1: 1},
        compiler_params=pltpu.CompilerParams(has_side_effects=DATAFLOW),
    )(g, zone, send_sem, recv_sem, after)[1]


def _reduce_finish(red, after, sc_arr):
    tag = red.names[0] if len(red.names) == 1 else "branches"
    recvs = _chips_send_wait(red.send_sems, red.recv_sems, red.sums, red.zones, after, "grads_to_chips_wait_" + tag)
    return [_sum_chips(g, l, r, sc_arr, "sum_chips_" + nm) for g, l, r, nm in zip(red.pieces, red.lands, recvs, red.names)]


def _t5_bucket(dist):
    max_exact = REL_BUCKETS // 2
    d = jnp.maximum(dist, 0)
    df = jnp.maximum(d, 1).astype(F32)
    large = max_exact + (jnp.log(df / max_exact) / math.log(REL_MAX_DIST / max_exact)
                         * (REL_BUCKETS - max_exact)).astype(jnp.int32)
    large = jnp.minimum(large, REL_BUCKETS - 1)
    return jnp.where(d < max_exact, d, large)


def _bucket_table():
    qi = jnp.arange(WINDOW)[:, None]
    si = jnp.arange(2 * WINDOW)[None, :]
    return _t5_bucket(qi + WINDOW - si)


TILE_WORDS = 8 * LANES


def _tile_rows(shape):
    return -(-math.prod(shape) // TILE_WORDS) * 8


def _rows_of(a):
    flat = a.reshape(-1).astype(F32)
    n = _tile_rows(a.shape) * LANES
    return jnp.pad(flat, (0, n - flat.shape[0])).reshape(-1, LANES)


def _pack_rows(parts):
    return jnp.concatenate([_rows_of(p) for p in parts], axis=0)


def _unpack_rows(packed, shapes):
    out, at = [], 0
    for shp in shapes:
        n, nr = math.prod(shp), _tile_rows(shp)
        out.append(packed[at:at + nr].reshape(-1)[:n].reshape(shp))
        at += nr
    return out


def kernel(x, norm_pre, w_in, rel_bias, attn_sinks, lb_logits, hgrn_norm, w_branch_attn, w_branch_hgrn, w_out, norm_post, loss_target, m_norm_pre, m_w_in, m_rel_bias, m_attn_sinks, m_lb_logits, m_hgrn_norm, m_w_branch_attn, m_w_branch_hgrn, m_w_out, m_norm_post, v_norm_pre, v_w_in, v_rel_bias, v_attn_sinks, v_lb_logits, v_hgrn_norm, v_w_branch_attn, v_w_branch_hgrn, v_w_out, v_norm_post):
    B_loc, S, D = x.shape
    T = B_loc * S
    x2 = x.reshape(T, D)
    tgt2 = loss_target.reshape(T, D)
    my_x, my_y, my_c = lax.axis_index("x"), lax.axis_index("y"), lax.axis_index("c")

    c_arr = jnp.reshape(my_c, (1,)).astype(jnp.int32)
    s_arr = jnp.reshape(2 * my_x + my_y, (1,)).astype(jnp.int32)
    sc_arr = jnp.concatenate([s_arr, c_arr])
    shard_ws = [w_in[0], w_branch_attn[0], w_branch_hgrn[0], w_out[0]]
    shard_axes = (1, 1, 1, 0)
    names = ["w_in", "w_branch_attn", "w_branch_hgrn", "w_out"]
    placed = [_cast_into_full(w, ax, s_arr, "cast_" + nm) for w, ax, nm in zip(shard_ws, shard_axes, names)]
    peer_slabs = [jnp.reshape(t, (1,)).astype(jnp.int32)
                  for t in (2 * (1 - my_x) + my_y, 2 * my_x + 1 - my_y, 2 * (1 - my_x) + 1 - my_y)]

    buckets = _bucket_table()
    onehot = (buckets.reshape(-1)[:, None] == jnp.arange(REL_BUCKETS)[None, :]).astype(F32)
    bias_tab = jnp.dot(onehot, rel_bias.astype(F32), precision=lax.Precision.HIGHEST).T.reshape(ATTN_HEADS, WINDOW, 2 * WINDOW)
    sinks_b = jnp.broadcast_to(attn_sinks[0].astype(F32)[:, None, None], (ATTN_HEADS, 8, LANES))
    lb_fn = lambda l: jnp.cumsum(jax.nn.softmax(l.astype(F32), axis=0), axis=0)[:1]
    lb, lb_vjp = jax.vjp(lb_fn, lb_logits)
    gain_h = hgrn_norm[0].reshape(1, HGRN_WIDTH)

    h, rstd = _rmsnorm_fwd(x2, norm_pre)
    gather_in = _Gather(placed[:1], [shard_ws[0].shape], shard_axes[:1], "w_in")
    gather_in.start([0, 1])
    proj = _matmul_slab(h, gather_in.fulls[0], s_arr, None, gather_in.token, "in_proj_own")
    gather_in.wait([0, 1], proj)
    gather_in.forward([0, 1])
    gather_in.start([2])
    proj = _matmul_slab(h, gather_in.fulls[0], peer_slabs[0], proj, gather_in.token, "in_proj_peer0")
    proj = _matmul_slab(h, gather_in.fulls[0], peer_slabs[1], proj, None, "in_proj_peer1")
    gather_in.wait([2], proj)
    gather_in.forward([2])
    gather_rest = _Gather(placed[1:], [w.shape for w in shard_ws[1:]], shard_axes[1:], "rest")
    gather_rest.start([0, 1, 2], gather_in.fulls[0])
    proj = _matmul_slab(h, gather_in.fulls[0], peer_slabs[2], proj, gather_rest.token, "in_proj_peer2")
    win_f = gather_in.fulls[0]
    ya, attn_probs = _attn_fwd(proj, bias_tab, sinks_b, B_loc, S)
    yh, states = _hgrn_fwd(proj, lb, gain_h, B_loc, S)
    gather_rest.wait([0, 1, 2], yh)
    gather_rest.forward([0, 1, 2])
    wba_f, wbh_f, wout_f = gather_rest.fulls
    ua = _matmul(ya, wba_f, "nn", BF16, "branch_attn_proj")
    uh, merged = _branch_proj_merge(yh, wbh_f, proj, ua)
    yv = _matmul(merged, wout_f, "nn", BF16, "out_proj")
    dy, dout, loss_p, gnpost_p = _post_loss(yv, x2, tgt2, norm_post)

    g_wout = _matmul(merged, dy, "tn", BF16, "out_proj_wgrad")
    d_ua, d_uh, dproj = _out_dgrad_merge_bwd(dy, wout_f, proj, ua, uh)
    g_wba = _matmul(ya, d_ua, "tn", BF16, "branch_attn_wgrad", slabs=N_CHIPS)
    g_wbh = _matmul(yh, d_uh, "tn", BF16, "branch_hgrn_wgrad", slabs=N_CHIPS)
    where_arr = jnp.concatenate([c_arr] + peer_slabs)
    late_pieces = [g.reshape(N_CHIPS, 2, -1, g.shape[-1]) for g in (g_wba, g_wbh, g_wout)]
    late = _reduce_start(late_pieces, _exchange_sibling_halves(late_pieces, "grads_to_sibling_branches"), names[1:], where_arr)
    d_ya = _matmul(d_ua, wba_f, "nt", BF16, "branch_attn_dgrad", after=late.token)
    d_yh = _matmul(d_uh, wbh_f, "nt", BF16, "branch_hgrn_dgrad", after=late.token)
    dproj, dbias_p, dsinks_p = _attn_bwd(proj, attn_probs, d_ya, dproj, B_loc, S)
    dproj, dlb_p, dgain_p = _hgrn_bwd(proj, states, lb, gain_h, d_yh, dproj, B_loc, S)
    g_give = _wgrad_half(h, dproj, 1 - c_arr, "in_proj_wgrad_sibling_half")
    s_sem, r_sem, g_give, zone, token = _sibling_send_start(g_give, "grads_to_sibling_start_w_in")
    g_keep = _wgrad_half(h, dproj, c_arr, "in_proj_wgrad_own_half", after=token)
    land = _sibling_send_wait(s_sem, r_sem, g_give, zone, g_keep, "grads_to_sibling_wait_w_in")
    last = _reduce_start([g_keep], [land], names[:1], where_arr)
    dh = _matmul(dproj, win_f, "nt", BF16, "in_proj_dgrad", after=last.token)
    grad_x2, gnpre_p = _rmsnorm_bwd(dh, x2, rstd, norm_pre, dout)
    halves = _reduce_finish(last, grad_x2, sc_arr) + _reduce_finish(late, grad_x2, sc_arr)

    grelb_p = jnp.dot(dbias_p.reshape(ATTN_HEADS, -1), onehot, precision=lax.Precision.HIGHEST).T
    gsinks_p = dsinks_p[:, 0, 0]
    dlb_sum = jnp.sum(dlb_p, axis=0).reshape(1, HGRN_WIDTH)
    ghn_p = jnp.sum(dgain_p, axis=0).reshape(HGRN_HEADS, HGRN_DIM)
    small_parts = [gnpre_p, gnpost_p, grelb_p, gsinks_p, dlb_sum, ghn_p, loss_p]
    small_shapes = [p.shape for p in small_parts]
    shared, pack_sum = _share_and_sum_small(halves, _pack_rows(small_parts), "grads_share_sibling")
    big_w = [w_in, w_branch_attn, w_branch_hgrn, w_out]
    big_m = [m_w_in, m_w_branch_attn, m_w_branch_hgrn, m_w_out]
    big_v = [v_w_in, v_w_branch_attn, v_w_branch_hgrn, v_w_out]
    big = {}
    for nm, gs, w, m, v in zip(names, shared, big_w, big_m, big_v):
        shp = w.shape
        g2 = gs.reshape(shp[1], shp[2])
        d, nm_, nv_, g_out = _adamw(w[0], g2, m[0], v[0], "adamw_" + nm)
        big[nm] = tuple(a.reshape(shp) for a in (g_out, d, nm_, nv_))

    gnpre, gnpost, grelb, gsinks, dlb, ghn, loss = _unpack_rows(pack_sum, small_shapes)
    (g_lb_logits,) = lb_vjp(dlb)
    small_names = ["norm_pre", "rel_bias", "attn_sinks", "lb_logits", "hgrn_norm", "norm_post"]
    small_w = [norm_pre, rel_bias, attn_sinks, lb_logits, hgrn_norm, norm_post]
    small_m = [m_norm_pre, m_rel_bias, m_attn_sinks, m_lb_logits, m_hgrn_norm, m_norm_post]
    small_v = [v_norm_pre, v_rel_bias, v_attn_sinks, v_lb_logits, v_hgrn_norm, v_norm_post]
    small_g = [gnpre.reshape(norm_pre.shape), grelb.reshape(rel_bias.shape), gsinks.reshape(attn_sinks.shape),
               g_lb_logits.reshape(lb_logits.shape), ghn.reshape(hgrn_norm.shape), gnpost.reshape(norm_post.shape)]
    shapes = [w.shape for w in small_w]
    d_s, nm_s, nv_s, _ = _adamw(_pack_rows(small_w), _pack_rows(small_g), _pack_rows(small_m), _pack_rows(small_v),
                             "adamw_small")
    small = {}
    for nm, g, d, m_, v_ in zip(small_names, small_g, _unpack_rows(d_s, shapes), _unpack_rows(nm_s, shapes),
                                _unpack_rows(nv_s, shapes)):
        small[nm] = (g, d, m_, v_)

    allw = {**big, **small}
    order = ["norm_pre", "w_in", "rel_bias", "attn_sinks", "lb_logits", "hgrn_norm", "w_branch_attn", "w_branch_hgrn",
             "w_out", "norm_post"]
    outs = [loss.reshape(()), grad_x2.reshape(B_loc, S, D)]
    for k in range(4):
        outs += [allw[nm][k] for nm in order]
    return tuple(outs)
```

```python
import math

import jax
import jax.numpy as jnp
from jax import lax
from jax.experimental import pallas as pl
from jax.experimental.pallas import tpu as pltpu

F32 = jnp.float32
BF16 = jnp.bfloat16
MESH = pl.DeviceIdType.MESH

ATTN_HEADS = 16
ATTN_KV_HEADS = 4
HEAD_DIM = 64
GROUP = ATTN_HEADS // ATTN_KV_HEADS
WINDOW = 128
ATTN_WIDTH = ATTN_HEADS * HEAD_DIM
KV_WIDTH = ATTN_KV_HEADS * HEAD_DIM
HGRN_HEADS = 8
HGRN_DIM = 128
HGRN_WIDTH = HGRN_HEADS * HGRN_DIM
CHUNK = 64
SUB = 16
NSUB = CHUNK // SUB
REL_BUCKETS = 32
REL_MAX_DIST = 128
NORM_EPS = 1e-6
ADAM_LR = 0.001
ADAM_B1 = 0.9
ADAM_B2 = 0.999
ADAM_EPS = 1e-08
ADAM_WD = 0.01
ADAM_STEP = 10
LANES = 128
N_CHIPS = 4
N_DEV = 8
VMEM_LIMIT = 48 * 1024 * 1024
MATMUL_OPERAND_BYTES = 34 * 1024 * 1024
MATMUL_VMEM_BYTES = 44 * 1024 * 1024

OFF_AQ = 0
OFF_AK = OFF_AQ + ATTN_WIDTH
OFF_AV = OFF_AK + KV_WIDTH
OFF_AG = OFF_AV + KV_WIDTH
OFF_HQ = OFF_AG + ATTN_WIDTH
OFF_HF = OFF_HQ + HGRN_WIDTH
OFF_HI = OFF_HF + HGRN_WIDTH
OFF_HG = OFF_HI + HGRN_WIDTH
OFF_GA = OFF_HG + HGRN_WIDTH

NT_DIMS = (((1,), (1,)), ((), ()))
TN_DIMS = (((0,), (0,)), ((), ()))
NN_DIMS = (((1,), (0,)), ((), ()))


def _pick(n, cands):
    for c in cands:
        if n % c == 0:
            return c
    raise ValueError(f"no tile for {n} in {cands}")


def _params(sem):
    return pltpu.CompilerParams(dimension_semantics=sem, vmem_limit_bytes=VMEM_LIMIT)


def _bdot(a, b, dims=NN_DIMS):
    return lax.dot_general(a.astype(BF16), b.astype(BF16), dims, preferred_element_type=F32)


def _matmul(a, b, mode, out_dtype, name, slabs=1, after=None):
    if mode == "nn":
        (M, K), (K2, N) = a.shape, b.shape
    elif mode == "nt":
        (M, K), (N, K2) = a.shape, b.shape
    else:
        (K, M), (K2, N) = a.shape, b.shape
    assert K == K2
    nslab = N // slabs
    tm = _pick(M, (1024, 512, 256, 128))
    out_bytes = jnp.dtype(out_dtype).itemsize
    choices = []
    for tn in (2688, 1024, 896, 512, 384, 256, 128):
        for tk in (4096, 3584, 2048, 1792, 1536, 1024, 512, 256, 128):
            acc = 0 if tk == K else 4 * tm * tn
            if (nslab % tn == 0 and K % tk == 0 and 4 * tk * (tm + tn) <= MATMUL_OPERAND_BYTES
                    and 4 * tk * (tm + tn) + 2 * out_bytes * tm * tn + acc <= MATMUL_VMEM_BYTES):
                choices.append((K // tk > 1, -tn, tn, tk))
                break
    _, _, tn, tk = min(choices)
    nk = K // tk
    per = nslab // tn
    dims = {"nn": NN_DIMS, "nt": NT_DIMS, "tn": TN_DIMS}[mode]

    n_in = 2 if after is None else 3

    def body(*refs):
        a_ref, b_ref, o_ref, acc = refs[0], refs[1], refs[n_in], refs[n_in + 1:]
        part = lax.dot_general(a_ref[...], b_ref[...], dims, preferred_element_type=F32)
        if nk == 1:
            o_ref[...] = part.astype(o_ref.dtype)
            return
        acc_ref, = acc
        k = pl.program_id(2)

        @pl.when(k == 0)
        def _():
            acc_ref[...] = part

        @pl.when((k > 0) & (k < nk - 1))
        def _():
            acc_ref[...] += part

        @pl.when(k == nk - 1)
        def _():
            o_ref[...] = (acc_ref[...] + part).astype(o_ref.dtype)

    if mode == "tn":
        a_spec = pl.BlockSpec((tk, tm), lambda i, j, k: (k, i))
    else:
        a_spec = pl.BlockSpec((tm, tk), lambda i, j, k: (i, k))
    if mode == "nt":
        b_spec = pl.BlockSpec((tn, tk), lambda i, j, k: (j, k))
    else:
        b_spec = pl.BlockSpec((tk, tn), lambda i, j, k: (k, j))
    if slabs == 1:
        o_shape = jax.ShapeDtypeStruct((M, N), out_dtype)
        o_spec = pl.BlockSpec((tm, tn), lambda i, j, k: (i, j))
    else:
        o_shape = jax.ShapeDtypeStruct((slabs, M, nslab), out_dtype)
        o_spec = pl.BlockSpec((None, tm, tn), lambda i, j, k: (j // per, i, j % per))
    return pl.pallas_call(
        body, name=name, grid=(M // tm, N // tn, nk), in_specs=[a_spec, b_spec] + ([] if after is None else [ANY]),
        out_specs=o_spec, out_shape=o_shape,
        scratch_shapes=[pltpu.VMEM((tm, tn), F32)] if nk > 1 else [],
        compiler_params=_params(("parallel", "parallel", "arbitrary")),
    )(*((a, b) if after is None else (a, b, after)))


def _rmsnorm_fwd(x2, gain):
    T, D = x2.shape
    tr = _pick(T, (256, 128))

    def body(x_ref, g_ref, h_ref, r_ref):
        xv = x_ref[...]
        r = lax.rsqrt(jnp.mean(xv * xv, axis=-1, keepdims=True) + NORM_EPS)
        h_ref[...] = (xv * r * g_ref[...]).astype(BF16)
        r_ref[...] = r

    return pl.pallas_call(
        body, name="rmsnorm_pre_fwd", grid=(T // tr,),
        in_specs=[pl.BlockSpec((tr, D), lambda i: (i, 0)), pl.BlockSpec((1, D), lambda i: (0, 0))],
        out_specs=[pl.BlockSpec((tr, D), lambda i: (i, 0)), pl.BlockSpec((tr, 1), lambda i: (i, 0))],
        out_shape=[jax.ShapeDtypeStruct((T, D), BF16), jax.ShapeDtypeStruct((T, 1), F32)],
        compiler_params=_params(("parallel",)),
    )(x2, gain)


def _rmsnorm_bwd(dh, x2, rstd, gain, dout):
    T, D = x2.shape
    tr = _pick(T, (256, 128))

    def body(dh_ref, x_ref, r_ref, g_ref, do_ref, gx_ref, gg_ref):
        @pl.when(pl.program_id(0) == 0)
        def _():
            gg_ref[...] = jnp.zeros_like(gg_ref)

        n = x_ref[...] * r_ref[...]
        dhv = dh_ref[...].astype(F32)
        dn = dhv * g_ref[...]
        gx_ref[...] = do_ref[...] + r_ref[...] * (dn - n * jnp.mean(dn * n, axis=-1, keepdims=True))
        gg_ref[...] += jnp.sum(dhv * n, axis=0, keepdims=True)

    row = pl.BlockSpec((tr, D), lambda i: (i, 0))
    vec = pl.BlockSpec((1, D), lambda i: (0, 0))
    return pl.pallas_call(
        body, name="rmsnorm_pre_bwd", grid=(T // tr,),
        in_specs=[row, row, pl.BlockSpec((tr, 1), lambda i: (i, 0)), vec, row],
        out_specs=[row, vec],
        out_shape=[jax.ShapeDtypeStruct((T, D), F32), jax.ShapeDtypeStruct((1, D), F32)],
        compiler_params=_params(("arbitrary",)),
    )(dh, x2, rstd, gain, dout)


def _post_loss(yv, x2, tgt2, gain):
    T, D = x2.shape
    tr = _pick(T, (256, 128))

    def body(y_ref, x_ref, t_ref, g_ref, dy_ref, do_ref, loss_ref, gg_ref):
        @pl.when(pl.program_id(0) == 0)
        def _():
            gg_ref[...] = jnp.zeros_like(gg_ref)
            loss_ref[...] = jnp.zeros_like(loss_ref)

        yv_ = y_ref[...].astype(F32)
        r = lax.rsqrt(jnp.mean(yv_ * yv_, axis=-1, keepdims=True) + NORM_EPS)
        n = yv_ * r
        e = (x_ref[...] + n * g_ref[...]) - t_ref[...]
        loss_ref[...] += 0.5 * jnp.sum(jnp.mean(e * e, axis=-1, keepdims=True), axis=0, keepdims=True)
        dz = e / D
        do_ref[...] = dz
        gg_ref[...] += jnp.sum(dz * n, axis=0, keepdims=True)
        dn = dz * g_ref[...]
        dy_ref[...] = (r * (dn - n * jnp.mean(dn * n, axis=-1, keepdims=True))).astype(BF16)

    row = pl.BlockSpec((tr, D), lambda i: (i, 0))
    vec = pl.BlockSpec((1, D), lambda i: (0, 0))
    return pl.pallas_call(
        body, name="post_norm_loss", grid=(T // tr,),
        in_specs=[row, row, row, vec],
        out_specs=[row, row, pl.BlockSpec((1, 1), lambda i: (0, 0)), vec],
        out_shape=[jax.ShapeDtypeStruct((T, D), BF16), jax.ShapeDtypeStruct((T, D), F32),
                   jax.ShapeDtypeStruct((1, 1), F32), jax.ShapeDtypeStruct((1, D), F32)],
        compiler_params=_params(("arbitrary",)),
    )(yv, x2, tgt2, gain)


def _window(rows, cols, at):
    return pl.BlockSpec((pl.Element(rows), pl.Element(cols)), at)


def _gate_windows(tm, tn, D):
    return [_window(tm, tn, lambda i, j: (i * tm, pl.multiple_of(OFF_GA + j * tn, LANES))),
            _window(tm, tn, lambda i, j: (i * tm, pl.multiple_of(OFF_GA + D + j * tn, LANES)))]


def _branch_proj_merge(yh, wbh, proj, ua):
    T, K = yh.shape
    D = wbh.shape[1]
    tm, tn = _pick(T, (512, 256, 128)), _pick(D, (1024, 512, 256))

    def body(a_ref, b_ref, ga_ref, gh_ref, ua_ref, uh_ref, m_ref):
        uh = jnp.dot(a_ref[...], b_ref[...], preferred_element_type=F32).astype(BF16)
        uh_ref[...] = uh
        m_ref[...] = (jax.nn.sigmoid(ga_ref[...]) * ua_ref[...].astype(F32)
                      + jax.nn.sigmoid(gh_ref[...]) * uh.astype(F32)).astype(BF16)

    blk = pl.BlockSpec((tm, tn), lambda i, j: (i, j))
    o = jax.ShapeDtypeStruct((T, D), BF16)
    return pl.pallas_call(
        body, name="branch_hgrn_proj_merge", grid=(T // tm, D // tn),
        in_specs=[pl.BlockSpec((tm, K), lambda i, j: (i, 0)), pl.BlockSpec((K, tn), lambda i, j: (0, j))]
        + _gate_windows(tm, tn, D) + [blk],
        out_specs=[blk, blk], out_shape=[o, o],
        compiler_params=_params(("parallel", "arbitrary")),
    )(yh, wbh, proj, proj, ua)


def _out_dgrad_merge_bwd(dy, wout, proj, ua, uh):
    T, K = dy.shape
    D = wout.shape[0]
    tm, tn = _pick(T, (512, 256, 128)), _pick(D, (1024, 512, 256))
    nj = D // tn

    def body(a_ref, b_ref, ga_ref, gh_ref, ua_ref, uh_ref, dua_ref, duh_ref, dproj_ref):
        d = lax.dot_general(a_ref[...], b_ref[...], NT_DIMS, preferred_element_type=F32)
        sa = jax.nn.sigmoid(ga_ref[...])
        sh = jax.nn.sigmoid(gh_ref[...])
        dua_ref[...] = (d * sa).astype(BF16)
        duh_ref[...] = (d * sh).astype(BF16)
        dga = (d * ua_ref[...].astype(F32) * sa * (1.0 - sa)).astype(BF16)
        dgh = (d * uh_ref[...].astype(F32) * sh * (1.0 - sh)).astype(BF16)
        for jj in range(nj):
            @pl.when(pl.program_id(1) == jj)
            def _():
                dproj_ref[:, jj * tn:(jj + 1) * tn] = dga
                dproj_ref[:, D + jj * tn:D + (jj + 1) * tn] = dgh

    blk = pl.BlockSpec((tm, tn), lambda i, j: (i, j))
    o = jax.ShapeDtypeStruct((T, D), BF16)
    return pl.pallas_call(
        body, name="out_proj_dgrad_merge_bwd", grid=(T // tm, nj),
        in_specs=[pl.BlockSpec((tm, K), lambda i, j: (i, 0)), pl.BlockSpec((tn, K), lambda i, j: (j, 0))]
        + _gate_windows(tm, tn, D) + [blk, blk],
        out_specs=[blk, blk, _window(tm, 2 * D, lambda i, j: (i * tm, OFF_GA))],
        out_shape=[o, o, jax.ShapeDtypeStruct((T, proj.shape[1]), BF16)],
        compiler_params=_params(("parallel", "arbitrary")),
    )(dy, wout, proj, proj, ua, uh)


KV_PAIR = 2
PAIR_HEADS = KV_PAIR * GROUP


def _attn_mask(n):
    qi = lax.broadcasted_iota(jnp.int32, (WINDOW, 2 * WINDOW), 0)
    si = lax.broadcasted_iota(jnp.int32, (WINDOW, 2 * WINDOW), 1)
    dist = qi + WINDOW - si
    return (dist >= 0) & (dist < WINDOW) & ((si >= WINDOW) | (n > 0))


def _first_key_column(shape):
    return lax.broadcasted_iota(jnp.int32, shape, 1) == 0


def _attn_group_fwd(mask, q, k, v, ag, bias, sink):
    mask = jnp.concatenate([mask] * GROUP, axis=0)
    s = lax.dot_general(q.astype(BF16), k.astype(BF16), NT_DIMS, preferred_element_type=F32) * (HEAD_DIM ** -0.5)
    s = jnp.where(mask, s + bias, -1e30)
    m = jnp.maximum(jnp.max(s, axis=-1, keepdims=True), sink)
    p = jnp.exp(s - m)
    e_sink = jnp.exp(sink - m)
    den = jnp.sum(p, axis=-1, keepdims=True) + e_sink
    pb = p.astype(BF16)
    o = jnp.dot(pb, v.astype(BF16), preferred_element_type=F32)
    return o * (jax.nn.silu(ag) / den), pb, e_sink.astype(BF16)


def _attn_group_bwd(q, k, v, ag, kept, dout):
    e_sink = kept[:, 0:1].astype(F32)
    pb = jnp.where(_first_key_column(kept.shape), jnp.zeros_like(kept), kept)
    pf = pb.astype(F32)
    rden = 1.0 / (jnp.sum(pf, axis=-1, keepdims=True) + e_sink)
    qb, kb, vb = q.astype(BF16), k.astype(BF16), v.astype(BF16)
    o = jnp.dot(pb, vb, preferred_element_type=F32) * rden
    sg = jax.nn.sigmoid(ag)
    d_o = dout * (ag * sg)
    dag = dout * o * (sg * (1.0 + ag * (1.0 - sg)))
    d_row = jnp.sum(d_o * o, axis=-1, keepdims=True)
    ds = (pf * rden) * (lax.dot_general(d_o.astype(BF16), vb, NT_DIMS, preferred_element_type=F32) - d_row)
    dsb = ds.astype(BF16)
    dq = jnp.dot(dsb, kb, preferred_element_type=F32) * (HEAD_DIM ** -0.5)
    dk = lax.dot_general(dsb, qb, TN_DIMS, preferred_element_type=F32) * (HEAD_DIM ** -0.5)
    dv = lax.dot_general(pb, (d_o * rden).astype(BF16), TN_DIMS, preferred_element_type=F32)
    return dq, dk, dv, dag, ds, -(e_sink * rden * d_row)


def _attn_operands(q_ref, kc_ref, kp_ref, vc_ref, vp_ref, ag_ref, bias_ref, sink_ref, j):
    lo, hi = j * HEAD_DIM, (j + 1) * HEAD_DIM
    k = jnp.concatenate([kp_ref[:, lo:hi], kc_ref[:, lo:hi]], axis=0)
    v = jnp.concatenate([vp_ref[:, lo:hi], vc_ref[:, lo:hi]], axis=0)
    heads = [j * GROUP + g for g in range(GROUP)]
    q = jnp.concatenate([q_ref[:, h * HEAD_DIM:(h + 1) * HEAD_DIM] for h in heads], axis=0)
    ag = jnp.concatenate([ag_ref[:, h * HEAD_DIM:(h + 1) * HEAD_DIM] for h in heads], axis=0)
    bias = jnp.concatenate([bias_ref[h] for h in heads], axis=0)
    sink = jnp.concatenate([jnp.broadcast_to(sink_ref[h, 0:1, 0:1], (WINDOW, 1)) for h in heads], axis=0)
    return q, k, v, ag, bias, sink


def _attn_fwd(proj, bias_tab, sinks_b, B_loc, S):
    T = B_loc * S
    nb = S // WINDOW

    def at(off, back=0):
        return lambda b, n: ((b * nb + jnp.maximum(n - back, 0)) * WINDOW, off)

    in_specs = [_window(WINDOW, ATTN_WIDTH, at(OFF_AQ)), _window(WINDOW, KV_WIDTH, at(OFF_AK)),
                _window(WINDOW, KV_WIDTH, at(OFF_AK, 1)), _window(WINDOW, KV_WIDTH, at(OFF_AV)),
                _window(WINDOW, KV_WIDTH, at(OFF_AV, 1)), _window(WINDOW, ATTN_WIDTH, at(OFF_AG)),
                pl.BlockSpec((ATTN_HEADS, WINDOW, 2 * WINDOW), lambda b, n: (0, 0, 0)),
                pl.BlockSpec((ATTN_HEADS, 8, LANES), lambda b, n: (0, 0, 0))]

    def body(q_ref, kc_ref, kp_ref, vc_ref, vp_ref, ag_ref, bias_ref, sink_ref, ya_ref, p_ref):
        mask = _attn_mask(pl.program_id(1))
        for j in range(ATTN_KV_HEADS):
            out, probs, e_sink = _attn_group_fwd(
                mask, *_attn_operands(q_ref, kc_ref, kp_ref, vc_ref, vp_ref, ag_ref, bias_ref, sink_ref, j))
            for g in range(GROUP):
                h = j * GROUP + g
                blk = slice(g * WINDOW, (g + 1) * WINDOW)
                ya_ref[:, h * HEAD_DIM:(h + 1) * HEAD_DIM] = out[blk].astype(BF16)
                p_ref[:, h * 2 * WINDOW:(h + 1) * 2 * WINDOW] = probs[blk]
                p_ref[:, h * 2 * WINDOW:h * 2 * WINDOW + 1] = e_sink[blk]

    return pl.pallas_call(
        body, name="attn_fwd", grid=(B_loc, nb), in_specs=in_specs,
        out_specs=[pl.BlockSpec((WINDOW, ATTN_WIDTH), lambda b, n: (b * nb + n, 0)),
                   pl.BlockSpec((WINDOW, ATTN_HEADS * 2 * WINDOW), lambda b, n: (b * nb + n, 0))],
        out_shape=[jax.ShapeDtypeStruct((T, ATTN_WIDTH), BF16),
                   jax.ShapeDtypeStruct((T, ATTN_HEADS * 2 * WINDOW), BF16)],
        compiler_params=_params(("parallel", "parallel")),
    )(proj, proj, proj, proj, proj, proj, bias_tab, sinks_b)


def _attn_bwd(proj, probs, dya, dproj, B_loc, S):
    nb = S // WINDOW
    n_pairs = ATTN_KV_HEADS // KV_PAIR
    qw, kw = PAIR_HEADS * HEAD_DIM, KV_PAIR * HEAD_DIM

    def at(off, width, back=0):
        return lambda b, i, p: ((b * nb + jnp.maximum(nb - 1 - i - back, 0)) * WINDOW,
                                pl.multiple_of(off + p * width, LANES))

    in_specs = [_window(WINDOW, qw, at(OFF_AQ, qw)), _window(WINDOW, kw, at(OFF_AK, kw)),
                _window(WINDOW, kw, at(OFF_AK, kw, 1)), _window(WINDOW, kw, at(OFF_AV, kw)),
                _window(WINDOW, kw, at(OFF_AV, kw, 1)), _window(WINDOW, qw, at(OFF_AG, qw)),
                pl.BlockSpec((WINDOW, PAIR_HEADS * 2 * WINDOW), lambda b, i, p: (b * nb + nb - 1 - i, p)),
                pl.BlockSpec((WINDOW, qw), lambda b, i, p: (b * nb + nb - 1 - i, p)), ANY]

    def body(q_ref, kc_ref, kp_ref, vc_ref, vp_ref, ag_ref, p_ref, dya_ref, dproj_in,
             dproj_ref, dbias_ref, dsink_ref, dkc_ref, dvc_ref):
        b, i, p = pl.program_id(0), pl.program_id(1), pl.program_id(2)

        @pl.when((b == 0) & (i == 0) & (p == 0))
        def _():
            dbias_ref[...] = jnp.zeros_like(dbias_ref)
            dsink_ref[...] = jnp.zeros_like(dsink_ref)

        @pl.when(i == 0)
        def _():
            dkc_ref[p] = jnp.zeros((WINDOW, kw), F32)
            dvc_ref[p] = jnp.zeros((WINDOW, kw), F32)

        dk_carry, dv_carry = dkc_ref[p], dvc_ref[p]
        dqs, dags, dbiases, dsinks, dks, dvs = [], [], [], [], [], []
        for j in range(KV_PAIR):
            heads = [j * GROUP + g for g in range(GROUP)]
            lo, hi = j * HEAD_DIM, (j + 1) * HEAD_DIM
            stack = lambda parts: jnp.concatenate(parts, axis=0)
            k = stack([kp_ref[:, lo:hi], kc_ref[:, lo:hi]])
            v = stack([vp_ref[:, lo:hi], vc_ref[:, lo:hi]])
            q = stack([q_ref[:, h * HEAD_DIM:(h + 1) * HEAD_DIM] for h in heads])
            ag = stack([ag_ref[:, h * HEAD_DIM:(h + 1) * HEAD_DIM] for h in heads])
            dout = stack([dya_ref[:, h * HEAD_DIM:(h + 1) * HEAD_DIM].astype(F32) for h in heads])
            kept = stack([p_ref[:, h * 2 * WINDOW:(h + 1) * 2 * WINDOW] for h in heads])
            dq, dk, dv, dag, dbias, dsink = _attn_group_bwd(q, k, v, ag, kept, dout)
            dks.append((dk[WINDOW:] + dk_carry[:, lo:hi]).astype(BF16))
            dvs.append((dv[WINDOW:] + dv_carry[:, lo:hi]).astype(BF16))
            dkc_ref[p, :, lo:hi] = dk[:WINDOW]
            dvc_ref[p, :, lo:hi] = dv[:WINDOW]
            for g in range(GROUP):
                blk = slice(g * WINDOW, (g + 1) * WINDOW)
                dqs.append(dq[blk].astype(BF16))
                dags.append(dag[blk].astype(BF16))
                dbiases.append(dbias[blk])
                dsinks.append(jnp.broadcast_to(jnp.sum(dsink[blk], axis=0, keepdims=True), (8, LANES)))

        for pair in range(n_pairs):
            @pl.when(p == pair)
            def _():
                for j in range(KV_PAIR):
                    col = (pair * KV_PAIR + j) * HEAD_DIM
                    dproj_ref[:, OFF_AK + col:OFF_AK + col + HEAD_DIM] = dks[j]
                    dproj_ref[:, OFF_AV + col:OFF_AV + col + HEAD_DIM] = dvs[j]
                for hh in range(PAIR_HEADS):
                    h = pair * PAIR_HEADS + hh
                    dproj_ref[:, OFF_AQ + h * HEAD_DIM:OFF_AQ + (h + 1) * HEAD_DIM] = dqs[hh]
                    dproj_ref[:, OFF_AG + h * HEAD_DIM:OFF_AG + (h + 1) * HEAD_DIM] = dags[hh]
                    dbias_ref[h] += dbiases[hh]
                    dsink_ref[h] += dsinks[hh]

    return pl.pallas_call(
        body, name="attn_bwd", grid=(B_loc, nb, n_pairs), in_specs=in_specs,
        out_specs=[_window(WINDOW, OFF_HQ, lambda b, i, p: ((b * nb + nb - 1 - i) * WINDOW, 0)),
                   pl.BlockSpec((ATTN_HEADS, WINDOW, 2 * WINDOW), lambda b, i, p: (0, 0, 0)),
                   pl.BlockSpec((ATTN_HEADS, 8, LANES), lambda b, i, p: (0, 0, 0))],
        out_shape=[jax.ShapeDtypeStruct(dproj.shape, dproj.dtype),
                   jax.ShapeDtypeStruct((ATTN_HEADS, WINDOW, 2 * WINDOW), F32),
                   jax.ShapeDtypeStruct((ATTN_HEADS, 8, LANES), F32)],
        input_output_aliases={8: 0},
        scratch_shapes=[pltpu.VMEM((n_pairs, WINDOW, kw), F32), pltpu.VMEM((n_pairs, WINDOW, kw), F32)],
        compiler_params=_params(("arbitrary", "arbitrary", "arbitrary")),
    )(proj, proj, proj, proj, proj, proj, probs, dya, dproj)


class _HgrnPre:
    def __init__(self, fr, qr, lb, g_scr):
        t = lax.broadcasted_iota(jnp.int32, (CHUNK, CHUNK), 0)
        s = lax.broadcasted_iota(jnp.int32, (CHUNK, CHUNK), 1)
        self.sg = jax.nn.sigmoid(fr)
        self.f = lb + (1.0 - lb) * self.sg
        g = jnp.dot((t >= s).astype(F32), jnp.log(self.f), precision=lax.Precision.HIGHEST, preferred_element_type=F32)
        g_scr[...] = g
        self.g = g
        self.row = lax.broadcasted_iota(jnp.int32, g.shape, 0)
        self.refs = [jnp.zeros((1, g.shape[1]), F32)] + [g_scr[pl.ds(i * SUB - 1, 1), :] for i in range(1, NSUB)]
        self.gend = g_scr[pl.ds(CHUNK - 1, 1), :]
        refrow = jnp.zeros_like(g)
        for i in range(1, NSUB):
            refrow = jnp.where(self.row >= i * SUB, self.refs[i], refrow)
        self.sigq = jax.nn.sigmoid(qr)
        self.qs = qr * self.sigq
        self.k = 1.0 - self.f
        self.eg = jnp.exp(g)
        self.eqd = jnp.exp(g - refrow)
        self.ekd = [jnp.exp(jnp.where(self.row < (i + 1) * SUB, self.refs[i] - g, 0.0)) for i in range(NSUB)]
        self.ekdec = jnp.exp(self.gend - g)
        self.qg = self.qs * self.eg
        self.qd = self.qs * self.eqd
        self.kd = [self.k * e for e in self.ekd]
        self.kdec = self.k * self.ekdec
        self.egend = jnp.exp(self.gend)


def _hgrn_pair_mask():
    t = lax.broadcasted_iota(jnp.int32, (CHUNK, NSUB * CHUNK), 0)
    col = lax.broadcasted_iota(jnp.int32, (CHUNK, NSUB * CHUNK), 1)
    return ((t // SUB) == (col // CHUNK)) & ((col % CHUNK) <= t)


def _hgrn_head_out(p, lanes, state_t, v, mask):
    qg, qd = p.qg[:, lanes], p.qd[:, lanes]
    kall = jnp.concatenate([kd[:, lanes] for kd in p.kd], axis=0)
    vst = jnp.concatenate([v] * NSUB, axis=0)
    am = jnp.where(mask, _bdot(qd, kall, NT_DIMS), 0.0)
    o = _bdot(qg, state_t, NT_DIMS) + _bdot(am, vst)
    return o, (qg, qd, kall, am, vst)


def _hgrn_fwd(proj, lb, gain, B_loc, S):
    T = B_loc * S
    nc = S // CHUNK
    nh = HGRN_HEADS

    def at(off):
        return lambda b, n: ((b * nc + n) * CHUNK, off)

    vec = pl.BlockSpec((1, HGRN_WIDTH), lambda b, n: (0, 0))

    def body(q_ref, f_ref, v_ref, hg_ref, lb_ref, gain_ref, yh_ref, st_ref, state_scr, g_scr):
        @pl.when(pl.program_id(1) == 0)
        def _():
            state_scr[...] = jnp.zeros_like(state_scr)

        p = _HgrnPre(f_ref[...], q_ref[...], lb_ref[...], g_scr)
        v = v_ref[...]
        gate = gain_ref[...] * jax.nn.silu(hg_ref[...])
        mask = _hgrn_pair_mask()
        for hd in range(nh):
            lanes = slice(hd * HGRN_DIM, (hd + 1) * HGRN_DIM)
            st = state_scr[hd]
            st_ref[hd] = st
            o, _ = _hgrn_head_out(p, lanes, st, v[:, lanes], mask)
            rs = lax.rsqrt(jnp.mean(o * o, axis=-1, keepdims=True) + NORM_EPS)
            yh_ref[:, lanes] = (o * rs * gate[:, lanes]).astype(BF16)
            state_scr[hd] = st * p.egend[:, lanes] + _bdot(v[:, lanes], p.kdec[:, lanes], TN_DIMS)

    return pl.pallas_call(
        body, name="hgrn_fwd", grid=(B_loc, nc),
        in_specs=[_window(CHUNK, HGRN_WIDTH, at(OFF_HQ)), _window(CHUNK, HGRN_WIDTH, at(OFF_HF)),
                  _window(CHUNK, HGRN_WIDTH, at(OFF_HI)), _window(CHUNK, HGRN_WIDTH, at(OFF_HG)), vec, vec],
        out_specs=[pl.BlockSpec((CHUNK, HGRN_WIDTH), lambda b, n: (b * nc + n, 0)),
                   pl.BlockSpec((None, None, nh, HGRN_DIM, HGRN_DIM), lambda b, n: (b, n, 0, 0, 0))],
        out_shape=[jax.ShapeDtypeStruct((T, HGRN_WIDTH), BF16),
                   jax.ShapeDtypeStruct((B_loc, nc, nh, HGRN_DIM, HGRN_DIM), F32)],
        scratch_shapes=[pltpu.VMEM((nh, HGRN_DIM, HGRN_DIM), F32), pltpu.VMEM((CHUNK, HGRN_WIDTH), F32)],
        compiler_params=_params(("parallel", "arbitrary")),
    )(proj, proj, proj, proj, lb, gain)


def _hgrn_bwd(proj, states, lb, gain, dyh, dproj, B_loc, S):
    nc = S // CHUNK
    nh = HGRN_HEADS

    def at(off):
        return lambda b, i: ((b * nc + nc - 1 - i) * CHUNK, off)

    vec = pl.BlockSpec((1, HGRN_WIDTH), lambda b, i: (0, 0))
    in_specs = [_window(CHUNK, HGRN_WIDTH, at(OFF_HQ)), _window(CHUNK, HGRN_WIDTH, at(OFF_HF)),
                _window(CHUNK, HGRN_WIDTH, at(OFF_HI)), _window(CHUNK, HGRN_WIDTH, at(OFF_HG)), vec, vec,
                pl.BlockSpec((None, None, nh, HGRN_DIM, HGRN_DIM), lambda b, i: (b, nc - 1 - i, 0, 0, 0)),
                pl.BlockSpec((CHUNK, HGRN_WIDTH), lambda b, i: (b * nc + nc - 1 - i, 0)), ANY]
    acc_spec = pl.BlockSpec((None, 1, HGRN_WIDTH), lambda b, i: (b, 0, 0))

    def body(q_ref, f_ref, v_ref, hg_ref, lb_ref, gain_ref, st_ref, dyh_ref, dproj_in,
             dproj_ref, dlb_ref, dgain_ref, dstate_scr, g_scr, dg_scr):
        dq_ref, df_ref, dv_ref, dhg_ref = [dproj_ref.at[:, pl.ds(k * HGRN_WIDTH, HGRN_WIDTH)] for k in range(4)]

        @pl.when(pl.program_id(1) == 0)
        def _():
            dstate_scr[...] = jnp.zeros_like(dstate_scr)
            dlb_ref[...] = jnp.zeros_like(dlb_ref)
            dgain_ref[...] = jnp.zeros_like(dgain_ref)

        qr, lb, gain, hg, v = q_ref[...], lb_ref[...], gain_ref[...], hg_ref[...], v_ref[...]
        p = _HgrnPre(f_ref[...], qr, lb, g_scr)
        sgh = jax.nn.sigmoid(hg)
        sil = hg * sgh
        dy = dyh_ref[...].astype(F32)
        mask = _hgrn_pair_mask()
        dqg, dqd, dkdec, dv, dhg, dgend, dgain = [], [], [], [], [], [], []
        dkd = [[] for _ in range(NSUB)]
        heads = [slice(hd * HGRN_DIM, (hd + 1) * HGRN_DIM) for hd in range(nh)]
        sts = [st_ref[hd] for hd in range(nh)]
        dnews = [dstate_scr[hd] for hd in range(nh)]
        fwd = [_hgrn_head_out(p, lanes, st, v[:, lanes], mask) for lanes, st in zip(heads, sts)]
        for lanes, st, dnew in zip(heads, sts, dnews):
            dkdec_h = _bdot(v[:, lanes], dnew)
            dkdec.append(dkdec_h)
            dgend.append(jnp.sum(dkdec_h * p.kdec[:, lanes], axis=0, keepdims=True)
                         + jnp.sum(dnew * st, axis=0, keepdims=True) * p.egend[:, lanes])
        dos = []
        gate_grad = sgh * (1.0 + hg * (1.0 - sgh))
        for lanes, (o, _) in zip(heads, fwd):
            rs = lax.rsqrt(jnp.mean(o * o, axis=-1, keepdims=True) + NORM_EPS)
            n = o * rs
            dyn = dy[:, lanes] * n
            dgain.append(jnp.sum(dyn * sil[:, lanes], axis=0, keepdims=True))
            dhg.append(dyn * gain[:, lanes] * gate_grad[:, lanes])
            dn = dy[:, lanes] * gain[:, lanes] * sil[:, lanes]
            dos.append(rs * (dn - n * jnp.mean(dn * n, axis=-1, keepdims=True)))
        drs = []
        for hd, (lanes, st, dnew, do, (_, (qg, qd, kall, am, vst))) in enumerate(zip(heads, sts, dnews, dos, fwd)):
            dqg.append(_bdot(do, st))
            dstate_scr[hd] = _bdot(do, qg, TN_DIMS) + dnew * p.egend[:, lanes]
            drs.append(jnp.where(mask, _bdot(do, vst, NT_DIMS), 0.0))
            dvst = _bdot(am, do, TN_DIMS)
            dv.append(sum(dvst[i * CHUNK:(i + 1) * CHUNK] for i in range(NSUB)) + _bdot(p.kdec[:, lanes], dnew, NT_DIMS))
        for dr, (_, (qg, qd, kall, am, vst)) in zip(drs, fwd):
            dqd.append(_bdot(dr, kall))
            dkall = _bdot(dr, qd, TN_DIMS)
            for i in range(NSUB):
                dkd[i].append(dkall[i * CHUNK:(i + 1) * CHUNK])

        wide = lambda parts: jnp.concatenate(parts, axis=1)
        dqg, dqd, dkdec = wide(dqg), wide(dqd), wide(dkdec)
        t2 = dqd * p.qd
        dg = dqg * p.qg + t2 - dkdec * p.kdec
        dk = dkdec * p.ekdec
        dg_scr[...] = jnp.zeros_like(dg_scr)
        for i in range(NSUB):
            dkd_i = wide(dkd[i])
            tk = jnp.where(p.row < (i + 1) * SUB, dkd_i * p.kd[i], 0.0)
            dg = dg - tk
            dk = dk + dkd_i * p.ekd[i]
            if i >= 1:
                in_blk = (p.row >= i * SUB) & (p.row < (i + 1) * SUB)
                dg_scr[pl.ds(i * SUB - 1, 1), :] = (jnp.sum(tk, axis=0, keepdims=True)
                                                    - jnp.sum(jnp.where(in_blk, t2, 0.0), axis=0, keepdims=True))
        dg_scr[pl.ds(CHUNK - 1, 1), :] = wide(dgend)
        t = lax.broadcasted_iota(jnp.int32, (CHUNK, CHUNK), 0)
        s = lax.broadcasted_iota(jnp.int32, (CHUNK, CHUNK), 1)
        dlogf = jnp.dot((t <= s).astype(F32), dg + dg_scr[...], precision=lax.Precision.HIGHEST, preferred_element_type=F32)
        df = dlogf / p.f - dk
        df_ref[...] = (df * (1.0 - lb) * p.sg * (1.0 - p.sg)).astype(BF16)
        dlb_ref[...] += jnp.sum(df * (1.0 - p.sg), axis=0, keepdims=True)
        dq_ref[...] = ((dqg * p.eg + dqd * p.eqd) * p.sigq * (1.0 + qr * (1.0 - p.sigq))).astype(BF16)
        dv_ref[...] = wide(dv).astype(BF16)
        dhg_ref[...] = wide(dhg).astype(BF16)
        dgain_ref[...] += wide(dgain)

    acc = jax.ShapeDtypeStruct((B_loc, 1, HGRN_WIDTH), F32)
    return pl.pallas_call(
        body, name="hgrn_bwd", grid=(B_loc, nc), in_specs=in_specs,
        out_specs=[_window(CHUNK, 4 * HGRN_WIDTH, at(OFF_HQ)), acc_spec, acc_spec],
        out_shape=[jax.ShapeDtypeStruct(dproj.shape, dproj.dtype), acc, acc],
        input_output_aliases={8: 0},
        scratch_shapes=[pltpu.VMEM((nh, HGRN_DIM, HGRN_DIM), F32), pltpu.VMEM((CHUNK, HGRN_WIDTH), F32),
                        pltpu.VMEM((CHUNK, HGRN_WIDTH), F32)],
        compiler_params=_params(("parallel", "arbitrary")),
    )(proj, proj, proj, proj, lb, gain, states, dyh, dproj)


def _adamw(w, g, m, v, name):
    R, C = w.shape
    tr = _pick(R, (128, 64, 32, 16, 8)) if C > 1024 else _pick(R, (512, 256, 128, 64, 32, 16, 8))

    def body(w_ref, g_ref, m_ref, v_ref, d_ref, nm_ref, nv_ref, g_out_ref):
        gv = g_ref[...]
        g_out_ref[...] = gv
        nm = ADAM_B1 * m_ref[...] + (1.0 - ADAM_B1) * gv
        nv = ADAM_B2 * v_ref[...] + (1.0 - ADAM_B2) * (gv * gv)
        m_hat = nm / (1.0 - ADAM_B1 ** ADAM_STEP)
        v_hat = nv / (1.0 - ADAM_B2 ** ADAM_STEP)
        d_ref[...] = -ADAM_LR * (m_hat / (jnp.sqrt(v_hat) + ADAM_EPS) + ADAM_WD * w_ref[...])
        nm_ref[...] = nm
        nv_ref[...] = nv

    blk = pl.BlockSpec((tr, C), lambda i: (i, 0))
    o = jax.ShapeDtypeStruct((R, C), F32)
    return pl.pallas_call(
        body, name=name, grid=(R // tr,), in_specs=[blk] * 4, out_specs=[blk] * 4, out_shape=[o, o, o, o],
        compiler_params=_params(("parallel",)),
    )(w, g, m, v)


ANY = pl.BlockSpec(memory_space=pl.ANY)
VMEM_SPEC = pl.BlockSpec(memory_space=pltpu.VMEM)


def _place():
    x, y, c = lax.axis_index("x"), lax.axis_index("y"), lax.axis_index("c")
    other_chips = [(1 - x, y), (x, 1 - y), (1 - x, 1 - y)]
    return x, y, c, other_chips


def _cast_into_full(w, ax, s_arr, name):
    R, C = w.shape
    tr = _pick(R, (256, 128))
    nr = R // tr

    def body(s_ref, w_ref, o_ref):
        o_ref[...] = w_ref[...].astype(BF16)

    if ax == 1:
        shape, o_map = (R, N_CHIPS * C), lambda i, s: (i, s[0])
    else:
        shape, o_map = (N_CHIPS * R, C), lambda i, s: (s[0] * nr + i, 0)
    return pl.pallas_call(
        body, name=name,
        grid_spec=pltpu.PrefetchScalarGridSpec(
            num_scalar_prefetch=1, grid=(nr,), in_specs=[pl.BlockSpec((tr, C), lambda i, s: (i, 0))],
            out_specs=pl.BlockSpec((tr, C), o_map)),
        out_shape=jax.ShapeDtypeStruct(shape, BF16),
        compiler_params=_params(("parallel",)),
    )(s_arr, w)


class _Gather:
    def __init__(self, fulls, shard_shapes, axes, tag):
        self.shapes, self.axes, self.tag, self.nw = shard_shapes, axes, tag, len(fulls)
        self.fulls, self.sems, self.token = list(fulls), {}, None

    def start(self, peers, after=None):
        nw, np_ = self.nw, len(peers)

        def body(*refs):
            ins, sems = refs[:nw], refs[nw + (after is not None):nw + (after is not None) + 2 * np_]
            for k, j in enumerate(peers):
                for cp in self._peer_copies(ins, sems[2 * k], sems[2 * k + 1], j):
                    cp.start()
            refs[-1][...] = jnp.zeros_like(refs[-1])

        out = pl.pallas_call(
            body, name="gather_start_%s_%s" % (self.tag, "".join(map(str, peers))),
            out_shape=(*[pltpu.SemaphoreType.DMA((nw,))] * (2 * np_),
                       *[pltpu.HBM(f.shape, f.dtype) for f in self.fulls], jax.ShapeDtypeStruct((8, LANES), F32)),
            in_specs=[HBM_SPEC] * nw + ([] if after is None else [ANY]),
            out_specs=(*[SEM_SPEC] * (2 * np_), *[HBM_SPEC] * nw, VMEM_SPEC),
            input_output_aliases={k: 2 * np_ + k for k in range(nw)},
            compiler_params=pltpu.CompilerParams(has_side_effects=DATAFLOW),
        )(*[pltpu.with_memory_space_constraint(f, pltpu.HBM) for f in self.fulls], *(() if after is None else (after,)))
        for k, j in enumerate(peers):
            self.sems[j] = (out[2 * k], out[2 * k + 1])
        self.fulls = list(out[2 * np_:2 * np_ + nw])
        self.token = out[-1]

    def _region(self, ref, i, t, half):
        R, C = self.shapes[i]
        hr = R // 2
        if self.axes[i] == 1:
            return ref.at[pl.ds(half * hr, hr), pl.ds(pl.multiple_of(t * C, LANES), C)]
        return ref.at[pl.ds(t * R + half * hr, hr), :]

    def _peer_copies(self, refs, send_sems, recv_sems, j):
        x, y, c, chips = _place()
        s = 2 * x + y
        return [pltpu.make_async_remote_copy(
            src_ref=self._region(refs[i], i, s, c), dst_ref=self._region(refs[i], i, s, c), send_sem=send_sems.at[i],
            recv_sem=recv_sems.at[i], device_id=(*chips[j], c), device_id_type=MESH) for i in range(self.nw)]

    def wait(self, peers, after):
        nw, np_ = self.nw, len(peers)

        def body(*refs):
            ins, sems = refs[:nw], refs[nw:nw + 2 * np_]
            for k, j in enumerate(peers):
                for cp in self._peer_copies(ins, sems[2 * k], sems[2 * k + 1], j):
                    cp.wait_send()
                    cp.wait_recv()

        sem_args = [s for j in peers for s in self.sems[j]]
        out = pl.pallas_call(
            body, name="gather_wait_%s_%s" % (self.tag, "".join(map(str, peers))),
            out_shape=tuple(pltpu.HBM(f.shape, f.dtype) for f in self.fulls),
            in_specs=[HBM_SPEC] * nw + [SEM_SPEC] * (2 * np_) + [ANY], out_specs=tuple([HBM_SPEC] * nw),
            input_output_aliases={k: k for k in range(nw)},
            compiler_params=pltpu.CompilerParams(has_side_effects=DATAFLOW),
        )(*self.fulls, *sem_args, after)
        self.fulls = list(out)

    def forward(self, peers):
        nw, np_ = self.nw, len(peers)

        def body(*refs):
            ins, outs = refs[:nw], refs[nw:2 * nw]
            send_sems, recv_sems = refs[2 * nw:]
            x, y, c, chips = _place()
            cps = []
            for i in range(nw):
                for k, j in enumerate(peers):
                    t = 2 * chips[j][0] + chips[j][1]
                    cp = pltpu.make_async_remote_copy(
                        src_ref=self._region(ins[i], i, t, c), dst_ref=self._region(outs[i], i, t, c),
                        send_sem=send_sems.at[i * np_ + k], recv_sem=recv_sems.at[i * np_ + k],
                        device_id=(x, y, 1 - c), device_id_type=MESH)
                    cp.start()
                    cps.append(cp)
            for cp in cps:
                cp.wait()

        out = pl.pallas_call(
            body, name="gather_forward_%s_%s" % (self.tag, "".join(map(str, peers))),
            in_specs=[ANY] * nw, out_specs=[ANY] * nw,
            out_shape=[jax.ShapeDtypeStruct(f.shape, f.dtype) for f in self.fulls],
            input_output_aliases={i: i for i in range(nw)},
            scratch_shapes=[pltpu.SemaphoreType.DMA((nw * np_,)), pltpu.SemaphoreType.DMA((nw * np_,))],
        )(*self.fulls)
        self.fulls = list(out)


def _matmul_slab(a, wfull, slab_arr, prev, after, name):
    M, K = a.shape
    N = wfull.shape[1]
    nslab = N // N_CHIPS
    tn = _pick(nslab, (2688, 896, 512, 384, 256, 128))
    tm = _pick(M, (512, 256, 128) if tn > 1024 else (1024, 512, 256, 128))
    per = nslab // tn
    extra = [e for e in (prev, after) if e is not None]

    def body(slab_ref, a_ref, b_ref, *rest):
        rest[len(extra)][...] = jnp.dot(a_ref[...], b_ref[...], preferred_element_type=F32)

    return pl.pallas_call(
        body, name=name,
        grid_spec=pltpu.PrefetchScalarGridSpec(
            num_scalar_prefetch=1, grid=(M // tm, per),
            in_specs=[pl.BlockSpec((tm, K), lambda i, j, sl: (i, 0)),
                      pl.BlockSpec((K, tn), lambda i, j, sl: (0, sl[0] * per + j))] + [ANY] * len(extra),
            out_specs=pl.BlockSpec((tm, tn), lambda i, j, sl: (i, sl[0] * per + j))),
        out_shape=jax.ShapeDtypeStruct((M, N), F32),
        input_output_aliases={} if prev is None else {3: 0},
        compiler_params=_params(("parallel", "arbitrary")),
    )(slab_arr, a, wfull, *extra)


def _exchange_sibling_halves(gs, name):
    nw = len(gs)

    def body(*refs):
        ins, outs = refs[:nw], refs[nw:2 * nw]
        send_sems, recv_sems = refs[2 * nw:]
        x, y, c, _ = _place()
        cps = []
        for i in range(nw):
            cp = pltpu.make_async_remote_copy(src_ref=ins[i].at[:, 1 - c], dst_ref=outs[i], send_sem=send_sems.at[i],
                                              recv_sem=recv_sems.at[i], device_id=(x, y, 1 - c), device_id_type=MESH)
            cp.start()
            cps.append(cp)
        for cp in cps:
            cp.wait()

    return pl.pallas_call(
        body, name=name, in_specs=[ANY] * nw, out_specs=[ANY] * nw,
        out_shape=[jax.ShapeDtypeStruct((g.shape[0],) + g.shape[2:], g.dtype) for g in gs],
        scratch_shapes=[pltpu.SemaphoreType.DMA((nw,)), pltpu.SemaphoreType.DMA((nw,))],
    )(*gs)


HBM_SPEC = pl.BlockSpec(memory_space=pltpu.HBM)
SEM_SPEC = pl.BlockSpec(memory_space=pltpu.SEMAPHORE)
DATAFLOW = pltpu.SideEffectType.DATAFLOW_SIDE_EFFECTING


def _chip_copies(ins, lands, send_sems, recv_sems):
    x, y, c, chips = _place()
    return [pltpu.make_async_remote_copy(
        src_ref=ins[i].at[2 * chip[0] + chip[1]], dst_ref=lands[i].at[j], send_sem=send_sems.at[i * 3 + j],
        recv_sem=recv_sems.at[i * 3 + j], device_id=(*chip, c), device_id_type=MESH)
        for i in range(len(ins)) for j, chip in enumerate(chips)]


def _chips_send_start(ss, name):
    nw = len(ss)
    lands = [pltpu.with_memory_space_constraint(lax.empty((N_CHIPS - 1,) + s.shape[1:], s.dtype), pltpu.HBM) for s in ss]

    def body(*refs):
        ins, land_refs = refs[:nw], refs[nw:2 * nw]
        send_sems, recv_sems = refs[2 * nw], refs[2 * nw + 1]
        token = refs[-1]
        for cp in _chip_copies(ins, land_refs, send_sems, recv_sems):
            cp.start()
        token[...] = jnp.zeros_like(token)

    n = 3 * nw
    out = pl.pallas_call(
        body, name=name,
        out_shape=(pltpu.SemaphoreType.DMA((n,)), pltpu.SemaphoreType.DMA((n,)),
                   *[pltpu.HBM(s.shape, s.dtype) for s in ss], *[pltpu.HBM(l.shape, l.dtype) for l in lands],
                   jax.ShapeDtypeStruct((8, LANES), F32)),
        in_specs=[HBM_SPEC] * (2 * nw), out_specs=(SEM_SPEC, SEM_SPEC, *[HBM_SPEC] * (2 * nw), VMEM_SPEC),
        input_output_aliases={k: 2 + k for k in range(2 * nw)},
        compiler_params=pltpu.CompilerParams(has_side_effects=DATAFLOW),
    )(*[pltpu.with_memory_space_constraint(s, pltpu.HBM) for s in ss], *lands)
    return out[0], out[1], list(out[2:2 + nw]), list(out[2 + nw:2 + 2 * nw]), out[-1]


def _chips_send_wait(send_sems, recv_sems, ss, lands, after, name):
    nw = len(ss)

    def body(*refs):
        ins, land_refs = refs[:nw], refs[nw:2 * nw]
        s_sems, r_sems = refs[2 * nw], refs[2 * nw + 1]
        for cp in _chip_copies(ins, land_refs, s_sems, r_sems):
            cp.wait_send()
            cp.wait_recv()

    out = pl.pallas_call(
        body, name=name,
        out_shape=(*[pltpu.HBM(s.shape, s.dtype) for s in ss], *[pltpu.HBM(l.shape, l.dtype) for l in lands]),
        in_specs=[HBM_SPEC] * (2 * nw) + [SEM_SPEC, SEM_SPEC, ANY], out_specs=tuple([HBM_SPEC] * (2 * nw)),
        input_output_aliases={k: k for k in range(2 * nw)},
        compiler_params=pltpu.CompilerParams(has_side_effects=DATAFLOW),
    )(*ss, *lands, send_sems, recv_sems, after)
    return list(out[nw:])


def _share_and_sum_small(fs, pack, name):
    nw = len(fs)
    rows = pack.shape[0]

    def body(*refs):
        ins, pack_ref = refs[:nw], refs[nw]
        outs, sum_ref = refs[nw + 1:2 * nw + 1], refs[2 * nw + 1]
        all_ref, send_sems, recv_sems = refs[2 * nw + 2:]
        x, y, c, _ = _place()
        me = 4 * x + 2 * y + c
        cps = []
        for i in range(nw):
            cp = pltpu.make_async_remote_copy(src_ref=ins[i].at[c], dst_ref=outs[i].at[c], send_sem=send_sems.at[i],
                                              recv_sem=recv_sems.at[i], device_id=(x, y, 1 - c), device_id_type=MESH)
            cp.start()
            cps.append(cp)
        all_ref[me] = pack_ref[...]
        for k in range(1, N_DEV):
            to = (1 - x if k & 4 else x, 1 - y if k & 2 else y, 1 - c if k & 1 else c)
            cp = pltpu.make_async_remote_copy(
                src_ref=pack_ref, dst_ref=all_ref.at[me], send_sem=send_sems.at[nw + k - 1],
                recv_sem=recv_sems.at[nw + k - 1], device_id=to, device_id_type=MESH)
            cp.start()
            cps.append(cp)
        for cp in cps:
            cp.wait()
        total = all_ref[0]
        for d in range(1, N_DEV):
            total = total + all_ref[d]
        sum_ref[...] = total

    n_sems = nw + N_DEV - 1
    out = pl.pallas_call(
        body, name=name, in_specs=[ANY] * nw + [VMEM_SPEC], out_specs=[ANY] * nw + [VMEM_SPEC],
        out_shape=[jax.ShapeDtypeStruct(f.shape, f.dtype) for f in fs] + [jax.ShapeDtypeStruct(pack.shape, F32)],
        input_output_aliases={i: i for i in range(nw)},
        scratch_shapes=[pltpu.VMEM((N_DEV, rows, LANES), F32), pltpu.SemaphoreType.DMA((n_sems,)),
                        pltpu.SemaphoreType.DMA((n_sems,))],
    )(*fs, pack)
    return list(out[:nw]), out[nw]


def _sum_sibling(g, land, where_arr, name):
    P, Q = g.shape[-2:]
    tp = _pick(P, (256, 128, 64))

    def body(w_ref, g_ref, l_ref, s_ref):
        s_ref[...] = (g_ref[...].astype(F32) + l_ref[...].astype(F32)).astype(BF16)

    if g.ndim == 4:
        g_spec = pl.BlockSpec((None, None, tp, Q), lambda j, r, w: (w[1 + j], w[0], r, 0))
    else:
        g_spec = pl.BlockSpec((None, tp, Q), lambda j, r, w: (w[1 + j], r, 0))
    slab = pl.BlockSpec((None, tp, Q), lambda j, r, w: (w[1 + j], r, 0))
    return pl.pallas_call(
        body, name=name,
        grid_spec=pltpu.PrefetchScalarGridSpec(
            num_scalar_prefetch=1, grid=(N_CHIPS - 1, P // tp), in_specs=[g_spec, slab], out_specs=slab),
        out_shape=jax.ShapeDtypeStruct((N_CHIPS, P, Q), BF16),
        compiler_params=_params(("parallel", "parallel")),
    )(where_arr, g, land)


def _sum_chips(g, land, recv, sc_arr, name):
    P, Q = g.shape[-2:]
    tp = _pick(P, (256, 128, 64))

    def body(sc_ref, g_ref, l_ref, r_ref, f_ref):
        acc = g_ref[...].astype(F32) + l_ref[...].astype(F32)
        for j in range(N_CHIPS - 1):
            acc = acc + r_ref[j].astype(F32)
        f_ref[...] = acc

    if g.ndim == 4:
        g_spec = pl.BlockSpec((None, None, tp, Q), lambda r, sc: (sc[0], sc[1], r, 0))
    else:
        g_spec = pl.BlockSpec((None, tp, Q), lambda r, sc: (sc[0], r, 0))
    return pl.pallas_call(
        body, name=name,
        grid_spec=pltpu.PrefetchScalarGridSpec(
            num_scalar_prefetch=1, grid=(P // tp,),
            in_specs=[g_spec,
                      pl.BlockSpec((None, tp, Q), lambda r, sc: (sc[0], r, 0)),
                      pl.BlockSpec((N_CHIPS - 1, tp, Q), lambda r, sc: (0, r, 0))],
            out_specs=pl.BlockSpec((None, tp, Q), lambda r, sc: (sc[1], r, 0))),
        out_shape=jax.ShapeDtypeStruct((2, P, Q), F32),
        compiler_params=_params(("parallel",)),
    )(sc_arr, g, land, recv)


class _Reduction:
    def __init__(self, names, pieces, lands, flight):
        self.names, self.pieces, self.lands = names, pieces, lands
        self.send_sems, self.recv_sems, self.sums, self.zones, self.token = flight


def _reduce_start(pieces, lands, names, where_arr):
    tag = names[0] if len(names) == 1 else "branches"
    sums = [_sum_sibling(g, l, where_arr, "sum_sibling_" + nm) for g, l, nm in zip(pieces, lands, names)]
    return _Reduction(names, pieces, lands, _chips_send_start(sums, "grads_to_chips_start_" + tag))


def _wgrad_half(a, b, half_arr, name, after=None):
    K, M = a.shape
    nslab = b.shape[1] // N_CHIPS
    tm = M // 2
    tn = _pick(nslab, (896, 512, 384, 256, 128))
    per = nslab // tn
    assert 4 * K * (tm + tn) <= MATMUL_OPERAND_BYTES

    def body(h_ref, a_ref, b_ref, *rest):
        rest[-1][...] = lax.dot_general(a_ref[...], b_ref[...], TN_DIMS, preferred_element_type=F32).astype(BF16)

    return pl.pallas_call(
        body, name=name,
        grid_spec=pltpu.PrefetchScalarGridSpec(
            num_scalar_prefetch=1, grid=(b.shape[1] // tn,),
            in_specs=[pl.BlockSpec((K, tm), lambda j, h: (0, h[0])), pl.BlockSpec((K, tn), lambda j, h: (0, j))]
            + ([] if after is None else [ANY]),
            out_specs=pl.BlockSpec((None, tm, tn), lambda j, h: (j // per, 0, j % per))),
        out_shape=jax.ShapeDtypeStruct((N_CHIPS, tm, nslab), BF16),
        compiler_params=_params(("parallel",)),
    )(*((half_arr, a, b) if after is None else (half_arr, a, b, after)))


def _sibling_send_start(g, name):
    zone = pltpu.with_memory_space_constraint(lax.empty(g.shape, g.dtype), pltpu.HBM)

    def body(g_ref, zone_ref, send_sem, recv_sem, g_thru, zone_thru, token):
        x, y, c, _ = _place()
        pltpu.make_async_remote_copy(src_ref=g_ref, dst_ref=zone_ref, send_sem=send_sem, recv_sem=recv_sem,
                                     device_id=(x, y, 1 - c), device_id_type=MESH).start()
        token[...] = jnp.zeros_like(token)

    return pl.pallas_call(
        body, name=name,
        out_shape=(pltpu.SemaphoreType.DMA(()), pltpu.SemaphoreType.DMA(()), pltpu.HBM(g.shape, g.dtype),
                   pltpu.HBM(g.shape, g.dtype), jax.ShapeDtypeStruct((8, LANES), F32)),
        in_specs=[HBM_SPEC, HBM_SPEC], out_specs=(SEM_SPEC, SEM_SPEC, HBM_SPEC, HBM_SPEC, VMEM_SPEC),
        input_output_aliases={0: 2, 1: 3},
        compiler_params=pltpu.CompilerParams(has_side_effects=DATAFLOW),
    )(pltpu.with_memory_space_constraint(g, pltpu.HBM), zone)


def _sibling_send_wait(send_sem, recv_sem, g, zone, after, name):
    def body(g_ref, zone_ref, s_sem, r_sem, after_ref, g_out, zone_out):
        x, y, c, _ = _place()
        cp = pltpu.make_async_remote_copy(src_ref=g_ref, dst_ref=zone_ref, send_sem=s_sem, recv_sem=r_sem,
                                          device_id=(x, y, 1 - c), device_id_type=MESH)
        cp.wait_send()
        cp.wait_recv()

    return pl.pallas_call(
        body, name=name, out_shape=(pltpu.HBM(g.shape, g.dtype), pltpu.HBM(zone.shape, zone.dtype)),
        in_specs=[HBM_SPEC, HBM_SPEC, SEM_SPEC, SEM_SPEC, ANY], out_specs=(HBM_SPEC, HBM_SPEC),
        input_output_aliases={0: 0, 1: 1},
        compiler_params=pltpu.CompilerParams(has_side_effects=DATAFLOW),
    )(g, zone, send_sem, recv_sem, after)[1]


def _reduce_finish(red, after, sc_arr):
    tag = red.names[0] if len(red.names) == 1 else "branches"
    recvs = _chips_send_wait(red.send_sems, red.recv_sems, red.sums, red.zones, after, "grads_to_chips_wait_" + tag)
    return [_sum_chips(g, l, r, sc_arr, "sum_chips_" + nm) for g, l, r, nm in zip(red.pieces, red.lands, recvs, red.names)]


def _t5_bucket(dist):
    max_exact = REL_BUCKETS // 2
    d = jnp.maximum(dist, 0)
    df = jnp.maximum(d, 1).astype(F32)
    large = max_exact + (jnp.log(df / max_exact) / math.log(REL_MAX_DIST / max_exact)
                         * (REL_BUCKETS - max_exact)).astype(jnp.int32)
    large = jnp.minimum(large, REL_BUCKETS - 1)
    return jnp.where(d < max_exact, d, large)


def _bucket_table():
    qi = jnp.arange(WINDOW)[:, None]
    si = jnp.arange(2 * WINDOW)[None, :]
    return _t5_bucket(qi + WINDOW - si)


TILE_WORDS = 8 * LANES


def _tile_rows(shape):
    return -(-math.prod(shape) // TILE_WORDS) * 8


def _rows_of(a):
    flat = a.reshape(-1).astype(F32)
    n = _tile_rows(a.shape) * LANES
    return jnp.pad(flat, (0, n - flat.shape[0])).reshape(-1, LANES)


def _pack_rows(parts):
    return jnp.concatenate([_rows_of(p) for p in parts], axis=0)


def _unpack_rows(packed, shapes):
    out, at = [], 0
    for shp in shapes:
        n, nr = math.prod(shp), _tile_rows(shp)
        out.append(packed[at:at + nr].reshape(-1)[:n].reshape(shp))
        at += nr
    return out


def kernel(x, norm_pre, w_in, rel_bias, attn_sinks, lb_logits, hgrn_norm, w_branch_attn, w_branch_hgrn, w_out, norm_post, loss_target, m_norm_pre, m_w_in, m_rel_bias, m_attn_sinks, m_lb_logits, m_hgrn_norm, m_w_branch_attn, m_w_branch_hgrn, m_w_out, m_norm_post, v_norm_pre, v_w_in, v_rel_bias, v_attn_sinks, v_lb_logits, v_hgrn_norm, v_w_branch_attn, v_w_branch_hgrn, v_w_out, v_norm_post):
    B_loc, S, D = x.shape
    T = B_loc * S
    x2 = x.reshape(T, D)
    tgt2 = loss_target.reshape(T, D)
    my_x, my_y, my_c = lax.axis_index("x"), lax.axis_index("y"), lax.axis_index("c")

    c_arr = jnp.reshape(my_c, (1,)).astype(jnp.int32)
    s_arr = jnp.reshape(2 * my_x + my_y, (1,)).astype(jnp.int32)
    sc_arr = jnp.concatenate([s_arr, c_arr])
    shard_ws = [w_in[0], w_branch_attn[0], w_branch_hgrn[0], w_out[0]]
    shard_axes = (1, 1, 1, 0)
    names = ["w_in", "w_branch_attn", "w_branch_hgrn", "w_out"]
    placed = [_cast_into_full(w, ax, s_arr, "cast_" + nm) for w, ax, nm in zip(shard_ws, shard_axes, names)]
    peer_slabs = [jnp.reshape(t, (1,)).astype(jnp.int32)
                  for t in (2 * (1 - my_x) + my_y, 2 * my_x + 1 - my_y, 2 * (1 - my_x) + 1 - my_y)]

    buckets = _bucket_table()
    onehot = (buckets.reshape(-1)[:, None] == jnp.arange(REL_BUCKETS)[None, :]).astype(F32)
    bias_tab = jnp.dot(onehot, rel_bias.astype(F32), precision=lax.Precision.HIGHEST).T.reshape(ATTN_HEADS, WINDOW, 2 * WINDOW)
    sinks_b = jnp.broadcast_to(attn_sinks[0].astype(F32)[:, None, None], (ATTN_HEADS, 8, LANES))
    lb_fn = lambda l: jnp.cumsum(jax.nn.softmax(l.astype(F32), axis=0), axis=0)[:1]
    lb, lb_vjp = jax.vjp(lb_fn, lb_logits)
    gain_h = hgrn_norm[0].reshape(1, HGRN_WIDTH)

    h, rstd = _rmsnorm_fwd(x2, norm_pre)
    gather_in = _Gather(placed[:1], [shard_ws[0].shape], shard_axes[:1], "w_in")
    gather_in.start([0, 1])
    proj = _matmul_slab(h, gather_in.fulls[0], s_arr, None, gather_in.token, "in_proj_own")
    gather_in.wait([0, 1], proj)
    gather_in.forward([0, 1])
    gather_in.start([2])
    proj = _matmul_slab(h, gather_in.fulls[0], peer_slabs[0], proj, gather_in.token, "in_proj_peer0")
    proj = _matmul_slab(h, gather_in.fulls[0], peer_slabs[1], proj, None, "in_proj_peer1")
    gather_in.wait([2], proj)
    gather_in.forward([2])
    gather_rest = _Gather(placed[1:], [w.shape for w in shard_ws[1:]], shard_axes[1:], "rest")
    gather_rest.start([0, 1, 2], gather_in.fulls[0])
    proj = _matmul_slab(h, gather_in.fulls[0], peer_slabs[2], proj, gather_rest.token, "in_proj_peer2")
    win_f = gather_in.fulls[0]
    ya, attn_probs = _attn_fwd(proj, bias_tab, sinks_b, B_loc, S)
    yh, states = _hgrn_fwd(proj, lb, gain_h, B_loc, S)
    gather_rest.wait([0, 1, 2], yh)
    gather_rest.forward([0, 1, 2])
    wba_f, wbh_f, wout_f = gather_rest.fulls
    ua = _matmul(ya, wba_f, "nn", BF16, "branch_attn_proj")
    uh, merged = _branch_proj_merge(yh, wbh_f, proj, ua)
    yv = _matmul(merged, wout_f, "nn", BF16, "out_proj")
    dy, dout, loss_p, gnpost_p = _post_loss(yv, x2, tgt2, norm_post)

    g_wout = _matmul(merged, dy, "tn", BF16, "out_proj_wgrad")
    d_ua, d_uh, dproj = _out_dgrad_merge_bwd(dy, wout_f, proj, ua, uh)
    g_wba = _matmul(ya, d_ua, "tn", BF16, "branch_attn_wgrad", slabs=N_CHIPS)
    g_wbh = _matmul(yh, d_uh, "tn", BF16, "branch_hgrn_wgrad", slabs=N_CHIPS)
    where_arr = jnp.concatenate([c_arr] + peer_slabs)
    late_pieces = [g.reshape(N_CHIPS, 2, -1, g.shape[-1]) for g in (g_wba, g_wbh, g_wout)]
    late = _reduce_start(late_pieces, _exchange_sibling_halves(late_pieces, "grads_to_sibling_branches"), names[1:], where_arr)
    d_ya = _matmul(d_ua, wba_f, "nt", BF16, "branch_attn_dgrad", after=late.token)
    d_yh = _matmul(d_uh, wbh_f, "nt", BF16, "branch_hgrn_dgrad", after=late.token)
    dproj, dbias_p, dsinks_p = _attn_bwd(proj, attn_probs, d_ya, dproj, B_loc, S)
    dproj, dlb_p, dgain_p = _hgrn_bwd(proj, states, lb, gain_h, d_yh, dproj, B_loc, S)
    g_give = _wgrad_half(h, dproj, 1 - c_arr, "in_proj_wgrad_sibling_half")
    s_sem, r_sem, g_give, zone, token = _sibling_send_start(g_give, "grads_to_sibling_start_w_in")
    g_keep = _wgrad_half(h, dproj, c_arr, "in_proj_wgrad_own_half", after=token)
    land = _sibling_send_wait(s_sem, r_sem, g_give, zone, g_keep, "grads_to_sibling_wait_w_in")
    last = _reduce_start([g_keep], [land], names[:1], where_arr)
    dh = _matmul(dproj, win_f, "nt", BF16, "in_proj_dgrad", after=last.token)
    grad_x2, gnpre_p = _rmsnorm_bwd(dh, x2, rstd, norm_pre, dout)
    halves = _reduce_finish(last, grad_x2, sc_arr) + _reduce_finish(late, grad_x2, sc_arr)

    grelb_p = jnp.dot(dbias_p.reshape(ATTN_HEADS, -1), onehot, precision=lax.Precision.HIGHEST).T
    gsinks_p = dsinks_p[:, 0, 0]
    dlb_sum = jnp.sum(dlb_p, axis=0).reshape(1, HGRN_WIDTH)
    ghn_p = jnp.sum(dgain_p, axis=0).reshape(HGRN_HEADS, HGRN_DIM)
    small_parts = [gnpre_p, gnpost_p, grelb_p, gsinks_p, dlb_sum, ghn_p, loss_p]
    small_shapes = [p.shape for p in small_parts]
    shared, pack_sum = _share_and_sum_small(halves, _pack_rows(small_parts), "grads_share_sibling")
    big_w = [w_in, w_branch_attn, w_branch_hgrn, w_out]
    big_m = [m_w_in, m_w_branch_attn, m_w_branch_hgrn, m_w_out]
    big_v = [v_w_in, v_w_branch_attn, v_w_branch_hgrn, v_w_out]
    big = {}
    for nm, gs, w, m, v in zip(names, shared, big_w, big_m, big_v):
        shp = w.shape
        g2 = gs.reshape(shp[1], shp[2])
        d, nm_, nv_, g_out = _adamw(w[0], g2, m[0], v[0], "adamw_" + nm)
        big[nm] = tuple(a.reshape(shp) for a in (g_out, d, nm_, nv_))

    gnpre, gnpost, grelb, gsinks, dlb, ghn, loss = _unpack_rows(pack_sum, small_shapes)
    (g_lb_logits,) = lb_vjp(dlb)
    small_names = ["norm_pre", "rel_bias", "attn_sinks", "lb_logits", "hgrn_norm", "norm_post"]
    small_w = [norm_pre, rel_bias, attn_sinks, lb_logits, hgrn_norm, norm_post]
    small_m = [m_norm_pre, m_rel_bias, m_attn_sinks, m_lb_logits, m_hgrn_norm, m_norm_post]
    small_v = [v_norm_pre, v_rel_bias, v_attn_sinks, v_lb_logits, v_hgrn_norm, v_norm_post]
    small_g = [gnpre.reshape(norm_pre.shape), grelb.reshape(rel_bias.shape), gsinks.reshape(attn_sinks.shape),
               g_lb_logits.reshape(lb_logits.shape), ghn.reshape(hgrn_norm.shape), gnpost.reshape(norm_post.shape)]
    shapes = [w.shape for w in small_w]
    d_s, nm_s, nv_s, _ = _adamw(_pack_rows(small_w), _pack_rows(small_g), _pack_rows(small_m), _pack_rows(small_v),
                             "adamw_small")
    small = {}
    for nm, g, d, m_, v_ in zip(small_names, small_g, _unpack_rows(d_s, shapes), _unpack_rows(nm_s, shapes),
                                _unpack_rows(nv_s, shapes)):
        small[nm] = (g, d, m_, v_)

    allw = {**big, **small}
    order = ["norm_pre", "w_in", "rel_bias", "attn_sinks", "lb_logits", "hgrn_norm", "w_branch_attn", "w_branch_hgrn",
             "w_out", "norm_post"]
    outs = [loss.reshape(()), grad_x2.reshape(B_loc, S, D)]
    for k in range(4):
        outs += [allw[nm][k] for nm in order]
    return tuple(outs)
```

```python
import math

import jax
import jax.numpy as jnp
from jax import lax
from jax.experimental import pallas as pl
from jax.experimental.pallas import tpu as pltpu

F32 = jnp.float32
BF16 = jnp.bfloat16
MESH = pl.DeviceIdType.MESH

ATTN_HEADS = 16
ATTN_KV_HEADS = 4
HEAD_DIM = 64
GROUP = ATTN_HEADS // ATTN_KV_HEADS
WINDOW = 128
ATTN_WIDTH = ATTN_HEADS * HEAD_DIM
KV_WIDTH = ATTN_KV_HEADS * HEAD_DIM
HGRN_HEADS = 8
HGRN_DIM = 128
HGRN_WIDTH = HGRN_HEADS * HGRN_DIM
CHUNK = 64
SUB = 16
NSUB = CHUNK // SUB
REL_BUCKETS = 32
REL_MAX_DIST = 128
NORM_EPS = 1e-6
ADAM_LR = 0.001
ADAM_B1 = 0.9
ADAM_B2 = 0.999
ADAM_EPS = 1e-08
ADAM_WD = 0.01
ADAM_STEP = 10
LANES = 128
N_CHIPS = 4
N_DEV = 8
VMEM_LIMIT = 48 * 1024 * 1024
MATMUL_OPERAND_BYTES = 34 * 1024 * 1024
MATMUL_VMEM_BYTES = 44 * 1024 * 1024

OFF_AQ = 0
OFF_AK = OFF_AQ + ATTN_WIDTH
OFF_AV = OFF_AK + KV_WIDTH
OFF_AG = OFF_AV + KV_WIDTH
OFF_HQ = OFF_AG + ATTN_WIDTH
OFF_HF = OFF_HQ + HGRN_WIDTH
OFF_HI = OFF_HF + HGRN_WIDTH
OFF_HG = OFF_HI + HGRN_WIDTH
OFF_GA = OFF_HG + HGRN_WIDTH

NT_DIMS = (((1,), (1,)), ((), ()))
TN_DIMS = (((0,), (0,)), ((), ()))
NN_DIMS = (((1,), (0,)), ((), ()))


def _pick(n, cands):
    for c in cands:
        if n % c == 0:
            return c
    raise ValueError(f"no tile for {n} in {cands}")


def _params(sem):
    return pltpu.CompilerParams(dimension_semantics=sem, vmem_limit_bytes=VMEM_LIMIT)


def _bdot(a, b, dims=NN_DIMS):
    return lax.dot_general(a.astype(BF16), b.astype(BF16), dims, preferred_element_type=F32)


def _matmul(a, b, mode, out_dtype, name, slabs=1, after=None):
    if mode == "nn":
        (M, K), (K2, N) = a.shape, b.shape
    elif mode == "nt":
        (M, K), (N, K2) = a.shape, b.shape
    else:
        (K, M), (K2, N) = a.shape, b.shape
    assert K == K2
    nslab = N // slabs
    tm = _pick(M, (1024, 512, 256, 128))
    out_bytes = jnp.dtype(out_dtype).itemsize
    choices = []
    for tn in (2688, 1024, 896, 512, 384, 256, 128):
        for tk in (4096, 3584, 2048, 1792, 1536, 1024, 512, 256, 128):
            acc = 0 if tk == K else 4 * tm * tn
            if (nslab % tn == 0 and K % tk == 0 and 4 * tk * (tm + tn) <= MATMUL_OPERAND_BYTES
                    and 4 * tk * (tm + tn) + 2 * out_bytes * tm * tn + acc <= MATMUL_VMEM_BYTES):
                choices.append((K // tk > 1, -tn, tn, tk))
                break
    _, _, tn, tk = min(choices)
    nk = K // tk
    per = nslab // tn
    dims = {"nn": NN_DIMS, "nt": NT_DIMS, "tn": TN_DIMS}[mode]

    n_in = 2 if after is None else 3

    def body(*refs):
        a_ref, b_ref, o_ref, acc = refs[0], refs[1], refs[n_in], refs[n_in + 1:]
        part = lax.dot_general(a_ref[...], b_ref[...], dims, preferred_element_type=F32)
        if nk == 1:
            o_ref[...] = part.astype(o_ref.dtype)
            return
        acc_ref, = acc
        k = pl.program_id(2)

        @pl.when(k == 0)
        def _():
            acc_ref[...] = part

        @pl.when((k > 0) & (k < nk - 1))
        def _():
            acc_ref[...] += part

        @pl.when(k == nk - 1)
        def _():
            o_ref[...] = (acc_ref[...] + part).astype(o_ref.dtype)

    if mode == "tn":
        a_spec = pl.BlockSpec((tk, tm), lambda i, j, k: (k, i))
    else:
        a_spec = pl.BlockSpec((tm, tk), lambda i, j, k: (i, k))
    if mode == "nt":
        b_spec = pl.BlockSpec((tn, tk), lambda i, j, k: (j, k))
    else:
        b_spec = pl.BlockSpec((tk, tn), lambda i, j, k: (k, j))
    if slabs == 1:
        o_shape = jax.ShapeDtypeStruct((M, N), out_dtype)
        o_spec = pl.BlockSpec((tm, tn), lambda i, j, k: (i, j))
    else:
        o_shape = jax.ShapeDtypeStruct((slabs, M, nslab), out_dtype)
        o_spec = pl.BlockSpec((None, tm, tn), lambda i, j, k: (j // per, i, j % per))
    return pl.pallas_call(
        body, name=name, grid=(M // tm, N // tn, nk), in_specs=[a_spec, b_spec] + ([] if after is None else [ANY]),
        out_specs=o_spec, out_shape=o_shape,
        scratch_shapes=[pltpu.VMEM((tm, tn), F32)] if nk > 1 else [],
        compiler_params=_params(("parallel", "parallel", "arbitrary")),
    )(*((a, b) if after is None else (a, b, after)))


def _rmsnorm_fwd(x2, gain):
    T, D = x2.shape
    tr = _pick(T, (256, 128))

    def body(x_ref, g_ref, h_ref, r_ref):
        xv = x_ref[...]
        r = lax.rsqrt(jnp.mean(xv * xv, axis=-1, keepdims=True) + NORM_EPS)
        h_ref[...] = (xv * r * g_ref[...]).astype(BF16)
        r_ref[...] = r

    return pl.pallas_call(
        body, name="rmsnorm_pre_fwd", grid=(T // tr,),
        in_specs=[pl.BlockSpec((tr, D), lambda i: (i, 0)), pl.BlockSpec((1, D), lambda i: (0, 0))],
        out_specs=[pl.BlockSpec((tr, D), lambda i: (i, 0)), pl.BlockSpec((tr, 1), lambda i: (i, 0))],
        out_shape=[jax.ShapeDtypeStruct((T, D), BF16), jax.ShapeDtypeStruct((T, 1), F32)],
        compiler_params=_params(("parallel",)),
    )(x2, gain)


def _rmsnorm_bwd(dh, x2, rstd, gain, dout):
    T, D = x2.shape
    tr = _pick(T, (256, 128))

    def body(dh_ref, x_ref, r_ref, g_ref, do_ref, gx_ref, gg_ref):
        @pl.when(pl.program_id(0) == 0)
        def _():
            gg_ref[...] = jnp.zeros_like(gg_ref)

        n = x_ref[...] * r_ref[...]
        dhv = dh_ref[...].astype(F32)
        dn = dhv * g_ref[...]
        gx_ref[...] = do_ref[...] + r_ref[...] * (dn - n * jnp.mean(dn * n, axis=-1, keepdims=True))
        gg_ref[...] += jnp.sum(dhv * n, axis=0, keepdims=True)

    row = pl.BlockSpec((tr, D), lambda i: (i, 0))
    vec = pl.BlockSpec((1, D), lambda i: (0, 0))
    return pl.pallas_call(
        body, name="rmsnorm_pre_bwd", grid=(T // tr,),
        in_specs=[row, row, pl.BlockSpec((tr, 1), lambda i: (i, 0)), vec, row],
        out_specs=[row, vec],
        out_shape=[jax.ShapeDtypeStruct((T, D), F32), jax.ShapeDtypeStruct((1, D), F32)],
        compiler_params=_params(("arbitrary",)),
    )(dh, x2, rstd, gain, dout)


def _post_loss(yv, x2, tgt2, gain):
    T, D = x2.shape
    tr = _pick(T, (256, 128))

    def body(y_ref, x_ref, t_ref, g_ref, dy_ref, do_ref, loss_ref, gg_ref):
        @pl.when(pl.program_id(0) == 0)
        def _():
            gg_ref[...] = jnp.zeros_like(gg_ref)
            loss_ref[...] = jnp.zeros_like(loss_ref)

        yv_ = y_ref[...].astype(F32)
        r = lax.rsqrt(jnp.mean(yv_ * yv_, axis=-1, keepdims=True) + NORM_EPS)
        n = yv_ * r
        e = (x_ref[...] + n * g_ref[...]) - t_ref[...]
        loss_ref[...] += 0.5 * jnp.sum(jnp.mean(e * e, axis=-1, keepdims=True), axis=0, keepdims=True)
        dz = e / D
        do_ref[...] = dz
        gg_ref[...] += jnp.sum(dz * n, axis=0, keepdims=True)
        dn = dz * g_ref[...]
        dy_ref[...] = (r * (dn - n * jnp.mean(dn * n, axis=-1, keepdims=True))).astype(BF16)

    row = pl.BlockSpec((tr, D), lambda i: (i, 0))
    vec = pl.BlockSpec((1, D), lambda i: (0, 0))
    return pl.pallas_call(
        body, name="post_norm_loss", grid=(T // tr,),
        in_specs=[row, row, row, vec],
        out_specs=[row, row, pl.BlockSpec((1, 1), lambda i: (0, 0)), vec],
        out_shape=[jax.ShapeDtypeStruct((T, D), BF16), jax.ShapeDtypeStruct((T, D), F32),
                   jax.ShapeDtypeStruct((1, 1), F32), jax.ShapeDtypeStruct((1, D), F32)],
        compiler_params=_params(("arbitrary",)),
    )(yv, x2, tgt2, gain)


def _window(rows, cols, at):
    return pl.BlockSpec((pl.Element(rows), pl.Element(cols)), at)


def _gate_windows(tm, tn, D):
    return [_window(tm, tn, lambda i, j: (i * tm, pl.multiple_of(OFF_GA + j * tn, LANES))),
            _window(tm, tn, lambda i, j: (i * tm, pl.multiple_of(OFF_GA + D + j * tn, LANES)))]


def _branch_proj_merge(yh, wbh, proj, ua):
    T, K = yh.shape
    D = wbh.shape[1]
    tm, tn = _pick(T, (512, 256, 128)), _pick(D, (1024, 512, 256))

    def body(a_ref, b_ref, ga_ref, gh_ref, ua_ref, uh_ref, m_ref):
        uh = jnp.dot(a_ref[...], b_ref[...], preferred_element_type=F32).astype(BF16)
        uh_ref[...] = uh
        m_ref[...] = (jax.nn.sigmoid(ga_ref[...]) * ua_ref[...].astype(F32)
                      + jax.nn.sigmoid(gh_ref[...]) * uh.astype(F32)).astype(BF16)

    blk = pl.BlockSpec((tm, tn), lambda i, j: (i, j))
    o = jax.ShapeDtypeStruct((T, D), BF16)
    return pl.pallas_call(
        body, name="branch_hgrn_proj_merge", grid=(T // tm, D // tn),
        in_specs=[pl.BlockSpec((tm, K), lambda i, j: (i, 0)), pl.BlockSpec((K, tn), lambda i, j: (0, j))]
        + _gate_windows(tm, tn, D) + [blk],
        out_specs=[blk, blk], out_shape=[o, o],
        compiler_params=_params(("parallel", "arbitrary")),
    )(yh, wbh, proj, proj, ua)


def _out_dgrad_merge_bwd(dy, wout, proj, ua, uh):
    T, K = dy.shape
    D = wout.shape[0]
    tm, tn = _pick(T, (512, 256, 128)), _pick(D, (1024, 512, 256))
    nj = D // tn

    def body(a_ref, b_ref, ga_ref, gh_ref, ua_ref, uh_ref, dua_ref, duh_ref, dproj_ref):
        d = lax.dot_general(a_ref[...], b_ref[...], NT_DIMS, preferred_element_type=F32)
        sa = jax.nn.sigmoid(ga_ref[...])
        sh = jax.nn.sigmoid(gh_ref[...])
        dua_ref[...] = (d * sa).astype(BF16)
        duh_ref[...] = (d * sh).astype(BF16)
        dga = (d * ua_ref[...].astype(F32) * sa * (1.0 - sa)).astype(BF16)
        dgh = (d * uh_ref[...].astype(F32) * sh * (1.0 - sh)).astype(BF16)
        for jj in range(nj):
            @pl.when(pl.program_id(1) == jj)
            def _():
                dproj_ref[:, jj * tn:(jj + 1) * tn] = dga
                dproj_ref[:, D + jj * tn:D + (jj + 1) * tn] = dgh

    blk = pl.BlockSpec((tm, tn), lambda i, j: (i, j))
    o = jax.ShapeDtypeStruct((T, D), BF16)
    return pl.pallas_call(
        body, name="out_proj_dgrad_merge_bwd", grid=(T // tm, nj),
        in_specs=[pl.BlockSpec((tm, K), lambda i, j: (i, 0)), pl.BlockSpec((tn, K), lambda i, j: (j, 0))]
        + _gate_windows(tm, tn, D) + [blk, blk],
        out_specs=[blk, blk, _window(tm, 2 * D, lambda i, j: (i * tm, OFF_GA))],
        out_shape=[o, o, jax.ShapeDtypeStruct((T, proj.shape[1]), BF16)],
        compiler_params=_params(("parallel", "arbitrary")),
    )(dy, wout, proj, proj, ua, uh)


KV_PAIR = 4
PAIR_HEADS = KV_PAIR * GROUP


def _attn_mask(n):
    qi = lax.broadcasted_iota(jnp.int32, (WINDOW, 2 * WINDOW), 0)
    si = lax.broadcasted_iota(jnp.int32, (WINDOW, 2 * WINDOW), 1)
    dist = qi + WINDOW - si
    return (dist >= 0) & (dist < WINDOW) & ((si >= WINDOW) | (n > 0))


def _first_key_column(shape):
    return lax.broadcasted_iota(jnp.int32, shape, 1) == 0


def _attn_group_fwd(mask, q, k, v, ag, bias, sink):
    mask = jnp.concatenate([mask] * GROUP, axis=0)
    s = lax.dot_general(q.astype(BF16), k.astype(BF16), NT_DIMS, preferred_element_type=F32) * (HEAD_DIM ** -0.5)
    s = jnp.where(mask, s + bias, -1e30)
    m = jnp.maximum(jnp.max(s, axis=-1, keepdims=True), sink)
    p = jnp.exp(s - m)
    e_sink = jnp.exp(sink - m)
    den = jnp.sum(p, axis=-1, keepdims=True) + e_sink
    pb = p.astype(BF16)
    o = jnp.dot(pb, v.astype(BF16), preferred_element_type=F32)
    return o * (jax.nn.silu(ag) / den), pb, e_sink.astype(BF16)


def _attn_group_bwd(q, k, v, ag, kept, dout):
    e_sink = kept[:, 0:1].astype(F32)
    pb = jnp.where(_first_key_column(kept.shape), jnp.zeros_like(kept), kept)
    pf = pb.astype(F32)
    rden = 1.0 / (jnp.sum(pf, axis=-1, keepdims=True) + e_sink)
    qb, kb, vb = q.astype(BF16), k.astype(BF16), v.astype(BF16)
    o = jnp.dot(pb, vb, preferred_element_type=F32) * rden
    sg = jax.nn.sigmoid(ag)
    d_o = dout * (ag * sg)
    dag = dout * o * (sg * (1.0 + ag * (1.0 - sg)))
    d_row = jnp.sum(d_o * o, axis=-1, keepdims=True)
    ds = (pf * rden) * (lax.dot_general(d_o.astype(BF16), vb, NT_DIMS, preferred_element_type=F32) - d_row)
    dsb = ds.astype(BF16)
    dq = jnp.dot(dsb, kb, preferred_element_type=F32) * (HEAD_DIM ** -0.5)
    dk = lax.dot_general(dsb, qb, TN_DIMS, preferred_element_type=F32) * (HEAD_DIM ** -0.5)
    dv = lax.dot_general(pb, (d_o * rden).astype(BF16), TN_DIMS, preferred_element_type=F32)
    return dq, dk, dv, dag, ds, -(e_sink * rden * d_row)


def _attn_operands(q_ref, kc_ref, kp_ref, vc_ref, vp_ref, ag_ref, bias_ref, sink_ref, j):
    lo, hi = j * HEAD_DIM, (j + 1) * HEAD_DIM
    k = jnp.concatenate([kp_ref[:, lo:hi], kc_ref[:, lo:hi]], axis=0)
    v = jnp.concatenate([vp_ref[:, lo:hi], vc_ref[:, lo:hi]], axis=0)
    heads = [j * GROUP + g for g in range(GROUP)]
    q = jnp.concatenate([q_ref[:, h * HEAD_DIM:(h + 1) * HEAD_DIM] for h in heads], axis=0)
    ag = jnp.concatenate([ag_ref[:, h * HEAD_DIM:(h + 1) * HEAD_DIM] for h in heads], axis=0)
    bias = jnp.concatenate([bias_ref[h] for h in heads], axis=0)
    sink = jnp.concatenate([jnp.broadcast_to(sink_ref[h, 0:1, 0:1], (WINDOW, 1)) for h in heads], axis=0)
    return q, k, v, ag, bias, sink


def _attn_fwd(proj, bias_tab, sinks_b, B_loc, S):
    T = B_loc * S
    nb = S // WINDOW

    def at(off, back=0):
        return lambda b, n: ((b * nb + jnp.maximum(n - back, 0)) * WINDOW, off)

    in_specs = [_window(WINDOW, ATTN_WIDTH, at(OFF_AQ)), _window(WINDOW, KV_WIDTH, at(OFF_AK)),
                _window(WINDOW, KV_WIDTH, at(OFF_AK, 1)), _window(WINDOW, KV_WIDTH, at(OFF_AV)),
                _window(WINDOW, KV_WIDTH, at(OFF_AV, 1)), _window(WINDOW, ATTN_WIDTH, at(OFF_AG)),
                pl.BlockSpec((ATTN_HEADS, WINDOW, 2 * WINDOW), lambda b, n: (0, 0, 0)),
                pl.BlockSpec((ATTN_HEADS, 8, LANES), lambda b, n: (0, 0, 0))]

    def body(q_ref, kc_ref, kp_ref, vc_ref, vp_ref, ag_ref, bias_ref, sink_ref, ya_ref, p_ref):
        mask = _attn_mask(pl.program_id(1))
        for j in range(ATTN_KV_HEADS):
            out, probs, e_sink = _attn_group_fwd(
                mask, *_attn_operands(q_ref, kc_ref, kp_ref, vc_ref, vp_ref, ag_ref, bias_ref, sink_ref, j))
            for g in range(GROUP):
                h = j * GROUP + g
                blk = slice(g * WINDOW, (g + 1) * WINDOW)
                ya_ref[:, h * HEAD_DIM:(h + 1) * HEAD_DIM] = out[blk].astype(BF16)
                p_ref[:, h * 2 * WINDOW:(h + 1) * 2 * WINDOW] = probs[blk]
                p_ref[:, h * 2 * WINDOW:h * 2 * WINDOW + 1] = e_sink[blk]

    return pl.pallas_call(
        body, name="attn_fwd", grid=(B_loc, nb), in_specs=in_specs,
        out_specs=[pl.BlockSpec((WINDOW, ATTN_WIDTH), lambda b, n: (b * nb + n, 0)),
                   pl.BlockSpec((WINDOW, ATTN_HEADS * 2 * WINDOW), lambda b, n: (b * nb + n, 0))],
        out_shape=[jax.ShapeDtypeStruct((T, ATTN_WIDTH), BF16),
                   jax.ShapeDtypeStruct((T, ATTN_HEADS * 2 * WINDOW), BF16)],
        compiler_params=_params(("parallel", "parallel")),
    )(proj, proj, proj, proj, proj, proj, bias_tab, sinks_b)


def _attn_bwd(proj, probs, dya, dproj, B_loc, S):
    nb = S // WINDOW
    n_pairs = ATTN_KV_HEADS // KV_PAIR
    qw, kw = PAIR_HEADS * HEAD_DIM, KV_PAIR * HEAD_DIM

    def at(off, width, back=0):
        return lambda b, i, p: ((b * nb + jnp.maximum(nb - 1 - i - back, 0)) * WINDOW,
                                pl.multiple_of(off + p * width, LANES))

    in_specs = [_window(WINDOW, qw, at(OFF_AQ, qw)), _window(WINDOW, kw, at(OFF_AK, kw)),
                _window(WINDOW, kw, at(OFF_AK, kw, 1)), _window(WINDOW, kw, at(OFF_AV, kw)),
                _window(WINDOW, kw, at(OFF_AV, kw, 1)), _window(WINDOW, qw, at(OFF_AG, qw)),
                pl.BlockSpec((WINDOW, PAIR_HEADS * 2 * WINDOW), lambda b, i, p: (b * nb + nb - 1 - i, p)),
                pl.BlockSpec((WINDOW, qw), lambda b, i, p: (b * nb + nb - 1 - i, p)), ANY]

    def body(q_ref, kc_ref, kp_ref, vc_ref, vp_ref, ag_ref, p_ref, dya_ref, dproj_in,
             dproj_ref, dbias_ref, dsink_ref, dkc_ref, dvc_ref):
        b, i, p = pl.program_id(0), pl.program_id(1), pl.program_id(2)

        @pl.when((b == 0) & (i == 0) & (p == 0))
        def _():
            dbias_ref[...] = jnp.zeros_like(dbias_ref)
            dsink_ref[...] = jnp.zeros_like(dsink_ref)

        @pl.when(i == 0)
        def _():
            dkc_ref[p] = jnp.zeros((WINDOW, kw), F32)
            dvc_ref[p] = jnp.zeros((WINDOW, kw), F32)

        dk_carry, dv_carry = dkc_ref[p], dvc_ref[p]
        dqs, dags, dbiases, dsinks, dks, dvs = [], [], [], [], [], []
        for j in range(KV_PAIR):
            heads = [j * GROUP + g for g in range(GROUP)]
            lo, hi = j * HEAD_DIM, (j + 1) * HEAD_DIM
            stack = lambda parts: jnp.concatenate(parts, axis=0)
            k = stack([kp_ref[:, lo:hi], kc_ref[:, lo:hi]])
            v = stack([vp_ref[:, lo:hi], vc_ref[:, lo:hi]])
            q = stack([q_ref[:, h * HEAD_DIM:(h + 1) * HEAD_DIM] for h in heads])
            ag = stack([ag_ref[:, h * HEAD_DIM:(h + 1) * HEAD_DIM] for h in heads])
            dout = stack([dya_ref[:, h * HEAD_DIM:(h + 1) * HEAD_DIM].astype(F32) for h in heads])
            kept = stack([p_ref[:, h * 2 * WINDOW:(h + 1) * 2 * WINDOW] for h in heads])
            dq, dk, dv, dag, dbias, dsink = _attn_group_bwd(q, k, v, ag, kept, dout)
            dks.append((dk[WINDOW:] + dk_carry[:, lo:hi]).astype(BF16))
            dvs.append((dv[WINDOW:] + dv_carry[:, lo:hi]).astype(BF16))
            dkc_ref[p, :, lo:hi] = dk[:WINDOW]
            dvc_ref[p, :, lo:hi] = dv[:WINDOW]
            for g in range(GROUP):
                blk = slice(g * WINDOW, (g + 1) * WINDOW)
                dqs.append(dq[blk].astype(BF16))
                dags.append(dag[blk].astype(BF16))
                dbiases.append(dbias[blk])
                dsinks.append(jnp.broadcast_to(jnp.sum(dsink[blk], axis=0, keepdims=True), (8, LANES)))

        for pair in range(n_pairs):
            @pl.when(p == pair)
            def _():
                for j in range(KV_PAIR):
                    col = (pair * KV_PAIR + j) * HEAD_DIM
                    dproj_ref[:, OFF_AK + col:OFF_AK + col + HEAD_DIM] = dks[j]
                    dproj_ref[:, OFF_AV + col:OFF_AV + col + HEAD_DIM] = dvs[j]
                for hh in range(PAIR_HEADS):
                    h = pair * PAIR_HEADS + hh
                    dproj_ref[:, OFF_AQ + h * HEAD_DIM:OFF_AQ + (h + 1) * HEAD_DIM] = dqs[hh]
                    dproj_ref[:, OFF_AG + h * HEAD_DIM:OFF_AG + (h + 1) * HEAD_DIM] = dags[hh]
                    dbias_ref[h] += dbiases[hh]
                    dsink_ref[h] += dsinks[hh]

    return pl.pallas_call(
        body, name="attn_bwd", grid=(B_loc, nb, n_pairs), in_specs=in_specs,
        out_specs=[_window(WINDOW, OFF_HQ, lambda b, i, p: ((b * nb + nb - 1 - i) * WINDOW, 0)),
                   pl.BlockSpec((ATTN_HEADS, WINDOW, 2 * WINDOW), lambda b, i, p: (0, 0, 0)),
                   pl.BlockSpec((ATTN_HEADS, 8, LANES), lambda b, i, p: (0, 0, 0))],
        out_shape=[jax.ShapeDtypeStruct(dproj.shape, dproj.dtype),
                   jax.ShapeDtypeStruct((ATTN_HEADS, WINDOW, 2 * WINDOW), F32),
                   jax.ShapeDtypeStruct((ATTN_HEADS, 8, LANES), F32)],
        input_output_aliases={8: 0},
        scratch_shapes=[pltpu.VMEM((n_pairs, WINDOW, kw), F32), pltpu.VMEM((n_pairs, WINDOW, kw), F32)],
        compiler_params=_params(("arbitrary", "arbitrary", "arbitrary")),
    )(proj, proj, proj, proj, proj, proj, probs, dya, dproj)


class _HgrnPre:
    def __init__(self, fr, qr, lb, g_scr):
        t = lax.broadcasted_iota(jnp.int32, (CHUNK, CHUNK), 0)
        s = lax.broadcasted_iota(jnp.int32, (CHUNK, CHUNK), 1)
        self.sg = jax.nn.sigmoid(fr)
        self.f = lb + (1.0 - lb) * self.sg
        g = jnp.dot((t >= s).astype(F32), jnp.log(self.f), precision=lax.Precision.HIGHEST, preferred_element_type=F32)
        g_scr[...] = g
        self.g = g
        self.row = lax.broadcasted_iota(jnp.int32, g.shape, 0)
        self.refs = [jnp.zeros((1, g.shape[1]), F32)] + [g_scr[pl.ds(i * SUB - 1, 1), :] for i in range(1, NSUB)]
        self.gend = g_scr[pl.ds(CHUNK - 1, 1), :]
        refrow = jnp.zeros_like(g)
        for i in range(1, NSUB):
            refrow = jnp.where(self.row >= i * SUB, self.refs[i], refrow)
        self.sigq = jax.nn.sigmoid(qr)
        self.qs = qr * self.sigq
        self.k = 1.0 - self.f
        self.eg = jnp.exp(g)
        self.eqd = jnp.exp(g - refrow)
        self.ekd = [jnp.exp(jnp.where(self.row < (i + 1) * SUB, self.refs[i] - g, 0.0)) for i in range(NSUB)]
        self.ekdec = jnp.exp(self.gend - g)
        self.qg = self.qs * self.eg
        self.qd = self.qs * self.eqd
        self.kd = [self.k * e for e in self.ekd]
        self.kdec = self.k * self.ekdec
        self.egend = jnp.exp(self.gend)


def _hgrn_pair_mask():
    t = lax.broadcasted_iota(jnp.int32, (CHUNK, NSUB * CHUNK), 0)
    col = lax.broadcasted_iota(jnp.int32, (CHUNK, NSUB * CHUNK), 1)
    return ((t // SUB) == (col // CHUNK)) & ((col % CHUNK) <= t)


def _hgrn_head_out(p, lanes, state_t, v, mask):
    qg, qd = p.qg[:, lanes], p.qd[:, lanes]
    kall = jnp.concatenate([kd[:, lanes] for kd in p.kd], axis=0)
    vst = jnp.concatenate([v] * NSUB, axis=0)
    am = jnp.where(mask, _bdot(qd, kall, NT_DIMS), 0.0)
    o = _bdot(qg, state_t, NT_DIMS) + _bdot(am, vst)
    return o, (qg, qd, kall, am, vst)


def _hgrn_fwd(proj, lb, gain, B_loc, S):
    T = B_loc * S
    nc = S // CHUNK
    nh = HGRN_HEADS

    def at(off):
        return lambda b, n: ((b * nc + n) * CHUNK, off)

    vec = pl.BlockSpec((1, HGRN_WIDTH), lambda b, n: (0, 0))

    def body(q_ref, f_ref, v_ref, hg_ref, lb_ref, gain_ref, yh_ref, st_ref, state_scr, g_scr):
        @pl.when(pl.program_id(1) == 0)
        def _():
            state_scr[...] = jnp.zeros_like(state_scr)

        p = _HgrnPre(f_ref[...], q_ref[...], lb_ref[...], g_scr)
        v = v_ref[...]
        gate = gain_ref[...] * jax.nn.silu(hg_ref[...])
        mask = _hgrn_pair_mask()
        for hd in range(nh):
            lanes = slice(hd * HGRN_DIM, (hd + 1) * HGRN_DIM)
            st = state_scr[hd]
            st_ref[hd] = st
            o, _ = _hgrn_head_out(p, lanes, st, v[:, lanes], mask)
            rs = lax.rsqrt(jnp.mean(o * o, axis=-1, keepdims=True) + NORM_EPS)
            yh_ref[:, lanes] = (o * rs * gate[:, lanes]).astype(BF16)
            state_scr[hd] = st * p.egend[:, lanes] + _bdot(v[:, lanes], p.kdec[:, lanes], TN_DIMS)

    return pl.pallas_call(
        body, name="hgrn_fwd", grid=(B_loc, nc),
        in_specs=[_window(CHUNK, HGRN_WIDTH, at(OFF_HQ)), _window(CHUNK, HGRN_WIDTH, at(OFF_HF)),
                  _window(CHUNK, HGRN_WIDTH, at(OFF_HI)), _window(CHUNK, HGRN_WIDTH, at(OFF_HG)), vec, vec],
        out_specs=[pl.BlockSpec((CHUNK, HGRN_WIDTH), lambda b, n: (b * nc + n, 0)),
                   pl.BlockSpec((None, None, nh, HGRN_DIM, HGRN_DIM), lambda b, n: (b, n, 0, 0, 0))],
        out_shape=[jax.ShapeDtypeStruct((T, HGRN_WIDTH), BF16),
                   jax.ShapeDtypeStruct((B_loc, nc, nh, HGRN_DIM, HGRN_DIM), F32)],
        scratch_shapes=[pltpu.VMEM((nh, HGRN_DIM, HGRN_DIM), F32), pltpu.VMEM((CHUNK, HGRN_WIDTH), F32)],
        compiler_params=_params(("parallel", "arbitrary")),
    )(proj, proj, proj, proj, lb, gain)


def _hgrn_bwd(proj, states, lb, gain, dyh, dproj, B_loc, S):
    nc = S // CHUNK
    nh = HGRN_HEADS

    def at(off):
        return lambda b, i: ((b * nc + nc - 1 - i) * CHUNK, off)

    vec = pl.BlockSpec((1, HGRN_WIDTH), lambda b, i: (0, 0))
    in_specs = [_window(CHUNK, HGRN_WIDTH, at(OFF_HQ)), _window(CHUNK, HGRN_WIDTH, at(OFF_HF)),
                _window(CHUNK, HGRN_WIDTH, at(OFF_HI)), _window(CHUNK, HGRN_WIDTH, at(OFF_HG)), vec, vec,
                pl.BlockSpec((None, None, nh, HGRN_DIM, HGRN_DIM), lambda b, i: (b, nc - 1 - i, 0, 0, 0)),
                pl.BlockSpec((CHUNK, HGRN_WIDTH), lambda b, i: (b * nc + nc - 1 - i, 0)), ANY]
    acc_spec = pl.BlockSpec((None, 1, HGRN_WIDTH), lambda b, i: (b, 0, 0))

    def body(q_ref, f_ref, v_ref, hg_ref, lb_ref, gain_ref, st_ref, dyh_ref, dproj_in,
             dproj_ref, dlb_ref, dgain_ref, dstate_scr, g_scr, dg_scr):
        dq_ref, df_ref, dv_ref, dhg_ref = [dproj_ref.at[:, pl.ds(k * HGRN_WIDTH, HGRN_WIDTH)] for k in range(4)]

        @pl.when(pl.program_id(1) == 0)
        def _():
            dstate_scr[...] = jnp.zeros_like(dstate_scr)
            dlb_ref[...] = jnp.zeros_like(dlb_ref)
            dgain_ref[...] = jnp.zeros_like(dgain_ref)

        qr, lb, gain, hg, v = q_ref[...], lb_ref[...], gain_ref[...], hg_ref[...], v_ref[...]
        p = _HgrnPre(f_ref[...], qr, lb, g_scr)
        sgh = jax.nn.sigmoid(hg)
        sil = hg * sgh
        dy = dyh_ref[...].astype(F32)
        mask = _hgrn_pair_mask()
        dqg, dqd, dkdec, dv, dhg, dgend, dgain = [], [], [], [], [], [], []
        dkd = [[] for _ in range(NSUB)]
        heads = [slice(hd * HGRN_DIM, (hd + 1) * HGRN_DIM) for hd in range(nh)]
        sts = [st_ref[hd] for hd in range(nh)]
        dnews = [dstate_scr[hd] for hd in range(nh)]
        fwd = [_hgrn_head_out(p, lanes, st, v[:, lanes], mask) for lanes, st in zip(heads, sts)]
        for lanes, st, dnew in zip(heads, sts, dnews):
            dkdec_h = _bdot(v[:, lanes], dnew)
            dkdec.append(dkdec_h)
            dgend.append(jnp.sum(dkdec_h * p.kdec[:, lanes], axis=0, keepdims=True)
                         + jnp.sum(dnew * st, axis=0, keepdims=True) * p.egend[:, lanes])
        dos = []
        gate_grad = sgh * (1.0 + hg * (1.0 - sgh))
        for lanes, (o, _) in zip(heads, fwd):
            rs = lax.rsqrt(jnp.mean(o * o, axis=-1, keepdims=True) + NORM_EPS)
            n = o * rs
            dyn = dy[:, lanes] * n
            dgain.append(jnp.sum(dyn * sil[:, lanes], axis=0, keepdims=True))
            dhg.append(dyn * gain[:, lanes] * gate_grad[:, lanes])
            dn = dy[:, lanes] * gain[:, lanes] * sil[:, lanes]
            dos.append(rs * (dn - n * jnp.mean(dn * n, axis=-1, keepdims=True)))
        drs = []
        for hd, (lanes, st, dnew, do, (_, (qg, qd, kall, am, vst))) in enumerate(zip(heads, sts, dnews, dos, fwd)):
            dqg.append(_bdot(do, st))
            dstate_scr[hd] = _bdot(do, qg, TN_DIMS) + dnew * p.egend[:, lanes]
            drs.append(jnp.where(mask, _bdot(do, vst, NT_DIMS), 0.0))
            dvst = _bdot(am, do, TN_DIMS)
            dv.append(sum(dvst[i * CHUNK:(i + 1) * CHUNK] for i in range(NSUB)) + _bdot(p.kdec[:, lanes], dnew, NT_DIMS))
        for dr, (_, (qg, qd, kall, am, vst)) in zip(drs, fwd):
            dqd.append(_bdot(dr, kall))
            dkall = _bdot(dr, qd, TN_DIMS)
            for i in range(NSUB):
                dkd[i].append(dkall[i * CHUNK:(i + 1) * CHUNK])

        wide = lambda parts: jnp.concatenate(parts, axis=1)
        dqg, dqd, dkdec = wide(dqg), wide(dqd), wide(dkdec)
        t2 = dqd * p.qd
        dg = dqg * p.qg + t2 - dkdec * p.kdec
        dk = dkdec * p.ekdec
        dg_scr[...] = jnp.zeros_like(dg_scr)
        for i in range(NSUB):
            dkd_i = wide(dkd[i])
            tk = jnp.where(p.row < (i + 1) * SUB, dkd_i * p.kd[i], 0.0)
            dg = dg - tk
            dk = dk + dkd_i * p.ekd[i]
            if i >= 1:
                in_blk = (p.row >= i * SUB) & (p.row < (i + 1) * SUB)
                dg_scr[pl.ds(i * SUB - 1, 1), :] = (jnp.sum(tk, axis=0, keepdims=True)
                                                    - jnp.sum(jnp.where(in_blk, t2, 0.0), axis=0, keepdims=True))
        dg_scr[pl.ds(CHUNK - 1, 1), :] = wide(dgend)
        t = lax.broadcasted_iota(jnp.int32, (CHUNK, CHUNK), 0)
        s = lax.broadcasted_iota(jnp.int32, (CHUNK, CHUNK), 1)
        dlogf = jnp.dot((t <= s).astype(F32), dg + dg_scr[...], precision=lax.Precision.HIGHEST, preferred_element_type=F32)
        df = dlogf / p.f - dk
        df_ref[...] = (df * (1.0 - lb) * p.sg * (1.0 - p.sg)).astype(BF16)
        dlb_ref[...] += jnp.sum(df * (1.0 - p.sg), axis=0, keepdims=True)
        dq_ref[...] = ((dqg * p.eg + dqd * p.eqd) * p.sigq * (1.0 + qr * (1.0 - p.sigq))).astype(BF16)
        dv_ref[...] = wide(dv).astype(BF16)
        dhg_ref[...] = wide(dhg).astype(BF16)
        dgain_ref[...] += wide(dgain)

    acc = jax.ShapeDtypeStruct((B_loc, 1, HGRN_WIDTH), F32)
    return pl.pallas_call(
        body, name="hgrn_bwd", grid=(B_loc, nc), in_specs=in_specs,
        out_specs=[_window(CHUNK, 4 * HGRN_WIDTH, at(OFF_HQ)), acc_spec, acc_spec],
        out_shape=[jax.ShapeDtypeStruct(dproj.shape, dproj.dtype), acc, acc],
        input_output_aliases={8: 0},
        scratch_shapes=[pltpu.VMEM((nh, HGRN_DIM, HGRN_DIM), F32), pltpu.VMEM((CHUNK, HGRN_WIDTH), F32),
                        pltpu.VMEM((CHUNK, HGRN_WIDTH), F32)],
        compiler_params=_params(("parallel", "arbitrary")),
    )(proj, proj, proj, proj, lb, gain, states, dyh, dproj)


def _adamw(w, g, m, v, name):
    R, C = w.shape
    tr = _pick(R, (128, 64, 32, 16, 8)) if C > 1024 else _pick(R, (512, 256, 128, 64, 32, 16, 8))

    def body(w_ref, g_ref, m_ref, v_ref, d_ref, nm_ref, nv_ref, g_out_ref):
        gv = g_ref[...]
        g_out_ref[...] = gv
        nm = ADAM_B1 * m_ref[...] + (1.0 - ADAM_B1) * gv
        nv = ADAM_B2 * v_ref[...] + (1.0 - ADAM_B2) * (gv * gv)
        m_hat = nm / (1.0 - ADAM_B1 ** ADAM_STEP)
        v_hat = nv / (1.0 - ADAM_B2 ** ADAM_STEP)
        d_ref[...] = -ADAM_LR * (m_hat / (jnp.sqrt(v_hat) + ADAM_EPS) + ADAM_WD * w_ref[...])
        nm_ref[...] = nm
        nv_ref[...] = nv

    blk = pl.BlockSpec((tr, C), lambda i: (i, 0))
    o = jax.ShapeDtypeStruct((R, C), F32)
    return pl.pallas_call(
        body, name=name, grid=(R // tr,), in_specs=[blk] * 4, out_specs=[blk] * 4, out_shape=[o, o, o, o],
        compiler_params=_params(("parallel",)),
    )(w, g, m, v)


ANY = pl.BlockSpec(memory_space=pl.ANY)
VMEM_SPEC = pl.BlockSpec(memory_space=pltpu.VMEM)


def _place():
    x, y, c = lax.axis_index("x"), lax.axis_index("y"), lax.axis_index("c")
    other_chips = [(1 - x, y), (x, 1 - y), (1 - x, 1 - y)]
    return x, y, c, other_chips


def _cast_into_full(w, ax, s_arr, name):
    R, C = w.shape
    tr = _pick(R, (256, 128))
    nr = R // tr

    def body(s_ref, w_ref, o_ref):
        o_ref[...] = w_ref[...].astype(BF16)

    if ax == 1:
        shape, o_map = (R, N_CHIPS * C), lambda i, s: (i, s[0])
    else:
        shape, o_map = (N_CHIPS * R, C), lambda i, s: (s[0] * nr + i, 0)
    return pl.pallas_call(
        body, name=name,
        grid_spec=pltpu.PrefetchScalarGridSpec(
            num_scalar_prefetch=1, grid=(nr,), in_specs=[pl.BlockSpec((tr, C), lambda i, s: (i, 0))],
            out_specs=pl.BlockSpec((tr, C), o_map)),
        out_shape=jax.ShapeDtypeStruct(shape, BF16),
        compiler_params=_params(("parallel",)),
    )(s_arr, w)


class _Gather:
    def __init__(self, fulls, shard_shapes, axes, tag):
        self.shapes, self.axes, self.tag, self.nw = shard_shapes, axes, tag, len(fulls)
        self.fulls, self.sems, self.token = list(fulls), {}, None

    def start(self, peers, after=None):
        nw, np_ = self.nw, len(peers)

        def body(*refs):
            ins, sems = refs[:nw], refs[nw + (after is not None):nw + (after is not None) + 2 * np_]
            for k, j in enumerate(peers):
                for cp in self._peer_copies(ins, sems[2 * k], sems[2 * k + 1], j):
                    cp.start()
            refs[-1][...] = jnp.zeros_like(refs[-1])

        out = pl.pallas_call(
            body, name="gather_start_%s_%s" % (self.tag, "".join(map(str, peers))),
            out_shape=(*[pltpu.SemaphoreType.DMA((nw,))] * (2 * np_),
                       *[pltpu.HBM(f.shape, f.dtype) for f in self.fulls], jax.ShapeDtypeStruct((8, LANES), F32)),
            in_specs=[HBM_SPEC] * nw + ([] if after is None else [ANY]),
            out_specs=(*[SEM_SPEC] * (2 * np_), *[HBM_SPEC] * nw, VMEM_SPEC),
            input_output_aliases={k: 2 * np_ + k for k in range(nw)},
            compiler_params=pltpu.CompilerParams(has_side_effects=DATAFLOW),
        )(*[pltpu.with_memory_space_constraint(f, pltpu.HBM) for f in self.fulls], *(() if after is None else (after,)))
        for k, j in enumerate(peers):
            self.sems[j] = (out[2 * k], out[2 * k + 1])
        self.fulls = list(out[2 * np_:2 * np_ + nw])
        self.token = out[-1]

    def _region(self, ref, i, t, half):
        R, C = self.shapes[i]
        hr = R // 2
        if self.axes[i] == 1:
            return ref.at[pl.ds(half * hr, hr), pl.ds(pl.multiple_of(t * C, LANES), C)]
        return ref.at[pl.ds(t * R + half * hr, hr), :]

    def _peer_copies(self, refs, send_sems, recv_sems, j):
        x, y, c, chips = _place()
        s = 2 * x + y
        return [pltpu.make_async_remote_copy(
            src_ref=self._region(refs[i], i, s, c), dst_ref=self._region(refs[i], i, s, c), send_sem=send_sems.at[i],
            recv_sem=recv_sems.at[i], device_id=(*chips[j], c), device_id_type=MESH) for i in range(self.nw)]

    def wait(self, peers, after):
        nw, np_ = self.nw, len(peers)

        def body(*refs):
            ins, sems = refs[:nw], refs[nw:nw + 2 * np_]
            for k, j in enumerate(peers):
                for cp in self._peer_copies(ins, sems[2 * k], sems[2 * k + 1], j):
                    cp.wait_send()
                    cp.wait_recv()

        sem_args = [s for j in peers for s in self.sems[j]]
        out = pl.pallas_call(
            body, name="gather_wait_%s_%s" % (self.tag, "".join(map(str, peers))),
            out_shape=tuple(pltpu.HBM(f.shape, f.dtype) for f in self.fulls),
            in_specs=[HBM_SPEC] * nw + [SEM_SPEC] * (2 * np_) + [ANY], out_specs=tuple([HBM_SPEC] * nw),
            input_output_aliases={k: k for k in range(nw)},
            compiler_params=pltpu.CompilerParams(has_side_effects=DATAFLOW),
        )(*self.fulls, *sem_args, after)
        self.fulls = list(out)

    def forward(self, peers):
        nw, np_ = self.nw, len(peers)

        def body(*refs):
            ins, outs = refs[:nw], refs[nw:2 * nw]
            send_sems, recv_sems = refs[2 * nw:]
            x, y, c, chips = _place()
            cps = []
            for i in range(nw):
                for k, j in enumerate(peers):
                    t = 2 * chips[j][0] + chips[j][1]
                    cp = pltpu.make_async_remote_copy(
                        src_ref=self._region(ins[i], i, t, c), dst_ref=self._region(outs[i], i, t, c),
                        send_sem=send_sems.at[i * np_ + k], recv_sem=recv_sems.at[i * np_ + k],
                        device_id=(x, y, 1 - c), device_id_type=MESH)
                    cp.start()
                    cps.append(cp)
            for cp in cps:
                cp.wait()

        out = pl.pallas_call(
            body, name="gather_forward_%s_%s" % (self.tag, "".join(map(str, peers))),
            in_specs=[ANY] * nw, out_specs=[ANY] * nw,
            out_shape=[jax.ShapeDtypeStruct(f.shape, f.dtype) for f in self.fulls],
            input_output_aliases={i: i for i in range(nw)},
            scratch_shapes=[pltpu.SemaphoreType.DMA((nw * np_,)), pltpu.SemaphoreType.DMA((nw * np_,))],
        )(*self.fulls)
        self.fulls = list(out)


def _matmul_slab(a, wfull, slab_arr, prev, after, name):
    M, K = a.shape
    N = wfull.shape[1]
    nslab = N // N_CHIPS
    tn = _pick(nslab, (2688, 896, 512, 384, 256, 128))
    tm = _pick(M, (512, 256, 128) if tn > 1024 else (1024, 512, 256, 128))
    per = nslab // tn
    extra = [e for e in (prev, after) if e is not None]

    def body(slab_ref, a_ref, b_ref, *rest):
        rest[len(extra)][...] = jnp.dot(a_ref[...], b_ref[...], preferred_element_type=F32)

    return pl.pallas_call(
        body, name=name,
        grid_spec=pltpu.PrefetchScalarGridSpec(
            num_scalar_prefetch=1, grid=(M // tm, per),
            in_specs=[pl.BlockSpec((tm, K), lambda i, j, sl: (i, 0)),
                      pl.BlockSpec((K, tn), lambda i, j, sl: (0, sl[0] * per + j))] + [ANY] * len(extra),
            out_specs=pl.BlockSpec((tm, tn), lambda i, j, sl: (i, sl[0] * per + j))),
        out_shape=jax.ShapeDtypeStruct((M, N), F32),
        input_output_aliases={} if prev is None else {3: 0},
        compiler_params=_params(("parallel", "arbitrary")),
    )(slab_arr, a, wfull, *extra)


def _exchange_sibling_halves(gs, name):
    nw = len(gs)

    def body(*refs):
        ins, outs = refs[:nw], refs[nw:2 * nw]
        send_sems, recv_sems = refs[2 * nw:]
        x, y, c, _ = _place()
        cps = []
        for i in range(nw):
            cp = pltpu.make_async_remote_copy(src_ref=ins[i].at[:, 1 - c], dst_ref=outs[i], send_sem=send_sems.at[i],
                                              recv_sem=recv_sems.at[i], device_id=(x, y, 1 - c), device_id_type=MESH)
            cp.start()
            cps.append(cp)
        for cp in cps:
            cp.wait()

    return pl.pallas_call(
        body, name=name, in_specs=[ANY] * nw, out_specs=[ANY] * nw,
        out_shape=[jax.ShapeDtypeStruct((g.shape[0],) + g.shape[2:], g.dtype) for g in gs],
        scratch_shapes=[pltpu.SemaphoreType.DMA((nw,)), pltpu.SemaphoreType.DMA((nw,))],
    )(*gs)


HBM_SPEC = pl.BlockSpec(memory_space=pltpu.HBM)
SEM_SPEC = pl.BlockSpec(memory_space=pltpu.SEMAPHORE)
DATAFLOW = pltpu.SideEffectType.DATAFLOW_SIDE_EFFECTING


def _chip_copies(ins, lands, send_sems, recv_sems):
    x, y, c, chips = _place()
    return [pltpu.make_async_remote_copy(
        src_ref=ins[i].at[2 * chip[0] + chip[1]], dst_ref=lands[i].at[j], send_sem=send_sems.at[i * 3 + j],
        recv_sem=recv_sems.at[i * 3 + j], device_id=(*chip, c), device_id_type=MESH)
        for i in range(len(ins)) for j, chip in enumerate(chips)]


def _chips_send_start(ss, name):
    nw = len(ss)
    lands = [pltpu.with_memory_space_constraint(lax.empty((N_CHIPS - 1,) + s.shape[1:], s.dtype), pltpu.HBM) for s in ss]

    def body(*refs):
        ins, land_refs = refs[:nw], refs[nw:2 * nw]
        send_sems, recv_sems = refs[2 * nw], refs[2 * nw + 1]
        token = refs[-1]
        for cp in _chip_copies(ins, land_refs, send_sems, recv_sems):
            cp.start()
        token[...] = jnp.zeros_like(token)

    n = 3 * nw
    out = pl.pallas_call(
        body, name=name,
        out_shape=(pltpu.SemaphoreType.DMA((n,)), pltpu.SemaphoreType.DMA((n,)),
                   *[pltpu.HBM(s.shape, s.dtype) for s in ss], *[pltpu.HBM(l.shape, l.dtype) for l in lands],
                   jax.ShapeDtypeStruct((8, LANES), F32)),
        in_specs=[HBM_SPEC] * (2 * nw), out_specs=(SEM_SPEC, SEM_SPEC, *[HBM_SPEC] * (2 * nw), VMEM_SPEC),
        input_output_aliases={k: 2 + k for k in range(2 * nw)},
        compiler_params=pltpu.CompilerParams(has_side_effects=DATAFLOW),
    )(*[pltpu.with_memory_space_constraint(s, pltpu.HBM) for s in ss], *lands)
    return out[0], out[1], list(out[2:2 + nw]), list(out[2 + nw:2 + 2 * nw]), out[-1]


def _chips_send_wait(send_sems, recv_sems, ss, lands, after, name):
    nw = len(ss)

    def body(*refs):
        ins, land_refs = refs[:nw], refs[nw:2 * nw]
        s_sems, r_sems = refs[2 * nw], refs[2 * nw + 1]
        for cp in _chip_copies(ins, land_refs, s_sems, r_sems):
            cp.wait_send()
            cp.wait_recv()

    out = pl.pallas_call(
        body, name=name,
        out_shape=(*[pltpu.HBM(s.shape, s.dtype) for s in ss], *[pltpu.HBM(l.shape, l.dtype) for l in lands]),
        in_specs=[HBM_SPEC] * (2 * nw) + [SEM_SPEC, SEM_SPEC, ANY], out_specs=tuple([HBM_SPEC] * (2 * nw)),
        input_output_aliases={k: k for k in range(2 * nw)},
        compiler_params=pltpu.CompilerParams(has_side_effects=DATAFLOW),
    )(*ss, *lands, send_sems, recv_sems, after)
    return list(out[nw:])


def _share_and_sum_small(fs, pack, name):
    nw = len(fs)
    rows = pack.shape[0]

    def body(*refs):
        ins, pack_ref = refs[:nw], refs[nw]
        outs, sum_ref = refs[nw + 1:2 * nw + 1], refs[2 * nw + 1]
        all_ref, send_sems, recv_sems = refs[2 * nw + 2:]
        x, y, c, _ = _place()
        me = 4 * x + 2 * y + c
        cps = []
        for i in range(nw):
            cp = pltpu.make_async_remote_copy(src_ref=ins[i].at[c], dst_ref=outs[i].at[c], send_sem=send_sems.at[i],
                                              recv_sem=recv_sems.at[i], device_id=(x, y, 1 - c), device_id_type=MESH)
            cp.start()
            cps.append(cp)
        all_ref[me] = pack_ref[...]
        for k in range(1, N_DEV):
            to = (1 - x if k & 4 else x, 1 - y if k & 2 else y, 1 - c if k & 1 else c)
            cp = pltpu.make_async_remote_copy(
                src_ref=pack_ref, dst_ref=all_ref.at[me], send_sem=send_sems.at[nw + k - 1],
                recv_sem=recv_sems.at[nw + k - 1], device_id=to, device_id_type=MESH)
            cp.start()
            cps.append(cp)
        for cp in cps:
            cp.wait()
        total = all_ref[0]
        for d in range(1, N_DEV):
            total = total + all_ref[d]
        sum_ref[...] = total

    n_sems = nw + N_DEV - 1
    out = pl.pallas_call(
        body, name=name, in_specs=[ANY] * nw + [VMEM_SPEC], out_specs=[ANY] * nw + [VMEM_SPEC],
        out_shape=[jax.ShapeDtypeStruct(f.shape, f.dtype) for f in fs] + [jax.ShapeDtypeStruct(pack.shape, F32)],
        input_output_aliases={i: i for i in range(nw)},
        scratch_shapes=[pltpu.VMEM((N_DEV, rows, LANES), F32), pltpu.SemaphoreType.DMA((n_sems,)),
                        pltpu.SemaphoreType.DMA((n_sems,))],
    )(*fs, pack)
    return list(out[:nw]), out[nw]


def _sum_sibling(g, land, where_arr, name):
    P, Q = g.shape[-2:]
    tp = _pick(P, (256, 128, 64))

    def body(w_ref, g_ref, l_ref, s_ref):
        s_ref[...] = (g_ref[...].astype(F32) + l_ref[...].astype(F32)).astype(BF16)

    if g.ndim == 4:
        g_spec = pl.BlockSpec((None, None, tp, Q), lambda j, r, w: (w[1 + j], w[0], r, 0))
    else:
        g_spec = pl.BlockSpec((None, tp, Q), lambda j, r, w: (w[1 + j], r, 0))
    slab = pl.BlockSpec((None, tp, Q), lambda j, r, w: (w[1 + j], r, 0))
    return pl.pallas_call(
        body, name=name,
        grid_spec=pltpu.PrefetchScalarGridSpec(
            num_scalar_prefetch=1, grid=(N_CHIPS - 1, P // tp), in_specs=[g_spec, slab], out_specs=slab),
        out_shape=jax.ShapeDtypeStruct((N_CHIPS, P, Q), BF16),
        compiler_params=_params(("parallel", "parallel")),
    )(where_arr, g, land)


def _sum_chips(g, land, recv, sc_arr, name):
    P, Q = g.shape[-2:]
    tp = _pick(P, (256, 128, 64))

    def body(sc_ref, g_ref, l_ref, r_ref, f_ref):
        acc = g_ref[...].astype(F32) + l_ref[...].astype(F32)
        for j in range(N_CHIPS - 1):
            acc = acc + r_ref[j].astype(F32)
        f_ref[...] = acc

    if g.ndim == 4:
        g_spec = pl.BlockSpec((None, None, tp, Q), lambda r, sc: (sc[0], sc[1], r, 0))
    else:
        g_spec = pl.BlockSpec((None, tp, Q), lambda r, sc: (sc[0], r, 0))
    return pl.pallas_call(
        body, name=name,
        grid_spec=pltpu.PrefetchScalarGridSpec(
            num_scalar_prefetch=1, grid=(P // tp,),
            in_specs=[g_spec,
                      pl.BlockSpec((None, tp, Q), lambda r, sc: (sc[0], r, 0)),
                      pl.BlockSpec((N_CHIPS - 1, tp, Q), lambda r, sc: (0, r, 0))],
            out_specs=pl.BlockSpec((None, tp, Q), lambda r, sc: (sc[1], r, 0))),
        out_shape=jax.ShapeDtypeStruct((2, P, Q), F32),
        compiler_params=_params(("parallel",)),
    )(sc_arr, g, land, recv)


class _Reduction:
    def __init__(self, names, pieces, lands, flight):
        self.names, self.pieces, self.lands = names, pieces, lands
        self.send_sems, self.recv_sems, self.sums, self.zones, self.token = flight


def _reduce_start(pieces, lands, names, where_arr):
    tag = names[0] if len(names) == 1 else "branches"
    sums = [_sum_sibling(g, l, where_arr, "sum_sibling_" + nm) for g, l, nm in zip(pieces, lands, names)]
    return _Reduction(names, pieces, lands, _chips_send_start(sums, "grads_to_chips_start_" + tag))


def _wgrad_half(a, b, half_arr, name, after=None):
    K, M = a.shape
    nslab = b.shape[1] // N_CHIPS
    tm = M // 2
    tn = _pick(nslab, (896, 512, 384, 256, 128))
    per = nslab // tn
    assert 4 * K * (tm + tn) <= MATMUL_OPERAND_BYTES

    def body(h_ref, a_ref, b_ref, *rest):
        rest[-1][...] = lax.dot_general(a_ref[...], b_ref[...], TN_DIMS, preferred_element_type=F32).astype(BF16)

    return pl.pallas_call(
        body, name=name,
        grid_spec=pltpu.PrefetchScalarGridSpec(
            num_scalar_prefetch=1, grid=(b.shape[1] // tn,),
            in_specs=[pl.BlockSpec((K, tm), lambda j, h: (0, h[0])), pl.BlockSpec((K, tn), lambda j, h: (0, j))]
            + ([] if after is None else [ANY]),
            out_specs=pl.BlockSpec((None, tm, tn), lambda j, h: (j // per, 0, j % per))),
        out_shape=jax.ShapeDtypeStruct((N_CHIPS, tm, nslab), BF16),
        compiler_params=_params(("parallel",)),
    )(*((half_arr, a, b) if after is None else (half_arr, a, b, after)))


def _sibling_send_start(g, name):
    zone = pltpu.with_memory_space_constraint(lax.empty(g.shape, g.dtype), pltpu.HBM)

    def body(g_ref, zone_ref, send_sem, recv_sem, g_thru, zone_thru, token):
        x, y, c, _ = _place()
        pltpu.make_async_remote_copy(src_ref=g_ref, dst_ref=zone_ref, send_sem=send_sem, recv_sem=recv_sem,
                                     device_id=(x, y, 1 - c), device_id_type=MESH).start()
        token[...] = jnp.zeros_like(token)

    return pl.pallas_call(
        body, name=name,
        out_shape=(pltpu.SemaphoreType.DMA(()), pltpu.SemaphoreType.DMA(()), pltpu.HBM(g.shape, g.dtype),
                   pltpu.HBM(g.shape, g.dtype), jax.ShapeDtypeStruct((8, LANES), F32)),
        in_specs=[HBM_SPEC, HBM_SPEC], out_specs=(SEM_SPEC, SEM_SPEC, HBM_SPEC, HBM_SPEC, VMEM_SPEC),
        input_output_aliases={0: 2, 1: 3},
        compiler_params=pltpu.CompilerParams(has_side_effects=DATAFLOW),
    )(pltpu.with_memory_space_constraint(g, pltpu.HBM), zone)


def _sibling_send_wait(send_sem, recv_sem, g, zone, after, name):
    def body(g_ref, zone_ref, s_sem, r_sem, after_ref, g_out, zone_out):
        x, y, c, _ = _place()
        cp = pltpu.make_async_remote_copy(src_ref=g_ref, dst_ref=zone_ref, send_sem=s_sem, recv_sem=r_sem,
                                          device_id=(x, y, 1 - c), device_id_type=MESH)
        cp.wait_send()
        cp.wait_recv()

    return pl.pallas_call(
        body, name=name, out_shape=(pltpu.HBM(g.shape, g.dtype), pltpu.HBM(zone.shape, zone.dtype)),
        in_specs=[HBM_SPEC, HBM_SPEC, SEM_SPEC, SEM_SPEC, ANY], out_specs=(HBM_SPEC, HBM_SPEC),
        input_output_aliases={0: 0, 1: 1},
        compiler_params=pltpu.CompilerParams(has_side_effects=DATAFLOW),
    )(g, zone, send_sem, recv_sem, after)[1]


def _reduce_finish(red, after, sc_arr):
    tag = red.names[0] if len(red.names) == 1 else "branches"
    recvs = _chips_send_wait(red.send_sems, red.recv_sems, red.sums, red.zones, after, "grads_to_chips_wait_" + tag)
    return [_sum_chips(g, l, r, sc_arr, "sum_chips_" + nm) for g, l, r, nm in zip(red.pieces, red.lands, recvs, red.names)]


def _t5_bucket(dist):
    max_exact = REL_BUCKETS // 2
    d = jnp.maximum(dist, 0)
    df = jnp.maximum(d, 1).astype(F32)
    large = max_exact + (jnp.log(df / max_exact) / math.log(REL_MAX_DIST / max_exact)
                         * (REL_BUCKETS - max_exact)).astype(jnp.int32)
    large = jnp.minimum(large, REL_BUCKETS - 1)
    return jnp.where(d < max_exact, d, large)


def _bucket_table():
    qi = jnp.arange(WINDOW)[:, None]
    si = jnp.arange(2 * WINDOW)[None, :]
    return _t5_bucket(qi + WINDOW - si)


TILE_WORDS = 8 * LANES


def _tile_rows(shape):
    return -(-math.prod(shape) // TILE_WORDS) * 8


def _rows_of(a):
    flat = a.reshape(-1).astype(F32)
    n = _tile_rows(a.shape) * LANES
    return jnp.pad(flat, (0, n - flat.shape[0])).reshape(-1, LANES)


def _pack_rows(parts):
    return jnp.concatenate([_rows_of(p) for p in parts], axis=0)


def _unpack_rows(packed, shapes):
    out, at = [], 0
    for shp in shapes:
        n, nr = math.prod(shp), _tile_rows(shp)
        out.append(packed[at:at + nr].reshape(-1)[:n].reshape(shp))
        at += nr
    return out


def kernel(x, norm_pre, w_in, rel_bias, attn_sinks, lb_logits, hgrn_norm, w_branch_attn, w_branch_hgrn, w_out, norm_post, loss_target, m_norm_pre, m_w_in, m_rel_bias, m_attn_sinks, m_lb_logits, m_hgrn_norm, m_w_branch_attn, m_w_branch_hgrn, m_w_out, m_norm_post, v_norm_pre, v_w_in, v_rel_bias, v_attn_sinks, v_lb_logits, v_hgrn_norm, v_w_branch_attn, v_w_branch_hgrn, v_w_out, v_norm_post):
    B_loc, S, D = x.shape
    T = B_loc * S
    x2 = x.reshape(T, D)
    tgt2 = loss_target.reshape(T, D)
    my_x, my_y, my_c = lax.axis_index("x"), lax.axis_index("y"), lax.axis_index("c")

    c_arr = jnp.reshape(my_c, (1,)).astype(jnp.int32)
    s_arr = jnp.reshape(2 * my_x + my_y, (1,)).astype(jnp.int32)
    sc_arr = jnp.concatenate([s_arr, c_arr])
    shard_ws = [w_in[0], w_branch_attn[0], w_branch_hgrn[0], w_out[0]]
    shard_axes = (1, 1, 1, 0)
    names = ["w_in", "w_branch_attn", "w_branch_hgrn", "w_out"]
    placed = [_cast_into_full(w, ax, s_arr, "cast_" + nm) for w, ax, nm in zip(shard_ws, shard_axes, names)]
    peer_slabs = [jnp.reshape(t, (1,)).astype(jnp.int32)
                  for t in (2 * (1 - my_x) + my_y, 2 * my_x + 1 - my_y, 2 * (1 - my_x) + 1 - my_y)]

    buckets = _bucket_table()
    onehot = (buckets.reshape(-1)[:, None] == jnp.arange(REL_BUCKETS)[None, :]).astype(F32)
    bias_tab = jnp.dot(onehot, rel_bias.astype(F32), precision=lax.Precision.HIGHEST).T.reshape(ATTN_HEADS, WINDOW, 2 * WINDOW)
    sinks_b = jnp.broadcast_to(attn_sinks[0].astype(F32)[:, None, None], (ATTN_HEADS, 8, LANES))
    lb_fn = lambda l: jnp.cumsum(jax.nn.softmax(l.astype(F32), axis=0), axis=0)[:1]
    lb, lb_vjp = jax.vjp(lb_fn, lb_logits)
    gain_h = hgrn_norm[0].reshape(1, HGRN_WIDTH)

    h, rstd = _rmsnorm_fwd(x2, norm_pre)
    gather_in = _Gather(placed[:1], [shard_ws[0].shape], shard_axes[:1], "w_in")
    gather_in.start([0, 1])
    proj = _matmul_slab(h, gather_in.fulls[0], s_arr, None, gather_in.token, "in_proj_own")
    gather_in.wait([0, 1], proj)
    gather_in.forward([0, 1])
    gather_in.start([2])
    proj = _matmul_slab(h, gather_in.fulls[0], peer_slabs[0], proj, gather_in.token, "in_proj_peer0")
    proj = _matmul_slab(h, gather_in.fulls[0], peer_slabs[1], proj, None, "in_proj_peer1")
    gather_in.wait([2], proj)
    gather_in.forward([2])
    gather_rest = _Gather(placed[1:], [w.shape for w in shard_ws[1:]], shard_axes[1:], "rest")
    gather_rest.start([0, 1, 2], gather_in.fulls[0])
    proj = _matmul_slab(h, gather_in.fulls[0], peer_slabs[2], proj, gather_rest.token, "in_proj_peer2")
    win_f = gather_in.fulls[0]
    ya, attn_probs = _attn_fwd(proj, bias_tab, sinks_b, B_loc, S)
    yh, states = _hgrn_fwd(proj, lb, gain_h, B_loc, S)
    gather_rest.wait([0, 1, 2], yh)
    gather_rest.forward([0, 1, 2])
    wba_f, wbh_f, wout_f = gather_rest.fulls
    ua = _matmul(ya, wba_f, "nn", BF16, "branch_attn_proj")
    uh, merged = _branch_proj_merge(yh, wbh_f, proj, ua)
    yv = _matmul(merged, wout_f, "nn", BF16, "out_proj")
    dy, dout, loss_p, gnpost_p = _post_loss(yv, x2, tgt2, norm_post)

    g_wout = _matmul(merged, dy, "tn", BF16, "out_proj_wgrad")
    d_ua, d_uh, dproj = _out_dgrad_merge_bwd(dy, wout_f, proj, ua, uh)
    g_wba = _matmul(ya, d_ua, "tn", BF16, "branch_attn_wgrad", slabs=N_CHIPS)
    g_wbh = _matmul(yh, d_uh, "tn", BF16, "branch_hgrn_wgrad", slabs=N_CHIPS)
    where_arr = jnp.concatenate([c_arr] + peer_slabs)
    late_pieces = [g.reshape(N_CHIPS, 2, -1, g.shape[-1]) for g in (g_wba, g_wbh, g_wout)]
    late = _reduce_start(late_pieces, _exchange_sibling_halves(late_pieces, "grads_to_sibling_branches"), names[1:], where_arr)
    d_ya = _matmul(d_ua, wba_f, "nt", BF16, "branch_attn_dgrad", after=late.token)
    d_yh = _matmul(d_uh, wbh_f, "nt", BF16, "branch_hgrn_dgrad", after=late.token)
    dproj, dbias_p, dsinks_p = _attn_bwd(proj, attn_probs, d_ya, dproj, B_loc, S)
    dproj, dlb_p, dgain_p = _hgrn_bwd(proj, states, lb, gain_h, d_yh, dproj, B_loc, S)
    g_give = _wgrad_half(h, dproj, 1 - c_arr, "in_proj_wgrad_sibling_half")
    s_sem, r_sem, g_give, zone, token = _sibling_send_start(g_give, "grads_to_sibling_start_w_in")
    g_keep = _wgrad_half(h, dproj, c_arr, "in_proj_wgrad_own_half", after=token)
    land = _sibling_send_wait(s_sem, r_sem, g_give, zone, g_keep, "grads_to_sibling_wait_w_in")
    last = _reduce_start([g_keep], [land], names[:1], where_arr)
    dh = _matmul(dproj, win_f, "nt", BF16, "in_proj_dgrad", after=last.token)
    grad_x2, gnpre_p = _rmsnorm_bwd(dh, x2, rstd, norm_pre, dout)
    halves = _reduce_finish(last, grad_x2, sc_arr) + _reduce_finish(late, grad_x2, sc_arr)

    grelb_p = jnp.dot(dbias_p.reshape(ATTN_HEADS, -1), onehot, precision=lax.Precision.HIGHEST).T
    gsinks_p = dsinks_p[:, 0, 0]
    dlb_sum = jnp.sum(dlb_p, axis=0).reshape(1, HGRN_WIDTH)
    ghn_p = jnp.sum(dgain_p, axis=0).reshape(HGRN_HEADS, HGRN_DIM)
    small_parts = [gnpre_p, gnpost_p, grelb_p, gsinks_p, dlb_sum, ghn_p, loss_p]
    small_shapes = [p.shape for p in small_parts]
    shared, pack_sum = _share_and_sum_small(halves, _pack_rows(small_parts), "grads_share_sibling")
    big_w = [w_in, w_branch_attn, w_branch_hgrn, w_out]
    big_m = [m_w_in, m_w_branch_attn, m_w_branch_hgrn, m_w_out]
    big_v = [v_w_in, v_w_branch_attn, v_w_branch_hgrn, v_w_out]
    big = {}
    for nm, gs, w, m, v in zip(names, shared, big_w, big_m, big_v):
        shp = w.shape
        g2 = gs.reshape(shp[1], shp[2])
        d, nm_, nv_, g_out = _adamw(w[0], g2, m[0], v[0], "adamw_" + nm)
        big[nm] = tuple(a.reshape(shp) for a in (g_out, d, nm_, nv_))

    gnpre, gnpost, grelb, gsinks, dlb, ghn, loss = _unpack_rows(pack_sum, small_shapes)
    (g_lb_logits,) = lb_vjp(dlb)
    small_names = ["norm_pre", "rel_bias", "attn_sinks", "lb_logits", "hgrn_norm", "norm_post"]
    small_w = [norm_pre, rel_bias, attn_sinks, lb_logits, hgrn_norm, norm_post]
    small_m = [m_norm_pre, m_rel_bias, m_attn_sinks, m_lb_logits, m_hgrn_norm, m_norm_post]
    small_v = [v_norm_pre, v_rel_bias, v_attn_sinks, v_lb_logits, v_hgrn_norm, v_norm_post]
    small_g = [gnpre.reshape(norm_pre.shape), grelb.reshape(rel_bias.shape), gsinks.reshape(attn_sinks.shape),
               g_lb_logits.reshape(lb_logits.shape), ghn.reshape(hgrn_norm.shape), gnpost.reshape(norm_post.shape)]
    shapes = [w.shape for w in small_w]
    d_s, nm_s, nv_s, _ = _adamw(_pack_rows(small_w), _pack_rows(small_g), _pack_rows(small_m), _pack_rows(small_v),
                             "adamw_small")
    small = {}
    for nm, g, d, m_, v_ in zip(small_names, small_g, _unpack_rows(d_s, shapes), _unpack_rows(nm_s, shapes),
                                _unpack_rows(nv_s, shapes)):
        small[nm] = (g, d, m_, v_)

    allw = {**big, **small}
    order = ["norm_pre", "w_in", "rel_bias", "attn_sinks", "lb_logits", "hgrn_norm", "w_branch_attn", "w_branch_hgrn",
             "w_out", "norm_post"]
    outs = [loss.reshape(()), grad_x2.reshape(B_loc, S, D)]
    for k in range(4):
        outs += [allw[nm][k] for nm in order]
    return tuple(outs)
```

```python
import math

import jax
import jax.numpy as jnp
from jax import lax
from jax.experimental import pallas as pl
from jax.experimental.pallas import tpu as pltpu

F32 = jnp.float32
BF16 = jnp.bfloat16
MESH = pl.DeviceIdType.MESH

ATTN_HEADS = 16
ATTN_KV_HEADS = 4
HEAD_DIM = 64
GROUP = ATTN_HEADS // ATTN_KV_HEADS
WINDOW = 128
ATTN_WIDTH = ATTN_HEADS * HEAD_DIM
KV_WIDTH = ATTN_KV_HEADS * HEAD_DIM
HGRN_HEADS = 8
HGRN_DIM = 128
HGRN_WIDTH = HGRN_HEADS * HGRN_DIM
CHUNK = 64
SUB = 16
NSUB = CHUNK // SUB
REL_BUCKETS = 32
REL_MAX_DIST = 128
NORM_EPS = 1e-6
ADAM_LR = 0.001
ADAM_B1 = 0.9
ADAM_B2 = 0.999
ADAM_EPS = 1e-08
ADAM_WD = 0.01
ADAM_STEP = 10
LANES = 128
N_CHIPS = 4
N_DEV = 8
VMEM_LIMIT = 48 * 1024 * 1024
MATMUL_OPERAND_BYTES = 34 * 1024 * 1024
MATMUL_VMEM_BYTES = 44 * 1024 * 1024

OFF_AQ = 0
OFF_AK = OFF_AQ + ATTN_WIDTH
OFF_AV = OFF_AK + KV_WIDTH
OFF_AG = OFF_AV + KV_WIDTH
OFF_HQ = OFF_AG + ATTN_WIDTH
OFF_HF = OFF_HQ + HGRN_WIDTH
OFF_HI = OFF_HF + HGRN_WIDTH
OFF_HG = OFF_HI + HGRN_WIDTH
OFF_GA = OFF_HG + HGRN_WIDTH

NT_DIMS = (((1,), (1,)), ((), ()))
TN_DIMS = (((0,), (0,)), ((), ()))
NN_DIMS = (((1,), (0,)), ((), ()))


def _pick(n, cands):
    for c in cands:
        if n % c == 0:
            return c
    raise ValueError(f"no tile for {n} in {cands}")


def _params(sem):
    return pltpu.CompilerParams(dimension_semantics=sem, vmem_limit_bytes=VMEM_LIMIT)


def _bdot(a, b, dims=NN_DIMS):
    return lax.dot_general(a.astype(BF16), b.astype(BF16), dims, preferred_element_type=F32)


def _matmul(a, b, mode, out_dtype, name, slabs=1, after=None):
    if mode == "nn":
        (M, K), (K2, N) = a.shape, b.shape
    elif mode == "nt":
        (M, K), (N, K2) = a.shape, b.shape
    else:
        (K, M), (K2, N) = a.shape, b.shape
    assert K == K2
    nslab = N // slabs
    tm = _pick(M, (1024, 512, 256, 128))
    out_bytes = jnp.dtype(out_dtype).itemsize
    choices = []
    for tn in (2688, 1024, 896, 512, 384, 256, 128):
        for tk in (4096, 3584, 2048, 1792, 1536, 1024, 512, 256, 128):
            acc = 0 if tk == K else 4 * tm * tn
            if (nslab % tn == 0 and K % tk == 0 and 4 * tk * (tm + tn) <= MATMUL_OPERAND_BYTES
                    and 4 * tk * (tm + tn) + 2 * out_bytes * tm * tn + acc <= MATMUL_VMEM_BYTES):
                choices.append((K // tk > 1, -tn, tn, tk))
                break
    _, _, tn, tk = min(choices)
    nk = K // tk
    per = nslab // tn
    dims = {"nn": NN_DIMS, "nt": NT_DIMS, "tn": TN_DIMS}[mode]

    n_in = 2 if after is None else 3

    def body(*refs):
        a_ref, b_ref, o_ref, acc = refs[0], refs[1], refs[n_in], refs[n_in + 1:]
        part = lax.dot_general(a_ref[...], b_ref[...], dims, preferred_element_type=F32)
        if nk == 1:
            o_ref[...] = part.astype(o_ref.dtype)
            return
        acc_ref, = acc
        k = pl.program_id(2)

        @pl.when(k == 0)
        def _():
            acc_ref[...] = part

        @pl.when((k > 0) & (k < nk - 1))
        def _():
            acc_ref[...] += part

        @pl.when(k == nk - 1)
        def _():
            o_ref[...] = (acc_ref[...] + part).astype(o_ref.dtype)

    if mode == "tn":
        a_spec = pl.BlockSpec((tk, tm), lambda i, j, k: (k, i))
    else:
        a_spec = pl.BlockSpec((tm, tk), lambda i, j, k: (i, k))
    if mode == "nt":
        b_spec = pl.BlockSpec((tn, tk), lambda i, j, k: (j, k))
    else:
        b_spec = pl.BlockSpec((tk, tn), lambda i, j, k: (k, j))
    if slabs == 1:
        o_shape = jax.ShapeDtypeStruct((M, N), out_dtype)
        o_spec = pl.BlockSpec((tm, tn), lambda i, j, k: (i, j))
    else:
        o_shape = jax.ShapeDtypeStruct((slabs, M, nslab), out_dtype)
        o_spec = pl.BlockSpec((None, tm, tn), lambda i, j, k: (j // per, i, j % per))
    return pl.pallas_call(
        body, name=name, grid=(M // tm, N // tn, nk), in_specs=[a_spec, b_spec] + ([] if after is None else [ANY]),
        out_specs=o_spec, out_shape=o_shape,
        scratch_shapes=[pltpu.VMEM((tm, tn), F32)] if nk > 1 else [],
        compiler_params=_params(("parallel", "parallel", "arbitrary")),
    )(*((a, b) if after is None else (a, b, after)))


def _rmsnorm_fwd(x2, gain):
    T, D = x2.shape
    tr = _pick(T, (256, 128))

    def body(x_ref, g_ref, h_ref, r_ref):
        xv = x_ref[...]
        r = lax.rsqrt(jnp.mean(xv * xv, axis=-1, keepdims=True) + NORM_EPS)
        h_ref[...] = (xv * r * g_ref[...]).astype(BF16)
        r_ref[...] = r

    return pl.pallas_call(
        body, name="rmsnorm_pre_fwd", grid=(T // tr,),
        in_specs=[pl.BlockSpec((tr, D), lambda i: (i, 0)), pl.BlockSpec((1, D), lambda i: (0, 0))],
        out_specs=[pl.BlockSpec((tr, D), lambda i: (i, 0)), pl.BlockSpec((tr, 1), lambda i: (i, 0))],
        out_shape=[jax.ShapeDtypeStruct((T, D), BF16), jax.ShapeDtypeStruct((T, 1), F32)],
        compiler_params=_params(("parallel",)),
    )(x2, gain)


def _rmsnorm_bwd(dh, x2, rstd, gain, dout):
    T, D = x2.shape
    tr = _pick(T, (256, 128))

    def body(dh_ref, x_ref, r_ref, g_ref, do_ref, gx_ref, gg_ref):
        @pl.when(pl.program_id(0) == 0)
        def _():
            gg_ref[...] = jnp.zeros_like(gg_ref)

        n = x_ref[...] * r_ref[...]
        dhv = dh_ref[...].astype(F32)
        dn = dhv * g_ref[...]
        gx_ref[...] = do_ref[...] + r_ref[...] * (dn - n * jnp.mean(dn * n, axis=-1, keepdims=True))
        gg_ref[...] += jnp.sum(dhv * n, axis=0, keepdims=True)

    row = pl.BlockSpec((tr, D), lambda i: (i, 0))
    vec = pl.BlockSpec((1, D), lambda i: (0, 0))
    return pl.pallas_call(
        body, name="rmsnorm_pre_bwd", grid=(T // tr,),
        in_specs=[row, row, pl.BlockSpec((tr, 1), lambda i: (i, 0)), vec, row],
        out_specs=[row, vec],
        out_shape=[jax.ShapeDtypeStruct((T, D), F32), jax.ShapeDtypeStruct((1, D), F32)],
        compiler_params=_params(("arbitrary",)),
    )(dh, x2, rstd, gain, dout)


def _post_loss(yv, x2, tgt2, gain):
    T, D = x2.shape
    tr = _pick(T, (256, 128))

    def body(y_ref, x_ref, t_ref, g_ref, dy_ref, do_ref, loss_ref, gg_ref):
        @pl.when(pl.program_id(0) == 0)
        def _():
            gg_ref[...] = jnp.zeros_like(gg_ref)
            loss_ref[...] = jnp.zeros_like(loss_ref)

        yv_ = y_ref[...].astype(F32)
        r = lax.rsqrt(jnp.mean(yv_ * yv_, axis=-1, keepdims=True) + NORM_EPS)
        n = yv_ * r
        e = (x_ref[...] + n * g_ref[...]) - t_ref[...]
        loss_ref[...] += 0.5 * jnp.sum(jnp.mean(e * e, axis=-1, keepdims=True), axis=0, keepdims=True)
        dz = e / D
        do_ref[...] = dz
        gg_ref[...] += jnp.sum(dz * n, axis=0, keepdims=True)
        dn = dz * g_ref[...]
        dy_ref[...] = (r * (dn - n * jnp.mean(dn * n, axis=-1, keepdims=True))).astype(BF16)

    row = pl.BlockSpec((tr, D), lambda i: (i, 0))
    vec = pl.BlockSpec((1, D), lambda i: (0, 0))
    return pl.pallas_call(
        body, name="post_norm_loss", grid=(T // tr,),
        in_specs=[row, row, row, vec],
        out_specs=[row, row, pl.BlockSpec((1, 1), lambda i: (0, 0)), vec],
        out_shape=[jax.ShapeDtypeStruct((T, D), BF16), jax.ShapeDtypeStruct((T, D), F32),
                   jax.ShapeDtypeStruct((1, 1), F32), jax.ShapeDtypeStruct((1, D), F32)],
        compiler_params=_params(("arbitrary",)),
    )(yv, x2, tgt2, gain)


def _window(rows, cols, at):
    return pl.BlockSpec((pl.Element(rows), pl.Element(cols)), at)


def _gate_windows(tm, tn, D):
    return [_window(tm, tn, lambda i, j: (i * tm, pl.multiple_of(OFF_GA + j * tn, LANES))),
            _window(tm, tn, lambda i, j: (i * tm, pl.multiple_of(OFF_GA + D + j * tn, LANES)))]


def _branch_proj_merge(yh, wbh, proj, ua):
    T, K = yh.shape
    D = wbh.shape[1]
    tm, tn = _pick(T, (512, 256, 128)), _pick(D, (1024, 512, 256))

    def body(a_ref, b_ref, ga_ref, gh_ref, ua_ref, uh_ref, m_ref):
        uh = jnp.dot(a_ref[...], b_ref[...], preferred_element_type=F32).astype(BF16)
        uh_ref[...] = uh
        m_ref[...] = (jax.nn.sigmoid(ga_ref[...]) * ua_ref[...].astype(F32)
                      + jax.nn.sigmoid(gh_ref[...]) * uh.astype(F32)).astype(BF16)

    blk = pl.BlockSpec((tm, tn), lambda i, j: (i, j))
    o = jax.ShapeDtypeStruct((T, D), BF16)
    return pl.pallas_call(
        body, name="branch_hgrn_proj_merge", grid=(T // tm, D // tn),
        in_specs=[pl.BlockSpec((tm, K), lambda i, j: (i, 0)), pl.BlockSpec((K, tn), lambda i, j: (0, j))]
        + _gate_windows(tm, tn, D) + [blk],
        out_specs=[blk, blk], out_shape=[o, o],
        compiler_params=_params(("parallel", "arbitrary")),
    )(yh, wbh, proj, proj, ua)


def _out_dgrad_merge_bwd(dy, wout, proj, ua, uh):
    T, K = dy.shape
    D = wout.shape[0]
    tm, tn = _pick(T, (512, 256, 128)), _pick(D, (1024, 512, 256))
    nj = D // tn

    def body(a_ref, b_ref, ga_ref, gh_ref, ua_ref, uh_ref, dua_ref, duh_ref, dproj_ref):
        d = lax.dot_general(a_ref[...], b_ref[...], NT_DIMS, preferred_element_type=F32)
        sa = jax.nn.sigmoid(ga_ref[...])
        sh = jax.nn.sigmoid(gh_ref[...])
        dua_ref[...] = (d * sa).astype(BF16)
        duh_ref[...] = (d * sh).astype(BF16)
        dga = (d * ua_ref[...].astype(F32) * sa * (1.0 - sa)).astype(BF16)
        dgh = (d * uh_ref[...].astype(F32) * sh * (1.0 - sh)).astype(BF16)
        for jj in range(nj):
            @pl.when(pl.program_id(1) == jj)
            def _():
                dproj_ref[:, jj * tn:(jj + 1) * tn] = dga
                dproj_ref[:, D + jj * tn:D + (jj + 1) * tn] = dgh

    blk = pl.BlockSpec((tm, tn), lambda i, j: (i, j))
    o = jax.ShapeDtypeStruct((T, D), BF16)
    return pl.pallas_call(
        body, name="out_proj_dgrad_merge_bwd", grid=(T // tm, nj),
        in_specs=[pl.BlockSpec((tm, K), lambda i, j: (i, 0)), pl.BlockSpec((tn, K), lambda i, j: (j, 0))]
        + _gate_windows(tm, tn, D) + [blk, blk],
        out_specs=[blk, blk, _window(tm, 2 * D, lambda i, j: (i * tm, OFF_GA))],
        out_shape=[o, o, jax.ShapeDtypeStruct((T, proj.shape[1]), BF16)],
        compiler_params=_params(("parallel", "arbitrary")),
    )(dy, wout, proj, proj, ua, uh)


KV_PAIR = 2
PAIR_HEADS = KV_PAIR * GROUP


def _attn_mask(n):
    qi = lax.broadcasted_iota(jnp.int32, (WINDOW, 2 * WINDOW), 0)
    si = lax.broadcasted_iota(jnp.int32, (WINDOW, 2 * WINDOW), 1)
    dist = qi + WINDOW - si
    return (dist >= 0) & (dist < WINDOW) & ((si >= WINDOW) | (n > 0))


def _first_key_column(shape):
    return lax.broadcasted_iota(jnp.int32, shape, 1) == 0


def _attn_group_fwd(mask, q, k, v, ag, bias, sink):
    mask = jnp.concatenate([mask] * GROUP, axis=0)
    s = lax.dot_general(q.astype(BF16), k.astype(BF16), NT_DIMS, preferred_element_type=F32) * (HEAD_DIM ** -0.5)
    s = jnp.where(mask, s + bias, -1e30)
    m = jnp.maximum(jnp.max(s, axis=-1, keepdims=True), sink)
    p = jnp.exp(s - m)
    e_sink = jnp.exp(sink - m)
    den = jnp.sum(p, axis=-1, keepdims=True) + e_sink
    pb = p.astype(BF16)
    o = jnp.dot(pb, v.astype(BF16), preferred_element_type=F32)
    return o * (jax.nn.silu(ag) / den), pb, e_sink.astype(BF16)


def _attn_group_bwd(q, k, v, ag, kept, dout):
    e_sink = kept[:, 0:1].astype(F32)
    pb = jnp.where(_first_key_column(kept.shape), jnp.zeros_like(kept), kept)
    pf = pb.astype(F32)
    rden = 1.0 / (jnp.sum(pf, axis=-1, keepdims=True) + e_sink)
    qb, kb, vb = q.astype(BF16), k.astype(BF16), v.astype(BF16)
    o = jnp.dot(pb, vb, preferred_element_type=F32) * rden
    sg = jax.nn.sigmoid(ag)
    d_o = dout * (ag * sg)
    dag = dout * o * (sg * (1.0 + ag * (1.0 - sg)))
    d_row = jnp.sum(d_o * o, axis=-1, keepdims=True)
    ds = (pf * rden) * (lax.dot_general(d_o.astype(BF16), vb, NT_DIMS, preferred_element_type=F32) - d_row)
    dsb = ds.astype(BF16)
    dq = jnp.dot(dsb, kb, preferred_element_type=F32) * (HEAD_DIM ** -0.5)
    dk = lax.dot_general(dsb, qb, TN_DIMS, preferred_element_type=F32) * (HEAD_DIM ** -0.5)
    dv = lax.dot_general(pb, (d_o * rden).astype(BF16), TN_DIMS, preferred_element_type=F32)
    return dq, dk, dv, dag, ds, -(e_sink * rden * d_row)


def _attn_operands(q_ref, kc_ref, kp_ref, vc_ref, vp_ref, ag_ref, bias_ref, sink_ref, j):
    lo, hi = j * HEAD_DIM, (j + 1) * HEAD_DIM
    k = jnp.concatenate([kp_ref[:, lo:hi], kc_ref[:, lo:hi]], axis=0)
    v = jnp.concatenate([vp_ref[:, lo:hi], vc_ref[:, lo:hi]], axis=0)
    heads = [j * GROUP + g for g in range(GROUP)]
    q = jnp.concatenate([q_ref[:, h * HEAD_DIM:(h + 1) * HEAD_DIM] for h in heads], axis=0)
    ag = jnp.concatenate([ag_ref[:, h * HEAD_DIM:(h + 1) * HEAD_DIM] for h in heads], axis=0)
    bias = jnp.concatenate([bias_ref[h] for h in heads], axis=0)
    sink = jnp.concatenate([jnp.broadcast_to(sink_ref[h, 0:1, 0:1], (WINDOW, 1)) for h in heads], axis=0)
    return q, k, v, ag, bias, sink


def _attn_fwd(proj, bias_tab, sinks_b, B_loc, S):
    T = B_loc * S
    nb = S // WINDOW

    def at(off, back=0):
        return lambda b, n: ((b * nb + jnp.maximum(n - back, 0)) * WINDOW, off)

    in_specs = [_window(WINDOW, ATTN_WIDTH, at(OFF_AQ)), _window(WINDOW, KV_WIDTH, at(OFF_AK)),
                _window(WINDOW, KV_WIDTH, at(OFF_AK, 1)), _window(WINDOW, KV_WIDTH, at(OFF_AV)),
                _window(WINDOW, KV_WIDTH, at(OFF_AV, 1)), _window(WINDOW, ATTN_WIDTH, at(OFF_AG)),
                pl.BlockSpec((ATTN_HEADS, WINDOW, 2 * WINDOW), lambda b, n: (0, 0, 0)),
                pl.BlockSpec((ATTN_HEADS, 8, LANES), lambda b, n: (0, 0, 0))]

    def body(q_ref, kc_ref, kp_ref, vc_ref, vp_ref, ag_ref, bias_ref, sink_ref, ya_ref, p_ref):
        mask = _attn_mask(pl.program_id(1))
        for j in range(ATTN_KV_HEADS):
            out, probs, e_sink = _attn_group_fwd(
                mask, *_attn_operands(q_ref, kc_ref, kp_ref, vc_ref, vp_ref, ag_ref, bias_ref, sink_ref, j))
            for g in range(GROUP):
                h = j * GROUP + g
                blk = slice(g * WINDOW, (g + 1) * WINDOW)
                ya_ref[:, h * HEAD_DIM:(h + 1) * HEAD_DIM] = out[blk].astype(BF16)
                p_ref[:, h * 2 * WINDOW:(h + 1) * 2 * WINDOW] = probs[blk]
                p_ref[:, h * 2 * WINDOW:h * 2 * WINDOW + 1] = e_sink[blk]

    return pl.pallas_call(
        body, name="attn_fwd", grid=(B_loc, nb), in_specs=in_specs,
        out_specs=[pl.BlockSpec((WINDOW, ATTN_WIDTH), lambda b, n: (b * nb + n, 0)),
                   pl.BlockSpec((WINDOW, ATTN_HEADS * 2 * WINDOW), lambda b, n: (b * nb + n, 0))],
        out_shape=[jax.ShapeDtypeStruct((T, ATTN_WIDTH), BF16),
                   jax.ShapeDtypeStruct((T, ATTN_HEADS * 2 * WINDOW), BF16)],
        compiler_params=_params(("parallel", "parallel")),
    )(proj, proj, proj, proj, proj, proj, bias_tab, sinks_b)


def _attn_bwd(proj, probs, dya, dproj, B_loc, S):
    nb = S // WINDOW
    n_pairs = ATTN_KV_HEADS // KV_PAIR
    qw, kw = PAIR_HEADS * HEAD_DIM, KV_PAIR * HEAD_DIM

    def at(off, width, back=0):
        return lambda b, i, p: ((b * nb + jnp.maximum(nb - 1 - i - back, 0)) * WINDOW,
                                pl.multiple_of(off + p * width, LANES))

    in_specs = [_window(WINDOW, qw, at(OFF_AQ, qw)), _window(WINDOW, kw, at(OFF_AK, kw)),
                _window(WINDOW, kw, at(OFF_AK, kw, 1)), _window(WINDOW, kw, at(OFF_AV, kw)),
                _window(WINDOW, kw, at(OFF_AV, kw, 1)), _window(WINDOW, qw, at(OFF_AG, qw)),
                pl.BlockSpec((WINDOW, PAIR_HEADS * 2 * WINDOW), lambda b, i, p: (b * nb + nb - 1 - i, p)),
                pl.BlockSpec((WINDOW, qw), lambda b, i, p: (b * nb + nb - 1 - i, p)), ANY]

    def body(q_ref, kc_ref, kp_ref, vc_ref, vp_ref, ag_ref, p_ref, dya_ref, dproj_in,
             dproj_ref, dbias_ref, dsink_ref, dkc_ref, dvc_ref):
        b, i, p = pl.program_id(0), pl.program_id(1), pl.program_id(2)

        @pl.when((b == 0) & (i == 0) & (p == 0))
        def _():
            dbias_ref[...] = jnp.zeros_like(dbias_ref)
            dsink_ref[...] = jnp.zeros_like(dsink_ref)

        @pl.when(i == 0)
        def _():
            dkc_ref[p] = jnp.zeros((WINDOW, kw), F32)
            dvc_ref[p] = jnp.zeros((WINDOW, kw), F32)

        dk_carry, dv_carry = dkc_ref[p], dvc_ref[p]
        dqs, dags, dbiases, dsinks, dks, dvs = [], [], [], [], [], []
        for j in range(KV_PAIR):
            heads = [j * GROUP + g for g in range(GROUP)]
            lo, hi = j * HEAD_DIM, (j + 1) * HEAD_DIM
            stack = lambda parts: jnp.concatenate(parts, axis=0)
            k = stack([kp_ref[:, lo:hi], kc_ref[:, lo:hi]])
            v = stack([vp_ref[:, lo:hi], vc_ref[:, lo:hi]])
            q = stack([q_ref[:, h * HEAD_DIM:(h + 1) * HEAD_DIM] for h in heads])
            ag = stack([ag_ref[:, h * HEAD_DIM:(h + 1) * HEAD_DIM] for h in heads])
            dout = stack([dya_ref[:, h * HEAD_DIM:(h + 1) * HEAD_DIM].astype(F32) for h in heads])
            kept = stack([p_ref[:, h * 2 * WINDOW:(h + 1) * 2 * WINDOW] for h in heads])
            dq, dk, dv, dag, dbias, dsink = _attn_group_bwd(q, k, v, ag, kept, dout)
            dks.append((dk[WINDOW:] + dk_carry[:, lo:hi]).astype(BF16))
            dvs.append((dv[WINDOW:] + dv_carry[:, lo:hi]).astype(BF16))
            dkc_ref[p, :, lo:hi] = dk[:WINDOW]
            dvc_ref[p, :, lo:hi] = dv[:WINDOW]
            for g in range(GROUP):
                blk = slice(g * WINDOW, (g + 1) * WINDOW)
                dqs.append(dq[blk].astype(BF16))
                dags.append(dag[blk].astype(BF16))
                dbiases.append(dbias[blk])
                dsinks.append(jnp.broadcast_to(jnp.sum(dsink[blk], axis=0, keepdims=True), (8, LANES)))

        for pair in range(n_pairs):
            @pl.when(p == pair)
            def _():
                for j in range(KV_PAIR):
                    col = (pair * KV_PAIR + j) * HEAD_DIM
                    dproj_ref[:, OFF_AK + col:OFF_AK + col + HEAD_DIM] = dks[j]
                    dproj_ref[:, OFF_AV + col:OFF_AV + col + HEAD_DIM] = dvs[j]
                for hh in range(PAIR_HEADS):
                    h = pair * PAIR_HEADS + hh
                    dproj_ref[:, OFF_AQ + h * HEAD_DIM:OFF_AQ + (h + 1) * HEAD_DIM] = dqs[hh]
                    dproj_ref[:, OFF_AG + h * HEAD_DIM:OFF_AG + (h + 1) * HEAD_DIM] = dags[hh]
                    dbias_ref[h] += dbiases[hh]
                    dsink_ref[h] += dsinks[hh]

    return pl.pallas_call(
        body, name="attn_bwd", grid=(B_loc, nb, n_pairs), in_specs=in_specs,
        out_specs=[_window(WINDOW, OFF_HQ, lambda b, i, p: ((b * nb + nb - 1 - i) * WINDOW, 0)),
                   pl.BlockSpec((ATTN_HEADS, WINDOW, 2 * WINDOW), lambda b, i, p: (0, 0, 0)),
                   pl.BlockSpec((ATTN_HEADS, 8, LANES), lambda b, i, p: (0, 0, 0))],
        out_shape=[jax.ShapeDtypeStruct(dproj.shape, dproj.dtype),
                   jax.ShapeDtypeStruct((ATTN_HEADS, WINDOW, 2 * WINDOW), F32),
                   jax.ShapeDtypeStruct((ATTN_HEADS, 8, LANES), F32)],
        input_output_aliases={8: 0},
        scratch_shapes=[pltpu.VMEM((n_pairs, WINDOW, kw), F32), pltpu.VMEM((n_pairs, WINDOW, kw), F32)],
        compiler_params=_params(("arbitrary", "arbitrary", "arbitrary")),
    )(proj, proj, proj, proj, proj, proj, probs, dya, dproj)


class _HgrnPre:
    def __init__(self, fr, qr, lb, g_scr):
        t = lax.broadcasted_iota(jnp.int32, (CHUNK, CHUNK), 0)
        s = lax.broadcasted_iota(jnp.int32, (CHUNK, CHUNK), 1)
        self.sg = jax.nn.sigmoid(fr)
        self.f = lb + (1.0 - lb) * self.sg
        g = jnp.dot((t >= s).astype(F32), jnp.log(self.f), precision=lax.Precision.HIGHEST, preferred_element_type=F32)
        g_scr[...] = g
        self.g = g
        self.row = lax.broadcasted_iota(jnp.int32, g.shape, 0)
        self.refs = [jnp.zeros((1, g.shape[1]), F32)] + [g_scr[pl.ds(i * SUB - 1, 1), :] for i in range(1, NSUB)]
        self.gend = g_scr[pl.ds(CHUNK - 1, 1), :]
        refrow = jnp.zeros_like(g)
        for i in range(1, NSUB):
            refrow = jnp.where(self.row >= i * SUB, self.refs[i], refrow)
        self.sigq = jax.nn.sigmoid(qr)
        self.qs = qr * self.sigq
        self.k = 1.0 - self.f
        self.eg = jnp.exp(g)
        self.eqd = jnp.exp(g - refrow)
        self.ekd = [jnp.exp(jnp.where(self.row < (i + 1) * SUB, self.refs[i] - g, 0.0)) for i in range(NSUB)]
        self.ekdec = jnp.exp(self.gend - g)
        self.qg = self.qs * self.eg
        self.qd = self.qs * self.eqd
        self.kd = [self.k * e for e in self.ekd]
        self.kdec = self.k * self.ekdec
        self.egend = jnp.exp(self.gend)


def _hgrn_pair_mask():
    t = lax.broadcasted_iota(jnp.int32, (CHUNK, NSUB * CHUNK), 0)
    col = lax.broadcasted_iota(jnp.int32, (CHUNK, NSUB * CHUNK), 1)
    return ((t // SUB) == (col // CHUNK)) & ((col % CHUNK) <= t)


def _hgrn_head_out(p, lanes, state_t, v, mask):
    qg, qd = p.qg[:, lanes], p.qd[:, lanes]
    kall = jnp.concatenate([kd[:, lanes] for kd in p.kd], axis=0)
    vst = jnp.concatenate([v] * NSUB, axis=0)
    am = jnp.where(mask, _bdot(qd, kall, NT_DIMS), 0.0)
    o = _bdot(qg, state_t, NT_DIMS) + _bdot(am, vst)
    return o, (qg, qd, kall, am, vst)


def _hgrn_fwd(proj, lb, gain, B_loc, S):
    T = B_loc * S
    nc = S // CHUNK
    nh = HGRN_HEADS

    def at(off):
        return lambda b, n: ((b * nc + n) * CHUNK, off)

    vec = pl.BlockSpec((1, HGRN_WIDTH), lambda b, n: (0, 0))

    def body(q_ref, f_ref, v_ref, hg_ref, lb_ref, gain_ref, yh_ref, st_ref, state_scr, g_scr):
        @pl.when(pl.program_id(1) == 0)
        def _():
            state_scr[...] = jnp.zeros_like(state_scr)

        p = _HgrnPre(f_ref[...], q_ref[...], lb_ref[...], g_scr)
        v = v_ref[...]
        gate = gain_ref[...] * jax.nn.silu(hg_ref[...])
        mask = _hgrn_pair_mask()
        for hd in range(nh):
            lanes = slice(hd * HGRN_DIM, (hd + 1) * HGRN_DIM)
            st = state_scr[hd]
            st_ref[hd] = st
            o, _ = _hgrn_head_out(p, lanes, st, v[:, lanes], mask)
            rs = lax.rsqrt(jnp.mean(o * o, axis=-1, keepdims=True) + NORM_EPS)
            yh_ref[:, lanes] = (o * rs * gate[:, lanes]).astype(BF16)
            state_scr[hd] = st * p.egend[:, lanes] + _bdot(v[:, lanes], p.kdec[:, lanes], TN_DIMS)

    return pl.pallas_call(
        body, name="hgrn_fwd", grid=(B_loc, nc),
        in_specs=[_window(CHUNK, HGRN_WIDTH, at(OFF_HQ)), _window(CHUNK, HGRN_WIDTH, at(OFF_HF)),
                  _window(CHUNK, HGRN_WIDTH, at(OFF_HI)), _window(CHUNK, HGRN_WIDTH, at(OFF_HG)), vec, vec],
        out_specs=[pl.BlockSpec((CHUNK, HGRN_WIDTH), lambda b, n: (b * nc + n, 0)),
                   pl.BlockSpec((None, None, nh, HGRN_DIM, HGRN_DIM), lambda b, n: (b, n, 0, 0, 0))],
        out_shape=[jax.ShapeDtypeStruct((T, HGRN_WIDTH), BF16),
                   jax.ShapeDtypeStruct((B_loc, nc, nh, HGRN_DIM, HGRN_DIM), F32)],
        scratch_shapes=[pltpu.VMEM((nh, HGRN_DIM, HGRN_DIM), F32), pltpu.VMEM((CHUNK, HGRN_WIDTH), F32)],
        compiler_params=_params(("parallel", "arbitrary")),
    )(proj, proj, proj, proj, lb, gain)


def _hgrn_bwd(proj, states, lb, gain, dyh, dproj, B_loc, S):
    nc = S // CHUNK
    nh = HGRN_HEADS

    def at(off):
        return lambda b, i: ((b * nc + nc - 1 - i) * CHUNK, off)

    vec = pl.BlockSpec((1, HGRN_WIDTH), lambda b, i: (0, 0))
    in_specs = [_window(CHUNK, HGRN_WIDTH, at(OFF_HQ)), _window(CHUNK, HGRN_WIDTH, at(OFF_HF)),
                _window(CHUNK, HGRN_WIDTH, at(OFF_HI)), _window(CHUNK, HGRN_WIDTH, at(OFF_HG)), vec, vec,
                pl.BlockSpec((None, None, nh, HGRN_DIM, HGRN_DIM), lambda b, i: (b, nc - 1 - i, 0, 0, 0)),
                pl.BlockSpec((CHUNK, HGRN_WIDTH), lambda b, i: (b * nc + nc - 1 - i, 0)), ANY]
    acc_spec = pl.BlockSpec((None, 1, HGRN_WIDTH), lambda b, i: (b, 0, 0))

    def body(q_ref, f_ref, v_ref, hg_ref, lb_ref, gain_ref, st_ref, dyh_ref, dproj_in,
             dproj_ref, dlb_ref, dgain_ref, dstate_scr, g_scr, dg_scr):
        dq_ref, df_ref, dv_ref, dhg_ref = [dproj_ref.at[:, pl.ds(k * HGRN_WIDTH, HGRN_WIDTH)] for k in range(4)]

        @pl.when(pl.program_id(1) == 0)
        def _():
            dstate_scr[...] = jnp.zeros_like(dstate_scr)
            dlb_ref[...] = jnp.zeros_like(dlb_ref)
            dgain_ref[...] = jnp.zeros_like(dgain_ref)

        qr, lb, gain, hg, v = q_ref[...], lb_ref[...], gain_ref[...], hg_ref[...], v_ref[...]
        p = _HgrnPre(f_ref[...], qr, lb, g_scr)
        sgh = jax.nn.sigmoid(hg)
        sil = hg * sgh
        dy = dyh_ref[...].astype(F32)
        mask = _hgrn_pair_mask()
        dqg, dqd, dkdec, dv, dhg, dgend, dgain = [], [], [], [], [], [], []
        dkd = [[] for _ in range(NSUB)]
        heads = [slice(hd * HGRN_DIM, (hd + 1) * HGRN_DIM) for hd in range(nh)]
        sts = [st_ref[hd] for hd in range(nh)]
        dnews = [dstate_scr[hd] for hd in range(nh)]
        fwd = [_hgrn_head_out(p, lanes, st, v[:, lanes], mask) for lanes, st in zip(heads, sts)]
        for lanes, st, dnew in zip(heads, sts, dnews):
            dkdec_h = _bdot(v[:, lanes], dnew)
            dkdec.append(dkdec_h)
            dgend.append(jnp.sum(dkdec_h * p.kdec[:, lanes], axis=0, keepdims=True)
                         + jnp.sum(dnew * st, axis=0, keepdims=True) * p.egend[:, lanes])
        dos = []
        gate_grad = sgh * (1.0 + hg * (1.0 - sgh))
        for lanes, (o, _) in zip(heads, fwd):
            rs = lax.rsqrt(jnp.mean(o * o, axis=-1, keepdims=True) + NORM_EPS)
            n = o * rs
            dyn = dy[:, lanes] * n
            dgain.append(jnp.sum(dyn * sil[:, lanes], axis=0, keepdims=True))
            dhg.append(dyn * gain[:, lanes] * gate_grad[:, lanes])
            dn = dy[:, lanes] * gain[:, lanes] * sil[:, lanes]
            dos.append(rs * (dn - n * jnp.mean(dn * n, axis=-1, keepdims=True)))
        drs = []
        for hd, (lanes, st, dnew, do, (_, (qg, qd, kall, am, vst))) in enumerate(zip(heads, sts, dnews, dos, fwd)):
            dqg.append(_bdot(do, st))
            dstate_scr[hd] = _bdot(do, qg, TN_DIMS) + dnew * p.egend[:, lanes]
            drs.append(jnp.where(mask, _bdot(do, vst, NT_DIMS), 0.0))
            dvst = _bdot(am, do, TN_DIMS)
            dv.append(sum(dvst[i * CHUNK:(i + 1) * CHUNK] for i in range(NSUB)) + _bdot(p.kdec[:, lanes], dnew, NT_DIMS))
        for dr, (_, (qg, qd, kall, am, vst)) in zip(drs, fwd):
            dqd.append(_bdot(dr, kall))
            dkall = _bdot(dr, qd, TN_DIMS)
            for i in range(NSUB):
                dkd[i].append(dkall[i * CHUNK:(i + 1) * CHUNK])

        wide = lambda parts: jnp.concatenate(parts, axis=1)
        dqg, dqd, dkdec = wide(dqg), wide(dqd), wide(dkdec)
        t2 = dqd * p.qd
        dg = dqg * p.qg + t2 - dkdec * p.kdec
        dk = dkdec * p.ekdec
        dg_scr[...] = jnp.zeros_like(dg_scr)
        for i in range(NSUB):
            dkd_i = wide(dkd[i])
            tk = jnp.where(p.row < (i + 1) * SUB, dkd_i * p.kd[i], 0.0)
            dg = dg - tk
            dk = dk + dkd_i * p.ekd[i]
            if i >= 1:
                in_blk = (p.row >= i * SUB) & (p.row < (i + 1) * SUB)
                dg_scr[pl.ds(i * SUB - 1, 1), :] = (jnp.sum(tk, axis=0, keepdims=True)
                                                    - jnp.sum(jnp.where(in_blk, t2, 0.0), axis=0, keepdims=True))
        dg_scr[pl.ds(CHUNK - 1, 1), :] = wide(dgend)
        t = lax.broadcasted_iota(jnp.int32, (CHUNK, CHUNK), 0)
        s = lax.broadcasted_iota(jnp.int32, (CHUNK, CHUNK), 1)
        dlogf = jnp.dot((t <= s).astype(F32), dg + dg_scr[...], precision=lax.Precision.HIGHEST, preferred_element_type=F32)
        df = dlogf / p.f - dk
        df_ref[...] = (df * (1.0 - lb) * p.sg * (1.0 - p.sg)).astype(BF16)
        dlb_ref[...] += jnp.sum(df * (1.0 - p.sg), axis=0, keepdims=True)
        dq_ref[...] = ((dqg * p.eg + dqd * p.eqd) * p.sigq * (1.0 + qr * (1.0 - p.sigq))).astype(BF16)
        dv_ref[...] = wide(dv).astype(BF16)
        dhg_ref[...] = wide(dhg).astype(BF16)
        dgain_ref[...] += wide(dgain)

    acc = jax.ShapeDtypeStruct((B_loc, 1, HGRN_WIDTH), F32)
    return pl.pallas_call(
        body, name="hgrn_bwd", grid=(B_loc, nc), in_specs=in_specs,
        out_specs=[_window(CHUNK, 4 * HGRN_WIDTH, at(OFF_HQ)), acc_spec, acc_spec],
        out_shape=[jax.ShapeDtypeStruct(dproj.shape, dproj.dtype), acc, acc],
        input_output_aliases={8: 0},
        scratch_shapes=[pltpu.VMEM((nh, HGRN_DIM, HGRN_DIM), F32), pltpu.VMEM((CHUNK, HGRN_WIDTH), F32),
                        pltpu.VMEM((CHUNK, HGRN_WIDTH), F32)],
        compiler_params=_params(("parallel", "arbitrary")),
    )(proj, proj, proj, proj, lb, gain, states, dyh, dproj)


def _adamw(w, g, m, v, name):
    R, C = w.shape
    tr = _pick(R, (128, 64, 32, 16, 8)) if C > 1024 else _pick(R, (512, 256, 128, 64, 32, 16, 8))

    def body(w_ref, g_ref, m_ref, v_ref, d_ref, nm_ref, nv_ref, g_out_ref):
        gv = g_ref[...]
        g_out_ref[...] = gv
        nm = ADAM_B1 * m_ref[...] + (1.0 - ADAM_B1) * gv
        nv = ADAM_B2 * v_ref[...] + (1.0 - ADAM_B2) * (gv * gv)
        m_hat = nm / (1.0 - ADAM_B1 ** ADAM_STEP)
        v_hat = nv / (1.0 - ADAM_B2 ** ADAM_STEP)
        d_ref[...] = -ADAM_LR * (m_hat / (jnp.sqrt(v_hat) + ADAM_EPS) + ADAM_WD * w_ref[...])
        nm_ref[...] = nm
        nv_ref[...] = nv

    blk = pl.BlockSpec((tr, C), lambda i: (i, 0))
    o = jax.ShapeDtypeStruct((R, C), F32)
    return pl.pallas_call(
        body, name=name, grid=(R // tr,), in_specs=[blk] * 4, out_specs=[blk] * 4, out_shape=[o, o, o, o],
        compiler_params=_params(("parallel",)),
    )(w, g, m, v)


ANY = pl.BlockSpec(memory_space=pl.ANY)
VMEM_SPEC = pl.BlockSpec(memory_space=pltpu.VMEM)


def _place():
    x, y, c = lax.axis_index("x"), lax.axis_index("y"), lax.axis_index("c")
    other_chips = [(1 - x, y), (x, 1 - y), (1 - x, 1 - y)]
    return x, y, c, other_chips


def _cast_into_full(w, ax, s_arr, name):
    R, C = w.shape
    tr = _pick(R, (256, 128))
    nr = R // tr

    def body(s_ref, w_ref, o_ref):
        o_ref[...] = w_ref[...].astype(BF16)

    if ax == 1:
        shape, o_map = (R, N_CHIPS * C), lambda i, s: (i, s[0])
    else:
        shape, o_map = (N_CHIPS * R, C), lambda i, s: (s[0] * nr + i, 0)
    return pl.pallas_call(
        body, name=name,
        grid_spec=pltpu.PrefetchScalarGridSpec(
            num_scalar_prefetch=1, grid=(nr,), in_specs=[pl.BlockSpec((tr, C), lambda i, s: (i, 0))],
            out_specs=pl.BlockSpec((tr, C), o_map)),
        out_shape=jax.ShapeDtypeStruct(shape, BF16),
        compiler_params=_params(("parallel",)),
    )(s_arr, w)


class _Gather:
    def __init__(self, fulls, shard_shapes, axes, tag):
        self.shapes, self.axes, self.tag, self.nw = shard_shapes, axes, tag, len(fulls)
        self.fulls, self.sems, self.token = list(fulls), {}, None

    def start(self, peers, after=None):
        nw, np_ = self.nw, len(peers)

        def body(*refs):
            ins, sems = refs[:nw], refs[nw + (after is not None):nw + (after is not None) + 2 * np_]
            for k, j in enumerate(peers):
                for cp in self._peer_copies(ins, sems[2 * k], sems[2 * k + 1], j):
                    cp.start()
            refs[-1][...] = jnp.zeros_like(refs[-1])

        out = pl.pallas_call(
            body, name="gather_start_%s_%s" % (self.tag, "".join(map(str, peers))),
            out_shape=(*[pltpu.SemaphoreType.DMA((nw,))] * (2 * np_),
                       *[pltpu.HBM(f.shape, f.dtype) for f in self.fulls], jax.ShapeDtypeStruct((8, LANES), F32)),
            in_specs=[HBM_SPEC] * nw + ([] if after is None else [ANY]),
            out_specs=(*[SEM_SPEC] * (2 * np_), *[HBM_SPEC] * nw, VMEM_SPEC),
            input_output_aliases={k: 2 * np_ + k for k in range(nw)},
            compiler_params=pltpu.CompilerParams(has_side_effects=DATAFLOW),
        )(*[pltpu.with_memory_space_constraint(f, pltpu.HBM) for f in self.fulls], *(() if after is None else (after,)))
        for k, j in enumerate(peers):
            self.sems[j] = (out[2 * k], out[2 * k + 1])
        self.fulls = list(out[2 * np_:2 * np_ + nw])
        self.token = out[-1]

    def _region(self, ref, i, t, half):
        R, C = self.shapes[i]
        hr = R // 2
        if self.axes[i] == 1:
            return ref.at[pl.ds(half * hr, hr), pl.ds(pl.multiple_of(t * C, LANES), C)]
        return ref.at[pl.ds(t * R + half * hr, hr), :]

    def _peer_copies(self, refs, send_sems, recv_sems, j):
        x, y, c, chips = _place()
        s = 2 * x + y
        return [pltpu.make_async_remote_copy(
            src_ref=self._region(refs[i], i, s, c), dst_ref=self._region(refs[i], i, s, c), send_sem=send_sems.at[i],
            recv_sem=recv_sems.at[i], device_id=(*chips[j], c), device_id_type=MESH) for i in range(self.nw)]

    def wait(self, peers, after):
        nw, np_ = self.nw, len(peers)

        def body(*refs):
            ins, sems = refs[:nw], refs[nw:nw + 2 * np_]
            for k, j in enumerate(peers):
                for cp in self._peer_copies(ins, sems[2 * k], sems[2 * k + 1], j):
                    cp.wait_send()
                    cp.wait_recv()

        sem_args = [s for j in peers for s in self.sems[j]]
        out = pl.pallas_call(
            body, name="gather_wait_%s_%s" % (self.tag, "".join(map(str, peers))),
            out_shape=tuple(pltpu.HBM(f.shape, f.dtype) for f in self.fulls),
            in_specs=[HBM_SPEC] * nw + [SEM_SPEC] * (2 * np_) + [ANY], out_specs=tuple([HBM_SPEC] * nw),
            input_output_aliases={k: k for k in range(nw)},
            compiler_params=pltpu.CompilerParams(has_side_effects=DATAFLOW),
        )(*self.fulls, *sem_args, after)
        self.fulls = list(out)

    def forward(self, peers):
        nw, np_ = self.nw, len(peers)

        def body(*refs):
            ins, outs = refs[:nw], refs[nw:2 * nw]
            send_sems, recv_sems = refs[2 * nw:]
            x, y, c, chips = _place()
            cps = []
            for i in range(nw):
                for k, j in enumerate(peers):
                    t = 2 * chips[j][0] + chips[j][1]
                    cp = pltpu.make_async_remote_copy(
                        src_ref=self._region(ins[i], i, t, c), dst_ref=self._region(outs[i], i, t, c),
                        send_sem=send_sems.at[i * np_ + k], recv_sem=recv_sems.at[i * np_ + k],
                        device_id=(x, y, 1 - c), device_id_type=MESH)
                    cp.start()
                    cps.append(cp)
            for cp in cps:
                cp.wait()

        out = pl.pallas_call(
            body, name="gather_forward_%s_%s" % (self.tag, "".join(map(str, peers))),
            in_specs=[ANY] * nw, out_specs=[ANY] * nw,
            out_shape=[jax.ShapeDtypeStruct(f.shape, f.dtype) for f in self.fulls],
            input_output_aliases={i: i for i in range(nw)},
            scratch_shapes=[pltpu.SemaphoreType.DMA((nw * np_,)), pltpu.SemaphoreType.DMA((nw * np_,))],
        )(*self.fulls)
        self.fulls = list(out)


def _matmul_slab(a, wfull, slab_arr, prev, after, name):
    M, K = a.shape
    N = wfull.shape[1]
    nslab = N // N_CHIPS
    tn = _pick(nslab, (2688, 896, 512, 384, 256, 128))
    tm = _pick(M, (512, 256, 128) if tn > 1024 else (1024, 512, 256, 128))
    per = nslab // tn
    extra = [e for e in (prev, after) if e is not None]

    def body(slab_ref, a_ref, b_ref, *rest):
        rest[len(extra)][...] = jnp.dot(a_ref[...], b_ref[...], preferred_element_type=F32)

    return pl.pallas_call(
        body, name=name,
        grid_spec=pltpu.PrefetchScalarGridSpec(
            num_scalar_prefetch=1, grid=(M // tm, per),
            in_specs=[pl.BlockSpec((tm, K), lambda i, j, sl: (i, 0)),
                      pl.BlockSpec((K, tn), lambda i, j, sl: (0, sl[0] * per + j))] + [ANY] * len(extra),
            out_specs=pl.BlockSpec((tm, tn), lambda i, j, sl: (i, sl[0] * per + j))),
        out_shape=jax.ShapeDtypeStruct((M, N), F32),
        input_output_aliases={} if prev is None else {3: 0},
        compiler_params=_params(("parallel", "arbitrary")),
    )(slab_arr, a, wfull, *extra)


def _exchange_sibling_halves(gs, name):
    nw = len(gs)

    def body(*refs):
        ins, outs = refs[:nw], refs[nw:2 * nw]
        send_sems, recv_sems = refs[2 * nw:]
        x, y, c, _ = _place()
        cps = []
        for i in range(nw):
            cp = pltpu.make_async_remote_copy(src_ref=ins[i].at[:, 1 - c], dst_ref=outs[i], send_sem=send_sems.at[i],
                                              recv_sem=recv_sems.at[i], device_id=(x, y, 1 - c), device_id_type=MESH)
            cp.start()
            cps.append(cp)
        for cp in cps:
            cp.wait()

    return pl.pallas_call(
        body, name=name, in_specs=[ANY] * nw, out_specs=[ANY] * nw,
        out_shape=[jax.ShapeDtypeStruct((g.shape[0],) + g.shape[2:], g.dtype) for g in gs],
        scratch_shapes=[pltpu.SemaphoreType.DMA((nw,)), pltpu.SemaphoreType.DMA((nw,))],
    )(*gs)


HBM_SPEC = pl.BlockSpec(memory_space=pltpu.HBM)
SEM_SPEC = pl.BlockSpec(memory_space=pltpu.SEMAPHORE)
DATAFLOW = pltpu.SideEffectType.DATAFLOW_SIDE_EFFECTING


def _chip_copies(ins, lands, send_sems, recv_sems):
    x, y, c, chips = _place()
    return [pltpu.make_async_remote_copy(
        src_ref=ins[i].at[2 * chip[0] + chip[1]], dst_ref=lands[i].at[j], send_sem=send_sems.at[i * 3 + j],
        recv_sem=recv_sems.at[i * 3 + j], device_id=(*chip, c), device_id_type=MESH)
        for i in range(len(ins)) for j, chip in enumerate(chips)]


def _chips_send_start(ss, name):
    nw = len(ss)
    lands = [pltpu.with_memory_space_constraint(lax.empty((N_CHIPS - 1,) + s.shape[1:], s.dtype), pltpu.HBM) for s in ss]

    def body(*refs):
        ins, land_refs = refs[:nw], refs[nw:2 * nw]
        send_sems, recv_sems = refs[2 * nw], refs[2 * nw + 1]
        token = refs[-1]
        for cp in _chip_copies(ins, land_refs, send_sems, recv_sems):
            cp.start()
        token[...] = jnp.zeros_like(token)

    n = 3 * nw
    out = pl.pallas_call(
        body, name=name,
        out_shape=(pltpu.SemaphoreType.DMA((n,)), pltpu.SemaphoreType.DMA((n,)),
                   *[pltpu.HBM(s.shape, s.dtype) for s in ss], *[pltpu.HBM(l.shape, l.dtype) for l in lands],
                   jax.ShapeDtypeStruct((8, LANES), F32)),
        in_specs=[HBM_SPEC] * (2 * nw), out_specs=(SEM_SPEC, SEM_SPEC, *[HBM_SPEC] * (2 * nw), VMEM_SPEC),
        input_output_aliases={k: 2 + k for k in range(2 * nw)},
        compiler_params=pltpu.CompilerParams(has_side_effects=DATAFLOW),
    )(*[pltpu.with_memory_space_constraint(s, pltpu.HBM) for s in ss], *lands)
    return out[0], out[1], list(out[2:2 + nw]), list(out[2 + nw:2 + 2 * nw]), out[-1]


def _chips_send_wait(send_sems, recv_sems, ss, lands, after, name):
    nw = len(ss)

    def body(*refs):
        ins, land_refs = refs[:nw], refs[nw:2 * nw]
        s_sems, r_sems = refs[2 * nw], refs[2 * nw + 1]
        for cp in _chip_copies(ins, land_refs, s_sems, r_sems):
            cp.wait_send()
            cp.wait_recv()

    out = pl.pallas_call(
        body, name=name,
        out_shape=(*[pltpu.HBM(s.shape, s.dtype) for s in ss], *[pltpu.HBM(l.shape, l.dtype) for l in lands]),
        in_specs=[HBM_SPEC] * (2 * nw) + [SEM_SPEC, SEM_SPEC, ANY], out_specs=tuple([HBM_SPEC] * (2 * nw)),
        input_output_aliases={k: k for k in range(2 * nw)},
        compiler_params=pltpu.CompilerParams(has_side_effects=DATAFLOW),
    )(*ss, *lands, send_sems, recv_sems, after)
    return list(out[nw:])


def _half_share_start(f, name):
    def body(f_ref, send_sem, recv_sem, f_thru, token):
        x, y, c, _ = _place()
        pltpu.make_async_remote_copy(src_ref=f_ref.at[c], dst_ref=f_ref.at[c], send_sem=send_sem, recv_sem=recv_sem,
                                     device_id=(x, y, 1 - c), device_id_type=MESH).start()
        token[...] = jnp.zeros_like(token)

    return pl.pallas_call(
        body, name=name,
        out_shape=(pltpu.SemaphoreType.DMA(()), pltpu.SemaphoreType.DMA(()), pltpu.HBM(f.shape, f.dtype),
                   jax.ShapeDtypeStruct((8, LANES), F32)),
        in_specs=[HBM_SPEC], out_specs=(SEM_SPEC, SEM_SPEC, HBM_SPEC, VMEM_SPEC), input_output_aliases={0: 2},
        compiler_params=pltpu.CompilerParams(has_side_effects=DATAFLOW),
    )(pltpu.with_memory_space_constraint(f, pltpu.HBM))


def _half_share_wait(send_sem, recv_sem, f, after, name):
    def body(f_ref, s_sem, r_sem, after_ref, f_out):
        x, y, c, _ = _place()
        cp = pltpu.make_async_remote_copy(src_ref=f_ref.at[c], dst_ref=f_ref.at[c], send_sem=s_sem, recv_sem=r_sem,
                                          device_id=(x, y, 1 - c), device_id_type=MESH)
        cp.wait_send()
        cp.wait_recv()

    return pl.pallas_call(
        body, name=name, out_shape=pltpu.HBM(f.shape, f.dtype),
        in_specs=[HBM_SPEC, SEM_SPEC, SEM_SPEC, ANY], out_specs=HBM_SPEC, input_output_aliases={0: 0},
        compiler_params=pltpu.CompilerParams(has_side_effects=DATAFLOW),
    )(f, send_sem, recv_sem, after)


def _share_and_sum_small(fs, pack, name, after=None):
    nw = len(fs)
    rows = pack.shape[0]
    n_in = nw + 1 + (after is not None)

    def body(*refs):
        ins, pack_ref = refs[:nw], refs[nw]
        outs, sum_ref = refs[n_in:n_in + nw], refs[n_in + nw]
        all_ref, send_sems, recv_sems = refs[n_in + nw + 1:]
        x, y, c, _ = _place()
        me = 4 * x + 2 * y + c
        cps = []
        for i in range(nw):
            cp = pltpu.make_async_remote_copy(src_ref=ins[i].at[c], dst_ref=outs[i].at[c], send_sem=send_sems.at[i],
                                              recv_sem=recv_sems.at[i], device_id=(x, y, 1 - c), device_id_type=MESH)
            cp.start()
            cps.append(cp)
        all_ref[me] = pack_ref[...]
        for k in range(1, N_DEV):
            to = (1 - x if k & 4 else x, 1 - y if k & 2 else y, 1 - c if k & 1 else c)
            cp = pltpu.make_async_remote_copy(
                src_ref=pack_ref, dst_ref=all_ref.at[me], send_sem=send_sems.at[nw + k - 1],
                recv_sem=recv_sems.at[nw + k - 1], device_id=to, device_id_type=MESH)
            cp.start()
            cps.append(cp)
        for cp in cps:
            cp.wait()
        total = all_ref[0]
        for d in range(1, N_DEV):
            total = total + all_ref[d]
        sum_ref[...] = total

    n_sems = nw + N_DEV - 1
    out = pl.pallas_call(
        body, name=name, in_specs=[ANY] * nw + [VMEM_SPEC] + ([] if after is None else [ANY]),
        out_specs=[ANY] * nw + [VMEM_SPEC],
        out_shape=[jax.ShapeDtypeStruct(f.shape, f.dtype) for f in fs] + [jax.ShapeDtypeStruct(pack.shape, F32)],
        input_output_aliases={i: i for i in range(nw)},
        scratch_shapes=[pltpu.VMEM((N_DEV, rows, LANES), F32), pltpu.SemaphoreType.DMA((n_sems,)),
                        pltpu.SemaphoreType.DMA((n_sems,))],
    )(*fs, pack, *(() if after is None else (after,)))
    return list(out[:nw]), out[nw]


def _sum_sibling(g, land, where_arr, name):
    P, Q = g.shape[-2:]
    tp = _pick(P, (256, 128, 64))

    def body(w_ref, g_ref, l_ref, s_ref):
        s_ref[...] = (g_ref[...].astype(F32) + l_ref[...].astype(F32)).astype(BF16)

    if g.ndim == 4:
        g_spec = pl.BlockSpec((None, None, tp, Q), lambda j, r, w: (w[1 + j], w[0], r, 0))
    else:
        g_spec = pl.BlockSpec((None, tp, Q), lambda j, r, w: (w[1 + j], r, 0))
    slab = pl.BlockSpec((None, tp, Q), lambda j, r, w: (w[1 + j], r, 0))
    return pl.pallas_call(
        body, name=name,
        grid_spec=pltpu.PrefetchScalarGridSpec(
            num_scalar_prefetch=1, grid=(N_CHIPS - 1, P // tp), in_specs=[g_spec, slab], out_specs=slab),
        out_shape=jax.ShapeDtypeStruct((N_CHIPS, P, Q), BF16),
        compiler_params=_params(("parallel", "parallel")),
    )(where_arr, g, land)


def _sum_chips(g, land, recv, sc_arr, name):
    P, Q = g.shape[-2:]
    tp = _pick(P, (256, 128, 64))

    def body(sc_ref, g_ref, l_ref, r_ref, f_ref):
        acc = g_ref[...].astype(F32) + l_ref[...].astype(F32)
        for j in range(N_CHIPS - 1):
            acc = acc + r_ref[j].astype(F32)
        f_ref[...] = acc

    if g.ndim == 4:
        g_spec = pl.BlockSpec((None, None, tp, Q), lambda r, sc: (sc[0], sc[1], r, 0))
    else:
        g_spec = pl.BlockSpec((None, tp, Q), lambda r, sc: (sc[0], r, 0))
    return pl.pallas_call(
        body, name=name,
        grid_spec=pltpu.PrefetchScalarGridSpec(
            num_scalar_prefetch=1, grid=(P // tp,),
            in_specs=[g_spec,
                      pl.BlockSpec((None, tp, Q), lambda r, sc: (sc[0], r, 0)),
                      pl.BlockSpec((N_CHIPS - 1, tp, Q), lambda r, sc: (0, r, 0))],
            out_specs=pl.BlockSpec((None, tp, Q), lambda r, sc: (sc[1], r, 0))),
        out_shape=jax.ShapeDtypeStruct((2, P, Q), F32),
        compiler_params=_params(("parallel",)),
    )(sc_arr, g, land, recv)


class _Reduction:
    def __init__(self, names, pieces, lands, flight):
        self.names, self.pieces, self.lands = names, pieces, lands
        self.send_sems, self.recv_sems, self.sums, self.zones, self.token = flight


def _reduce_start(pieces, lands, names, where_arr):
    tag = names[0] if len(names) == 1 else "branches"
    sums = [_sum_sibling(g, l, where_arr, "sum_sibling_" + nm) for g, l, nm in zip(pieces, lands, names)]
    return _Reduction(names, pieces, lands, _chips_send_start(sums, "grads_to_chips_start_" + tag))


def _wgrad_half(a, b, half_arr, name, after=None):
    K, M = a.shape
    nslab = b.shape[1] // N_CHIPS
    tm = M // 2
    tn = _pick(nslab, (896, 512, 384, 256, 128))
    per = nslab // tn
    assert 4 * K * (tm + tn) <= MATMUL_OPERAND_BYTES

    def body(h_ref, a_ref, b_ref, *rest):
        rest[-1][...] = lax.dot_general(a_ref[...], b_ref[...], TN_DIMS, preferred_element_type=F32).astype(BF16)

    return pl.pallas_call(
        body, name=name,
        grid_spec=pltpu.PrefetchScalarGridSpec(
            num_scalar_prefetch=1, grid=(b.shape[1] // tn,),
            in_specs=[pl.BlockSpec((K, tm), lambda j, h: (0, h[0])), pl.BlockSpec((K, tn), lambda j, h: (0, j))]
            + ([] if after is None else [ANY]),
            out_specs=pl.BlockSpec((None, tm, tn), lambda j, h: (j // per, 0, j % per))),
        out_shape=jax.ShapeDtypeStruct((N_CHIPS, tm, nslab), BF16),
        compiler_params=_params(("parallel",)),
    )(*((half_arr, a, b) if after is None else (half_arr, a, b, after)))


def _sibling_send_start(g, name):
    zone = pltpu.with_memory_space_constraint(lax.empty(g.shape, g.dtype), pltpu.HBM)

    def body(g_ref, zone_ref, send_sem, recv_sem, g_thru, zone_thru, token):
        x, y, c, _ = _place()
        pltpu.make_async_remote_copy(src_ref=g_ref, dst_ref=zone_ref, send_sem=send_sem, recv_sem=recv_sem,
                                     device_id=(x, y, 1 - c), device_id_type=MESH).start()
        token[...] = jnp.zeros_like(token)

    return pl.pallas_call(
        body, name=name,
        out_shape=(pltpu.SemaphoreType.DMA(()), pltpu.SemaphoreType.DMA(()), pltpu.HBM(g.shape, g.dtype),
                   pltpu.HBM(g.shape, g.dtype), jax.ShapeDtypeStruct((8, LANES), F32)),
        in_specs=[HBM_SPEC, HBM_SPEC], out_specs=(SEM_SPEC, SEM_SPEC, HBM_SPEC, HBM_SPEC, VMEM_SPEC),
        input_output_aliases={0: 2, 1: 3},
        compiler_params=pltpu.CompilerParams(has_side_effects=DATAFLOW),
    )(pltpu.with_memory_space_constraint(g, pltpu.HBM), zone)


def _sibling_send_wait(send_sem, recv_sem, g, zone, after, name):
    def body(g_ref, zone_ref, s_sem, r_sem, after_ref, g_out, zone_out):
        x, y, c, _ = _place()
        cp = pltpu.make_async_remote_copy(src_ref=g_ref, dst_ref=zone_ref, send_sem=s_sem, recv_sem=r_sem,
                                          device_id=(x, y, 1 - c), device_id_type=MESH)
        cp.wait_send()
        cp.wait_recv()

    return pl.pallas_call(
        body, name=name, out_shape=(pltpu.HBM(g.shape, g.dtype), pltpu.HBM(zone.shape, zone.dtype)),
        in_specs=[HBM_SPEC, HBM_SPEC, SEM_SPEC, SEM_SPEC, ANY], out_specs=(HBM_SPEC, HBM_SPEC),
        input_output_aliases={0: 0, 1: 1},
        compiler_params=pltpu.CompilerParams(has_side_effects=DATAFLOW),
    )(g, zone, send_sem, recv_sem, after)[1]


def _reduce_finish(red, after, sc_arr):
    tag = red.names[0] if len(red.names) == 1 else "branches"
    recvs = _chips_send_wait(red.send_sems, red.recv_sems, red.sums, red.zones, after, "grads_to_chips_wait_" + tag)
    return [_sum_chips(g, l, r, sc_arr, "sum_chips_" + nm) for g, l, r, nm in zip(red.pieces, red.lands, recvs, red.names)]


def _t5_bucket(dist):
    max_exact = REL_BUCKETS // 2
    d = jnp.maximum(dist, 0)
    df = jnp.maximum(d, 1).astype(F32)
    large = max_exact + (jnp.log(df / max_exact) / math.log(REL_MAX_DIST / max_exact)
                         * (REL_BUCKETS - max_exact)).astype(jnp.int32)
    large = jnp.minimum(large, REL_BUCKETS - 1)
    return jnp.where(d < max_exact, d, large)


def _bucket_table():
    qi = jnp.arange(WINDOW)[:, None]
    si = jnp.arange(2 * WINDOW)[None, :]
    return _t5_bucket(qi + WINDOW - si)


TILE_WORDS = 8 * LANES


def _tile_rows(shape):
    return -(-math.prod(shape) // TILE_WORDS) * 8


def _rows_of(a):
    flat = a.reshape(-1).astype(F32)
    n = _tile_rows(a.shape) * LANES
    return jnp.pad(flat, (0, n - flat.shape[0])).reshape(-1, LANES)


def _pack_rows(parts):
    return jnp.concatenate([_rows_of(p) for p in parts], axis=0)


def _unpack_rows(packed, shapes):
    out, at = [], 0
    for shp in shapes:
        n, nr = math.prod(shp), _tile_rows(shp)
        out.append(packed[at:at + nr].reshape(-1)[:n].reshape(shp))
        at += nr
    return out


def kernel(x, norm_pre, w_in, rel_bias, attn_sinks, lb_logits, hgrn_norm, w_branch_attn, w_branch_hgrn, w_out, norm_post, loss_target, m_norm_pre, m_w_in, m_rel_bias, m_attn_sinks, m_lb_logits, m_hgrn_norm, m_w_branch_attn, m_w_branch_hgrn, m_w_out, m_norm_post, v_norm_pre, v_w_in, v_rel_bias, v_attn_sinks, v_lb_logits, v_hgrn_norm, v_w_branch_attn, v_w_branch_hgrn, v_w_out, v_norm_post):
    B_loc, S, D = x.shape
    T = B_loc * S
    x2 = x.reshape(T, D)
    tgt2 = loss_target.reshape(T, D)
    my_x, my_y, my_c = lax.axis_index("x"), lax.axis_index("y"), lax.axis_index("c")

    c_arr = jnp.reshape(my_c, (1,)).astype(jnp.int32)
    s_arr = jnp.reshape(2 * my_x + my_y, (1,)).astype(jnp.int32)
    sc_arr = jnp.concatenate([s_arr, c_arr])
    shard_ws = [w_in[0], w_branch_attn[0], w_branch_hgrn[0], w_out[0]]
    shard_axes = (1, 1, 1, 0)
    names = ["w_in", "w_branch_attn", "w_branch_hgrn", "w_out"]
    placed = [_cast_into_full(w, ax, s_arr, "cast_" + nm) for w, ax, nm in zip(shard_ws, shard_axes, names)]
    peer_slabs = [jnp.reshape(t, (1,)).astype(jnp.int32)
                  for t in (2 * (1 - my_x) + my_y, 2 * my_x + 1 - my_y, 2 * (1 - my_x) + 1 - my_y)]

    buckets = _bucket_table()
    onehot = (buckets.reshape(-1)[:, None] == jnp.arange(REL_BUCKETS)[None, :]).astype(F32)
    bias_tab = jnp.dot(onehot, rel_bias.astype(F32), precision=lax.Precision.HIGHEST).T.reshape(ATTN_HEADS, WINDOW, 2 * WINDOW)
    sinks_b = jnp.broadcast_to(attn_sinks[0].astype(F32)[:, None, None], (ATTN_HEADS, 8, LANES))
    lb_fn = lambda l: jnp.cumsum(jax.nn.softmax(l.astype(F32), axis=0), axis=0)[:1]
    lb, lb_vjp = jax.vjp(lb_fn, lb_logits)
    gain_h = hgrn_norm[0].reshape(1, HGRN_WIDTH)

    h, rstd = _rmsnorm_fwd(x2, norm_pre)
    gather_in = _Gather(placed[:1], [shard_ws[0].shape], shard_axes[:1], "w_in")
    gather_in.start([0, 1])
    proj = _matmul_slab(h, gather_in.fulls[0], s_arr, None, gather_in.token, "in_proj_own")
    gather_in.wait([0, 1], proj)
    gather_in.forward([0, 1])
    gather_in.start([2])
    proj = _matmul_slab(h, gather_in.fulls[0], peer_slabs[0], proj, gather_in.token, "in_proj_peer0")
    proj = _matmul_slab(h, gather_in.fulls[0], peer_slabs[1], proj, None, "in_proj_peer1")
    gather_in.wait([2], proj)
    gather_in.forward([2])
    gather_rest = _Gather(placed[1:], [w.shape for w in shard_ws[1:]], shard_axes[1:], "rest")
    gather_rest.start([0, 1, 2], gather_in.fulls[0])
    proj = _matmul_slab(h, gather_in.fulls[0], peer_slabs[2], proj, gather_rest.token, "in_proj_peer2")
    win_f = gather_in.fulls[0]
    ya, attn_probs = _attn_fwd(proj, bias_tab, sinks_b, B_loc, S)
    yh, states = _hgrn_fwd(proj, lb, gain_h, B_loc, S)
    gather_rest.wait([0, 1, 2], yh)
    gather_rest.forward([0, 1, 2])
    wba_f, wbh_f, wout_f = gather_rest.fulls
    ua = _matmul(ya, wba_f, "nn", BF16, "branch_attn_proj")
    uh, merged = _branch_proj_merge(yh, wbh_f, proj, ua)
    yv = _matmul(merged, wout_f, "nn", BF16, "out_proj")
    dy, dout, loss_p, gnpost_p = _post_loss(yv, x2, tgt2, norm_post)

    g_wout = _matmul(merged, dy, "tn", BF16, "out_proj_wgrad")
    d_ua, d_uh, dproj = _out_dgrad_merge_bwd(dy, wout_f, proj, ua, uh)
    g_wba = _matmul(ya, d_ua, "tn", BF16, "branch_attn_wgrad", slabs=N_CHIPS)
    g_wbh = _matmul(yh, d_uh, "tn", BF16, "branch_hgrn_wgrad", slabs=N_CHIPS)
    where_arr = jnp.concatenate([c_arr] + peer_slabs)
    late_pieces = [g.reshape(N_CHIPS, 2, -1, g.shape[-1]) for g in (g_wba, g_wbh, g_wout)]
    late = _reduce_start(late_pieces, _exchange_sibling_halves(late_pieces, "grads_to_sibling_branches"), names[1:], where_arr)
    d_ya = _matmul(d_ua, wba_f, "nt", BF16, "branch_attn_dgrad", after=late.token)
    d_yh = _matmul(d_uh, wbh_f, "nt", BF16, "branch_hgrn_dgrad", after=late.token)
    dproj, dbias_p, dsinks_p = _attn_bwd(proj, attn_probs, d_ya, dproj, B_loc, S)
    dproj, dlb_p, dgain_p = _hgrn_bwd(proj, states, lb, gain_h, d_yh, dproj, B_loc, S)
    g_give = _wgrad_half(h, dproj, 1 - c_arr, "in_proj_wgrad_sibling_half")
    s_sem, r_sem, g_give, zone, token = _sibling_send_start(g_give, "grads_to_sibling_start_w_in")
    g_keep = _wgrad_half(h, dproj, c_arr, "in_proj_wgrad_own_half", after=token)
    land = _sibling_send_wait(s_sem, r_sem, g_give, zone, g_keep, "grads_to_sibling_wait_w_in")
    last = _reduce_start([g_keep], [land], names[:1], where_arr)
    dh = _matmul(dproj, win_f, "nt", BF16, "in_proj_dgrad", after=last.token)
    grad_x2, gnpre_p = _rmsnorm_bwd(dh, x2, rstd, norm_pre, dout)
    halves = _reduce_finish(last, grad_x2, sc_arr) + _reduce_finish(late, grad_x2, sc_arr)

    grelb_p = jnp.dot(dbias_p.reshape(ATTN_HEADS, -1), onehot, precision=lax.Precision.HIGHEST).T
    gsinks_p = dsinks_p[:, 0, 0]
    dlb_sum = jnp.sum(dlb_p, axis=0).reshape(1, HGRN_WIDTH)
    ghn_p = jnp.sum(dgain_p, axis=0).reshape(HGRN_HEADS, HGRN_DIM)
    small_parts = [gnpre_p, gnpost_p, grelb_p, gsinks_p, dlb_sum, ghn_p, loss_p]
    small_shapes = [p.shape for p in small_parts]
    s_sem, r_sem, win_half, share_token = _half_share_start(halves[0], "grads_share_start_w_in")
    shared_rest, pack_sum = _share_and_sum_small(halves[1:], _pack_rows(small_parts), "grads_share_sibling", share_token)
    big_w = [w_in, w_branch_attn, w_branch_hgrn, w_out]
    big_m = [m_w_in, m_w_branch_attn, m_w_branch_hgrn, m_w_out]
    big_v = [v_w_in, v_w_branch_attn, v_w_branch_hgrn, v_w_out]
    big = {}

    def update(nm, gs, w, m, v):
        shp = w.shape
        d, nm_, nv_, g_out = _adamw(w[0], gs.reshape(shp[1], shp[2]), m[0], v[0], "adamw_" + nm)
        big[nm] = tuple(a.reshape(shp) for a in (g_out, d, nm_, nv_))

    for nm, gs, w, m, v in zip(names[1:], shared_rest, big_w[1:], big_m[1:], big_v[1:]):
        update(nm, gs, w, m, v)

    gnpre, gnpost, grelb, gsinks, dlb, ghn, loss = _unpack_rows(pack_sum, small_shapes)
    (g_lb_logits,) = lb_vjp(dlb)
    small_names = ["norm_pre", "rel_bias", "attn_sinks", "lb_logits", "hgrn_norm", "norm_post"]
    small_w = [norm_pre, rel_bias, attn_sinks, lb_logits, hgrn_norm, norm_post]
    small_m = [m_norm_pre, m_rel_bias, m_attn_sinks, m_lb_logits, m_hgrn_norm, m_norm_post]
    small_v = [v_norm_pre, v_rel_bias, v_attn_sinks, v_lb_logits, v_hgrn_norm, v_norm_post]
    small_g = [gnpre.reshape(norm_pre.shape), grelb.reshape(rel_bias.shape), gsinks.reshape(attn_sinks.shape),
               g_lb_logits.reshape(lb_logits.shape), ghn.reshape(hgrn_norm.shape), gnpost.reshape(norm_post.shape)]
    shapes = [w.shape for w in small_w]
    d_s, nm_s, nv_s, _ = _adamw(_pack_rows(small_w), _pack_rows(small_g), _pack_rows(small_m), _pack_rows(small_v),
                             "adamw_small")
    small = {}
    for nm, g, d, m_, v_ in zip(small_names, small_g, _unpack_rows(d_s, shapes), _unpack_rows(nm_s, shapes),
                                _unpack_rows(nv_s, shapes)):
        small[nm] = (g, d, m_, v_)
    update(names[0], _half_share_wait(s_sem, r_sem, win_half, d_s, "grads_share_wait_w_in"), w_in, m_w_in, v_w_in)

    allw = {**big, **small}
    order = ["norm_pre", "w_in", "rel_bias", "attn_sinks", "lb_logits", "hgrn_norm", "w_branch_attn", "w_branch_hgrn",
             "w_out", "norm_post"]
    outs = [loss.reshape(()), grad_x2.reshape(B_loc, S, D)]
    for k in range(4):
        outs += [allw[nm][k] for nm in order]
    return tuple(outs)
```

```python
import math

import jax
import jax.numpy as jnp
from jax import lax
from jax.experimental import pallas as pl
from jax.experimental.pallas import tpu as pltpu

F32 = jnp.float32
BF16 = jnp.bfloat16
MESH = pl.DeviceIdType.MESH

ATTN_HEADS = 16
ATTN_KV_HEADS = 4
HEAD_DIM = 64
GROUP = ATTN_HEADS // ATTN_KV_HEADS
WINDOW = 128
ATTN_WIDTH = ATTN_HEADS * HEAD_DIM
KV_WIDTH = ATTN_KV_HEADS * HEAD_DIM
HGRN_HEADS = 8
HGRN_DIM = 128
HGRN_WIDTH = HGRN_HEADS * HGRN_DIM
CHUNK = 64
SUB = 16
NSUB = CHUNK // SUB
REL_BUCKETS = 32
REL_MAX_DIST = 128
NORM_EPS = 1e-6
ADAM_LR = 0.001
ADAM_B1 = 0.9
ADAM_B2 = 0.999
ADAM_EPS = 1e-08
ADAM_WD = 0.01
ADAM_STEP = 10
LANES = 128
N_CHIPS = 4
N_DEV = 8
VMEM_LIMIT = 48 * 1024 * 1024
MATMUL_OPERAND_BYTES = 34 * 1024 * 1024
MATMUL_VMEM_BYTES = 44 * 1024 * 1024

OFF_AQ = 0
OFF_AK = OFF_AQ + ATTN_WIDTH
OFF_AV = OFF_AK + KV_WIDTH
OFF_AG = OFF_AV + KV_WIDTH
OFF_HQ = OFF_AG + ATTN_WIDTH
OFF_HF = OFF_HQ + HGRN_WIDTH
OFF_HI = OFF_HF + HGRN_WIDTH
OFF_HG = OFF_HI + HGRN_WIDTH
OFF_GA = OFF_HG + HGRN_WIDTH

NT_DIMS = (((1,), (1,)), ((), ()))
TN_DIMS = (((0,), (0,)), ((), ()))
NN_DIMS = (((1,), (0,)), ((), ()))


def _pick(n, cands):
    for c in cands:
        if n % c == 0:
            return c
    raise ValueError(f"no tile for {n} in {cands}")


def _params(sem):
    return pltpu.CompilerParams(dimension_semantics=sem, vmem_limit_bytes=VMEM_LIMIT)


def _bdot(a, b, dims=NN_DIMS):
    return lax.dot_general(a.astype(BF16), b.astype(BF16), dims, preferred_element_type=F32)


def _matmul(a, b, mode, out_dtype, name, slabs=1, after=None):
    if mode == "nn":
        (M, K), (K2, N) = a.shape, b.shape
    elif mode == "nt":
        (M, K), (N, K2) = a.shape, b.shape
    else:
        (K, M), (K2, N) = a.shape, b.shape
    assert K == K2
    nslab = N // slabs
    out_bytes = jnp.dtype(out_dtype).itemsize
    choices = []
    for tm in (1024, 512, 256, 128):
        for tn in (2688, 1024, 896, 512, 384, 256, 128):
            for tk in (5376, 4096, 3584, 2048, 1792, 1536, 1024, 512, 256, 128):
                acc = 0 if tk == K else 4 * tm * tn
                if (M % tm == 0 and nslab % tn == 0 and K % tk == 0 and 4 * tk * (tm + tn) <= MATMUL_OPERAND_BYTES
                        and 4 * tk * (tm + tn) + 2 * out_bytes * tm * tn + acc <= MATMUL_VMEM_BYTES):
                    choices.append((K // tk, -tm * tn, -tn, tm, tn, tk))
                    break
    _, _, _, tm, tn, tk = min(choices)
    nk = K // tk
    per = nslab // tn
    dims = {"nn": NN_DIMS, "nt": NT_DIMS, "tn": TN_DIMS}[mode]

    n_in = 2 if after is None else 3

    def body(*refs):
        a_ref, b_ref, o_ref, acc = refs[0], refs[1], refs[n_in], refs[n_in + 1:]
        part = lax.dot_general(a_ref[...], b_ref[...], dims, preferred_element_type=F32)
        if nk == 1:
            o_ref[...] = part.astype(o_ref.dtype)
            return
        acc_ref, = acc
        k = pl.program_id(2)

        @pl.when(k == 0)
        def _():
            acc_ref[...] = part

        @pl.when((k > 0) & (k < nk - 1))
        def _():
            acc_ref[...] += part

        @pl.when(k == nk - 1)
        def _():
            o_ref[...] = (acc_ref[...] + part).astype(o_ref.dtype)

    if mode == "tn":
        a_spec = pl.BlockSpec((tk, tm), lambda i, j, k: (k, i))
    else:
        a_spec = pl.BlockSpec((tm, tk), lambda i, j, k: (i, k))
    if mode == "nt":
        b_spec = pl.BlockSpec((tn, tk), lambda i, j, k: (j, k))
    else:
        b_spec = pl.BlockSpec((tk, tn), lambda i, j, k: (k, j))
    if slabs == 1:
        o_shape = jax.ShapeDtypeStruct((M, N), out_dtype)
        o_spec = pl.BlockSpec((tm, tn), lambda i, j, k: (i, j))
    else:
        o_shape = jax.ShapeDtypeStruct((slabs, M, nslab), out_dtype)
        o_spec = pl.BlockSpec((None, tm, tn), lambda i, j, k: (j // per, i, j % per))
    return pl.pallas_call(
        body, name=name, grid=(M // tm, N // tn, nk), in_specs=[a_spec, b_spec] + ([] if after is None else [ANY]),
        out_specs=o_spec, out_shape=o_shape,
        scratch_shapes=[pltpu.VMEM((tm, tn), F32)] if nk > 1 else [],
        compiler_params=_params(("parallel", "parallel", "arbitrary")),
    )(*((a, b) if after is None else (a, b, after)))


def _rmsnorm_fwd(x2, gain):
    T, D = x2.shape
    tr = _pick(T, (256, 128))

    def body(x_ref, g_ref, h_ref, r_ref):
        xv = x_ref[...]
        r = lax.rsqrt(jnp.mean(xv * xv, axis=-1, keepdims=True) + NORM_EPS)
        h_ref[...] = (xv * r * g_ref[...]).astype(BF16)
        r_ref[...] = r

    return pl.pallas_call(
        body, name="rmsnorm_pre_fwd", grid=(T // tr,),
        in_specs=[pl.BlockSpec((tr, D), lambda i: (i, 0)), pl.BlockSpec((1, D), lambda i: (0, 0))],
        out_specs=[pl.BlockSpec((tr, D), lambda i: (i, 0)), pl.BlockSpec((tr, 1), lambda i: (i, 0))],
        out_shape=[jax.ShapeDtypeStruct((T, D), BF16), jax.ShapeDtypeStruct((T, 1), F32)],
        compiler_params=_params(("parallel",)),
    )(x2, gain)


def _rmsnorm_bwd(dh, x2, rstd, gain, dout):
    T, D = x2.shape
    tr = _pick(T, (256, 128))

    def body(dh_ref, x_ref, r_ref, g_ref, do_ref, gx_ref, gg_ref):
        @pl.when(pl.program_id(0) == 0)
        def _():
            gg_ref[...] = jnp.zeros_like(gg_ref)

        n = x_ref[...] * r_ref[...]
        dhv = dh_ref[...].astype(F32)
        dn = dhv * g_ref[...]
        gx_ref[...] = do_ref[...] + r_ref[...] * (dn - n * jnp.mean(dn * n, axis=-1, keepdims=True))
        gg_ref[...] += jnp.sum(dhv * n, axis=0, keepdims=True)

    row = pl.BlockSpec((tr, D), lambda i: (i, 0))
    vec = pl.BlockSpec((1, D), lambda i: (0, 0))
    return pl.pallas_call(
        body, name="rmsnorm_pre_bwd", grid=(T // tr,),
        in_specs=[row, row, pl.BlockSpec((tr, 1), lambda i: (i, 0)), vec, row],
        out_specs=[row, vec],
        out_shape=[jax.ShapeDtypeStruct((T, D), F32), jax.ShapeDtypeStruct((1, D), F32)],
        compiler_params=_params(("arbitrary",)),
    )(dh, x2, rstd, gain, dout)


def _post_loss(yv, x2, tgt2, gain):
    T, D = x2.shape
    tr = _pick(T, (256, 128))

    def body(y_ref, x_ref, t_ref, g_ref, dy_ref, do_ref, loss_ref, gg_ref):
        @pl.when(pl.program_id(0) == 0)
        def _():
            gg_ref[...] = jnp.zeros_like(gg_ref)
            loss_ref[...] = jnp.zeros_like(loss_ref)

        yv_ = y_ref[...].astype(F32)
        r = lax.rsqrt(jnp.mean(yv_ * yv_, axis=-1, keepdims=True) + NORM_EPS)
        n = yv_ * r
        e = (x_ref[...] + n * g_ref[...]) - t_ref[...]
        loss_ref[...] += 0.5 * jnp.sum(jnp.mean(e * e, axis=-1, keepdims=True), axis=0, keepdims=True)
        dz = e / D
        do_ref[...] = dz
        gg_ref[...] += jnp.sum(dz * n, axis=0, keepdims=True)
        dn = dz * g_ref[...]
        dy_ref[...] = (r * (dn - n * jnp.mean(dn * n, axis=-1, keepdims=True))).astype(BF16)

    row = pl.BlockSpec((tr, D), lambda i: (i, 0))
    vec = pl.BlockSpec((1, D), lambda i: (0, 0))
    return pl.pallas_call(
        body, name="post_norm_loss", grid=(T // tr,),
        in_specs=[row, row, row, vec],
        out_specs=[row, row, pl.BlockSpec((1, 1), lambda i: (0, 0)), vec],
        out_shape=[jax.ShapeDtypeStruct((T, D), BF16), jax.ShapeDtypeStruct((T, D), F32),
                   jax.ShapeDtypeStruct((1, 1), F32), jax.ShapeDtypeStruct((1, D), F32)],
        compiler_params=_params(("arbitrary",)),
    )(yv, x2, tgt2, gain)


def _window(rows, cols, at):
    return pl.BlockSpec((pl.Element(rows), pl.Element(cols)), at)


def _gate_windows(tm, tn, D):
    return [_window(tm, tn, lambda i, j: (i * tm, pl.multiple_of(OFF_GA + j * tn, LANES))),
            _window(tm, tn, lambda i, j: (i * tm, pl.multiple_of(OFF_GA + D + j * tn, LANES)))]


def _branch_proj_merge(yh, wbh, proj, ua):
    T, K = yh.shape
    D = wbh.shape[1]
    tm, tn = _pick(T, (512, 256, 128)), _pick(D, (1024, 512, 256))

    def body(a_ref, b_ref, ga_ref, gh_ref, ua_ref, uh_ref, m_ref):
        uh = jnp.dot(a_ref[...], b_ref[...], preferred_element_type=F32).astype(BF16)
        uh_ref[...] = uh
        m_ref[...] = (jax.nn.sigmoid(ga_ref[...]) * ua_ref[...].astype(F32)
                      + jax.nn.sigmoid(gh_ref[...]) * uh.astype(F32)).astype(BF16)

    blk = pl.BlockSpec((tm, tn), lambda i, j: (i, j))
    o = jax.ShapeDtypeStruct((T, D), BF16)
    return pl.pallas_call(
        body, name="branch_hgrn_proj_merge", grid=(T // tm, D // tn),
        in_specs=[pl.BlockSpec((tm, K), lambda i, j: (i, 0)), pl.BlockSpec((K, tn), lambda i, j: (0, j))]
        + _gate_windows(tm, tn, D) + [blk],
        out_specs=[blk, blk], out_shape=[o, o],
        compiler_params=_params(("parallel", "arbitrary")),
    )(yh, wbh, proj, proj, ua)


def _out_dgrad_merge_bwd(dy, wout, proj, ua, uh):
    T, K = dy.shape
    D = wout.shape[0]
    tm, tn = _pick(T, (512, 256, 128)), _pick(D, (1024, 512, 256))
    nj = D // tn

    def body(a_ref, b_ref, ga_ref, gh_ref, ua_ref, uh_ref, dua_ref, duh_ref, dproj_ref):
        d = lax.dot_general(a_ref[...], b_ref[...], NT_DIMS, preferred_element_type=F32)
        sa = jax.nn.sigmoid(ga_ref[...])
        sh = jax.nn.sigmoid(gh_ref[...])
        dua_ref[...] = (d * sa).astype(BF16)
        duh_ref[...] = (d * sh).astype(BF16)
        dga = (d * ua_ref[...].astype(F32) * sa * (1.0 - sa)).astype(BF16)
        dgh = (d * uh_ref[...].astype(F32) * sh * (1.0 - sh)).astype(BF16)
        for jj in range(nj):
            @pl.when(pl.program_id(1) == jj)
            def _():
                dproj_ref[:, jj * tn:(jj + 1) * tn] = dga
                dproj_ref[:, D + jj * tn:D + (jj + 1) * tn] = dgh

    blk = pl.BlockSpec((tm, tn), lambda i, j: (i, j))
    o = jax.ShapeDtypeStruct((T, D), BF16)
    return pl.pallas_call(
        body, name="out_proj_dgrad_merge_bwd", grid=(T // tm, nj),
        in_specs=[pl.BlockSpec((tm, K), lambda i, j: (i, 0)), pl.BlockSpec((tn, K), lambda i, j: (j, 0))]
        + _gate_windows(tm, tn, D) + [blk, blk],
        out_specs=[blk, blk, _window(tm, 2 * D, lambda i, j: (i * tm, OFF_GA))],
        out_shape=[o, o, jax.ShapeDtypeStruct((T, proj.shape[1]), BF16)],
        compiler_params=_params(("parallel", "arbitrary")),
    )(dy, wout, proj, proj, ua, uh)


KV_PAIR = 2
PAIR_HEADS = KV_PAIR * GROUP


def _attn_mask(n):
    qi = lax.broadcasted_iota(jnp.int32, (WINDOW, 2 * WINDOW), 0)
    si = lax.broadcasted_iota(jnp.int32, (WINDOW, 2 * WINDOW), 1)
    dist = qi + WINDOW - si
    return (dist >= 0) & (dist < WINDOW) & ((si >= WINDOW) | (n > 0))


def _first_key_column(shape):
    return lax.broadcasted_iota(jnp.int32, shape, 1) == 0


def _attn_group_fwd(mask, q, k, v, ag, bias, sink):
    mask = jnp.concatenate([mask] * GROUP, axis=0)
    s = lax.dot_general(q.astype(BF16), k.astype(BF16), NT_DIMS, preferred_element_type=F32) * (HEAD_DIM ** -0.5)
    s = jnp.where(mask, s + bias, -1e30)
    m = jnp.maximum(jnp.max(s, axis=-1, keepdims=True), sink)
    p = jnp.exp(s - m)
    e_sink = jnp.exp(sink - m)
    den = jnp.sum(p, axis=-1, keepdims=True) + e_sink
    pb = p.astype(BF16)
    o = jnp.dot(pb, v.astype(BF16), preferred_element_type=F32)
    return o * (jax.nn.silu(ag) / den), pb, e_sink.astype(BF16)


def _attn_group_bwd(q, k, v, ag, kept, dout):
    e_sink = kept[:, 0:1].astype(F32)
    pb = jnp.where(_first_key_column(kept.shape), jnp.zeros_like(kept), kept)
    pf = pb.astype(F32)
    rden = 1.0 / (jnp.sum(pf, axis=-1, keepdims=True) + e_sink)
    qb, kb, vb = q.astype(BF16), k.astype(BF16), v.astype(BF16)
    o = jnp.dot(pb, vb, preferred_element_type=F32) * rden
    sg = jax.nn.sigmoid(ag)
    d_o = dout * (ag * sg)
    dag = dout * o * (sg * (1.0 + ag * (1.0 - sg)))
    d_row = jnp.sum(d_o * o, axis=-1, keepdims=True)
    ds = (pf * rden) * (lax.dot_general(d_o.astype(BF16), vb, NT_DIMS, preferred_element_type=F32) - d_row)
    dsb = ds.astype(BF16)
    dq = jnp.dot(dsb, kb, preferred_element_type=F32) * (HEAD_DIM ** -0.5)
    dk = lax.dot_general(dsb, qb, TN_DIMS, preferred_element_type=F32) * (HEAD_DIM ** -0.5)
    dv = lax.dot_general(pb, (d_o * rden).astype(BF16), TN_DIMS, preferred_element_type=F32)
    return dq, dk, dv, dag, ds, -(e_sink * rden * d_row)


def _attn_operands(q_ref, kc_ref, kp_ref, vc_ref, vp_ref, ag_ref, bias_ref, sink_ref, j):
    lo, hi = j * HEAD_DIM, (j + 1) * HEAD_DIM
    k = jnp.concatenate([kp_ref[:, lo:hi], kc_ref[:, lo:hi]], axis=0)
    v = jnp.concatenate([vp_ref[:, lo:hi], vc_ref[:, lo:hi]], axis=0)
    heads = [j * GROUP + g for g in range(GROUP)]
    q = jnp.concatenate([q_ref[:, h * HEAD_DIM:(h + 1) * HEAD_DIM] for h in heads], axis=0)
    ag = jnp.concatenate([ag_ref[:, h * HEAD_DIM:(h + 1) * HEAD_DIM] for h in heads], axis=0)
    bias = jnp.concatenate([bias_ref[h] for h in heads], axis=0)
    sink = jnp.concatenate([jnp.broadcast_to(sink_ref[h, 0:1, 0:1], (WINDOW, 1)) for h in heads], axis=0)
    return q, k, v, ag, bias, sink


def _attn_fwd(proj, bias_tab, sinks_b, B_loc, S):
    T = B_loc * S
    nb = S // WINDOW

    def at(off, back=0):
        return lambda b, n: ((b * nb + jnp.maximum(n - back, 0)) * WINDOW, off)

    in_specs = [_window(WINDOW, ATTN_WIDTH, at(OFF_AQ)), _window(WINDOW, KV_WIDTH, at(OFF_AK)),
                _window(WINDOW, KV_WIDTH, at(OFF_AK, 1)), _window(WINDOW, KV_WIDTH, at(OFF_AV)),
                _window(WINDOW, KV_WIDTH, at(OFF_AV, 1)), _window(WINDOW, ATTN_WIDTH, at(OFF_AG)),
                pl.BlockSpec((ATTN_HEADS, WINDOW, 2 * WINDOW), lambda b, n: (0, 0, 0)),
                pl.BlockSpec((ATTN_HEADS, 8, LANES), lambda b, n: (0, 0, 0))]

    def body(q_ref, kc_ref, kp_ref, vc_ref, vp_ref, ag_ref, bias_ref, sink_ref, ya_ref, p_ref):
        mask = _attn_mask(pl.program_id(1))
        for j in range(ATTN_KV_HEADS):
            out, probs, e_sink = _attn_group_fwd(
                mask, *_attn_operands(q_ref, kc_ref, kp_ref, vc_ref, vp_ref, ag_ref, bias_ref, sink_ref, j))
            for g in range(GROUP):
                h = j * GROUP + g
                blk = slice(g * WINDOW, (g + 1) * WINDOW)
                ya_ref[:, h * HEAD_DIM:(h + 1) * HEAD_DIM] = out[blk].astype(BF16)
                p_ref[:, h * 2 * WINDOW:(h + 1) * 2 * WINDOW] = probs[blk]
                p_ref[:, h * 2 * WINDOW:h * 2 * WINDOW + 1] = e_sink[blk]

    return pl.pallas_call(
        body, name="attn_fwd", grid=(B_loc, nb), in_specs=in_specs,
        out_specs=[pl.BlockSpec((WINDOW, ATTN_WIDTH), lambda b, n: (b * nb + n, 0)),
                   pl.BlockSpec((WINDOW, ATTN_HEADS * 2 * WINDOW), lambda b, n: (b * nb + n, 0))],
        out_shape=[jax.ShapeDtypeStruct((T, ATTN_WIDTH), BF16),
                   jax.ShapeDtypeStruct((T, ATTN_HEADS * 2 * WINDOW), BF16)],
        compiler_params=_params(("parallel", "parallel")),
    )(proj, proj, proj, proj, proj, proj, bias_tab, sinks_b)


def _attn_bwd(proj, probs, dya, dproj, B_loc, S):
    nb = S // WINDOW
    n_pairs = ATTN_KV_HEADS // KV_PAIR
    qw, kw = PAIR_HEADS * HEAD_DIM, KV_PAIR * HEAD_DIM

    def at(off, width, back=0):
        return lambda b, i, p: ((b * nb + jnp.maximum(nb - 1 - i - back, 0)) * WINDOW,
                                pl.multiple_of(off + p * width, LANES))

    in_specs = [_window(WINDOW, qw, at(OFF_AQ, qw)), _window(WINDOW, kw, at(OFF_AK, kw)),
                _window(WINDOW, kw, at(OFF_AK, kw, 1)), _window(WINDOW, kw, at(OFF_AV, kw)),
                _window(WINDOW, kw, at(OFF_AV, kw, 1)), _window(WINDOW, qw, at(OFF_AG, qw)),
                pl.BlockSpec((WINDOW, PAIR_HEADS * 2 * WINDOW), lambda b, i, p: (b * nb + nb - 1 - i, p)),
                pl.BlockSpec((WINDOW, qw), lambda b, i, p: (b * nb + nb - 1 - i, p)), ANY]

    def body(q_ref, kc_ref, kp_ref, vc_ref, vp_ref, ag_ref, p_ref, dya_ref, dproj_in,
             dproj_ref, dbias_ref, dsink_ref, dkc_ref, dvc_ref):
        b, i, p = pl.program_id(0), pl.program_id(1), pl.program_id(2)

        @pl.when((b == 0) & (i == 0) & (p == 0))
        def _():
            dbias_ref[...] = jnp.zeros_like(dbias_ref)
            dsink_ref[...] = jnp.zeros_like(dsink_ref)

        @pl.when(i == 0)
        def _():
            dkc_ref[p] = jnp.zeros((WINDOW, kw), F32)
            dvc_ref[p] = jnp.zeros((WINDOW, kw), F32)

        dk_carry, dv_carry = dkc_ref[p], dvc_ref[p]
        dqs, dags, dbiases, dsinks, dks, dvs = [], [], [], [], [], []
        for j in range(KV_PAIR):
            heads = [j * GROUP + g for g in range(GROUP)]
            lo, hi = j * HEAD_DIM, (j + 1) * HEAD_DIM
            stack = lambda parts: jnp.concatenate(parts, axis=0)
            k = stack([kp_ref[:, lo:hi], kc_ref[:, lo:hi]])
            v = stack([vp_ref[:, lo:hi], vc_ref[:, lo:hi]])
            q = stack([q_ref[:, h * HEAD_DIM:(h + 1) * HEAD_DIM] for h in heads])
            ag = stack([ag_ref[:, h * HEAD_DIM:(h + 1) * HEAD_DIM] for h in heads])
            dout = stack([dya_ref[:, h * HEAD_DIM:(h + 1) * HEAD_DIM].astype(F32) for h in heads])
            kept = stack([p_ref[:, h * 2 * WINDOW:(h + 1) * 2 * WINDOW] for h in heads])
            dq, dk, dv, dag, dbias, dsink = _attn_group_bwd(q, k, v, ag, kept, dout)
            dks.append((dk[WINDOW:] + dk_carry[:, lo:hi]).astype(BF16))
            dvs.append((dv[WINDOW:] + dv_carry[:, lo:hi]).astype(BF16))
            dkc_ref[p, :, lo:hi] = dk[:WINDOW]
            dvc_ref[p, :, lo:hi] = dv[:WINDOW]
            for g in range(GROUP):
                blk = slice(g * WINDOW, (g + 1) * WINDOW)
                dqs.append(dq[blk].astype(BF16))
                dags.append(dag[blk].astype(BF16))
                dbiases.append(dbias[blk])
                dsinks.append(jnp.broadcast_to(jnp.sum(dsink[blk], axis=0, keepdims=True), (8, LANES)))

        for pair in range(n_pairs):
            @pl.when(p == pair)
            def _():
                for j in range(KV_PAIR):
                    col = (pair * KV_PAIR + j) * HEAD_DIM
                    dproj_ref[:, OFF_AK + col:OFF_AK + col + HEAD_DIM] = dks[j]
                    dproj_ref[:, OFF_AV + col:OFF_AV + col + HEAD_DIM] = dvs[j]
                for hh in range(PAIR_HEADS):
                    h = pair * PAIR_HEADS + hh
                    dproj_ref[:, OFF_AQ + h * HEAD_DIM:OFF_AQ + (h + 1) * HEAD_DIM] = dqs[hh]
                    dproj_ref[:, OFF_AG + h * HEAD_DIM:OFF_AG + (h + 1) * HEAD_DIM] = dags[hh]
                    dbias_ref[h] += dbiases[hh]
                    dsink_ref[h] += dsinks[hh]

    return pl.pallas_call(
        body, name="attn_bwd", grid=(B_loc, nb, n_pairs), in_specs=in_specs,
        out_specs=[_window(WINDOW, OFF_HQ, lambda b, i, p: ((b * nb + nb - 1 - i) * WINDOW, 0)),
                   pl.BlockSpec((ATTN_HEADS, WINDOW, 2 * WINDOW), lambda b, i, p: (0, 0, 0)),
                   pl.BlockSpec((ATTN_HEADS, 8, LANES), lambda b, i, p: (0, 0, 0))],
        out_shape=[jax.ShapeDtypeStruct(dproj.shape, dproj.dtype),
                   jax.ShapeDtypeStruct((ATTN_HEADS, WINDOW, 2 * WINDOW), F32),
                   jax.ShapeDtypeStruct((ATTN_HEADS, 8, LANES), F32)],
        input_output_aliases={8: 0},
        scratch_shapes=[pltpu.VMEM((n_pairs, WINDOW, kw), F32), pltpu.VMEM((n_pairs, WINDOW, kw), F32)],
        compiler_params=_params(("arbitrary", "arbitrary", "arbitrary")),
    )(proj, proj, proj, proj, proj, proj, probs, dya, dproj)


class _HgrnPre:
    def __init__(self, fr, qr, lb, g_scr):
        t = lax.broadcasted_iota(jnp.int32, (CHUNK, CHUNK), 0)
        s = lax.broadcasted_iota(jnp.int32, (CHUNK, CHUNK), 1)
        self.sg = jax.nn.sigmoid(fr)
        self.f = lb + (1.0 - lb) * self.sg
        g = jnp.dot((t >= s).astype(F32), jnp.log(self.f), precision=lax.Precision.HIGHEST, preferred_element_type=F32)
        g_scr[...] = g
        self.g = g
        self.row = lax.broadcasted_iota(jnp.int32, g.shape, 0)
        self.refs = [jnp.zeros((1, g.shape[1]), F32)] + [g_scr[pl.ds(i * SUB - 1, 1), :] for i in range(1, NSUB)]
        self.gend = g_scr[pl.ds(CHUNK - 1, 1), :]
        refrow = jnp.zeros_like(g)
        for i in range(1, NSUB):
            refrow = jnp.where(self.row >= i * SUB, self.refs[i], refrow)
        self.sigq = jax.nn.sigmoid(qr)
        self.qs = qr * self.sigq
        self.k = 1.0 - self.f
        self.eg = jnp.exp(g)
        self.eqd = jnp.exp(g - refrow)
        self.ekd = [jnp.exp(jnp.where(self.row < (i + 1) * SUB, self.refs[i] - g, 0.0)) for i in range(NSUB)]
        self.ekdec = jnp.exp(self.gend - g)
        self.qg = self.qs * self.eg
        self.qd = self.qs * self.eqd
        self.kd = [self.k * e for e in self.ekd]
        self.kdec = self.k * self.ekdec
        self.egend = jnp.exp(self.gend)


def _hgrn_pair_mask():
    t = lax.broadcasted_iota(jnp.int32, (CHUNK, NSUB * CHUNK), 0)
    col = lax.broadcasted_iota(jnp.int32, (CHUNK, NSUB * CHUNK), 1)
    return ((t // SUB) == (col // CHUNK)) & ((col % CHUNK) <= t)


def _hgrn_head_out(p, lanes, state_t, v, mask):
    qg, qd = p.qg[:, lanes], p.qd[:, lanes]
    kall = jnp.concatenate([kd[:, lanes] for kd in p.kd], axis=0)
    vst = jnp.concatenate([v] * NSUB, axis=0)
    am = jnp.where(mask, _bdot(qd, kall, NT_DIMS), 0.0)
    o = _bdot(qg, state_t, NT_DIMS) + _bdot(am, vst)
    return o, (qg, qd, kall, am, vst)


def _hgrn_fwd(proj, lb, gain, B_loc, S):
    T = B_loc * S
    nc = S // CHUNK
    nh = HGRN_HEADS

    def at(off):
        return lambda b, n: ((b * nc + n) * CHUNK, off)

    vec = pl.BlockSpec((1, HGRN_WIDTH), lambda b, n: (0, 0))

    def body(q_ref, f_ref, v_ref, hg_ref, lb_ref, gain_ref, yh_ref, st_ref, state_scr, g_scr):
        @pl.when(pl.program_id(1) == 0)
        def _():
            state_scr[...] = jnp.zeros_like(state_scr)

        p = _HgrnPre(f_ref[...], q_ref[...], lb_ref[...], g_scr)
        v = v_ref[...]
        gate = gain_ref[...] * jax.nn.silu(hg_ref[...])
        mask = _hgrn_pair_mask()
        for hd in range(nh):
            lanes = slice(hd * HGRN_DIM, (hd + 1) * HGRN_DIM)
            st = state_scr[hd]
            st_ref[hd] = st
            o, _ = _hgrn_head_out(p, lanes, st, v[:, lanes], mask)
            rs = lax.rsqrt(jnp.mean(o * o, axis=-1, keepdims=True) + NORM_EPS)
            yh_ref[:, lanes] = (o * rs * gate[:, lanes]).astype(BF16)
            state_scr[hd] = st * p.egend[:, lanes] + _bdot(v[:, lanes], p.kdec[:, lanes], TN_DIMS)

    return pl.pallas_call(
        body, name="hgrn_fwd", grid=(B_loc, nc),
        in_specs=[_window(CHUNK, HGRN_WIDTH, at(OFF_HQ)), _window(CHUNK, HGRN_WIDTH, at(OFF_HF)),
                  _window(CHUNK, HGRN_WIDTH, at(OFF_HI)), _window(CHUNK, HGRN_WIDTH, at(OFF_HG)), vec, vec],
        out_specs=[pl.BlockSpec((CHUNK, HGRN_WIDTH), lambda b, n: (b * nc + n, 0)),
                   pl.BlockSpec((None, None, nh, HGRN_DIM, HGRN_DIM), lambda b, n: (b, n, 0, 0, 0))],
        out_shape=[jax.ShapeDtypeStruct((T, HGRN_WIDTH), BF16),
                   jax.ShapeDtypeStruct((B_loc, nc, nh, HGRN_DIM, HGRN_DIM), F32)],
        scratch_shapes=[pltpu.VMEM((nh, HGRN_DIM, HGRN_DIM), F32), pltpu.VMEM((CHUNK, HGRN_WIDTH), F32)],
        compiler_params=_params(("parallel", "arbitrary")),
    )(proj, proj, proj, proj, lb, gain)


def _hgrn_bwd(proj, states, lb, gain, dyh, dproj, B_loc, S):
    nc = S // CHUNK
    nh = HGRN_HEADS

    def at(off):
        return lambda b, i: ((b * nc + nc - 1 - i) * CHUNK, off)

    vec = pl.BlockSpec((1, HGRN_WIDTH), lambda b, i: (0, 0))
    in_specs = [_window(CHUNK, HGRN_WIDTH, at(OFF_HQ)), _window(CHUNK, HGRN_WIDTH, at(OFF_HF)),
                _window(CHUNK, HGRN_WIDTH, at(OFF_HI)), _window(CHUNK, HGRN_WIDTH, at(OFF_HG)), vec, vec,
                pl.BlockSpec((None, None, nh, HGRN_DIM, HGRN_DIM), lambda b, i: (b, nc - 1 - i, 0, 0, 0)),
                pl.BlockSpec((CHUNK, HGRN_WIDTH), lambda b, i: (b * nc + nc - 1 - i, 0)), ANY]
    acc_spec = pl.BlockSpec((None, 1, HGRN_WIDTH), lambda b, i: (b, 0, 0))

    def body(q_ref, f_ref, v_ref, hg_ref, lb_ref, gain_ref, st_ref, dyh_ref, dproj_in,
             dproj_ref, dlb_ref, dgain_ref, dstate_scr, g_scr, dg_scr):
        dq_ref, df_ref, dv_ref, dhg_ref = [dproj_ref.at[:, pl.ds(k * HGRN_WIDTH, HGRN_WIDTH)] for k in range(4)]

        @pl.when(pl.program_id(1) == 0)
        def _():
            dstate_scr[...] = jnp.zeros_like(dstate_scr)
            dlb_ref[...] = jnp.zeros_like(dlb_ref)
            dgain_ref[...] = jnp.zeros_like(dgain_ref)

        qr, lb, gain, hg, v = q_ref[...], lb_ref[...], gain_ref[...], hg_ref[...], v_ref[...]
        p = _HgrnPre(f_ref[...], qr, lb, g_scr)
        sgh = jax.nn.sigmoid(hg)
        sil = hg * sgh
        dy = dyh_ref[...].astype(F32)
        mask = _hgrn_pair_mask()
        dqg, dqd, dkdec, dv, dhg, dgend, dgain = [], [], [], [], [], [], []
        dkd = [[] for _ in range(NSUB)]
        heads = [slice(hd * HGRN_DIM, (hd + 1) * HGRN_DIM) for hd in range(nh)]
        sts = [st_ref[hd] for hd in range(nh)]
        dnews = [dstate_scr[hd] for hd in range(nh)]
        fwd = [_hgrn_head_out(p, lanes, st, v[:, lanes], mask) for lanes, st in zip(heads, sts)]
        for lanes, st, dnew in zip(heads, sts, dnews):
            dkdec_h = _bdot(v[:, lanes], dnew)
            dkdec.append(dkdec_h)
            dgend.append(jnp.sum(dkdec_h * p.kdec[:, lanes], axis=0, keepdims=True)
                         + jnp.sum(dnew * st, axis=0, keepdims=True) * p.egend[:, lanes])
        dos = []
        gate_grad = sgh * (1.0 + hg * (1.0 - sgh))
        for lanes, (o, _) in zip(heads, fwd):
            rs = lax.rsqrt(jnp.mean(o * o, axis=-1, keepdims=True) + NORM_EPS)
            n = o * rs
            dyn = dy[:, lanes] * n
            dgain.append(jnp.sum(dyn * sil[:, lanes], axis=0, keepdims=True))
            dhg.append(dyn * gain[:, lanes] * gate_grad[:, lanes])
            dn = dy[:, lanes] * gain[:, lanes] * sil[:, lanes]
            dos.append(rs * (dn - n * jnp.mean(dn * n, axis=-1, keepdims=True)))
        drs = []
        for hd, (lanes, st, dnew, do, (_, (qg, qd, kall, am, vst))) in enumerate(zip(heads, sts, dnews, dos, fwd)):
            dqg.append(_bdot(do, st))
            dstate_scr[hd] = _bdot(do, qg, TN_DIMS) + dnew * p.egend[:, lanes]
            drs.append(jnp.where(mask, _bdot(do, vst, NT_DIMS), 0.0))
            dvst = _bdot(am, do, TN_DIMS)
            dv.append(sum(dvst[i * CHUNK:(i + 1) * CHUNK] for i in range(NSUB)) + _bdot(p.kdec[:, lanes], dnew, NT_DIMS))
        for dr, (_, (qg, qd, kall, am, vst)) in zip(drs, fwd):
            dqd.append(_bdot(dr, kall))
            dkall = _bdot(dr, qd, TN_DIMS)
            for i in range(NSUB):
                dkd[i].append(dkall[i * CHUNK:(i + 1) * CHUNK])

        wide = lambda parts: jnp.concatenate(parts, axis=1)
        dqg, dqd, dkdec = wide(dqg), wide(dqd), wide(dkdec)
        t2 = dqd * p.qd
        dg = dqg * p.qg + t2 - dkdec * p.kdec
        dk = dkdec * p.ekdec
        dg_scr[...] = jnp.zeros_like(dg_scr)
        for i in range(NSUB):
            dkd_i = wide(dkd[i])
            tk = jnp.where(p.row < (i + 1) * SUB, dkd_i * p.kd[i], 0.0)
            dg = dg - tk
            dk = dk + dkd_i * p.ekd[i]
            if i >= 1:
                in_blk = (p.row >= i * SUB) & (p.row < (i + 1) * SUB)
                dg_scr[pl.ds(i * SUB - 1, 1), :] = (jnp.sum(tk, axis=0, keepdims=True)
                                                    - jnp.sum(jnp.where(in_blk, t2, 0.0), axis=0, keepdims=True))
        dg_scr[pl.ds(CHUNK - 1, 1), :] = wide(dgend)
        t = lax.broadcasted_iota(jnp.int32, (CHUNK, CHUNK), 0)
        s = lax.broadcasted_iota(jnp.int32, (CHUNK, CHUNK), 1)
        dlogf = jnp.dot((t <= s).astype(F32), dg + dg_scr[...], precision=lax.Precision.HIGHEST, preferred_element_type=F32)
        df = dlogf / p.f - dk
        df_ref[...] = (df * (1.0 - lb) * p.sg * (1.0 - p.sg)).astype(BF16)
        dlb_ref[...] += jnp.sum(df * (1.0 - p.sg), axis=0, keepdims=True)
        dq_ref[...] = ((dqg * p.eg + dqd * p.eqd) * p.sigq * (1.0 + qr * (1.0 - p.sigq))).astype(BF16)
        dv_ref[...] = wide(dv).astype(BF16)
        dhg_ref[...] = wide(dhg).astype(BF16)
        dgain_ref[...] += wide(dgain)

    acc = jax.ShapeDtypeStruct((B_loc, 1, HGRN_WIDTH), F32)
    return pl.pallas_call(
        body, name="hgrn_bwd", grid=(B_loc, nc), in_specs=in_specs,
        out_specs=[_window(CHUNK, 4 * HGRN_WIDTH, at(OFF_HQ)), acc_spec, acc_spec],
        out_shape=[jax.ShapeDtypeStruct(dproj.shape, dproj.dtype), acc, acc],
        input_output_aliases={8: 0},
        scratch_shapes=[pltpu.VMEM((nh, HGRN_DIM, HGRN_DIM), F32), pltpu.VMEM((CHUNK, HGRN_WIDTH), F32),
                        pltpu.VMEM((CHUNK, HGRN_WIDTH), F32)],
        compiler_params=_params(("parallel", "arbitrary")),
    )(proj, proj, proj, proj, lb, gain, states, dyh, dproj)


def _adamw(w, g, m, v, name):
    R, C = w.shape
    tr = _pick(R, (128, 64, 32, 16, 8)) if C > 1024 else _pick(R, (512, 256, 128, 64, 32, 16, 8))

    def body(w_ref, g_ref, m_ref, v_ref, d_ref, nm_ref, nv_ref, g_out_ref):
        gv = g_ref[...]
        g_out_ref[...] = gv
        nm = ADAM_B1 * m_ref[...] + (1.0 - ADAM_B1) * gv
        nv = ADAM_B2 * v_ref[...] + (1.0 - ADAM_B2) * (gv * gv)
        m_hat = nm / (1.0 - ADAM_B1 ** ADAM_STEP)
        v_hat = nv / (1.0 - ADAM_B2 ** ADAM_STEP)
        d_ref[...] = -ADAM_LR * (m_hat / (jnp.sqrt(v_hat) + ADAM_EPS) + ADAM_WD * w_ref[...])
        nm_ref[...] = nm
        nv_ref[...] = nv

    blk = pl.BlockSpec((tr, C), lambda i: (i, 0))
    o = jax.ShapeDtypeStruct((R, C), F32)
    return pl.pallas_call(
        body, name=name, grid=(R // tr,), in_specs=[blk] * 4, out_specs=[blk] * 4, out_shape=[o, o, o, o],
        compiler_params=_params(("parallel",)),
    )(w, g, m, v)


ANY = pl.BlockSpec(memory_space=pl.ANY)
VMEM_SPEC = pl.BlockSpec(memory_space=pltpu.VMEM)


def _place():
    x, y, c = lax.axis_index("x"), lax.axis_index("y"), lax.axis_index("c")
    other_chips = [(1 - x, y), (x, 1 - y), (1 - x, 1 - y)]
    return x, y, c, other_chips


def _cast_into_full(w, ax, s_arr, name):
    R, C = w.shape
    tr = _pick(R, (256, 128))
    nr = R // tr

    def body(s_ref, w_ref, o_ref):
        o_ref[...] = w_ref[...].astype(BF16)

    if ax == 1:
        shape, o_map = (R, N_CHIPS * C), lambda i, s: (i, s[0])
    else:
        shape, o_map = (N_CHIPS * R, C), lambda i, s: (s[0] * nr + i, 0)
    return pl.pallas_call(
        body, name=name,
        grid_spec=pltpu.PrefetchScalarGridSpec(
            num_scalar_prefetch=1, grid=(nr,), in_specs=[pl.BlockSpec((tr, C), lambda i, s: (i, 0))],
            out_specs=pl.BlockSpec((tr, C), o_map)),
        out_shape=jax.ShapeDtypeStruct(shape, BF16),
        compiler_params=_params(("parallel",)),
    )(s_arr, w)


class _Gather:
    def __init__(self, fulls, shard_shapes, axes, tag):
        self.shapes, self.axes, self.tag, self.nw = shard_shapes, axes, tag, len(fulls)
        self.fulls, self.sems, self.token = list(fulls), {}, None

    def start(self, peers, after=None):
        nw, np_ = self.nw, len(peers)

        def body(*refs):
            ins, sems = refs[:nw], refs[nw + (after is not None):nw + (after is not None) + 2 * np_]
            for k, j in enumerate(peers):
                for cp in self._peer_copies(ins, sems[2 * k], sems[2 * k + 1], j):
                    cp.start()
            refs[-1][...] = jnp.zeros_like(refs[-1])

        out = pl.pallas_call(
            body, name="gather_start_%s_%s" % (self.tag, "".join(map(str, peers))),
            out_shape=(*[pltpu.SemaphoreType.DMA((nw,))] * (2 * np_),
                       *[pltpu.HBM(f.shape, f.dtype) for f in self.fulls], jax.ShapeDtypeStruct((8, LANES), F32)),
            in_specs=[HBM_SPEC] * nw + ([] if after is None else [ANY]),
            out_specs=(*[SEM_SPEC] * (2 * np_), *[HBM_SPEC] * nw, VMEM_SPEC),
            input_output_aliases={k: 2 * np_ + k for k in range(nw)},
            compiler_params=pltpu.CompilerParams(has_side_effects=DATAFLOW),
        )(*[pltpu.with_memory_space_constraint(f, pltpu.HBM) for f in self.fulls], *(() if after is None else (after,)))
        for k, j in enumerate(peers):
            self.sems[j] = (out[2 * k], out[2 * k + 1])
        self.fulls = list(out[2 * np_:2 * np_ + nw])
        self.token = out[-1]

    def _region(self, ref, i, t, half):
        R, C = self.shapes[i]
        hr = R // 2
        if self.axes[i] == 1:
            return ref.at[pl.ds(half * hr, hr), pl.ds(pl.multiple_of(t * C, LANES), C)]
        return ref.at[pl.ds(t * R + half * hr, hr), :]

    def _peer_copies(self, refs, send_sems, recv_sems, j):
        x, y, c, chips = _place()
        s = 2 * x + y
        return [pltpu.make_async_remote_copy(
            src_ref=self._region(refs[i], i, s, c), dst_ref=self._region(refs[i], i, s, c), send_sem=send_sems.at[i],
            recv_sem=recv_sems.at[i], device_id=(*chips[j], c), device_id_type=MESH) for i in range(self.nw)]

    def wait(self, peers, after):
        nw, np_ = self.nw, len(peers)

        def body(*refs):
            ins, sems = refs[:nw], refs[nw:nw + 2 * np_]
            for k, j in enumerate(peers):
                for cp in self._peer_copies(ins, sems[2 * k], sems[2 * k + 1], j):
                    cp.wait_send()
                    cp.wait_recv()

        sem_args = [s for j in peers for s in self.sems[j]]
        out = pl.pallas_call(
            body, name="gather_wait_%s_%s" % (self.tag, "".join(map(str, peers))),
            out_shape=tuple(pltpu.HBM(f.shape, f.dtype) for f in self.fulls),
            in_specs=[HBM_SPEC] * nw + [SEM_SPEC] * (2 * np_) + [ANY], out_specs=tuple([HBM_SPEC] * nw),
            input_output_aliases={k: k for k in range(nw)},
            compiler_params=pltpu.CompilerParams(has_side_effects=DATAFLOW),
        )(*self.fulls, *sem_args, after)
        self.fulls = list(out)

    def forward(self, peers):
        nw, np_ = self.nw, len(peers)

        def body(*refs):
            ins, outs = refs[:nw], refs[nw:2 * nw]
            send_sems, recv_sems = refs[2 * nw:]
            x, y, c, chips = _place()
            cps = []
            for i in range(nw):
                for k, j in enumerate(peers):
                    t = 2 * chips[j][0] + chips[j][1]
                    cp = pltpu.make_async_remote_copy(
                        src_ref=self._region(ins[i], i, t, c), dst_ref=self._region(outs[i], i, t, c),
                        send_sem=send_sems.at[i * np_ + k], recv_sem=recv_sems.at[i * np_ + k],
                        device_id=(x, y, 1 - c), device_id_type=MESH)
                    cp.start()
                    cps.append(cp)
            for cp in cps:
                cp.wait()

        out = pl.pallas_call(
            body, name="gather_forward_%s_%s" % (self.tag, "".join(map(str, peers))),
            in_specs=[ANY] * nw, out_specs=[ANY] * nw,
            out_shape=[jax.ShapeDtypeStruct(f.shape, f.dtype) for f in self.fulls],
            input_output_aliases={i: i for i in range(nw)},
            scratch_shapes=[pltpu.SemaphoreType.DMA((nw * np_,)), pltpu.SemaphoreType.DMA((nw * np_,))],
        )(*self.fulls)
        self.fulls = list(out)


def _matmul_slab(a, wfull, slab_arr, prev, after, name):
    M, K = a.shape
    N = wfull.shape[1]
    nslab = N // N_CHIPS
    tn = _pick(nslab, (2688, 896, 512, 384, 256, 128))
    tm = _pick(M, (512, 256, 128) if tn > 1024 else (1024, 512, 256, 128))
    per = nslab // tn
    extra = [e for e in (prev, after) if e is not None]

    def body(slab_ref, a_ref, b_ref, *rest):
        rest[len(extra)][...] = jnp.dot(a_ref[...], b_ref[...], preferred_element_type=F32)

    return pl.pallas_call(
        body, name=name,
        grid_spec=pltpu.PrefetchScalarGridSpec(
            num_scalar_prefetch=1, grid=(M // tm, per),
            in_specs=[pl.BlockSpec((tm, K), lambda i, j, sl: (i, 0)),
                      pl.BlockSpec((K, tn), lambda i, j, sl: (0, sl[0] * per + j))] + [ANY] * len(extra),
            out_specs=pl.BlockSpec((tm, tn), lambda i, j, sl: (i, sl[0] * per + j))),
        out_shape=jax.ShapeDtypeStruct((M, N), F32),
        input_output_aliases={} if prev is None else {3: 0},
        compiler_params=_params(("parallel", "arbitrary")),
    )(slab_arr, a, wfull, *extra)


def _exchange_sibling_halves(gs, name):
    nw = len(gs)

    def body(*refs):
        ins, outs = refs[:nw], refs[nw:2 * nw]
        send_sems, recv_sems = refs[2 * nw:]
        x, y, c, _ = _place()
        cps = []
        for i in range(nw):
            cp = pltpu.make_async_remote_copy(src_ref=ins[i].at[:, 1 - c], dst_ref=outs[i], send_sem=send_sems.at[i],
                                              recv_sem=recv_sems.at[i], device_id=(x, y, 1 - c), device_id_type=MESH)
            cp.start()
            cps.append(cp)
        for cp in cps:
            cp.wait()

    return pl.pallas_call(
        body, name=name, in_specs=[ANY] * nw, out_specs=[ANY] * nw,
        out_shape=[jax.ShapeDtypeStruct((g.shape[0],) + g.shape[2:], g.dtype) for g in gs],
        scratch_shapes=[pltpu.SemaphoreType.DMA((nw,)), pltpu.SemaphoreType.DMA((nw,))],
    )(*gs)


HBM_SPEC = pl.BlockSpec(memory_space=pltpu.HBM)
SEM_SPEC = pl.BlockSpec(memory_space=pltpu.SEMAPHORE)
DATAFLOW = pltpu.SideEffectType.DATAFLOW_SIDE_EFFECTING


def _chip_copies(ins, lands, send_sems, recv_sems):
    x, y, c, chips = _place()
    return [pltpu.make_async_remote_copy(
        src_ref=ins[i].at[2 * chip[0] + chip[1]], dst_ref=lands[i].at[j], send_sem=send_sems.at[i * 3 + j],
        recv_sem=recv_sems.at[i * 3 + j], device_id=(*chip, c), device_id_type=MESH)
        for i in range(len(ins)) for j, chip in enumerate(chips)]


def _chips_send_start(ss, name):
    nw = len(ss)
    lands = [pltpu.with_memory_space_constraint(lax.empty((N_CHIPS - 1,) + s.shape[1:], s.dtype), pltpu.HBM) for s in ss]

    def body(*refs):
        ins, land_refs = refs[:nw], refs[nw:2 * nw]
        send_sems, recv_sems = refs[2 * nw], refs[2 * nw + 1]
        token = refs[-1]
        for cp in _chip_copies(ins, land_refs, send_sems, recv_sems):
            cp.start()
        token[...] = jnp.zeros_like(token)

    n = 3 * nw
    out = pl.pallas_call(
        body, name=name,
        out_shape=(pltpu.SemaphoreType.DMA((n,)), pltpu.SemaphoreType.DMA((n,)),
                   *[pltpu.HBM(s.shape, s.dtype) for s in ss], *[pltpu.HBM(l.shape, l.dtype) for l in lands],
                   jax.ShapeDtypeStruct((8, LANES), F32)),
        in_specs=[HBM_SPEC] * (2 * nw), out_specs=(SEM_SPEC, SEM_SPEC, *[HBM_SPEC] * (2 * nw), VMEM_SPEC),
        input_output_aliases={k: 2 + k for k in range(2 * nw)},
        compiler_params=pltpu.CompilerParams(has_side_effects=DATAFLOW),
    )(*[pltpu.with_memory_space_constraint(s, pltpu.HBM) for s in ss], *lands)
    return out[0], out[1], list(out[2:2 + nw]), list(out[2 + nw:2 + 2 * nw]), out[-1]


def _chips_send_wait(send_sems, recv_sems, ss, lands, after, name):
    nw = len(ss)

    def body(*refs):
        ins, land_refs = refs[:nw], refs[nw:2 * nw]
        s_sems, r_sems = refs[2 * nw], refs[2 * nw + 1]
        for cp in _chip_copies(ins, land_refs, s_sems, r_sems):
            cp.wait_send()
            cp.wait_recv()

    out = pl.pallas_call(
        body, name=name,
        out_shape=(*[pltpu.HBM(s.shape, s.dtype) for s in ss], *[pltpu.HBM(l.shape, l.dtype) for l in lands]),
        in_specs=[HBM_SPEC] * (2 * nw) + [SEM_SPEC, SEM_SPEC, ANY], out_specs=tuple([HBM_SPEC] * (2 * nw)),
        input_output_aliases={k: k for k in range(2 * nw)},
        compiler_params=pltpu.CompilerParams(has_side_effects=DATAFLOW),
    )(*ss, *lands, send_sems, recv_sems, after)
    return list(out[nw:])


def _share_and_sum_small(fs, pack, name):
    nw = len(fs)
    rows = pack.shape[0]

    def body(*refs):
        ins, pack_ref = refs[:nw], refs[nw]
        outs, sum_ref = refs[nw + 1:2 * nw + 1], refs[2 * nw + 1]
        all_ref, send_sems, recv_sems = refs[2 * nw + 2:]
        x, y, c, _ = _place()
        me = 4 * x + 2 * y + c
        cps = []
        for i in range(nw):
            cp = pltpu.make_async_remote_copy(src_ref=ins[i].at[c], dst_ref=outs[i].at[c], send_sem=send_sems.at[i],
                                              recv_sem=recv_sems.at[i], device_id=(x, y, 1 - c), device_id_type=MESH)
            cp.start()
            cps.append(cp)
        all_ref[me] = pack_ref[...]
        for k in range(1, N_DEV):
            to = (1 - x if k & 4 else x, 1 - y if k & 2 else y, 1 - c if k & 1 else c)
            cp = pltpu.make_async_remote_copy(
                src_ref=pack_ref, dst_ref=all_ref.at[me], send_sem=send_sems.at[nw + k - 1],
                recv_sem=recv_sems.at[nw + k - 1], device_id=to, device_id_type=MESH)
            cp.start()
            cps.append(cp)
        for cp in cps:
            cp.wait()
        total = all_ref[0]
        for d in range(1, N_DEV):
            total = total + all_ref[d]
        sum_ref[...] = total

    n_sems = nw + N_DEV - 1
    out = pl.pallas_call(
        body, name=name, in_specs=[ANY] * nw + [VMEM_SPEC], out_specs=[ANY] * nw + [VMEM_SPEC],
        out_shape=[jax.ShapeDtypeStruct(f.shape, f.dtype) for f in fs] + [jax.ShapeDtypeStruct(pack.shape, F32)],
        input_output_aliases={i: i for i in range(nw)},
        scratch_shapes=[pltpu.VMEM((N_DEV, rows, LANES), F32), pltpu.SemaphoreType.DMA((n_sems,)),
                        pltpu.SemaphoreType.DMA((n_sems,))],
    )(*fs, pack)
    return list(out[:nw]), out[nw]


def _sum_sibling(g, land, where_arr, name):
    P, Q = g.shape[-2:]
    tp = _pick(P, (256, 128, 64))

    def body(w_ref, g_ref, l_ref, s_ref):
        s_ref[...] = (g_ref[...].astype(F32) + l_ref[...].astype(F32)).astype(BF16)

    if g.ndim == 4:
        g_spec = pl.BlockSpec((None, None, tp, Q), lambda j, r, w: (w[1 + j], w[0], r, 0))
    else:
        g_spec = pl.BlockSpec((None, tp, Q), lambda j, r, w: (w[1 + j], r, 0))
    slab = pl.BlockSpec((None, tp, Q), lambda j, r, w: (w[1 + j], r, 0))
    return pl.pallas_call(
        body, name=name,
        grid_spec=pltpu.PrefetchScalarGridSpec(
            num_scalar_prefetch=1, grid=(N_CHIPS - 1, P // tp), in_specs=[g_spec, slab], out_specs=slab),
        out_shape=jax.ShapeDtypeStruct((N_CHIPS, P, Q), BF16),
        compiler_params=_params(("parallel", "parallel")),
    )(where_arr, g, land)


def _sum_chips(g, land, recv, sc_arr, name):
    P, Q = g.shape[-2:]
    tp = _pick(P, (256, 128, 64))

    def body(sc_ref, g_ref, l_ref, r_ref, f_ref):
        acc = g_ref[...].astype(F32) + l_ref[...].astype(F32)
        for j in range(N_CHIPS - 1):
            acc = acc + r_ref[j].astype(F32)
        f_ref[...] = acc

    if g.ndim == 4:
        g_spec = pl.BlockSpec((None, None, tp, Q), lambda r, sc: (sc[0], sc[1], r, 0))
    else:
        g_spec = pl.BlockSpec((None, tp, Q), lambda r, sc: (sc[0], r, 0))
    return pl.pallas_call(
        body, name=name,
        grid_spec=pltpu.PrefetchScalarGridSpec(
            num_scalar_prefetch=1, grid=(P // tp,),
            in_specs=[g_spec,
                      pl.BlockSpec((None, tp, Q), lambda r, sc: (sc[0], r, 0)),
                      pl.BlockSpec((N_CHIPS - 1, tp, Q), lambda r, sc: (0, r, 0))],
            out_specs=pl.BlockSpec((None, tp, Q), lambda r, sc: (sc[1], r, 0))),
        out_shape=jax.ShapeDtypeStruct((2, P, Q), F32),
        compiler_params=_params(("parallel",)),
    )(sc_arr, g, land, recv)


class _Reduction:
    def __init__(self, names, pieces, lands, flight):
        self.names, self.pieces, self.lands = names, pieces, lands
        self.send_sems, self.recv_sems, self.sums, self.zones, self.token = flight


def _reduce_start(pieces, lands, names, where_arr):
    tag = names[0] if len(names) == 1 else "branches"
    sums = [_sum_sibling(g, l, where_arr, "sum_sibling_" + nm) for g, l, nm in zip(pieces, lands, names)]
    return _Reduction(names, pieces, lands, _chips_send_start(sums, "grads_to_chips_start_" + tag))


def _wgrad_half(a, b, half_arr, name, after=None):
    K, M = a.shape
    nslab = b.shape[1] // N_CHIPS
    tm = M // 2
    tn = _pick(nslab, (896, 512, 384, 256, 128))
    per = nslab // tn
    assert 4 * K * (tm + tn) <= MATMUL_OPERAND_BYTES

    def body(h_ref, a_ref, b_ref, *rest):
        rest[-1][...] = lax.dot_general(a_ref[...], b_ref[...], TN_DIMS, preferred_element_type=F32).astype(BF16)

    return pl.pallas_call(
        body, name=name,
        grid_spec=pltpu.PrefetchScalarGridSpec(
            num_scalar_prefetch=1, grid=(b.shape[1] // tn,),
            in_specs=[pl.BlockSpec((K, tm), lambda j, h: (0, h[0])), pl.BlockSpec((K, tn), lambda j, h: (0, j))]
            + ([] if after is None else [ANY]),
            out_specs=pl.BlockSpec((None, tm, tn), lambda j, h: (j // per, 0, j % per))),
        out_shape=jax.ShapeDtypeStruct((N_CHIPS, tm, nslab), BF16),
        compiler_params=_params(("parallel",)),
    )(*((half_arr, a, b) if after is None else (half_arr, a, b, after)))


def _sibling_send_start(g, name):
    zone = pltpu.with_memory_space_constraint(lax.empty(g.shape, g.dtype), pltpu.HBM)

    def body(g_ref, zone_ref, send_sem, recv_sem, g_thru, zone_thru, token):
        x, y, c, _ = _place()
        pltpu.make_async_remote_copy(src_ref=g_ref, dst_ref=zone_ref, send_sem=send_sem, recv_sem=recv_sem,
                                     device_id=(x, y, 1 - c), device_id_type=MESH).start()
        token[...] = jnp.zeros_like(token)

    return pl.pallas_call(
        body, name=name,
        out_shape=(pltpu.SemaphoreType.DMA(()), pltpu.SemaphoreType.DMA(()), pltpu.HBM(g.shape, g.dtype),
                   pltpu.HBM(g.shape, g.dtype), jax.ShapeDtypeStruct((8, LANES), F32)),
        in_specs=[HBM_SPEC, HBM_SPEC], out_specs=(SEM_SPEC, SEM_SPEC, HBM_SPEC, HBM_SPEC, VMEM_SPEC),
        input_output_aliases={0: 2, 1: 3},
        compiler_params=pltpu.CompilerParams(has_side_effects=DATAFLOW),
    )(pltpu.with_memory_space_constraint(g, pltpu.HBM), zone)


def _sibling_send_wait(send_sem, recv_sem, g, zone, after, name):
    def body(g_ref, zone_ref, s_sem, r_sem, after_ref, g_out, zone_out):
        x, y, c, _ = _place()
        cp = pltpu.make_async_remote_copy(src_ref=g_ref, dst_ref=zone_ref, send_sem=s_sem, recv_sem=r_sem,
                                          device_id=(x, y, 1 - c), device_id_type=MESH)
        cp.wait_send()
        cp.wait_recv()

    return pl.pallas_call(
        body, name=name, out_shape=(pltpu.HBM(g.shape, g.dtype), pltpu.HBM(zone.shape, zone.dtype)),
        in_specs=[HBM_SPEC, HBM_SPEC, SEM_SPEC, SEM_SPEC, ANY], out_specs=(HBM_SPEC, HBM_SPEC),
        input_output_aliases={0: 0, 1: 1},
        compiler_params=pltpu.CompilerParams(has_side_effects=DATAFLOW),
    )(g, zone, send_sem, recv_sem, after)[1]


def _reduce_finish(red, after, sc_arr):
    tag = red.names[0] if len(red.names) == 1 else "branches"
    recvs = _chips_send_wait(red.send_sems, red.recv_sems, red.sums, red.zones, after, "grads_to_chips_wait_" + tag)
    return [_sum_chips(g, l, r, sc_arr, "sum_chips_" + nm) for g, l, r, nm in zip(red.pieces, red.lands, recvs, red.names)]


def _t5_bucket(dist):
    max_exact = REL_BUCKETS // 2
    d = jnp.maximum(dist, 0)
    df = jnp.maximum(d, 1).astype(F32)
    large = max_exact + (jnp.log(df / max_exact) / math.log(REL_MAX_DIST / max_exact)
                         * (REL_BUCKETS - max_exact)).astype(jnp.int32)
    large = jnp.minimum(large, REL_BUCKETS - 1)
    return jnp.where(d < max_exact, d, large)


def _bucket_table():
    qi = jnp.arange(WINDOW)[:, None]
    si = jnp.arange(2 * WINDOW)[None, :]
    return _t5_bucket(qi + WINDOW - si)


TILE_WORDS = 8 * LANES


def _tile_rows(shape):
    return -(-math.prod(shape) // TILE_WORDS) * 8


def _rows_of(a):
    flat = a.reshape(-1).astype(F32)
    n = _tile_rows(a.shape) * LANES
    return jnp.pad(flat, (0, n - flat.shape[0])).reshape(-1, LANES)


def _pack_rows(parts):
    return jnp.concatenate([_rows_of(p) for p in parts], axis=0)


def _unpack_rows(packed, shapes):
    out, at = [], 0
    for shp in shapes:
        n, nr = math.prod(shp), _tile_rows(shp)
        out.append(packed[at:at + nr].reshape(-1)[:n].reshape(shp))
        at += nr
    return out


def kernel(x, norm_pre, w_in, rel_bias, attn_sinks, lb_logits, hgrn_norm, w_branch_attn, w_branch_hgrn, w_out, norm_post, loss_target, m_norm_pre, m_w_in, m_rel_bias, m_attn_sinks, m_lb_logits, m_hgrn_norm, m_w_branch_attn, m_w_branch_hgrn, m_w_out, m_norm_post, v_norm_pre, v_w_in, v_rel_bias, v_attn_sinks, v_lb_logits, v_hgrn_norm, v_w_branch_attn, v_w_branch_hgrn, v_w_out, v_norm_post):
    B_loc, S, D = x.shape
    T = B_loc * S
    x2 = x.reshape(T, D)
    tgt2 = loss_target.reshape(T, D)
    my_x, my_y, my_c = lax.axis_index("x"), lax.axis_index("y"), lax.axis_index("c")

    c_arr = jnp.reshape(my_c, (1,)).astype(jnp.int32)
    s_arr = jnp.reshape(2 * my_x + my_y, (1,)).astype(jnp.int32)
    sc_arr = jnp.concatenate([s_arr, c_arr])
    shard_ws = [w_in[0], w_branch_attn[0], w_branch_hgrn[0], w_out[0]]
    shard_axes = (1, 1, 1, 0)
    names = ["w_in", "w_branch_attn", "w_branch_hgrn", "w_out"]
    placed = [_cast_into_full(w, ax, s_arr, "cast_" + nm) for w, ax, nm in zip(shard_ws, shard_axes, names)]
    peer_slabs = [jnp.reshape(t, (1,)).astype(jnp.int32)
                  for t in (2 * (1 - my_x) + my_y, 2 * my_x + 1 - my_y, 2 * (1 - my_x) + 1 - my_y)]

    buckets = _bucket_table()
    onehot = (buckets.reshape(-1)[:, None] == jnp.arange(REL_BUCKETS)[None, :]).astype(F32)
    bias_tab = jnp.dot(onehot, rel_bias.astype(F32), precision=lax.Precision.HIGHEST).T.reshape(ATTN_HEADS, WINDOW, 2 * WINDOW)
    sinks_b = jnp.broadcast_to(attn_sinks[0].astype(F32)[:, None, None], (ATTN_HEADS, 8, LANES))
    lb_fn = lambda l: jnp.cumsum(jax.nn.softmax(l.astype(F32), axis=0), axis=0)[:1]
    lb, lb_vjp = jax.vjp(lb_fn, lb_logits)
    gain_h = hgrn_norm[0].reshape(1, HGRN_WIDTH)

    h, rstd = _rmsnorm_fwd(x2, norm_pre)
    gather_in = _Gather(placed[:1], [shard_ws[0].shape], shard_axes[:1], "w_in")
    gather_in.start([0, 1])
    proj = _matmul_slab(h, gather_in.fulls[0], s_arr, None, gather_in.token, "in_proj_own")
    gather_in.wait([0, 1], proj)
    gather_in.forward([0, 1])
    gather_in.start([2])
    proj = _matmul_slab(h, gather_in.fulls[0], peer_slabs[0], proj, gather_in.token, "in_proj_peer0")
    proj = _matmul_slab(h, gather_in.fulls[0], peer_slabs[1], proj, None, "in_proj_peer1")
    gather_in.wait([2], proj)
    gather_in.forward([2])
    gather_rest = _Gather(placed[1:], [w.shape for w in shard_ws[1:]], shard_axes[1:], "rest")
    gather_rest.start([0, 1, 2], gather_in.fulls[0])
    proj = _matmul_slab(h, gather_in.fulls[0], peer_slabs[2], proj, gather_rest.token, "in_proj_peer2")
    win_f = gather_in.fulls[0]
    ya, attn_probs = _attn_fwd(proj, bias_tab, sinks_b, B_loc, S)
    yh, states = _hgrn_fwd(proj, lb, gain_h, B_loc, S)
    gather_rest.wait([0, 1, 2], yh)
    gather_rest.forward([0, 1, 2])
    wba_f, wbh_f, wout_f = gather_rest.fulls
    ua = _matmul(ya, wba_f, "nn", BF16, "branch_attn_proj")
    uh, merged = _branch_proj_merge(yh, wbh_f, proj, ua)
    yv = _matmul(merged, wout_f, "nn", BF16, "out_proj")
    dy, dout, loss_p, gnpost_p = _post_loss(yv, x2, tgt2, norm_post)

    g_wout = _matmul(merged, dy, "tn", BF16, "out_proj_wgrad")
    d_ua, d_uh, dproj = _out_dgrad_merge_bwd(dy, wout_f, proj, ua, uh)
    g_wba = _matmul(ya, d_ua, "tn", BF16, "branch_attn_wgrad", slabs=N_CHIPS)
    g_wbh = _matmul(yh, d_uh, "tn", BF16, "branch_hgrn_wgrad", slabs=N_CHIPS)
    where_arr = jnp.concatenate([c_arr] + peer_slabs)
    late_pieces = [g.reshape(N_CHIPS, 2, -1, g.shape[-1]) for g in (g_wba, g_wbh, g_wout)]
    late = _reduce_start(late_pieces, _exchange_sibling_halves(late_pieces, "grads_to_sibling_branches"), names[1:], where_arr)
    d_ya = _matmul(d_ua, wba_f, "nt", BF16, "branch_attn_dgrad", after=late.token)
    d_yh = _matmul(d_uh, wbh_f, "nt", BF16, "branch_hgrn_dgrad", after=late.token)
    dproj, dbias_p, dsinks_p = _attn_bwd(proj, attn_probs, d_ya, dproj, B_loc, S)
    dproj, dlb_p, dgain_p = _hgrn_bwd(proj, states, lb, gain_h, d_yh, dproj, B_loc, S)
    g_give = _wgrad_half(h, dproj, 1 - c_arr, "in_proj_wgrad_sibling_half")
    s_sem, r_sem, g_give, zone, token = _sibling_send_start(g_give, "grads_to_sibling_start_w_in")
    g_keep = _wgrad_half(h, dproj, c_arr, "in_proj_wgrad_own_half", after=token)
    land = _sibling_send_wait(s_sem, r_sem, g_give, zone, g_keep, "grads_to_sibling_wait_w_in")
    last = _reduce_start([g_keep], [land], names[:1], where_arr)
    dh = _matmul(dproj, win_f, "nt", BF16, "in_proj_dgrad", after=last.token)
    grad_x2, gnpre_p = _rmsnorm_bwd(dh, x2, rstd, norm_pre, dout)
    halves = _reduce_finish(last, grad_x2, sc_arr) + _reduce_finish(late, grad_x2, sc_arr)

    grelb_p = jnp.dot(dbias_p.reshape(ATTN_HEADS, -1), onehot, precision=lax.Precision.HIGHEST).T
    gsinks_p = dsinks_p[:, 0, 0]
    dlb_sum = jnp.sum(dlb_p, axis=0).reshape(1, HGRN_WIDTH)
    ghn_p = jnp.sum(dgain_p, axis=0).reshape(HGRN_HEADS, HGRN_DIM)
    small_parts = [gnpre_p, gnpost_p, grelb_p, gsinks_p, dlb_sum, ghn_p, loss_p]
    small_shapes = [p.shape for p in small_parts]
    shared, pack_sum = _share_and_sum_small(halves, _pack_rows(small_parts), "grads_share_sibling")
    big_w = [w_in, w_branch_attn, w_branch_hgrn, w_out]
    big_m = [m_w_in, m_w_branch_attn, m_w_branch_hgrn, m_w_out]
    big_v = [v_w_in, v_w_branch_attn, v_w_branch_hgrn, v_w_out]
    big = {}
    for nm, gs, w, m, v in zip(names, shared, big_w, big_m, big_v):
        shp = w.shape
        g2 = gs.reshape(shp[1], shp[2])
        d, nm_, nv_, g_out = _adamw(w[0], g2, m[0], v[0], "adamw_" + nm)
        big[nm] = tuple(a.reshape(shp) for a in (g_out, d, nm_, nv_))

    gnpre, gnpost, grelb, gsinks, dlb, ghn, loss = _unpack_rows(pack_sum, small_shapes)
    (g_lb_logits,) = lb_vjp(dlb)
    small_names = ["norm_pre", "rel_bias", "attn_sinks", "lb_logits", "hgrn_norm", "norm_post"]
    small_w = [norm_pre, rel_bias, attn_sinks, lb_logits, hgrn_norm, norm_post]
    small_m = [m_norm_pre, m_rel_bias, m_attn_sinks, m_lb_logits, m_hgrn_norm, m_norm_post]
    small_v = [v_norm_pre, v_rel_bias, v_attn_sinks, v_lb_logits, v_hgrn_norm, v_norm_post]
    small_g = [gnpre.reshape(norm_pre.shape), grelb.reshape(rel_bias.shape), gsinks.reshape(attn_sinks.shape),
               g_lb_logits.reshape(lb_logits.shape), ghn.reshape(hgrn_norm.shape), gnpost.reshape(norm_post.shape)]
    shapes = [w.shape for w in small_w]
    d_s, nm_s, nv_s, _ = _adamw(_pack_rows(small_w), _pack_rows(small_g), _pack_rows(small_m), _pack_rows(small_v),
                             "adamw_small")
    small = {}
    for nm, g, d, m_, v_ in zip(small_names, small_g, _unpack_rows(d_s, shapes), _unpack_rows(nm_s, shapes),
                                _unpack_rows(nv_s, shapes)):
        small[nm] = (g, d, m_, v_)

    allw = {**big, **small}
    order = ["norm_pre", "w_in", "rel_bias", "attn_sinks", "lb_logits", "hgrn_norm", "w_branch_attn", "w_branch_hgrn",
             "w_out", "norm_post"]
    outs = [loss.reshape(()), grad_x2.reshape(B_loc, S, D)]
    for k in range(4):
        outs += [allw[nm][k] for nm in order]
    return tuple(outs)
```

```python
import math

import jax
import jax.numpy as jnp
from jax import lax
from jax.experimental import pallas as pl
from jax.experimental.pallas import tpu as pltpu

F32 = jnp.float32
BF16 = jnp.bfloat16
MESH = pl.DeviceIdType.MESH

ATTN_HEADS = 16
ATTN_KV_HEADS = 4
HEAD_DIM = 64
GROUP = ATTN_HEADS // ATTN_KV_HEADS
WINDOW = 128
ATTN_WIDTH = ATTN_HEADS * HEAD_DIM
KV_WIDTH = ATTN_KV_HEADS * HEAD_DIM
HGRN_HEADS = 8
HGRN_DIM = 128
HGRN_WIDTH = HGRN_HEADS * HGRN_DIM
CHUNK = 64
SUB = 16
NSUB = CHUNK // SUB
REL_BUCKETS = 32
REL_MAX_DIST = 128
NORM_EPS = 1e-6
ADAM_LR = 0.001
ADAM_B1 = 0.9
ADAM_B2 = 0.999
ADAM_EPS = 1e-08
ADAM_WD = 0.01
ADAM_STEP = 10
LANES = 128
N_CHIPS = 4
N_DEV = 8
VMEM_LIMIT = 48 * 1024 * 1024
MATMUL_OPERAND_BYTES = 34 * 1024 * 1024
MATMUL_VMEM_BYTES = 44 * 1024 * 1024

OFF_AQ = 0
OFF_AK = OFF_AQ + ATTN_WIDTH
OFF_AV = OFF_AK + KV_WIDTH
OFF_AG = OFF_AV + KV_WIDTH
OFF_HQ = OFF_AG + ATTN_WIDTH
OFF_HF = OFF_HQ + HGRN_WIDTH
OFF_HI = OFF_HF + HGRN_WIDTH
OFF_HG = OFF_HI + HGRN_WIDTH
OFF_GA = OFF_HG + HGRN_WIDTH

NT_DIMS = (((1,), (1,)), ((), ()))
TN_DIMS = (((0,), (0,)), ((), ()))
NN_DIMS = (((1,), (0,)), ((), ()))


def _pick(n, cands):
    for c in cands:
        if n % c == 0:
            return c
    raise ValueError(f"no tile for {n} in {cands}")


def _params(sem):
    return pltpu.CompilerParams(dimension_semantics=sem, vmem_limit_bytes=VMEM_LIMIT)


def _bdot(a, b, dims=NN_DIMS):
    return lax.dot_general(a.astype(BF16), b.astype(BF16), dims, preferred_element_type=F32)


def _matmul(a, b, mode, out_dtype, name, slabs=1, after=None):
    if mode == "nn":
        (M, K), (K2, N) = a.shape, b.shape
    elif mode == "nt":
        (M, K), (N, K2) = a.shape, b.shape
    else:
        (K, M), (K2, N) = a.shape, b.shape
    assert K == K2
    nslab = N // slabs
    tm = _pick(M, (1024, 512, 256, 128))
    out_bytes = jnp.dtype(out_dtype).itemsize
    choices = []
    for tn in (2688, 1024, 896, 512, 384, 256, 128):
        for tk in (4096, 3584, 2048, 1792, 1536, 1024, 512, 256, 128):
            acc = 0 if tk == K else 4 * tm * tn
            if (nslab % tn == 0 and K % tk == 0 and 4 * tk * (tm + tn) <= MATMUL_OPERAND_BYTES
                    and 4 * tk * (tm + tn) + 2 * out_bytes * tm * tn + acc <= MATMUL_VMEM_BYTES):
                choices.append((K // tk > 1, -tn, tn, tk))
                break
    _, _, tn, tk = min(choices)
    nk = K // tk
    per = nslab // tn
    dims = {"nn": NN_DIMS, "nt": NT_DIMS, "tn": TN_DIMS}[mode]

    n_in = 2 if after is None else 3

    def body(*refs):
        a_ref, b_ref, o_ref, acc = refs[0], refs[1], refs[n_in], refs[n_in + 1:]
        part = lax.dot_general(a_ref[...], b_ref[...], dims, preferred_element_type=F32)
        if nk == 1:
            o_ref[...] = part.astype(o_ref.dtype)
            return
        acc_ref, = acc
        k = pl.program_id(2)

        @pl.when(k == 0)
        def _():
            acc_ref[...] = part

        @pl.when((k > 0) & (k < nk - 1))
        def _():
            acc_ref[...] += part

        @pl.when(k == nk - 1)
        def _():
            o_ref[...] = (acc_ref[...] + part).astype(o_ref.dtype)

    if mode == "tn":
        a_spec = pl.BlockSpec((tk, tm), lambda i, j, k: (k, i))
    else:
        a_spec = pl.BlockSpec((tm, tk), lambda i, j, k: (i, k))
    if mode == "nt":
        b_spec = pl.BlockSpec((tn, tk), lambda i, j, k: (j, k))
    else:
        b_spec = pl.BlockSpec((tk, tn), lambda i, j, k: (k, j))
    if slabs == 1:
        o_shape = jax.ShapeDtypeStruct((M, N), out_dtype)
        o_spec = pl.BlockSpec((tm, tn), lambda i, j, k: (i, j))
    else:
        o_shape = jax.ShapeDtypeStruct((slabs, M, nslab), out_dtype)
        o_spec = pl.BlockSpec((None, tm, tn), lambda i, j, k: (j // per, i, j % per))
    return pl.pallas_call(
        body, name=name, grid=(M // tm, N // tn, nk), in_specs=[a_spec, b_spec] + ([] if after is None else [ANY]),
        out_specs=o_spec, out_shape=o_shape,
        scratch_shapes=[pltpu.VMEM((tm, tn), F32)] if nk > 1 else [],
        compiler_params=_params(("parallel", "parallel", "arbitrary")),
    )(*((a, b) if after is None else (a, b, after)))


def _rmsnorm_fwd(x2, gain):
    T, D = x2.shape
    tr = _pick(T, (256, 128))

    def body(x_ref, g_ref, h_ref, r_ref):
        xv = x_ref[...]
        r = lax.rsqrt(jnp.mean(xv * xv, axis=-1, keepdims=True) + NORM_EPS)
        h_ref[...] = (xv * r * g_ref[...]).astype(BF16)
        r_ref[...] = r

    return pl.pallas_call(
        body, name="rmsnorm_pre_fwd", grid=(T // tr,),
        in_specs=[pl.BlockSpec((tr, D), lambda i: (i, 0)), pl.BlockSpec((1, D), lambda i: (0, 0))],
        out_specs=[pl.BlockSpec((tr, D), lambda i: (i, 0)), pl.BlockSpec((tr, 1), lambda i: (i, 0))],
        out_shape=[jax.ShapeDtypeStruct((T, D), BF16), jax.ShapeDtypeStruct((T, 1), F32)],
        compiler_params=_params(("parallel",)),
    )(x2, gain)


def _rmsnorm_bwd(dh, x2, rstd, gain, dout):
    T, D = x2.shape
    tr = _pick(T, (256, 128))

    def body(dh_ref, x_ref, r_ref, g_ref, do_ref, gx_ref, gg_ref):
        @pl.when(pl.program_id(0) == 0)
        def _():
            gg_ref[...] = jnp.zeros_like(gg_ref)

        n = x_ref[...] * r_ref[...]
        dhv = dh_ref[...].astype(F32)
        dn = dhv * g_ref[...]
        gx_ref[...] = do_ref[...] + r_ref[...] * (dn - n * jnp.mean(dn * n, axis=-1, keepdims=True))
        gg_ref[...] += jnp.sum(dhv * n, axis=0, keepdims=True)

    row = pl.BlockSpec((tr, D), lambda i: (i, 0))
    vec = pl.BlockSpec((1, D), lambda i: (0, 0))
    return pl.pallas_call(
        body, name="rmsnorm_pre_bwd", grid=(T // tr,),
        in_specs=[row, row, pl.BlockSpec((tr, 1), lambda i: (i, 0)), vec, row],
        out_specs=[row, vec],
        out_shape=[jax.ShapeDtypeStruct((T, D), F32), jax.ShapeDtypeStruct((1, D), F32)],
        compiler_params=_params(("arbitrary",)),
    )(dh, x2, rstd, gain, dout)


def _post_loss(yv, x2, tgt2, gain):
    T, D = x2.shape
    tr = _pick(T, (256, 128))

    def body(y_ref, x_ref, t_ref, g_ref, dy_ref, do_ref, loss_ref, gg_ref):
        @pl.when(pl.program_id(0) == 0)
        def _():
            gg_ref[...] = jnp.zeros_like(gg_ref)
            loss_ref[...] = jnp.zeros_like(loss_ref)

        yv_ = y_ref[...].astype(F32)
        r = lax.rsqrt(jnp.mean(yv_ * yv_, axis=-1, keepdims=True) + NORM_EPS)
        n = yv_ * r
        e = (x_ref[...] + n * g_ref[...]) - t_ref[...]
        loss_ref[...] += 0.5 * jnp.sum(jnp.mean(e * e, axis=-1, keepdims=True), axis=0, keepdims=True)
        dz = e / D
        do_ref[...] = dz
        gg_ref[...] += jnp.sum(dz * n, axis=0, keepdims=True)
        dn = dz * g_ref[...]
        dy_ref[...] = (r * (dn - n * jnp.mean(dn * n, axis=-1, keepdims=True))).astype(BF16)

    row = pl.BlockSpec((tr, D), lambda i: (i, 0))
    vec = pl.BlockSpec((1, D), lambda i: (0, 0))
    return pl.pallas_call(
        body, name="post_norm_loss", grid=(T // tr,),
        in_specs=[row, row, row, vec],
        out_specs=[row, row, pl.BlockSpec((1, 1), lambda i: (0, 0)), vec],
        out_shape=[jax.ShapeDtypeStruct((T, D), BF16), jax.ShapeDtypeStruct((T, D), F32),
                   jax.ShapeDtypeStruct((1, 1), F32), jax.ShapeDtypeStruct((1, D), F32)],
        compiler_params=_params(("arbitrary",)),
    )(yv, x2, tgt2, gain)


def _window(rows, cols, at):
    return pl.BlockSpec((pl.Element(rows), pl.Element(cols)), at)


def _gate_windows(tm, tn, D):
    return [_window(tm, tn, lambda i, j: (i * tm, pl.multiple_of(OFF_GA + j * tn, LANES))),
            _window(tm, tn, lambda i, j: (i * tm, pl.multiple_of(OFF_GA + D + j * tn, LANES)))]


def _branch_proj_merge(yh, wbh, proj, ua):
    T, K = yh.shape
    D = wbh.shape[1]
    tm, tn = _pick(T, (512, 256, 128)), _pick(D, (1024, 512, 256))

    def body(a_ref, b_ref, ga_ref, gh_ref, ua_ref, uh_ref, m_ref):
        uh = jnp.dot(a_ref[...], b_ref[...], preferred_element_type=F32).astype(BF16)
        uh_ref[...] = uh
        m_ref[...] = (jax.nn.sigmoid(ga_ref[...]) * ua_ref[...].astype(F32)
                      + jax.nn.sigmoid(gh_ref[...]) * uh.astype(F32)).astype(BF16)

    blk = pl.BlockSpec((tm, tn), lambda i, j: (i, j))
    o = jax.ShapeDtypeStruct((T, D), BF16)
    return pl.pallas_call(
        body, name="branch_hgrn_proj_merge", grid=(T // tm, D // tn),
        in_specs=[pl.BlockSpec((tm, K), lambda i, j: (i, 0)), pl.BlockSpec((K, tn), lambda i, j: (0, j))]
        + _gate_windows(tm, tn, D) + [blk],
        out_specs=[blk, blk], out_shape=[o, o],
        compiler_params=_params(("parallel", "arbitrary")),
    )(yh, wbh, proj, proj, ua)


def _out_dgrad_merge_bwd(dy, wout, proj, ua, uh):
    T, K = dy.shape
    D = wout.shape[0]
    tm, tn = _pick(T, (512, 256, 128)), _pick(D, (1024, 512, 256))
    nj = D // tn

    def body(a_ref, b_ref, ga_ref, gh_ref, ua_ref, uh_ref, dua_ref, duh_ref, dproj_ref):
        d = lax.dot_general(a_ref[...], b_ref[...], NT_DIMS, preferred_element_type=F32)
        sa = jax.nn.sigmoid(ga_ref[...])
        sh = jax.nn.sigmoid(gh_ref[...])
        dua_ref[...] = (d * sa).astype(BF16)
        duh_ref[...] = (d * sh).astype(BF16)
        dga = (d * ua_ref[...].astype(F32) * sa * (1.0 - sa)).astype(BF16)
        dgh = (d * uh_ref[...].astype(F32) * sh * (1.0 - sh)).astype(BF16)
        for jj in range(nj):
            @pl.when(pl.program_id(1) == jj)
            def _():
                dproj_ref[:, jj * tn:(jj + 1) * tn] = dga
                dproj_ref[:, D + jj * tn:D + (jj + 1) * tn] = dgh

    blk = pl.BlockSpec((tm, tn), lambda i, j: (i, j))
    o = jax.ShapeDtypeStruct((T, D), BF16)
    return pl.pallas_call(
        body, name="out_proj_dgrad_merge_bwd", grid=(T // tm, nj),
        in_specs=[pl.BlockSpec((tm, K), lambda i, j: (i, 0)), pl.BlockSpec((tn, K), lambda i, j: (j, 0))]
        + _gate_windows(tm, tn, D) + [blk, blk],
        out_specs=[blk, blk, _window(tm, 2 * D, lambda i, j: (i * tm, OFF_GA))],
        out_shape=[o, o, jax.ShapeDtypeStruct((T, proj.shape[1]), BF16)],
        compiler_params=_params(("parallel", "arbitrary")),
    )(dy, wout, proj, proj, ua, uh)


KV_PAIR = 2
PAIR_HEADS = KV_PAIR * GROUP


def _attn_mask(n):
    qi = lax.broadcasted_iota(jnp.int32, (WINDOW, 2 * WINDOW), 0)
    si = lax.broadcasted_iota(jnp.int32, (WINDOW, 2 * WINDOW), 1)
    dist = qi + WINDOW - si
    return (dist >= 0) & (dist < WINDOW) & ((si >= WINDOW) | (n > 0))


def _first_key_column(shape):
    return lax.broadcasted_iota(jnp.int32, shape, 1) == 0


def _attn_group_fwd(mask, q, k, v, ag, bias, sink):
    mask = jnp.concatenate([mask] * GROUP, axis=0)
    s = lax.dot_general(q.astype(BF16), k.astype(BF16), NT_DIMS, preferred_element_type=F32) * (HEAD_DIM ** -0.5)
    s = jnp.where(mask, s + bias, -1e30)
    m = jnp.maximum(jnp.max(s, axis=-1, keepdims=True), sink)
    p = jnp.exp(s - m)
    e_sink = jnp.exp(sink - m)
    den = jnp.sum(p, axis=-1, keepdims=True) + e_sink
    pb = p.astype(BF16)
    o = jnp.dot(pb, v.astype(BF16), preferred_element_type=F32)
    return o * (jax.nn.silu(ag) / den), pb, e_sink.astype(BF16)


def _attn_group_bwd(q, k, v, ag, kept, dout):
    e_sink = kept[:, 0:1].astype(F32)
    pb = jnp.where(_first_key_column(kept.shape), jnp.zeros_like(kept), kept)
    pf = pb.astype(F32)
    rden = 1.0 / (jnp.sum(pf, axis=-1, keepdims=True) + e_sink)
    qb, kb, vb = q.astype(BF16), k.astype(BF16), v.astype(BF16)
    o = jnp.dot(pb, vb, preferred_element_type=F32) * rden
    sg = jax.nn.sigmoid(ag)
    d_o = dout * (ag * sg)
    dag = dout * o * (sg * (1.0 + ag * (1.0 - sg)))
    d_row = jnp.sum(d_o * o, axis=-1, keepdims=True)
    ds = (pf * rden) * (lax.dot_general(d_o.astype(BF16), vb, NT_DIMS, preferred_element_type=F32) - d_row)
    dsb = ds.astype(BF16)
    dq = jnp.dot(dsb, kb, preferred_element_type=F32) * (HEAD_DIM ** -0.5)
    dk = lax.dot_general(dsb, qb, TN_DIMS, preferred_element_type=F32) * (HEAD_DIM ** -0.5)
    dv = lax.dot_general(pb, (d_o * rden).astype(BF16), TN_DIMS, preferred_element_type=F32)
    return dq, dk, dv, dag, ds, -(e_sink * rden * d_row)


def _attn_operands(q_ref, kc_ref, kp_ref, vc_ref, vp_ref, ag_ref, bias_ref, sink_ref, j):
    lo, hi = j * HEAD_DIM, (j + 1) * HEAD_DIM
    k = jnp.concatenate([kp_ref[:, lo:hi], kc_ref[:, lo:hi]], axis=0)
    v = jnp.concatenate([vp_ref[:, lo:hi], vc_ref[:, lo:hi]], axis=0)
    heads = [j * GROUP + g for g in range(GROUP)]
    q = jnp.concatenate([q_ref[:, h * HEAD_DIM:(h + 1) * HEAD_DIM] for h in heads], axis=0)
    ag = jnp.concatenate([ag_ref[:, h * HEAD_DIM:(h + 1) * HEAD_DIM] for h in heads], axis=0)
    bias = jnp.concatenate([bias_ref[h] for h in heads], axis=0)
    sink = jnp.concatenate([jnp.broadcast_to(sink_ref[h, 0:1, 0:1], (WINDOW, 1)) for h in heads], axis=0)
    return q, k, v, ag, bias, sink


def _attn_fwd(proj, bias_tab, sinks_b, B_loc, S):
    T = B_loc * S
    nb = S // WINDOW

    def at(off, back=0):
        return lambda b, n: ((b * nb + jnp.maximum(n - back, 0)) * WINDOW, off)

    in_specs = [_window(WINDOW, ATTN_WIDTH, at(OFF_AQ)), _window(WINDOW, KV_WIDTH, at(OFF_AK)),
                _window(WINDOW, KV_WIDTH, at(OFF_AK, 1)), _window(WINDOW, KV_WIDTH, at(OFF_AV)),
                _window(WINDOW, KV_WIDTH, at(OFF_AV, 1)), _window(WINDOW, ATTN_WIDTH, at(OFF_AG)),
                pl.BlockSpec((ATTN_HEADS, WINDOW, 2 * WINDOW), lambda b, n: (0, 0, 0)),
                pl.BlockSpec((ATTN_HEADS, 8, LANES), lambda b, n: (0, 0, 0))]

    def body(q_ref, kc_ref, kp_ref, vc_ref, vp_ref, ag_ref, bias_ref, sink_ref, ya_ref, p_ref):
        mask = _attn_mask(pl.program_id(1))
        for j in range(ATTN_KV_HEADS):
            out, probs, e_sink = _attn_group_fwd(
                mask, *_attn_operands(q_ref, kc_ref, kp_ref, vc_ref, vp_ref, ag_ref, bias_ref, sink_ref, j))
            for g in range(GROUP):
                h = j * GROUP + g
                blk = slice(g * WINDOW, (g + 1) * WINDOW)
                ya_ref[:, h * HEAD_DIM:(h + 1) * HEAD_DIM] = out[blk].astype(BF16)
                p_ref[:, h * 2 * WINDOW:(h + 1) * 2 * WINDOW] = probs[blk]
                p_ref[:, h * 2 * WINDOW:h * 2 * WINDOW + 1] = e_sink[blk]

    return pl.pallas_call(
        body, name="attn_fwd", grid=(B_loc, nb), in_specs=in_specs,
        out_specs=[pl.BlockSpec((WINDOW, ATTN_WIDTH), lambda b, n: (b * nb + n, 0)),
                   pl.BlockSpec((WINDOW, ATTN_HEADS * 2 * WINDOW), lambda b, n: (b * nb + n, 0))],
        out_shape=[jax.ShapeDtypeStruct((T, ATTN_WIDTH), BF16),
                   jax.ShapeDtypeStruct((T, ATTN_HEADS * 2 * WINDOW), BF16)],
        compiler_params=_params(("parallel", "parallel")),
    )(proj, proj, proj, proj, proj, proj, bias_tab, sinks_b)


def _attn_bwd(proj, probs, dya, dproj, B_loc, S):
    nb = S // WINDOW
    n_pairs = ATTN_KV_HEADS // KV_PAIR
    qw, kw = PAIR_HEADS * HEAD_DIM, KV_PAIR * HEAD_DIM

    def at(off, width, back=0):
        return lambda b, i, p: ((b * nb + jnp.maximum(nb - 1 - i - back, 0)) * WINDOW,
                                pl.multiple_of(off + p * width, LANES))

    in_specs = [_window(WINDOW, qw, at(OFF_AQ, qw)), _window(WINDOW, kw, at(OFF_AK, kw)),
                _window(WINDOW, kw, at(OFF_AK, kw, 1)), _window(WINDOW, kw, at(OFF_AV, kw)),
                _window(WINDOW, kw, at(OFF_AV, kw, 1)), _window(WINDOW, qw, at(OFF_AG, qw)),
                pl.BlockSpec((WINDOW, PAIR_HEADS * 2 * WINDOW), lambda b, i, p: (b * nb + nb - 1 - i, p)),
                pl.BlockSpec((WINDOW, qw), lambda b, i, p: (b * nb + nb - 1 - i, p)), ANY]

    def body(q_ref, kc_ref, kp_ref, vc_ref, vp_ref, ag_ref, p_ref, dya_ref, dproj_in,
             dproj_ref, dbias_ref, dsink_ref, dkc_ref, dvc_ref):
        b, i, p = pl.program_id(0), pl.program_id(1), pl.program_id(2)

        @pl.when((b == 0) & (i == 0) & (p == 0))
        def _():
            dbias_ref[...] = jnp.zeros_like(dbias_ref)
            dsink_ref[...] = jnp.zeros_like(dsink_ref)

        @pl.when(i == 0)
        def _():
            dkc_ref[p] = jnp.zeros((WINDOW, kw), F32)
            dvc_ref[p] = jnp.zeros((WINDOW, kw), F32)

        dk_carry, dv_carry = dkc_ref[p], dvc_ref[p]
        dqs, dags, dbiases, dsinks, dks, dvs = [], [], [], [], [], []
        for j in range(KV_PAIR):
            heads = [j * GROUP + g for g in range(GROUP)]
            lo, hi = j * HEAD_DIM, (j + 1) * HEAD_DIM
            stack = lambda parts: jnp.concatenate(parts, axis=0)
            k = stack([kp_ref[:, lo:hi], kc_ref[:, lo:hi]])
            v = stack([vp_ref[:, lo:hi], vc_ref[:, lo:hi]])
            q = stack([q_ref[:, h * HEAD_DIM:(h + 1) * HEAD_DIM] for h in heads])
            ag = stack([ag_ref[:, h * HEAD_DIM:(h + 1) * HEAD_DIM] for h in heads])
            dout = stack([dya_ref[:, h * HEAD_DIM:(h + 1) * HEAD_DIM].astype(F32) for h in heads])
            kept = stack([p_ref[:, h * 2 * WINDOW:(h + 1) * 2 * WINDOW] for h in heads])
            dq, dk, dv, dag, dbias, dsink = _attn_group_bwd(q, k, v, ag, kept, dout)
            dks.append((dk[WINDOW:] + dk_carry[:, lo:hi]).astype(BF16))
            dvs.append((dv[WINDOW:] + dv_carry[:, lo:hi]).astype(BF16))
            dkc_ref[p, :, lo:hi] = dk[:WINDOW]
            dvc_ref[p, :, lo:hi] = dv[:WINDOW]
            for g in range(GROUP):
                blk = slice(g * WINDOW, (g + 1) * WINDOW)
                dqs.append(dq[blk].astype(BF16))
                dags.append(dag[blk].astype(BF16))
                dbiases.append(dbias[blk])
                dsinks.append(jnp.broadcast_to(jnp.sum(dsink[blk], axis=0, keepdims=True), (8, LANES)))

        for pair in range(n_pairs):
            @pl.when(p == pair)
            def _():
                for j in range(KV_PAIR):
                    col = (pair * KV_PAIR + j) * HEAD_DIM
                    dproj_ref[:, OFF_AK + col:OFF_AK + col + HEAD_DIM] = dks[j]
                    dproj_ref[:, OFF_AV + col:OFF_AV + col + HEAD_DIM] = dvs[j]
                for hh in range(PAIR_HEADS):
                    h = pair * PAIR_HEADS + hh
                    dproj_ref[:, OFF_AQ + h * HEAD_DIM:OFF_AQ + (h + 1) * HEAD_DIM] = dqs[hh]
                    dproj_ref[:, OFF_AG + h * HEAD_DIM:OFF_AG + (h + 1) * HEAD_DIM] = dags[hh]
                    dbias_ref[h] += dbiases[hh]
                    dsink_ref[h] += dsinks[hh]

    return pl.pallas_call(
        body, name="attn_bwd", grid=(B_loc, nb, n_pairs), in_specs=in_specs,
        out_specs=[_window(WINDOW, OFF_HQ, lambda b, i, p: ((b * nb + nb - 1 - i) * WINDOW, 0)),
                   pl.BlockSpec((ATTN_HEADS, WINDOW, 2 * WINDOW), lambda b, i, p: (0, 0, 0)),
                   pl.BlockSpec((ATTN_HEADS, 8, LANES), lambda b, i, p: (0, 0, 0))],
        out_shape=[jax.ShapeDtypeStruct(dproj.shape, dproj.dtype),
                   jax.ShapeDtypeStruct((ATTN_HEADS, WINDOW, 2 * WINDOW), F32),
                   jax.ShapeDtypeStruct((ATTN_HEADS, 8, LANES), F32)],
        input_output_aliases={8: 0},
        scratch_shapes=[pltpu.VMEM((n_pairs, WINDOW, kw), F32), pltpu.VMEM((n_pairs, WINDOW, kw), F32)],
        compiler_params=_params(("arbitrary", "arbitrary", "arbitrary")),
    )(proj, proj, proj, proj, proj, proj, probs, dya, dproj)


class _HgrnPre:
    def __init__(self, fr, qr, lb, g_scr):
        t = lax.broadcasted_iota(jnp.int32, (CHUNK, CHUNK), 0)
        s = lax.broadcasted_iota(jnp.int32, (CHUNK, CHUNK), 1)
        self.sg = jax.nn.sigmoid(fr)
        self.f = lb + (1.0 - lb) * self.sg
        g = jnp.dot((t >= s).astype(F32), jnp.log(self.f), precision=lax.Precision.HIGHEST, preferred_element_type=F32)
        g_scr[...] = g
        self.g = g
        self.row = lax.broadcasted_iota(jnp.int32, g.shape, 0)
        self.refs = [jnp.zeros((1, g.shape[1]), F32)] + [g_scr[pl.ds(i * SUB - 1, 1), :] for i in range(1, NSUB)]
        self.gend = g_scr[pl.ds(CHUNK - 1, 1), :]
        refrow = jnp.zeros_like(g)
        for i in range(1, NSUB):
            refrow = jnp.where(self.row >= i * SUB, self.refs[i], refrow)
        self.sigq = jax.nn.sigmoid(qr)
        self.qs = qr * self.sigq
        self.k = 1.0 - self.f
        self.eg = jnp.exp(g)
        self.eqd = jnp.exp(g - refrow)
        self.ekd = [jnp.exp(jnp.where(self.row < (i + 1) * SUB, self.refs[i] - g, 0.0)) for i in range(NSUB)]
        self.ekdec = jnp.exp(self.gend - g)
        self.qg = self.qs * self.eg
        self.qd = self.qs * self.eqd
        self.kd = [self.k * e for e in self.ekd]
        self.kdec = self.k * self.ekdec
        self.egend = jnp.exp(self.gend)


def _hgrn_pair_mask():
    t = lax.broadcasted_iota(jnp.int32, (CHUNK, NSUB * CHUNK), 0)
    col = lax.broadcasted_iota(jnp.int32, (CHUNK, NSUB * CHUNK), 1)
    return ((t // SUB) == (col // CHUNK)) & ((col % CHUNK) <= t)


def _hgrn_head_out(p, lanes, state_t, v, mask):
    qg, qd = p.qg[:, lanes], p.qd[:, lanes]
    kall = jnp.concatenate([kd[:, lanes] for kd in p.kd], axis=0)
    vst = jnp.concatenate([v] * NSUB, axis=0)
    am = jnp.where(mask, _bdot(qd, kall, NT_DIMS), 0.0)
    o = _bdot(qg, state_t, NT_DIMS) + _bdot(am, vst)
    return o, (qg, qd, kall, am, vst)


def _hgrn_fwd(proj, lb, gain, B_loc, S):
    T = B_loc * S
    nc = S // CHUNK
    nh = HGRN_HEADS

    def at(off):
        return lambda b, n: ((b * nc + n) * CHUNK, off)

    vec = pl.BlockSpec((1, HGRN_WIDTH), lambda b, n: (0, 0))

    def body(q_ref, f_ref, v_ref, hg_ref, lb_ref, gain_ref, yh_ref, st_ref, state_scr, g_scr):
        @pl.when(pl.program_id(1) == 0)
        def _():
            state_scr[...] = jnp.zeros_like(state_scr)

        p = _HgrnPre(f_ref[...], q_ref[...], lb_ref[...], g_scr)
        v = v_ref[...]
        gate = gain_ref[...] * jax.nn.silu(hg_ref[...])
        mask = _hgrn_pair_mask()
        for hd in range(nh):
            lanes = slice(hd * HGRN_DIM, (hd + 1) * HGRN_DIM)
            st = state_scr[hd]
            st_ref[hd] = st
            o, _ = _hgrn_head_out(p, lanes, st, v[:, lanes], mask)
            rs = lax.rsqrt(jnp.mean(o * o, axis=-1, keepdims=True) + NORM_EPS)
            yh_ref[:, lanes] = (o * rs * gate[:, lanes]).astype(BF16)
            state_scr[hd] = st * p.egend[:, lanes] + _bdot(v[:, lanes], p.kdec[:, lanes], TN_DIMS)

    return pl.pallas_call(
        body, name="hgrn_fwd", grid=(B_loc, nc),
        in_specs=[_window(CHUNK, HGRN_WIDTH, at(OFF_HQ)), _window(CHUNK, HGRN_WIDTH, at(OFF_HF)),
                  _window(CHUNK, HGRN_WIDTH, at(OFF_HI)), _window(CHUNK, HGRN_WIDTH, at(OFF_HG)), vec, vec],
        out_specs=[pl.BlockSpec((CHUNK, HGRN_WIDTH), lambda b, n: (b * nc + n, 0)),
                   pl.BlockSpec((None, None, nh, HGRN_DIM, HGRN_DIM), lambda b, n: (b, n, 0, 0, 0))],
        out_shape=[jax.ShapeDtypeStruct((T, HGRN_WIDTH), BF16),
                   jax.ShapeDtypeStruct((B_loc, nc, nh, HGRN_DIM, HGRN_DIM), F32)],
        scratch_shapes=[pltpu.VMEM((nh, HGRN_DIM, HGRN_DIM), F32), pltpu.VMEM((CHUNK, HGRN_WIDTH), F32)],
        compiler_params=_params(("parallel", "arbitrary")),
    )(proj, proj, proj, proj, lb, gain)


def _hgrn_bwd(proj, states, lb, gain, dyh, dproj, B_loc, S):
    nc = S // CHUNK
    nh = HGRN_HEADS

    def at(off):
        return lambda b, i: ((b * nc + nc - 1 - i) * CHUNK, off)

    vec = pl.BlockSpec((1, HGRN_WIDTH), lambda b, i: (0, 0))
    in_specs = [_window(CHUNK, HGRN_WIDTH, at(OFF_HQ)), _window(CHUNK, HGRN_WIDTH, at(OFF_HF)),
                _window(CHUNK, HGRN_WIDTH, at(OFF_HI)), _window(CHUNK, HGRN_WIDTH, at(OFF_HG)), vec, vec,
                pl.BlockSpec((None, None, nh, HGRN_DIM, HGRN_DIM), lambda b, i: (b, nc - 1 - i, 0, 0, 0)),
                pl.BlockSpec((CHUNK, HGRN_WIDTH), lambda b, i: (b * nc + nc - 1 - i, 0)), ANY]
    acc_spec = pl.BlockSpec((None, 1, HGRN_WIDTH), lambda b, i: (b, 0, 0))

    def body(q_ref, f_ref, v_ref, hg_ref, lb_ref, gain_ref, st_ref, dyh_ref, dproj_in,
             dproj_ref, dlb_ref, dgain_ref, dstate_scr, g_scr, dg_scr):
        dq_ref, df_ref, dv_ref, dhg_ref = [dproj_ref.at[:, pl.ds(k * HGRN_WIDTH, HGRN_WIDTH)] for k in range(4)]

        @pl.when(pl.program_id(1) == 0)
        def _():
            dstate_scr[...] = jnp.zeros_like(dstate_scr)
            dlb_ref[...] = jnp.zeros_like(dlb_ref)
            dgain_ref[...] = jnp.zeros_like(dgain_ref)

        qr, lb, gain, hg, v = q_ref[...], lb_ref[...], gain_ref[...], hg_ref[...], v_ref[...]
        p = _HgrnPre(f_ref[...], qr, lb, g_scr)
        sgh = jax.nn.sigmoid(hg)
        sil = hg * sgh
        dy = dyh_ref[...].astype(F32)
        mask = _hgrn_pair_mask()
        dqg, dqd, dkdec, dv, dhg, dgend, dgain = [], [], [], [], [], [], []
        dkd = [[] for _ in range(NSUB)]
        heads = [slice(hd * HGRN_DIM, (hd + 1) * HGRN_DIM) for hd in range(nh)]
        sts = [st_ref[hd] for hd in range(nh)]
        dnews = [dstate_scr[hd] for hd in range(nh)]
        fwd = [_hgrn_head_out(p, lanes, st, v[:, lanes], mask) for lanes, st in zip(heads, sts)]
        for lanes, st, dnew in zip(heads, sts, dnews):
            dkdec_h = _bdot(v[:, lanes], dnew)
            dkdec.append(dkdec_h)
            dgend.append(jnp.sum(dkdec_h * p.kdec[:, lanes], axis=0, keepdims=True)
                         + jnp.sum(dnew * st, axis=0, keepdims=True) * p.egend[:, lanes])
        dos = []
        gate_grad = sgh * (1.0 + hg * (1.0 - sgh))
        for lanes, (o, _) in zip(heads, fwd):
            rs = lax.rsqrt(jnp.mean(o * o, axis=-1, keepdims=True) + NORM_EPS)
            n = o * rs
            dyn = dy[:, lanes] * n
            dgain.append(jnp.sum(dyn * sil[:, lanes], axis=0, keepdims=True))
            dhg.append(dyn * gain[:, lanes] * gate_grad[:, lanes])
            dn = dy[:, lanes] * gain[:, lanes] * sil[:, lanes]
            dos.append(rs * (dn - n * jnp.mean(dn * n, axis=-1, keepdims=True)))
        drs = []
        for hd, (lanes, st, dnew, do, (_, (qg, qd, kall, am, vst))) in enumerate(zip(heads, sts, dnews, dos, fwd)):
            dqg.append(_bdot(do, st))
            dstate_scr[hd] = _bdot(do, qg, TN_DIMS) + dnew * p.egend[:, lanes]
            drs.append(jnp.where(mask, _bdot(do, vst, NT_DIMS), 0.0))
            dvst = _bdot(am, do, TN_DIMS)
            dv.append(sum(dvst[i * CHUNK:(i + 1) * CHUNK] for i in range(NSUB)) + _bdot(p.kdec[:, lanes], dnew, NT_DIMS))
        for dr, (_, (qg, qd, kall, am, vst)) in zip(drs, fwd):
            dqd.append(_bdot(dr, kall))
            dkall = _bdot(dr, qd, TN_DIMS)
            for i in range(NSUB):
                dkd[i].append(dkall[i * CHUNK:(i + 1) * CHUNK])

        wide = lambda parts: jnp.concatenate(parts, axis=1)
        dqg, dqd, dkdec = wide(dqg), wide(dqd), wide(dkdec)
        t2 = dqd * p.qd
        dg = dqg * p.qg + t2 - dkdec * p.kdec
        dk = dkdec * p.ekdec
        dg_scr[...] = jnp.zeros_like(dg_scr)
        for i in range(NSUB):
            dkd_i = wide(dkd[i])
            tk = jnp.where(p.row < (i + 1) * SUB, dkd_i * p.kd[i], 0.0)
            dg = dg - tk
            dk = dk + dkd_i * p.ekd[i]
            if i >= 1:
                in_blk = (p.row >= i * SUB) & (p.row < (i + 1) * SUB)
                dg_scr[pl.ds(i * SUB - 1, 1), :] = (jnp.sum(tk, axis=0, keepdims=True)
                                                    - jnp.sum(jnp.where(in_blk, t2, 0.0), axis=0, keepdims=True))
        dg_scr[pl.ds(CHUNK - 1, 1), :] = wide(dgend)
        t = lax.broadcasted_iota(jnp.int32, (CHUNK, CHUNK), 0)
        s = lax.broadcasted_iota(jnp.int32, (CHUNK, CHUNK), 1)
        dlogf = jnp.dot((t <= s).astype(F32), dg + dg_scr[...], precision=lax.Precision.HIGHEST, preferred_element_type=F32)
        df = dlogf / p.f - dk
        df_ref[...] = (df * (1.0 - lb) * p.sg * (1.0 - p.sg)).astype(BF16)
        dlb_ref[...] += jnp.sum(df * (1.0 - p.sg), axis=0, keepdims=True)
        dq_ref[...] = ((dqg * p.eg + dqd * p.eqd) * p.sigq * (1.0 + qr * (1.0 - p.sigq))).astype(BF16)
        dv_ref[...] = wide(dv).astype(BF16)
        dhg_ref[...] = wide(dhg).astype(BF16)
        dgain_ref[...] += wide(dgain)

    acc = jax.ShapeDtypeStruct((B_loc, 1, HGRN_WIDTH), F32)
    return pl.pallas_call(
        body, name="hgrn_bwd", grid=(B_loc, nc), in_specs=in_specs,
        out_specs=[_window(CHUNK, 4 * HGRN_WIDTH, at(OFF_HQ)), acc_spec, acc_spec],
        out_shape=[jax.ShapeDtypeStruct(dproj.shape, dproj.dtype), acc, acc],
        input_output_aliases={8: 0},
        scratch_shapes=[pltpu.VMEM((nh, HGRN_DIM, HGRN_DIM), F32), pltpu.VMEM((CHUNK, HGRN_WIDTH), F32),
                        pltpu.VMEM((CHUNK, HGRN_WIDTH), F32)],
        compiler_params=_params(("parallel", "arbitrary")),
    )(proj, proj, proj, proj, lb, gain, states, dyh, dproj)


def _adamw(w, g, m, v, name):
    R, C = w.shape
    tr = _pick(R, (128, 64, 32, 16, 8)) if C > 1024 else _pick(R, (512, 256, 128, 64, 32, 16, 8))

    def body(w_ref, g_ref, m_ref, v_ref, d_ref, nm_ref, nv_ref, g_out_ref):
        gv = g_ref[...]
        g_out_ref[...] = gv
        nm = ADAM_B1 * m_ref[...] + (1.0 - ADAM_B1) * gv
        nv = ADAM_B2 * v_ref[...] + (1.0 - ADAM_B2) * (gv * gv)
        m_hat = nm / (1.0 - ADAM_B1 ** ADAM_STEP)
        v_hat = nv / (1.0 - ADAM_B2 ** ADAM_STEP)
        d_ref[...] = -ADAM_LR * (m_hat / (jnp.sqrt(v_hat) + ADAM_EPS) + ADAM_WD * w_ref[...])
        nm_ref[...] = nm
        nv_ref[...] = nv

    blk = pl.BlockSpec((tr, C), lambda i: (i, 0))
    o = jax.ShapeDtypeStruct((R, C), F32)
    return pl.pallas_call(
        body, name=name, grid=(R // tr,), in_specs=[blk] * 4, out_specs=[blk] * 4, out_shape=[o, o, o, o],
        compiler_params=_params(("parallel",)),
    )(w, g, m, v)


ANY = pl.BlockSpec(memory_space=pl.ANY)
VMEM_SPEC = pl.BlockSpec(memory_space=pltpu.VMEM)


def _place():
    x, y, c = lax.axis_index("x"), lax.axis_index("y"), lax.axis_index("c")
    other_chips = [(1 - x, y), (x, 1 - y), (1 - x, 1 - y)]
    return x, y, c, other_chips


def _cast_into_full(w, ax, s_arr, name):
    R, C = w.shape
    tr = _pick(R, (256, 128))
    nr = R // tr

    def body(s_ref, w_ref, o_ref):
        o_ref[...] = w_ref[...].astype(BF16)

    if ax == 1:
        shape, o_map = (R, N_CHIPS * C), lambda i, s: (i, s[0])
    else:
        shape, o_map = (N_CHIPS * R, C), lambda i, s: (s[0] * nr + i, 0)
    return pl.pallas_call(
        body, name=name,
        grid_spec=pltpu.PrefetchScalarGridSpec(
            num_scalar_prefetch=1, grid=(nr,), in_specs=[pl.BlockSpec((tr, C), lambda i, s: (i, 0))],
            out_specs=pl.BlockSpec((tr, C), o_map)),
        out_shape=jax.ShapeDtypeStruct(shape, BF16),
        compiler_params=_params(("parallel",)),
    )(s_arr, w)


class _Gather:
    def __init__(self, fulls, shard_shapes, axes, tag):
        self.shapes, self.axes, self.tag, self.nw = shard_shapes, axes, tag, len(fulls)
        self.fulls, self.sems, self.token = list(fulls), {}, None

    def start(self, peers, after=None):
        nw, np_ = self.nw, len(peers)

        def body(*refs):
            ins, sems = refs[:nw], refs[nw + (after is not None):nw + (after is not None) + 2 * np_]
            for k, j in enumerate(peers):
                for cp in self._peer_copies(ins, sems[2 * k], sems[2 * k + 1], j):
                    cp.start()
            refs[-1][...] = jnp.zeros_like(refs[-1])

        out = pl.pallas_call(
            body, name="gather_start_%s_%s" % (self.tag, "".join(map(str, peers))),
            out_shape=(*[pltpu.SemaphoreType.DMA((nw,))] * (2 * np_),
                       *[pltpu.HBM(f.shape, f.dtype) for f in self.fulls], jax.ShapeDtypeStruct((8, LANES), F32)),
            in_specs=[HBM_SPEC] * nw + ([] if after is None else [ANY]),
            out_specs=(*[SEM_SPEC] * (2 * np_), *[HBM_SPEC] * nw, VMEM_SPEC),
            input_output_aliases={k: 2 * np_ + k for k in range(nw)},
            compiler_params=pltpu.CompilerParams(has_side_effects=DATAFLOW),
        )(*[pltpu.with_memory_space_constraint(f, pltpu.HBM) for f in self.fulls], *(() if after is None else (after,)))
        for k, j in enumerate(peers):
            self.sems[j] = (out[2 * k], out[2 * k + 1])
        self.fulls = list(out[2 * np_:2 * np_ + nw])
        self.token = out[-1]

    def _region(self, ref, i, t, half):
        R, C = self.shapes[i]
        hr = R // 2
        if self.axes[i] == 1:
            return ref.at[pl.ds(half * hr, hr), pl.ds(pl.multiple_of(t * C, LANES), C)]
        return ref.at[pl.ds(t * R + half * hr, hr), :]

    def _peer_copies(self, refs, send_sems, recv_sems, j):
        x, y, c, chips = _place()
        s = 2 * x + y
        return [pltpu.make_async_remote_copy(
            src_ref=self._region(refs[i], i, s, c), dst_ref=self._region(refs[i], i, s, c), send_sem=send_sems.at[i],
            recv_sem=recv_sems.at[i], device_id=(*chips[j], c), device_id_type=MESH) for i in range(self.nw)]

    def wait(self, peers, after):
        nw, np_ = self.nw, len(peers)

        def body(*refs):
            ins, sems = refs[:nw], refs[nw:nw + 2 * np_]
            for k, j in enumerate(peers):
                for cp in self._peer_copies(ins, sems[2 * k], sems[2 * k + 1], j):
                    cp.wait_send()
                    cp.wait_recv()

        sem_args = [s for j in peers for s in self.sems[j]]
        out = pl.pallas_call(
            body, name="gather_wait_%s_%s" % (self.tag, "".join(map(str, peers))),
            out_shape=tuple(pltpu.HBM(f.shape, f.dtype) for f in self.fulls),
            in_specs=[HBM_SPEC] * nw + [SEM_SPEC] * (2 * np_) + [ANY], out_specs=tuple([HBM_SPEC] * nw),
            input_output_aliases={k: k for k in range(nw)},
            compiler_params=pltpu.CompilerParams(has_side_effects=DATAFLOW),
        )(*self.fulls, *sem_args, after)
        self.fulls = list(out)

    def forward(self, peers):
        nw, np_ = self.nw, len(peers)

        def body(*refs):
            ins, outs = refs[:nw], refs[nw:2 * nw]
            send_sems, recv_sems = refs[2 * nw:]
            x, y, c, chips = _place()
            cps = []
            for i in range(nw):
                for k, j in enumerate(peers):
                    t = 2 * chips[j][0] + chips[j][1]
                    cp = pltpu.make_async_remote_copy(
                        src_ref=self._region(ins[i], i, t, c), dst_ref=self._region(outs[i], i, t, c),
                        send_sem=send_sems.at[i * np_ + k], recv_sem=recv_sems.at[i * np_ + k],
                        device_id=(x, y, 1 - c), device_id_type=MESH)
                    cp.start()
                    cps.append(cp)
            for cp in cps:
                cp.wait()

        out = pl.pallas_call(
            body, name="gather_forward_%s_%s" % (self.tag, "".join(map(str, peers))),
            in_specs=[ANY] * nw, out_specs=[ANY] * nw,
            out_shape=[jax.ShapeDtypeStruct(f.shape, f.dtype) for f in self.fulls],
            input_output_aliases={i: i for i in range(nw)},
            scratch_shapes=[pltpu.SemaphoreType.DMA((nw * np_,)), pltpu.SemaphoreType.DMA((nw * np_,))],
        )(*self.fulls)
        self.fulls = list(out)


def _matmul_slab(a, wfull, slab_arr, prev, after, name):
    M, K = a.shape
    N = wfull.shape[1]
    nslab = N // N_CHIPS
    tn = _pick(nslab, (2688, 896, 512, 384, 256, 128))
    tm = _pick(M, (512, 256, 128) if tn > 1024 else (1024, 512, 256, 128))
    per = nslab // tn
    extra = [e for e in (prev, after) if e is not None]

    def body(slab_ref, a_ref, b_ref, *rest):
        rest[len(extra)][...] = jnp.dot(a_ref[...], b_ref[...], preferred_element_type=F32)

    return pl.pallas_call(
        body, name=name,
        grid_spec=pltpu.PrefetchScalarGridSpec(
            num_scalar_prefetch=1, grid=(slab_arr.shape[0], M // tm, per),
            in_specs=[pl.BlockSpec((tm, K), lambda s, i, j, sl: (i, 0)),
                      pl.BlockSpec((K, tn), lambda s, i, j, sl: (0, sl[s] * per + j))] + [ANY] * len(extra),
            out_specs=pl.BlockSpec((tm, tn), lambda s, i, j, sl: (i, sl[s] * per + j))),
        out_shape=jax.ShapeDtypeStruct((M, N), F32),
        input_output_aliases={} if prev is None else {3: 0},
        compiler_params=_params(("arbitrary", "parallel", "arbitrary")),
    )(slab_arr, a, wfull, *extra)


def _exchange_sibling_halves(gs, name):
    nw = len(gs)

    def body(*refs):
        ins, outs = refs[:nw], refs[nw:2 * nw]
        send_sems, recv_sems = refs[2 * nw:]
        x, y, c, _ = _place()
        cps = []
        for i in range(nw):
            cp = pltpu.make_async_remote_copy(src_ref=ins[i].at[:, 1 - c], dst_ref=outs[i], send_sem=send_sems.at[i],
                                              recv_sem=recv_sems.at[i], device_id=(x, y, 1 - c), device_id_type=MESH)
            cp.start()
            cps.append(cp)
        for cp in cps:
            cp.wait()

    return pl.pallas_call(
        body, name=name, in_specs=[ANY] * nw, out_specs=[ANY] * nw,
        out_shape=[jax.ShapeDtypeStruct((g.shape[0],) + g.shape[2:], g.dtype) for g in gs],
        scratch_shapes=[pltpu.SemaphoreType.DMA((nw,)), pltpu.SemaphoreType.DMA((nw,))],
    )(*gs)


HBM_SPEC = pl.BlockSpec(memory_space=pltpu.HBM)
SEM_SPEC = pl.BlockSpec(memory_space=pltpu.SEMAPHORE)
DATAFLOW = pltpu.SideEffectType.DATAFLOW_SIDE_EFFECTING


def _chip_copies(ins, lands, send_sems, recv_sems):
    x, y, c, chips = _place()
    return [pltpu.make_async_remote_copy(
        src_ref=ins[i].at[2 * chip[0] + chip[1]], dst_ref=lands[i].at[j], send_sem=send_sems.at[i * 3 + j],
        recv_sem=recv_sems.at[i * 3 + j], device_id=(*chip, c), device_id_type=MESH)
        for i in range(len(ins)) for j, chip in enumerate(chips)]


def _chips_send_start(ss, name):
    nw = len(ss)
    lands = [pltpu.with_memory_space_constraint(lax.empty((N_CHIPS - 1,) + s.shape[1:], s.dtype), pltpu.HBM) for s in ss]

    def body(*refs):
        ins, land_refs = refs[:nw], refs[nw:2 * nw]
        send_sems, recv_sems = refs[2 * nw], refs[2 * nw + 1]
        token = refs[-1]
        for cp in _chip_copies(ins, land_refs, send_sems, recv_sems):
            cp.start()
        token[...] = jnp.zeros_like(token)

    n = 3 * nw
    out = pl.pallas_call(
        body, name=name,
        out_shape=(pltpu.SemaphoreType.DMA((n,)), pltpu.SemaphoreType.DMA((n,)),
                   *[pltpu.HBM(s.shape, s.dtype) for s in ss], *[pltpu.HBM(l.shape, l.dtype) for l in lands],
                   jax.ShapeDtypeStruct((8, LANES), F32)),
        in_specs=[HBM_SPEC] * (2 * nw), out_specs=(SEM_SPEC, SEM_SPEC, *[HBM_SPEC] * (2 * nw), VMEM_SPEC),
        input_output_aliases={k: 2 + k for k in range(2 * nw)},
        compiler_params=pltpu.CompilerParams(has_side_effects=DATAFLOW),
    )(*[pltpu.with_memory_space_constraint(s, pltpu.HBM) for s in ss], *lands)
    return out[0], out[1], list(out[2:2 + nw]), list(out[2 + nw:2 + 2 * nw]), out[-1]


def _chips_send_wait(send_sems, recv_sems, ss, lands, after, name):
    nw = len(ss)

    def body(*refs):
        ins, land_refs = refs[:nw], refs[nw:2 * nw]
        s_sems, r_sems = refs[2 * nw], refs[2 * nw + 1]
        for cp in _chip_copies(ins, land_refs, s_sems, r_sems):
            cp.wait_send()
            cp.wait_recv()

    out = pl.pallas_call(
        body, name=name,
        out_shape=(*[pltpu.HBM(s.shape, s.dtype) for s in ss], *[pltpu.HBM(l.shape, l.dtype) for l in lands]),
        in_specs=[HBM_SPEC] * (2 * nw) + [SEM_SPEC, SEM_SPEC, ANY], out_specs=tuple([HBM_SPEC] * (2 * nw)),
        input_output_aliases={k: k for k in range(2 * nw)},
        compiler_params=pltpu.CompilerParams(has_side_effects=DATAFLOW),
    )(*ss, *lands, send_sems, recv_sems, after)
    return list(out[nw:])


def _share_and_sum_small(fs, pack, name):
    nw = len(fs)
    rows = pack.shape[0]

    def body(*refs):
        ins, pack_ref = refs[:nw], refs[nw]
        outs, sum_ref = refs[nw + 1:2 * nw + 1], refs[2 * nw + 1]
        all_ref, send_sems, recv_sems = refs[2 * nw + 2:]
        x, y, c, _ = _place()
        me = 4 * x + 2 * y + c
        cps = []
        for i in range(nw):
            cp = pltpu.make_async_remote_copy(src_ref=ins[i].at[c], dst_ref=outs[i].at[c], send_sem=send_sems.at[i],
                                              recv_sem=recv_sems.at[i], device_id=(x, y, 1 - c), device_id_type=MESH)
            cp.start()
            cps.append(cp)
        all_ref[me] = pack_ref[...]
        for k in range(1, N_DEV):
            to = (1 - x if k & 4 else x, 1 - y if k & 2 else y, 1 - c if k & 1 else c)
            cp = pltpu.make_async_remote_copy(
                src_ref=pack_ref, dst_ref=all_ref.at[me], send_sem=send_sems.at[nw + k - 1],
                recv_sem=recv_sems.at[nw + k - 1], device_id=to, device_id_type=MESH)
            cp.start()
            cps.append(cp)
        for cp in cps:
            cp.wait()
        total = all_ref[0]
        for d in range(1, N_DEV):
            total = total + all_ref[d]
        sum_ref[...] = total

    n_sems = nw + N_DEV - 1
    out = pl.pallas_call(
        body, name=name, in_specs=[ANY] * nw + [VMEM_SPEC], out_specs=[ANY] * nw + [VMEM_SPEC],
        out_shape=[jax.ShapeDtypeStruct(f.shape, f.dtype) for f in fs] + [jax.ShapeDtypeStruct(pack.shape, F32)],
        input_output_aliases={i: i for i in range(nw)},
        scratch_shapes=[pltpu.VMEM((N_DEV, rows, LANES), F32), pltpu.SemaphoreType.DMA((n_sems,)),
                        pltpu.SemaphoreType.DMA((n_sems,))],
    )(*fs, pack)
    return list(out[:nw]), out[nw]


def _sum_sibling(g, land, where_arr, name):
    P, Q = g.shape[-2:]
    tp = _pick(P, (256, 128, 64))

    def body(w_ref, g_ref, l_ref, s_ref):
        s_ref[...] = (g_ref[...].astype(F32) + l_ref[...].astype(F32)).astype(BF16)

    if g.ndim == 4:
        g_spec = pl.BlockSpec((None, None, tp, Q), lambda j, r, w: (w[1 + j], w[0], r, 0))
    else:
        g_spec = pl.BlockSpec((None, tp, Q), lambda j, r, w: (w[1 + j], r, 0))
    slab = pl.BlockSpec((None, tp, Q), lambda j, r, w: (w[1 + j], r, 0))
    return pl.pallas_call(
        body, name=name,
        grid_spec=pltpu.PrefetchScalarGridSpec(
            num_scalar_prefetch=1, grid=(N_CHIPS - 1, P // tp), in_specs=[g_spec, slab], out_specs=slab),
        out_shape=jax.ShapeDtypeStruct((N_CHIPS, P, Q), BF16),
        compiler_params=_params(("parallel", "parallel")),
    )(where_arr, g, land)


def _sum_chips(g, land, recv, sc_arr, name):
    P, Q = g.shape[-2:]
    tp = _pick(P, (256, 128, 64))

    def body(sc_ref, g_ref, l_ref, r_ref, f_ref):
        acc = g_ref[...].astype(F32) + l_ref[...].astype(F32)
        for j in range(N_CHIPS - 1):
            acc = acc + r_ref[j].astype(F32)
        f_ref[...] = acc

    if g.ndim == 4:
        g_spec = pl.BlockSpec((None, None, tp, Q), lambda r, sc: (sc[0], sc[1], r, 0))
    else:
        g_spec = pl.BlockSpec((None, tp, Q), lambda r, sc: (sc[0], r, 0))
    return pl.pallas_call(
        body, name=name,
        grid_spec=pltpu.PrefetchScalarGridSpec(
            num_scalar_prefetch=1, grid=(P // tp,),
            in_specs=[g_spec,
                      pl.BlockSpec((None, tp, Q), lambda r, sc: (sc[0], r, 0)),
                      pl.BlockSpec((N_CHIPS - 1, tp, Q), lambda r, sc: (0, r, 0))],
            out_specs=pl.BlockSpec((None, tp, Q), lambda r, sc: (sc[1], r, 0))),
        out_shape=jax.ShapeDtypeStruct((2, P, Q), F32),
        compiler_params=_params(("parallel",)),
    )(sc_arr, g, land, recv)


class _Reduction:
    def __init__(self, names, pieces, lands, flight):
        self.names, self.pieces, self.lands = names, pieces, lands
        self.send_sems, self.recv_sems, self.sums, self.zones, self.token = flight


def _reduce_start(pieces, lands, names, where_arr):
    tag = names[0] if len(names) == 1 else "branches"
    sums = [_sum_sibling(g, l, where_arr, "sum_sibling_" + nm) for g, l, nm in zip(pieces, lands, names)]
    return _Reduction(names, pieces, lands, _chips_send_start(sums, "grads_to_chips_start_" + tag))


def _wgrad_half(a, b, half_arr, name, after=None):
    K, M = a.shape
    nslab = b.shape[1] // N_CHIPS
    tm = M // 2
    tn = _pick(nslab, (896, 512, 384, 256, 128))
    per = nslab // tn
    assert 4 * K * (tm + tn) <= MATMUL_OPERAND_BYTES

    def body(h_ref, a_ref, b_ref, *rest):
        rest[-1][...] = lax.dot_general(a_ref[...], b_ref[...], TN_DIMS, preferred_element_type=F32).astype(BF16)

    return pl.pallas_call(
        body, name=name,
        grid_spec=pltpu.PrefetchScalarGridSpec(
            num_scalar_prefetch=1, grid=(b.shape[1] // tn,),
            in_specs=[pl.BlockSpec((K, tm), lambda j, h: (0, h[0])), pl.BlockSpec((K, tn), lambda j, h: (0, j))]
            + ([] if after is None else [ANY]),
            out_specs=pl.BlockSpec((None, tm, tn), lambda j, h: (j // per, 0, j % per))),
        out_shape=jax.ShapeDtypeStruct((N_CHIPS, tm, nslab), BF16),
        compiler_params=_params(("parallel",)),
    )(*((half_arr, a, b) if after is None else (half_arr, a, b, after)))


def _sibling_send_start(g, name):
    zone = pltpu.with_memory_space_constraint(lax.empty(g.shape, g.dtype), pltpu.HBM)

    def body(g_ref, zone_ref, send_sem, recv_sem, g_thru, zone_thru, token):
        x, y, c, _ = _place()
        pltpu.make_async_remote_copy(src_ref=g_ref, dst_ref=zone_ref, send_sem=send_sem, recv_sem=recv_sem,
                                     device_id=(x, y, 1 - c), device_id_type=MESH).start()
        token[...] = jnp.zeros_like(token)

    return pl.pallas_call(
        body, name=name,
        out_shape=(pltpu.SemaphoreType.DMA(()), pltpu.SemaphoreType.DMA(()), pltpu.HBM(g.shape, g.dtype),
                   pltpu.HBM(g.shape, g.dtype), jax.ShapeDtypeStruct((8, LANES), F32)),
        in_specs=[HBM_SPEC, HBM_SPEC], out_specs=(SEM_SPEC, SEM_SPEC, HBM_SPEC, HBM_SPEC, VMEM_SPEC),
        input_output_aliases={0: 2, 1: 3},
        compiler_params=pltpu.CompilerParams(has_side_effects=DATAFLOW),
    )(pltpu.with_memory_space_constraint(g, pltpu.HBM), zone)


def _sibling_send_wait(send_sem, recv_sem, g, zone, after, name):
    def body(g_ref, zone_ref, s_sem, r_sem, after_ref, g_out, zone_out):
        x, y, c, _ = _place()
        cp = pltpu.make_async_remote_copy(src_ref=g_ref, dst_ref=zone_ref, send_sem=s_sem, recv_sem=r_sem,
                                          device_id=(x, y, 1 - c), device_id_type=MESH)
        cp.wait_send()
        cp.wait_recv()

    return pl.pallas_call(
        body, name=name, out_shape=(pltpu.HBM(g.shape, g.dtype), pltpu.HBM(zone.shape, zone.dtype)),
        in_specs=[HBM_SPEC, HBM_SPEC, SEM_SPEC, SEM_SPEC, ANY], out_specs=(HBM_SPEC, HBM_SPEC),
        input_output_aliases={0: 0, 1: 1},
        compiler_params=pltpu.CompilerParams(has_side_effects=DATAFLOW),
    )(g, zone, send_sem, recv_sem, after)[1]


def _reduce_finish(red, after, sc_arr):
    tag = red.names[0] if len(red.names) == 1 else "branches"
    recvs = _chips_send_wait(red.send_sems, red.recv_sems, red.sums, red.zones, after, "grads_to_chips_wait_" + tag)
    return [_sum_chips(g, l, r, sc_arr, "sum_chips_" + nm) for g, l, r, nm in zip(red.pieces, red.lands, recvs, red.names)]


def _t5_bucket(dist):
    max_exact = REL_BUCKETS // 2
    d = jnp.maximum(dist, 0)
    df = jnp.maximum(d, 1).astype(F32)
    large = max_exact + (jnp.log(df / max_exact) / math.log(REL_MAX_DIST / max_exact)
                         * (REL_BUCKETS - max_exact)).astype(jnp.int32)
    large = jnp.minimum(large, REL_BUCKETS - 1)
    return jnp.where(d < max_exact, d, large)


def _bucket_table():
    qi = jnp.arange(WINDOW)[:, None]
    si = jnp.arange(2 * WINDOW)[None, :]
    return _t5_bucket(qi + WINDOW - si)


TILE_WORDS = 8 * LANES


def _tile_rows(shape):
    return -(-math.prod(shape) // TILE_WORDS) * 8


def _rows_of(a):
    flat = a.reshape(-1).astype(F32)
    n = _tile_rows(a.shape) * LANES
    return jnp.pad(flat, (0, n - flat.shape[0])).reshape(-1, LANES)


def _pack_rows(parts):
    return jnp.concatenate([_rows_of(p) for p in parts], axis=0)


def _unpack_rows(packed, shapes):
    out, at = [], 0
    for shp in shapes:
        n, nr = math.prod(shp), _tile_rows(shp)
        out.append(packed[at:at + nr].reshape(-1)[:n].reshape(shp))
        at += nr
    return out


def kernel(x, norm_pre, w_in, rel_bias, attn_sinks, lb_logits, hgrn_norm, w_branch_attn, w_branch_hgrn, w_out, norm_post, loss_target, m_norm_pre, m_w_in, m_rel_bias, m_attn_sinks, m_lb_logits, m_hgrn_norm, m_w_branch_attn, m_w_branch_hgrn, m_w_out, m_norm_post, v_norm_pre, v_w_in, v_rel_bias, v_attn_sinks, v_lb_logits, v_hgrn_norm, v_w_branch_attn, v_w_branch_hgrn, v_w_out, v_norm_post):
    B_loc, S, D = x.shape
    T = B_loc * S
    x2 = x.reshape(T, D)
    tgt2 = loss_target.reshape(T, D)
    my_x, my_y, my_c = lax.axis_index("x"), lax.axis_index("y"), lax.axis_index("c")

    c_arr = jnp.reshape(my_c, (1,)).astype(jnp.int32)
    s_arr = jnp.reshape(2 * my_x + my_y, (1,)).astype(jnp.int32)
    sc_arr = jnp.concatenate([s_arr, c_arr])
    shard_ws = [w_in[0], w_branch_attn[0], w_branch_hgrn[0], w_out[0]]
    shard_axes = (1, 1, 1, 0)
    names = ["w_in", "w_branch_attn", "w_branch_hgrn", "w_out"]
    placed = [_cast_into_full(w, ax, s_arr, "cast_" + nm) for w, ax, nm in zip(shard_ws, shard_axes, names)]
    peer_slabs = [jnp.reshape(t, (1,)).astype(jnp.int32)
                  for t in (2 * (1 - my_x) + my_y, 2 * my_x + 1 - my_y, 2 * (1 - my_x) + 1 - my_y)]

    buckets = _bucket_table()
    onehot = (buckets.reshape(-1)[:, None] == jnp.arange(REL_BUCKETS)[None, :]).astype(F32)
    bias_tab = jnp.dot(onehot, rel_bias.astype(F32), precision=lax.Precision.HIGHEST).T.reshape(ATTN_HEADS, WINDOW, 2 * WINDOW)
    sinks_b = jnp.broadcast_to(attn_sinks[0].astype(F32)[:, None, None], (ATTN_HEADS, 8, LANES))
    lb_fn = lambda l: jnp.cumsum(jax.nn.softmax(l.astype(F32), axis=0), axis=0)[:1]
    lb, lb_vjp = jax.vjp(lb_fn, lb_logits)
    gain_h = hgrn_norm[0].reshape(1, HGRN_WIDTH)

    h, rstd = _rmsnorm_fwd(x2, norm_pre)
    gather_in = _Gather(placed[:1], [shard_ws[0].shape], shard_axes[:1], "w_in")
    gather_in.start([0, 1])
    proj = _matmul_slab(h, gather_in.fulls[0], s_arr, None, gather_in.token, "in_proj_own")
    gather_in.wait([0, 1], proj)
    gather_in.forward([0, 1])
    gather_in.start([2])
    proj = _matmul_slab(h, gather_in.fulls[0], jnp.concatenate(peer_slabs[:2]), proj, gather_in.token, "in_proj_peer01")
    gather_in.wait([2], proj)
    gather_in.forward([2])
    gather_rest = _Gather(placed[1:], [w.shape for w in shard_ws[1:]], shard_axes[1:], "rest")
    gather_rest.start([0, 1, 2], gather_in.fulls[0])
    proj = _matmul_slab(h, gather_in.fulls[0], peer_slabs[2], proj, gather_rest.token, "in_proj_peer2")
    win_f = gather_in.fulls[0]
    ya, attn_probs = _attn_fwd(proj, bias_tab, sinks_b, B_loc, S)
    yh, states = _hgrn_fwd(proj, lb, gain_h, B_loc, S)
    gather_rest.wait([0, 1, 2], yh)
    gather_rest.forward([0, 1, 2])
    wba_f, wbh_f, wout_f = gather_rest.fulls
    ua = _matmul(ya, wba_f, "nn", BF16, "branch_attn_proj")
    uh, merged = _branch_proj_merge(yh, wbh_f, proj, ua)
    yv = _matmul(merged, wout_f, "nn", BF16, "out_proj")
    dy, dout, loss_p, gnpost_p = _post_loss(yv, x2, tgt2, norm_post)

    g_wout = _matmul(merged, dy, "tn", BF16, "out_proj_wgrad")
    d_ua, d_uh, dproj = _out_dgrad_merge_bwd(dy, wout_f, proj, ua, uh)
    g_wba = _matmul(ya, d_ua, "tn", BF16, "branch_attn_wgrad", slabs=N_CHIPS)
    g_wbh = _matmul(yh, d_uh, "tn", BF16, "branch_hgrn_wgrad", slabs=N_CHIPS)
    where_arr = jnp.concatenate([c_arr] + peer_slabs)
    late_pieces = [g.reshape(N_CHIPS, 2, -1, g.shape[-1]) for g in (g_wba, g_wbh, g_wout)]
    late = _reduce_start(late_pieces, _exchange_sibling_halves(late_pieces, "grads_to_sibling_branches"), names[1:], where_arr)
    d_ya = _matmul(d_ua, wba_f, "nt", BF16, "branch_attn_dgrad", after=late.token)
    d_yh = _matmul(d_uh, wbh_f, "nt", BF16, "branch_hgrn_dgrad", after=late.token)
    dproj, dbias_p, dsinks_p = _attn_bwd(proj, attn_probs, d_ya, dproj, B_loc, S)
    dproj, dlb_p, dgain_p = _hgrn_bwd(proj, states, lb, gain_h, d_yh, dproj, B_loc, S)
    g_give = _wgrad_half(h, dproj, 1 - c_arr, "in_proj_wgrad_sibling_half")
    s_sem, r_sem, g_give, zone, token = _sibling_send_start(g_give, "grads_to_sibling_start_w_in")
    g_keep = _wgrad_half(h, dproj, c_arr, "in_proj_wgrad_own_half", after=token)
    land = _sibling_send_wait(s_sem, r_sem, g_give, zone, g_keep, "grads_to_sibling_wait_w_in")
    last = _reduce_start([g_keep], [land], names[:1], where_arr)
    dh = _matmul(dproj, win_f, "nt", BF16, "in_proj_dgrad", after=last.token)
    grad_x2, gnpre_p = _rmsnorm_bwd(dh, x2, rstd, norm_pre, dout)
    halves = _reduce_finish(last, grad_x2, sc_arr) + _reduce_finish(late, grad_x2, sc_arr)

    grelb_p = jnp.dot(dbias_p.reshape(ATTN_HEADS, -1), onehot, precision=lax.Precision.HIGHEST).T
    gsinks_p = dsinks_p[:, 0, 0]
    dlb_sum = jnp.sum(dlb_p, axis=0).reshape(1, HGRN_WIDTH)
    ghn_p = jnp.sum(dgain_p, axis=0).reshape(HGRN_HEADS, HGRN_DIM)
    small_parts = [gnpre_p, gnpost_p, grelb_p, gsinks_p, dlb_sum, ghn_p, loss_p]
    small_shapes = [p.shape for p in small_parts]
    shared, pack_sum = _share_and_sum_small(halves, _pack_rows(small_parts), "grads_share_sibling")
    big_w = [w_in, w_branch_attn, w_branch_hgrn, w_out]
    big_m = [m_w_in, m_w_branch_attn, m_w_branch_hgrn, m_w_out]
    big_v = [v_w_in, v_w_branch_attn, v_w_branch_hgrn, v_w_out]
    big = {}
    for nm, gs, w, m, v in zip(names, shared, big_w, big_m, big_v):
        shp = w.shape
        g2 = gs.reshape(shp[1], shp[2])
        d, nm_, nv_, g_out = _adamw(w[0], g2, m[0], v[0], "adamw_" + nm)
        big[nm] = tuple(a.reshape(shp) for a in (g_out, d, nm_, nv_))

    gnpre, gnpost, grelb, gsinks, dlb, ghn, loss = _unpack_rows(pack_sum, small_shapes)
    (g_lb_logits,) = lb_vjp(dlb)
    small_names = ["norm_pre", "rel_bias", "attn_sinks", "lb_logits", "hgrn_norm", "norm_post"]
    small_w = [norm_pre, rel_bias, attn_sinks, lb_logits, hgrn_norm, norm_post]
    small_m = [m_norm_pre, m_rel_bias, m_attn_sinks, m_lb_logits, m_hgrn_norm, m_norm_post]
    small_v = [v_norm_pre, v_rel_bias, v_attn_sinks, v_lb_logits, v_hgrn_norm, v_norm_post]
    small_g = [gnpre.reshape(norm_pre.shape), grelb.reshape(rel_bias.shape), gsinks.reshape(attn_sinks.shape),
               g_lb_logits.reshape(lb_logits.shape), ghn.reshape(hgrn_norm.shape), gnpost.reshape(norm_post.shape)]
    shapes = [w.shape for w in small_w]
    d_s, nm_s, nv_s, _ = _adamw(_pack_rows(small_w), _pack_rows(small_g), _pack_rows(small_m), _pack_rows(small_v),
                             "adamw_small")
    small = {}
    for nm, g, d, m_, v_ in zip(small_names, small_g, _unpack_rows(d_s, shapes), _unpack_rows(nm_s, shapes),
                                _unpack_rows(nv_s, shapes)):
        small[nm] = (g, d, m_, v_)

    allw = {**big, **small}
    order = ["norm_pre", "w_in", "rel_bias", "attn_sinks", "lb_logits", "hgrn_norm", "w_branch_attn", "w_branch_hgrn",
             "w_out", "norm_post"]
    outs = [loss.reshape(()), grad_x2.reshape(B_loc, S, D)]
    for k in range(4):
        outs += [allw[nm][k] for nm in order]
    return tuple(outs)
```
